```python
import jax, jax.numpy as jnp
from jax import lax
import numpy as np

D_MODEL = 1024
BATCH = 16
SEQ = 2048
DEPTH = 1

CHUNK = 64
PLE_DIM = 256
EPS = 1e-6

A_HEADS = 8
A_HEAD_DIM = 64
A_WIDTH = A_HEADS * A_HEAD_DIM
A_LOOKBACK = 8
A_BAND = A_LOOKBACK + 1
REL_CLIP = 128
N_REL = 2 * REL_CLIP + 1

B_HEADS = 4
B_KEY_DIM = 128
B_VAL_DIM = 128
B_QK_WIDTH = B_HEADS * B_KEY_DIM
B_V_WIDTH = B_HEADS * B_VAL_DIM
CONV_WIDTH = 4
B_CONV_CH = 2 * B_QK_WIDTH + B_V_WIDTH

D_FF = ((8 * D_MODEL // 3 + 255) // 256) * 256

N_BRANCH = 2
SPLIT_A = 3 * A_WIDTH
SPLIT_CONV = SPLIT_A + B_CONV_CH
SPLIT_Z = SPLIT_CONV + B_V_WIDTH
SPLIT_BETA = SPLIT_Z + B_HEADS
SPLIT_DECAY = SPLIT_BETA + B_HEADS
D_IN = SPLIT_DECAY + N_BRANCH * D_MODEL

kernel_name = "hybrid_chunked_attn_gated_deltanet_block"


def rmsnorm(x, g):
    xf = x.astype(jnp.float32)
    y = xf * lax.rsqrt(jnp.mean(xf * xf, axis=-1, keepdims=True) + EPS)
    return (y * g.astype(jnp.float32)).astype(x.dtype)


def l2norm(x):
    return x * lax.rsqrt(jnp.sum(x * x, axis=-1, keepdims=True) + EPS)


def causal_depthwise_conv(x, w):
    s = x.shape[1]
    k = w.shape[0]
    xp = jnp.pad(x, ((0, 0), (k - 1, 0), (0, 0)))
    out = xp[:, 0:s] * w[0]
    for i in range(1, k):
        out = out + xp[:, i:i + s] * w[i]
    return out


def chunked_band_attention(q, k, v, rel_bias):
    b, s, h, d = q.shape
    nc = s // CHUNK
    qc = q.reshape(b, nc, CHUNK, h, d) * (d ** -0.5)
    pad = ((0, 0), (A_LOOKBACK, 0), (0, 0), (0, 0), (0, 0))
    kc = jnp.pad(k.reshape(b, nc, CHUNK, h, d), pad)
    vc = jnp.pad(v.reshape(b, nc, CHUNK, h, d), pad)
    scores = jnp.stack(
        [jnp.einsum('bnqhd,bnkhd->bhnqk', qc, kc[:, j:j + nc]) for j in range(A_BAND)],
        axis=4)
    scores = scores.reshape(b, h, nc, CHUNK, A_BAND * CHUNK).astype(jnp.float32)
    qi = jnp.arange(CHUNK)
    kj = jnp.arange(A_BAND * CHUNK)
    rel = (A_LOOKBACK * CHUNK + qi[:, None]) - kj[None, :]
    idx = jnp.clip(rel, -REL_CLIP, REL_CLIP) + REL_CLIP
    bias = rel_bias[:, idx].astype(jnp.float32)
    valid = (jnp.arange(nc)[:, None] + jnp.arange(A_BAND)[None, :]) >= A_LOOKBACK
    valid = jnp.repeat(valid, CHUNK, axis=1)
    scores = jnp.where(valid[None, None, :, None, :], scores + bias[None, :, None], -1e30)
    probs = jax.nn.softmax(scores, axis=-1).astype(v.dtype)
    probs = probs.reshape(b, h, nc, CHUNK, A_BAND, CHUNK)
    out = jnp.einsum('bhnqk,bnkhd->bnqhd', probs[:, :, :, :, 0], vc[:, 0:nc])
    for j in range(1, A_BAND):
        out = out + jnp.einsum('bhnqk,bnkhd->bnqhd', probs[:, :, :, :, j], vc[:, j:j + nc])
    return out.reshape(b, s, h * d)


def gated_delta_rule(q, k, v, g, beta):
    out_dtype = v.dtype
    f32 = jnp.float32
    b, s, h, dk = q.shape
    dv = v.shape[-1]
    nc = s // CHUNK
    L = CHUNK

    def to_chunks(t):
        return jnp.moveaxis(t.reshape((b, nc, L, h) + t.shape[3:]), 3, 1)

    q = to_chunks(l2norm(q.astype(f32)) * (dk ** -0.5))
    k = to_chunks(l2norm(k.astype(f32)))
    v = to_chunks(v.astype(f32))
    g = to_chunks(g.astype(f32))
    beta = to_chunks(beta.astype(f32))

    gc = jnp.cumsum(g, axis=-1)
    ti = jnp.arange(L)
    tri_incl = ti[:, None] >= ti[None, :]
    tri_strict = ti[:, None] > ti[None, :]
    diff = gc[..., :, None] - gc[..., None, :]
    decay = jnp.exp(jnp.where(tri_incl, diff, -jnp.inf))
    kk = jnp.einsum('bhnid,bhnjd->bhnij', k, k)
    a_mat = jnp.where(tri_strict, beta[..., :, None] * kk * decay, 0.0)
    eye = jnp.eye(L, dtype=f32)
    rhs = jnp.concatenate([beta[..., None] * v,
                           (beta * jnp.exp(gc))[..., None] * k], axis=-1)
    sol = lax.linalg.triangular_solve(eye + a_mat, rhs, left_side=True,
                                      lower=True, unit_diagonal=True)
    u, wk = sol[..., :dv], sol[..., dv:]
    pqk = jnp.einsum('bhnid,bhnjd->bhnij', q, k) * decay
    gamma = jnp.exp(gc)
    g_last = gc[..., -1]
    kdec = k * jnp.exp(g_last[..., None] - gc)[..., None]

    def front(t):
        return jnp.moveaxis(t, 2, 0)

    xs = (front(u), front(wk), front(q), front(pqk), front(gamma), front(kdec), front(g_last))

    def step(state, inp):
        u_c, wk_c, q_c, p_c, gam_c, kd_c, gl_c = inp
        w = u_c - jnp.einsum('bhld,bhvd->bhlv', wk_c, state)
        o = gam_c[..., None] * jnp.einsum('bhld,bhvd->bhlv', q_c, state) \
            + jnp.einsum('bhts,bhsv->bhtv', p_c, w)
        state = jnp.exp(gl_c)[..., None, None] * state + jnp.einsum('bhlv,bhld->bhvd', w, kd_c)
        return state, o

    s0 = jnp.zeros((b, h, dv, dk), f32)
    _, o = lax.scan(step, s0, xs)
    o = jnp.transpose(o, (1, 0, 3, 2, 4)).reshape(b, s, h, dv)
    return o.astype(out_dtype)


def _fwd_setup_inputs(seed: int = 0) -> dict:
    key = jax.random.key(seed)
    ks = jax.random.split(key, 24)
    f32 = jnp.float32

    def nrm(k, shape, scale):
        return jax.random.normal(k, shape, f32) * scale

    def gain(k, shape):
        return 1.0 + 0.1 * jax.random.normal(k, shape, f32)

    x = jax.random.normal(ks[0], (BATCH, SEQ, D_MODEL), f32)
    p = jax.random.normal(ks[1], (DEPTH, BATCH, SEQ, PLE_DIM), f32)
    g_mix = gain(ks[2], (DEPTH, D_MODEL))
    w_in = nrm(ks[3], (DEPTH, D_MODEL, D_IN), D_MODEL ** -0.5)
    conv_w = nrm(ks[4], (DEPTH, CONV_WIDTH, B_CONV_CH), CONV_WIDTH ** -0.5)
    a_log = jnp.log(jax.random.uniform(ks[5], (DEPTH, B_HEADS), f32, 1.0, 16.0))
    dt = jnp.exp(jax.random.uniform(ks[6], (DEPTH, B_HEADS), f32,
                                    np.log(1e-3), np.log(1e-1)))
    dt_bias = dt + jnp.log(-jnp.expm1(-dt))
    rel_bias = nrm(ks[7], (DEPTH, A_HEADS, N_REL), 0.1)
    w_onorm = gain(ks[8], (DEPTH, B_VAL_DIM))
    w_branch_a = nrm(ks[9], (DEPTH, A_WIDTH, D_MODEL), A_WIDTH ** -0.5)
    w_branch_b = nrm(ks[10], (DEPTH, B_V_WIDTH, D_MODEL), B_V_WIDTH ** -0.5)
    w_out = nrm(ks[11], (DEPTH, D_MODEL, D_MODEL), D_MODEL ** -0.5)
    g_ffn = gain(ks[12], (DEPTH, D_MODEL))
    w_gate_up = nrm(ks[13], (DEPTH, D_MODEL, 2 * D_FF), D_MODEL ** -0.5)
    w_down = nrm(ks[14], (DEPTH, D_FF, D_MODEL), D_FF ** -0.5)
    g_ple = gain(ks[15], (DEPTH, D_MODEL))
    w_ple_gate = nrm(ks[16], (DEPTH, D_MODEL, D_MODEL), D_MODEL ** -0.5)
    w_ple_proj = nrm(ks[17], (DEPTH, PLE_DIM, D_MODEL), PLE_DIM ** -0.5)
    g_final = gain(ks[18], (D_MODEL,))
    return {"x": x, "p": p, "g_mix": g_mix, "w_in": w_in, "conv_w": conv_w,
            "a_log": a_log, "dt_bias": dt_bias, "rel_bias": rel_bias, "w_onorm": w_onorm,
            "w_branch_a": w_branch_a, "w_branch_b": w_branch_b, "w_out": w_out,
            "g_ffn": g_ffn, "w_gate_up": w_gate_up, "w_down": w_down,
            "g_ple": g_ple, "w_ple_gate": w_ple_gate, "w_ple_proj": w_ple_proj,
            "g_final": g_final}


def _fwd_reference(x, p, g_mix, w_in, conv_w, a_log, dt_bias, rel_bias, w_onorm,
              w_branch_a, w_branch_b, w_out, g_ffn, w_gate_up, w_down,
              g_ple, w_ple_gate, w_ple_proj, g_final):
    b, s, _ = x.shape
    for i in range(DEPTH):
        h = rmsnorm(x, g_mix[i])
        proj = h @ w_in[i]
        qkv_a = proj[..., :SPLIT_A]
        conv_in = proj[..., SPLIT_A:SPLIT_CONV]
        z = proj[..., SPLIT_CONV:SPLIT_Z]
        beta_raw = proj[..., SPLIT_Z:SPLIT_BETA]
        decay_raw = proj[..., SPLIT_BETA:SPLIT_DECAY]
        gates = proj[..., SPLIT_DECAY:]

        q_a = qkv_a[..., :A_WIDTH].reshape(b, s, A_HEADS, A_HEAD_DIM)
        k_a = qkv_a[..., A_WIDTH:2 * A_WIDTH].reshape(b, s, A_HEADS, A_HEAD_DIM)
        v_a = qkv_a[..., 2 * A_WIDTH:].reshape(b, s, A_HEADS, A_HEAD_DIM)
        y_a = chunked_band_attention(q_a, k_a, v_a, rel_bias[i])

        c = jax.nn.silu(causal_depthwise_conv(conv_in, conv_w[i]))
        q_b = c[..., :B_QK_WIDTH].reshape(b, s, B_HEADS, B_KEY_DIM)
        k_b = c[..., B_QK_WIDTH:2 * B_QK_WIDTH].reshape(b, s, B_HEADS, B_KEY_DIM)
        v_b = c[..., 2 * B_QK_WIDTH:].reshape(b, s, B_HEADS, B_VAL_DIM)
        beta = jax.nn.sigmoid(beta_raw)
        g = -jnp.exp(a_log[i]) * jax.nn.softplus(decay_raw + dt_bias[i])
        o_b = gated_delta_rule(q_b, k_b, v_b, g, beta)
        o_b = rmsnorm(o_b, w_onorm[i]) * jax.nn.silu(z.reshape(b, s, B_HEADS, B_VAL_DIM))
        y_b = o_b.reshape(b, s, B_V_WIDTH)

        gate_a = jax.nn.sigmoid(gates[..., :D_MODEL])
        gate_b = jax.nn.sigmoid(gates[..., D_MODEL:])
        merged = gate_a * (y_a @ w_branch_a[i]) + gate_b * (y_b @ w_branch_b[i])
        x = x + merged @ w_out[i]

        h = rmsnorm(x, g_ffn[i])
        gu = h @ w_gate_up[i]
        x = x + (jax.nn.silu(gu[..., :D_FF]) * gu[..., D_FF:]) @ w_down[i]

        ple_gate = jax.nn.sigmoid(rmsnorm(x, g_ple[i]) @ w_ple_gate[i])
        x = x + ple_gate * (p[i] @ w_ple_proj[i])
    return rmsnorm(x, g_final)


import jax as _jax
import jax.numpy as _jnp

TWIN_FORMAT = 'train_step'
FWD_PARAMS = ['x', 'p', 'g_mix', 'w_in', 'conv_w', 'a_log', 'dt_bias', 'rel_bias', 'w_onorm', 'w_branch_a', 'w_branch_b', 'w_out', 'g_ffn', 'w_gate_up', 'w_down', 'g_ple', 'w_ple_gate', 'w_ple_proj', 'g_final']
TWIN_WEIGHTS = ['g_mix', 'w_in', 'conv_w', 'a_log', 'dt_bias', 'rel_bias', 'w_onorm', 'w_branch_a', 'w_branch_b', 'w_out', 'g_ffn', 'w_gate_up', 'w_down', 'g_ple', 'w_ple_gate', 'w_ple_proj', 'g_final']
TWIN_DIFF_INPUT = 'x'
TWIN_INPUTS = ['x', 'p', 'g_mix', 'w_in', 'conv_w', 'a_log', 'dt_bias', 'rel_bias', 'w_onorm', 'w_branch_a', 'w_branch_b', 'w_out', 'g_ffn', 'w_gate_up', 'w_down', 'g_ple', 'w_ple_gate', 'w_ple_proj', 'g_final', 'loss_target', 'm_g_mix', 'm_w_in', 'm_conv_w', 'm_a_log', 'm_dt_bias', 'm_rel_bias', 'm_w_onorm', 'm_w_branch_a', 'm_w_branch_b', 'm_w_out', 'm_g_ffn', 'm_w_gate_up', 'm_w_down', 'm_g_ple', 'm_w_ple_gate', 'm_w_ple_proj', 'm_g_final', 'v_g_mix', 'v_w_in', 'v_conv_w', 'v_a_log', 'v_dt_bias', 'v_rel_bias', 'v_w_onorm', 'v_w_branch_a', 'v_w_branch_b', 'v_w_out', 'v_g_ffn', 'v_w_gate_up', 'v_w_down', 'v_g_ple', 'v_w_ple_gate', 'v_w_ple_proj', 'v_g_final']
TWIN_OUTPUTS = ['loss', 'grad_x', 'grad_g_mix', 'grad_w_in', 'grad_conv_w', 'grad_a_log', 'grad_dt_bias', 'grad_rel_bias', 'grad_w_onorm', 'grad_w_branch_a', 'grad_w_branch_b', 'grad_w_out', 'grad_g_ffn', 'grad_w_gate_up', 'grad_w_down', 'grad_g_ple', 'grad_w_ple_gate', 'grad_w_ple_proj', 'grad_g_final', 'delta_g_mix', 'delta_w_in', 'delta_conv_w', 'delta_a_log', 'delta_dt_bias', 'delta_rel_bias', 'delta_w_onorm', 'delta_w_branch_a', 'delta_w_branch_b', 'delta_w_out', 'delta_g_ffn', 'delta_w_gate_up', 'delta_w_down', 'delta_g_ple', 'delta_w_ple_gate', 'delta_w_ple_proj', 'delta_g_final', 'new_m_g_mix', 'new_m_w_in', 'new_m_conv_w', 'new_m_a_log', 'new_m_dt_bias', 'new_m_rel_bias', 'new_m_w_onorm', 'new_m_w_branch_a', 'new_m_w_branch_b', 'new_m_w_out', 'new_m_g_ffn', 'new_m_w_gate_up', 'new_m_w_down', 'new_m_g_ple', 'new_m_w_ple_gate', 'new_m_w_ple_proj', 'new_m_g_final', 'new_v_g_mix', 'new_v_w_in', 'new_v_conv_w', 'new_v_a_log', 'new_v_dt_bias', 'new_v_rel_bias', 'new_v_w_onorm', 'new_v_w_branch_a', 'new_v_w_branch_b', 'new_v_w_out', 'new_v_g_ffn', 'new_v_w_gate_up', 'new_v_w_down', 'new_v_g_ple', 'new_v_w_ple_gate', 'new_v_w_ple_proj', 'new_v_g_final']
TWIN_LEAF_KINDS = {'loss': 'loss', 'grad_x': 'grad_x', 'grad_g_mix': 'grad_w', 'grad_w_in': 'grad_w', 'grad_conv_w': 'grad_w', 'grad_a_log': 'grad_w', 'grad_dt_bias': 'grad_w', 'grad_rel_bias': 'grad_w', 'grad_w_onorm': 'grad_w', 'grad_w_branch_a': 'grad_w', 'grad_w_branch_b': 'grad_w', 'grad_w_out': 'grad_w', 'grad_g_ffn': 'grad_w', 'grad_w_gate_up': 'grad_w', 'grad_w_down': 'grad_w', 'grad_g_ple': 'grad_w', 'grad_w_ple_gate': 'grad_w', 'grad_w_ple_proj': 'grad_w', 'grad_g_final': 'grad_w', 'delta_g_mix': 'delta_w', 'delta_w_in': 'delta_w', 'delta_conv_w': 'delta_w', 'delta_a_log': 'delta_w', 'delta_dt_bias': 'delta_w', 'delta_rel_bias': 'delta_w', 'delta_w_onorm': 'delta_w', 'delta_w_branch_a': 'delta_w', 'delta_w_branch_b': 'delta_w', 'delta_w_out': 'delta_w', 'delta_g_ffn': 'delta_w', 'delta_w_gate_up': 'delta_w', 'delta_w_down': 'delta_w', 'delta_g_ple': 'delta_w', 'delta_w_ple_gate': 'delta_w', 'delta_w_ple_proj': 'delta_w', 'delta_g_final': 'delta_w', 'new_m_g_mix': 'new_m', 'new_m_w_in': 'new_m', 'new_m_conv_w': 'new_m', 'new_m_a_log': 'new_m', 'new_m_dt_bias': 'new_m', 'new_m_rel_bias': 'new_m', 'new_m_w_onorm': 'new_m', 'new_m_w_branch_a': 'new_m', 'new_m_w_branch_b': 'new_m', 'new_m_w_out': 'new_m', 'new_m_g_ffn': 'new_m', 'new_m_w_gate_up': 'new_m', 'new_m_w_down': 'new_m', 'new_m_g_ple': 'new_m', 'new_m_w_ple_gate': 'new_m', 'new_m_w_ple_proj': 'new_m', 'new_m_g_final': 'new_m', 'new_v_g_mix': 'new_v', 'new_v_w_in': 'new_v', 'new_v_conv_w': 'new_v', 'new_v_a_log': 'new_v', 'new_v_dt_bias': 'new_v', 'new_v_rel_bias': 'new_v', 'new_v_w_onorm': 'new_v', 'new_v_w_branch_a': 'new_v', 'new_v_w_branch_b': 'new_v', 'new_v_w_out': 'new_v', 'new_v_g_ffn': 'new_v', 'new_v_w_gate_up': 'new_v', 'new_v_w_down': 'new_v', 'new_v_g_ple': 'new_v', 'new_v_w_ple_gate': 'new_v', 'new_v_w_ple_proj': 'new_v', 'new_v_g_final': 'new_v'}


def _forward(args):
    return _fwd_reference(*[args[k] for k in FWD_PARAMS])


def _output_shape():
    out = _jax.eval_shape(lambda: _forward(_fwd_setup_inputs(0)))
    return out.shape, out.dtype

N_MICROBATCH = 1
ADAM_LR = 0.001
ADAM_B1 = 0.9
ADAM_B2 = 0.999
ADAM_EPS = 1e-08
ADAM_WD = 0.01
ADAM_STEP = 10
PER_EXAMPLE_BATCH_AXIS = {'x': 0, 'p': 1, 'loss_target': 0}
SHARED_INPUTS = []
_WEIGHT_DTYPES = {'g_mix': _jnp.float32, 'w_in': _jnp.float32, 'conv_w': _jnp.float32, 'a_log': _jnp.float32, 'dt_bias': _jnp.float32, 'rel_bias': _jnp.float32, 'w_onorm': _jnp.float32, 'w_branch_a': _jnp.float32, 'w_branch_b': _jnp.float32, 'w_out': _jnp.float32, 'g_ffn': _jnp.float32, 'w_gate_up': _jnp.float32, 'w_down': _jnp.float32, 'g_ple': _jnp.float32, 'w_ple_gate': _jnp.float32, 'w_ple_proj': _jnp.float32, 'g_final': _jnp.float32}
MOMENT_SCALE = {'g_mix': 1.050068e-01, 'w_in': 4.306850e-02, 'conv_w': 6.511430e-02, 'a_log': 6.822348e-01, 'dt_bias': 6.723481e-01, 'rel_bias': 9.660735e-03, 'w_onorm': 1.877409e-01, 'w_branch_a': 1.475774e-02, 'w_branch_b': 5.858486e-02, 'w_out': 5.830025e-02, 'g_ffn': 1.337538e-01, 'w_gate_up': 5.242156e-02, 'w_down': 8.716680e-02, 'g_ple': 3.101422e-02, 'w_ple_gate': 3.094709e-02, 'w_ple_proj': 8.133027e-02, 'g_final': 3.217795e+01}


def _to_microbatches(a, axis):
    t = _jnp.moveaxis(a, axis, 0)
    t = t.reshape((N_MICROBATCH, t.shape[0] // N_MICROBATCH) + t.shape[1:])
    return _jnp.moveaxis(t, 1, axis + 1)


def setup_inputs(seed: int = 0) -> dict:
    inp = _fwd_setup_inputs(seed)
    key = _jax.random.fold_in(_jax.random.key(seed), 7919)
    shape, _ = _output_shape()
    out = dict(inp)
    out["loss_target"] = _jax.random.normal(_jax.random.fold_in(key, 0), shape, _jnp.float32)
    for i, name in enumerate(TWIN_WEIGHTS):
        w = inp[name].astype(_jnp.float32)
        if MOMENT_SCALE is None:
            s = _jnp.sqrt(_jnp.mean(_jnp.square(w)) + 1e-30)
        else:
            s = MOMENT_SCALE[name]
        km, kv = _jax.random.split(_jax.random.fold_in(key, i + 1))
        out[name] = w
        out["m_" + name] = s * _jax.random.normal(km, w.shape, _jnp.float32)
        out["v_" + name] = (s * s) * _jax.random.uniform(kv, w.shape, _jnp.float32, 0.5, 1.5)
    if N_MICROBATCH > 1:
        for name, axis in PER_EXAMPLE_BATCH_AXIS.items():
            out[name] = _to_microbatches(out[name], axis)
    return {'x': out['x'], 'p': out['p'], 'g_mix': out['g_mix'], 'w_in': out['w_in'], 'conv_w': out['conv_w'], 'a_log': out['a_log'], 'dt_bias': out['dt_bias'], 'rel_bias': out['rel_bias'], 'w_onorm': out['w_onorm'], 'w_branch_a': out['w_branch_a'], 'w_branch_b': out['w_branch_b'], 'w_out': out['w_out'], 'g_ffn': out['g_ffn'], 'w_gate_up': out['w_gate_up'], 'w_down': out['w_down'], 'g_ple': out['g_ple'], 'w_ple_gate': out['w_ple_gate'], 'w_ple_proj': out['w_ple_proj'], 'g_final': out['g_final'], 'loss_target': out['loss_target'], 'm_g_mix': out['m_g_mix'], 'm_w_in': out['m_w_in'], 'm_conv_w': out['m_conv_w'], 'm_a_log': out['m_a_log'], 'm_dt_bias': out['m_dt_bias'], 'm_rel_bias': out['m_rel_bias'], 'm_w_onorm': out['m_w_onorm'], 'm_w_branch_a': out['m_w_branch_a'], 'm_w_branch_b': out['m_w_branch_b'], 'm_w_out': out['m_w_out'], 'm_g_ffn': out['m_g_ffn'], 'm_w_gate_up': out['m_w_gate_up'], 'm_w_down': out['m_w_down'], 'm_g_ple': out['m_g_ple'], 'm_w_ple_gate': out['m_w_ple_gate'], 'm_w_ple_proj': out['m_w_ple_proj'], 'm_g_final': out['m_g_final'], 'v_g_mix': out['v_g_mix'], 'v_w_in': out['v_w_in'], 'v_conv_w': out['v_conv_w'], 'v_a_log': out['v_a_log'], 'v_dt_bias': out['v_dt_bias'], 'v_rel_bias': out['v_rel_bias'], 'v_w_onorm': out['v_w_onorm'], 'v_w_branch_a': out['v_w_branch_a'], 'v_w_branch_b': out['v_w_branch_b'], 'v_w_out': out['v_w_out'], 'v_g_ffn': out['v_g_ffn'], 'v_w_gate_up': out['v_w_gate_up'], 'v_w_down': out['v_w_down'], 'v_g_ple': out['v_g_ple'], 'v_w_ple_gate': out['v_w_ple_gate'], 'v_w_ple_proj': out['v_w_ple_proj'], 'v_g_final': out['v_g_final']}


def _loss(weights, diff, rest, loss_target):
    with _jax.named_scope("forward"):
        args = {**rest, TWIN_DIFF_INPUT: diff, **{k: w.astype(_WEIGHT_DTYPES[k]) for k, w in weights.items()}}
        y = _forward(args)
    with _jax.named_scope("loss_head"):
        err = _jnp.square(y.astype(_jnp.float32) - loss_target)
        return 0.5 * _jnp.sum(_jnp.mean(err, axis=-1)) if err.ndim else 0.5 * err


def _adamw(w, g, m, v):
    m = ADAM_B1 * m + (1.0 - ADAM_B1) * g
    v = ADAM_B2 * v + (1.0 - ADAM_B2) * _jnp.square(g)
    m_hat = m / (1.0 - ADAM_B1 ** ADAM_STEP)
    v_hat = v / (1.0 - ADAM_B2 ** ADAM_STEP)
    delta = -ADAM_LR * (m_hat / (_jnp.sqrt(v_hat) + ADAM_EPS) + ADAM_WD * w)
    return delta, m, v


def reference(x, p, g_mix, w_in, conv_w, a_log, dt_bias, rel_bias, w_onorm, w_branch_a, w_branch_b, w_out, g_ffn, w_gate_up, w_down, g_ple, w_ple_gate, w_ple_proj, g_final, loss_target, m_g_mix, m_w_in, m_conv_w, m_a_log, m_dt_bias, m_rel_bias, m_w_onorm, m_w_branch_a, m_w_branch_b, m_w_out, m_g_ffn, m_w_gate_up, m_w_down, m_g_ple, m_w_ple_gate, m_w_ple_proj, m_g_final, v_g_mix, v_w_in, v_conv_w, v_a_log, v_dt_bias, v_rel_bias, v_w_onorm, v_w_branch_a, v_w_branch_b, v_w_out, v_g_ffn, v_w_gate_up, v_w_down, v_g_ple, v_w_ple_gate, v_w_ple_proj, v_g_final):
    given = dict(x=x, p=p, g_mix=g_mix, w_in=w_in, conv_w=conv_w, a_log=a_log, dt_bias=dt_bias, rel_bias=rel_bias, w_onorm=w_onorm, w_branch_a=w_branch_a, w_branch_b=w_branch_b, w_out=w_out, g_ffn=g_ffn, w_gate_up=w_gate_up, w_down=w_down, g_ple=g_ple, w_ple_gate=w_ple_gate, w_ple_proj=w_ple_proj, g_final=g_final, loss_target=loss_target, m_g_mix=m_g_mix, m_w_in=m_w_in, m_conv_w=m_conv_w, m_a_log=m_a_log, m_dt_bias=m_dt_bias, m_rel_bias=m_rel_bias, m_w_onorm=m_w_onorm, m_w_branch_a=m_w_branch_a, m_w_branch_b=m_w_branch_b, m_w_out=m_w_out, m_g_ffn=m_g_ffn, m_w_gate_up=m_w_gate_up, m_w_down=m_w_down, m_g_ple=m_g_ple, m_w_ple_gate=m_w_ple_gate, m_w_ple_proj=m_w_ple_proj, m_g_final=m_g_final, v_g_mix=v_g_mix, v_w_in=v_w_in, v_conv_w=v_conv_w, v_a_log=v_a_log, v_dt_bias=v_dt_bias, v_rel_bias=v_rel_bias, v_w_onorm=v_w_onorm, v_w_branch_a=v_w_branch_a, v_w_branch_b=v_w_branch_b, v_w_out=v_w_out, v_g_ffn=v_g_ffn, v_w_gate_up=v_w_gate_up, v_w_down=v_w_down, v_g_ple=v_g_ple, v_w_ple_gate=v_w_ple_gate, v_w_ple_proj=v_w_ple_proj, v_g_final=v_g_final)
    weights = {n: given[n] for n in TWIN_WEIGHTS}
    shared = {n: given[n] for n in SHARED_INPUTS}
    per_example = {n: given[n] for n in ['x', 'p']}
    grad_fn = _jax.value_and_grad(_loss, argnums=(0, 1))

    def one_microbatch(ex, loss_target):
        ex = dict(ex)
        diff = ex.pop(TWIN_DIFF_INPUT)
        return grad_fn(weights, diff, {**shared, **ex}, loss_target)

    if N_MICROBATCH == 1:
        loss, (grad_w, grad_x) = one_microbatch(per_example, given["loss_target"])
    else:
        def body(carry, xs):
            loss_sum, grad_sum = carry
            l_k, (gw_k, gx_k) = one_microbatch(xs[0], xs[1])
            with _jax.named_scope("update"):
                return (loss_sum + l_k, _jax.tree.map(_jnp.add, grad_sum, gw_k)), gx_k

        init = (_jnp.zeros((), _jnp.float32), _jax.tree.map(_jnp.zeros_like, weights))
        (loss, grad_w), grad_x = _jax.lax.scan(body, init, (per_example, given["loss_target"]))
    with _jax.named_scope("update"):
        delta_w, new_m, new_v = {}, {}, {}
        for n in TWIN_WEIGHTS:
            delta_w[n], new_m[n], new_v[n] = _adamw(weights[n], grad_w[n], given["m_" + n], given["v_" + n])
    return (loss, grad_x, *[grad_w[n] for n in TWIN_WEIGHTS], *[delta_w[n] for n in TWIN_WEIGHTS],
            *[new_m[n] for n in TWIN_WEIGHTS], *[new_v[n] for n in TWIN_WEIGHTS])
```

```python
import jax
import jax.numpy as jnp
from jax import lax
from jax.experimental import pallas as pl
from jax.experimental.pallas import tpu as pltpu

F32 = jnp.float32
BF16 = jnp.bfloat16
HI = lax.Precision.HIGHEST
MESH = pl.DeviceIdType.MESH

N_DEV = 8
D = 1024
CHUNK = 64
EPS = 1e-6
A_HEADS, A_DIM, A_WIDTH = 8, 64, 512
A_BAND = 9 * CHUNK
A_PAD = 8 * CHUNK
REL_CLIP = 128
B_HEADS, B_DIM = 4, 128
B_CONV = 1536
D_FF = 2816
D_IN = 5640
P_CONV, P_Z, P_GATE, P_BD, P_END = 1536, 3072, 3584, 5632, 5760
LANE = 128

ADAM_LR, ADAM_B1, ADAM_B2, ADAM_EPS, ADAM_WD, ADAM_STEP = 0.001, 0.9, 0.999, 1e-08, 0.01, 10

NT = (((1,), (1,)), ((), ()))
TN = (((0,), (0,)), ((), ()))
NN = (((1,), (0,)), ((), ()))

HBM_SPEC = pl.BlockSpec(memory_space=pltpu.HBM)


def _tile(n, target, align=LANE):
    if n <= target:
        return n
    best = None
    for t in range(align, target + 1, align):
        if n % t == 0:
            best = t
    assert best is not None, (n, target, align)
    return best


def _mm(a, b, *, name, ta=False, add=None, out_dtype=F32, tm=512, tn=640, tk=1024):
    if ta:
        k_dim, m_dim = a.shape
    else:
        m_dim, k_dim = a.shape
    assert b.shape[0] == k_dim
    n_dim = b.shape[1]
    tm, tn = _tile(m_dim, tm), _tile(n_dim, tn)
    tk = _tile(k_dim, tk, 8 if ta else LANE)
    nk = k_dim // tk
    dn = TN if ta else NN

    def body(*refs):
        if add is None:
            a_ref, b_ref, o_ref = refs[:3]
            add_ref = None
        else:
            a_ref, b_ref, add_ref, o_ref = refs[:4]
        part = lax.dot_general(a_ref[...].astype(BF16), b_ref[...].astype(BF16), dn, preferred_element_type=F32)

        def finish(r):
            if add_ref is not None:
                r = r + add_ref[...]
            o_ref[...] = r.astype(o_ref.dtype)

        if nk == 1:
            finish(part)
        else:
            acc_ref = refs[-1]
            k = pl.program_id(2)

            @pl.when(k == 0)
            def _():
                acc_ref[...] = part

            @pl.when(k > 0)
            def _():
                acc_ref[...] += part

            @pl.when(k == nk - 1)
            def _():
                finish(acc_ref[...])

    a_spec = pl.BlockSpec((tk, tm), lambda i, j, k: (k, i)) if ta else pl.BlockSpec((tm, tk), lambda i, j, k: (i, k))
    in_specs = [a_spec, pl.BlockSpec((tk, tn), lambda i, j, k: (k, j))]
    args = [a, b]
    if add is not None:
        in_specs.append(pl.BlockSpec((tm, tn), lambda i, j, k: (i, j)))
        args.append(add)
    return pl.pallas_call(
        body,
        name=name,
        grid=(m_dim // tm, n_dim // tn, nk),
        in_specs=in_specs,
        out_specs=pl.BlockSpec((tm, tn), lambda i, j, k: (i, j)),
        out_shape=jax.ShapeDtypeStruct((m_dim, n_dim), out_dtype),
        scratch_shapes=[pltpu.VMEM((tm, tn), F32)] if nk > 1 else [],
        compiler_params=pltpu.CompilerParams(dimension_semantics=("parallel", "parallel", "arbitrary")),
    )(*args)


def _rowwise(fn, rows, bcs, outs, reds=(), *, name, tr, ncol=1):
    n_rows = rows[0][0].shape[0]
    tr = _tile(n_rows, tr, 8)
    nrow = n_rows // tr
    n_in, n_out = len(rows) + len(bcs), len(outs)

    def body(*refs):
        j, i = pl.program_id(0), pl.program_id(1)
        o_vals, r_vals = fn(*[r[...] for r in refs[:n_in]])
        for ref, val in zip(refs[n_in:n_in + n_out], o_vals):
            ref[...] = val.astype(ref.dtype)
        for ref, val, (_, _, stride) in zip(refs[n_in + n_out:], r_vals, reds):
            first = (i == 0) if stride else jnp.logical_and(i == 0, j == 0)

            @pl.when(first)
            def _():
                ref[...] = val

            @pl.when(jnp.logical_not(first))
            def _():
                ref[...] += val

    def spec(r, w, off, st, row_dep=True):
        if row_dep:
            return pl.BlockSpec((r, w), lambda j, i: (i, off + st * j))
        return pl.BlockSpec((r, w), lambda j, i: (0, off + st * j))

    in_specs = [spec(tr, w, off, st) for (_, w, off, st) in rows]
    in_specs += [spec(a.shape[0], w, off, st, False) for (a, w, off, st) in bcs]
    out_specs = [spec(tr, w, off, st) for (_, _, w, off, st) in outs]
    out_specs += [spec(1, w, 0, st, False) for (_, w, st) in reds]
    out_shape = [jax.ShapeDtypeStruct((n_rows, c), dt) for (c, dt, _, _, _) in outs]
    out_shape += [jax.ShapeDtypeStruct((1, c), F32) for (c, _, _) in reds]
    return pl.pallas_call(
        body,
        name=name,
        grid=(ncol, nrow),
        in_specs=in_specs,
        out_specs=out_specs,
        out_shape=out_shape,
        compiler_params=pltpu.CompilerParams(dimension_semantics=("arbitrary", "arbitrary")),
    )(*[r[0] for r in rows], *[b[0] for b in bcs])


def _full(a):
    return (a, a.shape[1], 0, 0)


def _rms(x, g):
    return x * lax.rsqrt(jnp.mean(x * x, axis=-1, keepdims=True) + EPS) * g


def _silu(x):
    return x * jax.nn.sigmoid(x)


def _softplus(x):
    return jnp.maximum(x, 0.0) + jnp.log(1.0 + jnp.exp(-jnp.abs(x)))


def _rms_fwd(x, g, *, name):
    (h,) = _rowwise(lambda xb, gb: ([_rms(xb, gb)], []), [_full(x)], [_full(g)], [(D, BF16, D, 0, 0)], name=name, tr=512)
    return h


def _rms_bwd(x, g, dh, dres, *, name):
    def fn(xb, dhb, dresb, gb):
        _, vjp = jax.vjp(_rms, xb, gb)
        dx, dg = vjp(dhb)
        return [dx + dresb], [dg]

    return _rowwise(fn, [_full(x), _full(dh), _full(dres)], [_full(g)], [(D, F32, D, 0, 0)], [(D, D, 0)], name=name, tr=256)


def _gate_scalars(raw, al_row, dtb_row):
    lane = lax.broadcasted_iota(jnp.int32, raw.shape, 1)
    beta = jax.nn.sigmoid(raw)
    g = -jnp.exp(al_row) * _softplus(raw + dtb_row)
    return jnp.where(lane < B_HEADS, beta, jnp.where(lane < 2 * B_HEADS, g, 0.0))


def _gated_norm(o, z, w):
    return _rms(o, w) * _silu(z)


def _merge(ga, gb, ta, tb):
    return jax.nn.sigmoid(ga) * ta + jax.nn.sigmoid(gb) * tb


def _swiglu(gu):
    return _silu(gu[:, :D_FF]) * gu[:, D_FF:]


def _head_loss(x2, pg, pp, tgt, g):
    x3 = x2 + jax.nn.sigmoid(pg) * pp
    err = _rms(x3, g) - tgt
    return 0.5 * jnp.sum(jnp.mean(err * err, axis=-1))


CONV_W = 256


def _conv_taps(x, w):
    row = lax.broadcasted_iota(jnp.int32, x.shape, 0)
    shifted = [x] + [jnp.where(row >= s, pltpu.roll(x, s, 0), 0.0) for s in (1, 2, 3)]
    pre = shifted[0] * w[3:4]
    for s in (1, 2, 3):
        pre = pre + shifted[s] * w[3 - s:4 - s]
    return pre, shifted


def _conv_fwd(projp, conv_w, n_batch, seq):
    ncol = B_CONV // CONV_W
    first = P_CONV // CONV_W

    def body(x_ref, w_ref, o_ref):
        pre, _ = _conv_taps(x_ref[...], w_ref[...])
        o_ref[...] = _silu(pre)

    return pl.pallas_call(
        body,
        name="conv_fwd",
        grid=(ncol, n_batch),
        in_specs=[pl.BlockSpec((seq, CONV_W), lambda j, b: (b, first + j)), pl.BlockSpec((4, CONV_W), lambda j, b: (0, j))],
        out_specs=pl.BlockSpec((seq, CONV_W), lambda j, b: (b, j)),
        out_shape=jax.ShapeDtypeStruct((n_batch * seq, B_CONV), F32),
        compiler_params=pltpu.CompilerParams(dimension_semantics=("parallel", "parallel")),
    )(projp, conv_w)


def _conv_bwd(projp, conv_w, dc, seg, n_batch, seq):
    width = dc.shape[1]
    ncol = width // CONV_W
    first_w = seg * ncol
    first_x = P_CONV // CONV_W + first_w

    def body(x_ref, w_ref, dc_ref, dx_ref, dw_ref):
        b = pl.program_id(1)
        w = w_ref[...]
        pre, shifted = _conv_taps(x_ref[...], w)
        sg = jax.nn.sigmoid(pre)
        dpre = dc_ref[...] * (sg * (1.0 + pre * (1.0 - sg)))
        row = lax.broadcasted_iota(jnp.int32, dpre.shape, 0)
        dx = dpre * w[3:4]
        for s in (1, 2, 3):
            dx = dx + jnp.where(row < seq - s, pltpu.roll(dpre, seq - s, 0), 0.0) * w[3 - s:4 - s]
        dx_ref[...] = dx.astype(dx_ref.dtype)
        for s in (0, 1, 2, 3):
            part = jnp.sum(dpre * shifted[s], axis=0, keepdims=True)

            @pl.when(b == 0)
            def _():
                dw_ref[3 - s:4 - s, :] = part

            @pl.when(b > 0)
            def _():
                dw_ref[3 - s:4 - s, :] += part

    return pl.pallas_call(
        body,
        name=f"conv_bwd_{seg}",
        grid=(ncol, n_batch),
        in_specs=[
            pl.BlockSpec((seq, CONV_W), lambda j, b: (b, first_x + j)),
            pl.BlockSpec((4, CONV_W), lambda j, b: (0, first_w + j)),
            pl.BlockSpec((seq, CONV_W), lambda j, b: (b, j)),
        ],
        out_specs=[pl.BlockSpec((seq, CONV_W), lambda j, b: (b, j)), pl.BlockSpec((4, CONV_W), lambda j, b: (0, j))],
        out_shape=[jax.ShapeDtypeStruct((n_batch * seq, width), BF16), jax.ShapeDtypeStruct((4, width), F32)],
        compiler_params=pltpu.CompilerParams(dimension_semantics=("arbitrary", "arbitrary")),
    )(projp, conv_w, dc)


def _attn_chunk(qc, kb, vb, bias2, valid, lane_lo):
    out = None
    for e in (0, 1):
        sel = lane_lo if e == 0 else jnp.logical_not(lane_lo)
        qm = jnp.where(sel, qc, 0.0) * (A_DIM ** -0.5)
        s = lax.dot_general(qm.astype(BF16), kb.astype(BF16), NT, preferred_element_type=F32) + bias2[e]
        s = jnp.where(valid, s, -1e30)
        p = jnp.exp(s - lax.stop_gradient(jnp.max(s, axis=-1, keepdims=True)))
        p = p / jnp.sum(p, axis=-1, keepdims=True)
        oe = jnp.where(sel, jnp.dot(p.astype(BF16), vb.astype(BF16), preferred_element_type=F32), 0.0)
        out = oe if out is None else out + oe
    return out


def _attn_masks(c):
    col = lax.broadcasted_iota(jnp.int32, (CHUNK, A_BAND), 1)
    valid = col + c * CHUNK >= A_PAD
    lane_lo = lax.broadcasted_iota(jnp.int32, (1, LANE), 1) < A_DIM
    return valid, lane_lo


def _attn_specs(seq):
    def blk(first):
        return pl.BlockSpec((seq, LANE), lambda hp, b: (b, first + hp))

    return blk, pl.BlockSpec((2, CHUNK, A_BAND), lambda hp, b: (hp, 0, 0))


def _attn_fwd(projp, bias, n_batch, seq):
    nc = seq // CHUNK
    blk, bias_spec = _attn_specs(seq)

    def body(q_ref, k_ref, v_ref, b_ref, o_ref, kp_ref, vp_ref):
        kp_ref[0:A_PAD, :] = jnp.zeros((A_PAD, LANE), F32)
        vp_ref[0:A_PAD, :] = jnp.zeros((A_PAD, LANE), F32)
        kp_ref[A_PAD:, :] = k_ref[...]
        vp_ref[A_PAD:, :] = v_ref[...]
        bias2 = b_ref[...]

        def step(c, carry):
            r0 = pl.multiple_of(c * CHUNK, CHUNK)
            valid, lane_lo = _attn_masks(c)
            out = _attn_chunk(q_ref[pl.ds(r0, CHUNK), :], kp_ref[pl.ds(r0, A_BAND), :], vp_ref[pl.ds(r0, A_BAND), :],
                              bias2, valid, lane_lo)
            o_ref[pl.ds(r0, CHUNK), :] = out.astype(o_ref.dtype)
            return carry

        lax.fori_loop(0, nc, step, 0)

    return pl.pallas_call(
        body,
        name="attn_fwd",
        grid=(A_HEADS // 2, n_batch),
        in_specs=[blk(0), blk(4), blk(8), bias_spec],
        out_specs=pl.BlockSpec((seq, LANE), lambda hp, b: (b, hp)),
        out_shape=jax.ShapeDtypeStruct((n_batch * seq, A_WIDTH), BF16),
        scratch_shapes=[pltpu.VMEM((A_PAD + seq, LANE), F32), pltpu.VMEM((A_PAD + seq, LANE), F32)],
        compiler_params=pltpu.CompilerParams(dimension_semantics=("parallel", "parallel")),
    )(projp, projp, projp, bias)


def _attn_bwd(projp, bias, dy, n_batch, seq):
    nc = seq // CHUNK
    blk, bias_spec = _attn_specs(seq)
    out_blk = pl.BlockSpec((seq, LANE), lambda hp, b: (b, hp))

    def body(q_ref, k_ref, v_ref, b_ref, dy_ref, dq_ref, dk_ref, dv_ref, db_ref, kp_ref, vp_ref, dkp_ref, dvp_ref):
        b = pl.program_id(1)
        kp_ref[0:A_PAD, :] = jnp.zeros((A_PAD, LANE), F32)
        vp_ref[0:A_PAD, :] = jnp.zeros((A_PAD, LANE), F32)
        kp_ref[A_PAD:, :] = k_ref[...]
        vp_ref[A_PAD:, :] = v_ref[...]
        dkp_ref[...] = jnp.zeros_like(dkp_ref)
        dvp_ref[...] = jnp.zeros_like(dvp_ref)
        bias2 = b_ref[...]

        @pl.when(b == 0)
        def _():
            db_ref[...] = jnp.zeros_like(db_ref)

        def step(c, carry):
            r0 = pl.multiple_of(c * CHUNK, CHUNK)
            valid, lane_lo = _attn_masks(c)
            _, vjp = jax.vjp(lambda q, k, v, bb: _attn_chunk(q, k, v, bb, valid, lane_lo), q_ref[pl.ds(r0, CHUNK), :],
                             kp_ref[pl.ds(r0, A_BAND), :], vp_ref[pl.ds(r0, A_BAND), :], bias2)
            dq, dk, dv, dbias = vjp(dy_ref[pl.ds(r0, CHUNK), :])
            dq_ref[pl.ds(r0, CHUNK), :] = dq.astype(dq_ref.dtype)
            dkp_ref[pl.ds(r0, A_BAND), :] += dk
            dvp_ref[pl.ds(r0, A_BAND), :] += dv
            db_ref[...] += dbias
            return carry

        lax.fori_loop(0, nc, step, 0)
        dk_ref[...] = dkp_ref[A_PAD:, :].astype(dk_ref.dtype)
        dv_ref[...] = dvp_ref[A_PAD:, :].astype(dv_ref.dtype)

    n_tok = n_batch * seq
    pad = pltpu.VMEM((A_PAD + seq, LANE), F32)
    return pl.pallas_call(
        body,
        name="attn_bwd",
        grid=(A_HEADS // 2, n_batch),
        in_specs=[blk(0), blk(4), blk(8), bias_spec, out_blk],
        out_specs=[out_blk, out_blk, out_blk, bias_spec],
        out_shape=[jax.ShapeDtypeStruct((n_tok, A_WIDTH), BF16)] * 3 + [jax.ShapeDtypeStruct((A_HEADS, CHUNK, A_BAND), F32)],
        scratch_shapes=[pad, pad, pad, pad],
        compiler_params=pltpu.CompilerParams(dimension_semantics=("arbitrary", "arbitrary")),
    )(projp, projp, projp, bias, dy)


def _rel_bias_table(rel_bias):
    far = rel_bias[:, 2 * REL_CLIP:]
    rev = rel_bias[:, ::-1]
    rows = []
    for i in range(CHUNK):
        n_far = A_BAND - 3 * CHUNK + i
        rows.append(jnp.concatenate([jnp.broadcast_to(far, (A_HEADS, n_far)), rev[:, :A_BAND - n_far]], axis=1))
    return jnp.stack(rows, axis=1)


def _delta_chunk(r_state, cq, ck, cv, beta, g):
    ii = lax.broadcasted_iota(jnp.int32, (CHUNK, CHUNK), 0)
    jj = lax.broadcasted_iota(jnp.int32, (CHUNK, CHUNK), 1)
    incl, strict, eye = ii >= jj, ii > jj, ii == jj
    q = cq * lax.rsqrt(jnp.sum(cq * cq, axis=-1, keepdims=True) + EPS) * (B_DIM ** -0.5)
    k = ck * lax.rsqrt(jnp.sum(ck * ck, axis=-1, keepdims=True) + EPS)
    g_b = jnp.broadcast_to(g, (CHUNK, CHUNK))
    g_row = jnp.sum(jnp.where(eye, g_b, 0.0), axis=0, keepdims=True)
    gc_col = jnp.sum(jnp.where(incl, g_row, 0.0), axis=1, keepdims=True)
    gc_row = jnp.sum(jnp.where(ii <= jj, g_b, 0.0), axis=0, keepdims=True)
    decay = jnp.where(incl, jnp.exp(jnp.where(incl, gc_col - gc_row, 0.0)), 0.0)
    kk = lax.dot_general(k, k, NT, precision=HI, preferred_element_type=F32)
    x = jnp.where(strict, -(beta * kk * decay), 0.0)
    inv = jnp.where(eye, 1.0, 0.0) + x
    pw = x
    for _ in range(5):
        pw = jnp.dot(pw, pw, precision=HI, preferred_element_type=F32)
        inv = inv + jnp.dot(inv, pw, precision=HI, preferred_element_type=F32)
    egc = jnp.exp(gc_col)
    u = jnp.dot(inv, beta * cv, precision=HI, preferred_element_type=F32)
    wk = jnp.dot(inv, (beta * egc) * k, precision=HI, preferred_element_type=F32)
    pqk = lax.dot_general(q, k, NT, precision=HI, preferred_element_type=F32) * decay
    g_last = gc_col[CHUNK - 1:CHUNK, :]
    kdec = k * jnp.exp(g_last - gc_col)
    w = u - jnp.dot(wk, r_state, precision=HI, preferred_element_type=F32)
    o = egc * jnp.dot(q, r_state, precision=HI, preferred_element_type=F32)
    o = o + jnp.dot(pqk, w, precision=HI, preferred_element_type=F32)
    r_new = jnp.exp(g_last) * r_state + lax.dot_general(kdec, w, TN, precision=HI, preferred_element_type=F32)
    return o, r_new


def _head_columns(bg, h):
    lane = lax.broadcasted_iota(jnp.int32, bg.shape, 1)
    beta = jnp.sum(jnp.where(lane == h, bg, 0.0), axis=1, keepdims=True)
    g = jnp.sum(jnp.where(lane == h + B_HEADS, bg, 0.0), axis=1, keepdims=True)
    return beta, g


def _delta_specs(seq):
    def blk(first):
        return pl.BlockSpec((seq, LANE), lambda b, h: (b, first + h))

    return blk, pl.BlockSpec((seq, LANE), lambda b, h: (b, 0))


def _delta_fwd(conv, bg, n_batch, seq):
    nc = seq // CHUNK
    blk, bg_spec = _delta_specs(seq)

    def body(q_ref, k_ref, v_ref, bg_ref, o_ref, st_ref, r_ref):
        h = pl.program_id(1)
        r_ref[...] = jnp.zeros_like(r_ref)

        def step(c, carry):
            rows = pl.ds(pl.multiple_of(c * CHUNK, CHUNK), CHUNK)
            beta, g = _head_columns(bg_ref[rows, :], h)
            r_state = r_ref[...]
            st_ref[0, c] = r_state
            o, r_new = _delta_chunk(r_state, q_ref[rows, :], k_ref[rows, :], v_ref[rows, :], beta, g)
            o_ref[rows, :] = o
            r_ref[...] = r_new
            return carry

        lax.fori_loop(0, nc, step, 0)

    n_tok = n_batch * seq
    return pl.pallas_call(
        body,
        name="delta_fwd",
        grid=(n_batch, B_HEADS),
        in_specs=[blk(0), blk(4), blk(8), bg_spec],
        out_specs=[pl.BlockSpec((seq, LANE), lambda b, h: (b, h)),
                   pl.BlockSpec((1, nc, B_DIM, B_DIM), lambda b, h: (b * B_HEADS + h, 0, 0, 0))],
        out_shape=[jax.ShapeDtypeStruct((n_tok, B_HEADS * B_DIM), F32),
                   jax.ShapeDtypeStruct((n_batch * B_HEADS, nc, B_DIM, B_DIM), F32)],
        scratch_shapes=[pltpu.VMEM((B_DIM, B_DIM), F32)],
        compiler_params=pltpu.CompilerParams(dimension_semantics=("parallel", "parallel")),
    )(conv, conv, conv, bg)


def _delta_bwd(conv, bg, states, do, n_batch, seq):
    nc = seq // CHUNK
    blk, bg_spec = _delta_specs(seq)
    out_blk = pl.BlockSpec((seq, LANE), lambda b, h: (b, h))

    def body(q_ref, k_ref, v_ref, bg_ref, st_ref, do_ref, dq_ref, dk_ref, dv_ref, dbg_ref, dr_ref):
        h = pl.program_id(1)
        dr_ref[...] = jnp.zeros_like(dr_ref)

        @pl.when(h == 0)
        def _():
            dbg_ref[...] = jnp.zeros_like(dbg_ref)

        def step(n, carry):
            c = nc - 1 - n
            rows = pl.ds(pl.multiple_of(c * CHUNK, CHUNK), CHUNK)
            bg_c = bg_ref[rows, :]
            beta, g = _head_columns(bg_c, h)
            _, vjp = jax.vjp(_delta_chunk, st_ref[0, c], q_ref[rows, :], k_ref[rows, :], v_ref[rows, :], beta, g)
            dr, dq, dk, dv, dbeta, dg = vjp((do_ref[rows, :], dr_ref[...]))
            dr_ref[...] = dr
            dq_ref[rows, :] = dq
            dk_ref[rows, :] = dk
            dv_ref[rows, :] = dv
            lane = lax.broadcasted_iota(jnp.int32, bg_c.shape, 1)
            dbg_ref[rows, :] += jnp.where(lane == h, dbeta, 0.0) + jnp.where(lane == h + B_HEADS, dg, 0.0)
            return carry

        lax.fori_loop(0, nc, step, 0)

    n_tok = n_batch * seq
    wide = jax.ShapeDtypeStruct((n_tok, B_HEADS * B_DIM), F32)
    return pl.pallas_call(
        body,
        name="delta_bwd",
        grid=(n_batch, B_HEADS),
        in_specs=[blk(0), blk(4), blk(8), bg_spec,
                  pl.BlockSpec((1, nc, B_DIM, B_DIM), lambda b, h: (b * B_HEADS + h, 0, 0, 0)), out_blk],
        out_specs=[out_blk, out_blk, out_blk, bg_spec],
        out_shape=[wide, wide, wide, jax.ShapeDtypeStruct((n_tok, LANE), F32)],
        scratch_shapes=[pltpu.VMEM((B_DIM, B_DIM), F32)],
        compiler_params=pltpu.CompilerParams(dimension_semantics=("arbitrary", "arbitrary")),
    )(conv, conv, conv, bg, states, do)


def _lane_row(vec4, first):
    return jnp.concatenate([jnp.zeros((1, first), F32), vec4.reshape(1, B_HEADS).astype(F32),
                            jnp.zeros((1, LANE - first - B_HEADS), F32)], axis=1)


def _local_step(x3d, p3d, tgt3d, w, small):
    n_batch, seq, _ = x3d.shape
    n_tok = n_batch * seq
    x, p, tgt = x3d.reshape(n_tok, D), p3d.reshape(n_tok, -1), tgt3d.reshape(n_tok, D)
    g_mix, g_ffn, g_ple, g_final = (small[k].reshape(1, D) for k in ("g_mix", "g_ffn", "g_ple", "g_final"))
    w_onorm = small["w_onorm"].reshape(1, B_DIM)
    al_row = _lane_row(small["a_log"], B_HEADS)
    dtb_row = _lane_row(small["dt_bias"], B_HEADS)
    rel_bias = small["rel_bias"].reshape(A_HEADS, -1)
    bias = _rel_bias_table(rel_bias)
    conv_w = small["conv_w"].reshape(4, B_CONV)
    bd_blk = P_BD // LANE

    h1 = _rms_fwd(x, g_mix, name="rms_mix")
    projp = _mm(h1, w["w_in"], name="mm_proj", tn=640)
    y_a = _attn_fwd(projp, bias, n_batch, seq)
    conv = _conv_fwd(projp, conv_w, n_batch, seq)
    (bg,) = _rowwise(lambda raw, al, dtb: ([_gate_scalars(raw, al, dtb)], []), [(projp, LANE, bd_blk, 0)],
                     [_full(al_row), _full(dtb_row)], [(LANE, F32, LANE, 0, 0)], name="gate_scalars", tr=1024)
    o_b, states = _delta_fwd(conv, bg, n_batch, seq)
    (y_b,) = _rowwise(lambda o, z, wn: ([_gated_norm(o, z, wn)], []), [(o_b, LANE, 0, 1), (projp, LANE, P_Z // LANE, 1)],
                      [_full(w_onorm)], [(B_HEADS * B_DIM, BF16, LANE, 0, 1)], name="gated_norm", tr=1024, ncol=B_HEADS)
    t_a = _mm(y_a, w["w_branch_a"], name="mm_branch_a", tn=1024)
    t_b = _mm(y_b, w["w_branch_b"], name="mm_branch_b", tn=1024)
    half = D // 2
    gate_rows = [(projp, half, P_GATE // half, 1), (projp, half, P_GATE // half + 2, 1), (t_a, half, 0, 1), (t_b, half, 0, 1)]
    (merged,) = _rowwise(lambda ga, gb, ta, tb: ([_merge(ga, gb, ta, tb)], []), gate_rows, [], [(D, BF16, half, 0, 1)],
                         name="merge", tr=512, ncol=2)
    x1 = _mm(merged, w["w_out"], add=x, name="mm_out", tn=1024)
    h2 = _rms_fwd(x1, g_ffn, name="rms_ffn")
    gu = _mm(h2, w["w_gate_up"], name="mm_gate_up", tn=512)
    (act,) = _rowwise(lambda gub: ([_swiglu(gub)], []), [_full(gu)], [], [(D_FF, BF16, D_FF, 0, 0)], name="swiglu", tr=256)
    x2 = _mm(act, w["w_down"], add=x1, name="mm_down", tn=1024, tk=1408)
    h3 = _rms_fwd(x2, g_ple, name="rms_ple")
    pg = _mm(h3, w["w_ple_gate"], name="mm_ple_gate", tn=1024)
    pp = _mm(p, w["w_ple_proj"], name="mm_ple_proj", tn=1024)

    def head_fn(x2b, pgb, ppb, tb, gb):
        loss, (dx2, dpg, dpp, dg) = jax.value_and_grad(_head_loss, argnums=(0, 1, 2, 4))(x2b, pgb, ppb, tb, gb)
        return [dx2, dpg, dpp], [dg, jnp.full((1, LANE), loss, F32)]

    dx3, dpg, dpp, dg_final, loss_row = _rowwise(
        head_fn, [_full(x2), _full(pg), _full(pp), _full(tgt)], [_full(g_final)],
        [(D, F32, D, 0, 0), (D, BF16, D, 0, 0), (D, BF16, D, 0, 0)], [(D, D, 0), (LANE, LANE, 0)], name="loss_head", tr=256)
    gw = {}
    gw["w_ple_proj"] = _mm(p, dpp, ta=True, name="mm_d_ple_proj", tm=256, tn=1024)
    gw["w_ple_gate"] = _mm(h3, dpg, ta=True, name="mm_d_ple_gate", tn=1024)
    dh3 = _mm(dpg, w["w_ple_gate"].T, name="mm_dh3", tn=1024)
    dx2, dg_ple = _rms_bwd(x2, g_ple, dh3, dx3, name="rms_ple_bwd")
    gw["w_down"] = _mm(act, dx2, ta=True, name="mm_d_down", tm=256, tn=1024)
    dact = _mm(dx2, w["w_down"].T, name="mm_dact", tn=1408)

    def swiglu_bwd(gub, dab):
        _, vjp = jax.vjp(_swiglu, gub)
        return [vjp(dab)[0]], []

    (dgu,) = _rowwise(swiglu_bwd, [_full(gu), _full(dact)], [], [(2 * D_FF, BF16, 2 * D_FF, 0, 0)], name="swiglu_bwd", tr=256)
    gw["w_gate_up"] = _mm(h2, dgu, ta=True, name="mm_d_gate_up", tn=512)
    dh2 = _mm(dgu, w["w_gate_up"].T, name="mm_dh2", tn=1024, tk=1408)
    dx1, dg_ffn = _rms_bwd(x1, g_ffn, dh2, dx2, name="rms_ffn_bwd")
    gw["w_out"] = _mm(merged, dx1, ta=True, name="mm_d_out", tn=1024)
    dmerged = _mm(dx1, w["w_out"].T, name="mm_dmerged", tn=1024)

    def merge_bwd(ga, gb, ta, tb, dm):
        _, vjp = jax.vjp(_merge, ga, gb, ta, tb)
        return list(vjp(dm)), []

    dga, dgb, dta, dtb = _rowwise(merge_bwd, gate_rows + [(dmerged, half, 0, 1)], [], [(D, BF16, half, 0, 1)] * 4,
                                  name="merge_bwd", tr=512, ncol=2)
    gw["w_branch_a"] = _mm(y_a, dta, ta=True, name="mm_d_branch_a", tn=1024)
    gw["w_branch_b"] = _mm(y_b, dtb, ta=True, name="mm_d_branch_b", tn=1024)
    dya = _mm(dta, w["w_branch_a"].T, name="mm_dya", tn=512)
    dyb = _mm(dtb, w["w_branch_b"].T, name="mm_dyb", tn=512)

    def gated_norm_bwd(o, z, dy, wn):
        _, vjp = jax.vjp(_gated_norm, o, z, wn)
        do, dz, dwn = vjp(dy)
        return [do, dz], [dwn]

    do_b, dz, dw_onorm = _rowwise(
        gated_norm_bwd, [(o_b, LANE, 0, 1), (projp, LANE, P_Z // LANE, 1), (dyb, LANE, 0, 1)], [_full(w_onorm)],
        [(B_HEADS * B_DIM, F32, LANE, 0, 1), (B_HEADS * B_DIM, BF16, LANE, 0, 1)], [(B_DIM, B_DIM, 0)],
        name="gated_norm_bwd", tr=1024, ncol=B_HEADS)
    dcq, dck, dcv, dbg = _delta_bwd(conv, bg, states, do_b, n_batch, seq)

    def gate_scalars_bwd(raw, dbgb, al, dtb):
        _, vjp = jax.vjp(_gate_scalars, raw, al, dtb)
        draw, dal, ddtb = vjp(dbgb)
        return [draw], [dal, ddtb]

    dbd, dal_row, ddtb_row = _rowwise(gate_scalars_bwd, [(projp, LANE, bd_blk, 0), _full(dbg)], [_full(al_row), _full(dtb_row)],
                                      [(LANE, BF16, LANE, 0, 0)], [(LANE, LANE, 0), (LANE, LANE, 0)], name="gate_scalars_bwd",
                                      tr=1024)
    dconv, dconv_w = [], []
    for seg, dc in enumerate((dcq, dck, dcv)):
        dx_seg, dw_seg = _conv_bwd(projp, conv_w, dc, seg, n_batch, seq)
        dconv.append(dx_seg)
        dconv_w.append(dw_seg)
    dq_a, dk_a, dv_a, dbias = _attn_bwd(projp, bias, dya, n_batch, seq)
    dprojp = jnp.concatenate([dq_a, dk_a, dv_a] + dconv + [dz, dga, dgb, dbd], axis=1)
    gw["w_in"] = _mm(h1, dprojp, ta=True, name="mm_d_in", tn=640)
    dh1 = _mm(dprojp, w["w_in"].T, name="mm_dh1", tn=1024, tk=1152)
    grad_x, dg_mix = _rms_bwd(x, g_mix, dh1, dx1, name="rms_mix_bwd")

    _, bias_vjp = jax.vjp(_rel_bias_table, rel_bias)
    gs = {
        "g_mix": dg_mix, "g_ffn": dg_ffn, "g_ple": dg_ple, "g_final": dg_final, "w_onorm": dw_onorm,
        "conv_w": jnp.concatenate(dconv_w, axis=1), "rel_bias": bias_vjp(dbias)[0],
        "a_log": dal_row[0, B_HEADS:2 * B_HEADS], "dt_bias": ddtb_row[0, B_HEADS:2 * B_HEADS],
    }
    return loss_row[:, :1], grad_x.reshape(n_batch, seq, D), gw, gs


MATRICES = (
    ("w_in", (D, D_IN), 1), ("w_gate_up", (D, 2 * D_FF), 1), ("w_branch_a", (A_WIDTH, D), 1), ("w_branch_b", (A_WIDTH, D), 1),
    ("w_out", (D, D), 0), ("w_down", (D_FF, D), 0), ("w_ple_gate", (D, D), 0), ("w_ple_proj", (256, D), 1),
)
MATRIX_ROWS = sum(s[0] * s[1] for _, s, _ in MATRICES) // (N_DEV * D)
TAP_ROWS = 2
PACK_ROWS = 2208
TAPS_PER_SHARD = B_CONV // N_DEV


def _shard_shape(shape, axis):
    return tuple(n // N_DEV if a == axis else n for a, n in enumerate(shape))


def _pack_shards(shards, taps):
    parts = [shards[name].reshape(-1, D).astype(BF16) for name, _, _ in MATRICES]
    tap_bits = lax.bitcast_convert_type(taps, BF16).reshape(1, -1)
    parts.append(jnp.concatenate([tap_bits, jnp.zeros((1, TAP_ROWS * D - tap_bits.shape[1]), BF16)], axis=1).reshape(TAP_ROWS, D))
    parts.append(jnp.zeros((PACK_ROWS - MATRIX_ROWS - TAP_ROWS, D), BF16))
    return jnp.concatenate(parts, axis=0)


def _unpack_gathered(slabs):
    out, r0 = {}, 0
    for name, shape, axis in MATRICES:
        shard = _shard_shape(shape, axis)
        rows = shard[0] * shard[1] // D
        part = slabs[:, r0:r0 + rows, :].reshape((N_DEV,) + shard)
        out[name] = part.reshape(shape) if axis == 0 else jnp.transpose(part, (1, 0, 2)).reshape(shape)
        r0 += rows
    tap_bits = slabs[:, r0:r0 + TAP_ROWS, :].reshape(N_DEV, TAP_ROWS * D)[:, :4 * TAPS_PER_SHARD * 2]
    taps = lax.bitcast_convert_type(tap_bits.reshape(N_DEV, 4, TAPS_PER_SHARD, 2), F32)
    return out, jnp.transpose(taps, (1, 0, 2)).reshape(4, B_CONV)


def _pack_by_owner(full):
    parts = []
    for name, shape, axis in MATRICES:
        shard = _shard_shape(shape, axis)
        m = full[name].astype(BF16)
        if axis == 0:
            m = m.reshape((N_DEV,) + shard)
        else:
            m = jnp.transpose(m.reshape(shape[0], N_DEV, shard[1]), (1, 0, 2))
        parts.append(m.reshape(N_DEV, -1, D))
    parts.append(jnp.zeros((N_DEV, PACK_ROWS - MATRIX_ROWS, D), BF16))
    return jnp.concatenate(parts, axis=1)


def _unpack_shard(slab):
    out, r0 = {}, 0
    for name, shape, axis in MATRICES:
        shard = _shard_shape(shape, axis)
        rows = shard[0] * shard[1] // D
        out[name] = slab[r0:r0 + rows, :].reshape(shard)
        r0 += rows
    return out


def _permute_w_in(w_in):
    zeros = jnp.zeros((D, P_END - D_IN), w_in.dtype)
    return jnp.concatenate([w_in[:, :P_GATE], w_in[:, P_GATE + 8:], w_in[:, P_GATE:P_GATE + 8], zeros], axis=1)


def _unpermute_w_in(gp):
    return jnp.concatenate([gp[:, :P_GATE], gp[:, P_BD:P_BD + 8], gp[:, P_GATE:P_BD]], axis=1)


SMALL_ROWS = 16
SMALL_LAYOUT = (("g_mix", 0, D), ("g_ffn", 1, D), ("g_ple", 2, D), ("g_final", 3, D), ("conv_w", 4, 4 * B_CONV),
                ("rel_bias", 10, A_HEADS * (2 * REL_CLIP + 1)), ("w_onorm", 13, B_DIM), ("a_log", 14, B_HEADS),
                ("dt_bias", 14, B_HEADS), ("loss", 15, 1))


def _pack_small(gs):
    rows = {}
    for name, row, n in SMALL_LAYOUT:
        rows.setdefault(row, []).append(gs[name].reshape(-1).astype(F32))
    parts = []
    for row in sorted(rows):
        flat = jnp.concatenate(rows[row])
        parts.append(jnp.concatenate([flat, jnp.zeros((-flat.shape[0] % D,), F32)]))
    flat = jnp.concatenate(parts)
    assert flat.shape[0] == SMALL_ROWS * D, flat.shape
    return flat.reshape(SMALL_ROWS, D)


def _unpack_small(blk):
    flat, out, used = blk.reshape(-1), {}, {}
    for name, row, n in SMALL_LAYOUT:
        start = row * D + used.get(row, 0)
        out[name] = flat[start:start + n]
        used[row] = used.get(row, 0) + n
    return out


def _position():
    return lax.axis_index("x"), lax.axis_index("y"), lax.axis_index("c")


def _weights_allgather(pack):
    def body(x_ref, out_ref, send_sems, recv_sems, local_sem):
        x, y, c = _position()
        me, sibling = (x, y, c), (x, y, 1 - c)
        chips = [(1 - x, y), (x, 1 - y), (1 - x, 1 - y)]

        def slab(px, py, pc):
            return out_ref.at[4 * px + 2 * py + pc]

        def copy(k, block, to, src=None):
            return pltpu.make_async_remote_copy(src_ref=slab(*block) if src is None else src, dst_ref=slab(*block),
                                                send_sem=send_sems.at[k], recv_sem=recv_sems.at[k], device_id=to,
                                                device_id_type=MESH)

        mine = pltpu.make_async_copy(x_ref, slab(*me), local_sem)
        mine.start()
        first = [copy(0, me, sibling, src=x_ref)]
        first += [copy(1 + j, me, (*chip, c), src=x_ref) for j, chip in enumerate(chips)]
        for cp in first:
            cp.start()
        passed = [copy(4 + j, (*chip, c), sibling) for j, chip in enumerate(chips)]
        for j, chip in enumerate(chips):
            copy(1 + j, (*chip, c), me).wait_recv()
            passed[j].start()
        copy(0, sibling, me).wait_recv()
        for j, chip in enumerate(chips):
            copy(4 + j, (*chip, 1 - c), me).wait_recv()
        for cp in first + passed:
            cp.wait_send()
        mine.wait()

    return pl.pallas_call(
        body,
        name="weights_allgather",
        out_shape=jax.ShapeDtypeStruct((N_DEV,) + pack.shape, pack.dtype),
        in_specs=[HBM_SPEC],
        out_specs=HBM_SPEC,
        scratch_shapes=[pltpu.SemaphoreType.DMA((7,)), pltpu.SemaphoreType.DMA((7,)), pltpu.SemaphoreType.DMA],
    )(pack)


def _grads_exchange(by_owner, small):
    def body(g_ref, s_ref, got_ref, got_s_ref, send_sems, recv_sems, local_sems):
        x, y, c = _position()
        mine = 4 * x + 2 * y + c
        own = pltpu.make_async_copy(g_ref.at[mine], got_ref.at[mine], local_sems.at[0])
        own_s = pltpu.make_async_copy(s_ref, got_s_ref.at[mine], local_sems.at[1])
        own.start()
        own_s.start()
        flips = [(dx, dy, dc) for dx in (0, 1) for dy in (0, 1) for dc in (0, 1) if dx + dy + dc]
        copies = []
        for k, (dx, dy, dc) in enumerate(flips):
            px, py, pc = (1 - x if dx else x), (1 - y if dy else y), (1 - c if dc else c)
            peer = 4 * px + 2 * py + pc

            def remote(src, dst, sem):
                return pltpu.make_async_remote_copy(src_ref=src, dst_ref=dst, send_sem=send_sems.at[sem],
                                                    recv_sem=recv_sems.at[sem], device_id=(px, py, pc), device_id_type=MESH)

            big, sml = remote(g_ref.at[peer], got_ref.at[mine], k), remote(s_ref, got_s_ref.at[mine], 7 + k)
            big.start()
            sml.start()
            copies += [(big, remote(g_ref.at[peer], got_ref.at[peer], k)), (sml, remote(s_ref, got_s_ref.at[peer], 7 + k))]
        for sent, landed in copies:
            landed.wait_recv()
            sent.wait_send()
        own.wait()
        own_s.wait()

    return pl.pallas_call(
        body,
        name="grads_exchange",
        out_shape=[jax.ShapeDtypeStruct(by_owner.shape, by_owner.dtype), jax.ShapeDtypeStruct((N_DEV,) + small.shape, small.dtype)],
        in_specs=[HBM_SPEC, HBM_SPEC],
        out_specs=[HBM_SPEC, HBM_SPEC],
        scratch_shapes=[pltpu.SemaphoreType.DMA((14,)), pltpu.SemaphoreType.DMA((14,)), pltpu.SemaphoreType.DMA((2,))],
    )(by_owner, small)


def _sum_slots(got, *, name, tr):
    _, rows, cols = got.shape
    tr = _tile(rows, tr, 16)

    def body(g_ref, o_ref):
        acc = g_ref[0].astype(F32)
        for j in range(1, N_DEV):
            acc = acc + g_ref[j].astype(F32)
        o_ref[...] = acc

    return pl.pallas_call(
        body,
        name=name,
        grid=(rows // tr,),
        in_specs=[pl.BlockSpec((N_DEV, tr, cols), lambda i: (0, i, 0))],
        out_specs=pl.BlockSpec((tr, cols), lambda i: (i, 0)),
        out_shape=jax.ShapeDtypeStruct((rows, cols), F32),
        compiler_params=pltpu.CompilerParams(dimension_semantics=("parallel",)),
    )(got)


def _adamw(wt, g, m, v, *, name):
    shape = wt.shape
    two_d = (-1, shape[-1]) if wt.ndim > 1 else (1, -1)
    args = [a.reshape(two_d) for a in (wt, g, m, v)]
    rows, cols = args[0].shape
    tr = _tile(rows, 256, 8) if rows % 8 == 0 else rows

    def body(w_ref, g_ref, m_ref, v_ref, d_ref, nm_ref, nv_ref):
        gv = g_ref[...]
        m2 = ADAM_B1 * m_ref[...] + (1.0 - ADAM_B1) * gv
        v2 = ADAM_B2 * v_ref[...] + (1.0 - ADAM_B2) * (gv * gv)
        m_hat = m2 / (1.0 - ADAM_B1 ** ADAM_STEP)
        v_hat = v2 / (1.0 - ADAM_B2 ** ADAM_STEP)
        d_ref[...] = -ADAM_LR * (m_hat / (jnp.sqrt(v_hat) + ADAM_EPS) + ADAM_WD * w_ref[...])
        nm_ref[...] = m2
        nv_ref[...] = v2

    spec = pl.BlockSpec((tr, cols), lambda i: (i, 0))
    outs = pl.pallas_call(
        body,
        name=name,
        grid=(rows // tr,),
        in_specs=[spec] * 4,
        out_specs=[spec] * 3,
        out_shape=[jax.ShapeDtypeStruct((rows, cols), F32)] * 3,
        compiler_params=pltpu.CompilerParams(dimension_semantics=("parallel",)),
    )(*args)
    return tuple(o.reshape(shape) for o in outs)


WEIGHTS = ("g_mix", "w_in", "conv_w", "a_log", "dt_bias", "rel_bias", "w_onorm", "w_branch_a", "w_branch_b", "w_out", "g_ffn",
           "w_gate_up", "w_down", "g_ple", "w_ple_gate", "w_ple_proj", "g_final")


def kernel(x, p, g_mix, w_in, conv_w, a_log, dt_bias, rel_bias, w_onorm, w_branch_a, w_branch_b, w_out, g_ffn, w_gate_up, w_down, g_ple, w_ple_gate, w_ple_proj, g_final, loss_target, m_g_mix, m_w_in, m_conv_w, m_a_log, m_dt_bias, m_rel_bias, m_w_onorm, m_w_branch_a, m_w_branch_b, m_w_out, m_g_ffn, m_w_gate_up, m_w_down, m_g_ple, m_w_ple_gate, m_w_ple_proj, m_g_final, v_g_mix, v_w_in, v_conv_w, v_a_log, v_dt_bias, v_rel_bias, v_w_onorm, v_w_branch_a, v_w_branch_b, v_w_out, v_g_ffn, v_w_gate_up, v_w_down, v_g_ple, v_w_ple_gate, v_w_ple_proj, v_g_final):
    given = dict(g_mix=g_mix, w_in=w_in, conv_w=conv_w, a_log=a_log, dt_bias=dt_bias, rel_bias=rel_bias, w_onorm=w_onorm,
                 w_branch_a=w_branch_a, w_branch_b=w_branch_b, w_out=w_out, g_ffn=g_ffn, w_gate_up=w_gate_up, w_down=w_down,
                 g_ple=g_ple, w_ple_gate=w_ple_gate, w_ple_proj=w_ple_proj, g_final=g_final)
    mom1 = dict(g_mix=m_g_mix, w_in=m_w_in, conv_w=m_conv_w, a_log=m_a_log, dt_bias=m_dt_bias, rel_bias=m_rel_bias,
                w_onorm=m_w_onorm, w_branch_a=m_w_branch_a, w_branch_b=m_w_branch_b, w_out=m_w_out, g_ffn=m_g_ffn,
                w_gate_up=m_w_gate_up, w_down=m_w_down, g_ple=m_g_ple, w_ple_gate=m_w_ple_gate, w_ple_proj=m_w_ple_proj,
                g_final=m_g_final)
    mom2 = dict(g_mix=v_g_mix, w_in=v_w_in, conv_w=v_conv_w, a_log=v_a_log, dt_bias=v_dt_bias, rel_bias=v_rel_bias,
                w_onorm=v_w_onorm, w_branch_a=v_w_branch_a, w_branch_b=v_w_branch_b, w_out=v_w_out, g_ffn=v_g_ffn,
                w_gate_up=v_w_gate_up, w_down=v_w_down, g_ple=v_g_ple, w_ple_gate=v_w_ple_gate, w_ple_proj=v_w_ple_proj,
                g_final=v_g_final)
    mine = 4 * lax.axis_index("x") + 2 * lax.axis_index("y") + lax.axis_index("c")

    slabs = _weights_allgather(_pack_shards({name: given[name][0] for name, _, _ in MATRICES}, conv_w[0]))
    full, conv_full = _unpack_gathered(slabs)
    full["w_in"] = _permute_w_in(full["w_in"])
    small = dict(g_mix=g_mix, g_ffn=g_ffn, g_ple=g_ple, g_final=g_final, w_onorm=w_onorm, a_log=a_log, dt_bias=dt_bias,
                 rel_bias=rel_bias, conv_w=conv_full)

    loss_part, grad_x, gw, gs = _local_step(x, p[0], loss_target, full, small)
    gw["w_in"] = _unpermute_w_in(gw["w_in"])
    gs["loss"] = loss_part

    got, got_small = _grads_exchange(_pack_by_owner(gw), _pack_small(gs))
    grads = _unpack_shard(_sum_slots(got, name="sum_matrix_grads", tr=128))
    small_sum = _unpack_small(_sum_slots(got_small, name="sum_small_grads", tr=16))
    loss = small_sum.pop("loss")[0]
    conv_all = small_sum.pop("conv_w").reshape(4, N_DEV, TAPS_PER_SHARD)
    grads["conv_w"] = lax.dynamic_index_in_dim(conv_all, mine, axis=1, keepdims=False)
    grads.update(small_sum)

    out_g, out_d, out_m, out_v = [], [], [], []
    for name in WEIGHTS:
        g = grads[name].reshape(given[name].shape)
        delta, new_m, new_v = _adamw(given[name], g, mom1[name], mom2[name], name=f"adamw_{name}")
        out_g.append(g)
        out_d.append(delta)
        out_m.append(new_m)
        out_v.append(new_v)
    return (loss, grad_x, *out_g, *out_d, *out_m, *out_v)
```

```python
import jax
import jax.numpy as jnp
from jax import lax
from jax.experimental import pallas as pl
from jax.experimental.pallas import tpu as pltpu

F32 = jnp.float32
BF16 = jnp.bfloat16
DELTA_PREC = lax.Precision.HIGH
MESH = pl.DeviceIdType.MESH

N_DEV = 8
D = 1024
CHUNK = 64
EPS = 1e-6
A_HEADS, A_DIM, A_WIDTH = 8, 64, 512
A_BAND = 9 * CHUNK
A_PAD = 8 * CHUNK
REL_CLIP = 128
B_HEADS, B_DIM = 4, 128
B_CONV = 1536
D_FF = 2816
D_IN = 5640
P_CONV, P_Z, P_GATE, P_BD, P_END = 1536, 3072, 3584, 5632, 5760
LANE = 128

ADAM_LR, ADAM_B1, ADAM_B2, ADAM_EPS, ADAM_WD, ADAM_STEP = 0.001, 0.9, 0.999, 1e-08, 0.01, 10

NT = (((1,), (1,)), ((), ()))
TN = (((0,), (0,)), ((), ()))
NN = (((1,), (0,)), ((), ()))

HBM_SPEC = pl.BlockSpec(memory_space=pltpu.HBM)


def _tile(n, target, align=LANE):
    if n <= target:
        return n
    best = None
    for t in range(align, target + 1, align):
        if n % t == 0:
            best = t
    assert best is not None, (n, target, align)
    return best


def _mm(a, b, *, name, ta=False, add=None, out_dtype=F32, tm=512, tn=640, tk=1024):
    if ta:
        k_dim, m_dim = a.shape
    else:
        m_dim, k_dim = a.shape
    assert b.shape[0] == k_dim
    n_dim = b.shape[1]
    tm, tn = _tile(m_dim, tm), _tile(n_dim, tn)
    tk = _tile(k_dim, tk, 8 if ta else LANE)
    nk = k_dim // tk
    dn = TN if ta else NN

    def body(*refs):
        if add is None:
            a_ref, b_ref, o_ref = refs[:3]
            add_ref = None
        else:
            a_ref, b_ref, add_ref, o_ref = refs[:4]
        part = lax.dot_general(a_ref[...].astype(BF16), b_ref[...].astype(BF16), dn, preferred_element_type=F32)

        def finish(r):
            if add_ref is not None:
                r = r + add_ref[...]
            o_ref[...] = r.astype(o_ref.dtype)

        if nk == 1:
            finish(part)
        else:
            acc_ref = refs[-1]
            k = pl.program_id(2)

            @pl.when(k == 0)
            def _():
                acc_ref[...] = part

            @pl.when(k > 0)
            def _():
                acc_ref[...] += part

            @pl.when(k == nk - 1)
            def _():
                finish(acc_ref[...])

    a_spec = pl.BlockSpec((tk, tm), lambda i, j, k: (k, i)) if ta else pl.BlockSpec((tm, tk), lambda i, j, k: (i, k))
    in_specs = [a_spec, pl.BlockSpec((tk, tn), lambda i, j, k: (k, j))]
    args = [a, b]
    if add is not None:
        in_specs.append(pl.BlockSpec((tm, tn), lambda i, j, k: (i, j)))
        args.append(add)
    return pl.pallas_call(
        body,
        name=name,
        grid=(m_dim // tm, n_dim // tn, nk),
        in_specs=in_specs,
        out_specs=pl.BlockSpec((tm, tn), lambda i, j, k: (i, j)),
        out_shape=jax.ShapeDtypeStruct((m_dim, n_dim), out_dtype),
        scratch_shapes=[pltpu.VMEM((tm, tn), F32)] if nk > 1 else [],
        compiler_params=pltpu.CompilerParams(dimension_semantics=("parallel", "parallel", "arbitrary")),
    )(*args)


def _rowwise(fn, rows, bcs, outs, reds=(), *, name, tr, ncol=1):
    n_rows = rows[0][0].shape[0]
    tr = _tile(n_rows, tr, 8)
    nrow = n_rows // tr
    n_in, n_out = len(rows) + len(bcs), len(outs)

    def body(*refs):
        j, i = pl.program_id(0), pl.program_id(1)
        o_vals, r_vals = fn(*[r[...] for r in refs[:n_in]])
        for ref, val in zip(refs[n_in:n_in + n_out], o_vals):
            ref[...] = val.astype(ref.dtype)
        for ref, val, (_, _, stride) in zip(refs[n_in + n_out:], r_vals, reds):
            first = (i == 0) if stride else jnp.logical_and(i == 0, j == 0)

            @pl.when(first)
            def _():
                ref[...] = val

            @pl.when(jnp.logical_not(first))
            def _():
                ref[...] += val

    def spec(r, w, off, st, row_dep=True):
        if row_dep:
            return pl.BlockSpec((r, w), lambda j, i: (i, off + st * j))
        return pl.BlockSpec((r, w), lambda j, i: (0, off + st * j))

    in_specs = [spec(tr, w, off, st) for (_, w, off, st) in rows]
    in_specs += [spec(a.shape[0], w, off, st, False) for (a, w, off, st) in bcs]
    out_specs = [spec(tr, w, off, st) for (_, _, w, off, st) in outs]
    out_specs += [spec(1, w, 0, st, False) for (_, w, st) in reds]
    out_shape = [jax.ShapeDtypeStruct((n_rows, c), dt) for (c, dt, _, _, _) in outs]
    out_shape += [jax.ShapeDtypeStruct((1, c), F32) for (c, _, _) in reds]
    return pl.pallas_call(
        body,
        name=name,
        grid=(ncol, nrow),
        in_specs=in_specs,
        out_specs=out_specs,
        out_shape=out_shape,
        compiler_params=pltpu.CompilerParams(dimension_semantics=("arbitrary", "arbitrary")),
    )(*[r[0] for r in rows], *[b[0] for b in bcs])


def _full(a):
    return (a, a.shape[1], 0, 0)


def _rms(x, g):
    return x * lax.rsqrt(jnp.mean(x * x, axis=-1, keepdims=True) + EPS) * g


def _silu(x):
    return x * jax.nn.sigmoid(x)


def _softplus(x):
    return jnp.maximum(x, 0.0) + jnp.log(1.0 + jnp.exp(-jnp.abs(x)))


def _rms_fwd(x, g, *, name):
    (h,) = _rowwise(lambda xb, gb: ([_rms(xb, gb)], []), [_full(x)], [_full(g)], [(D, BF16, D, 0, 0)], name=name, tr=512)
    return h


def _rms_bwd(x, g, dh, dres, *, name):
    def fn(xb, dhb, dresb, gb):
        _, vjp = jax.vjp(_rms, xb, gb)
        dx, dg = vjp(dhb)
        return [dx + dresb], [dg]

    return _rowwise(fn, [_full(x), _full(dh), _full(dres)], [_full(g)], [(D, F32, D, 0, 0)], [(D, D, 0)], name=name, tr=256)


def _gate_scalars(raw, al_row, dtb_row):
    lane = lax.broadcasted_iota(jnp.int32, raw.shape, 1)
    beta = jax.nn.sigmoid(raw)
    g = -jnp.exp(al_row) * _softplus(raw + dtb_row)
    return jnp.where(lane < B_HEADS, beta, jnp.where(lane < 2 * B_HEADS, g, 0.0))


def _gated_norm(o, z, w):
    return _rms(o, w) * _silu(z)


def _merge(ga, gb, ta, tb):
    return jax.nn.sigmoid(ga) * ta + jax.nn.sigmoid(gb) * tb


def _swiglu(gu):
    return _silu(gu[:, :D_FF]) * gu[:, D_FF:]


def _head_loss(x2, pg, pp, tgt, g):
    x3 = x2 + jax.nn.sigmoid(pg) * pp
    err = _rms(x3, g) - tgt
    return 0.5 * jnp.sum(jnp.mean(err * err, axis=-1))


CONV_W = 256


def _conv_taps(x, w):
    row = lax.broadcasted_iota(jnp.int32, x.shape, 0)
    shifted = [x] + [jnp.where(row >= s, pltpu.roll(x, s, 0), 0.0) for s in (1, 2, 3)]
    pre = shifted[0] * w[3:4]
    for s in (1, 2, 3):
        pre = pre + shifted[s] * w[3 - s:4 - s]
    return pre, shifted


def _conv_fwd(projp, conv_w, n_batch, seq):
    ncol = B_CONV // CONV_W
    first = P_CONV // CONV_W

    def body(x_ref, w_ref, o_ref):
        pre, _ = _conv_taps(x_ref[...], w_ref[...])
        o_ref[...] = _silu(pre)

    return pl.pallas_call(
        body,
        name="conv_fwd",
        grid=(ncol, n_batch),
        in_specs=[pl.BlockSpec((seq, CONV_W), lambda j, b: (b, first + j)), pl.BlockSpec((4, CONV_W), lambda j, b: (0, j))],
        out_specs=pl.BlockSpec((seq, CONV_W), lambda j, b: (b, j)),
        out_shape=jax.ShapeDtypeStruct((n_batch * seq, B_CONV), F32),
        compiler_params=pltpu.CompilerParams(dimension_semantics=("parallel", "parallel")),
    )(projp, conv_w)


def _conv_bwd(projp, conv_w, dc, n_batch, seq):
    width = dc.shape[1]
    ncol = width // CONV_W
    first_x = P_CONV // CONV_W

    def body(x_ref, w_ref, dc_ref, dx_ref, dw_ref):
        b = pl.program_id(1)
        w = w_ref[...]
        pre, shifted = _conv_taps(x_ref[...], w)
        sg = jax.nn.sigmoid(pre)
        dpre = dc_ref[...] * (sg * (1.0 + pre * (1.0 - sg)))
        row = lax.broadcasted_iota(jnp.int32, dpre.shape, 0)
        dx = dpre * w[3:4]
        for s in (1, 2, 3):
            dx = dx + jnp.where(row < seq - s, pltpu.roll(dpre, seq - s, 0), 0.0) * w[3 - s:4 - s]
        dx_ref[...] = dx.astype(dx_ref.dtype)
        for s in (0, 1, 2, 3):
            part = jnp.sum(dpre * shifted[s], axis=0, keepdims=True)

            @pl.when(b == 0)
            def _():
                dw_ref[3 - s:4 - s, :] = part

            @pl.when(b > 0)
            def _():
                dw_ref[3 - s:4 - s, :] += part

    return pl.pallas_call(
        body,
        name="conv_bwd",
        grid=(ncol, n_batch),
        in_specs=[
            pl.BlockSpec((seq, CONV_W), lambda j, b: (b, first_x + j)),
            pl.BlockSpec((4, CONV_W), lambda j, b: (0, j)),
            pl.BlockSpec((seq, CONV_W), lambda j, b: (b, j)),
        ],
        out_specs=[pl.BlockSpec((seq, CONV_W), lambda j, b: (b, j)), pl.BlockSpec((4, CONV_W), lambda j, b: (0, j))],
        out_shape=[jax.ShapeDtypeStruct((n_batch * seq, width), BF16), jax.ShapeDtypeStruct((4, width), F32)],
        compiler_params=pltpu.CompilerParams(dimension_semantics=("arbitrary", "arbitrary")),
    )(projp, conv_w, dc)


def _attn_chunk(qc, kb, vb, bias2, valid, lane_lo):
    out = None
    for e in (0, 1):
        sel = lane_lo if e == 0 else jnp.logical_not(lane_lo)
        qm = jnp.where(sel, qc, 0.0) * (A_DIM ** -0.5)
        s = lax.dot_general(qm.astype(BF16), kb.astype(BF16), NT, preferred_element_type=F32) + bias2[e]
        s = jnp.where(valid, s, -1e30)
        p = jnp.exp(s - lax.stop_gradient(jnp.max(s, axis=-1, keepdims=True)))
        p = p / jnp.sum(p, axis=-1, keepdims=True)
        oe = jnp.where(sel, jnp.dot(p.astype(BF16), vb.astype(BF16), preferred_element_type=F32), 0.0)
        out = oe if out is None else out + oe
    return out


def _attn_masks(c):
    col = lax.broadcasted_iota(jnp.int32, (CHUNK, A_BAND), 1)
    valid = col + c * CHUNK >= A_PAD
    lane_lo = lax.broadcasted_iota(jnp.int32, (1, LANE), 1) < A_DIM
    return valid, lane_lo


def _attn_specs(seq):
    def blk(first):
        return pl.BlockSpec((seq, LANE), lambda hp, b: (b, first + hp))

    return blk, pl.BlockSpec((2, CHUNK, A_BAND), lambda hp, b: (hp, 0, 0))


def _attn_fwd(projp, bias, n_batch, seq):
    nc = seq // CHUNK
    blk, bias_spec = _attn_specs(seq)

    def body(q_ref, k_ref, v_ref, b_ref, o_ref, kp_ref, vp_ref):
        kp_ref[0:A_PAD, :] = jnp.zeros((A_PAD, LANE), F32)
        vp_ref[0:A_PAD, :] = jnp.zeros((A_PAD, LANE), F32)
        kp_ref[A_PAD:, :] = k_ref[...]
        vp_ref[A_PAD:, :] = v_ref[...]
        bias2 = b_ref[...]

        def step(c, carry):
            r0 = pl.multiple_of(c * CHUNK, CHUNK)
            valid, lane_lo = _attn_masks(c)
            out = _attn_chunk(q_ref[pl.ds(r0, CHUNK), :], kp_ref[pl.ds(r0, A_BAND), :], vp_ref[pl.ds(r0, A_BAND), :],
                              bias2, valid, lane_lo)
            o_ref[pl.ds(r0, CHUNK), :] = out.astype(o_ref.dtype)
            return carry

        lax.fori_loop(0, nc, step, 0)

    return pl.pallas_call(
        body,
        name="attn_fwd",
        grid=(A_HEADS // 2, n_batch),
        in_specs=[blk(0), blk(4), blk(8), bias_spec],
        out_specs=pl.BlockSpec((seq, LANE), lambda hp, b: (b, hp)),
        out_shape=jax.ShapeDtypeStruct((n_batch * seq, A_WIDTH), BF16),
        scratch_shapes=[pltpu.VMEM((A_PAD + seq, LANE), F32), pltpu.VMEM((A_PAD + seq, LANE), F32)],
        compiler_params=pltpu.CompilerParams(dimension_semantics=("parallel", "parallel")),
    )(projp, projp, projp, bias)


def _attn_bwd(projp, bias, dy, n_batch, seq):
    nc = seq // CHUNK
    blk, bias_spec = _attn_specs(seq)
    out_blk = pl.BlockSpec((seq, LANE), lambda hp, b: (b, hp))

    def body(q_ref, k_ref, v_ref, b_ref, dy_ref, dq_ref, dk_ref, dv_ref, db_ref, kp_ref, vp_ref, dkp_ref, dvp_ref):
        b = pl.program_id(1)
        kp_ref[0:A_PAD, :] = jnp.zeros((A_PAD, LANE), F32)
        vp_ref[0:A_PAD, :] = jnp.zeros((A_PAD, LANE), F32)
        kp_ref[A_PAD:, :] = k_ref[...]
        vp_ref[A_PAD:, :] = v_ref[...]
        dkp_ref[...] = jnp.zeros_like(dkp_ref)
        dvp_ref[...] = jnp.zeros_like(dvp_ref)
        bias2 = b_ref[...]

        @pl.when(b == 0)
        def _():
            db_ref[...] = jnp.zeros_like(db_ref)

        def step(c, carry):
            r0 = pl.multiple_of(c * CHUNK, CHUNK)
            valid, lane_lo = _attn_masks(c)
            _, vjp = jax.vjp(lambda q, k, v, bb: _attn_chunk(q, k, v, bb, valid, lane_lo), q_ref[pl.ds(r0, CHUNK), :],
                             kp_ref[pl.ds(r0, A_BAND), :], vp_ref[pl.ds(r0, A_BAND), :], bias2)
            dq, dk, dv, dbias = vjp(dy_ref[pl.ds(r0, CHUNK), :])
            dq_ref[pl.ds(r0, CHUNK), :] = dq.astype(dq_ref.dtype)
            dkp_ref[pl.ds(r0, A_BAND), :] += dk
            dvp_ref[pl.ds(r0, A_BAND), :] += dv
            db_ref[...] += dbias
            return carry

        lax.fori_loop(0, nc, step, 0)
        dk_ref[...] = dkp_ref[A_PAD:, :].astype(dk_ref.dtype)
        dv_ref[...] = dvp_ref[A_PAD:, :].astype(dv_ref.dtype)

    n_tok = n_batch * seq
    pad = pltpu.VMEM((A_PAD + seq, LANE), F32)
    return pl.pallas_call(
        body,
        name="attn_bwd",
        grid=(A_HEADS // 2, n_batch),
        in_specs=[blk(0), blk(4), blk(8), bias_spec, out_blk],
        out_specs=[out_blk, out_blk, out_blk, bias_spec],
        out_shape=[jax.ShapeDtypeStruct((n_tok, A_WIDTH), BF16)] * 3 + [jax.ShapeDtypeStruct((A_HEADS, CHUNK, A_BAND), F32)],
        scratch_shapes=[pad, pad, pad, pad],
        compiler_params=pltpu.CompilerParams(dimension_semantics=("arbitrary", "arbitrary")),
    )(projp, projp, projp, bias, dy)


def _rel_bias_table(rel_bias):
    far = rel_bias[:, 2 * REL_CLIP:]
    rev = rel_bias[:, ::-1]
    rows = []
    for i in range(CHUNK):
        n_far = A_BAND - 3 * CHUNK + i
        rows.append(jnp.concatenate([jnp.broadcast_to(far, (A_HEADS, n_far)), rev[:, :A_BAND - n_far]], axis=1))
    return jnp.stack(rows, axis=1)


def _delta_chunk(r_state, cq, ck, cv, beta, g):
    ii = lax.broadcasted_iota(jnp.int32, (CHUNK, CHUNK), 0)
    jj = lax.broadcasted_iota(jnp.int32, (CHUNK, CHUNK), 1)
    incl, strict, eye = ii >= jj, ii > jj, ii == jj
    q = cq * lax.rsqrt(jnp.sum(cq * cq, axis=-1, keepdims=True) + EPS) * (B_DIM ** -0.5)
    k = ck * lax.rsqrt(jnp.sum(ck * ck, axis=-1, keepdims=True) + EPS)
    g_b = jnp.broadcast_to(g, (CHUNK, CHUNK))
    g_row = jnp.sum(jnp.where(eye, g_b, 0.0), axis=0, keepdims=True)
    gc_col = jnp.sum(jnp.where(incl, g_row, 0.0), axis=1, keepdims=True)
    gc_row = jnp.sum(jnp.where(ii <= jj, g_b, 0.0), axis=0, keepdims=True)
    decay = jnp.where(incl, jnp.exp(jnp.where(incl, gc_col - gc_row, 0.0)), 0.0)
    kk = lax.dot_general(k, k, NT, precision=DELTA_PREC, preferred_element_type=F32)
    x = jnp.where(strict, -(beta * kk * decay), 0.0)
    inv = jnp.where(eye, 1.0, 0.0) + x
    pw = x
    for _ in range(5):
        pw = jnp.dot(pw, pw, precision=DELTA_PREC, preferred_element_type=F32)
        inv = inv + jnp.dot(inv, pw, precision=DELTA_PREC, preferred_element_type=F32)
    egc = jnp.exp(gc_col)
    u = jnp.dot(inv, beta * cv, precision=DELTA_PREC, preferred_element_type=F32)
    wk = jnp.dot(inv, (beta * egc) * k, precision=DELTA_PREC, preferred_element_type=F32)
    pqk = lax.dot_general(q, k, NT, precision=DELTA_PREC, preferred_element_type=F32) * decay
    g_last = gc_col[CHUNK - 1:CHUNK, :]
    kdec = k * jnp.exp(g_last - gc_col)
    w = u - jnp.dot(wk, r_state, precision=DELTA_PREC, preferred_element_type=F32)
    o = egc * jnp.dot(q, r_state, precision=DELTA_PREC, preferred_element_type=F32)
    o = o + jnp.dot(pqk, w, precision=DELTA_PREC, preferred_element_type=F32)
    r_new = jnp.exp(g_last) * r_state + lax.dot_general(kdec, w, TN, precision=DELTA_PREC, preferred_element_type=F32)
    return o, r_new


DELTA_BLK = 512


def _delta_blocks(n_batch, seq):
    nblk = seq // DELTA_BLK
    cpb = DELTA_BLK // CHUNK

    def rows(width, order):
        return pl.BlockSpec((DELTA_BLK, width), lambda b, i: (b * nblk + order(i), 0))

    def states(order):
        return pl.BlockSpec((cpb, B_HEADS, B_DIM, B_DIM), lambda b, i: (b * nblk + order(i), 0, 0, 0))

    return nblk, cpb, rows, states


def _head_cols(h):
    return [pl.ds(part * B_HEADS * B_DIM + h * B_DIM, B_DIM) for part in range(3)]


def _delta_fwd(conv, bg, n_batch, seq):
    nblk, cpb, rows_spec, states_spec = _delta_blocks(n_batch, seq)

    def forward(i):
        return i

    def body(c_ref, bg_ref, o_ref, st_ref, r_ref):
        @pl.when(pl.program_id(1) == 0)
        def _():
            r_ref[...] = jnp.zeros_like(r_ref)

        def step(c, carry):
            rows = pl.ds(pl.multiple_of(c * CHUNK, CHUNK), CHUNK)
            bg_c = bg_ref[rows, :]
            for h in range(B_HEADS):
                cq, ck, cv = _head_cols(h)
                r_state = r_ref[h]
                st_ref[c, h] = r_state
                o, r_new = _delta_chunk(r_state, c_ref[rows, cq], c_ref[rows, ck], c_ref[rows, cv], bg_c[:, h:h + 1],
                                        bg_c[:, B_HEADS + h:B_HEADS + h + 1])
                o_ref[rows, pl.ds(h * B_DIM, B_DIM)] = o
                r_ref[h] = r_new
            return carry

        lax.fori_loop(0, cpb, step, 0)

    n_tok = n_batch * seq
    return pl.pallas_call(
        body,
        name="delta_fwd",
        grid=(n_batch, nblk),
        in_specs=[rows_spec(B_CONV, forward), rows_spec(LANE, forward)],
        out_specs=[rows_spec(B_HEADS * B_DIM, forward), states_spec(forward)],
        out_shape=[jax.ShapeDtypeStruct((n_tok, B_HEADS * B_DIM), F32),
                   jax.ShapeDtypeStruct((n_tok // CHUNK, B_HEADS, B_DIM, B_DIM), F32)],
        scratch_shapes=[pltpu.VMEM((B_HEADS, B_DIM, B_DIM), F32)],
        compiler_params=pltpu.CompilerParams(dimension_semantics=("arbitrary", "arbitrary")),
    )(conv, bg)


def _delta_bwd(conv, bg, states, do, n_batch, seq):
    nblk, cpb, rows_spec, states_spec = _delta_blocks(n_batch, seq)

    def backward(i):
        return nblk - 1 - i

    def body(c_ref, bg_ref, st_ref, do_ref, dc_ref, dbg_ref, dr_ref):
        @pl.when(pl.program_id(1) == 0)
        def _():
            dr_ref[...] = jnp.zeros_like(dr_ref)

        def step(n, carry):
            c = cpb - 1 - n
            rows = pl.ds(pl.multiple_of(c * CHUNK, CHUNK), CHUNK)
            bg_c = bg_ref[rows, :]
            lane = lax.broadcasted_iota(jnp.int32, bg_c.shape, 1)
            dbg = jnp.zeros_like(bg_c)
            for h in range(B_HEADS):
                cq, ck, cv = _head_cols(h)
                _, vjp = jax.vjp(_delta_chunk, st_ref[c, h], c_ref[rows, cq], c_ref[rows, ck], c_ref[rows, cv],
                                 bg_c[:, h:h + 1], bg_c[:, B_HEADS + h:B_HEADS + h + 1])
                dr, dq, dk, dv, dbeta, dg = vjp((do_ref[rows, pl.ds(h * B_DIM, B_DIM)], dr_ref[h]))
                dr_ref[h] = dr
                dc_ref[rows, cq] = dq
                dc_ref[rows, ck] = dk
                dc_ref[rows, cv] = dv
                dbg = dbg + jnp.where(lane == h, dbeta, 0.0) + jnp.where(lane == h + B_HEADS, dg, 0.0)
            dbg_ref[rows, :] = dbg
            return carry

        lax.fori_loop(0, cpb, step, 0)

    n_tok = n_batch * seq
    return pl.pallas_call(
        body,
        name="delta_bwd",
        grid=(n_batch, nblk),
        in_specs=[rows_spec(B_CONV, backward), rows_spec(LANE, backward), states_spec(backward),
                  rows_spec(B_HEADS * B_DIM, backward)],
        out_specs=[rows_spec(B_CONV, backward), rows_spec(LANE, backward)],
        out_shape=[jax.ShapeDtypeStruct((n_tok, B_CONV), F32), jax.ShapeDtypeStruct((n_tok, LANE), F32)],
        scratch_shapes=[pltpu.VMEM((B_HEADS, B_DIM, B_DIM), F32)],
        compiler_params=pltpu.CompilerParams(dimension_semantics=("arbitrary", "arbitrary")),
    )(conv, bg, states, do)


def _lane_row(vec4, first):
    return jnp.concatenate([jnp.zeros((1, first), F32), vec4.reshape(1, B_HEADS).astype(F32),
                            jnp.zeros((1, LANE - first - B_HEADS), F32)], axis=1)


def _local_step(x3d, p3d, tgt3d, w, small):
    n_batch, seq, _ = x3d.shape
    n_tok = n_batch * seq
    x, p, tgt = x3d.reshape(n_tok, D), p3d.reshape(n_tok, -1), tgt3d.reshape(n_tok, D)
    g_mix, g_ffn, g_ple, g_final = (small[k].reshape(1, D) for k in ("g_mix", "g_ffn", "g_ple", "g_final"))
    w_onorm = small["w_onorm"].reshape(1, B_DIM)
    al_row = _lane_row(small["a_log"], B_HEADS)
    dtb_row = _lane_row(small["dt_bias"], B_HEADS)
    rel_bias = small["rel_bias"].reshape(A_HEADS, -1)
    bias = _rel_bias_table(rel_bias)
    conv_w = small["conv_w"].reshape(4, B_CONV)
    bd_blk = P_BD // LANE

    h1 = _rms_fwd(x, g_mix, name="rms_mix")
    projp = _mm(h1, w["w_in"], name="mm_proj", tn=640)
    y_a = _attn_fwd(projp, bias, n_batch, seq)
    conv = _conv_fwd(projp, conv_w, n_batch, seq)
    (bg,) = _rowwise(lambda raw, al, dtb: ([_gate_scalars(raw, al, dtb)], []), [(projp, LANE, bd_blk, 0)],
                     [_full(al_row), _full(dtb_row)], [(LANE, F32, LANE, 0, 0)], name="gate_scalars", tr=1024)
    o_b, states = _delta_fwd(conv, bg, n_batch, seq)
    (y_b,) = _rowwise(lambda o, z, wn: ([_gated_norm(o, z, wn)], []), [(o_b, LANE, 0, 1), (projp, LANE, P_Z // LANE, 1)],
                      [_full(w_onorm)], [(B_HEADS * B_DIM, BF16, LANE, 0, 1)], name="gated_norm", tr=1024, ncol=B_HEADS)
    t_a = _mm(y_a, w["w_branch_a"], name="mm_branch_a", tn=1024)
    t_b = _mm(y_b, w["w_branch_b"], name="mm_branch_b", tn=1024)
    half = D // 2
    gate_rows = [(projp, half, P_GATE // half, 1), (projp, half, P_GATE // half + 2, 1), (t_a, half, 0, 1), (t_b, half, 0, 1)]
    (merged,) = _rowwise(lambda ga, gb, ta, tb: ([_merge(ga, gb, ta, tb)], []), gate_rows, [], [(D, BF16, half, 0, 1)],
                         name="merge", tr=512, ncol=2)
    x1 = _mm(merged, w["w_out"], add=x, name="mm_out", tn=1024)
    h2 = _rms_fwd(x1, g_ffn, name="rms_ffn")
    gu = _mm(h2, w["w_gate_up"], name="mm_gate_up", tn=512)
    (act,) = _rowwise(lambda gub: ([_swiglu(gub)], []), [_full(gu)], [], [(D_FF, BF16, D_FF, 0, 0)], name="swiglu", tr=256)
    x2 = _mm(act, w["w_down"], add=x1, name="mm_down", tn=1024, tk=1408)
    h3 = _rms_fwd(x2, g_ple, name="rms_ple")
    pg = _mm(h3, w["w_ple_gate"], name="mm_ple_gate", tn=1024)
    pp = _mm(p, w["w_ple_proj"], name="mm_ple_proj", tn=1024)

    def head_fn(x2b, pgb, ppb, tb, gb):
        loss, (dx2, dpg, dpp, dg) = jax.value_and_grad(_head_loss, argnums=(0, 1, 2, 4))(x2b, pgb, ppb, tb, gb)
        return [dx2, dpg, dpp], [dg, jnp.full((1, LANE), loss, F32)]

    dx3, dpg, dpp, dg_final, loss_row = _rowwise(
        head_fn, [_full(x2), _full(pg), _full(pp), _full(tgt)], [_full(g_final)],
        [(D, F32, D, 0, 0), (D, BF16, D, 0, 0), (D, BF16, D, 0, 0)], [(D, D, 0), (LANE, LANE, 0)], name="loss_head", tr=256)
    gw = {}
    gw["w_ple_proj"] = _mm(p, dpp, ta=True, name="mm_d_ple_proj", tm=256, tn=1024)
    gw["w_ple_gate"] = _mm(h3, dpg, ta=True, name="mm_d_ple_gate", tn=1024)
    dh3 = _mm(dpg, w["w_ple_gate"].T, name="mm_dh3", tn=1024)
    dx2, dg_ple = _rms_bwd(x2, g_ple, dh3, dx3, name="rms_ple_bwd")
    gw["w_down"] = _mm(act, dx2, ta=True, name="mm_d_down", tm=256, tn=1024)
    dact = _mm(dx2, w["w_down"].T, name="mm_dact", tn=1408)

    def swiglu_bwd(gub, dab):
        _, vjp = jax.vjp(_swiglu, gub)
        return [vjp(dab)[0]], []

    (dgu,) = _rowwise(swiglu_bwd, [_full(gu), _full(dact)], [], [(2 * D_FF, BF16, 2 * D_FF, 0, 0)], name="swiglu_bwd", tr=256)
    gw["w_gate_up"] = _mm(h2, dgu, ta=True, name="mm_d_gate_up", tn=512)
    dh2 = _mm(dgu, w["w_gate_up"].T, name="mm_dh2", tn=1024, tk=1408)
    dx1, dg_ffn = _rms_bwd(x1, g_ffn, dh2, dx2, name="rms_ffn_bwd")
    gw["w_out"] = _mm(merged, dx1, ta=True, name="mm_d_out", tn=1024)
    dmerged = _mm(dx1, w["w_out"].T, name="mm_dmerged", tn=1024)

    def merge_bwd(ga, gb, ta, tb, dm):
        _, vjp = jax.vjp(_merge, ga, gb, ta, tb)
        return list(vjp(dm)), []

    dga, dgb, dta, dtb = _rowwise(merge_bwd, gate_rows + [(dmerged, half, 0, 1)], [], [(D, BF16, half, 0, 1)] * 4,
                                  name="merge_bwd", tr=512, ncol=2)
    gw["w_branch_a"] = _mm(y_a, dta, ta=True, name="mm_d_branch_a", tn=1024)
    gw["w_branch_b"] = _mm(y_b, dtb, ta=True, name="mm_d_branch_b", tn=1024)
    dya = _mm(dta, w["w_branch_a"].T, name="mm_dya", tn=512)
    dyb = _mm(dtb, w["w_branch_b"].T, name="mm_dyb", tn=512)

    def gated_norm_bwd(o, z, dy, wn):
        _, vjp = jax.vjp(_gated_norm, o, z, wn)
        do, dz, dwn = vjp(dy)
        return [do, dz], [dwn]

    do_b, dz, dw_onorm = _rowwise(
        gated_norm_bwd, [(o_b, LANE, 0, 1), (projp, LANE, P_Z // LANE, 1), (dyb, LANE, 0, 1)], [_full(w_onorm)],
        [(B_HEADS * B_DIM, F32, LANE, 0, 1), (B_HEADS * B_DIM, BF16, LANE, 0, 1)], [(B_DIM, B_DIM, 0)],
        name="gated_norm_bwd", tr=1024, ncol=B_HEADS)
    dconv_out, dbg = _delta_bwd(conv, bg, states, do_b, n_batch, seq)

    def gate_scalars_bwd(raw, dbgb, al, dtb):
        _, vjp = jax.vjp(_gate_scalars, raw, al, dtb)
        draw, dal, ddtb = vjp(dbgb)
        return [draw], [dal, ddtb]

    dbd, dal_row, ddtb_row = _rowwise(gate_scalars_bwd, [(projp, LANE, bd_blk, 0), _full(dbg)], [_full(al_row), _full(dtb_row)],
                                      [(LANE, BF16, LANE, 0, 0)], [(LANE, LANE, 0), (LANE, LANE, 0)], name="gate_scalars_bwd",
                                      tr=1024)
    dconv, dconv_w = _conv_bwd(projp, conv_w, dconv_out, n_batch, seq)
    dq_a, dk_a, dv_a, dbias = _attn_bwd(projp, bias, dya, n_batch, seq)
    dprojp = jnp.concatenate([dq_a, dk_a, dv_a, dconv, dz, dga, dgb, dbd], axis=1)
    gw["w_in"] = _mm(h1, dprojp, ta=True, name="mm_d_in", tn=640)
    dh1 = _mm(dprojp, w["w_in"].T, name="mm_dh1", tn=1024, tk=1152)
    grad_x, dg_mix = _rms_bwd(x, g_mix, dh1, dx1, name="rms_mix_bwd")

    _, bias_vjp = jax.vjp(_rel_bias_table, rel_bias)
    gs = {
        "g_mix": dg_mix, "g_ffn": dg_ffn, "g_ple": dg_ple, "g_final": dg_final, "w_onorm": dw_onorm,
        "conv_w": dconv_w, "rel_bias": bias_vjp(dbias)[0],
        "a_log": dal_row[0, B_HEADS:2 * B_HEADS], "dt_bias": ddtb_row[0, B_HEADS:2 * B_HEADS],
    }
    return loss_row[:, :1], grad_x.reshape(n_batch, seq, D), gw, gs


MATRICES = (
    ("w_in", (D, D_IN), 1), ("w_gate_up", (D, 2 * D_FF), 1), ("w_branch_a", (A_WIDTH, D), 1), ("w_branch_b", (A_WIDTH, D), 1),
    ("w_out", (D, D), 0), ("w_down", (D_FF, D), 0), ("w_ple_gate", (D, D), 0), ("w_ple_proj", (256, D), 1),
)
MATRIX_ROWS = sum(s[0] * s[1] for _, s, _ in MATRICES) // (N_DEV * D)
TAP_ROWS = 2
PACK_ROWS = 2208
TAPS_PER_SHARD = B_CONV // N_DEV


def _shard_shape(shape, axis):
    return tuple(n // N_DEV if a == axis else n for a, n in enumerate(shape))


def _pack_shards(shards, taps):
    parts = [shards[name].reshape(-1, D).astype(BF16) for name, _, _ in MATRICES]
    tap_bits = lax.bitcast_convert_type(taps, BF16).reshape(1, -1)
    parts.append(jnp.concatenate([tap_bits, jnp.zeros((1, TAP_ROWS * D - tap_bits.shape[1]), BF16)], axis=1).reshape(TAP_ROWS, D))
    parts.append(jnp.zeros((PACK_ROWS - MATRIX_ROWS - TAP_ROWS, D), BF16))
    return jnp.concatenate(parts, axis=0)


def _unpack_gathered(slabs):
    out, r0 = {}, 0
    for name, shape, axis in MATRICES:
        shard = _shard_shape(shape, axis)
        rows = shard[0] * shard[1] // D
        part = slabs[:, r0:r0 + rows, :].reshape((N_DEV,) + shard)
        out[name] = part.reshape(shape) if axis == 0 else jnp.transpose(part, (1, 0, 2)).reshape(shape)
        r0 += rows
    tap_bits = slabs[:, r0:r0 + TAP_ROWS, :].reshape(N_DEV, TAP_ROWS * D)[:, :4 * TAPS_PER_SHARD * 2]
    taps = lax.bitcast_convert_type(tap_bits.reshape(N_DEV, 4, TAPS_PER_SHARD, 2), F32)
    return out, jnp.transpose(taps, (1, 0, 2)).reshape(4, B_CONV)


def _pack_by_owner(full):
    parts = []
    for name, shape, axis in MATRICES:
        shard = _shard_shape(shape, axis)
        m = full[name].astype(BF16)
        if axis == 0:
            m = m.reshape((N_DEV,) + shard)
        else:
            m = jnp.transpose(m.reshape(shape[0], N_DEV, shard[1]), (1, 0, 2))
        parts.append(m.reshape(N_DEV, -1, D))
    parts.append(jnp.zeros((N_DEV, PACK_ROWS - MATRIX_ROWS, D), BF16))
    return jnp.concatenate(parts, axis=1)


def _unpack_shard(slab):
    out, r0 = {}, 0
    for name, shape, axis in MATRICES:
        shard = _shard_shape(shape, axis)
        rows = shard[0] * shard[1] // D
        out[name] = slab[r0:r0 + rows, :].reshape(shard)
        r0 += rows
    return out


def _permute_w_in(w_in):
    zeros = jnp.zeros((D, P_END - D_IN), w_in.dtype)
    return jnp.concatenate([w_in[:, :P_GATE], w_in[:, P_GATE + 8:], w_in[:, P_GATE:P_GATE + 8], zeros], axis=1)


def _unpermute_w_in(gp):
    return jnp.concatenate([gp[:, :P_GATE], gp[:, P_BD:P_BD + 8], gp[:, P_GATE:P_BD]], axis=1)


SMALL_ROWS = 16
SMALL_LAYOUT = (("g_mix", 0, D), ("g_ffn", 1, D), ("g_ple", 2, D), ("g_final", 3, D), ("conv_w", 4, 4 * B_CONV),
                ("rel_bias", 10, A_HEADS * (2 * REL_CLIP + 1)), ("w_onorm", 13, B_DIM), ("a_log", 14, B_HEADS),
                ("dt_bias", 14, B_HEADS), ("loss", 15, 1))


def _pack_small(gs):
    rows = {}
    for name, row, n in SMALL_LAYOUT:
        rows.setdefault(row, []).append(gs[name].reshape(-1).astype(F32))
    parts = []
    for row in sorted(rows):
        flat = jnp.concatenate(rows[row])
        parts.append(jnp.concatenate([flat, jnp.zeros((-flat.shape[0] % D,), F32)]))
    flat = jnp.concatenate(parts)
    assert flat.shape[0] == SMALL_ROWS * D, flat.shape
    return flat.reshape(SMALL_ROWS, D)


def _unpack_small(blk):
    flat, out, used = blk.reshape(-1), {}, {}
    for name, row, n in SMALL_LAYOUT:
        start = row * D + used.get(row, 0)
        out[name] = flat[start:start + n]
        used[row] = used.get(row, 0) + n
    return out


def _position():
    return lax.axis_index("x"), lax.axis_index("y"), lax.axis_index("c")


def _weights_allgather(pack):
    def body(x_ref, out_ref, send_sems, recv_sems, local_sem):
        x, y, c = _position()
        me, sibling = (x, y, c), (x, y, 1 - c)
        chips = [(1 - x, y), (x, 1 - y), (1 - x, 1 - y)]

        def slab(px, py, pc):
            return out_ref.at[4 * px + 2 * py + pc]

        def copy(k, block, to, src=None):
            return pltpu.make_async_remote_copy(src_ref=slab(*block) if src is None else src, dst_ref=slab(*block),
                                                send_sem=send_sems.at[k], recv_sem=recv_sems.at[k], device_id=to,
                                                device_id_type=MESH)

        mine = pltpu.make_async_copy(x_ref, slab(*me), local_sem)
        mine.start()
        first = [copy(0, me, sibling, src=x_ref)]
        first += [copy(1 + j, me, (*chip, c), src=x_ref) for j, chip in enumerate(chips)]
        for cp in first:
            cp.start()
        passed = [copy(4 + j, (*chip, c), sibling) for j, chip in enumerate(chips)]
        for j, chip in enumerate(chips):
            copy(1 + j, (*chip, c), me).wait_recv()
            passed[j].start()
        copy(0, sibling, me).wait_recv()
        for j, chip in enumerate(chips):
            copy(4 + j, (*chip, 1 - c), me).wait_recv()
        for cp in first + passed:
            cp.wait_send()
        mine.wait()

    return pl.pallas_call(
        body,
        name="weights_allgather",
        out_shape=jax.ShapeDtypeStruct((N_DEV,) + pack.shape, pack.dtype),
        in_specs=[HBM_SPEC],
        out_specs=HBM_SPEC,
        scratch_shapes=[pltpu.SemaphoreType.DMA((7,)), pltpu.SemaphoreType.DMA((7,)), pltpu.SemaphoreType.DMA],
    )(pack)


def _grads_exchange(by_owner, small):
    def body(g_ref, s_ref, got_ref, got_s_ref, send_sems, recv_sems, local_sems):
        x, y, c = _position()
        mine = 4 * x + 2 * y + c
        own = pltpu.make_async_copy(g_ref.at[mine], got_ref.at[mine], local_sems.at[0])
        own_s = pltpu.make_async_copy(s_ref, got_s_ref.at[mine], local_sems.at[1])
        own.start()
        own_s.start()
        flips = [(dx, dy, dc) for dx in (0, 1) for dy in (0, 1) for dc in (0, 1) if dx + dy + dc]
        copies = []
        for k, (dx, dy, dc) in enumerate(flips):
            px, py, pc = (1 - x if dx else x), (1 - y if dy else y), (1 - c if dc else c)
            peer = 4 * px + 2 * py + pc

            def remote(src, dst, sem):
                return pltpu.make_async_remote_copy(src_ref=src, dst_ref=dst, send_sem=send_sems.at[sem],
                                                    recv_sem=recv_sems.at[sem], device_id=(px, py, pc), device_id_type=MESH)

            big, sml = remote(g_ref.at[peer], got_ref.at[mine], k), remote(s_ref, got_s_ref.at[mine], 7 + k)
            big.start()
            sml.start()
            copies += [(big, remote(g_ref.at[peer], got_ref.at[peer], k)), (sml, remote(s_ref, got_s_ref.at[peer], 7 + k))]
        for sent, landed in copies:
            landed.wait_recv()
            sent.wait_send()
        own.wait()
        own_s.wait()

    return pl.pallas_call(
        body,
        name="grads_exchange",
        out_shape=[jax.ShapeDtypeStruct(by_owner.shape, by_owner.dtype), jax.ShapeDtypeStruct((N_DEV,) + small.shape, small.dtype)],
        in_specs=[HBM_SPEC, HBM_SPEC],
        out_specs=[HBM_SPEC, HBM_SPEC],
        scratch_shapes=[pltpu.SemaphoreType.DMA((14,)), pltpu.SemaphoreType.DMA((14,)), pltpu.SemaphoreType.DMA((2,))],
    )(by_owner, small)


def _sum_slots(got, *, name, tr):
    _, rows, cols = got.shape
    tr = _tile(rows, tr, 16)

    def body(g_ref, o_ref):
        acc = g_ref[0].astype(F32)
        for j in range(1, N_DEV):
            acc = acc + g_ref[j].astype(F32)
        o_ref[...] = acc

    return pl.pallas_call(
        body,
        name=name,
        grid=(rows // tr,),
        in_specs=[pl.BlockSpec((N_DEV, tr, cols), lambda i: (0, i, 0))],
        out_specs=pl.BlockSpec((tr, cols), lambda i: (i, 0)),
        out_shape=jax.ShapeDtypeStruct((rows, cols), F32),
        compiler_params=pltpu.CompilerParams(dimension_semantics=("parallel",)),
    )(got)


def _adamw(wt, g, m, v, *, name):
    shape = wt.shape
    two_d = (-1, shape[-1]) if wt.ndim > 1 else (1, -1)
    args = [a.reshape(two_d) for a in (wt, g, m, v)]
    rows, cols = args[0].shape
    tr = _tile(rows, 256, 8) if rows % 8 == 0 else rows

    def body(w_ref, g_ref, m_ref, v_ref, d_ref, nm_ref, nv_ref):
        gv = g_ref[...]
        m2 = ADAM_B1 * m_ref[...] + (1.0 - ADAM_B1) * gv
        v2 = ADAM_B2 * v_ref[...] + (1.0 - ADAM_B2) * (gv * gv)
        m_hat = m2 / (1.0 - ADAM_B1 ** ADAM_STEP)
        v_hat = v2 / (1.0 - ADAM_B2 ** ADAM_STEP)
        d_ref[...] = -ADAM_LR * (m_hat / (jnp.sqrt(v_hat) + ADAM_EPS) + ADAM_WD * w_ref[...])
        nm_ref[...] = m2
        nv_ref[...] = v2

    spec = pl.BlockSpec((tr, cols), lambda i: (i, 0))
    outs = pl.pallas_call(
        body,
        name=name,
        grid=(rows // tr,),
        in_specs=[spec] * 4,
        out_specs=[spec] * 3,
        out_shape=[jax.ShapeDtypeStruct((rows, cols), F32)] * 3,
        compiler_params=pltpu.CompilerParams(dimension_semantics=("parallel",)),
    )(*args)
    return tuple(o.reshape(shape) for o in outs)


WEIGHTS = ("g_mix", "w_in", "conv_w", "a_log", "dt_bias", "rel_bias", "w_onorm", "w_branch_a", "w_branch_b", "w_out", "g_ffn",
           "w_gate_up", "w_down", "g_ple", "w_ple_gate", "w_ple_proj", "g_final")


def kernel(x, p, g_mix, w_in, conv_w, a_log, dt_bias, rel_bias, w_onorm, w_branch_a, w_branch_b, w_out, g_ffn, w_gate_up, w_down, g_ple, w_ple_gate, w_ple_proj, g_final, loss_target, m_g_mix, m_w_in, m_conv_w, m_a_log, m_dt_bias, m_rel_bias, m_w_onorm, m_w_branch_a, m_w_branch_b, m_w_out, m_g_ffn, m_w_gate_up, m_w_down, m_g_ple, m_w_ple_gate, m_w_ple_proj, m_g_final, v_g_mix, v_w_in, v_conv_w, v_a_log, v_dt_bias, v_rel_bias, v_w_onorm, v_w_branch_a, v_w_branch_b, v_w_out, v_g_ffn, v_w_gate_up, v_w_down, v_g_ple, v_w_ple_gate, v_w_ple_proj, v_g_final):
    given = dict(g_mix=g_mix, w_in=w_in, conv_w=conv_w, a_log=a_log, dt_bias=dt_bias, rel_bias=rel_bias, w_onorm=w_onorm,
                 w_branch_a=w_branch_a, w_branch_b=w_branch_b, w_out=w_out, g_ffn=g_ffn, w_gate_up=w_gate_up, w_down=w_down,
                 g_ple=g_ple, w_ple_gate=w_ple_gate, w_ple_proj=w_ple_proj, g_final=g_final)
    mom1 = dict(g_mix=m_g_mix, w_in=m_w_in, conv_w=m_conv_w, a_log=m_a_log, dt_bias=m_dt_bias, rel_bias=m_rel_bias,
                w_onorm=m_w_onorm, w_branch_a=m_w_branch_a, w_branch_b=m_w_branch_b, w_out=m_w_out, g_ffn=m_g_ffn,
                w_gate_up=m_w_gate_up, w_down=m_w_down, g_ple=m_g_ple, w_ple_gate=m_w_ple_gate, w_ple_proj=m_w_ple_proj,
                g_final=m_g_final)
    mom2 = dict(g_mix=v_g_mix, w_in=v_w_in, conv_w=v_conv_w, a_log=v_a_log, dt_bias=v_dt_bias, rel_bias=v_rel_bias,
                w_onorm=v_w_onorm, w_branch_a=v_w_branch_a, w_branch_b=v_w_branch_b, w_out=v_w_out, g_ffn=v_g_ffn,
                w_gate_up=v_w_gate_up, w_down=v_w_down, g_ple=v_g_ple, w_ple_gate=v_w_ple_gate, w_ple_proj=v_w_ple_proj,
                g_final=v_g_final)
    mine = 4 * lax.axis_index("x") + 2 * lax.axis_index("y") + lax.axis_index("c")

    slabs = _weights_allgather(_pack_shards({name: given[name][0] for name, _, _ in MATRICES}, conv_w[0]))
    full, conv_full = _unpack_gathered(slabs)
    full["w_in"] = _permute_w_in(full["w_in"])
    small = dict(g_mix=g_mix, g_ffn=g_ffn, g_ple=g_ple, g_final=g_final, w_onorm=w_onorm, a_log=a_log, dt_bias=dt_bias,
                 rel_bias=rel_bias, conv_w=conv_full)

    loss_part, grad_x, gw, gs = _local_step(x, p[0], loss_target, full, small)
    gw["w_in"] = _unpermute_w_in(gw["w_in"])
    gs["loss"] = loss_part

    got, got_small = _grads_exchange(_pack_by_owner(gw), _pack_small(gs))
    grads = _unpack_shard(_sum_slots(got, name="sum_matrix_grads", tr=128))
    small_sum = _unpack_small(_sum_slots(got_small, name="sum_small_grads", tr=16))
    loss = small_sum.pop("loss")[0]
    conv_all = small_sum.pop("conv_w").reshape(4, N_DEV, TAPS_PER_SHARD)
    grads["conv_w"] = lax.dynamic_index_in_dim(conv_all, mine, axis=1, keepdims=False)
    grads.update(small_sum)

    out_g, out_d, out_m, out_v = [], [], [], []
    for name in WEIGHTS:
        g = grads[name].reshape(given[name].shape)
        delta, new_m, new_v = _adamw(given[name], g, mom1[name], mom2[name], name=f"adamw_{name}")
        out_g.append(g)
        out_d.append(delta)
        out_m.append(new_m)
        out_v.append(new_v)
    return (loss, grad_x, *out_g, *out_d, *out_m, *out_v)
```

```python
import jax
import jax.numpy as jnp
from jax import lax
from jax.experimental import pallas as pl
from jax.experimental.pallas import tpu as pltpu

F32 = jnp.float32
BF16 = jnp.bfloat16
DELTA_PREC = lax.Precision.HIGH
MESH = pl.DeviceIdType.MESH

N_DEV = 8
D = 1024
CHUNK = 64
EPS = 1e-6
A_HEADS, A_DIM, A_WIDTH = 8, 64, 512
A_BAND = 9 * CHUNK
A_PAD = 8 * CHUNK
REL_CLIP = 128
B_HEADS, B_DIM = 4, 128
B_CONV = 1536
D_FF = 2816
D_IN = 5640
P_CONV, P_Z, P_GATE, P_BD, P_END = 1536, 3072, 3584, 5632, 5760
LANE = 128

ADAM_LR, ADAM_B1, ADAM_B2, ADAM_EPS, ADAM_WD, ADAM_STEP = 0.001, 0.9, 0.999, 1e-08, 0.01, 10

NT = (((1,), (1,)), ((), ()))
TN = (((0,), (0,)), ((), ()))
NN = (((1,), (0,)), ((), ()))

HBM_SPEC = pl.BlockSpec(memory_space=pltpu.HBM)


def _tile(n, target, align=LANE):
    if n <= target:
        return n
    best = None
    for t in range(align, target + 1, align):
        if n % t == 0:
            best = t
    assert best is not None, (n, target, align)
    return best


def _mm(a, b, *, name, ta=False, add=None, out_dtype=F32, tm=512, tn=640, tk=1024):
    if ta:
        k_dim, m_dim = a.shape
    else:
        m_dim, k_dim = a.shape
    assert b.shape[0] == k_dim
    n_dim = b.shape[1]
    tm, tn = _tile(m_dim, tm), _tile(n_dim, tn)
    tk = _tile(k_dim, tk, 8 if ta else LANE)
    nk = k_dim // tk
    dn = TN if ta else NN

    def body(*refs):
        if add is None:
            a_ref, b_ref, o_ref = refs[:3]
            add_ref = None
        else:
            a_ref, b_ref, add_ref, o_ref = refs[:4]
        part = lax.dot_general(a_ref[...].astype(BF16), b_ref[...].astype(BF16), dn, preferred_element_type=F32)

        def finish(r):
            if add_ref is not None:
                r = r + add_ref[...]
            o_ref[...] = r.astype(o_ref.dtype)

        if nk == 1:
            finish(part)
        else:
            acc_ref = refs[-1]
            k = pl.program_id(2)

            @pl.when(k == 0)
            def _():
                acc_ref[...] = part

            @pl.when(k > 0)
            def _():
                acc_ref[...] += part

            @pl.when(k == nk - 1)
            def _():
                finish(acc_ref[...])

    a_spec = pl.BlockSpec((tk, tm), lambda i, j, k: (k, i)) if ta else pl.BlockSpec((tm, tk), lambda i, j, k: (i, k))
    in_specs = [a_spec, pl.BlockSpec((tk, tn), lambda i, j, k: (k, j))]
    args = [a, b]
    if add is not None:
        in_specs.append(pl.BlockSpec((tm, tn), lambda i, j, k: (i, j)))
        args.append(add)
    return pl.pallas_call(
        body,
        name=name,
        grid=(m_dim // tm, n_dim // tn, nk),
        in_specs=in_specs,
        out_specs=pl.BlockSpec((tm, tn), lambda i, j, k: (i, j)),
        out_shape=jax.ShapeDtypeStruct((m_dim, n_dim), out_dtype),
        scratch_shapes=[pltpu.VMEM((tm, tn), F32)] if nk > 1 else [],
        compiler_params=pltpu.CompilerParams(dimension_semantics=("parallel", "parallel", "arbitrary")),
    )(*args)


def _rowwise(fn, rows, bcs, outs, reds=(), *, name, tr, ncol=1):
    n_rows = rows[0][0].shape[0]
    tr = _tile(n_rows, tr, 8)
    nrow = n_rows // tr
    n_in, n_out = len(rows) + len(bcs), len(outs)

    def body(*refs):
        j, i = pl.program_id(0), pl.program_id(1)
        o_vals, r_vals = fn(*[r[...] for r in refs[:n_in]])
        for ref, val in zip(refs[n_in:n_in + n_out], o_vals):
            ref[...] = val.astype(ref.dtype)
        for ref, val, (_, _, stride) in zip(refs[n_in + n_out:], r_vals, reds):
            first = (i == 0) if stride else jnp.logical_and(i == 0, j == 0)

            @pl.when(first)
            def _():
                ref[...] = val

            @pl.when(jnp.logical_not(first))
            def _():
                ref[...] += val

    def spec(r, w, off, st, row_dep=True):
        if row_dep:
            return pl.BlockSpec((r, w), lambda j, i: (i, off + st * j))
        return pl.BlockSpec((r, w), lambda j, i: (0, off + st * j))

    in_specs = [spec(tr, w, off, st) for (_, w, off, st) in rows]
    in_specs += [spec(a.shape[0], w, off, st, False) for (a, w, off, st) in bcs]
    out_specs = [spec(tr, w, off, st) for (_, _, w, off, st) in outs]
    out_specs += [spec(1, w, 0, st, False) for (_, w, st) in reds]
    out_shape = [jax.ShapeDtypeStruct((n_rows, c), dt) for (c, dt, _, _, _) in outs]
    out_shape += [jax.ShapeDtypeStruct((1, c), F32) for (c, _, _) in reds]
    return pl.pallas_call(
        body,
        name=name,
        grid=(ncol, nrow),
        in_specs=in_specs,
        out_specs=out_specs,
        out_shape=out_shape,
        compiler_params=pltpu.CompilerParams(dimension_semantics=("arbitrary", "arbitrary")),
    )(*[r[0] for r in rows], *[b[0] for b in bcs])


def _full(a):
    return (a, a.shape[1], 0, 0)


def _rms(x, g):
    return x * lax.rsqrt(jnp.mean(x * x, axis=-1, keepdims=True) + EPS) * g


def _silu(x):
    return x * jax.nn.sigmoid(x)


def _softplus(x):
    return jnp.maximum(x, 0.0) + jnp.log(1.0 + jnp.exp(-jnp.abs(x)))


def _rms_fwd(x, g, *, name):
    (h,) = _rowwise(lambda xb, gb: ([_rms(xb, gb)], []), [_full(x)], [_full(g)], [(D, BF16, D, 0, 0)], name=name, tr=512)
    return h


def _rms_bwd(x, g, dh, dres, *, name):
    def fn(xb, dhb, dresb, gb):
        _, vjp = jax.vjp(_rms, xb, gb)
        dx, dg = vjp(dhb)
        return [dx + dresb], [dg]

    return _rowwise(fn, [_full(x), _full(dh), _full(dres)], [_full(g)], [(D, F32, D, 0, 0)], [(D, D, 0)], name=name, tr=256)


def _gate_scalars(raw, al_row, dtb_row):
    lane = lax.broadcasted_iota(jnp.int32, raw.shape, 1)
    beta = jax.nn.sigmoid(raw)
    g = -jnp.exp(al_row) * _softplus(raw + dtb_row)
    return jnp.where(lane < B_HEADS, beta, jnp.where(lane < 2 * B_HEADS, g, 0.0))


def _gated_norm(o, z, w):
    return _rms(o, w) * _silu(z)


def _merge(ga, gb, ta, tb):
    return jax.nn.sigmoid(ga) * ta + jax.nn.sigmoid(gb) * tb


def _swiglu(gu):
    return _silu(gu[:, :D_FF]) * gu[:, D_FF:]


def _head_loss(x2, pg, pp, tgt, g):
    x3 = x2 + jax.nn.sigmoid(pg) * pp
    err = _rms(x3, g) - tgt
    return 0.5 * jnp.sum(jnp.mean(err * err, axis=-1))


CONV_W = 256


def _conv_taps(x, w):
    row = lax.broadcasted_iota(jnp.int32, x.shape, 0)
    shifted = [x] + [jnp.where(row >= s, pltpu.roll(x, s, 0), 0.0) for s in (1, 2, 3)]
    pre = shifted[0] * w[3:4]
    for s in (1, 2, 3):
        pre = pre + shifted[s] * w[3 - s:4 - s]
    return pre, shifted


def _conv_fwd(projp, conv_w, n_batch, seq):
    ncol = B_CONV // CONV_W
    first = P_CONV // CONV_W

    def body(x_ref, w_ref, o_ref):
        pre, _ = _conv_taps(x_ref[...], w_ref[...])
        o_ref[...] = _silu(pre)

    return pl.pallas_call(
        body,
        name="conv_fwd",
        grid=(ncol, n_batch),
        in_specs=[pl.BlockSpec((seq, CONV_W), lambda j, b: (b, first + j)), pl.BlockSpec((4, CONV_W), lambda j, b: (0, j))],
        out_specs=pl.BlockSpec((seq, CONV_W), lambda j, b: (b, j)),
        out_shape=jax.ShapeDtypeStruct((n_batch * seq, B_CONV), F32),
        compiler_params=pltpu.CompilerParams(dimension_semantics=("parallel", "parallel")),
    )(projp, conv_w)


def _conv_bwd(projp, conv_w, dc, n_batch, seq):
    width = dc.shape[1]
    ncol = width // CONV_W
    first_x = P_CONV // CONV_W

    def body(x_ref, w_ref, dc_ref, dx_ref, dw_ref):
        b = pl.program_id(1)
        w = w_ref[...]
        pre, shifted = _conv_taps(x_ref[...], w)
        sg = jax.nn.sigmoid(pre)
        dpre = dc_ref[...] * (sg * (1.0 + pre * (1.0 - sg)))
        row = lax.broadcasted_iota(jnp.int32, dpre.shape, 0)
        dx = dpre * w[3:4]
        for s in (1, 2, 3):
            dx = dx + jnp.where(row < seq - s, pltpu.roll(dpre, seq - s, 0), 0.0) * w[3 - s:4 - s]
        dx_ref[...] = dx.astype(dx_ref.dtype)
        for s in (0, 1, 2, 3):
            part = jnp.sum(dpre * shifted[s], axis=0, keepdims=True)

            @pl.when(b == 0)
            def _():
                dw_ref[3 - s:4 - s, :] = part

            @pl.when(b > 0)
            def _():
                dw_ref[3 - s:4 - s, :] += part

    return pl.pallas_call(
        body,
        name="conv_bwd",
        grid=(ncol, n_batch),
        in_specs=[
            pl.BlockSpec((seq, CONV_W), lambda j, b: (b, first_x + j)),
            pl.BlockSpec((4, CONV_W), lambda j, b: (0, j)),
            pl.BlockSpec((seq, CONV_W), lambda j, b: (b, j)),
        ],
        out_specs=[pl.BlockSpec((seq, CONV_W), lambda j, b: (b, j)), pl.BlockSpec((4, CONV_W), lambda j, b: (0, j))],
        out_shape=[jax.ShapeDtypeStruct((n_batch * seq, width), BF16), jax.ShapeDtypeStruct((4, width), F32)],
        compiler_params=pltpu.CompilerParams(dimension_semantics=("arbitrary", "arbitrary")),
    )(projp, conv_w, dc)


def _attn_chunk(qc, kb, vb, bias2, valid, lane_lo):
    out = None
    for e in (0, 1):
        sel = lane_lo if e == 0 else jnp.logical_not(lane_lo)
        qm = jnp.where(sel, qc, 0.0) * (A_DIM ** -0.5)
        s = lax.dot_general(qm.astype(BF16), kb.astype(BF16), NT, preferred_element_type=F32) + bias2[e]
        s = jnp.where(valid, s, -1e30)
        p = jnp.exp(s - lax.stop_gradient(jnp.max(s, axis=-1, keepdims=True)))
        p = p / jnp.sum(p, axis=-1, keepdims=True)
        oe = jnp.where(sel, jnp.dot(p.astype(BF16), vb.astype(BF16), preferred_element_type=F32), 0.0)
        out = oe if out is None else out + oe
    return out


def _attn_masks(c):
    col = lax.broadcasted_iota(jnp.int32, (CHUNK, A_BAND), 1)
    valid = col + c * CHUNK >= A_PAD
    lane_lo = lax.broadcasted_iota(jnp.int32, (1, LANE), 1) < A_DIM
    return valid, lane_lo


def _attn_specs(seq):
    def blk(first):
        return pl.BlockSpec((seq, LANE), lambda hp, b: (b, first + hp))

    return blk, pl.BlockSpec((2, CHUNK, A_BAND), lambda hp, b: (hp, 0, 0))


def _attn_fwd(projp, bias, n_batch, seq):
    nc = seq // CHUNK
    blk, bias_spec = _attn_specs(seq)

    def body(q_ref, k_ref, v_ref, b_ref, o_ref, kp_ref, vp_ref):
        kp_ref[0:A_PAD, :] = jnp.zeros((A_PAD, LANE), F32)
        vp_ref[0:A_PAD, :] = jnp.zeros((A_PAD, LANE), F32)
        kp_ref[A_PAD:, :] = k_ref[...]
        vp_ref[A_PAD:, :] = v_ref[...]
        bias2 = b_ref[...]

        def step(c, carry):
            r0 = pl.multiple_of(c * CHUNK, CHUNK)
            valid, lane_lo = _attn_masks(c)
            out = _attn_chunk(q_ref[pl.ds(r0, CHUNK), :], kp_ref[pl.ds(r0, A_BAND), :], vp_ref[pl.ds(r0, A_BAND), :],
                              bias2, valid, lane_lo)
            o_ref[pl.ds(r0, CHUNK), :] = out.astype(o_ref.dtype)
            return carry

        lax.fori_loop(0, nc, step, 0)

    return pl.pallas_call(
        body,
        name="attn_fwd",
        grid=(A_HEADS // 2, n_batch),
        in_specs=[blk(0), blk(4), blk(8), bias_spec],
        out_specs=pl.BlockSpec((seq, LANE), lambda hp, b: (b, hp)),
        out_shape=jax.ShapeDtypeStruct((n_batch * seq, A_WIDTH), BF16),
        scratch_shapes=[pltpu.VMEM((A_PAD + seq, LANE), F32), pltpu.VMEM((A_PAD + seq, LANE), F32)],
        compiler_params=pltpu.CompilerParams(dimension_semantics=("parallel", "parallel")),
    )(projp, projp, projp, bias)


def _attn_bwd(projp, bias, dy, n_batch, seq):
    nc = seq // CHUNK
    blk, bias_spec = _attn_specs(seq)
    out_blk = pl.BlockSpec((seq, LANE), lambda hp, b: (b, hp))

    def body(q_ref, k_ref, v_ref, b_ref, dy_ref, dq_ref, dk_ref, dv_ref, db_ref, kp_ref, vp_ref, dkp_ref, dvp_ref):
        b = pl.program_id(1)
        kp_ref[0:A_PAD, :] = jnp.zeros((A_PAD, LANE), F32)
        vp_ref[0:A_PAD, :] = jnp.zeros((A_PAD, LANE), F32)
        kp_ref[A_PAD:, :] = k_ref[...]
        vp_ref[A_PAD:, :] = v_ref[...]
        dkp_ref[...] = jnp.zeros_like(dkp_ref)
        dvp_ref[...] = jnp.zeros_like(dvp_ref)
        bias2 = b_ref[...]

        @pl.when(b == 0)
        def _():
            db_ref[...] = jnp.zeros_like(db_ref)

        def step(c, carry):
            r0 = pl.multiple_of(c * CHUNK, CHUNK)
            valid, lane_lo = _attn_masks(c)
            _, vjp = jax.vjp(lambda q, k, v, bb: _attn_chunk(q, k, v, bb, valid, lane_lo), q_ref[pl.ds(r0, CHUNK), :],
                             kp_ref[pl.ds(r0, A_BAND), :], vp_ref[pl.ds(r0, A_BAND), :], bias2)
            dq, dk, dv, dbias = vjp(dy_ref[pl.ds(r0, CHUNK), :])
            dq_ref[pl.ds(r0, CHUNK), :] = dq.astype(dq_ref.dtype)
            dkp_ref[pl.ds(r0, A_BAND), :] += dk
            dvp_ref[pl.ds(r0, A_BAND), :] += dv
            db_ref[...] += dbias
            return carry

        lax.fori_loop(0, nc, step, 0)
        dk_ref[...] = dkp_ref[A_PAD:, :].astype(dk_ref.dtype)
        dv_ref[...] = dvp_ref[A_PAD:, :].astype(dv_ref.dtype)

    n_tok = n_batch * seq
    pad = pltpu.VMEM((A_PAD + seq, LANE), F32)
    return pl.pallas_call(
        body,
        name="attn_bwd",
        grid=(A_HEADS // 2, n_batch),
        in_specs=[blk(0), blk(4), blk(8), bias_spec, out_blk],
        out_specs=[out_blk, out_blk, out_blk, bias_spec],
        out_shape=[jax.ShapeDtypeStruct((n_tok, A_WIDTH), BF16)] * 3 + [jax.ShapeDtypeStruct((A_HEADS, CHUNK, A_BAND), F32)],
        scratch_shapes=[pad, pad, pad, pad],
        compiler_params=pltpu.CompilerParams(dimension_semantics=("arbitrary", "arbitrary")),
    )(projp, projp, projp, bias, dy)


def _rel_bias_table(rel_bias):
    far = rel_bias[:, 2 * REL_CLIP:]
    rev = rel_bias[:, ::-1]
    rows = []
    for i in range(CHUNK):
        n_far = A_BAND - 3 * CHUNK + i
        rows.append(jnp.concatenate([jnp.broadcast_to(far, (A_HEADS, n_far)), rev[:, :A_BAND - n_far]], axis=1))
    return jnp.stack(rows, axis=1)


def _dot(a, b, dn=NN):
    return lax.dot_general(a, b, dn, precision=DELTA_PREC, preferred_element_type=F32)


def _each(fn, *lists):
    return [fn(*vals) for vals in zip(*lists)]


def _delta_chunk(r_state, cq, ck, cv, beta, g):
    ii = lax.broadcasted_iota(jnp.int32, (CHUNK, CHUNK), 0)
    jj = lax.broadcasted_iota(jnp.int32, (CHUNK, CHUNK), 1)
    incl, strict, eye = ii >= jj, ii > jj, ii == jj
    q = _each(lambda t: t * lax.rsqrt(jnp.sum(t * t, axis=-1, keepdims=True) + EPS) * (B_DIM ** -0.5), cq)
    k = _each(lambda t: t * lax.rsqrt(jnp.sum(t * t, axis=-1, keepdims=True) + EPS), ck)
    g_b = _each(lambda t: jnp.broadcast_to(t, (CHUNK, CHUNK)), g)
    g_row = _each(lambda t: jnp.sum(jnp.where(eye, t, 0.0), axis=0, keepdims=True), g_b)
    gc_col = _each(lambda t: jnp.sum(jnp.where(incl, t, 0.0), axis=1, keepdims=True), g_row)
    gc_row = _each(lambda t: jnp.sum(jnp.where(ii <= jj, t, 0.0), axis=0, keepdims=True), g_b)
    decay = _each(lambda c, r: jnp.where(incl, jnp.exp(jnp.where(incl, c - r, 0.0)), 0.0), gc_col, gc_row)
    kk = _each(lambda t: _dot(t, t, NT), k)
    x = _each(lambda b, m, d: jnp.where(strict, -(b * m * d), 0.0), beta, kk, decay)
    inv = _each(lambda t: jnp.where(eye, 1.0, 0.0) + t, x)
    pw = x
    for _ in range(5):
        pw = _each(lambda t: _dot(t, t), pw)
        inv = _each(lambda t, s: t + _dot(t, s), inv, pw)
    egc = _each(jnp.exp, gc_col)
    u = _each(lambda t, b, v: _dot(t, b * v), inv, beta, cv)
    wk = _each(lambda t, b, e, kh: _dot(t, (b * e) * kh), inv, beta, egc, k)
    pqk = _each(lambda qh, kh, d: _dot(qh, kh, NT) * d, q, k, decay)
    g_last = _each(lambda c: c[CHUNK - 1:CHUNK, :], gc_col)
    kdec = _each(lambda kh, gl, c: kh * jnp.exp(gl - c), k, g_last, gc_col)
    w = _each(lambda uh, wkh, r: uh - _dot(wkh, r), u, wk, r_state)
    o = _each(lambda e, qh, r, ph, wh: e * _dot(qh, r) + _dot(ph, wh), egc, q, r_state, pqk, w)
    r_new = _each(lambda gl, r, kd, wh: jnp.exp(gl) * r + _dot(kd, wh, TN), g_last, r_state, kdec, w)
    return o, r_new


DELTA_BLK = 512


def _delta_blocks(n_batch, seq):
    nblk = seq // DELTA_BLK
    cpb = DELTA_BLK // CHUNK

    def rows(width, order):
        return pl.BlockSpec((DELTA_BLK, width), lambda b, i: (b * nblk + order(i), 0))

    def states(order):
        return pl.BlockSpec((cpb, B_HEADS, B_DIM, B_DIM), lambda b, i: (b * nblk + order(i), 0, 0, 0))

    return nblk, cpb, rows, states


def _head_cols(h):
    return [pl.ds(part * B_HEADS * B_DIM + h * B_DIM, B_DIM) for part in range(3)]


def _load_heads(c_ref, bg_ref, state_ref, rows):
    bg_c = bg_ref[rows, :]
    cols = [_head_cols(h) for h in range(B_HEADS)]
    return ([state_ref[h] for h in range(B_HEADS)], [c_ref[rows, c[0]] for c in cols], [c_ref[rows, c[1]] for c in cols],
            [c_ref[rows, c[2]] for c in cols], [bg_c[:, h:h + 1] for h in range(B_HEADS)],
            [bg_c[:, B_HEADS + h:B_HEADS + h + 1] for h in range(B_HEADS)])


def _delta_fwd(conv, bg, n_batch, seq):
    nblk, cpb, rows_spec, states_spec = _delta_blocks(n_batch, seq)

    def forward(i):
        return i

    def body(c_ref, bg_ref, o_ref, st_ref, r_ref):
        @pl.when(pl.program_id(1) == 0)
        def _():
            r_ref[...] = jnp.zeros_like(r_ref)

        def step(c, carry):
            rows = pl.ds(pl.multiple_of(c * CHUNK, CHUNK), CHUNK)
            args = _load_heads(c_ref, bg_ref, r_ref, rows)
            o, r_new = _delta_chunk(*args)
            for h in range(B_HEADS):
                st_ref[c, h] = args[0][h]
                o_ref[rows, pl.ds(h * B_DIM, B_DIM)] = o[h]
            for h in range(B_HEADS):
                r_ref[h] = r_new[h]
            return carry

        lax.fori_loop(0, cpb, step, 0)

    n_tok = n_batch * seq
    return pl.pallas_call(
        body,
        name="delta_fwd",
        grid=(n_batch, nblk),
        in_specs=[rows_spec(B_CONV, forward), rows_spec(LANE, forward)],
        out_specs=[rows_spec(B_HEADS * B_DIM, forward), states_spec(forward)],
        out_shape=[jax.ShapeDtypeStruct((n_tok, B_HEADS * B_DIM), F32),
                   jax.ShapeDtypeStruct((n_tok // CHUNK, B_HEADS, B_DIM, B_DIM), F32)],
        scratch_shapes=[pltpu.VMEM((B_HEADS, B_DIM, B_DIM), F32)],
        compiler_params=pltpu.CompilerParams(dimension_semantics=("arbitrary", "arbitrary")),
    )(conv, bg)


def _delta_bwd(conv, bg, states, do, n_batch, seq):
    nblk, cpb, rows_spec, states_spec = _delta_blocks(n_batch, seq)

    def backward(i):
        return nblk - 1 - i

    def body(c_ref, bg_ref, st_ref, do_ref, dc_ref, dbg_ref, dr_ref):
        @pl.when(pl.program_id(1) == 0)
        def _():
            dr_ref[...] = jnp.zeros_like(dr_ref)

        def step(n, carry):
            c = cpb - 1 - n
            rows = pl.ds(pl.multiple_of(c * CHUNK, CHUNK), CHUNK)
            _, vjp = jax.vjp(_delta_chunk, *_load_heads(c_ref, bg_ref, st_ref.at[c], rows))
            do = [do_ref[rows, pl.ds(h * B_DIM, B_DIM)] for h in range(B_HEADS)]
            dr, dq, dk, dv, dbeta, dg = vjp((do, [dr_ref[h] for h in range(B_HEADS)]))
            lane = lax.broadcasted_iota(jnp.int32, (CHUNK, LANE), 1)
            dbg = jnp.zeros((CHUNK, LANE), F32)
            for h in range(B_HEADS):
                cq, ck, cv = _head_cols(h)
                dr_ref[h] = dr[h]
                dc_ref[rows, cq] = dq[h]
                dc_ref[rows, ck] = dk[h]
                dc_ref[rows, cv] = dv[h]
                dbg = dbg + jnp.where(lane == h, dbeta[h], 0.0) + jnp.where(lane == h + B_HEADS, dg[h], 0.0)
            dbg_ref[rows, :] = dbg
            return carry

        lax.fori_loop(0, cpb, step, 0)

    n_tok = n_batch * seq
    return pl.pallas_call(
        body,
        name="delta_bwd",
        grid=(n_batch, nblk),
        in_specs=[rows_spec(B_CONV, backward), rows_spec(LANE, backward), states_spec(backward),
                  rows_spec(B_HEADS * B_DIM, backward)],
        out_specs=[rows_spec(B_CONV, backward), rows_spec(LANE, backward)],
        out_shape=[jax.ShapeDtypeStruct((n_tok, B_CONV), F32), jax.ShapeDtypeStruct((n_tok, LANE), F32)],
        scratch_shapes=[pltpu.VMEM((B_HEADS, B_DIM, B_DIM), F32)],
        compiler_params=pltpu.CompilerParams(dimension_semantics=("arbitrary", "arbitrary")),
    )(conv, bg, states, do)


def _lane_row(vec4, first):
    return jnp.concatenate([jnp.zeros((1, first), F32), vec4.reshape(1, B_HEADS).astype(F32),
                            jnp.zeros((1, LANE - first - B_HEADS), F32)], axis=1)


def _local_step(x3d, p3d, tgt3d, w, small):
    n_batch, seq, _ = x3d.shape
    n_tok = n_batch * seq
    x, p, tgt = x3d.reshape(n_tok, D), p3d.reshape(n_tok, -1), tgt3d.reshape(n_tok, D)
    g_mix, g_ffn, g_ple, g_final = (small[k].reshape(1, D) for k in ("g_mix", "g_ffn", "g_ple", "g_final"))
    w_onorm = small["w_onorm"].reshape(1, B_DIM)
    al_row = _lane_row(small["a_log"], B_HEADS)
    dtb_row = _lane_row(small["dt_bias"], B_HEADS)
    rel_bias = small["rel_bias"].reshape(A_HEADS, -1)
    bias = _rel_bias_table(rel_bias)
    conv_w = small["conv_w"].reshape(4, B_CONV)
    bd_blk = P_BD // LANE

    h1 = _rms_fwd(x, g_mix, name="rms_mix")
    projp = _mm(h1, w["w_in"], name="mm_proj", tn=640)
    y_a = _attn_fwd(projp, bias, n_batch, seq)
    conv = _conv_fwd(projp, conv_w, n_batch, seq)
    (bg,) = _rowwise(lambda raw, al, dtb: ([_gate_scalars(raw, al, dtb)], []), [(projp, LANE, bd_blk, 0)],
                     [_full(al_row), _full(dtb_row)], [(LANE, F32, LANE, 0, 0)], name="gate_scalars", tr=1024)
    o_b, states = _delta_fwd(conv, bg, n_batch, seq)
    (y_b,) = _rowwise(lambda o, z, wn: ([_gated_norm(o, z, wn)], []), [(o_b, LANE, 0, 1), (projp, LANE, P_Z // LANE, 1)],
                      [_full(w_onorm)], [(B_HEADS * B_DIM, BF16, LANE, 0, 1)], name="gated_norm", tr=1024, ncol=B_HEADS)
    t_a = _mm(y_a, w["w_branch_a"], name="mm_branch_a", tn=1024)
    t_b = _mm(y_b, w["w_branch_b"], name="mm_branch_b", tn=1024)
    half = D // 2
    gate_rows = [(projp, half, P_GATE // half, 1), (projp, half, P_GATE // half + 2, 1), (t_a, half, 0, 1), (t_b, half, 0, 1)]
    (merged,) = _rowwise(lambda ga, gb, ta, tb: ([_merge(ga, gb, ta, tb)], []), gate_rows, [], [(D, BF16, half, 0, 1)],
                         name="merge", tr=512, ncol=2)
    x1 = _mm(merged, w["w_out"], add=x, name="mm_out", tn=1024)
    h2 = _rms_fwd(x1, g_ffn, name="rms_ffn")
    gu = _mm(h2, w["w_gate_up"], name="mm_gate_up", tn=512)
    (act,) = _rowwise(lambda gub: ([_swiglu(gub)], []), [_full(gu)], [], [(D_FF, BF16, D_FF, 0, 0)], name="swiglu", tr=256)
    x2 = _mm(act, w["w_down"], add=x1, name="mm_down", tn=1024, tk=1408)
    h3 = _rms_fwd(x2, g_ple, name="rms_ple")
    pg = _mm(h3, w["w_ple_gate"], name="mm_ple_gate", tn=1024)
    pp = _mm(p, w["w_ple_proj"], name="mm_ple_proj", tn=1024)

    def head_fn(x2b, pgb, ppb, tb, gb):
        loss, (dx2, dpg, dpp, dg) = jax.value_and_grad(_head_loss, argnums=(0, 1, 2, 4))(x2b, pgb, ppb, tb, gb)
        return [dx2, dpg, dpp], [dg, jnp.full((1, LANE), loss, F32)]

    dx3, dpg, dpp, dg_final, loss_row = _rowwise(
        head_fn, [_full(x2), _full(pg), _full(pp), _full(tgt)], [_full(g_final)],
        [(D, F32, D, 0, 0), (D, BF16, D, 0, 0), (D, BF16, D, 0, 0)], [(D, D, 0), (LANE, LANE, 0)], name="loss_head", tr=256)
    gw = {}
    gw["w_ple_proj"] = _mm(p, dpp, ta=True, name="mm_d_ple_proj", tm=256, tn=1024)
    gw["w_ple_gate"] = _mm(h3, dpg, ta=True, name="mm_d_ple_gate", tn=1024)
    dh3 = _mm(dpg, w["w_ple_gate"].T, name="mm_dh3", tn=1024)
    dx2, dg_ple = _rms_bwd(x2, g_ple, dh3, dx3, name="rms_ple_bwd")
    gw["w_down"] = _mm(act, dx2, ta=True, name="mm_d_down", tm=256, tn=1024)
    dact = _mm(dx2, w["w_down"].T, name="mm_dact", tn=1408)

    def swiglu_bwd(gub, dab):
        _, vjp = jax.vjp(_swiglu, gub)
        return [vjp(dab)[0]], []

    (dgu,) = _rowwise(swiglu_bwd, [_full(gu), _full(dact)], [], [(2 * D_FF, BF16, 2 * D_FF, 0, 0)], name="swiglu_bwd", tr=256)
    gw["w_gate_up"] = _mm(h2, dgu, ta=True, name="mm_d_gate_up", tn=512)
    dh2 = _mm(dgu, w["w_gate_up"].T, name="mm_dh2", tn=1024, tk=1408)
    dx1, dg_ffn = _rms_bwd(x1, g_ffn, dh2, dx2, name="rms_ffn_bwd")
    gw["w_out"] = _mm(merged, dx1, ta=True, name="mm_d_out", tn=1024)
    dmerged = _mm(dx1, w["w_out"].T, name="mm_dmerged", tn=1024)

    def merge_bwd(ga, gb, ta, tb, dm):
        _, vjp = jax.vjp(_merge, ga, gb, ta, tb)
        return list(vjp(dm)), []

    dga, dgb, dta, dtb = _rowwise(merge_bwd, gate_rows + [(dmerged, half, 0, 1)], [], [(D, BF16, half, 0, 1)] * 4,
                                  name="merge_bwd", tr=512, ncol=2)
    gw["w_branch_a"] = _mm(y_a, dta, ta=True, name="mm_d_branch_a", tn=1024)
    gw["w_branch_b"] = _mm(y_b, dtb, ta=True, name="mm_d_branch_b", tn=1024)
    dya = _mm(dta, w["w_branch_a"].T, name="mm_dya", tn=512)
    dyb = _mm(dtb, w["w_branch_b"].T, name="mm_dyb", tn=512)

    def gated_norm_bwd(o, z, dy, wn):
        _, vjp = jax.vjp(_gated_norm, o, z, wn)
        do, dz, dwn = vjp(dy)
        return [do, dz], [dwn]

    do_b, dz, dw_onorm = _rowwise(
        gated_norm_bwd, [(o_b, LANE, 0, 1), (projp, LANE, P_Z // LANE, 1), (dyb, LANE, 0, 1)], [_full(w_onorm)],
        [(B_HEADS * B_DIM, F32, LANE, 0, 1), (B_HEADS * B_DIM, BF16, LANE, 0, 1)], [(B_DIM, B_DIM, 0)],
        name="gated_norm_bwd", tr=1024, ncol=B_HEADS)
    dconv_out, dbg = _delta_bwd(conv, bg, states, do_b, n_batch, seq)

    def gate_scalars_bwd(raw, dbgb, al, dtb):
        _, vjp = jax.vjp(_gate_scalars, raw, al, dtb)
        draw, dal, ddtb = vjp(dbgb)
        return [draw], [dal, ddtb]

    dbd, dal_row, ddtb_row = _rowwise(gate_scalars_bwd, [(projp, LANE, bd_blk, 0), _full(dbg)], [_full(al_row), _full(dtb_row)],
                                      [(LANE, BF16, LANE, 0, 0)], [(LANE, LANE, 0), (LANE, LANE, 0)], name="gate_scalars_bwd",
                                      tr=1024)
    dconv, dconv_w = _conv_bwd(projp, conv_w, dconv_out, n_batch, seq)
    dq_a, dk_a, dv_a, dbias = _attn_bwd(projp, bias, dya, n_batch, seq)
    dprojp = jnp.concatenate([dq_a, dk_a, dv_a, dconv, dz, dga, dgb, dbd], axis=1)
    gw["w_in"] = _mm(h1, dprojp, ta=True, name="mm_d_in", tn=640)
    dh1 = _mm(dprojp, w["w_in"].T, name="mm_dh1", tn=1024, tk=1152)
    grad_x, dg_mix = _rms_bwd(x, g_mix, dh1, dx1, name="rms_mix_bwd")

    _, bias_vjp = jax.vjp(_rel_bias_table, rel_bias)
    gs = {
        "g_mix": dg_mix, "g_ffn": dg_ffn, "g_ple": dg_ple, "g_final": dg_final, "w_onorm": dw_onorm,
        "conv_w": dconv_w, "rel_bias": bias_vjp(dbias)[0],
        "a_log": dal_row[0, B_HEADS:2 * B_HEADS], "dt_bias": ddtb_row[0, B_HEADS:2 * B_HEADS],
    }
    return loss_row[:, :1], grad_x.reshape(n_batch, seq, D), gw, gs


MATRICES = (
    ("w_in", (D, D_IN), 1), ("w_gate_up", (D, 2 * D_FF), 1), ("w_branch_a", (A_WIDTH, D), 1), ("w_branch_b", (A_WIDTH, D), 1),
    ("w_out", (D, D), 0), ("w_down", (D_FF, D), 0), ("w_ple_gate", (D, D), 0), ("w_ple_proj", (256, D), 1),
)
MATRIX_ROWS = sum(s[0] * s[1] for _, s, _ in MATRICES) // (N_DEV * D)
TAP_ROWS = 2
PACK_ROWS = 2208
TAPS_PER_SHARD = B_CONV // N_DEV


def _shard_shape(shape, axis):
    return tuple(n // N_DEV if a == axis else n for a, n in enumerate(shape))


def _pack_shards(shards, taps):
    parts = [shards[name].reshape(-1, D).astype(BF16) for name, _, _ in MATRICES]
    tap_bits = lax.bitcast_convert_type(taps, BF16).reshape(1, -1)
    parts.append(jnp.concatenate([tap_bits, jnp.zeros((1, TAP_ROWS * D - tap_bits.shape[1]), BF16)], axis=1).reshape(TAP_ROWS, D))
    parts.append(jnp.zeros((PACK_ROWS - MATRIX_ROWS - TAP_ROWS, D), BF16))
    return jnp.concatenate(parts, axis=0)


def _unpack_gathered(slabs):
    out, r0 = {}, 0
    for name, shape, axis in MATRICES:
        shard = _shard_shape(shape, axis)
        rows = shard[0] * shard[1] // D
        part = slabs[:, r0:r0 + rows, :].reshape((N_DEV,) + shard)
        out[name] = part.reshape(shape) if axis == 0 else jnp.transpose(part, (1, 0, 2)).reshape(shape)
        r0 += rows
    tap_bits = slabs[:, r0:r0 + TAP_ROWS, :].reshape(N_DEV, TAP_ROWS * D)[:, :4 * TAPS_PER_SHARD * 2]
    taps = lax.bitcast_convert_type(tap_bits.reshape(N_DEV, 4, TAPS_PER_SHARD, 2), F32)
    return out, jnp.transpose(taps, (1, 0, 2)).reshape(4, B_CONV)


def _pack_by_owner(full):
    parts = []
    for name, shape, axis in MATRICES:
        shard = _shard_shape(shape, axis)
        m = full[name].astype(BF16)
        if axis == 0:
            m = m.reshape((N_DEV,) + shard)
        else:
            m = jnp.transpose(m.reshape(shape[0], N_DEV, shard[1]), (1, 0, 2))
        parts.append(m.reshape(N_DEV, -1, D))
    parts.append(jnp.zeros((N_DEV, PACK_ROWS - MATRIX_ROWS, D), BF16))
    return jnp.concatenate(parts, axis=1)


def _unpack_shard(slab):
    out, r0 = {}, 0
    for name, shape, axis in MATRICES:
        shard = _shard_shape(shape, axis)
        rows = shard[0] * shard[1] // D
        out[name] = slab[r0:r0 + rows, :].reshape(shard)
        r0 += rows
    return out


def _permute_w_in(w_in):
    zeros = jnp.zeros((D, P_END - D_IN), w_in.dtype)
    return jnp.concatenate([w_in[:, :P_GATE], w_in[:, P_GATE + 8:], w_in[:, P_GATE:P_GATE + 8], zeros], axis=1)


def _unpermute_w_in(gp):
    return jnp.concatenate([gp[:, :P_GATE], gp[:, P_BD:P_BD + 8], gp[:, P_GATE:P_BD]], axis=1)


SMALL_ROWS = 16
SMALL_LAYOUT = (("g_mix", 0, D), ("g_ffn", 1, D), ("g_ple", 2, D), ("g_final", 3, D), ("conv_w", 4, 4 * B_CONV),
                ("rel_bias", 10, A_HEADS * (2 * REL_CLIP + 1)), ("w_onorm", 13, B_DIM), ("a_log", 14, B_HEADS),
                ("dt_bias", 14, B_HEADS), ("loss", 15, 1))


def _pack_small(gs):
    rows = {}
    for name, row, n in SMALL_LAYOUT:
        rows.setdefault(row, []).append(gs[name].reshape(-1).astype(F32))
    parts = []
    for row in sorted(rows):
        flat = jnp.concatenate(rows[row])
        parts.append(jnp.concatenate([flat, jnp.zeros((-flat.shape[0] % D,), F32)]))
    flat = jnp.concatenate(parts)
    assert flat.shape[0] == SMALL_ROWS * D, flat.shape
    return flat.reshape(SMALL_ROWS, D)


def _unpack_small(blk):
    flat, out, used = blk.reshape(-1), {}, {}
    for name, row, n in SMALL_LAYOUT:
        start = row * D + used.get(row, 0)
        out[name] = flat[start:start + n]
        used[row] = used.get(row, 0) + n
    return out


def _position():
    return lax.axis_index("x"), lax.axis_index("y"), lax.axis_index("c")


def _weights_allgather(pack):
    def body(x_ref, out_ref, send_sems, recv_sems, local_sem):
        x, y, c = _position()
        me, sibling = (x, y, c), (x, y, 1 - c)
        chips = [(1 - x, y), (x, 1 - y), (1 - x, 1 - y)]

        def slab(px, py, pc):
            return out_ref.at[4 * px + 2 * py + pc]

        def copy(k, block, to, src=None):
            return pltpu.make_async_remote_copy(src_ref=slab(*block) if src is None else src, dst_ref=slab(*block),
                                                send_sem=send_sems.at[k], recv_sem=recv_sems.at[k], device_id=to,
                                                device_id_type=MESH)

        mine = pltpu.make_async_copy(x_ref, slab(*me), local_sem)
        mine.start()
        first = [copy(0, me, sibling, src=x_ref)]
        first += [copy(1 + j, me, (*chip, c), src=x_ref) for j, chip in enumerate(chips)]
        for cp in first:
            cp.start()
        passed = [copy(4 + j, (*chip, c), sibling) for j, chip in enumerate(chips)]
        for j, chip in enumerate(chips):
            copy(1 + j, (*chip, c), me).wait_recv()
            passed[j].start()
        copy(0, sibling, me).wait_recv()
        for j, chip in enumerate(chips):
            copy(4 + j, (*chip, 1 - c), me).wait_recv()
        for cp in first + passed:
            cp.wait_send()
        mine.wait()

    return pl.pallas_call(
        body,
        name="weights_allgather",
        out_shape=jax.ShapeDtypeStruct((N_DEV,) + pack.shape, pack.dtype),
        in_specs=[HBM_SPEC],
        out_specs=HBM_SPEC,
        scratch_shapes=[pltpu.SemaphoreType.DMA((7,)), pltpu.SemaphoreType.DMA((7,)), pltpu.SemaphoreType.DMA],
    )(pack)


def _grads_exchange(by_owner, small):
    def body(g_ref, s_ref, got_ref, got_s_ref, send_sems, recv_sems, local_sems):
        x, y, c = _position()
        mine = 4 * x + 2 * y + c
        own = pltpu.make_async_copy(g_ref.at[mine], got_ref.at[mine], local_sems.at[0])
        own_s = pltpu.make_async_copy(s_ref, got_s_ref.at[mine], local_sems.at[1])
        own.start()
        own_s.start()
        flips = [(dx, dy, dc) for dx in (0, 1) for dy in (0, 1) for dc in (0, 1) if dx + dy + dc]
        copies = []
        for k, (dx, dy, dc) in enumerate(flips):
            px, py, pc = (1 - x if dx else x), (1 - y if dy else y), (1 - c if dc else c)
            peer = 4 * px + 2 * py + pc

            def remote(src, dst, sem):
                return pltpu.make_async_remote_copy(src_ref=src, dst_ref=dst, send_sem=send_sems.at[sem],
                                                    recv_sem=recv_sems.at[sem], device_id=(px, py, pc), device_id_type=MESH)

            big, sml = remote(g_ref.at[peer], got_ref.at[mine], k), remote(s_ref, got_s_ref.at[mine], 7 + k)
            big.start()
            sml.start()
            copies += [(big, remote(g_ref.at[peer], got_ref.at[peer], k)), (sml, remote(s_ref, got_s_ref.at[peer], 7 + k))]
        for sent, landed in copies:
            landed.wait_recv()
            sent.wait_send()
        own.wait()
        own_s.wait()

    return pl.pallas_call(
        body,
        name="grads_exchange",
        out_shape=[jax.ShapeDtypeStruct(by_owner.shape, by_owner.dtype), jax.ShapeDtypeStruct((N_DEV,) + small.shape, small.dtype)],
        in_specs=[HBM_SPEC, HBM_SPEC],
        out_specs=[HBM_SPEC, HBM_SPEC],
        scratch_shapes=[pltpu.SemaphoreType.DMA((14,)), pltpu.SemaphoreType.DMA((14,)), pltpu.SemaphoreType.DMA((2,))],
    )(by_owner, small)


def _sum_slots(got, *, name, tr):
    _, rows, cols = got.shape
    tr = _tile(rows, tr, 16)

    def body(g_ref, o_ref):
        acc = g_ref[0].astype(F32)
        for j in range(1, N_DEV):
            acc = acc + g_ref[j].astype(F32)
        o_ref[...] = acc

    return pl.pallas_call(
        body,
        name=name,
        grid=(rows // tr,),
        in_specs=[pl.BlockSpec((N_DEV, tr, cols), lambda i: (0, i, 0))],
        out_specs=pl.BlockSpec((tr, cols), lambda i: (i, 0)),
        out_shape=jax.ShapeDtypeStruct((rows, cols), F32),
        compiler_params=pltpu.CompilerParams(dimension_semantics=("parallel",)),
    )(got)


def _adamw(wt, g, m, v, *, name):
    shape = wt.shape
    two_d = (-1, shape[-1]) if wt.ndim > 1 else (1, -1)
    args = [a.reshape(two_d) for a in (wt, g, m, v)]
    rows, cols = args[0].shape
    tr = _tile(rows, 256, 8) if rows % 8 == 0 else rows

    def body(w_ref, g_ref, m_ref, v_ref, d_ref, nm_ref, nv_ref):
        gv = g_ref[...]
        m2 = ADAM_B1 * m_ref[...] + (1.0 - ADAM_B1) * gv
        v2 = ADAM_B2 * v_ref[...] + (1.0 - ADAM_B2) * (gv * gv)
        m_hat = m2 / (1.0 - ADAM_B1 ** ADAM_STEP)
        v_hat = v2 / (1.0 - ADAM_B2 ** ADAM_STEP)
        d_ref[...] = -ADAM_LR * (m_hat / (jnp.sqrt(v_hat) + ADAM_EPS) + ADAM_WD * w_ref[...])
        nm_ref[...] = m2
        nv_ref[...] = v2

    spec = pl.BlockSpec((tr, cols), lambda i: (i, 0))
    outs = pl.pallas_call(
        body,
        name=name,
        grid=(rows // tr,),
        in_specs=[spec] * 4,
        out_specs=[spec] * 3,
        out_shape=[jax.ShapeDtypeStruct((rows, cols), F32)] * 3,
        compiler_params=pltpu.CompilerParams(dimension_semantics=("parallel",)),
    )(*args)
    return tuple(o.reshape(shape) for o in outs)


WEIGHTS = ("g_mix", "w_in", "conv_w", "a_log", "dt_bias", "rel_bias", "w_onorm", "w_branch_a", "w_branch_b", "w_out", "g_ffn",
           "w_gate_up", "w_down", "g_ple", "w_ple_gate", "w_ple_proj", "g_final")


def kernel(x, p, g_mix, w_in, conv_w, a_log, dt_bias, rel_bias, w_onorm, w_branch_a, w_branch_b, w_out, g_ffn, w_gate_up, w_down, g_ple, w_ple_gate, w_ple_proj, g_final, loss_target, m_g_mix, m_w_in, m_conv_w, m_a_log, m_dt_bias, m_rel_bias, m_w_onorm, m_w_branch_a, m_w_branch_b, m_w_out, m_g_ffn, m_w_gate_up, m_w_down, m_g_ple, m_w_ple_gate, m_w_ple_proj, m_g_final, v_g_mix, v_w_in, v_conv_w, v_a_log, v_dt_bias, v_rel_bias, v_w_onorm, v_w_branch_a, v_w_branch_b, v_w_out, v_g_ffn, v_w_gate_up, v_w_down, v_g_ple, v_w_ple_gate, v_w_ple_proj, v_g_final):
    given = dict(g_mix=g_mix, w_in=w_in, conv_w=conv_w, a_log=a_log, dt_bias=dt_bias, rel_bias=rel_bias, w_onorm=w_onorm,
                 w_branch_a=w_branch_a, w_branch_b=w_branch_b, w_out=w_out, g_ffn=g_ffn, w_gate_up=w_gate_up, w_down=w_down,
                 g_ple=g_ple, w_ple_gate=w_ple_gate, w_ple_proj=w_ple_proj, g_final=g_final)
    mom1 = dict(g_mix=m_g_mix, w_in=m_w_in, conv_w=m_conv_w, a_log=m_a_log, dt_bias=m_dt_bias, rel_bias=m_rel_bias,
                w_onorm=m_w_onorm, w_branch_a=m_w_branch_a, w_branch_b=m_w_branch_b, w_out=m_w_out, g_ffn=m_g_ffn,
                w_gate_up=m_w_gate_up, w_down=m_w_down, g_ple=m_g_ple, w_ple_gate=m_w_ple_gate, w_ple_proj=m_w_ple_proj,
                g_final=m_g_final)
    mom2 = dict(g_mix=v_g_mix, w_in=v_w_in, conv_w=v_conv_w, a_log=v_a_log, dt_bias=v_dt_bias, rel_bias=v_rel_bias,
                w_onorm=v_w_onorm, w_branch_a=v_w_branch_a, w_branch_b=v_w_branch_b, w_out=v_w_out, g_ffn=v_g_ffn,
                w_gate_up=v_w_gate_up, w_down=v_w_down, g_ple=v_g_ple, w_ple_gate=v_w_ple_gate, w_ple_proj=v_w_ple_proj,
                g_final=v_g_final)
    mine = 4 * lax.axis_index("x") + 2 * lax.axis_index("y") + lax.axis_index("c")

    slabs = _weights_allgather(_pack_shards({name: given[name][0] for name, _, _ in MATRICES}, conv_w[0]))
    full, conv_full = _unpack_gathered(slabs)
    full["w_in"] = _permute_w_in(full["w_in"])
    small = dict(g_mix=g_mix, g_ffn=g_ffn, g_ple=g_ple, g_final=g_final, w_onorm=w_onorm, a_log=a_log, dt_bias=dt_bias,
                 rel_bias=rel_bias, conv_w=conv_full)

    loss_part, grad_x, gw, gs = _local_step(x, p[0], loss_target, full, small)
    gw["w_in"] = _unpermute_w_in(gw["w_in"])
    gs["loss"] = loss_part

    got, got_small = _grads_exchange(_pack_by_owner(gw), _pack_small(gs))
    grads = _unpack_shard(_sum_slots(got, name="sum_matrix_grads", tr=128))
    small_sum = _unpack_small(_sum_slots(got_small, name="sum_small_grads", tr=16))
    loss = small_sum.pop("loss")[0]
    conv_all = small_sum.pop("conv_w").reshape(4, N_DEV, TAPS_PER_SHARD)
    grads["conv_w"] = lax.dynamic_index_in_dim(conv_all, mine, axis=1, keepdims=False)
    grads.update(small_sum)

    out_g, out_d, out_m, out_v = [], [], [], []
    for name in WEIGHTS:
        g = grads[name].reshape(given[name].shape)
        delta, new_m, new_v = _adamw(given[name], g, mom1[name], mom2[name], name=f"adamw_{name}")
        out_g.append(g)
        out_d.append(delta)
        out_m.append(new_m)
        out_v.append(new_v)
    return (loss, grad_x, *out_g, *out_d, *out_m, *out_v)
```

```python
import jax
import jax.numpy as jnp
from jax import lax
from jax.experimental import pallas as pl
from jax.experimental.pallas import tpu as pltpu

F32 = jnp.float32
BF16 = jnp.bfloat16
DELTA_PREC = lax.Precision.HIGH
MESH = pl.DeviceIdType.MESH

N_DEV = 8
D = 1024
CHUNK = 64
EPS = 1e-6
A_HEADS, A_DIM, A_WIDTH = 8, 64, 512
A_BAND = 9 * CHUNK
A_PAD = 8 * CHUNK
REL_CLIP = 128
B_HEADS, B_DIM = 4, 128
B_CONV = 1536
D_FF = 2816
D_IN = 5640
P_CONV, P_Z, P_GATE, P_BD, P_END = 1536, 3072, 3584, 5632, 5760
LANE = 128

ADAM_LR, ADAM_B1, ADAM_B2, ADAM_EPS, ADAM_WD, ADAM_STEP = 0.001, 0.9, 0.999, 1e-08, 0.01, 10

NT = (((1,), (1,)), ((), ()))
TN = (((0,), (0,)), ((), ()))
NN = (((1,), (0,)), ((), ()))

HBM_SPEC = pl.BlockSpec(memory_space=pltpu.HBM)


def _tile(n, target, align=LANE):
    if n <= target:
        return n
    best = None
    for t in range(align, target + 1, align):
        if n % t == 0:
            best = t
    assert best is not None, (n, target, align)
    return best


def _mm(a, b, *, name, ta=False, add=None, out_dtype=F32, tm=512, tn=640, tk=1024):
    if ta:
        k_dim, m_dim = a.shape
    else:
        m_dim, k_dim = a.shape
    assert b.shape[0] == k_dim
    n_dim = b.shape[1]
    tm, tn = _tile(m_dim, tm), _tile(n_dim, tn)
    tk = _tile(k_dim, tk, 8 if ta else LANE)
    nk = k_dim // tk
    dn = TN if ta else NN

    def body(*refs):
        if add is None:
            a_ref, b_ref, o_ref = refs[:3]
            add_ref = None
        else:
            a_ref, b_ref, add_ref, o_ref = refs[:4]
        part = lax.dot_general(a_ref[...].astype(BF16), b_ref[...].astype(BF16), dn, preferred_element_type=F32)

        def finish(r):
            if add_ref is not None:
                r = r + add_ref[...]
            o_ref[...] = r.astype(o_ref.dtype)

        if nk == 1:
            finish(part)
        else:
            acc_ref = refs[-1]
            k = pl.program_id(2)

            @pl.when(k == 0)
            def _():
                acc_ref[...] = part

            @pl.when(k > 0)
            def _():
                acc_ref[...] += part

            @pl.when(k == nk - 1)
            def _():
                finish(acc_ref[...])

    a_spec = pl.BlockSpec((tk, tm), lambda i, j, k: (k, i)) if ta else pl.BlockSpec((tm, tk), lambda i, j, k: (i, k))
    in_specs = [a_spec, pl.BlockSpec((tk, tn), lambda i, j, k: (k, j))]
    args = [a, b]
    if add is not None:
        in_specs.append(pl.BlockSpec((tm, tn), lambda i, j, k: (i, j)))
        args.append(add)
    return pl.pallas_call(
        body,
        name=name,
        grid=(m_dim // tm, n_dim // tn, nk),
        in_specs=in_specs,
        out_specs=pl.BlockSpec((tm, tn), lambda i, j, k: (i, j)),
        out_shape=jax.ShapeDtypeStruct((m_dim, n_dim), out_dtype),
        scratch_shapes=[pltpu.VMEM((tm, tn), F32)] if nk > 1 else [],
        compiler_params=pltpu.CompilerParams(dimension_semantics=("parallel", "parallel", "arbitrary")),
    )(*args)


def _rowwise(fn, rows, bcs, outs, reds=(), *, name, tr, ncol=1):
    n_rows = rows[0][0].shape[0]
    tr = _tile(n_rows, tr, 8)
    nrow = n_rows // tr
    n_in, n_out = len(rows) + len(bcs), len(outs)

    def body(*refs):
        j, i = pl.program_id(0), pl.program_id(1)
        o_vals, r_vals = fn(*[r[...] for r in refs[:n_in]])
        for ref, val in zip(refs[n_in:n_in + n_out], o_vals):
            ref[...] = val.astype(ref.dtype)
        for ref, val, (_, _, stride) in zip(refs[n_in + n_out:], r_vals, reds):
            first = (i == 0) if stride else jnp.logical_and(i == 0, j == 0)

            @pl.when(first)
            def _():
                ref[...] = val

            @pl.when(jnp.logical_not(first))
            def _():
                ref[...] += val

    def spec(r, w, off, st, row_dep=True):
        if row_dep:
            return pl.BlockSpec((r, w), lambda j, i: (i, off + st * j))
        return pl.BlockSpec((r, w), lambda j, i: (0, off + st * j))

    in_specs = [spec(tr, w, off, st) for (_, w, off, st) in rows]
    in_specs += [spec(a.shape[0], w, off, st, False) for (a, w, off, st) in bcs]
    out_specs = [spec(tr, w, off, st) for (_, _, w, off, st) in outs]
    out_specs += [spec(1, w, 0, st, False) for (_, w, st) in reds]
    out_shape = [jax.ShapeDtypeStruct((n_rows, c), dt) for (c, dt, _, _, _) in outs]
    out_shape += [jax.ShapeDtypeStruct((1, c), F32) for (c, _, _) in reds]
    return pl.pallas_call(
        body,
        name=name,
        grid=(ncol, nrow),
        in_specs=in_specs,
        out_specs=out_specs,
        out_shape=out_shape,
        compiler_params=pltpu.CompilerParams(dimension_semantics=("arbitrary", "arbitrary")),
    )(*[r[0] for r in rows], *[b[0] for b in bcs])


def _full(a):
    return (a, a.shape[1], 0, 0)


def _rms(x, g):
    return x * lax.rsqrt(jnp.mean(x * x, axis=-1, keepdims=True) + EPS) * g


def _silu(x):
    return x * jax.nn.sigmoid(x)


def _softplus(x):
    return jnp.maximum(x, 0.0) + jnp.log(1.0 + jnp.exp(-jnp.abs(x)))


def _rms_fwd(x, g, *, name):
    (h,) = _rowwise(lambda xb, gb: ([_rms(xb, gb)], []), [_full(x)], [_full(g)], [(D, BF16, D, 0, 0)], name=name, tr=512)
    return h


def _rms_bwd(x, g, dh, dres, *, name):
    def fn(xb, dhb, dresb, gb):
        _, vjp = jax.vjp(_rms, xb, gb)
        dx, dg = vjp(dhb)
        return [dx + dresb], [dg]

    return _rowwise(fn, [_full(x), _full(dh), _full(dres)], [_full(g)], [(D, F32, D, 0, 0)], [(D, D, 0)], name=name, tr=256)


def _gate_scalars(raw, al_row, dtb_row):
    lane = lax.broadcasted_iota(jnp.int32, raw.shape, 1)
    beta = jax.nn.sigmoid(raw)
    g = -jnp.exp(al_row) * _softplus(raw + dtb_row)
    return jnp.where(lane < B_HEADS, beta, jnp.where(lane < 2 * B_HEADS, g, 0.0))


def _gated_norm(o, z, w):
    return _rms(o, w) * _silu(z)


def _merge(ga, gb, ta, tb):
    return jax.nn.sigmoid(ga) * ta + jax.nn.sigmoid(gb) * tb


def _swiglu(gu):
    return _silu(gu[:, :D_FF]) * gu[:, D_FF:]


def _head_loss(x2, pg, pp, tgt, g):
    x3 = x2 + jax.nn.sigmoid(pg) * pp
    err = _rms(x3, g) - tgt
    return 0.5 * jnp.sum(jnp.mean(err * err, axis=-1))


CONV_W = 256


def _conv_taps(x, w):
    row = lax.broadcasted_iota(jnp.int32, x.shape, 0)
    shifted = [x] + [jnp.where(row >= s, pltpu.roll(x, s, 0), 0.0) for s in (1, 2, 3)]
    pre = shifted[0] * w[3:4]
    for s in (1, 2, 3):
        pre = pre + shifted[s] * w[3 - s:4 - s]
    return pre, shifted


def _conv_fwd(projp, conv_w, n_batch, seq):
    ncol = B_CONV // CONV_W
    first = P_CONV // CONV_W

    def body(x_ref, w_ref, o_ref):
        pre, _ = _conv_taps(x_ref[...], w_ref[...])
        o_ref[...] = _silu(pre)

    return pl.pallas_call(
        body,
        name="conv_fwd",
        grid=(ncol, n_batch),
        in_specs=[pl.BlockSpec((seq, CONV_W), lambda j, b: (b, first + j)), pl.BlockSpec((4, CONV_W), lambda j, b: (0, j))],
        out_specs=pl.BlockSpec((seq, CONV_W), lambda j, b: (b, j)),
        out_shape=jax.ShapeDtypeStruct((n_batch * seq, B_CONV), F32),
        compiler_params=pltpu.CompilerParams(dimension_semantics=("parallel", "parallel")),
    )(projp, conv_w)


def _conv_bwd(projp, conv_w, dc, n_batch, seq):
    width = dc.shape[1]
    ncol = width // CONV_W
    first_x = P_CONV // CONV_W

    def body(x_ref, w_ref, dc_ref, dx_ref, dw_ref):
        b = pl.program_id(1)
        w = w_ref[...]
        pre, shifted = _conv_taps(x_ref[...], w)
        sg = jax.nn.sigmoid(pre)
        dpre = dc_ref[...] * (sg * (1.0 + pre * (1.0 - sg)))
        row = lax.broadcasted_iota(jnp.int32, dpre.shape, 0)
        dx = dpre * w[3:4]
        for s in (1, 2, 3):
            dx = dx + jnp.where(row < seq - s, pltpu.roll(dpre, seq - s, 0), 0.0) * w[3 - s:4 - s]
        dx_ref[...] = dx.astype(dx_ref.dtype)
        for s in (0, 1, 2, 3):
            part = jnp.sum(dpre * shifted[s], axis=0, keepdims=True)

            @pl.when(b == 0)
            def _():
                dw_ref[3 - s:4 - s, :] = part

            @pl.when(b > 0)
            def _():
                dw_ref[3 - s:4 - s, :] += part

    return pl.pallas_call(
        body,
        name="conv_bwd",
        grid=(ncol, n_batch),
        in_specs=[
            pl.BlockSpec((seq, CONV_W), lambda j, b: (b, first_x + j)),
            pl.BlockSpec((4, CONV_W), lambda j, b: (0, j)),
            pl.BlockSpec((seq, CONV_W), lambda j, b: (b, j)),
        ],
        out_specs=[pl.BlockSpec((seq, CONV_W), lambda j, b: (b, j)), pl.BlockSpec((4, CONV_W), lambda j, b: (0, j))],
        out_shape=[jax.ShapeDtypeStruct((n_batch * seq, width), BF16), jax.ShapeDtypeStruct((4, width), F32)],
        compiler_params=pltpu.CompilerParams(dimension_semantics=("arbitrary", "arbitrary")),
    )(projp, conv_w, dc)


def _attn_chunk(qc, kb, vb, bias2, valid, lane_lo):
    out = None
    for e in (0, 1):
        sel = lane_lo if e == 0 else jnp.logical_not(lane_lo)
        qm = jnp.where(sel, qc, 0.0) * (A_DIM ** -0.5)
        s = lax.dot_general(qm.astype(BF16), kb.astype(BF16), NT, preferred_element_type=F32) + bias2[e]
        s = jnp.where(valid, s, -1e30)
        p = jnp.exp(s - lax.stop_gradient(jnp.max(s, axis=-1, keepdims=True)))
        p = p / jnp.sum(p, axis=-1, keepdims=True)
        oe = jnp.where(sel, jnp.dot(p.astype(BF16), vb.astype(BF16), preferred_element_type=F32), 0.0)
        out = oe if out is None else out + oe
    return out


def _attn_masks(c):
    col = lax.broadcasted_iota(jnp.int32, (CHUNK, A_BAND), 1)
    valid = col + c * CHUNK >= A_PAD
    lane_lo = lax.broadcasted_iota(jnp.int32, (1, LANE), 1) < A_DIM
    return valid, lane_lo


def _attn_specs(seq):
    def blk(first):
        return pl.BlockSpec((seq, LANE), lambda hp, b: (b, first + hp))

    return blk, pl.BlockSpec((2, CHUNK, A_BAND), lambda hp, b: (hp, 0, 0))


def _attn_fwd(projp, bias, n_batch, seq):
    nc = seq // CHUNK
    blk, bias_spec = _attn_specs(seq)

    def body(q_ref, k_ref, v_ref, b_ref, o_ref, kp_ref, vp_ref):
        kp_ref[0:A_PAD, :] = jnp.zeros((A_PAD, LANE), F32)
        vp_ref[0:A_PAD, :] = jnp.zeros((A_PAD, LANE), F32)
        kp_ref[A_PAD:, :] = k_ref[...]
        vp_ref[A_PAD:, :] = v_ref[...]
        bias2 = b_ref[...]

        def step(c, carry):
            r0 = pl.multiple_of(c * CHUNK, CHUNK)
            valid, lane_lo = _attn_masks(c)
            out = _attn_chunk(q_ref[pl.ds(r0, CHUNK), :], kp_ref[pl.ds(r0, A_BAND), :], vp_ref[pl.ds(r0, A_BAND), :],
                              bias2, valid, lane_lo)
            o_ref[pl.ds(r0, CHUNK), :] = out.astype(o_ref.dtype)
            return carry

        lax.fori_loop(0, nc, step, 0)

    return pl.pallas_call(
        body,
        name="attn_fwd",
        grid=(A_HEADS // 2, n_batch),
        in_specs=[blk(0), blk(4), blk(8), bias_spec],
        out_specs=pl.BlockSpec((seq, LANE), lambda hp, b: (b, hp)),
        out_shape=jax.ShapeDtypeStruct((n_batch * seq, A_WIDTH), BF16),
        scratch_shapes=[pltpu.VMEM((A_PAD + seq, LANE), F32), pltpu.VMEM((A_PAD + seq, LANE), F32)],
        compiler_params=pltpu.CompilerParams(dimension_semantics=("parallel", "parallel")),
    )(projp, projp, projp, bias)


def _attn_bwd(projp, bias, dy, n_batch, seq):
    nc = seq // CHUNK
    blk, bias_spec = _attn_specs(seq)
    out_blk = pl.BlockSpec((seq, LANE), lambda hp, b: (b, hp))

    def body(q_ref, k_ref, v_ref, b_ref, dy_ref, dq_ref, dk_ref, dv_ref, db_ref, kp_ref, vp_ref, dkp_ref, dvp_ref):
        b = pl.program_id(1)
        kp_ref[0:A_PAD, :] = jnp.zeros((A_PAD, LANE), F32)
        vp_ref[0:A_PAD, :] = jnp.zeros((A_PAD, LANE), F32)
        kp_ref[A_PAD:, :] = k_ref[...]
        vp_ref[A_PAD:, :] = v_ref[...]
        dkp_ref[...] = jnp.zeros_like(dkp_ref)
        dvp_ref[...] = jnp.zeros_like(dvp_ref)
        bias2 = b_ref[...]

        @pl.when(b == 0)
        def _():
            db_ref[...] = jnp.zeros_like(db_ref)

        def step(c, carry):
            r0 = pl.multiple_of(c * CHUNK, CHUNK)
            valid, lane_lo = _attn_masks(c)
            _, vjp = jax.vjp(lambda q, k, v, bb: _attn_chunk(q, k, v, bb, valid, lane_lo), q_ref[pl.ds(r0, CHUNK), :],
                             kp_ref[pl.ds(r0, A_BAND), :], vp_ref[pl.ds(r0, A_BAND), :], bias2)
            dq, dk, dv, dbias = vjp(dy_ref[pl.ds(r0, CHUNK), :])
            dq_ref[pl.ds(r0, CHUNK), :] = dq.astype(dq_ref.dtype)
            dkp_ref[pl.ds(r0, A_BAND), :] += dk
            dvp_ref[pl.ds(r0, A_BAND), :] += dv
            db_ref[...] += dbias
            return carry

        lax.fori_loop(0, nc, step, 0)
        dk_ref[...] = dkp_ref[A_PAD:, :].astype(dk_ref.dtype)
        dv_ref[...] = dvp_ref[A_PAD:, :].astype(dv_ref.dtype)

    n_tok = n_batch * seq
    pad = pltpu.VMEM((A_PAD + seq, LANE), F32)
    return pl.pallas_call(
        body,
        name="attn_bwd",
        grid=(A_HEADS // 2, n_batch),
        in_specs=[blk(0), blk(4), blk(8), bias_spec, out_blk],
        out_specs=[out_blk, out_blk, out_blk, bias_spec],
        out_shape=[jax.ShapeDtypeStruct((n_tok, A_WIDTH), BF16)] * 3 + [jax.ShapeDtypeStruct((A_HEADS, CHUNK, A_BAND), F32)],
        scratch_shapes=[pad, pad, pad, pad],
        compiler_params=pltpu.CompilerParams(dimension_semantics=("arbitrary", "arbitrary")),
    )(projp, projp, projp, bias, dy)


def _rel_bias_table(rel_bias):
    span = CHUNK + A_BAND - 1
    near = REL_CLIP + CHUNK
    far = jnp.broadcast_to(rel_bias[:, 2 * REL_CLIP:], (A_HEADS, span - near))
    t = jnp.concatenate([rel_bias[:, 2 * REL_CLIP + 1 - near:], far], axis=1)
    u = jnp.concatenate([t[:, :A_BAND][:, ::-1], t[:, A_BAND:][:, ::-1]], axis=1)
    rolled = jnp.tile(u, (1, CHUNK))[:, :CHUNK * (span - 1)].reshape(A_HEADS, CHUNK, span - 1)
    return rolled[:, :, :A_BAND]


def _dot(a, b, dn=NN):
    return lax.dot_general(a, b, dn, precision=DELTA_PREC, preferred_element_type=F32)


def _each(fn, *lists):
    return [fn(*vals) for vals in zip(*lists)]


def _delta_chunk(r_state, cq, ck, cv, beta, g):
    ii = lax.broadcasted_iota(jnp.int32, (CHUNK, CHUNK), 0)
    jj = lax.broadcasted_iota(jnp.int32, (CHUNK, CHUNK), 1)
    incl, strict, eye = ii >= jj, ii > jj, ii == jj
    q = _each(lambda t: t * lax.rsqrt(jnp.sum(t * t, axis=-1, keepdims=True) + EPS) * (B_DIM ** -0.5), cq)
    k = _each(lambda t: t * lax.rsqrt(jnp.sum(t * t, axis=-1, keepdims=True) + EPS), ck)
    g_b = _each(lambda t: jnp.broadcast_to(t, (CHUNK, CHUNK)), g)
    g_row = _each(lambda t: jnp.sum(jnp.where(eye, t, 0.0), axis=0, keepdims=True), g_b)
    gc_col = _each(lambda t: jnp.sum(jnp.where(incl, t, 0.0), axis=1, keepdims=True), g_row)
    gc_row = _each(lambda t: jnp.sum(jnp.where(ii <= jj, t, 0.0), axis=0, keepdims=True), g_b)
    decay = _each(lambda c, r: jnp.where(incl, jnp.exp(jnp.where(incl, c - r, 0.0)), 0.0), gc_col, gc_row)
    kk = _each(lambda t: _dot(t, t, NT), k)
    x = _each(lambda b, m, d: jnp.where(strict, -(b * m * d), 0.0), beta, kk, decay)
    inv = _each(lambda t: jnp.where(eye, 1.0, 0.0) + t, x)
    pw = x
    for _ in range(5):
        pw = _each(lambda t: _dot(t, t), pw)
        inv = _each(lambda t, s: t + _dot(t, s), inv, pw)
    egc = _each(jnp.exp, gc_col)
    u = _each(lambda t, b, v: _dot(t, b * v), inv, beta, cv)
    wk = _each(lambda t, b, e, kh: _dot(t, (b * e) * kh), inv, beta, egc, k)
    pqk = _each(lambda qh, kh, d: _dot(qh, kh, NT) * d, q, k, decay)
    g_last = _each(lambda c: c[CHUNK - 1:CHUNK, :], gc_col)
    kdec = _each(lambda kh, gl, c: kh * jnp.exp(gl - c), k, g_last, gc_col)
    w = _each(lambda uh, wkh, r: uh - _dot(wkh, r), u, wk, r_state)
    o = _each(lambda e, qh, r, ph, wh: e * _dot(qh, r) + _dot(ph, wh), egc, q, r_state, pqk, w)
    r_new = _each(lambda gl, r, kd, wh: jnp.exp(gl) * r + _dot(kd, wh, TN), g_last, r_state, kdec, w)
    return o, r_new


DELTA_BLK = 512


def _delta_blocks(n_batch, seq):
    nblk = seq // DELTA_BLK
    cpb = DELTA_BLK // CHUNK

    def rows(width, order):
        return pl.BlockSpec((DELTA_BLK, width), lambda b, i: (b * nblk + order(i), 0))

    def states(order):
        return pl.BlockSpec((cpb, B_HEADS, B_DIM, B_DIM), lambda b, i: (b * nblk + order(i), 0, 0, 0))

    return nblk, cpb, rows, states


def _head_cols(h):
    return [pl.ds(part * B_HEADS * B_DIM + h * B_DIM, B_DIM) for part in range(3)]


def _load_heads(c_ref, bg_ref, state_ref, rows):
    bg_c = bg_ref[rows, :]
    cols = [_head_cols(h) for h in range(B_HEADS)]
    return ([state_ref[h] for h in range(B_HEADS)], [c_ref[rows, c[0]] for c in cols], [c_ref[rows, c[1]] for c in cols],
            [c_ref[rows, c[2]] for c in cols], [bg_c[:, h:h + 1] for h in range(B_HEADS)],
            [bg_c[:, B_HEADS + h:B_HEADS + h + 1] for h in range(B_HEADS)])


def _delta_fwd(conv, bg, n_batch, seq):
    nblk, cpb, rows_spec, states_spec = _delta_blocks(n_batch, seq)

    def forward(i):
        return i

    def body(c_ref, bg_ref, o_ref, st_ref, r_ref):
        @pl.when(pl.program_id(1) == 0)
        def _():
            r_ref[...] = jnp.zeros_like(r_ref)

        def step(c, carry):
            rows = pl.ds(pl.multiple_of(c * CHUNK, CHUNK), CHUNK)
            args = _load_heads(c_ref, bg_ref, r_ref, rows)
            o, r_new = _delta_chunk(*args)
            for h in range(B_HEADS):
                st_ref[c, h] = args[0][h]
                o_ref[rows, pl.ds(h * B_DIM, B_DIM)] = o[h]
            for h in range(B_HEADS):
                r_ref[h] = r_new[h]
            return carry

        lax.fori_loop(0, cpb, step, 0)

    n_tok = n_batch * seq
    return pl.pallas_call(
        body,
        name="delta_fwd",
        grid=(n_batch, nblk),
        in_specs=[rows_spec(B_CONV, forward), rows_spec(LANE, forward)],
        out_specs=[rows_spec(B_HEADS * B_DIM, forward), states_spec(forward)],
        out_shape=[jax.ShapeDtypeStruct((n_tok, B_HEADS * B_DIM), F32),
                   jax.ShapeDtypeStruct((n_tok // CHUNK, B_HEADS, B_DIM, B_DIM), F32)],
        scratch_shapes=[pltpu.VMEM((B_HEADS, B_DIM, B_DIM), F32)],
        compiler_params=pltpu.CompilerParams(dimension_semantics=("arbitrary", "arbitrary")),
    )(conv, bg)


def _delta_bwd(conv, bg, states, do, n_batch, seq):
    nblk, cpb, rows_spec, states_spec = _delta_blocks(n_batch, seq)

    def backward(i):
        return nblk - 1 - i

    def body(c_ref, bg_ref, st_ref, do_ref, dc_ref, dbg_ref, dr_ref):
        @pl.when(pl.program_id(1) == 0)
        def _():
            dr_ref[...] = jnp.zeros_like(dr_ref)

        def step(n, carry):
            c = cpb - 1 - n
            rows = pl.ds(pl.multiple_of(c * CHUNK, CHUNK), CHUNK)
            _, vjp = jax.vjp(_delta_chunk, *_load_heads(c_ref, bg_ref, st_ref.at[c], rows))
            do = [do_ref[rows, pl.ds(h * B_DIM, B_DIM)] for h in range(B_HEADS)]
            dr, dq, dk, dv, dbeta, dg = vjp((do, [dr_ref[h] for h in range(B_HEADS)]))
            lane = lax.broadcasted_iota(jnp.int32, (CHUNK, LANE), 1)
            dbg = jnp.zeros((CHUNK, LANE), F32)
            for h in range(B_HEADS):
                cq, ck, cv = _head_cols(h)
                dr_ref[h] = dr[h]
                dc_ref[rows, cq] = dq[h]
                dc_ref[rows, ck] = dk[h]
                dc_ref[rows, cv] = dv[h]
                dbg = dbg + jnp.where(lane == h, dbeta[h], 0.0) + jnp.where(lane == h + B_HEADS, dg[h], 0.0)
            dbg_ref[rows, :] = dbg
            return carry

        lax.fori_loop(0, cpb, step, 0)

    n_tok = n_batch * seq
    return pl.pallas_call(
        body,
        name="delta_bwd",
        grid=(n_batch, nblk),
        in_specs=[rows_spec(B_CONV, backward), rows_spec(LANE, backward), states_spec(backward),
                  rows_spec(B_HEADS * B_DIM, backward)],
        out_specs=[rows_spec(B_CONV, backward), rows_spec(LANE, backward)],
        out_shape=[jax.ShapeDtypeStruct((n_tok, B_CONV), F32), jax.ShapeDtypeStruct((n_tok, LANE), F32)],
        scratch_shapes=[pltpu.VMEM((B_HEADS, B_DIM, B_DIM), F32)],
        compiler_params=pltpu.CompilerParams(dimension_semantics=("arbitrary", "arbitrary")),
    )(conv, bg, states, do)


def _lane_row(vec4, first):
    return jnp.concatenate([jnp.zeros((1, first), F32), vec4.reshape(1, B_HEADS).astype(F32),
                            jnp.zeros((1, LANE - first - B_HEADS), F32)], axis=1)


def _local_step(x3d, p3d, tgt3d, w, small):
    n_batch, seq, _ = x3d.shape
    n_tok = n_batch * seq
    x, p, tgt = x3d.reshape(n_tok, D), p3d.reshape(n_tok, -1), tgt3d.reshape(n_tok, D)
    g_mix, g_ffn, g_ple, g_final = (small[k].reshape(1, D) for k in ("g_mix", "g_ffn", "g_ple", "g_final"))
    w_onorm = small["w_onorm"].reshape(1, B_DIM)
    al_row = _lane_row(small["a_log"], B_HEADS)
    dtb_row = _lane_row(small["dt_bias"], B_HEADS)
    rel_bias = small["rel_bias"].reshape(A_HEADS, -1)
    bias = _rel_bias_table(rel_bias)
    conv_w = small["conv_w"].reshape(4, B_CONV)
    bd_blk = P_BD // LANE

    h1 = _rms_fwd(x, g_mix, name="rms_mix")
    projp = _mm(h1, w["w_in"], name="mm_proj", tn=640)
    y_a = _attn_fwd(projp, bias, n_batch, seq)
    conv = _conv_fwd(projp, conv_w, n_batch, seq)
    (bg,) = _rowwise(lambda raw, al, dtb: ([_gate_scalars(raw, al, dtb)], []), [(projp, LANE, bd_blk, 0)],
                     [_full(al_row), _full(dtb_row)], [(LANE, F32, LANE, 0, 0)], name="gate_scalars", tr=1024)
    o_b, states = _delta_fwd(conv, bg, n_batch, seq)
    (y_b,) = _rowwise(lambda o, z, wn: ([_gated_norm(o, z, wn)], []), [(o_b, LANE, 0, 1), (projp, LANE, P_Z // LANE, 1)],
                      [_full(w_onorm)], [(B_HEADS * B_DIM, BF16, LANE, 0, 1)], name="gated_norm", tr=1024, ncol=B_HEADS)
    t_a = _mm(y_a, w["w_branch_a"], name="mm_branch_a", tn=1024)
    t_b = _mm(y_b, w["w_branch_b"], name="mm_branch_b", tn=1024)
    half = D // 2
    gate_rows = [(projp, half, P_GATE // half, 1), (projp, half, P_GATE // half + 2, 1), (t_a, half, 0, 1), (t_b, half, 0, 1)]
    (merged,) = _rowwise(lambda ga, gb, ta, tb: ([_merge(ga, gb, ta, tb)], []), gate_rows, [], [(D, BF16, half, 0, 1)],
                         name="merge", tr=512, ncol=2)
    x1 = _mm(merged, w["w_out"], add=x, name="mm_out", tn=1024)
    h2 = _rms_fwd(x1, g_ffn, name="rms_ffn")
    gu = _mm(h2, w["w_gate_up"], name="mm_gate_up", tn=512)
    (act,) = _rowwise(lambda gub: ([_swiglu(gub)], []), [_full(gu)], [], [(D_FF, BF16, D_FF, 0, 0)], name="swiglu", tr=256)
    x2 = _mm(act, w["w_down"], add=x1, name="mm_down", tn=1024, tk=1408)
    h3 = _rms_fwd(x2, g_ple, name="rms_ple")
    pg = _mm(h3, w["w_ple_gate"], name="mm_ple_gate", tn=1024)
    pp = _mm(p, w["w_ple_proj"], name="mm_ple_proj", tn=1024)

    def head_fn(x2b, pgb, ppb, tb, gb):
        loss, (dx2, dpg, dpp, dg) = jax.value_and_grad(_head_loss, argnums=(0, 1, 2, 4))(x2b, pgb, ppb, tb, gb)
        return [dx2, dpg, dpp], [dg, jnp.full((1, LANE), loss, F32)]

    dx3, dpg, dpp, dg_final, loss_row = _rowwise(
        head_fn, [_full(x2), _full(pg), _full(pp), _full(tgt)], [_full(g_final)],
        [(D, F32, D, 0, 0), (D, BF16, D, 0, 0), (D, BF16, D, 0, 0)], [(D, D, 0), (LANE, LANE, 0)], name="loss_head", tr=256)
    gw = {}
    gw["w_ple_proj"] = _mm(p, dpp, ta=True, out_dtype=BF16, name="mm_d_ple_proj", tm=256, tn=1024)
    gw["w_ple_gate"] = _mm(h3, dpg, ta=True, out_dtype=BF16, name="mm_d_ple_gate", tn=1024)
    dh3 = _mm(dpg, w["w_ple_gate"].T, name="mm_dh3", tn=1024)
    dx2, dg_ple = _rms_bwd(x2, g_ple, dh3, dx3, name="rms_ple_bwd")
    gw["w_down"] = _mm(act, dx2, ta=True, out_dtype=BF16, name="mm_d_down", tm=256, tn=1024)
    dact = _mm(dx2, w["w_down"].T, name="mm_dact", tn=1408)

    def swiglu_bwd(gub, dab):
        _, vjp = jax.vjp(_swiglu, gub)
        return [vjp(dab)[0]], []

    (dgu,) = _rowwise(swiglu_bwd, [_full(gu), _full(dact)], [], [(2 * D_FF, BF16, 2 * D_FF, 0, 0)], name="swiglu_bwd", tr=256)
    gw["w_gate_up"] = _mm(h2, dgu, ta=True, out_dtype=BF16, name="mm_d_gate_up", tn=512)
    dh2 = _mm(dgu, w["w_gate_up"].T, name="mm_dh2", tn=1024, tk=1408)
    dx1, dg_ffn = _rms_bwd(x1, g_ffn, dh2, dx2, name="rms_ffn_bwd")
    gw["w_out"] = _mm(merged, dx1, ta=True, out_dtype=BF16, name="mm_d_out", tn=1024)
    dmerged = _mm(dx1, w["w_out"].T, name="mm_dmerged", tn=1024)

    def merge_bwd(ga, gb, ta, tb, dm):
        _, vjp = jax.vjp(_merge, ga, gb, ta, tb)
        return list(vjp(dm)), []

    dga, dgb, dta, dtb = _rowwise(merge_bwd, gate_rows + [(dmerged, half, 0, 1)], [], [(D, BF16, half, 0, 1)] * 4,
                                  name="merge_bwd", tr=512, ncol=2)
    gw["w_branch_a"] = _mm(y_a, dta, ta=True, out_dtype=BF16, name="mm_d_branch_a", tn=1024)
    gw["w_branch_b"] = _mm(y_b, dtb, ta=True, out_dtype=BF16, name="mm_d_branch_b", tn=1024)
    dya = _mm(dta, w["w_branch_a"].T, name="mm_dya", tn=512)
    dyb = _mm(dtb, w["w_branch_b"].T, name="mm_dyb", tn=512)

    def gated_norm_bwd(o, z, dy, wn):
        _, vjp = jax.vjp(_gated_norm, o, z, wn)
        do, dz, dwn = vjp(dy)
        return [do, dz], [dwn]

    do_b, dz, dw_onorm = _rowwise(
        gated_norm_bwd, [(o_b, LANE, 0, 1), (projp, LANE, P_Z // LANE, 1), (dyb, LANE, 0, 1)], [_full(w_onorm)],
        [(B_HEADS * B_DIM, F32, LANE, 0, 1), (B_HEADS * B_DIM, BF16, LANE, 0, 1)], [(B_DIM, B_DIM, 0)],
        name="gated_norm_bwd", tr=1024, ncol=B_HEADS)
    dconv_out, dbg = _delta_bwd(conv, bg, states, do_b, n_batch, seq)

    def gate_scalars_bwd(raw, dbgb, al, dtb):
        _, vjp = jax.vjp(_gate_scalars, raw, al, dtb)
        draw, dal, ddtb = vjp(dbgb)
        return [draw], [dal, ddtb]

    dbd, dal_row, ddtb_row = _rowwise(gate_scalars_bwd, [(projp, LANE, bd_blk, 0), _full(dbg)], [_full(al_row), _full(dtb_row)],
                                      [(LANE, BF16, LANE, 0, 0)], [(LANE, LANE, 0), (LANE, LANE, 0)], name="gate_scalars_bwd",
                                      tr=1024)
    dconv, dconv_w = _conv_bwd(projp, conv_w, dconv_out, n_batch, seq)
    dq_a, dk_a, dv_a, dbias = _attn_bwd(projp, bias, dya, n_batch, seq)
    dprojp = jnp.concatenate([dq_a, dk_a, dv_a, dconv, dz, dga, dgb, dbd], axis=1)
    gw["w_in"] = _mm(h1, dprojp, ta=True, out_dtype=BF16, name="mm_d_in", tn=640)
    dh1 = _mm(dprojp, w["w_in"].T, name="mm_dh1", tn=1024, tk=1152)
    grad_x, dg_mix = _rms_bwd(x, g_mix, dh1, dx1, name="rms_mix_bwd")

    _, bias_vjp = jax.vjp(_rel_bias_table, rel_bias)
    gs = {
        "g_mix": dg_mix, "g_ffn": dg_ffn, "g_ple": dg_ple, "g_final": dg_final, "w_onorm": dw_onorm,
        "conv_w": dconv_w, "rel_bias": bias_vjp(dbias)[0],
        "a_log": dal_row[0, B_HEADS:2 * B_HEADS], "dt_bias": ddtb_row[0, B_HEADS:2 * B_HEADS],
    }
    return loss_row[:, :1], grad_x.reshape(n_batch, seq, D), gw, gs


MATRICES = (
    ("w_in", (D, D_IN), 1), ("w_gate_up", (D, 2 * D_FF), 1), ("w_branch_a", (A_WIDTH, D), 1), ("w_branch_b", (A_WIDTH, D), 1),
    ("w_out", (D, D), 0), ("w_down", (D_FF, D), 0), ("w_ple_gate", (D, D), 0), ("w_ple_proj", (256, D), 1),
)
TAPS_PER_SHARD = B_CONV // N_DEV


def _from_gathered(slabs, axis):
    _, rows, cols = slabs.shape
    if axis == 0:
        return slabs.reshape(N_DEV * rows, cols)
    return jnp.transpose(slabs, (1, 0, 2)).reshape(rows, N_DEV * cols)


def _to_owner(full, axis):
    rows, cols = full.shape
    if axis == 0:
        return full.reshape(N_DEV, rows // N_DEV, cols)
    return jnp.transpose(full.reshape(rows, N_DEV, cols // N_DEV), (1, 0, 2))


def _permute_w_in(w_in):
    zeros = jnp.zeros((D, P_END - D_IN), w_in.dtype)
    return jnp.concatenate([w_in[:, :P_GATE], w_in[:, P_GATE + 8:], w_in[:, P_GATE:P_GATE + 8], zeros], axis=1)


def _unpermute_w_in(gp):
    return jnp.concatenate([gp[:, :P_GATE], gp[:, P_BD:P_BD + 8], gp[:, P_GATE:P_BD]], axis=1)


SMALL_ROWS = 16
SMALL_LAYOUT = (("g_mix", 0, D), ("g_ffn", 1, D), ("g_ple", 2, D), ("g_final", 3, D), ("conv_w", 4, 4 * B_CONV),
                ("rel_bias", 10, A_HEADS * (2 * REL_CLIP + 1)), ("w_onorm", 13, B_DIM), ("a_log", 14, B_HEADS),
                ("dt_bias", 14, B_HEADS), ("loss", 15, 1))


def _pack_small(gs):
    rows = {}
    for name, row, n in SMALL_LAYOUT:
        rows.setdefault(row, []).append(gs[name].reshape(-1).astype(F32))
    parts = []
    for row in sorted(rows):
        flat = jnp.concatenate(rows[row])
        parts.append(jnp.concatenate([flat, jnp.zeros((-flat.shape[0] % D,), F32)]))
    flat = jnp.concatenate(parts)
    assert flat.shape[0] == SMALL_ROWS * D, flat.shape
    return flat.reshape(SMALL_ROWS, D)


def _unpack_small(blk):
    flat, out, used = blk.reshape(-1), {}, {}
    for name, row, n in SMALL_LAYOUT:
        start = row * D + used.get(row, 0)
        out[name] = flat[start:start + n]
        used[row] = used.get(row, 0) + n
    return out


def _position():
    return lax.axis_index("x"), lax.axis_index("y"), lax.axis_index("c")


PEERS = N_DEV - 1


def _comm_call(body, arrays, out_shapes, *, name):
    n = len(arrays)
    return pl.pallas_call(
        body,
        name=name,
        out_shape=out_shapes,
        in_specs=[HBM_SPEC] * n,
        out_specs=[HBM_SPEC] * n,
        scratch_shapes=[pltpu.SemaphoreType.DMA((PEERS * n,)), pltpu.SemaphoreType.DMA((PEERS * n,)),
                        pltpu.SemaphoreType.DMA((n,))],
    )(*arrays)


def _weights_allgather(shards):
    n = len(shards)

    def body(*refs):
        ins, outs = refs[:n], refs[n:2 * n]
        send_sems, recv_sems, local_sems = refs[2 * n:]
        x, y, c = _position()
        me, sibling = (x, y, c), (x, y, 1 - c)
        chips = [(1 - x, y), (x, 1 - y), (1 - x, 1 - y)]

        def slab(a, px, py, pc):
            return outs[a].at[4 * px + 2 * py + pc]

        def copy(a, k, block, to, src=None):
            return pltpu.make_async_remote_copy(src_ref=slab(a, *block) if src is None else src, dst_ref=slab(a, *block),
                                                send_sem=send_sems.at[PEERS * a + k], recv_sem=recv_sems.at[PEERS * a + k],
                                                device_id=to, device_id_type=MESH)

        local = [pltpu.make_async_copy(ins[a], slab(a, *me), local_sems.at[a]) for a in range(n)]
        sent = [copy(a, 1 + j, me, (*chip, c), src=ins[a]) for a in range(n) for j, chip in enumerate(chips)]
        sent += [copy(a, 0, me, sibling, src=ins[a]) for a in range(n)]
        for cp in sent + local:
            cp.start()
        for a in range(n):
            for j, chip in enumerate(chips):
                copy(a, 1 + j, (*chip, c), me).wait_recv()
                passed = copy(a, 4 + j, (*chip, c), sibling)
                passed.start()
                sent.append(passed)
        for a in range(n):
            copy(a, 0, sibling, me).wait_recv()
            for j, chip in enumerate(chips):
                copy(a, 4 + j, (*chip, 1 - c), me).wait_recv()
        for cp in sent:
            cp.wait_send()
        for cp in local:
            cp.wait()

    return _comm_call(body, shards, [jax.ShapeDtypeStruct((N_DEV,) + s.shape, s.dtype) for s in shards],
                      name="weights_allgather")


def _grads_exchange(by_owner):
    n = len(by_owner)

    def body(*refs):
        ins, outs = refs[:n], refs[n:2 * n]
        send_sems, recv_sems, local_sems = refs[2 * n:]
        x, y, c = _position()
        mine = 4 * x + 2 * y + c
        local = [pltpu.make_async_copy(ins[a].at[mine], outs[a].at[mine], local_sems.at[a]) for a in range(n)]
        for cp in local:
            cp.start()
        flips = [(dx, dy, dc) for dx in (0, 1) for dy in (0, 1) for dc in (0, 1) if dx + dy + dc]
        pending = []
        for k, (dx, dy, dc) in enumerate(flips):
            px, py, pc = (1 - x if dx else x), (1 - y if dy else y), (1 - c if dc else c)
            peer = 4 * px + 2 * py + pc
            for a in range(n):
                def remote(slot):
                    return pltpu.make_async_remote_copy(src_ref=ins[a].at[peer], dst_ref=outs[a].at[slot],
                                                        send_sem=send_sems.at[PEERS * a + k], recv_sem=recv_sems.at[PEERS * a + k],
                                                        device_id=(px, py, pc), device_id_type=MESH)

                sent = remote(mine)
                sent.start()
                pending.append((sent, remote(peer)))
        for sent, landed in pending:
            landed.wait_recv()
            sent.wait_send()
        for cp in local:
            cp.wait()

    return _comm_call(body, by_owner, [jax.ShapeDtypeStruct(g.shape, g.dtype) for g in by_owner], name="grads_exchange")


def _sum_slots(got, *, name, tr):
    _, rows, cols = got.shape
    tr = _tile(rows, tr, 16)

    def body(g_ref, o_ref):
        acc = g_ref[0].astype(F32)
        for j in range(1, N_DEV):
            acc = acc + g_ref[j].astype(F32)
        o_ref[...] = acc

    return pl.pallas_call(
        body,
        name=name,
        grid=(rows // tr,),
        in_specs=[pl.BlockSpec((N_DEV, tr, cols), lambda i: (0, i, 0))],
        out_specs=pl.BlockSpec((tr, cols), lambda i: (i, 0)),
        out_shape=jax.ShapeDtypeStruct((rows, cols), F32),
        compiler_params=pltpu.CompilerParams(dimension_semantics=("parallel",)),
    )(got)


def _adamw(wt, g, m, v, *, name, slots=False):
    shape = wt.shape
    two_d = (-1, shape[-1]) if wt.ndim > 1 else (1, -1)
    args = [a.reshape(two_d) for a in (wt, m, v)]
    rows, cols = args[0].shape
    tr = _tile(rows, 256, 16) if rows % 16 == 0 else rows
    args.insert(1, g.reshape((N_DEV, rows, cols) if slots else (rows, cols)))

    def body(w_ref, g_ref, m_ref, v_ref, go_ref, d_ref, nm_ref, nv_ref):
        if slots:
            gv = g_ref[0].astype(F32)
            for j in range(1, N_DEV):
                gv = gv + g_ref[j].astype(F32)
        else:
            gv = g_ref[...]
        go_ref[...] = gv
        m2 = ADAM_B1 * m_ref[...] + (1.0 - ADAM_B1) * gv
        v2 = ADAM_B2 * v_ref[...] + (1.0 - ADAM_B2) * (gv * gv)
        m_hat = m2 / (1.0 - ADAM_B1 ** ADAM_STEP)
        v_hat = v2 / (1.0 - ADAM_B2 ** ADAM_STEP)
        d_ref[...] = -ADAM_LR * (m_hat / (jnp.sqrt(v_hat) + ADAM_EPS) + ADAM_WD * w_ref[...])
        nm_ref[...] = m2
        nv_ref[...] = v2

    spec = pl.BlockSpec((tr, cols), lambda i: (i, 0))
    g_spec = pl.BlockSpec((N_DEV, tr, cols), lambda i: (0, i, 0)) if slots else spec
    outs = pl.pallas_call(
        body,
        name=name,
        grid=(rows // tr,),
        in_specs=[spec, g_spec, spec, spec],
        out_specs=[spec] * 4,
        out_shape=[jax.ShapeDtypeStruct((rows, cols), F32)] * 4,
        compiler_params=pltpu.CompilerParams(dimension_semantics=("parallel",)),
    )(*args)
    return tuple(o.reshape(shape) for o in outs)


WEIGHTS = ("g_mix", "w_in", "conv_w", "a_log", "dt_bias", "rel_bias", "w_onorm", "w_branch_a", "w_branch_b", "w_out", "g_ffn",
           "w_gate_up", "w_down", "g_ple", "w_ple_gate", "w_ple_proj", "g_final")


def kernel(x, p, g_mix, w_in, conv_w, a_log, dt_bias, rel_bias, w_onorm, w_branch_a, w_branch_b, w_out, g_ffn, w_gate_up, w_down, g_ple, w_ple_gate, w_ple_proj, g_final, loss_target, m_g_mix, m_w_in, m_conv_w, m_a_log, m_dt_bias, m_rel_bias, m_w_onorm, m_w_branch_a, m_w_branch_b, m_w_out, m_g_ffn, m_w_gate_up, m_w_down, m_g_ple, m_w_ple_gate, m_w_ple_proj, m_g_final, v_g_mix, v_w_in, v_conv_w, v_a_log, v_dt_bias, v_rel_bias, v_w_onorm, v_w_branch_a, v_w_branch_b, v_w_out, v_g_ffn, v_w_gate_up, v_w_down, v_g_ple, v_w_ple_gate, v_w_ple_proj, v_g_final):
    given = dict(g_mix=g_mix, w_in=w_in, conv_w=conv_w, a_log=a_log, dt_bias=dt_bias, rel_bias=rel_bias, w_onorm=w_onorm,
                 w_branch_a=w_branch_a, w_branch_b=w_branch_b, w_out=w_out, g_ffn=g_ffn, w_gate_up=w_gate_up, w_down=w_down,
                 g_ple=g_ple, w_ple_gate=w_ple_gate, w_ple_proj=w_ple_proj, g_final=g_final)
    mom1 = dict(g_mix=m_g_mix, w_in=m_w_in, conv_w=m_conv_w, a_log=m_a_log, dt_bias=m_dt_bias, rel_bias=m_rel_bias,
                w_onorm=m_w_onorm, w_branch_a=m_w_branch_a, w_branch_b=m_w_branch_b, w_out=m_w_out, g_ffn=m_g_ffn,
                w_gate_up=m_w_gate_up, w_down=m_w_down, g_ple=m_g_ple, w_ple_gate=m_w_ple_gate, w_ple_proj=m_w_ple_proj,
                g_final=m_g_final)
    mom2 = dict(g_mix=v_g_mix, w_in=v_w_in, conv_w=v_conv_w, a_log=v_a_log, dt_bias=v_dt_bias, rel_bias=v_rel_bias,
                w_onorm=v_w_onorm, w_branch_a=v_w_branch_a, w_branch_b=v_w_branch_b, w_out=v_w_out, g_ffn=v_g_ffn,
                w_gate_up=v_w_gate_up, w_down=v_w_down, g_ple=v_g_ple, w_ple_gate=v_w_ple_gate, w_ple_proj=v_w_ple_proj,
                g_final=v_g_final)
    mine = 4 * lax.axis_index("x") + 2 * lax.axis_index("y") + lax.axis_index("c")

    gathered = _weights_allgather([given[name][0].astype(BF16) for name, _, _ in MATRICES] + [conv_w[0]])
    full = {name: _from_gathered(slabs, axis) for (name, _, axis), slabs in zip(MATRICES, gathered)}
    full["w_in"] = _permute_w_in(full["w_in"])
    small = dict(g_mix=g_mix, g_ffn=g_ffn, g_ple=g_ple, g_final=g_final, w_onorm=w_onorm, a_log=a_log, dt_bias=dt_bias,
                 rel_bias=rel_bias, conv_w=_from_gathered(gathered[-1], 1))

    loss_part, grad_x, gw, gs = _local_step(x, p[0], loss_target, full, small)
    gw["w_in"] = _unpermute_w_in(gw["w_in"])
    gs["loss"] = loss_part

    by_owner = [_to_owner(gw[name], axis) for name, _, axis in MATRICES]
    got = _grads_exchange(by_owner + [jnp.broadcast_to(_pack_small(gs), (N_DEV, SMALL_ROWS, D))])
    grads = dict(zip([name for name, _, _ in MATRICES], got))
    small_sum = _unpack_small(_sum_slots(got[-1], name="sum_small_grads", tr=16))
    loss = small_sum.pop("loss")[0]
    conv_all = small_sum.pop("conv_w").reshape(4, N_DEV, TAPS_PER_SHARD)
    small_sum["conv_w"] = lax.dynamic_index_in_dim(conv_all, mine, axis=1, keepdims=False)

    out_g, out_d, out_m, out_v = [], [], [], []
    for name in WEIGHTS:
        if name in grads:
            g, delta, new_m, new_v = _adamw(given[name], grads[name], mom1[name], mom2[name], name=f"adamw_{name}", slots=True)
        else:
            g, delta, new_m, new_v = _adamw(given[name], small_sum[name].reshape(given[name].shape), mom1[name], mom2[name],
                                            name=f"adamw_{name}")
        out_g.append(g)
        out_d.append(delta)
        out_m.append(new_m)
        out_v.append(new_v)
    return (loss, grad_x, *out_g, *out_d, *out_m, *out_v)
```

```python
import jax
import jax.numpy as jnp
from jax import lax
from jax.experimental import pallas as pl
from jax.experimental.pallas import tpu as pltpu

F32 = jnp.float32
BF16 = jnp.bfloat16
DELTA_PREC = lax.Precision.HIGH
MESH = pl.DeviceIdType.MESH

N_DEV = 8
D = 1024
CHUNK = 64
EPS = 1e-6
A_HEADS, A_DIM, A_WIDTH = 8, 64, 512
A_BAND = 9 * CHUNK
A_PAD = 8 * CHUNK
REL_CLIP = 128
B_HEADS, B_DIM = 4, 128
B_CONV = 1536
D_FF = 2816
D_IN = 5640
P_CONV, P_Z, P_GATE, P_BD, P_END = 1536, 3072, 3584, 5632, 5760
LANE = 128

ADAM_LR, ADAM_B1, ADAM_B2, ADAM_EPS, ADAM_WD, ADAM_STEP = 0.001, 0.9, 0.999, 1e-08, 0.01, 10

NT = (((1,), (1,)), ((), ()))
TN = (((0,), (0,)), ((), ()))
NN = (((1,), (0,)), ((), ()))

HBM_SPEC = pl.BlockSpec(memory_space=pltpu.HBM)


def _tile(n, target, align=LANE):
    if n <= target:
        return n
    best = None
    for t in range(align, target + 1, align):
        if n % t == 0:
            best = t
    assert best is not None, (n, target, align)
    return best


def _mm(a, b, *, name, ta=False, add=None, out_dtype=F32, tm=1024, tn=640, tk=None):
    if ta:
        k_dim, m_dim = a.shape
    else:
        m_dim, k_dim = a.shape
    assert b.shape[0] == k_dim
    n_dim = b.shape[1]
    tm, tn = _tile(m_dim, tm), _tile(n_dim, tn)
    tk = _tile(k_dim, tk or (4096 if ta else 1024), 8 if ta else LANE)
    nk = k_dim // tk
    dn = TN if ta else NN

    def body(*refs):
        if add is None:
            a_ref, b_ref, o_ref = refs[:3]
            add_ref = None
        else:
            a_ref, b_ref, add_ref, o_ref = refs[:4]
        part = lax.dot_general(a_ref[...].astype(BF16), b_ref[...].astype(BF16), dn, preferred_element_type=F32)

        def finish(r):
            if add_ref is not None:
                r = r + add_ref[...]
            o_ref[...] = r.astype(o_ref.dtype)

        if nk == 1:
            finish(part)
        else:
            acc_ref = refs[-1]
            k = pl.program_id(2)

            @pl.when(k == 0)
            def _():
                acc_ref[...] = part

            @pl.when(k > 0)
            def _():
                acc_ref[...] += part

            @pl.when(k == nk - 1)
            def _():
                finish(acc_ref[...])

    a_spec = pl.BlockSpec((tk, tm), lambda i, j, k: (k, i)) if ta else pl.BlockSpec((tm, tk), lambda i, j, k: (i, k))
    in_specs = [a_spec, pl.BlockSpec((tk, tn), lambda i, j, k: (k, j))]
    args = [a, b]
    if add is not None:
        in_specs.append(pl.BlockSpec((tm, tn), lambda i, j, k: (i, j)))
        args.append(add)
    return pl.pallas_call(
        body,
        name=name,
        grid=(m_dim // tm, n_dim // tn, nk),
        in_specs=in_specs,
        out_specs=pl.BlockSpec((tm, tn), lambda i, j, k: (i, j)),
        out_shape=jax.ShapeDtypeStruct((m_dim, n_dim), out_dtype),
        scratch_shapes=[pltpu.VMEM((tm, tn), F32)] if nk > 1 else [],
        compiler_params=pltpu.CompilerParams(dimension_semantics=("parallel", "parallel", "arbitrary")),
    )(*args)


def _rowwise(fn, rows, bcs, outs, reds=(), *, name, tr, ncol=1):
    n_rows = rows[0][0].shape[0]
    tr = _tile(n_rows, tr, 8)
    nrow = n_rows // tr
    n_in, n_out = len(rows) + len(bcs), len(outs)

    def body(*refs):
        j, i = pl.program_id(0), pl.program_id(1)
        o_vals, r_vals = fn(*[r[...] for r in refs[:n_in]])
        for ref, val in zip(refs[n_in:n_in + n_out], o_vals):
            ref[...] = val.astype(ref.dtype)
        for ref, val, (_, _, stride) in zip(refs[n_in + n_out:], r_vals, reds):
            first = (i == 0) if stride else jnp.logical_and(i == 0, j == 0)

            @pl.when(first)
            def _():
                ref[...] = val

            @pl.when(jnp.logical_not(first))
            def _():
                ref[...] += val

    def spec(r, w, off, st, row_dep=True):
        if row_dep:
            return pl.BlockSpec((r, w), lambda j, i: (i, off + st * j))
        return pl.BlockSpec((r, w), lambda j, i: (0, off + st * j))

    in_specs = [spec(tr, w, off, st) for (_, w, off, st) in rows]
    in_specs += [spec(a.shape[0], w, off, st, False) for (a, w, off, st) in bcs]
    out_specs = [spec(tr, w, off, st) for (_, _, w, off, st) in outs]
    out_specs += [spec(1, w, 0, st, False) for (_, w, st) in reds]
    out_shape = [jax.ShapeDtypeStruct((n_rows, c), dt) for (c, dt, _, _, _) in outs]
    out_shape += [jax.ShapeDtypeStruct((1, c), F32) for (c, _, _) in reds]
    return pl.pallas_call(
        body,
        name=name,
        grid=(ncol, nrow),
        in_specs=in_specs,
        out_specs=out_specs,
        out_shape=out_shape,
        compiler_params=pltpu.CompilerParams(dimension_semantics=("arbitrary", "arbitrary")),
    )(*[r[0] for r in rows], *[b[0] for b in bcs])


def _full(a):
    return (a, a.shape[1], 0, 0)


def _rms(x, g):
    return x * lax.rsqrt(jnp.mean(x * x, axis=-1, keepdims=True) + EPS) * g


def _silu(x):
    return x * jax.nn.sigmoid(x)


def _softplus(x):
    return jnp.maximum(x, 0.0) + jnp.log(1.0 + jnp.exp(-jnp.abs(x)))


def _rms_fwd(x, g, *, name):
    (h,) = _rowwise(lambda xb, gb: ([_rms(xb, gb)], []), [_full(x)], [_full(g)], [(D, BF16, D, 0, 0)], name=name, tr=512)
    return h


def _rms_bwd(x, g, dh, dres, *, name):
    def fn(xb, dhb, dresb, gb):
        _, vjp = jax.vjp(_rms, xb, gb)
        dx, dg = vjp(dhb)
        return [dx + dresb], [dg]

    return _rowwise(fn, [_full(x), _full(dh), _full(dres)], [_full(g)], [(D, F32, D, 0, 0)], [(D, D, 0)], name=name, tr=256)


def _gate_scalars(raw, al_row, dtb_row):
    lane = lax.broadcasted_iota(jnp.int32, raw.shape, 1)
    beta = jax.nn.sigmoid(raw)
    g = -jnp.exp(al_row) * _softplus(raw + dtb_row)
    return jnp.where(lane < B_HEADS, beta, jnp.where(lane < 2 * B_HEADS, g, 0.0))


def _gated_norm(o, z, w):
    return _rms(o, w) * _silu(z)


def _merge(ga, gb, ta, tb):
    return jax.nn.sigmoid(ga) * ta + jax.nn.sigmoid(gb) * tb


def _swiglu(gu):
    return _silu(gu[:, :D_FF]) * gu[:, D_FF:]


def _head_loss(x2, pg, pp, tgt, g):
    x3 = x2 + jax.nn.sigmoid(pg) * pp
    err = _rms(x3, g) - tgt
    return 0.5 * jnp.sum(jnp.mean(err * err, axis=-1))


CONV_W = 256


def _conv_taps(x, w):
    row = lax.broadcasted_iota(jnp.int32, x.shape, 0)
    shifted = [x] + [jnp.where(row >= s, pltpu.roll(x, s, 0), 0.0) for s in (1, 2, 3)]
    pre = shifted[0] * w[3:4]
    for s in (1, 2, 3):
        pre = pre + shifted[s] * w[3 - s:4 - s]
    return pre, shifted


def _conv_fwd(projp, conv_w, n_batch, seq):
    ncol = B_CONV // CONV_W
    first = P_CONV // CONV_W

    def body(x_ref, w_ref, o_ref):
        pre, _ = _conv_taps(x_ref[...], w_ref[...])
        o_ref[...] = _silu(pre)

    return pl.pallas_call(
        body,
        name="conv_fwd",
        grid=(ncol, n_batch),
        in_specs=[pl.BlockSpec((seq, CONV_W), lambda j, b: (b, first + j)), pl.BlockSpec((4, CONV_W), lambda j, b: (0, j))],
        out_specs=pl.BlockSpec((seq, CONV_W), lambda j, b: (b, j)),
        out_shape=jax.ShapeDtypeStruct((n_batch * seq, B_CONV), F32),
        compiler_params=pltpu.CompilerParams(dimension_semantics=("parallel", "parallel")),
    )(projp, conv_w)


def _conv_bwd(projp, conv_w, dc, n_batch, seq):
    width = dc.shape[1]
    ncol = width // CONV_W
    first_x = P_CONV // CONV_W

    def body(x_ref, w_ref, dc_ref, dx_ref, dw_ref):
        b = pl.program_id(1)
        w = w_ref[...]
        pre, shifted = _conv_taps(x_ref[...], w)
        sg = jax.nn.sigmoid(pre)
        dpre = dc_ref[...] * (sg * (1.0 + pre * (1.0 - sg)))
        row = lax.broadcasted_iota(jnp.int32, dpre.shape, 0)
        dx = dpre * w[3:4]
        for s in (1, 2, 3):
            dx = dx + jnp.where(row < seq - s, pltpu.roll(dpre, seq - s, 0), 0.0) * w[3 - s:4 - s]
        dx_ref[...] = dx.astype(dx_ref.dtype)
        for s in (0, 1, 2, 3):
            part = jnp.sum(dpre * shifted[s], axis=0, keepdims=True)

            @pl.when(b == 0)
            def _():
                dw_ref[3 - s:4 - s, :] = part

            @pl.when(b > 0)
            def _():
                dw_ref[3 - s:4 - s, :] += part

    return pl.pallas_call(
        body,
        name="conv_bwd",
        grid=(ncol, n_batch),
        in_specs=[
            pl.BlockSpec((seq, CONV_W), lambda j, b: (b, first_x + j)),
            pl.BlockSpec((4, CONV_W), lambda j, b: (0, j)),
            pl.BlockSpec((seq, CONV_W), lambda j, b: (b, j)),
        ],
        out_specs=[pl.BlockSpec((seq, CONV_W), lambda j, b: (b, j)), pl.BlockSpec((4, CONV_W), lambda j, b: (0, j))],
        out_shape=[jax.ShapeDtypeStruct((n_batch * seq, width), BF16), jax.ShapeDtypeStruct((4, width), F32)],
        compiler_params=pltpu.CompilerParams(dimension_semantics=("arbitrary", "arbitrary")),
    )(projp, conv_w, dc)


def _attn_chunk(qc, kb, vb, bias2, valid, lane_lo):
    sel = (lane_lo, jnp.logical_not(lane_lo))
    items = [(i, e) for i in range(len(qc)) for e in (0, 1)]
    k16, v16 = [t.astype(BF16) for t in kb], [t.astype(BF16) for t in vb]
    qm = [(jnp.where(sel[e], qc[i], 0.0) * (A_DIM ** -0.5)).astype(BF16) for i, e in items]
    s = [lax.dot_general(qm[n], k16[i], NT, preferred_element_type=F32) + bias2[e] for n, (i, e) in enumerate(items)]
    s = [jnp.where(valid[i], s[n], -1e30) for n, (i, e) in enumerate(items)]
    p = [jnp.exp(t - lax.stop_gradient(jnp.max(t, axis=-1, keepdims=True))) for t in s]
    p = [t * (1.0 / jnp.sum(t, axis=-1, keepdims=True)) for t in p]
    o = [jnp.where(sel[e], jnp.dot(p[n].astype(BF16), v16[i], preferred_element_type=F32), 0.0)
         for n, (i, e) in enumerate(items)]
    return [o[2 * i] + o[2 * i + 1] for i in range(len(qc))]


ATTN_GROUP_FWD, ATTN_GROUP_BWD = 4, 4


def _attn_group(g, group, q_ref, kp_ref, vp_ref):
    col = lax.broadcasted_iota(jnp.int32, (CHUNK, A_BAND), 1)
    lane_lo = lax.broadcasted_iota(jnp.int32, (1, LANE), 1) < A_DIM
    starts = [pl.multiple_of((g * group + i) * CHUNK, CHUNK) for i in range(group)]
    rows = [pl.ds(r0, CHUNK) for r0 in starts]
    bands = [pl.ds(r0, A_BAND) for r0 in starts]
    valid = [col + r0 >= A_PAD for r0 in starts]
    loaded = [q_ref[r, :] for r in rows], [kp_ref[b, :] for b in bands], [vp_ref[b, :] for b in bands]
    return rows, bands, loaded, valid, lane_lo


def _attn_specs(seq):
    def blk(first):
        return pl.BlockSpec((seq, LANE), lambda hp, b: (b, first + hp))

    return blk, pl.BlockSpec((2, CHUNK, A_BAND), lambda hp, b: (hp, 0, 0))


def _attn_fwd(projp, bias, n_batch, seq):
    nc = seq // CHUNK
    blk, bias_spec = _attn_specs(seq)

    def body(q_ref, k_ref, v_ref, b_ref, o_ref, kp_ref, vp_ref):
        kp_ref[0:A_PAD, :] = jnp.zeros((A_PAD, LANE), F32)
        vp_ref[0:A_PAD, :] = jnp.zeros((A_PAD, LANE), F32)
        kp_ref[A_PAD:, :] = k_ref[...]
        vp_ref[A_PAD:, :] = v_ref[...]
        bias2 = b_ref[...]

        def step(g, carry):
            rows, _, (qc, kb, vb), valid, lane_lo = _attn_group(g, ATTN_GROUP_FWD, q_ref, kp_ref, vp_ref)
            out = _attn_chunk(qc, kb, vb, bias2, valid, lane_lo)
            for r, o in zip(rows, out):
                o_ref[r, :] = o.astype(o_ref.dtype)
            return carry

        lax.fori_loop(0, nc // ATTN_GROUP_FWD, step, 0)

    return pl.pallas_call(
        body,
        name="attn_fwd",
        grid=(A_HEADS // 2, n_batch),
        in_specs=[blk(0), blk(4), blk(8), bias_spec],
        out_specs=pl.BlockSpec((seq, LANE), lambda hp, b: (b, hp)),
        out_shape=jax.ShapeDtypeStruct((n_batch * seq, A_WIDTH), BF16),
        scratch_shapes=[pltpu.VMEM((A_PAD + seq, LANE), F32), pltpu.VMEM((A_PAD + seq, LANE), F32)],
        compiler_params=pltpu.CompilerParams(dimension_semantics=("parallel", "parallel")),
    )(projp, projp, projp, bias)


def _attn_bwd(projp, bias, dy, n_batch, seq):
    nc = seq // CHUNK
    blk, bias_spec = _attn_specs(seq)
    out_blk = pl.BlockSpec((seq, LANE), lambda hp, b: (b, hp))

    def body(q_ref, k_ref, v_ref, b_ref, dy_ref, dq_ref, dk_ref, dv_ref, db_ref, kp_ref, vp_ref, dkp_ref, dvp_ref):
        b = pl.program_id(1)
        kp_ref[0:A_PAD, :] = jnp.zeros((A_PAD, LANE), F32)
        vp_ref[0:A_PAD, :] = jnp.zeros((A_PAD, LANE), F32)
        kp_ref[A_PAD:, :] = k_ref[...]
        vp_ref[A_PAD:, :] = v_ref[...]
        dkp_ref[...] = jnp.zeros_like(dkp_ref)
        dvp_ref[...] = jnp.zeros_like(dvp_ref)
        bias2 = b_ref[...]

        @pl.when(b == 0)
        def _():
            db_ref[...] = jnp.zeros_like(db_ref)

        def step(g, carry):
            rows, bands, (qc, kb, vb), valid, lane_lo = _attn_group(g, ATTN_GROUP_BWD, q_ref, kp_ref, vp_ref)
            _, vjp = jax.vjp(lambda q, k, v, bb: _attn_chunk(q, k, v, bb, valid, lane_lo), qc, kb, vb, bias2)
            dq, dk, dv, dbias = vjp([dy_ref[r, :] for r in rows])
            for i, r in enumerate(rows):
                dq_ref[r, :] = dq[i].astype(dq_ref.dtype)
            for i, band in enumerate(bands):
                dkp_ref[band, :] += dk[i]
                dvp_ref[band, :] += dv[i]
            db_ref[...] += dbias
            return carry

        lax.fori_loop(0, nc // ATTN_GROUP_BWD, step, 0)
        dk_ref[...] = dkp_ref[A_PAD:, :].astype(dk_ref.dtype)
        dv_ref[...] = dvp_ref[A_PAD:, :].astype(dv_ref.dtype)

    n_tok = n_batch * seq
    pad = pltpu.VMEM((A_PAD + seq, LANE), F32)
    return pl.pallas_call(
        body,
        name="attn_bwd",
        grid=(A_HEADS // 2, n_batch),
        in_specs=[blk(0), blk(4), blk(8), bias_spec, out_blk],
        out_specs=[out_blk, out_blk, out_blk, bias_spec],
        out_shape=[jax.ShapeDtypeStruct((n_tok, A_WIDTH), BF16)] * 3 + [jax.ShapeDtypeStruct((A_HEADS, CHUNK, A_BAND), F32)],
        scratch_shapes=[pad, pad, pad, pad],
        compiler_params=pltpu.CompilerParams(dimension_semantics=("arbitrary", "arbitrary")),
    )(projp, projp, projp, bias, dy)


def _rel_bias_table(rel_bias):
    span = CHUNK + A_BAND - 1
    near = REL_CLIP + CHUNK
    far = jnp.broadcast_to(rel_bias[:, 2 * REL_CLIP:], (A_HEADS, span - near))
    t = jnp.concatenate([rel_bias[:, 2 * REL_CLIP + 1 - near:], far], axis=1)
    u = jnp.concatenate([t[:, :A_BAND][:, ::-1], t[:, A_BAND:][:, ::-1]], axis=1)
    rolled = jnp.tile(u, (1, CHUNK))[:, :CHUNK * (span - 1)].reshape(A_HEADS, CHUNK, span - 1)
    return rolled[:, :, :A_BAND]


def _dot(a, b, dn=NN):
    return lax.dot_general(a, b, dn, precision=DELTA_PREC, preferred_element_type=F32)


def _each(fn, *lists):
    return [fn(*vals) for vals in zip(*lists)]


def _delta_chunk(r_state, cq, ck, cv, beta, g):
    ii = lax.broadcasted_iota(jnp.int32, (CHUNK, CHUNK), 0)
    jj = lax.broadcasted_iota(jnp.int32, (CHUNK, CHUNK), 1)
    incl, strict, eye = ii >= jj, ii > jj, ii == jj
    q = _each(lambda t: t * lax.rsqrt(jnp.sum(t * t, axis=-1, keepdims=True) + EPS) * (B_DIM ** -0.5), cq)
    k = _each(lambda t: t * lax.rsqrt(jnp.sum(t * t, axis=-1, keepdims=True) + EPS), ck)
    g_b = _each(lambda t: jnp.broadcast_to(t, (CHUNK, CHUNK)), g)
    g_row = _each(lambda t: jnp.sum(jnp.where(eye, t, 0.0), axis=0, keepdims=True), g_b)
    gc_col = _each(lambda t: jnp.sum(jnp.where(incl, t, 0.0), axis=1, keepdims=True), g_row)
    gc_row = _each(lambda t: jnp.sum(jnp.where(ii <= jj, t, 0.0), axis=0, keepdims=True), g_b)
    decay = _each(lambda c, r: jnp.where(incl, jnp.exp(jnp.where(incl, c - r, 0.0)), 0.0), gc_col, gc_row)
    kk = _each(lambda t: _dot(t, t, NT), k)
    x = _each(lambda b, m, d: jnp.where(strict, -(b * m * d), 0.0), beta, kk, decay)
    inv = _each(lambda t: jnp.where(eye, 1.0, 0.0) + t, x)
    pw = x
    for _ in range(5):
        pw = _each(lambda t: _dot(t, t), pw)
        inv = _each(lambda t, s: t + _dot(t, s), inv, pw)
    egc = _each(jnp.exp, gc_col)
    u = _each(lambda t, b, v: _dot(t, b * v), inv, beta, cv)
    wk = _each(lambda t, b, e, kh: _dot(t, (b * e) * kh), inv, beta, egc, k)
    pqk = _each(lambda qh, kh, d: _dot(qh, kh, NT) * d, q, k, decay)
    g_last = _each(lambda c: c[CHUNK - 1:CHUNK, :], gc_col)
    kdec = _each(lambda kh, gl, c: kh * jnp.exp(gl - c), k, g_last, gc_col)
    w = _each(lambda uh, wkh, r: uh - _dot(wkh, r), u, wk, r_state)
    o = _each(lambda e, qh, r, ph, wh: e * _dot(qh, r) + _dot(ph, wh), egc, q, r_state, pqk, w)
    r_new = _each(lambda gl, r, kd, wh: jnp.exp(gl) * r + _dot(kd, wh, TN), g_last, r_state, kdec, w)
    return o, r_new


DELTA_BLK = 512


def _delta_blocks(n_batch, seq):
    nblk = seq // DELTA_BLK
    cpb = DELTA_BLK // CHUNK

    def rows(width, order):
        return pl.BlockSpec((DELTA_BLK, width), lambda b, i: (b * nblk + order(i), 0))

    def states(order):
        return pl.BlockSpec((cpb, B_HEADS, B_DIM, B_DIM), lambda b, i: (b * nblk + order(i), 0, 0, 0))

    return nblk, cpb, rows, states


def _head_cols(h):
    return [pl.ds(part * B_HEADS * B_DIM + h * B_DIM, B_DIM) for part in range(3)]


def _load_heads(c_ref, bg_ref, state_ref, rows):
    bg_c = bg_ref[rows, :]
    cols = [_head_cols(h) for h in range(B_HEADS)]
    return ([state_ref[h] for h in range(B_HEADS)], [c_ref[rows, c[0]] for c in cols], [c_ref[rows, c[1]] for c in cols],
            [c_ref[rows, c[2]] for c in cols], [bg_c[:, h:h + 1] for h in range(B_HEADS)],
            [bg_c[:, B_HEADS + h:B_HEADS + h + 1] for h in range(B_HEADS)])


def _delta_fwd(conv, bg, n_batch, seq):
    nblk, cpb, rows_spec, states_spec = _delta_blocks(n_batch, seq)

    def forward(i):
        return i

    def body(c_ref, bg_ref, o_ref, st_ref, r_ref):
        @pl.when(pl.program_id(1) == 0)
        def _():
            r_ref[...] = jnp.zeros_like(r_ref)

        def step(c, carry):
            rows = pl.ds(pl.multiple_of(c * CHUNK, CHUNK), CHUNK)
            args = _load_heads(c_ref, bg_ref, r_ref, rows)
            o, r_new = _delta_chunk(*args)
            for h in range(B_HEADS):
                st_ref[c, h] = args[0][h]
                o_ref[rows, pl.ds(h * B_DIM, B_DIM)] = o[h]
            for h in range(B_HEADS):
                r_ref[h] = r_new[h]
            return carry

        lax.fori_loop(0, cpb, step, 0)

    n_tok = n_batch * seq
    return pl.pallas_call(
        body,
        name="delta_fwd",
        grid=(n_batch, nblk),
        in_specs=[rows_spec(B_CONV, forward), rows_spec(LANE, forward)],
        out_specs=[rows_spec(B_HEADS * B_DIM, forward), states_spec(forward)],
        out_shape=[jax.ShapeDtypeStruct((n_tok, B_HEADS * B_DIM), F32),
                   jax.ShapeDtypeStruct((n_tok // CHUNK, B_HEADS, B_DIM, B_DIM), F32)],
        scratch_shapes=[pltpu.VMEM((B_HEADS, B_DIM, B_DIM), F32)],
        compiler_params=pltpu.CompilerParams(dimension_semantics=("arbitrary", "arbitrary")),
    )(conv, bg)


def _delta_bwd(conv, bg, states, do, n_batch, seq):
    nblk, cpb, rows_spec, states_spec = _delta_blocks(n_batch, seq)

    def backward(i):
        return nblk - 1 - i

    def body(c_ref, bg_ref, st_ref, do_ref, dc_ref, dbg_ref, dr_ref):
        @pl.when(pl.program_id(1) == 0)
        def _():
            dr_ref[...] = jnp.zeros_like(dr_ref)

        def step(n, carry):
            c = cpb - 1 - n
            rows = pl.ds(pl.multiple_of(c * CHUNK, CHUNK), CHUNK)
            _, vjp = jax.vjp(_delta_chunk, *_load_heads(c_ref, bg_ref, st_ref.at[c], rows))
            do = [do_ref[rows, pl.ds(h * B_DIM, B_DIM)] for h in range(B_HEADS)]
            dr, dq, dk, dv, dbeta, dg = vjp((do, [dr_ref[h] for h in range(B_HEADS)]))
            lane = lax.broadcasted_iota(jnp.int32, (CHUNK, LANE), 1)
            dbg = jnp.zeros((CHUNK, LANE), F32)
            for h in range(B_HEADS):
                cq, ck, cv = _head_cols(h)
                dr_ref[h] = dr[h]
                dc_ref[rows, cq] = dq[h]
                dc_ref[rows, ck] = dk[h]
                dc_ref[rows, cv] = dv[h]
                dbg = dbg + jnp.where(lane == h, dbeta[h], 0.0) + jnp.where(lane == h + B_HEADS, dg[h], 0.0)
            dbg_ref[rows, :] = dbg
            return carry

        lax.fori_loop(0, cpb, step, 0)

    n_tok = n_batch * seq
    return pl.pallas_call(
        body,
        name="delta_bwd",
        grid=(n_batch, nblk),
        in_specs=[rows_spec(B_CONV, backward), rows_spec(LANE, backward), states_spec(backward),
                  rows_spec(B_HEADS * B_DIM, backward)],
        out_specs=[rows_spec(B_CONV, backward), rows_spec(LANE, backward)],
        out_shape=[jax.ShapeDtypeStruct((n_tok, B_CONV), F32), jax.ShapeDtypeStruct((n_tok, LANE), F32)],
        scratch_shapes=[pltpu.VMEM((B_HEADS, B_DIM, B_DIM), F32)],
        compiler_params=pltpu.CompilerParams(dimension_semantics=("arbitrary", "arbitrary")),
    )(conv, bg, states, do)


def _lane_row(vec4, first):
    return jnp.concatenate([jnp.zeros((1, first), F32), vec4.reshape(1, B_HEADS).astype(F32),
                            jnp.zeros((1, LANE - first - B_HEADS), F32)], axis=1)


def _local_step(x3d, p3d, tgt3d, w, small):
    n_batch, seq, _ = x3d.shape
    n_tok = n_batch * seq
    x, p, tgt = x3d.reshape(n_tok, D), p3d.reshape(n_tok, -1), tgt3d.reshape(n_tok, D)
    g_mix, g_ffn, g_ple, g_final = (small[k].reshape(1, D) for k in ("g_mix", "g_ffn", "g_ple", "g_final"))
    w_onorm = small["w_onorm"].reshape(1, B_DIM)
    al_row = _lane_row(small["a_log"], B_HEADS)
    dtb_row = _lane_row(small["dt_bias"], B_HEADS)
    rel_bias = small["rel_bias"].reshape(A_HEADS, -1)
    bias = _rel_bias_table(rel_bias)
    conv_w = small["conv_w"].reshape(4, B_CONV)
    bd_blk = P_BD // LANE

    h1 = _rms_fwd(x, g_mix, name="rms_mix")
    projp = _mm(h1, w["w_in"], name="mm_proj", tn=640)
    y_a = _attn_fwd(projp, bias, n_batch, seq)
    conv = _conv_fwd(projp, conv_w, n_batch, seq)
    (bg,) = _rowwise(lambda raw, al, dtb: ([_gate_scalars(raw, al, dtb)], []), [(projp, LANE, bd_blk, 0)],
                     [_full(al_row), _full(dtb_row)], [(LANE, F32, LANE, 0, 0)], name="gate_scalars", tr=1024)
    o_b, states = _delta_fwd(conv, bg, n_batch, seq)
    (y_b,) = _rowwise(lambda o, z, wn: ([_gated_norm(o, z, wn)], []), [(o_b, LANE, 0, 1), (projp, LANE, P_Z // LANE, 1)],
                      [_full(w_onorm)], [(B_HEADS * B_DIM, BF16, LANE, 0, 1)], name="gated_norm", tr=1024, ncol=B_HEADS)
    t_a = _mm(y_a, w["w_branch_a"], name="mm_branch_a", tn=1024)
    t_b = _mm(y_b, w["w_branch_b"], name="mm_branch_b", tn=1024)
    half = D // 2
    gate_rows = [(projp, half, P_GATE // half, 1), (projp, half, P_GATE // half + 2, 1), (t_a, half, 0, 1), (t_b, half, 0, 1)]
    (merged,) = _rowwise(lambda ga, gb, ta, tb: ([_merge(ga, gb, ta, tb)], []), gate_rows, [], [(D, BF16, half, 0, 1)],
                         name="merge", tr=512, ncol=2)
    x1 = _mm(merged, w["w_out"], add=x, name="mm_out", tn=1024)
    h2 = _rms_fwd(x1, g_ffn, name="rms_ffn")
    gu = _mm(h2, w["w_gate_up"], name="mm_gate_up", tn=512)
    (act,) = _rowwise(lambda gub: ([_swiglu(gub)], []), [_full(gu)], [], [(D_FF, BF16, D_FF, 0, 0)], name="swiglu", tr=256)
    x2 = _mm(act, w["w_down"], add=x1, name="mm_down", tn=1024, tk=1408)
    h3 = _rms_fwd(x2, g_ple, name="rms_ple")
    pg = _mm(h3, w["w_ple_gate"], name="mm_ple_gate", tn=1024)
    pp = _mm(p, w["w_ple_proj"], name="mm_ple_proj", tn=1024)

    def head_fn(x2b, pgb, ppb, tb, gb):
        loss, (dx2, dpg, dpp, dg) = jax.value_and_grad(_head_loss, argnums=(0, 1, 2, 4))(x2b, pgb, ppb, tb, gb)
        return [dx2, dpg, dpp], [dg, jnp.full((1, LANE), loss, F32)]

    dx3, dpg, dpp, dg_final, loss_row = _rowwise(
        head_fn, [_full(x2), _full(pg), _full(pp), _full(tgt)], [_full(g_final)],
        [(D, F32, D, 0, 0), (D, BF16, D, 0, 0), (D, BF16, D, 0, 0)], [(D, D, 0), (LANE, LANE, 0)], name="loss_head", tr=256)
    gw = {}
    gw["w_ple_proj"] = _mm(p, dpp, ta=True, out_dtype=BF16, name="mm_d_ple_proj", tm=256, tn=1024)
    gw["w_ple_gate"] = _mm(h3, dpg, ta=True, out_dtype=BF16, name="mm_d_ple_gate", tn=512)
    dh3 = _mm(dpg, w["w_ple_gate"].T, name="mm_dh3", tn=1024)
    dx2, dg_ple = _rms_bwd(x2, g_ple, dh3, dx3, name="rms_ple_bwd")
    gw["w_down"] = _mm(act, dx2, ta=True, out_dtype=BF16, name="mm_d_down", tm=1408, tn=512, tk=2048)
    dact = _mm(dx2, w["w_down"].T, name="mm_dact", tn=1408)

    def swiglu_bwd(gub, dab):
        _, vjp = jax.vjp(_swiglu, gub)
        return [vjp(dab)[0]], []

    (dgu,) = _rowwise(swiglu_bwd, [_full(gu), _full(dact)], [], [(2 * D_FF, BF16, 2 * D_FF, 0, 0)], name="swiglu_bwd", tr=256)
    gw["w_gate_up"] = _mm(h2, dgu, ta=True, out_dtype=BF16, name="mm_d_gate_up", tn=512)
    dh2 = _mm(dgu, w["w_gate_up"].T, name="mm_dh2", tn=1024, tk=1408)
    dx1, dg_ffn = _rms_bwd(x1, g_ffn, dh2, dx2, name="rms_ffn_bwd")
    gw["w_out"] = _mm(merged, dx1, ta=True, out_dtype=BF16, name="mm_d_out", tn=512)
    dmerged = _mm(dx1, w["w_out"].T, name="mm_dmerged", tn=1024)

    def merge_bwd(ga, gb, ta, tb, dm):
        _, vjp = jax.vjp(_merge, ga, gb, ta, tb)
        return list(vjp(dm)), []

    dga, dgb, dta, dtb = _rowwise(merge_bwd, gate_rows + [(dmerged, half, 0, 1)], [], [(D, BF16, half, 0, 1)] * 4,
                                  name="merge_bwd", tr=512, ncol=2)
    gw["w_branch_a"] = _mm(y_a, dta, ta=True, out_dtype=BF16, name="mm_d_branch_a", tn=1024)
    gw["w_branch_b"] = _mm(y_b, dtb, ta=True, out_dtype=BF16, name="mm_d_branch_b", tn=1024)
    dya = _mm(dta, w["w_branch_a"].T, name="mm_dya", tn=512)
    dyb = _mm(dtb, w["w_branch_b"].T, name="mm_dyb", tn=512)

    def gated_norm_bwd(o, z, dy, wn):
        _, vjp = jax.vjp(_gated_norm, o, z, wn)
        do, dz, dwn = vjp(dy)
        return [do, dz], [dwn]

    do_b, dz, dw_onorm = _rowwise(
        gated_norm_bwd, [(o_b, LANE, 0, 1), (projp, LANE, P_Z // LANE, 1), (dyb, LANE, 0, 1)], [_full(w_onorm)],
        [(B_HEADS * B_DIM, F32, LANE, 0, 1), (B_HEADS * B_DIM, BF16, LANE, 0, 1)], [(B_DIM, B_DIM, 0)],
        name="gated_norm_bwd", tr=1024, ncol=B_HEADS)
    dconv_out, dbg = _delta_bwd(conv, bg, states, do_b, n_batch, seq)

    def gate_scalars_bwd(raw, dbgb, al, dtb):
        _, vjp = jax.vjp(_gate_scalars, raw, al, dtb)
        draw, dal, ddtb = vjp(dbgb)
        return [draw], [dal, ddtb]

    dbd, dal_row, ddtb_row = _rowwise(gate_scalars_bwd, [(projp, LANE, bd_blk, 0), _full(dbg)], [_full(al_row), _full(dtb_row)],
                                      [(LANE, BF16, LANE, 0, 0)], [(LANE, LANE, 0), (LANE, LANE, 0)], name="gate_scalars_bwd",
                                      tr=1024)
    dconv, dconv_w = _conv_bwd(projp, conv_w, dconv_out, n_batch, seq)
    dq_a, dk_a, dv_a, dbias = _attn_bwd(projp, bias, dya, n_batch, seq)
    dprojp = jnp.concatenate([dq_a, dk_a, dv_a, dconv, dz, dga, dgb, dbd], axis=1)
    gw["w_in"] = _mm(h1, dprojp, ta=True, out_dtype=BF16, name="mm_d_in", tn=640)
    dh1 = _mm(dprojp, w["w_in"].T, name="mm_dh1", tn=1024, tk=1152)
    grad_x, dg_mix = _rms_bwd(x, g_mix, dh1, dx1, name="rms_mix_bwd")

    _, bias_vjp = jax.vjp(_rel_bias_table, rel_bias)
    gs = {
        "g_mix": dg_mix, "g_ffn": dg_ffn, "g_ple": dg_ple, "g_final": dg_final, "w_onorm": dw_onorm,
        "conv_w": dconv_w, "rel_bias": bias_vjp(dbias)[0],
        "a_log": dal_row[0, B_HEADS:2 * B_HEADS], "dt_bias": ddtb_row[0, B_HEADS:2 * B_HEADS],
    }
    return loss_row[:, :1], grad_x.reshape(n_batch, seq, D), gw, gs


MATRICES = (
    ("w_in", (D, D_IN), 1), ("w_gate_up", (D, 2 * D_FF), 1), ("w_branch_a", (A_WIDTH, D), 1), ("w_branch_b", (A_WIDTH, D), 1),
    ("w_out", (D, D), 0), ("w_down", (D_FF, D), 0), ("w_ple_gate", (D, D), 0), ("w_ple_proj", (256, D), 1),
)
TAPS_PER_SHARD = B_CONV // N_DEV


def _from_gathered(slabs, axis):
    _, rows, cols = slabs.shape
    if axis == 0:
        return slabs.reshape(N_DEV * rows, cols)
    return jnp.transpose(slabs, (1, 0, 2)).reshape(rows, N_DEV * cols)


def _to_owner(full, axis):
    rows, cols = full.shape
    if axis == 0:
        return full.reshape(N_DEV, rows // N_DEV, cols)
    return jnp.transpose(full.reshape(rows, N_DEV, cols // N_DEV), (1, 0, 2))


def _permute_w_in(w_in):
    zeros = jnp.zeros((D, P_END - D_IN), w_in.dtype)
    return jnp.concatenate([w_in[:, :P_GATE], w_in[:, P_GATE + 8:], w_in[:, P_GATE:P_GATE + 8], zeros], axis=1)


def _unpermute_w_in(gp):
    return jnp.concatenate([gp[:, :P_GATE], gp[:, P_BD:P_BD + 8], gp[:, P_GATE:P_BD]], axis=1)


SMALL_ROWS = 16
SMALL_LAYOUT = (("g_mix", 0, D), ("g_ffn", 1, D), ("g_ple", 2, D), ("g_final", 3, D), ("conv_w", 4, 4 * B_CONV),
                ("rel_bias", 10, A_HEADS * (2 * REL_CLIP + 1)), ("w_onorm", 13, B_DIM), ("a_log", 14, B_HEADS),
                ("dt_bias", 14, B_HEADS), ("loss", 15, 1))


def _pack_small(gs):
    rows = {}
    for name, row, n in SMALL_LAYOUT:
        rows.setdefault(row, []).append(gs[name].reshape(-1).astype(F32))
    parts = []
    for row in sorted(rows):
        flat = jnp.concatenate(rows[row])
        parts.append(jnp.concatenate([flat, jnp.zeros((-flat.shape[0] % D,), F32)]))
    flat = jnp.concatenate(parts)
    assert flat.shape[0] == SMALL_ROWS * D, flat.shape
    return flat.reshape(SMALL_ROWS, D)


def _unpack_small(blk):
    flat, out, used = blk.reshape(-1), {}, {}
    for name, row, n in SMALL_LAYOUT:
        start = row * D + used.get(row, 0)
        out[name] = flat[start:start + n]
        used[row] = used.get(row, 0) + n
    return out


def _position():
    return lax.axis_index("x"), lax.axis_index("y"), lax.axis_index("c")


PEERS = N_DEV - 1


def _comm_call(body, arrays, out_shapes, *, name):
    n = len(arrays)
    return pl.pallas_call(
        body,
        name=name,
        out_shape=out_shapes,
        in_specs=[HBM_SPEC] * n,
        out_specs=[HBM_SPEC] * n,
        scratch_shapes=[pltpu.SemaphoreType.DMA((PEERS * n,)), pltpu.SemaphoreType.DMA((PEERS * n,)),
                        pltpu.SemaphoreType.DMA((n,))],
    )(*arrays)


def _weights_allgather(shards):
    n = len(shards)

    def body(*refs):
        ins, outs = refs[:n], refs[n:2 * n]
        send_sems, recv_sems, local_sems = refs[2 * n:]
        x, y, c = _position()
        me, sibling = (x, y, c), (x, y, 1 - c)
        chips = [(1 - x, y), (x, 1 - y), (1 - x, 1 - y)]

        def slab(a, px, py, pc):
            return outs[a].at[4 * px + 2 * py + pc]

        def copy(a, k, block, to, src=None):
            return pltpu.make_async_remote_copy(src_ref=slab(a, *block) if src is None else src, dst_ref=slab(a, *block),
                                                send_sem=send_sems.at[PEERS * a + k], recv_sem=recv_sems.at[PEERS * a + k],
                                                device_id=to, device_id_type=MESH)

        local = [pltpu.make_async_copy(ins[a], slab(a, *me), local_sems.at[a]) for a in range(n)]
        sent = [copy(a, 1 + j, me, (*chip, c), src=ins[a]) for a in range(n) for j, chip in enumerate(chips)]
        sent += [copy(a, 0, me, sibling, src=ins[a]) for a in range(n)]
        for cp in sent + local:
            cp.start()
        for a in range(n):
            for j, chip in enumerate(chips):
                copy(a, 1 + j, (*chip, c), me).wait_recv()
                passed = copy(a, 4 + j, (*chip, c), sibling)
                passed.start()
                sent.append(passed)
        for a in range(n):
            copy(a, 0, sibling, me).wait_recv()
            for j, chip in enumerate(chips):
                copy(a, 4 + j, (*chip, 1 - c), me).wait_recv()
        for cp in sent:
            cp.wait_send()
        for cp in local:
            cp.wait()

    return _comm_call(body, shards, [jax.ShapeDtypeStruct((N_DEV,) + s.shape, s.dtype) for s in shards],
                      name="weights_allgather")


def _grads_exchange(by_owner):
    n = len(by_owner)

    def body(*refs):
        ins, outs = refs[:n], refs[n:2 * n]
        send_sems, recv_sems, local_sems = refs[2 * n:]
        x, y, c = _position()
        mine = 4 * x + 2 * y + c
        local = [pltpu.make_async_copy(ins[a].at[mine], outs[a].at[mine], local_sems.at[a]) for a in range(n)]
        for cp in local:
            cp.start()
        flips = [(dx, dy, dc) for dx in (0, 1) for dy in (0, 1) for dc in (0, 1) if dx + dy + dc]
        pending = []
        for k, (dx, dy, dc) in enumerate(flips):
            px, py, pc = (1 - x if dx else x), (1 - y if dy else y), (1 - c if dc else c)
            peer = 4 * px + 2 * py + pc
            for a in range(n):
                def remote(slot):
                    return pltpu.make_async_remote_copy(src_ref=ins[a].at[peer], dst_ref=outs[a].at[slot],
                                                        send_sem=send_sems.at[PEERS * a + k], recv_sem=recv_sems.at[PEERS * a + k],
                                                        device_id=(px, py, pc), device_id_type=MESH)

                sent = remote(mine)
                sent.start()
                pending.append((sent, remote(peer)))
        for sent, landed in pending:
            landed.wait_recv()
            sent.wait_send()
        for cp in local:
            cp.wait()

    return _comm_call(body, by_owner, [jax.ShapeDtypeStruct(g.shape, g.dtype) for g in by_owner], name="grads_exchange")


def _sum_slots(got, *, name, tr):
    _, rows, cols = got.shape
    tr = _tile(rows, tr, 16)

    def body(g_ref, o_ref):
        acc = g_ref[0].astype(F32)
        for j in range(1, N_DEV):
            acc = acc + g_ref[j].astype(F32)
        o_ref[...] = acc

    return pl.pallas_call(
        body,
        name=name,
        grid=(rows // tr,),
        in_specs=[pl.BlockSpec((N_DEV, tr, cols), lambda i: (0, i, 0))],
        out_specs=pl.BlockSpec((tr, cols), lambda i: (i, 0)),
        out_shape=jax.ShapeDtypeStruct((rows, cols), F32),
        compiler_params=pltpu.CompilerParams(dimension_semantics=("parallel",)),
    )(got)


def _adamw(wt, g, m, v, *, name, slots=False):
    shape = wt.shape
    two_d = (-1, shape[-1]) if wt.ndim > 1 else (1, -1)
    args = [a.reshape(two_d) for a in (wt, m, v)]
    rows, cols = args[0].shape
    tr = _tile(rows, 256, 16) if rows % 16 == 0 else rows
    args.insert(1, g.reshape((N_DEV, rows, cols) if slots else (rows, cols)))

    def body(w_ref, g_ref, m_ref, v_ref, go_ref, d_ref, nm_ref, nv_ref):
        if slots:
            gv = g_ref[0].astype(F32)
            for j in range(1, N_DEV):
                gv = gv + g_ref[j].astype(F32)
        else:
            gv = g_ref[...]
        go_ref[...] = gv
        m2 = ADAM_B1 * m_ref[...] + (1.0 - ADAM_B1) * gv
        v2 = ADAM_B2 * v_ref[...] + (1.0 - ADAM_B2) * (gv * gv)
        m_hat = m2 / (1.0 - ADAM_B1 ** ADAM_STEP)
        v_hat = v2 / (1.0 - ADAM_B2 ** ADAM_STEP)
        d_ref[...] = -ADAM_LR * (m_hat / (jnp.sqrt(v_hat) + ADAM_EPS) + ADAM_WD * w_ref[...])
        nm_ref[...] = m2
        nv_ref[...] = v2

    spec = pl.BlockSpec((tr, cols), lambda i: (i, 0))
    g_spec = pl.BlockSpec((N_DEV, tr, cols), lambda i: (0, i, 0)) if slots else spec
    outs = pl.pallas_call(
        body,
        name=name,
        grid=(rows // tr,),
        in_specs=[spec, g_spec, spec, spec],
        out_specs=[spec] * 4,
        out_shape=[jax.ShapeDtypeStruct((rows, cols), F32)] * 4,
        compiler_params=pltpu.CompilerParams(dimension_semantics=("parallel",)),
    )(*args)
    return tuple(o.reshape(shape) for o in outs)


WEIGHTS = ("g_mix", "w_in", "conv_w", "a_log", "dt_bias", "rel_bias", "w_onorm", "w_branch_a", "w_branch_b", "w_out", "g_ffn",
           "w_gate_up", "w_down", "g_ple", "w_ple_gate", "w_ple_proj", "g_final")


def kernel(x, p, g_mix, w_in, conv_w, a_log, dt_bias, rel_bias, w_onorm, w_branch_a, w_branch_b, w_out, g_ffn, w_gate_up, w_down, g_ple, w_ple_gate, w_ple_proj, g_final, loss_target, m_g_mix, m_w_in, m_conv_w, m_a_log, m_dt_bias, m_rel_bias, m_w_onorm, m_w_branch_a, m_w_branch_b, m_w_out, m_g_ffn, m_w_gate_up, m_w_down, m_g_ple, m_w_ple_gate, m_w_ple_proj, m_g_final, v_g_mix, v_w_in, v_conv_w, v_a_log, v_dt_bias, v_rel_bias, v_w_onorm, v_w_branch_a, v_w_branch_b, v_w_out, v_g_ffn, v_w_gate_up, v_w_down, v_g_ple, v_w_ple_gate, v_w_ple_proj, v_g_final):
    given = dict(g_mix=g_mix, w_in=w_in, conv_w=conv_w, a_log=a_log, dt_bias=dt_bias, rel_bias=rel_bias, w_onorm=w_onorm,
                 w_branch_a=w_branch_a, w_branch_b=w_branch_b, w_out=w_out, g_ffn=g_ffn, w_gate_up=w_gate_up, w_down=w_down,
                 g_ple=g_ple, w_ple_gate=w_ple_gate, w_ple_proj=w_ple_proj, g_final=g_final)
    mom1 = dict(g_mix=m_g_mix, w_in=m_w_in, conv_w=m_conv_w, a_log=m_a_log, dt_bias=m_dt_bias, rel_bias=m_rel_bias,
                w_onorm=m_w_onorm, w_branch_a=m_w_branch_a, w_branch_b=m_w_branch_b, w_out=m_w_out, g_ffn=m_g_ffn,
                w_gate_up=m_w_gate_up, w_down=m_w_down, g_ple=m_g_ple, w_ple_gate=m_w_ple_gate, w_ple_proj=m_w_ple_proj,
                g_final=m_g_final)
    mom2 = dict(g_mix=v_g_mix, w_in=v_w_in, conv_w=v_conv_w, a_log=v_a_log, dt_bias=v_dt_bias, rel_bias=v_rel_bias,
                w_onorm=v_w_onorm, w_branch_a=v_w_branch_a, w_branch_b=v_w_branch_b, w_out=v_w_out, g_ffn=v_g_ffn,
                w_gate_up=v_w_gate_up, w_down=v_w_down, g_ple=v_g_ple, w_ple_gate=v_w_ple_gate, w_ple_proj=v_w_ple_proj,
                g_final=v_g_final)
    mine = 4 * lax.axis_index("x") + 2 * lax.axis_index("y") + lax.axis_index("c")

    gathered = _weights_allgather([given[name][0].astype(BF16) for name, _, _ in MATRICES] + [conv_w[0]])
    full = {name: _from_gathered(slabs, axis) for (name, _, axis), slabs in zip(MATRICES, gathered)}
    full["w_in"] = _permute_w_in(full["w_in"])
    small = dict(g_mix=g_mix, g_ffn=g_ffn, g_ple=g_ple, g_final=g_final, w_onorm=w_onorm, a_log=a_log, dt_bias=dt_bias,
                 rel_bias=rel_bias, conv_w=_from_gathered(gathered[-1], 1))

    loss_part, grad_x, gw, gs = _local_step(x, p[0], loss_target, full, small)
    gw["w_in"] = _unpermute_w_in(gw["w_in"])
    gs["loss"] = loss_part

    by_owner = [_to_owner(gw[name], axis) for name, _, axis in MATRICES]
    got = _grads_exchange(by_owner + [jnp.broadcast_to(_pack_small(gs), (N_DEV, SMALL_ROWS, D))])
    grads = dict(zip([name for name, _, _ in MATRICES], got))
    small_sum = _unpack_small(_sum_slots(got[-1], name="sum_small_grads", tr=16))
    loss = small_sum.pop("loss")[0]
    conv_all = small_sum.pop("conv_w").reshape(4, N_DEV, TAPS_PER_SHARD)
    small_sum["conv_w"] = lax.dynamic_index_in_dim(conv_all, mine, axis=1, keepdims=False)

    out_g, out_d, out_m, out_v = [], [], [], []
    for name in WEIGHTS:
        if name in grads:
            g, delta, new_m, new_v = _adamw(given[name], grads[name], mom1[name], mom2[name], name=f"adamw_{name}", slots=True)
        else:
            g, delta, new_m, new_v = _adamw(given[name], small_sum[name].reshape(given[name].shape), mom1[name], mom2[name],
                                            name=f"adamw_{name}")
        out_g.append(g)
        out_d.append(delta)
        out_m.append(new_m)
        out_v.append(new_v)
    return (loss, grad_x, *out_g, *out_d, *out_m, *out_v)
```

```python
import jax
import jax.numpy as jnp
from jax import lax
from jax.experimental import pallas as pl
from jax.experimental.pallas import tpu as pltpu

F32 = jnp.float32
BF16 = jnp.bfloat16
DELTA_PREC = lax.Precision.HIGH
MESH = pl.DeviceIdType.MESH

N_DEV = 8
D = 1024
CHUNK = 64
EPS = 1e-6
A_HEADS, A_DIM, A_WIDTH = 8, 64, 512
A_BAND = 9 * CHUNK
A_PAD = 8 * CHUNK
REL_CLIP = 128
B_HEADS, B_DIM = 4, 128
B_CONV = 1536
D_FF = 2816
D_IN = 5640
P_CONV, P_Z, P_GATE, P_BD, P_END = 1536, 3072, 3584, 5632, 5760
LANE = 128

ADAM_LR, ADAM_B1, ADAM_B2, ADAM_EPS, ADAM_WD, ADAM_STEP = 0.001, 0.9, 0.999, 1e-08, 0.01, 10

NT = (((1,), (1,)), ((), ()))
TN = (((0,), (0,)), ((), ()))
NN = (((1,), (0,)), ((), ()))

HBM_SPEC = pl.BlockSpec(memory_space=pltpu.HBM)


def _tile(n, target, align=LANE):
    if n <= target:
        return n
    best = None
    for t in range(align, target + 1, align):
        if n % t == 0:
            best = t
    assert best is not None, (n, target, align)
    return best


def _mm(a, b, *, name, ta=False, add=None, out_dtype=F32, tm=1024, tn=640, tk=None):
    if ta:
        k_dim, m_dim = a.shape
    else:
        m_dim, k_dim = a.shape
    assert b.shape[0] == k_dim
    n_dim = b.shape[1]
    tm, tn = _tile(m_dim, tm), _tile(n_dim, tn)
    tk = _tile(k_dim, tk or (4096 if ta else 1024), 8 if ta else LANE)
    nk = k_dim // tk
    dn = TN if ta else NN

    def body(*refs):
        if add is None:
            a_ref, b_ref, o_ref = refs[:3]
            add_ref = None
        else:
            a_ref, b_ref, add_ref, o_ref = refs[:4]
        part = lax.dot_general(a_ref[...].astype(BF16), b_ref[...].astype(BF16), dn, preferred_element_type=F32)

        def finish(r):
            if add_ref is not None:
                r = r + add_ref[...]
            o_ref[...] = r.astype(o_ref.dtype)

        if nk == 1:
            finish(part)
        else:
            acc_ref = refs[-1]
            k = pl.program_id(2)

            @pl.when(k == 0)
            def _():
                acc_ref[...] = part

            @pl.when(k > 0)
            def _():
                acc_ref[...] += part

            @pl.when(k == nk - 1)
            def _():
                finish(acc_ref[...])

    a_spec = pl.BlockSpec((tk, tm), lambda i, j, k: (k, i)) if ta else pl.BlockSpec((tm, tk), lambda i, j, k: (i, k))
    in_specs = [a_spec, pl.BlockSpec((tk, tn), lambda i, j, k: (k, j))]
    args = [a, b]
    if add is not None:
        in_specs.append(pl.BlockSpec((tm, tn), lambda i, j, k: (i, j)))
        args.append(add)
    return pl.pallas_call(
        body,
        name=name,
        grid=(m_dim // tm, n_dim // tn, nk),
        in_specs=in_specs,
        out_specs=pl.BlockSpec((tm, tn), lambda i, j, k: (i, j)),
        out_shape=jax.ShapeDtypeStruct((m_dim, n_dim), out_dtype),
        scratch_shapes=[pltpu.VMEM((tm, tn), F32)] if nk > 1 else [],
        compiler_params=pltpu.CompilerParams(dimension_semantics=("parallel", "parallel", "arbitrary")),
    )(*args)


def _rowwise(fn, rows, bcs, outs, reds=(), *, name, tr, ncol=1):
    n_rows = rows[0][0].shape[0]
    tr = _tile(n_rows, tr, 8)
    nrow = n_rows // tr
    n_in, n_out = len(rows) + len(bcs), len(outs)

    def body(*refs):
        j, i = pl.program_id(0), pl.program_id(1)
        o_vals, r_vals = fn(*[r[...] for r in refs[:n_in]])
        for ref, val in zip(refs[n_in:n_in + n_out], o_vals):
            ref[...] = val.astype(ref.dtype)
        for ref, val, (_, _, stride) in zip(refs[n_in + n_out:], r_vals, reds):
            first = (i == 0) if stride else jnp.logical_and(i == 0, j == 0)

            @pl.when(first)
            def _():
                ref[...] = val

            @pl.when(jnp.logical_not(first))
            def _():
                ref[...] += val

    def spec(r, w, off, st, row_dep=True):
        if row_dep:
            return pl.BlockSpec((r, w), lambda j, i: (i, off + st * j))
        return pl.BlockSpec((r, w), lambda j, i: (0, off + st * j))

    in_specs = [spec(tr, w, off, st) for (_, w, off, st) in rows]
    in_specs += [spec(a.shape[0], w, off, st, False) for (a, w, off, st) in bcs]
    out_specs = [spec(tr, w, off, st) for (_, _, w, off, st) in outs]
    out_specs += [spec(1, w, 0, st, False) for (_, w, st) in reds]
    out_shape = [jax.ShapeDtypeStruct((n_rows, c), dt) for (c, dt, _, _, _) in outs]
    out_shape += [jax.ShapeDtypeStruct((1, c), F32) for (c, _, _) in reds]
    return pl.pallas_call(
        body,
        name=name,
        grid=(ncol, nrow),
        in_specs=in_specs,
        out_specs=out_specs,
        out_shape=out_shape,
        compiler_params=pltpu.CompilerParams(dimension_semantics=("arbitrary", "arbitrary")),
    )(*[r[0] for r in rows], *[b[0] for b in bcs])


def _full(a):
    return (a, a.shape[1], 0, 0)


def _rms(x, g):
    return x * lax.rsqrt(jnp.mean(x * x, axis=-1, keepdims=True) + EPS) * g


def _silu(x):
    return x * jax.nn.sigmoid(x)


def _softplus(x):
    return jnp.maximum(x, 0.0) + jnp.log(1.0 + jnp.exp(-jnp.abs(x)))


def _rms_fwd(x, g, *, name):
    (h,) = _rowwise(lambda xb, gb: ([_rms(xb, gb)], []), [_full(x)], [_full(g)], [(D, BF16, D, 0, 0)], name=name, tr=512)
    return h


def _rms_bwd(x, g, dh, dres, *, name):
    def fn(xb, dhb, dresb, gb):
        _, vjp = jax.vjp(_rms, xb, gb)
        dx, dg = vjp(dhb)
        return [dx + dresb], [dg]

    return _rowwise(fn, [_full(x), _full(dh), _full(dres)], [_full(g)], [(D, F32, D, 0, 0)], [(D, D, 0)], name=name, tr=256)


def _gate_scalars(raw, al_row, dtb_row):
    lane = lax.broadcasted_iota(jnp.int32, raw.shape, 1)
    beta = jax.nn.sigmoid(raw)
    g = -jnp.exp(al_row) * _softplus(raw + dtb_row)
    return jnp.where(lane < B_HEADS, beta, jnp.where(lane < 2 * B_HEADS, g, 0.0))


def _gated_norm(o, z, w):
    return _rms(o, w) * _silu(z)


def _merge(ga, gb, ta, tb):
    return jax.nn.sigmoid(ga) * ta + jax.nn.sigmoid(gb) * tb


def _swiglu(gu):
    return _silu(gu[:, :D_FF]) * gu[:, D_FF:]


def _head_loss(x2, pg, pp, tgt, g):
    x3 = x2 + jax.nn.sigmoid(pg) * pp
    err = _rms(x3, g) - tgt
    return 0.5 * jnp.sum(jnp.mean(err * err, axis=-1))


CONV_W = 256


def _conv_taps(x, w):
    row = lax.broadcasted_iota(jnp.int32, x.shape, 0)
    shifted = [x] + [jnp.where(row >= s, pltpu.roll(x, s, 0), 0.0) for s in (1, 2, 3)]
    pre = shifted[0] * w[3:4]
    for s in (1, 2, 3):
        pre = pre + shifted[s] * w[3 - s:4 - s]
    return pre, shifted


def _conv_fwd(projp, conv_w, n_batch, seq):
    ncol = B_CONV // CONV_W
    first = P_CONV // CONV_W

    def body(x_ref, w_ref, o_ref):
        pre, _ = _conv_taps(x_ref[...], w_ref[...])
        o_ref[...] = _silu(pre)

    return pl.pallas_call(
        body,
        name="conv_fwd",
        grid=(ncol, n_batch),
        in_specs=[pl.BlockSpec((seq, CONV_W), lambda j, b: (b, first + j)), pl.BlockSpec((4, CONV_W), lambda j, b: (0, j))],
        out_specs=pl.BlockSpec((seq, CONV_W), lambda j, b: (b, j)),
        out_shape=jax.ShapeDtypeStruct((n_batch * seq, B_CONV), F32),
        compiler_params=pltpu.CompilerParams(dimension_semantics=("parallel", "parallel")),
    )(projp, conv_w)


def _conv_bwd(projp, conv_w, dc, n_batch, seq):
    width = dc.shape[1]
    ncol = width // CONV_W
    first_x = P_CONV // CONV_W

    def body(x_ref, w_ref, dc_ref, dx_ref, dw_ref):
        b = pl.program_id(1)
        w = w_ref[...]
        pre, shifted = _conv_taps(x_ref[...], w)
        sg = jax.nn.sigmoid(pre)
        dpre = dc_ref[...] * (sg * (1.0 + pre * (1.0 - sg)))
        row = lax.broadcasted_iota(jnp.int32, dpre.shape, 0)
        dx = dpre * w[3:4]
        for s in (1, 2, 3):
            dx = dx + jnp.where(row < seq - s, pltpu.roll(dpre, seq - s, 0), 0.0) * w[3 - s:4 - s]
        dx_ref[...] = dx.astype(dx_ref.dtype)
        for s in (0, 1, 2, 3):
            part = jnp.sum(dpre * shifted[s], axis=0, keepdims=True)

            @pl.when(b == 0)
            def _():
                dw_ref[3 - s:4 - s, :] = part

            @pl.when(b > 0)
            def _():
                dw_ref[3 - s:4 - s, :] += part

    return pl.pallas_call(
        body,
        name="conv_bwd",
        grid=(ncol, n_batch),
        in_specs=[
            pl.BlockSpec((seq, CONV_W), lambda j, b: (b, first_x + j)),
            pl.BlockSpec((4, CONV_W), lambda j, b: (0, j)),
            pl.BlockSpec((seq, CONV_W), lambda j, b: (b, j)),
        ],
        out_specs=[pl.BlockSpec((seq, CONV_W), lambda j, b: (b, j)), pl.BlockSpec((4, CONV_W), lambda j, b: (0, j))],
        out_shape=[jax.ShapeDtypeStruct((n_batch * seq, width), BF16), jax.ShapeDtypeStruct((4, width), F32)],
        compiler_params=pltpu.CompilerParams(dimension_semantics=("arbitrary", "arbitrary")),
    )(projp, conv_w, dc)


def _attn_chunk(qc, kb, vb, bias2, valid, lane_lo):
    sel = (lane_lo, jnp.logical_not(lane_lo))
    items = [(i, e) for i in range(len(qc)) for e in (0, 1)]
    k16, v16 = [t.astype(BF16) for t in kb], [t.astype(BF16) for t in vb]
    qm = [(jnp.where(sel[e], qc[i], 0.0) * (A_DIM ** -0.5)).astype(BF16) for i, e in items]
    s = [lax.dot_general(qm[n], k16[i], NT, preferred_element_type=F32) + bias2[e] for n, (i, e) in enumerate(items)]
    s = [jnp.where(valid[i], s[n], -1e30) for n, (i, e) in enumerate(items)]
    p = [jnp.exp(t - lax.stop_gradient(jnp.max(t, axis=-1, keepdims=True))) for t in s]
    p = [t * (1.0 / jnp.sum(t, axis=-1, keepdims=True)) for t in p]
    o = [jnp.where(sel[e], jnp.dot(p[n].astype(BF16), v16[i], preferred_element_type=F32), 0.0)
         for n, (i, e) in enumerate(items)]
    return [o[2 * i] + o[2 * i + 1] for i in range(len(qc))]


ATTN_GROUP_FWD, ATTN_GROUP_BWD = 4, 4


def _attn_group(g, group, q_ref, kp_ref, vp_ref):
    col = lax.broadcasted_iota(jnp.int32, (CHUNK, A_BAND), 1)
    lane_lo = lax.broadcasted_iota(jnp.int32, (1, LANE), 1) < A_DIM
    starts = [pl.multiple_of((g * group + i) * CHUNK, CHUNK) for i in range(group)]
    rows = [pl.ds(r0, CHUNK) for r0 in starts]
    bands = [pl.ds(r0, A_BAND) for r0 in starts]
    valid = [col + r0 >= A_PAD for r0 in starts]
    loaded = [q_ref[r, :] for r in rows], [kp_ref[b, :] for b in bands], [vp_ref[b, :] for b in bands]
    return rows, bands, loaded, valid, lane_lo


def _attn_specs(seq):
    def blk(first):
        return pl.BlockSpec((seq, LANE), lambda hp, b: (b, first + hp))

    return blk, pl.BlockSpec((2, CHUNK, A_BAND), lambda hp, b: (hp, 0, 0))


def _attn_fwd(projp, bias, n_batch, seq):
    nc = seq // CHUNK
    blk, bias_spec = _attn_specs(seq)

    def body(q_ref, k_ref, v_ref, b_ref, o_ref, kp_ref, vp_ref):
        kp_ref[0:A_PAD, :] = jnp.zeros((A_PAD, LANE), F32)
        vp_ref[0:A_PAD, :] = jnp.zeros((A_PAD, LANE), F32)
        kp_ref[A_PAD:, :] = k_ref[...]
        vp_ref[A_PAD:, :] = v_ref[...]
        bias2 = b_ref[...]

        def step(g, carry):
            rows, _, (qc, kb, vb), valid, lane_lo = _attn_group(g, ATTN_GROUP_FWD, q_ref, kp_ref, vp_ref)
            out = _attn_chunk(qc, kb, vb, bias2, valid, lane_lo)
            for r, o in zip(rows, out):
                o_ref[r, :] = o.astype(o_ref.dtype)
            return carry

        lax.fori_loop(0, nc // ATTN_GROUP_FWD, step, 0)

    return pl.pallas_call(
        body,
        name="attn_fwd",
        grid=(A_HEADS // 2, n_batch),
        in_specs=[blk(0), blk(4), blk(8), bias_spec],
        out_specs=pl.BlockSpec((seq, LANE), lambda hp, b: (b, hp)),
        out_shape=jax.ShapeDtypeStruct((n_batch * seq, A_WIDTH), BF16),
        scratch_shapes=[pltpu.VMEM((A_PAD + seq, LANE), F32), pltpu.VMEM((A_PAD + seq, LANE), F32)],
        compiler_params=pltpu.CompilerParams(dimension_semantics=("parallel", "parallel")),
    )(projp, projp, projp, bias)


def _attn_bwd(projp, bias, dy, n_batch, seq):
    nc = seq // CHUNK
    blk, bias_spec = _attn_specs(seq)
    out_blk = pl.BlockSpec((seq, LANE), lambda hp, b: (b, hp))

    def body(q_ref, k_ref, v_ref, b_ref, dy_ref, dq_ref, dk_ref, dv_ref, db_ref, kp_ref, vp_ref, dkp_ref, dvp_ref):
        b = pl.program_id(1)
        kp_ref[0:A_PAD, :] = jnp.zeros((A_PAD, LANE), F32)
        vp_ref[0:A_PAD, :] = jnp.zeros((A_PAD, LANE), F32)
        kp_ref[A_PAD:, :] = k_ref[...]
        vp_ref[A_PAD:, :] = v_ref[...]
        dkp_ref[...] = jnp.zeros_like(dkp_ref)
        dvp_ref[...] = jnp.zeros_like(dvp_ref)
        bias2 = b_ref[...]

        @pl.when(b == 0)
        def _():
            db_ref[...] = jnp.zeros_like(db_ref)

        def step(g, carry):
            rows, bands, (qc, kb, vb), valid, lane_lo = _attn_group(g, ATTN_GROUP_BWD, q_ref, kp_ref, vp_ref)
            _, vjp = jax.vjp(lambda q, k, v, bb: _attn_chunk(q, k, v, bb, valid, lane_lo), qc, kb, vb, bias2)
            dq, dk, dv, dbias = vjp([dy_ref[r, :] for r in rows])
            for i, r in enumerate(rows):
                dq_ref[r, :] = dq[i].astype(dq_ref.dtype)
            for i, band in enumerate(bands):
                dkp_ref[band, :] += dk[i]
                dvp_ref[band, :] += dv[i]
            db_ref[...] += dbias
            return carry

        lax.fori_loop(0, nc // ATTN_GROUP_BWD, step, 0)
        dk_ref[...] = dkp_ref[A_PAD:, :].astype(dk_ref.dtype)
        dv_ref[...] = dvp_ref[A_PAD:, :].astype(dv_ref.dtype)

    n_tok = n_batch * seq
    pad = pltpu.VMEM((A_PAD + seq, LANE), F32)
    return pl.pallas_call(
        body,
        name="attn_bwd",
        grid=(A_HEADS // 2, n_batch),
        in_specs=[blk(0), blk(4), blk(8), bias_spec, out_blk],
        out_specs=[out_blk, out_blk, out_blk, bias_spec],
        out_shape=[jax.ShapeDtypeStruct((n_tok, A_WIDTH), BF16)] * 3 + [jax.ShapeDtypeStruct((A_HEADS, CHUNK, A_BAND), F32)],
        scratch_shapes=[pad, pad, pad, pad],
        compiler_params=pltpu.CompilerParams(dimension_semantics=("arbitrary", "arbitrary")),
    )(projp, projp, projp, bias, dy)


def _rel_bias_table(rel_bias):
    span = CHUNK + A_BAND - 1
    near = REL_CLIP + CHUNK
    far = jnp.broadcast_to(rel_bias[:, 2 * REL_CLIP:], (A_HEADS, span - near))
    t = jnp.concatenate([rel_bias[:, 2 * REL_CLIP + 1 - near:], far], axis=1)
    u = jnp.concatenate([t[:, :A_BAND][:, ::-1], t[:, A_BAND:][:, ::-1]], axis=1)
    rolled = jnp.tile(u, (1, CHUNK))[:, :CHUNK * (span - 1)].reshape(A_HEADS, CHUNK, span - 1)
    return rolled[:, :, :A_BAND]


def _dot(a, b, dn=NN):
    return lax.dot_general(a, b, dn, precision=DELTA_PREC, preferred_element_type=F32)


def _each(fn, *lists):
    return [fn(*vals) for vals in zip(*lists)]


def _delta_chunk(r_state, cq, ck, cv, beta, g):
    ii = lax.broadcasted_iota(jnp.int32, (CHUNK, CHUNK), 0)
    jj = lax.broadcasted_iota(jnp.int32, (CHUNK, CHUNK), 1)
    incl, strict, eye = ii >= jj, ii > jj, ii == jj
    q = _each(lambda t: t * lax.rsqrt(jnp.sum(t * t, axis=-1, keepdims=True) + EPS) * (B_DIM ** -0.5), cq)
    k = _each(lambda t: t * lax.rsqrt(jnp.sum(t * t, axis=-1, keepdims=True) + EPS), ck)
    g_b = _each(lambda t: jnp.broadcast_to(t, (CHUNK, CHUNK)), g)
    g_row = _each(lambda t: jnp.sum(jnp.where(eye, t, 0.0), axis=0, keepdims=True), g_b)
    gc_col = _each(lambda t: jnp.sum(jnp.where(incl, t, 0.0), axis=1, keepdims=True), g_row)
    gc_row = _each(lambda t: jnp.sum(jnp.where(ii <= jj, t, 0.0), axis=0, keepdims=True), g_b)
    decay = _each(lambda c, r: jnp.where(incl, jnp.exp(jnp.where(incl, c - r, 0.0)), 0.0), gc_col, gc_row)
    kk = _each(lambda t: _dot(t, t, NT), k)
    x = _each(lambda b, m, d: jnp.where(strict, -(b * m * d), 0.0), beta, kk, decay)
    inv = _each(lambda t: jnp.where(eye, 1.0, 0.0) + t, x)
    pw = x
    for _ in range(5):
        pw = _each(lambda t: _dot(t, t), pw)
        inv = _each(lambda t, s: t + _dot(t, s), inv, pw)
    egc = _each(jnp.exp, gc_col)
    u = _each(lambda t, b, v: _dot(t, b * v), inv, beta, cv)
    wk = _each(lambda t, b, e, kh: _dot(t, (b * e) * kh), inv, beta, egc, k)
    pqk = _each(lambda qh, kh, d: _dot(qh, kh, NT) * d, q, k, decay)
    g_last = _each(lambda c: c[CHUNK - 1:CHUNK, :], gc_col)
    kdec = _each(lambda kh, gl, c: kh * jnp.exp(gl - c), k, g_last, gc_col)
    w = _each(lambda uh, wkh, r: uh - _dot(wkh, r), u, wk, r_state)
    o = _each(lambda e, qh, r, ph, wh: e * _dot(qh, r) + _dot(ph, wh), egc, q, r_state, pqk, w)
    r_new = _each(lambda gl, r, kd, wh: jnp.exp(gl) * r + _dot(kd, wh, TN), g_last, r_state, kdec, w)
    return o, r_new


DELTA_BLK = 512


def _delta_blocks(n_batch, seq):
    nblk = seq // DELTA_BLK
    cpb = DELTA_BLK // CHUNK

    def rows(width, order):
        return pl.BlockSpec((DELTA_BLK, width), lambda b, i: (b * nblk + order(i), 0))

    def states(order):
        return pl.BlockSpec((cpb, B_HEADS, B_DIM, B_DIM), lambda b, i: (b * nblk + order(i), 0, 0, 0))

    return nblk, cpb, rows, states


def _head_cols(h):
    return [pl.ds(part * B_HEADS * B_DIM + h * B_DIM, B_DIM) for part in range(3)]


def _load_heads(c_ref, bg_ref, state_ref, rows):
    bg_c = bg_ref[rows, :]
    cols = [_head_cols(h) for h in range(B_HEADS)]
    return ([state_ref[h] for h in range(B_HEADS)], [c_ref[rows, c[0]] for c in cols], [c_ref[rows, c[1]] for c in cols],
            [c_ref[rows, c[2]] for c in cols], [bg_c[:, h:h + 1] for h in range(B_HEADS)],
            [bg_c[:, B_HEADS + h:B_HEADS + h + 1] for h in range(B_HEADS)])


def _delta_fwd(conv, bg, n_batch, seq):
    nblk, cpb, rows_spec, states_spec = _delta_blocks(n_batch, seq)

    def forward(i):
        return i

    def body(c_ref, bg_ref, o_ref, st_ref, r_ref):
        @pl.when(pl.program_id(1) == 0)
        def _():
            r_ref[...] = jnp.zeros_like(r_ref)

        def step(c, carry):
            rows = pl.ds(pl.multiple_of(c * CHUNK, CHUNK), CHUNK)
            args = _load_heads(c_ref, bg_ref, r_ref, rows)
            o, r_new = _delta_chunk(*args)
            for h in range(B_HEADS):
                st_ref[c, h] = args[0][h]
                o_ref[rows, pl.ds(h * B_DIM, B_DIM)] = o[h]
            for h in range(B_HEADS):
                r_ref[h] = r_new[h]
            return carry

        lax.fori_loop(0, cpb, step, 0)

    n_tok = n_batch * seq
    return pl.pallas_call(
        body,
        name="delta_fwd",
        grid=(n_batch, nblk),
        in_specs=[rows_spec(B_CONV, forward), rows_spec(LANE, forward)],
        out_specs=[rows_spec(B_HEADS * B_DIM, forward), states_spec(forward)],
        out_shape=[jax.ShapeDtypeStruct((n_tok, B_HEADS * B_DIM), F32),
                   jax.ShapeDtypeStruct((n_tok // CHUNK, B_HEADS, B_DIM, B_DIM), F32)],
        scratch_shapes=[pltpu.VMEM((B_HEADS, B_DIM, B_DIM), F32)],
        compiler_params=pltpu.CompilerParams(dimension_semantics=("arbitrary", "arbitrary")),
    )(conv, bg)


def _delta_bwd(conv, bg, states, do, n_batch, seq):
    nblk, cpb, rows_spec, states_spec = _delta_blocks(n_batch, seq)

    def backward(i):
        return nblk - 1 - i

    def body(c_ref, bg_ref, st_ref, do_ref, dc_ref, dbg_ref, dr_ref):
        @pl.when(pl.program_id(1) == 0)
        def _():
            dr_ref[...] = jnp.zeros_like(dr_ref)

        def step(n, carry):
            c = cpb - 1 - n
            rows = pl.ds(pl.multiple_of(c * CHUNK, CHUNK), CHUNK)
            _, vjp = jax.vjp(_delta_chunk, *_load_heads(c_ref, bg_ref, st_ref.at[c], rows))
            do = [do_ref[rows, pl.ds(h * B_DIM, B_DIM)] for h in range(B_HEADS)]
            dr, dq, dk, dv, dbeta, dg = vjp((do, [dr_ref[h] for h in range(B_HEADS)]))
            lane = lax.broadcasted_iota(jnp.int32, (CHUNK, LANE), 1)
            dbg = jnp.zeros((CHUNK, LANE), F32)
            for h in range(B_HEADS):
                cq, ck, cv = _head_cols(h)
                dr_ref[h] = dr[h]
                dc_ref[rows, cq] = dq[h]
                dc_ref[rows, ck] = dk[h]
                dc_ref[rows, cv] = dv[h]
                dbg = dbg + jnp.where(lane == h, dbeta[h], 0.0) + jnp.where(lane == h + B_HEADS, dg[h], 0.0)
            dbg_ref[rows, :] = dbg
            return carry

        lax.fori_loop(0, cpb, step, 0)

    n_tok = n_batch * seq
    return pl.pallas_call(
        body,
        name="delta_bwd",
        grid=(n_batch, nblk),
        in_specs=[rows_spec(B_CONV, backward), rows_spec(LANE, backward), states_spec(backward),
                  rows_spec(B_HEADS * B_DIM, backward)],
        out_specs=[rows_spec(B_CONV, backward), rows_spec(LANE, backward)],
        out_shape=[jax.ShapeDtypeStruct((n_tok, B_CONV), F32), jax.ShapeDtypeStruct((n_tok, LANE), F32)],
        scratch_shapes=[pltpu.VMEM((B_HEADS, B_DIM, B_DIM), F32)],
        compiler_params=pltpu.CompilerParams(dimension_semantics=("arbitrary", "arbitrary")),
    )(conv, bg, states, do)


def _lane_row(vec4, first):
    return jnp.concatenate([jnp.zeros((1, first), F32), vec4.reshape(1, B_HEADS).astype(F32),
                            jnp.zeros((1, LANE - first - B_HEADS), F32)], axis=1)


def _local_step(x3d, p3d, tgt3d, w_in, small, rest_weights, send_grads, send_w_in):
    n_batch, seq, _ = x3d.shape
    n_tok = n_batch * seq
    x, p, tgt = x3d.reshape(n_tok, D), p3d.reshape(n_tok, -1), tgt3d.reshape(n_tok, D)
    g_mix, g_ffn, g_ple, g_final = (small[k].reshape(1, D) for k in ("g_mix", "g_ffn", "g_ple", "g_final"))
    w_onorm = small["w_onorm"].reshape(1, B_DIM)
    al_row = _lane_row(small["a_log"], B_HEADS)
    dtb_row = _lane_row(small["dt_bias"], B_HEADS)
    rel_bias = small["rel_bias"].reshape(A_HEADS, -1)
    bias = _rel_bias_table(rel_bias)
    conv_w = small["conv_w"].reshape(4, B_CONV)
    bd_blk = P_BD // LANE

    h1 = _rms_fwd(x, g_mix, name="rms_mix")
    projp = _mm(h1, w_in, name="mm_proj", tn=640)
    y_a = _attn_fwd(projp, bias, n_batch, seq)
    conv = _conv_fwd(projp, conv_w, n_batch, seq)
    (bg,) = _rowwise(lambda raw, al, dtb: ([_gate_scalars(raw, al, dtb)], []), [(projp, LANE, bd_blk, 0)],
                     [_full(al_row), _full(dtb_row)], [(LANE, F32, LANE, 0, 0)], name="gate_scalars", tr=1024)
    o_b, states = _delta_fwd(conv, bg, n_batch, seq)
    (y_b,) = _rowwise(lambda o, z, wn: ([_gated_norm(o, z, wn)], []), [(o_b, LANE, 0, 1), (projp, LANE, P_Z // LANE, 1)],
                      [_full(w_onorm)], [(B_HEADS * B_DIM, BF16, LANE, 0, 1)], name="gated_norm", tr=1024, ncol=B_HEADS)
    w = rest_weights(y_b)
    t_a = _mm(y_a, w["w_branch_a"], name="mm_branch_a", tn=1024)
    t_b = _mm(y_b, w["w_branch_b"], name="mm_branch_b", tn=1024)
    half = D // 2
    gate_rows = [(projp, half, P_GATE // half, 1), (projp, half, P_GATE // half + 2, 1), (t_a, half, 0, 1), (t_b, half, 0, 1)]
    (merged,) = _rowwise(lambda ga, gb, ta, tb: ([_merge(ga, gb, ta, tb)], []), gate_rows, [], [(D, BF16, half, 0, 1)],
                         name="merge", tr=512, ncol=2)
    x1 = _mm(merged, w["w_out"], add=x, name="mm_out", tn=1024)
    h2 = _rms_fwd(x1, g_ffn, name="rms_ffn")
    gu = _mm(h2, w["w_gate_up"], name="mm_gate_up", tn=512)
    (act,) = _rowwise(lambda gub: ([_swiglu(gub)], []), [_full(gu)], [], [(D_FF, BF16, D_FF, 0, 0)], name="swiglu", tr=256)
    x2 = _mm(act, w["w_down"], add=x1, name="mm_down", tn=1024, tk=1408)
    h3 = _rms_fwd(x2, g_ple, name="rms_ple")
    pg = _mm(h3, w["w_ple_gate"], name="mm_ple_gate", tn=1024)
    pp = _mm(p, w["w_ple_proj"], name="mm_ple_proj", tn=1024)

    def head_fn(x2b, pgb, ppb, tb, gb):
        loss, (dx2, dpg, dpp, dg) = jax.value_and_grad(_head_loss, argnums=(0, 1, 2, 4))(x2b, pgb, ppb, tb, gb)
        return [dx2, dpg, dpp], [dg, jnp.full((1, LANE), loss, F32)]

    dx3, dpg, dpp, dg_final, loss_row = _rowwise(
        head_fn, [_full(x2), _full(pg), _full(pp), _full(tgt)], [_full(g_final)],
        [(D, F32, D, 0, 0), (D, BF16, D, 0, 0), (D, BF16, D, 0, 0)], [(D, D, 0), (LANE, LANE, 0)], name="loss_head", tr=256)
    gw = {}
    gw["w_ple_proj"] = _mm(p, dpp, ta=True, out_dtype=BF16, name="mm_d_ple_proj", tm=256, tn=1024)
    gw["w_ple_gate"] = _mm(h3, dpg, ta=True, out_dtype=BF16, name="mm_d_ple_gate", tn=512)
    dh3 = _mm(dpg, w["w_ple_gate"].T, name="mm_dh3", tn=1024)
    dx2, dg_ple = _rms_bwd(x2, g_ple, dh3, dx3, name="rms_ple_bwd")
    gw["w_down"] = _mm(act, dx2, ta=True, out_dtype=BF16, name="mm_d_down", tm=1408, tn=512, tk=2048)
    dact = _mm(dx2, w["w_down"].T, name="mm_dact", tn=1408)

    def swiglu_bwd(gub, dab):
        _, vjp = jax.vjp(_swiglu, gub)
        return [vjp(dab)[0]], []

    (dgu,) = _rowwise(swiglu_bwd, [_full(gu), _full(dact)], [], [(2 * D_FF, BF16, 2 * D_FF, 0, 0)], name="swiglu_bwd", tr=256)
    gw["w_gate_up"] = _mm(h2, dgu, ta=True, out_dtype=BF16, name="mm_d_gate_up", tn=512)
    dh2 = _mm(dgu, w["w_gate_up"].T, name="mm_dh2", tn=1024, tk=1408)
    dx1, dg_ffn = _rms_bwd(x1, g_ffn, dh2, dx2, name="rms_ffn_bwd")
    gw["w_out"] = _mm(merged, dx1, ta=True, out_dtype=BF16, name="mm_d_out", tn=512)
    dmerged = _mm(dx1, w["w_out"].T, name="mm_dmerged", tn=1024)

    def merge_bwd(ga, gb, ta, tb, dm):
        _, vjp = jax.vjp(_merge, ga, gb, ta, tb)
        return list(vjp(dm)), []

    dga, dgb, dta, dtb = _rowwise(merge_bwd, gate_rows + [(dmerged, half, 0, 1)], [], [(D, BF16, half, 0, 1)] * 4,
                                  name="merge_bwd", tr=512, ncol=2)
    gw["w_branch_a"] = _mm(y_a, dta, ta=True, out_dtype=BF16, name="mm_d_branch_a", tn=1024)
    gw["w_branch_b"] = _mm(y_b, dtb, ta=True, out_dtype=BF16, name="mm_d_branch_b", tn=1024)
    dya = _mm(dta, w["w_branch_a"].T, name="mm_dya", tn=512)
    dyb = _mm(dtb, w["w_branch_b"].T, name="mm_dyb", tn=512)

    w_onorm = w_onorm + send_grads(gw)[0, 0]

    def gated_norm_bwd(o, z, dy, wn):
        _, vjp = jax.vjp(_gated_norm, o, z, wn)
        do, dz, dwn = vjp(dy)
        return [do, dz], [dwn]

    do_b, dz, dw_onorm = _rowwise(
        gated_norm_bwd, [(o_b, LANE, 0, 1), (projp, LANE, P_Z // LANE, 1), (dyb, LANE, 0, 1)], [_full(w_onorm)],
        [(B_HEADS * B_DIM, F32, LANE, 0, 1), (B_HEADS * B_DIM, BF16, LANE, 0, 1)], [(B_DIM, B_DIM, 0)],
        name="gated_norm_bwd", tr=1024, ncol=B_HEADS)
    dconv_out, dbg = _delta_bwd(conv, bg, states, do_b, n_batch, seq)

    def gate_scalars_bwd(raw, dbgb, al, dtb):
        _, vjp = jax.vjp(_gate_scalars, raw, al, dtb)
        draw, dal, ddtb = vjp(dbgb)
        return [draw], [dal, ddtb]

    dbd, dal_row, ddtb_row = _rowwise(gate_scalars_bwd, [(projp, LANE, bd_blk, 0), _full(dbg)], [_full(al_row), _full(dtb_row)],
                                      [(LANE, BF16, LANE, 0, 0)], [(LANE, LANE, 0), (LANE, LANE, 0)], name="gate_scalars_bwd",
                                      tr=1024)
    dconv, dconv_w = _conv_bwd(projp, conv_w, dconv_out, n_batch, seq)
    dq_a, dk_a, dv_a, dbias = _attn_bwd(projp, bias, dya, n_batch, seq)
    dprojp = jnp.concatenate([dq_a, dk_a, dv_a, dconv, dz, dga, dgb, dbd], axis=1)
    sent = send_w_in(_mm(h1, dprojp, ta=True, out_dtype=BF16, name="mm_d_in", tn=640))
    dh1 = _mm(dprojp, w_in.T, name="mm_dh1", tn=1024, tk=1152)
    grad_x, dg_mix = _rms_bwd(x, g_mix + sent[0, 0], dh1, dx1, name="rms_mix_bwd")

    _, bias_vjp = jax.vjp(_rel_bias_table, rel_bias)
    gs = {
        "g_mix": dg_mix, "g_ffn": dg_ffn, "g_ple": dg_ple, "g_final": dg_final, "w_onorm": dw_onorm,
        "conv_w": dconv_w, "rel_bias": bias_vjp(dbias)[0],
        "a_log": dal_row[0, B_HEADS:2 * B_HEADS], "dt_bias": ddtb_row[0, B_HEADS:2 * B_HEADS],
    }
    return loss_row[:, :1], grad_x.reshape(n_batch, seq, D), gs


MATRICES = (
    ("w_in", (D, D_IN), 1), ("w_gate_up", (D, 2 * D_FF), 1), ("w_branch_a", (A_WIDTH, D), 1), ("w_branch_b", (A_WIDTH, D), 1),
    ("w_out", (D, D), 0), ("w_down", (D_FF, D), 0), ("w_ple_gate", (D, D), 0), ("w_ple_proj", (256, D), 1),
)
TAPS_PER_SHARD = B_CONV // N_DEV


def _from_gathered(slabs, axis):
    _, rows, cols = slabs.shape
    if axis == 0:
        return slabs.reshape(N_DEV * rows, cols)
    return jnp.transpose(slabs, (1, 0, 2)).reshape(rows, N_DEV * cols)


def _to_owner(full, axis):
    rows, cols = full.shape
    if axis == 0:
        return full.reshape(N_DEV, rows // N_DEV, cols)
    return jnp.transpose(full.reshape(rows, N_DEV, cols // N_DEV), (1, 0, 2))


def _permute_w_in(w_in):
    zeros = jnp.zeros((D, P_END - D_IN), w_in.dtype)
    return jnp.concatenate([w_in[:, :P_GATE], w_in[:, P_GATE + 8:], w_in[:, P_GATE:P_GATE + 8], zeros], axis=1)


def _unpermute_w_in(gp):
    return jnp.concatenate([gp[:, :P_GATE], gp[:, P_BD:P_BD + 8], gp[:, P_GATE:P_BD]], axis=1)


SMALL_ROWS = 16
SMALL_LAYOUT = (("g_mix", 0, D), ("g_ffn", 1, D), ("g_ple", 2, D), ("g_final", 3, D), ("conv_w", 4, 4 * B_CONV),
                ("rel_bias", 10, A_HEADS * (2 * REL_CLIP + 1)), ("w_onorm", 13, B_DIM), ("a_log", 14, B_HEADS),
                ("dt_bias", 14, B_HEADS), ("loss", 15, 1))


def _pack_small(gs):
    rows = {}
    for name, row, n in SMALL_LAYOUT:
        rows.setdefault(row, []).append(gs[name].reshape(-1).astype(F32))
    parts = []
    for row in sorted(rows):
        flat = jnp.concatenate(rows[row])
        parts.append(jnp.concatenate([flat, jnp.zeros((-flat.shape[0] % D,), F32)]))
    flat = jnp.concatenate(parts)
    assert flat.shape[0] == SMALL_ROWS * D, flat.shape
    return flat.reshape(SMALL_ROWS, D)


def _unpack_small(blk):
    flat, out, used = blk.reshape(-1), {}, {}
    for name, row, n in SMALL_LAYOUT:
        start = row * D + used.get(row, 0)
        out[name] = flat[start:start + n]
        used[row] = used.get(row, 0) + n
    return out


def _position():
    return lax.axis_index("x"), lax.axis_index("y"), lax.axis_index("c")


PEERS = N_DEV - 1


def _comm_call(body, arrays, out_shapes, *, name):
    n = len(arrays)
    return pl.pallas_call(
        body,
        name=name,
        out_shape=out_shapes,
        in_specs=[HBM_SPEC] * n,
        out_specs=[HBM_SPEC] * n,
        scratch_shapes=[pltpu.SemaphoreType.DMA((PEERS * n,)), pltpu.SemaphoreType.DMA((PEERS * n,)),
                        pltpu.SemaphoreType.DMA((n,))],
    )(*arrays)


def _weights_allgather(shards):
    n = len(shards)

    def body(*refs):
        ins, outs = refs[:n], refs[n:2 * n]
        send_sems, recv_sems, local_sems = refs[2 * n:]
        x, y, c = _position()
        me, sibling = (x, y, c), (x, y, 1 - c)
        chips = [(1 - x, y), (x, 1 - y), (1 - x, 1 - y)]

        def slab(a, px, py, pc):
            return outs[a].at[4 * px + 2 * py + pc]

        def copy(a, k, block, to, src=None):
            return pltpu.make_async_remote_copy(src_ref=slab(a, *block) if src is None else src, dst_ref=slab(a, *block),
                                                send_sem=send_sems.at[PEERS * a + k], recv_sem=recv_sems.at[PEERS * a + k],
                                                device_id=to, device_id_type=MESH)

        local = [pltpu.make_async_copy(ins[a], slab(a, *me), local_sems.at[a]) for a in range(n)]
        sent = [copy(a, 1 + j, me, (*chip, c), src=ins[a]) for a in range(n) for j, chip in enumerate(chips)]
        sent += [copy(a, 0, me, sibling, src=ins[a]) for a in range(n)]
        for cp in sent + local:
            cp.start()
        for a in range(n):
            for j, chip in enumerate(chips):
                copy(a, 1 + j, (*chip, c), me).wait_recv()
                passed = copy(a, 4 + j, (*chip, c), sibling)
                passed.start()
                sent.append(passed)
        for a in range(n):
            copy(a, 0, sibling, me).wait_recv()
            for j, chip in enumerate(chips):
                copy(a, 4 + j, (*chip, 1 - c), me).wait_recv()
        for cp in sent:
            cp.wait_send()
        for cp in local:
            cp.wait()

    return _comm_call(body, shards, [jax.ShapeDtypeStruct((N_DEV,) + s.shape, s.dtype) for s in shards],
                      name="weights_allgather")


def _grads_exchange(by_owner):
    n = len(by_owner)

    def body(*refs):
        ins, outs = refs[:n], refs[n:2 * n]
        send_sems, recv_sems, local_sems = refs[2 * n:]
        x, y, c = _position()
        mine = 4 * x + 2 * y + c
        local = [pltpu.make_async_copy(ins[a].at[mine], outs[a].at[mine], local_sems.at[a]) for a in range(n)]
        for cp in local:
            cp.start()
        flips = [(dx, dy, dc) for dx in (0, 1) for dy in (0, 1) for dc in (0, 1) if dx + dy + dc]
        pending = []
        for k, (dx, dy, dc) in enumerate(flips):
            px, py, pc = (1 - x if dx else x), (1 - y if dy else y), (1 - c if dc else c)
            peer = 4 * px + 2 * py + pc
            for a in range(n):
                def remote(slot):
                    return pltpu.make_async_remote_copy(src_ref=ins[a].at[peer], dst_ref=outs[a].at[slot],
                                                        send_sem=send_sems.at[PEERS * a + k], recv_sem=recv_sems.at[PEERS * a + k],
                                                        device_id=(px, py, pc), device_id_type=MESH)

                sent = remote(mine)
                sent.start()
                pending.append((sent, remote(peer)))
        for sent, landed in pending:
            landed.wait_recv()
            sent.wait_send()
        for cp in local:
            cp.wait()

    return _comm_call(body, by_owner, [jax.ShapeDtypeStruct(g.shape, g.dtype) for g in by_owner], name="grads_exchange")


SEM_SPEC = pl.BlockSpec(memory_space=pltpu.SEMAPHORE)
DATAFLOW = pltpu.SideEffectType.DATAFLOW_SIDE_EFFECTING


def _peer_copies(srcs, lands, send_sems, recv_sems, by_owner, arrival):
    x, y, c = _position()
    mine = 4 * x + 2 * y + c
    copies = []
    for k, (dx, dy, dc) in enumerate([(dx, dy, dc) for dx in (0, 1) for dy in (0, 1) for dc in (0, 1) if dx + dy + dc]):
        px, py, pc = (1 - x if dx else x), (1 - y if dy else y), (1 - c if dc else c)
        peer = 4 * px + 2 * py + pc
        for a, (src, land) in enumerate(zip(srcs, lands)):
            copies.append(pltpu.make_async_remote_copy(
                src_ref=src.at[peer] if by_owner else src, dst_ref=land.at[peer if arrival else mine],
                send_sem=send_sems.at[PEERS * a + k], recv_sem=recv_sems.at[PEERS * a + k],
                device_id=(px, py, pc), device_id_type=MESH))
    return copies


def _exchange_start(sources, by_owner, *, name):
    n = len(sources)
    lands = [lax.empty((N_DEV,) + (s.shape[1:] if by_owner else s.shape), s.dtype) for s in sources]

    def body(*refs):
        send_sems, recv_sems, token = refs[2 * n], refs[2 * n + 1], refs[-1]
        for copy in _peer_copies(refs[:n], refs[n:2 * n], send_sems, recv_sems, by_owner, arrival=False):
            copy.start()
        token[...] = jnp.zeros_like(token)

    sems = pltpu.SemaphoreType.DMA((PEERS * n,))
    outs = pl.pallas_call(
        body,
        name=name,
        out_shape=(sems, sems, *[pltpu.HBM(a.shape, a.dtype) for a in sources + lands], jax.ShapeDtypeStruct((8, LANE), F32)),
        in_specs=[HBM_SPEC] * (2 * n),
        out_specs=(SEM_SPEC, SEM_SPEC, *[HBM_SPEC] * (2 * n), pl.BlockSpec(memory_space=pltpu.VMEM)),
        input_output_aliases={i: 2 + i for i in range(2 * n)},
        compiler_params=pltpu.CompilerParams(has_side_effects=DATAFLOW),
    )(*[pltpu.with_memory_space_constraint(a, pltpu.HBM) for a in sources + lands])
    return outs[:-1], outs[-1]


def _exchange_wait(started, after, by_owner, *, name):
    send_sems, recv_sems, *arrays = started
    n = len(arrays) // 2

    def body(*refs):
        for copy in _peer_copies(refs[:n], refs[n:2 * n], refs[2 * n], refs[2 * n + 1], by_owner, arrival=True):
            copy.wait_send()
            copy.wait_recv()

    outs = pl.pallas_call(
        body,
        name=name,
        out_shape=[pltpu.HBM(a.shape, a.dtype) for a in arrays],
        in_specs=[HBM_SPEC] * (2 * n) + [SEM_SPEC, SEM_SPEC, pl.BlockSpec(memory_space=pl.ANY)],
        out_specs=[HBM_SPEC] * (2 * n),
        input_output_aliases={i: i for i in range(2 * n)},
        compiler_params=pltpu.CompilerParams(has_side_effects=DATAFLOW),
    )(*arrays, send_sems, recv_sems, after)
    return outs[:n], outs[n:]


def _sum_slots(got, *, name, tr):
    _, rows, cols = got.shape
    tr = _tile(rows, tr, 16)

    def body(g_ref, o_ref):
        acc = g_ref[0].astype(F32)
        for j in range(1, N_DEV):
            acc = acc + g_ref[j].astype(F32)
        o_ref[...] = acc

    return pl.pallas_call(
        body,
        name=name,
        grid=(rows // tr,),
        in_specs=[pl.BlockSpec((N_DEV, tr, cols), lambda i: (0, i, 0))],
        out_specs=pl.BlockSpec((tr, cols), lambda i: (i, 0)),
        out_shape=jax.ShapeDtypeStruct((rows, cols), F32),
        compiler_params=pltpu.CompilerParams(dimension_semantics=("parallel",)),
    )(got)


def _adamw(wt, g, m, v, *, name, own=None):
    slots = own is not None
    shape = wt.shape
    two_d = (-1, shape[-1]) if wt.ndim > 1 else (1, -1)
    args = [a.reshape(two_d) for a in (wt, m, v)]
    rows, cols = args[0].shape
    tr = _tile(rows, 256, 16) if rows % 16 == 0 else rows
    args.insert(1, g.reshape((N_DEV, rows, cols) if slots else (rows, cols)))
    if slots:
        args.append(own.reshape(rows, cols))

    def body(w_ref, g_ref, m_ref, v_ref, *refs):
        go_ref, d_ref, nm_ref, nv_ref = refs[-4:]
        if slots:
            x, y, c = _position()
            mine = 4 * x + 2 * y + c
            gv = None
            for j in range(N_DEV):
                part = jnp.where(mine == j, refs[0][...], g_ref[j]).astype(F32)
                gv = part if gv is None else gv + part
        else:
            gv = g_ref[...]
        go_ref[...] = gv
        m2 = ADAM_B1 * m_ref[...] + (1.0 - ADAM_B1) * gv
        v2 = ADAM_B2 * v_ref[...] + (1.0 - ADAM_B2) * (gv * gv)
        m_hat = m2 / (1.0 - ADAM_B1 ** ADAM_STEP)
        v_hat = v2 / (1.0 - ADAM_B2 ** ADAM_STEP)
        d_ref[...] = -ADAM_LR * (m_hat / (jnp.sqrt(v_hat) + ADAM_EPS) + ADAM_WD * w_ref[...])
        nm_ref[...] = m2
        nv_ref[...] = v2

    spec = pl.BlockSpec((tr, cols), lambda i: (i, 0))
    g_spec = pl.BlockSpec((N_DEV, tr, cols), lambda i: (0, i, 0)) if slots else spec
    outs = pl.pallas_call(
        body,
        name=name,
        grid=(rows // tr,),
        in_specs=[spec, g_spec, spec, spec] + ([spec] if slots else []),
        out_specs=[spec] * 4,
        out_shape=[jax.ShapeDtypeStruct((rows, cols), F32)] * 4,
        compiler_params=pltpu.CompilerParams(dimension_semantics=("parallel",)),
    )(*args)
    return tuple(o.reshape(shape) for o in outs)


WEIGHTS = ("g_mix", "w_in", "conv_w", "a_log", "dt_bias", "rel_bias", "w_onorm", "w_branch_a", "w_branch_b", "w_out", "g_ffn",
           "w_gate_up", "w_down", "g_ple", "w_ple_gate", "w_ple_proj", "g_final")


def kernel(x, p, g_mix, w_in, conv_w, a_log, dt_bias, rel_bias, w_onorm, w_branch_a, w_branch_b, w_out, g_ffn, w_gate_up, w_down, g_ple, w_ple_gate, w_ple_proj, g_final, loss_target, m_g_mix, m_w_in, m_conv_w, m_a_log, m_dt_bias, m_rel_bias, m_w_onorm, m_w_branch_a, m_w_branch_b, m_w_out, m_g_ffn, m_w_gate_up, m_w_down, m_g_ple, m_w_ple_gate, m_w_ple_proj, m_g_final, v_g_mix, v_w_in, v_conv_w, v_a_log, v_dt_bias, v_rel_bias, v_w_onorm, v_w_branch_a, v_w_branch_b, v_w_out, v_g_ffn, v_w_gate_up, v_w_down, v_g_ple, v_w_ple_gate, v_w_ple_proj, v_g_final):
    given = dict(g_mix=g_mix, w_in=w_in, conv_w=conv_w, a_log=a_log, dt_bias=dt_bias, rel_bias=rel_bias, w_onorm=w_onorm,
                 w_branch_a=w_branch_a, w_branch_b=w_branch_b, w_out=w_out, g_ffn=g_ffn, w_gate_up=w_gate_up, w_down=w_down,
                 g_ple=g_ple, w_ple_gate=w_ple_gate, w_ple_proj=w_ple_proj, g_final=g_final)
    mom1 = dict(g_mix=m_g_mix, w_in=m_w_in, conv_w=m_conv_w, a_log=m_a_log, dt_bias=m_dt_bias, rel_bias=m_rel_bias,
                w_onorm=m_w_onorm, w_branch_a=m_w_branch_a, w_branch_b=m_w_branch_b, w_out=m_w_out, g_ffn=m_g_ffn,
                w_gate_up=m_w_gate_up, w_down=m_w_down, g_ple=m_g_ple, w_ple_gate=m_w_ple_gate, w_ple_proj=m_w_ple_proj,
                g_final=m_g_final)
    mom2 = dict(g_mix=v_g_mix, w_in=v_w_in, conv_w=v_conv_w, a_log=v_a_log, dt_bias=v_dt_bias, rel_bias=v_rel_bias,
                w_onorm=v_w_onorm, w_branch_a=v_w_branch_a, w_branch_b=v_w_branch_b, w_out=v_w_out, g_ffn=v_g_ffn,
                w_gate_up=v_w_gate_up, w_down=v_w_down, g_ple=v_g_ple, w_ple_gate=v_w_ple_gate, w_ple_proj=v_w_ple_proj,
                g_final=v_g_final)
    mine = 4 * lax.axis_index("x") + 2 * lax.axis_index("y") + lax.axis_index("c")

    my_slot = (jnp.arange(N_DEV) == mine)[:, None, None]
    rest = MATRICES[1:]
    in_flight = {}

    got_in, got_taps = _weights_allgather([w_in[0].astype(BF16), conv_w[0]])
    in_flight["weights"], weights_sent = _exchange_start([given[name][0].astype(BF16) for name, _, _ in rest], False,
                                                         name="weights_start")
    small = dict(g_mix=g_mix + weights_sent[0, 0], g_ffn=g_ffn, g_ple=g_ple, g_final=g_final, w_onorm=w_onorm, a_log=a_log,
                 dt_bias=dt_bias, rel_bias=rel_bias, conv_w=_from_gathered(got_taps, 1))

    def rest_weights(after):
        shards, landed = _exchange_wait(in_flight.pop("weights"), after, False, name="weights_wait")
        return {name: _from_gathered(jnp.where(my_slot, shard[None], slabs), axis)
                for (name, _, axis), shard, slabs in zip(rest, shards, landed)}

    def send_grads(gw):
        in_flight["grads"], sent = _exchange_start([_to_owner(gw[name], axis) for name, _, axis in rest], True, name="grads_start")
        return sent

    def send_w_in(g_in):
        in_flight["grad_in"], sent = _exchange_start([_to_owner(_unpermute_w_in(g_in), 1)], True, name="grad_in_start")
        return sent

    loss_part, grad_x, gs = _local_step(x, p[0], loss_target, _permute_w_in(_from_gathered(got_in, 1)), small, rest_weights,
                                        send_grads, send_w_in)
    gs["loss"] = loss_part

    mine_rest, got_rest = _exchange_wait(in_flight["grads"], grad_x, True, name="grads_wait")
    mine_in, got_in_grad = _exchange_wait(in_flight["grad_in"], grad_x, True, name="grad_in_wait")
    names = [name for name, _, _ in rest] + ["w_in"]
    grads, own = dict(zip(names, got_rest + got_in_grad)), dict(zip(names, mine_rest + mine_in))
    (got_small,) = _grads_exchange([jnp.broadcast_to(_pack_small(gs), (N_DEV, SMALL_ROWS, D))])
    small_sum = _unpack_small(_sum_slots(got_small, name="sum_small_grads", tr=16))
    loss = small_sum.pop("loss")[0]
    conv_all = small_sum.pop("conv_w").reshape(4, N_DEV, TAPS_PER_SHARD)
    small_sum["conv_w"] = lax.dynamic_index_in_dim(conv_all, mine, axis=1, keepdims=False)

    out_g, out_d, out_m, out_v = [], [], [], []
    for name in WEIGHTS:
        if name in grads:
            mine_of = lax.dynamic_index_in_dim(own[name], mine, axis=0, keepdims=False)
            g, delta, new_m, new_v = _adamw(given[name], grads[name], mom1[name], mom2[name], name=f"adamw_{name}", own=mine_of)
        else:
            g, delta, new_m, new_v = _adamw(given[name], small_sum[name].reshape(given[name].shape), mom1[name], mom2[name],
                                            name=f"adamw_{name}")
        out_g.append(g)
        out_d.append(delta)
        out_m.append(new_m)
        out_v.append(new_v)
    return (loss, grad_x, *out_g, *out_d, *out_m, *out_v)
```

```python
import jax
import jax.numpy as jnp
from jax import lax
from jax.experimental import pallas as pl
from jax.experimental.pallas import tpu as pltpu

F32 = jnp.float32
BF16 = jnp.bfloat16
DELTA_PREC = lax.Precision.HIGH
MESH = pl.DeviceIdType.MESH

N_DEV = 8
D = 1024
CHUNK = 64
EPS = 1e-6
A_HEADS, A_DIM, A_WIDTH = 8, 64, 512
A_BAND = 9 * CHUNK
A_PAD = 8 * CHUNK
REL_CLIP = 128
B_HEADS, B_DIM = 4, 128
B_CONV = 1536
D_FF = 2816
D_IN = 5640
P_CONV, P_Z, P_GATE, P_BD, P_END = 1536, 3072, 3584, 5632, 5760
LANE = 128

ADAM_LR, ADAM_B1, ADAM_B2, ADAM_EPS, ADAM_WD, ADAM_STEP = 0.001, 0.9, 0.999, 1e-08, 0.01, 10

NT = (((1,), (1,)), ((), ()))
TN = (((0,), (0,)), ((), ()))
NN = (((1,), (0,)), ((), ()))

HBM_SPEC = pl.BlockSpec(memory_space=pltpu.HBM)


def _tile(n, target, align=LANE):
    if n <= target:
        return n
    best = None
    for t in range(align, target + 1, align):
        if n % t == 0:
            best = t
    assert best is not None, (n, target, align)
    return best


def _mm(a, b, *, name, ta=False, add=None, out_dtype=F32, tm=1024, tn=640, tk=None):
    if ta:
        k_dim, m_dim = a.shape
    else:
        m_dim, k_dim = a.shape
    assert b.shape[0] == k_dim
    n_dim = b.shape[1]
    tm, tn = _tile(m_dim, tm), _tile(n_dim, tn)
    tk = _tile(k_dim, tk or (4096 if ta else 1024), 8 if ta else LANE)
    nk = k_dim // tk
    dn = TN if ta else NN

    def body(*refs):
        if add is None:
            a_ref, b_ref, o_ref = refs[:3]
            add_ref = None
        else:
            a_ref, b_ref, add_ref, o_ref = refs[:4]
        part = lax.dot_general(a_ref[...].astype(BF16), b_ref[...].astype(BF16), dn, preferred_element_type=F32)

        def finish(r):
            if add_ref is not None:
                r = r + add_ref[...]
            o_ref[...] = r.astype(o_ref.dtype)

        if nk == 1:
            finish(part)
        else:
            acc_ref = refs[-1]
            k = pl.program_id(2)

            @pl.when(k == 0)
            def _():
                acc_ref[...] = part

            @pl.when(k > 0)
            def _():
                acc_ref[...] += part

            @pl.when(k == nk - 1)
            def _():
                finish(acc_ref[...])

    a_spec = pl.BlockSpec((tk, tm), lambda i, j, k: (k, i)) if ta else pl.BlockSpec((tm, tk), lambda i, j, k: (i, k))
    in_specs = [a_spec, pl.BlockSpec((tk, tn), lambda i, j, k: (k, j))]
    args = [a, b]
    if add is not None:
        in_specs.append(pl.BlockSpec((tm, tn), lambda i, j, k: (i, j)))
        args.append(add)
    return pl.pallas_call(
        body,
        name=name,
        grid=(m_dim // tm, n_dim // tn, nk),
        in_specs=in_specs,
        out_specs=pl.BlockSpec((tm, tn), lambda i, j, k: (i, j)),
        out_shape=jax.ShapeDtypeStruct((m_dim, n_dim), out_dtype),
        scratch_shapes=[pltpu.VMEM((tm, tn), F32)] if nk > 1 else [],
        compiler_params=pltpu.CompilerParams(dimension_semantics=("parallel", "parallel", "arbitrary")),
    )(*args)


def _rowwise(fn, rows, bcs, outs, reds=(), *, name, tr, ncol=1):
    n_rows = rows[0][0].shape[0]
    tr = _tile(n_rows, tr, 8)
    nrow = n_rows // tr
    n_in, n_out = len(rows) + len(bcs), len(outs)

    def body(*refs):
        j, i = pl.program_id(0), pl.program_id(1)
        o_vals, r_vals = fn(*[r[...] for r in refs[:n_in]])
        for ref, val in zip(refs[n_in:n_in + n_out], o_vals):
            ref[...] = val.astype(ref.dtype)
        for ref, val, (_, _, stride) in zip(refs[n_in + n_out:], r_vals, reds):
            first = (i == 0) if stride else jnp.logical_and(i == 0, j == 0)

            @pl.when(first)
            def _():
                ref[...] = val

            @pl.when(jnp.logical_not(first))
            def _():
                ref[...] += val

    def spec(r, w, off, st, row_dep=True):
        if row_dep:
            return pl.BlockSpec((r, w), lambda j, i: (i, off + st * j))
        return pl.BlockSpec((r, w), lambda j, i: (0, off + st * j))

    in_specs = [spec(tr, w, off, st) for (_, w, off, st) in rows]
    in_specs += [spec(a.shape[0], w, off, st, False) for (a, w, off, st) in bcs]
    out_specs = [spec(tr, w, off, st) for (_, _, w, off, st) in outs]
    out_specs += [spec(1, w, 0, st, False) for (_, w, st) in reds]
    out_shape = [jax.ShapeDtypeStruct((n_rows, c), dt) for (c, dt, _, _, _) in outs]
    out_shape += [jax.ShapeDtypeStruct((1, c), F32) for (c, _, _) in reds]
    return pl.pallas_call(
        body,
        name=name,
        grid=(ncol, nrow),
        in_specs=in_specs,
        out_specs=out_specs,
        out_shape=out_shape,
        compiler_params=pltpu.CompilerParams(dimension_semantics=("arbitrary", "arbitrary")),
    )(*[r[0] for r in rows], *[b[0] for b in bcs])


def _full(a):
    return (a, a.shape[1], 0, 0)


def _rms(x, g):
    return x * lax.rsqrt(jnp.mean(x * x, axis=-1, keepdims=True) + EPS) * g


def _silu(x):
    return x * jax.nn.sigmoid(x)


def _softplus(x):
    return jnp.maximum(x, 0.0) + jnp.log(1.0 + jnp.exp(-jnp.abs(x)))


def _rms_fwd(x, g, *, name):
    (h,) = _rowwise(lambda xb, gb: ([_rms(xb, gb)], []), [_full(x)], [_full(g)], [(D, BF16, D, 0, 0)], name=name, tr=512)
    return h


def _rms_bwd(x, g, dh, dres, *, name):
    def fn(xb, dhb, dresb, gb):
        _, vjp = jax.vjp(_rms, xb, gb)
        dx, dg = vjp(dhb)
        return [dx + dresb], [dg]

    return _rowwise(fn, [_full(x), _full(dh), _full(dres)], [_full(g)], [(D, F32, D, 0, 0)], [(D, D, 0)], name=name, tr=256)


def _gate_scalars(raw, al_row, dtb_row):
    lane = lax.broadcasted_iota(jnp.int32, raw.shape, 1)
    beta = jax.nn.sigmoid(raw)
    g = -jnp.exp(al_row) * _softplus(raw + dtb_row)
    return jnp.where(lane < B_HEADS, beta, jnp.where(lane < 2 * B_HEADS, g, 0.0))


def _gated_norm(o, z, w):
    return _rms(o, w) * _silu(z)


def _merge(ga, gb, ta, tb):
    return jax.nn.sigmoid(ga) * ta + jax.nn.sigmoid(gb) * tb


def _swiglu(gu):
    return _silu(gu[:, :D_FF]) * gu[:, D_FF:]


def _head_loss(x2, pg, pp, tgt, g):
    x3 = x2 + jax.nn.sigmoid(pg) * pp
    err = _rms(x3, g) - tgt
    return 0.5 * jnp.sum(jnp.mean(err * err, axis=-1))


CONV_W = 256


def _conv_taps(x, w):
    row = lax.broadcasted_iota(jnp.int32, x.shape, 0)
    shifted = [x] + [jnp.where(row >= s, pltpu.roll(x, s, 0), 0.0) for s in (1, 2, 3)]
    pre = shifted[0] * w[3:4]
    for s in (1, 2, 3):
        pre = pre + shifted[s] * w[3 - s:4 - s]
    return pre, shifted


def _conv_fwd(projp, conv_w, n_batch, seq):
    ncol = B_CONV // CONV_W
    first = P_CONV // CONV_W

    def body(x_ref, w_ref, o_ref):
        pre, _ = _conv_taps(x_ref[...], w_ref[...])
        o_ref[...] = _silu(pre)

    return pl.pallas_call(
        body,
        name="conv_fwd",
        grid=(ncol, n_batch),
        in_specs=[pl.BlockSpec((seq, CONV_W), lambda j, b: (b, first + j)), pl.BlockSpec((4, CONV_W), lambda j, b: (0, j))],
        out_specs=pl.BlockSpec((seq, CONV_W), lambda j, b: (b, j)),
        out_shape=jax.ShapeDtypeStruct((n_batch * seq, B_CONV), F32),
        compiler_params=pltpu.CompilerParams(dimension_semantics=("parallel", "parallel")),
    )(projp, conv_w)


def _conv_bwd(projp, conv_w, dc, n_batch, seq):
    width = dc.shape[1]
    ncol = width // CONV_W
    first_x = P_CONV // CONV_W

    def body(x_ref, w_ref, dc_ref, dx_ref, dw_ref):
        b = pl.program_id(1)
        w = w_ref[...]
        pre, shifted = _conv_taps(x_ref[...], w)
        sg = jax.nn.sigmoid(pre)
        dpre = dc_ref[...] * (sg * (1.0 + pre * (1.0 - sg)))
        row = lax.broadcasted_iota(jnp.int32, dpre.shape, 0)
        dx = dpre * w[3:4]
        for s in (1, 2, 3):
            dx = dx + jnp.where(row < seq - s, pltpu.roll(dpre, seq - s, 0), 0.0) * w[3 - s:4 - s]
        dx_ref[...] = dx.astype(dx_ref.dtype)
        for s in (0, 1, 2, 3):
            part = jnp.sum(dpre * shifted[s], axis=0, keepdims=True)

            @pl.when(b == 0)
            def _():
                dw_ref[3 - s:4 - s, :] = part

            @pl.when(b > 0)
            def _():
                dw_ref[3 - s:4 - s, :] += part

    return pl.pallas_call(
        body,
        name="conv_bwd",
        grid=(ncol, n_batch),
        in_specs=[
            pl.BlockSpec((seq, CONV_W), lambda j, b: (b, first_x + j)),
            pl.BlockSpec((4, CONV_W), lambda j, b: (0, j)),
            pl.BlockSpec((seq, CONV_W), lambda j, b: (b, j)),
        ],
        out_specs=[pl.BlockSpec((seq, CONV_W), lambda j, b: (b, j)), pl.BlockSpec((4, CONV_W), lambda j, b: (0, j))],
        out_shape=[jax.ShapeDtypeStruct((n_batch * seq, width), BF16), jax.ShapeDtypeStruct((4, width), F32)],
        compiler_params=pltpu.CompilerParams(dimension_semantics=("arbitrary", "arbitrary")),
    )(projp, conv_w, dc)


def _attn_chunk(qc, kb, vb, bias2, valid, lane_lo):
    sel = (lane_lo, jnp.logical_not(lane_lo))
    items = [(i, e) for i in range(len(qc)) for e in (0, 1)]
    k16, v16 = [t.astype(BF16) for t in kb], [t.astype(BF16) for t in vb]
    qm = [(jnp.where(sel[e], qc[i], 0.0) * (A_DIM ** -0.5)).astype(BF16) for i, e in items]
    s = [lax.dot_general(qm[n], k16[i], NT, preferred_element_type=F32) + bias2[e] for n, (i, e) in enumerate(items)]
    s = [jnp.where(valid[i], s[n], -1e30) for n, (i, e) in enumerate(items)]
    p = [jnp.exp(t - lax.stop_gradient(jnp.max(t, axis=-1, keepdims=True))) for t in s]
    p = [t * (1.0 / jnp.sum(t, axis=-1, keepdims=True)) for t in p]
    o = [jnp.where(sel[e], jnp.dot(p[n].astype(BF16), v16[i], preferred_element_type=F32), 0.0)
         for n, (i, e) in enumerate(items)]
    return [o[2 * i] + o[2 * i + 1] for i in range(len(qc))]


ATTN_GROUP_FWD, ATTN_GROUP_BWD = 4, 4


def _attn_group(g, group, q_ref, kp_ref, vp_ref):
    col = lax.broadcasted_iota(jnp.int32, (CHUNK, A_BAND), 1)
    lane_lo = lax.broadcasted_iota(jnp.int32, (1, LANE), 1) < A_DIM
    starts = [pl.multiple_of((g * group + i) * CHUNK, CHUNK) for i in range(group)]
    rows = [pl.ds(r0, CHUNK) for r0 in starts]
    bands = [pl.ds(r0, A_BAND) for r0 in starts]
    valid = [col + r0 >= A_PAD for r0 in starts]
    loaded = [q_ref[r, :] for r in rows], [kp_ref[b, :] for b in bands], [vp_ref[b, :] for b in bands]
    return rows, bands, loaded, valid, lane_lo


def _attn_specs(seq):
    def blk(first):
        return pl.BlockSpec((seq, LANE), lambda hp, b: (b, first + hp))

    return blk, pl.BlockSpec((2, CHUNK, A_BAND), lambda hp, b: (hp, 0, 0))


def _attn_fwd(projp, bias, n_batch, seq):
    nc = seq // CHUNK
    blk, bias_spec = _attn_specs(seq)

    def body(q_ref, k_ref, v_ref, b_ref, o_ref, kp_ref, vp_ref):
        kp_ref[0:A_PAD, :] = jnp.zeros((A_PAD, LANE), F32)
        vp_ref[0:A_PAD, :] = jnp.zeros((A_PAD, LANE), F32)
        kp_ref[A_PAD:, :] = k_ref[...]
        vp_ref[A_PAD:, :] = v_ref[...]
        bias2 = b_ref[...]

        def step(g, carry):
            rows, _, (qc, kb, vb), valid, lane_lo = _attn_group(g, ATTN_GROUP_FWD, q_ref, kp_ref, vp_ref)
            out = _attn_chunk(qc, kb, vb, bias2, valid, lane_lo)
            for r, o in zip(rows, out):
                o_ref[r, :] = o.astype(o_ref.dtype)
            return carry

        lax.fori_loop(0, nc // ATTN_GROUP_FWD, step, 0)

    return pl.pallas_call(
        body,
        name="attn_fwd",
        grid=(A_HEADS // 2, n_batch),
        in_specs=[blk(0), blk(4), blk(8), bias_spec],
        out_specs=pl.BlockSpec((seq, LANE), lambda hp, b: (b, hp)),
        out_shape=jax.ShapeDtypeStruct((n_batch * seq, A_WIDTH), BF16),
        scratch_shapes=[pltpu.VMEM((A_PAD + seq, LANE), F32), pltpu.VMEM((A_PAD + seq, LANE), F32)],
        compiler_params=pltpu.CompilerParams(dimension_semantics=("parallel", "parallel")),
    )(projp, projp, projp, bias)


def _attn_bwd(projp, bias, dy, n_batch, seq):
    nc = seq // CHUNK
    blk, bias_spec = _attn_specs(seq)
    out_blk = pl.BlockSpec((seq, LANE), lambda hp, b: (b, hp))

    def body(q_ref, k_ref, v_ref, b_ref, dy_ref, dq_ref, dk_ref, dv_ref, db_ref, kp_ref, vp_ref, dkp_ref, dvp_ref):
        b = pl.program_id(1)
        kp_ref[0:A_PAD, :] = jnp.zeros((A_PAD, LANE), F32)
        vp_ref[0:A_PAD, :] = jnp.zeros((A_PAD, LANE), F32)
        kp_ref[A_PAD:, :] = k_ref[...]
        vp_ref[A_PAD:, :] = v_ref[...]
        dkp_ref[...] = jnp.zeros_like(dkp_ref)
        dvp_ref[...] = jnp.zeros_like(dvp_ref)
        bias2 = b_ref[...]

        @pl.when(b == 0)
        def _():
            db_ref[...] = jnp.zeros_like(db_ref)

        def step(g, carry):
            rows, bands, (qc, kb, vb), valid, lane_lo = _attn_group(g, ATTN_GROUP_BWD, q_ref, kp_ref, vp_ref)
            _, vjp = jax.vjp(lambda q, k, v, bb: _attn_chunk(q, k, v, bb, valid, lane_lo), qc, kb, vb, bias2)
            dq, dk, dv, dbias = vjp([dy_ref[r, :] for r in rows])
            for i, r in enumerate(rows):
                dq_ref[r, :] = dq[i].astype(dq_ref.dtype)
            for i, band in enumerate(bands):
                dkp_ref[band, :] += dk[i]
                dvp_ref[band, :] += dv[i]
            db_ref[...] += dbias
            return carry

        lax.fori_loop(0, nc // ATTN_GROUP_BWD, step, 0)
        dk_ref[...] = dkp_ref[A_PAD:, :].astype(dk_ref.dtype)
        dv_ref[...] = dvp_ref[A_PAD:, :].astype(dv_ref.dtype)

    n_tok = n_batch * seq
    pad = pltpu.VMEM((A_PAD + seq, LANE), F32)
    return pl.pallas_call(
        body,
        name="attn_bwd",
        grid=(A_HEADS // 2, n_batch),
        in_specs=[blk(0), blk(4), blk(8), bias_spec, out_blk],
        out_specs=[out_blk, out_blk, out_blk, bias_spec],
        out_shape=[jax.ShapeDtypeStruct((n_tok, A_WIDTH), BF16)] * 3 + [jax.ShapeDtypeStruct((A_HEADS, CHUNK, A_BAND), F32)],
        scratch_shapes=[pad, pad, pad, pad],
        compiler_params=pltpu.CompilerParams(dimension_semantics=("arbitrary", "arbitrary")),
    )(projp, projp, projp, bias, dy)


def _rel_bias_table(rel_bias):
    span = CHUNK + A_BAND - 1
    near = REL_CLIP + CHUNK
    far = jnp.broadcast_to(rel_bias[:, 2 * REL_CLIP:], (A_HEADS, span - near))
    t = jnp.concatenate([rel_bias[:, 2 * REL_CLIP + 1 - near:], far], axis=1)
    u = jnp.concatenate([t[:, :A_BAND][:, ::-1], t[:, A_BAND:][:, ::-1]], axis=1)
    rolled = jnp.tile(u, (1, CHUNK))[:, :CHUNK * (span - 1)].reshape(A_HEADS, CHUNK, span - 1)
    return rolled[:, :, :A_BAND]


def _dot(a, b, dn=NN):
    return lax.dot_general(a, b, dn, precision=DELTA_PREC, preferred_element_type=F32)


def _each(fn, *lists):
    return [fn(*vals) for vals in zip(*lists)]


@jax.custom_vjp
def _saved_inverse(x, inv):
    return inv


def _saved_inverse_fwd(x, inv):
    return inv, inv


def _saved_inverse_bwd(inv, ct):
    return _dot(_dot(inv, ct, TN), inv, NT), jnp.zeros_like(inv)


_saved_inverse.defvjp(_saved_inverse_fwd, _saved_inverse_bwd)


def _delta_chunk(r_state, cq, ck, cv, beta, g, saved_inv=None):
    ii = lax.broadcasted_iota(jnp.int32, (CHUNK, CHUNK), 0)
    jj = lax.broadcasted_iota(jnp.int32, (CHUNK, CHUNK), 1)
    incl, strict, eye = ii >= jj, ii > jj, ii == jj
    q = _each(lambda t: t * lax.rsqrt(jnp.sum(t * t, axis=-1, keepdims=True) + EPS) * (B_DIM ** -0.5), cq)
    k = _each(lambda t: t * lax.rsqrt(jnp.sum(t * t, axis=-1, keepdims=True) + EPS), ck)
    g_b = _each(lambda t: jnp.broadcast_to(t, (CHUNK, CHUNK)), g)
    g_row = _each(lambda t: jnp.sum(jnp.where(eye, t, 0.0), axis=0, keepdims=True), g_b)
    gc_col = _each(lambda t: jnp.sum(jnp.where(incl, t, 0.0), axis=1, keepdims=True), g_row)
    gc_row = _each(lambda t: jnp.sum(jnp.where(ii <= jj, t, 0.0), axis=0, keepdims=True), g_b)
    decay = _each(lambda c, r: jnp.where(incl, jnp.exp(jnp.where(incl, c - r, 0.0)), 0.0), gc_col, gc_row)
    kk = _each(lambda t: _dot(t, t, NT), k)
    x = _each(lambda b, m, d: jnp.where(strict, -(b * m * d), 0.0), beta, kk, decay)
    if saved_inv is None:
        inv = _each(lambda t: jnp.where(eye, 1.0, 0.0) + t, x)
        pw = x
        for _ in range(5):
            pw = _each(lambda t: _dot(t, t), pw)
            inv = _each(lambda t, s: t + _dot(t, s), inv, pw)
    else:
        inv = _each(_saved_inverse, x, saved_inv)
    egc = _each(jnp.exp, gc_col)
    u = _each(lambda t, b, v: _dot(t, b * v), inv, beta, cv)
    wk = _each(lambda t, b, e, kh: _dot(t, (b * e) * kh), inv, beta, egc, k)
    pqk = _each(lambda qh, kh, d: _dot(qh, kh, NT) * d, q, k, decay)
    g_last = _each(lambda c: c[CHUNK - 1:CHUNK, :], gc_col)
    kdec = _each(lambda kh, gl, c: kh * jnp.exp(gl - c), k, g_last, gc_col)
    w = _each(lambda uh, wkh, r: uh - _dot(wkh, r), u, wk, r_state)
    o = _each(lambda e, qh, r, ph, wh: e * _dot(qh, r) + _dot(ph, wh), egc, q, r_state, pqk, w)
    r_new = _each(lambda gl, r, kd, wh: jnp.exp(gl) * r + _dot(kd, wh, TN), g_last, r_state, kdec, w)
    return o, r_new, inv


DELTA_BLK = 512


def _delta_blocks(n_batch, seq):
    nblk = seq // DELTA_BLK
    cpb = DELTA_BLK // CHUNK

    def rows(width, order):
        return pl.BlockSpec((DELTA_BLK, width), lambda b, i: (b * nblk + order(i), 0))

    def states(order, side):
        return pl.BlockSpec((cpb, B_HEADS, side, side), lambda b, i: (b * nblk + order(i), 0, 0, 0))

    return nblk, cpb, rows, states


def _head_cols(h):
    return [pl.ds(part * B_HEADS * B_DIM + h * B_DIM, B_DIM) for part in range(3)]


def _load_heads(c_ref, bg_ref, state_ref, rows):
    bg_c = bg_ref[rows, :]
    cols = [_head_cols(h) for h in range(B_HEADS)]
    return ([state_ref[h] for h in range(B_HEADS)], [c_ref[rows, c[0]] for c in cols], [c_ref[rows, c[1]] for c in cols],
            [c_ref[rows, c[2]] for c in cols], [bg_c[:, h:h + 1] for h in range(B_HEADS)],
            [bg_c[:, B_HEADS + h:B_HEADS + h + 1] for h in range(B_HEADS)])


def _delta_fwd(conv, bg, n_batch, seq):
    nblk, cpb, rows_spec, states_spec = _delta_blocks(n_batch, seq)

    def forward(i):
        return i

    def body(c_ref, bg_ref, o_ref, st_ref, inv_ref, r_ref):
        @pl.when(pl.program_id(1) == 0)
        def _():
            r_ref[...] = jnp.zeros_like(r_ref)

        def step(c, carry):
            rows = pl.ds(pl.multiple_of(c * CHUNK, CHUNK), CHUNK)
            args = _load_heads(c_ref, bg_ref, r_ref, rows)
            o, r_new, inv = _delta_chunk(*args)
            for h in range(B_HEADS):
                st_ref[c, h] = args[0][h]
                inv_ref[c, h] = inv[h]
                o_ref[rows, pl.ds(h * B_DIM, B_DIM)] = o[h]
            for h in range(B_HEADS):
                r_ref[h] = r_new[h]
            return carry

        lax.fori_loop(0, cpb, step, 0)

    n_tok = n_batch * seq
    return pl.pallas_call(
        body,
        name="delta_fwd",
        grid=(n_batch, nblk),
        in_specs=[rows_spec(B_CONV, forward), rows_spec(LANE, forward)],
        out_specs=[rows_spec(B_HEADS * B_DIM, forward), states_spec(forward, B_DIM), states_spec(forward, CHUNK)],
        out_shape=[jax.ShapeDtypeStruct((n_tok, B_HEADS * B_DIM), F32),
                   jax.ShapeDtypeStruct((n_tok // CHUNK, B_HEADS, B_DIM, B_DIM), F32),
                   jax.ShapeDtypeStruct((n_tok // CHUNK, B_HEADS, CHUNK, CHUNK), F32)],
        scratch_shapes=[pltpu.VMEM((B_HEADS, B_DIM, B_DIM), F32)],
        compiler_params=pltpu.CompilerParams(dimension_semantics=("arbitrary", "arbitrary")),
    )(conv, bg)


def _delta_bwd(conv, bg, states, inverses, do, n_batch, seq):
    nblk, cpb, rows_spec, states_spec = _delta_blocks(n_batch, seq)

    def backward(i):
        return nblk - 1 - i

    def body(c_ref, bg_ref, st_ref, inv_ref, do_ref, dc_ref, dbg_ref, dr_ref):
        @pl.when(pl.program_id(1) == 0)
        def _():
            dr_ref[...] = jnp.zeros_like(dr_ref)

        def step(n, carry):
            c = cpb - 1 - n
            rows = pl.ds(pl.multiple_of(c * CHUNK, CHUNK), CHUNK)
            saved = [inv_ref[c, h] for h in range(B_HEADS)]
            _, vjp = jax.vjp(lambda *args: _delta_chunk(*args, saved_inv=saved)[:2],
                             *_load_heads(c_ref, bg_ref, st_ref.at[c], rows))
            do = [do_ref[rows, pl.ds(h * B_DIM, B_DIM)] for h in range(B_HEADS)]
            dr, dq, dk, dv, dbeta, dg = vjp((do, [dr_ref[h] for h in range(B_HEADS)]))
            lane = lax.broadcasted_iota(jnp.int32, (CHUNK, LANE), 1)
            dbg = jnp.zeros((CHUNK, LANE), F32)
            for h in range(B_HEADS):
                cq, ck, cv = _head_cols(h)
                dr_ref[h] = dr[h]
                dc_ref[rows, cq] = dq[h]
                dc_ref[rows, ck] = dk[h]
                dc_ref[rows, cv] = dv[h]
                dbg = dbg + jnp.where(lane == h, dbeta[h], 0.0) + jnp.where(lane == h + B_HEADS, dg[h], 0.0)
            dbg_ref[rows, :] = dbg
            return carry

        lax.fori_loop(0, cpb, step, 0)

    n_tok = n_batch * seq
    return pl.pallas_call(
        body,
        name="delta_bwd",
        grid=(n_batch, nblk),
        in_specs=[rows_spec(B_CONV, backward), rows_spec(LANE, backward), states_spec(backward, B_DIM),
                  states_spec(backward, CHUNK), rows_spec(B_HEADS * B_DIM, backward)],
        out_specs=[rows_spec(B_CONV, backward), rows_spec(LANE, backward)],
        out_shape=[jax.ShapeDtypeStruct((n_tok, B_CONV), F32), jax.ShapeDtypeStruct((n_tok, LANE), F32)],
        scratch_shapes=[pltpu.VMEM((B_HEADS, B_DIM, B_DIM), F32)],
        compiler_params=pltpu.CompilerParams(dimension_semantics=("arbitrary", "arbitrary")),
    )(conv, bg, states, inverses, do)


def _lane_row(vec4, first):
    return jnp.concatenate([jnp.zeros((1, first), F32), vec4.reshape(1, B_HEADS).astype(F32),
                            jnp.zeros((1, LANE - first - B_HEADS), F32)], axis=1)


def _local_step(x3d, p3d, tgt3d, w_in, small, rest_weights, send_grads, send_w_in):
    n_batch, seq, _ = x3d.shape
    n_tok = n_batch * seq
    x, p, tgt = x3d.reshape(n_tok, D), p3d.reshape(n_tok, -1), tgt3d.reshape(n_tok, D)
    g_mix, g_ffn, g_ple, g_final = (small[k].reshape(1, D) for k in ("g_mix", "g_ffn", "g_ple", "g_final"))
    w_onorm = small["w_onorm"].reshape(1, B_DIM)
    al_row = _lane_row(small["a_log"], B_HEADS)
    dtb_row = _lane_row(small["dt_bias"], B_HEADS)
    rel_bias = small["rel_bias"].reshape(A_HEADS, -1)
    bias = _rel_bias_table(rel_bias)
    conv_w = small["conv_w"].reshape(4, B_CONV)
    bd_blk = P_BD // LANE

    h1 = _rms_fwd(x, g_mix, name="rms_mix")
    projp = _mm(h1, w_in, name="mm_proj", tn=640)
    y_a = _attn_fwd(projp, bias, n_batch, seq)
    conv = _conv_fwd(projp, conv_w, n_batch, seq)
    (bg,) = _rowwise(lambda raw, al, dtb: ([_gate_scalars(raw, al, dtb)], []), [(projp, LANE, bd_blk, 0)],
                     [_full(al_row), _full(dtb_row)], [(LANE, F32, LANE, 0, 0)], name="gate_scalars", tr=1024)
    o_b, states, inverses = _delta_fwd(conv, bg, n_batch, seq)
    (y_b,) = _rowwise(lambda o, z, wn: ([_gated_norm(o, z, wn)], []), [(o_b, LANE, 0, 1), (projp, LANE, P_Z // LANE, 1)],
                      [_full(w_onorm)], [(B_HEADS * B_DIM, BF16, LANE, 0, 1)], name="gated_norm", tr=1024, ncol=B_HEADS)
    w = rest_weights(y_b)
    t_a = _mm(y_a, w["w_branch_a"], name="mm_branch_a", tn=1024)
    t_b = _mm(y_b, w["w_branch_b"], name="mm_branch_b", tn=1024)
    half = D // 2
    gate_rows = [(projp, half, P_GATE // half, 1), (projp, half, P_GATE // half + 2, 1), (t_a, half, 0, 1), (t_b, half, 0, 1)]
    (merged,) = _rowwise(lambda ga, gb, ta, tb: ([_merge(ga, gb, ta, tb)], []), gate_rows, [], [(D, BF16, half, 0, 1)],
                         name="merge", tr=512, ncol=2)
    x1 = _mm(merged, w["w_out"], add=x, name="mm_out", tn=1024)
    h2 = _rms_fwd(x1, g_ffn, name="rms_ffn")
    gu = _mm(h2, w["w_gate_up"], name="mm_gate_up", tn=512)
    (act,) = _rowwise(lambda gub: ([_swiglu(gub)], []), [_full(gu)], [], [(D_FF, BF16, D_FF, 0, 0)], name="swiglu", tr=256)
    x2 = _mm(act, w["w_down"], add=x1, name="mm_down", tn=1024, tk=1408)
    h3 = _rms_fwd(x2, g_ple, name="rms_ple")
    pg = _mm(h3, w["w_ple_gate"], name="mm_ple_gate", tn=1024)
    pp = _mm(p, w["w_ple_proj"], name="mm_ple_proj", tn=1024)

    def head_fn(x2b, pgb, ppb, tb, gb):
        loss, (dx2, dpg, dpp, dg) = jax.value_and_grad(_head_loss, argnums=(0, 1, 2, 4))(x2b, pgb, ppb, tb, gb)
        return [dx2, dpg, dpp], [dg, jnp.full((1, LANE), loss, F32)]

    dx3, dpg, dpp, dg_final, loss_row = _rowwise(
        head_fn, [_full(x2), _full(pg), _full(pp), _full(tgt)], [_full(g_final)],
        [(D, F32, D, 0, 0), (D, BF16, D, 0, 0), (D, BF16, D, 0, 0)], [(D, D, 0), (LANE, LANE, 0)], name="loss_head", tr=256)
    gw = {}
    gw["w_ple_proj"] = _mm(p, dpp, ta=True, out_dtype=BF16, name="mm_d_ple_proj", tm=256, tn=1024)
    gw["w_ple_gate"] = _mm(h3, dpg, ta=True, out_dtype=BF16, name="mm_d_ple_gate", tn=512)
    dh3 = _mm(dpg, w["w_ple_gate"].T, name="mm_dh3", tn=1024)
    dx2, dg_ple = _rms_bwd(x2, g_ple, dh3, dx3, name="rms_ple_bwd")
    gw["w_down"] = _mm(act, dx2, ta=True, out_dtype=BF16, name="mm_d_down", tm=1408, tn=512, tk=2048)
    dact = _mm(dx2, w["w_down"].T, name="mm_dact", tn=1408)

    def swiglu_bwd(gub, dab):
        _, vjp = jax.vjp(_swiglu, gub)
        return [vjp(dab)[0]], []

    (dgu,) = _rowwise(swiglu_bwd, [_full(gu), _full(dact)], [], [(2 * D_FF, BF16, 2 * D_FF, 0, 0)], name="swiglu_bwd", tr=256)
    gw["w_gate_up"] = _mm(h2, dgu, ta=True, out_dtype=BF16, name="mm_d_gate_up", tn=512)
    dh2 = _mm(dgu, w["w_gate_up"].T, name="mm_dh2", tn=1024, tk=1408)
    dx1, dg_ffn = _rms_bwd(x1, g_ffn, dh2, dx2, name="rms_ffn_bwd")
    gw["w_out"] = _mm(merged, dx1, ta=True, out_dtype=BF16, name="mm_d_out", tn=512)
    dmerged = _mm(dx1, w["w_out"].T, name="mm_dmerged", tn=1024)

    def merge_bwd(ga, gb, ta, tb, dm):
        _, vjp = jax.vjp(_merge, ga, gb, ta, tb)
        return list(vjp(dm)), []

    dga, dgb, dta, dtb = _rowwise(merge_bwd, gate_rows + [(dmerged, half, 0, 1)], [], [(D, BF16, half, 0, 1)] * 4,
                                  name="merge_bwd", tr=512, ncol=2)
    gw["w_branch_a"] = _mm(y_a, dta, ta=True, out_dtype=BF16, name="mm_d_branch_a", tn=1024)
    gw["w_branch_b"] = _mm(y_b, dtb, ta=True, out_dtype=BF16, name="mm_d_branch_b", tn=1024)
    dya = _mm(dta, w["w_branch_a"].T, name="mm_dya", tn=512)
    dyb = _mm(dtb, w["w_branch_b"].T, name="mm_dyb", tn=512)

    w_onorm = w_onorm + send_grads(gw)[0, 0]

    def gated_norm_bwd(o, z, dy, wn):
        _, vjp = jax.vjp(_gated_norm, o, z, wn)
        do, dz, dwn = vjp(dy)
        return [do, dz], [dwn]

    do_b, dz, dw_onorm = _rowwise(
        gated_norm_bwd, [(o_b, LANE, 0, 1), (projp, LANE, P_Z // LANE, 1), (dyb, LANE, 0, 1)], [_full(w_onorm)],
        [(B_HEADS * B_DIM, F32, LANE, 0, 1), (B_HEADS * B_DIM, BF16, LANE, 0, 1)], [(B_DIM, B_DIM, 0)],
        name="gated_norm_bwd", tr=1024, ncol=B_HEADS)
    dconv_out, dbg = _delta_bwd(conv, bg, states, inverses, do_b, n_batch, seq)

    def gate_scalars_bwd(raw, dbgb, al, dtb):
        _, vjp = jax.vjp(_gate_scalars, raw, al, dtb)
        draw, dal, ddtb = vjp(dbgb)
        return [draw], [dal, ddtb]

    dbd, dal_row, ddtb_row = _rowwise(gate_scalars_bwd, [(projp, LANE, bd_blk, 0), _full(dbg)], [_full(al_row), _full(dtb_row)],
                                      [(LANE, BF16, LANE, 0, 0)], [(LANE, LANE, 0), (LANE, LANE, 0)], name="gate_scalars_bwd",
                                      tr=1024)
    dconv, dconv_w = _conv_bwd(projp, conv_w, dconv_out, n_batch, seq)
    dq_a, dk_a, dv_a, dbias = _attn_bwd(projp, bias, dya, n_batch, seq)
    dprojp = jnp.concatenate([dq_a, dk_a, dv_a, dconv, dz, dga, dgb, dbd], axis=1)
    sent = send_w_in(_mm(h1, dprojp, ta=True, out_dtype=BF16, name="mm_d_in", tn=640))
    dh1 = _mm(dprojp, w_in.T, name="mm_dh1", tn=1024, tk=1152)
    grad_x, dg_mix = _rms_bwd(x, g_mix + sent[0, 0], dh1, dx1, name="rms_mix_bwd")

    _, bias_vjp = jax.vjp(_rel_bias_table, rel_bias)
    gs = {
        "g_mix": dg_mix, "g_ffn": dg_ffn, "g_ple": dg_ple, "g_final": dg_final, "w_onorm": dw_onorm,
        "conv_w": dconv_w, "rel_bias": bias_vjp(dbias)[0],
        "a_log": dal_row[0, B_HEADS:2 * B_HEADS], "dt_bias": ddtb_row[0, B_HEADS:2 * B_HEADS],
    }
    return loss_row[:, :1], grad_x.reshape(n_batch, seq, D), gs


MATRICES = (
    ("w_in", (D, D_IN), 1), ("w_gate_up", (D, 2 * D_FF), 1), ("w_branch_a", (A_WIDTH, D), 1), ("w_branch_b", (A_WIDTH, D), 1),
    ("w_out", (D, D), 0), ("w_down", (D_FF, D), 0), ("w_ple_gate", (D, D), 0), ("w_ple_proj", (256, D), 1),
)
TAPS_PER_SHARD = B_CONV // N_DEV


def _from_gathered(slabs, axis):
    _, rows, cols = slabs.shape
    if axis == 0:
        return slabs.reshape(N_DEV * rows, cols)
    return jnp.transpose(slabs, (1, 0, 2)).reshape(rows, N_DEV * cols)


def _to_owner(full, axis):
    rows, cols = full.shape
    if axis == 0:
        return full.reshape(N_DEV, rows // N_DEV, cols)
    return jnp.transpose(full.reshape(rows, N_DEV, cols // N_DEV), (1, 0, 2))


def _permute_w_in(w_in):
    zeros = jnp.zeros((D, P_END - D_IN), w_in.dtype)
    return jnp.concatenate([w_in[:, :P_GATE], w_in[:, P_GATE + 8:], w_in[:, P_GATE:P_GATE + 8], zeros], axis=1)


def _unpermute_w_in(gp):
    return jnp.concatenate([gp[:, :P_GATE], gp[:, P_BD:P_BD + 8], gp[:, P_GATE:P_BD]], axis=1)


SMALL_ROWS = 16
SMALL_LAYOUT = (("g_mix", 0, D), ("g_ffn", 1, D), ("g_ple", 2, D), ("g_final", 3, D), ("conv_w", 4, 4 * B_CONV),
                ("rel_bias", 10, A_HEADS * (2 * REL_CLIP + 1)), ("w_onorm", 13, B_DIM), ("a_log", 14, B_HEADS),
                ("dt_bias", 14, B_HEADS), ("loss", 15, 1))


def _pack_small(gs):
    rows = {}
    for name, row, n in SMALL_LAYOUT:
        rows.setdefault(row, []).append(gs[name].reshape(-1).astype(F32))
    parts = []
    for row in sorted(rows):
        flat = jnp.concatenate(rows[row])
        parts.append(jnp.concatenate([flat, jnp.zeros((-flat.shape[0] % D,), F32)]))
    flat = jnp.concatenate(parts)
    assert flat.shape[0] == SMALL_ROWS * D, flat.shape
    return flat.reshape(SMALL_ROWS, D)


def _unpack_small(blk):
    flat, out, used = blk.reshape(-1), {}, {}
    for name, row, n in SMALL_LAYOUT:
        start = row * D + used.get(row, 0)
        out[name] = flat[start:start + n]
        used[row] = used.get(row, 0) + n
    return out


def _position():
    return lax.axis_index("x"), lax.axis_index("y"), lax.axis_index("c")


PEERS = N_DEV - 1


def _comm_call(body, arrays, out_shapes, *, name):
    n = len(arrays)
    return pl.pallas_call(
        body,
        name=name,
        out_shape=out_shapes,
        in_specs=[HBM_SPEC] * n,
        out_specs=[HBM_SPEC] * n,
        scratch_shapes=[pltpu.SemaphoreType.DMA((PEERS * n,)), pltpu.SemaphoreType.DMA((PEERS * n,)),
                        pltpu.SemaphoreType.DMA((n,))],
    )(*arrays)


def _weights_allgather(shards):
    n = len(shards)

    def body(*refs):
        ins, outs = refs[:n], refs[n:2 * n]
        send_sems, recv_sems, local_sems = refs[2 * n:]
        x, y, c = _position()
        me, sibling = (x, y, c), (x, y, 1 - c)
        chips = [(1 - x, y), (x, 1 - y), (1 - x, 1 - y)]

        def slab(a, px, py, pc):
            return outs[a].at[4 * px + 2 * py + pc]

        def copy(a, k, block, to, src=None):
            return pltpu.make_async_remote_copy(src_ref=slab(a, *block) if src is None else src, dst_ref=slab(a, *block),
                                                send_sem=send_sems.at[PEERS * a + k], recv_sem=recv_sems.at[PEERS * a + k],
                                                device_id=to, device_id_type=MESH)

        local = [pltpu.make_async_copy(ins[a], slab(a, *me), local_sems.at[a]) for a in range(n)]
        sent = [copy(a, 1 + j, me, (*chip, c), src=ins[a]) for a in range(n) for j, chip in enumerate(chips)]
        sent += [copy(a, 0, me, sibling, src=ins[a]) for a in range(n)]
        for cp in sent + local:
            cp.start()
        for a in range(n):
            for j, chip in enumerate(chips):
                copy(a, 1 + j, (*chip, c), me).wait_recv()
                passed = copy(a, 4 + j, (*chip, c), sibling)
                passed.start()
                sent.append(passed)
        for a in range(n):
            copy(a, 0, sibling, me).wait_recv()
            for j, chip in enumerate(chips):
                copy(a, 4 + j, (*chip, 1 - c), me).wait_recv()
        for cp in sent:
            cp.wait_send()
        for cp in local:
            cp.wait()

    return _comm_call(body, shards, [jax.ShapeDtypeStruct((N_DEV,) + s.shape, s.dtype) for s in shards],
                      name="weights_allgather")


def _grads_exchange(by_owner):
    n = len(by_owner)

    def body(*refs):
        ins, outs = refs[:n], refs[n:2 * n]
        send_sems, recv_sems, local_sems = refs[2 * n:]
        x, y, c = _position()
        mine = 4 * x + 2 * y + c
        local = [pltpu.make_async_copy(ins[a].at[mine], outs[a].at[mine], local_sems.at[a]) for a in range(n)]
        for cp in local:
            cp.start()
        flips = [(dx, dy, dc) for dx in (0, 1) for dy in (0, 1) for dc in (0, 1) if dx + dy + dc]
        pending = []
        for k, (dx, dy, dc) in enumerate(flips):
            px, py, pc = (1 - x if dx else x), (1 - y if dy else y), (1 - c if dc else c)
            peer = 4 * px + 2 * py + pc
            for a in range(n):
                def remote(slot):
                    return pltpu.make_async_remote_copy(src_ref=ins[a].at[peer], dst_ref=outs[a].at[slot],
                                                        send_sem=send_sems.at[PEERS * a + k], recv_sem=recv_sems.at[PEERS * a + k],
                                                        device_id=(px, py, pc), device_id_type=MESH)

                sent = remote(mine)
                sent.start()
                pending.append((sent, remote(peer)))
        for sent, landed in pending:
            landed.wait_recv()
            sent.wait_send()
        for cp in local:
            cp.wait()

    return _comm_call(body, by_owner, [jax.ShapeDtypeStruct(g.shape, g.dtype) for g in by_owner], name="grads_exchange")


SEM_SPEC = pl.BlockSpec(memory_space=pltpu.SEMAPHORE)
DATAFLOW = pltpu.SideEffectType.DATAFLOW_SIDE_EFFECTING


def _peer_copies(srcs, lands, send_sems, recv_sems, by_owner, arrival):
    x, y, c = _position()
    mine = 4 * x + 2 * y + c
    copies = []
    for k, (dx, dy, dc) in enumerate([(dx, dy, dc) for dx in (0, 1) for dy in (0, 1) for dc in (0, 1) if dx + dy + dc]):
        px, py, pc = (1 - x if dx else x), (1 - y if dy else y), (1 - c if dc else c)
        peer = 4 * px + 2 * py + pc
        for a, (src, land) in enumerate(zip(srcs, lands)):
            copies.append(pltpu.make_async_remote_copy(
                src_ref=src.at[peer] if by_owner else src, dst_ref=land.at[peer if arrival else mine],
                send_sem=send_sems.at[PEERS * a + k], recv_sem=recv_sems.at[PEERS * a + k],
                device_id=(px, py, pc), device_id_type=MESH))
    return copies


def _exchange_start(sources, by_owner, *, name):
    n = len(sources)
    lands = [lax.empty((N_DEV,) + (s.shape[1:] if by_owner else s.shape), s.dtype) for s in sources]

    def body(*refs):
        send_sems, recv_sems, token = refs[2 * n], refs[2 * n + 1], refs[-1]
        for copy in _peer_copies(refs[:n], refs[n:2 * n], send_sems, recv_sems, by_owner, arrival=False):
            copy.start()
        token[...] = jnp.zeros_like(token)

    sems = pltpu.SemaphoreType.DMA((PEERS * n,))
    outs = pl.pallas_call(
        body,
        name=name,
        out_shape=(sems, sems, *[pltpu.HBM(a.shape, a.dtype) for a in sources + lands], jax.ShapeDtypeStruct((8, LANE), F32)),
        in_specs=[HBM_SPEC] * (2 * n),
        out_specs=(SEM_SPEC, SEM_SPEC, *[HBM_SPEC] * (2 * n), pl.BlockSpec(memory_space=pltpu.VMEM)),
        input_output_aliases={i: 2 + i for i in range(2 * n)},
        compiler_params=pltpu.CompilerParams(has_side_effects=DATAFLOW),
    )(*[pltpu.with_memory_space_constraint(a, pltpu.HBM) for a in sources + lands])
    return outs[:-1], outs[-1]


def _exchange_wait(started, after, by_owner, *, name):
    send_sems, recv_sems, *arrays = started
    n = len(arrays) // 2

    def body(*refs):
        for copy in _peer_copies(refs[:n], refs[n:2 * n], refs[2 * n], refs[2 * n + 1], by_owner, arrival=True):
            copy.wait_send()
            copy.wait_recv()

    outs = pl.pallas_call(
        body,
        name=name,
        out_shape=[pltpu.HBM(a.shape, a.dtype) for a in arrays],
        in_specs=[HBM_SPEC] * (2 * n) + [SEM_SPEC, SEM_SPEC, pl.BlockSpec(memory_space=pl.ANY)],
        out_specs=[HBM_SPEC] * (2 * n),
        input_output_aliases={i: i for i in range(2 * n)},
        compiler_params=pltpu.CompilerParams(has_side_effects=DATAFLOW),
    )(*arrays, send_sems, recv_sems, after)
    return outs[:n], outs[n:]


def _sum_slots(got, *, name, tr):
    _, rows, cols = got.shape
    tr = _tile(rows, tr, 16)

    def body(g_ref, o_ref):
        acc = g_ref[0].astype(F32)
        for j in range(1, N_DEV):
            acc = acc + g_ref[j].astype(F32)
        o_ref[...] = acc

    return pl.pallas_call(
        body,
        name=name,
        grid=(rows // tr,),
        in_specs=[pl.BlockSpec((N_DEV, tr, cols), lambda i: (0, i, 0))],
        out_specs=pl.BlockSpec((tr, cols), lambda i: (i, 0)),
        out_shape=jax.ShapeDtypeStruct((rows, cols), F32),
        compiler_params=pltpu.CompilerParams(dimension_semantics=("parallel",)),
    )(got)


def _adamw(wt, g, m, v, *, name, own=None):
    slots = own is not None
    shape = wt.shape
    two_d = (-1, shape[-1]) if wt.ndim > 1 else (1, -1)
    args = [a.reshape(two_d) for a in (wt, m, v)]
    rows, cols = args[0].shape
    tr = _tile(rows, 256, 16) if rows % 16 == 0 else rows
    args.insert(1, g.reshape((N_DEV, rows, cols) if slots else (rows, cols)))
    if slots:
        args.append(own.reshape(rows, cols))

    def body(w_ref, g_ref, m_ref, v_ref, *refs):
        go_ref, d_ref, nm_ref, nv_ref = refs[-4:]
        if slots:
            x, y, c = _position()
            mine = 4 * x + 2 * y + c
            gv = None
            for j in range(N_DEV):
                part = jnp.where(mine == j, refs[0][...], g_ref[j]).astype(F32)
                gv = part if gv is None else gv + part
        else:
            gv = g_ref[...]
        go_ref[...] = gv
        m2 = ADAM_B1 * m_ref[...] + (1.0 - ADAM_B1) * gv
        v2 = ADAM_B2 * v_ref[...] + (1.0 - ADAM_B2) * (gv * gv)
        m_hat = m2 / (1.0 - ADAM_B1 ** ADAM_STEP)
        v_hat = v2 / (1.0 - ADAM_B2 ** ADAM_STEP)
        d_ref[...] = -ADAM_LR * (m_hat / (jnp.sqrt(v_hat) + ADAM_EPS) + ADAM_WD * w_ref[...])
        nm_ref[...] = m2
        nv_ref[...] = v2

    spec = pl.BlockSpec((tr, cols), lambda i: (i, 0))
    g_spec = pl.BlockSpec((N_DEV, tr, cols), lambda i: (0, i, 0)) if slots else spec
    outs = pl.pallas_call(
        body,
        name=name,
        grid=(rows // tr,),
        in_specs=[spec, g_spec, spec, spec] + ([spec] if slots else []),
        out_specs=[spec] * 4,
        out_shape=[jax.ShapeDtypeStruct((rows, cols), F32)] * 4,
        compiler_params=pltpu.CompilerParams(dimension_semantics=("parallel",)),
    )(*args)
    return tuple(o.reshape(shape) for o in outs)


WEIGHTS = ("g_mix", "w_in", "conv_w", "a_log", "dt_bias", "rel_bias", "w_onorm", "w_branch_a", "w_branch_b", "w_out", "g_ffn",
           "w_gate_up", "w_down", "g_ple", "w_ple_gate", "w_ple_proj", "g_final")


def kernel(x, p, g_mix, w_in, conv_w, a_log, dt_bias, rel_bias, w_onorm, w_branch_a, w_branch_b, w_out, g_ffn, w_gate_up, w_down, g_ple, w_ple_gate, w_ple_proj, g_final, loss_target, m_g_mix, m_w_in, m_conv_w, m_a_log, m_dt_bias, m_rel_bias, m_w_onorm, m_w_branch_a, m_w_branch_b, m_w_out, m_g_ffn, m_w_gate_up, m_w_down, m_g_ple, m_w_ple_gate, m_w_ple_proj, m_g_final, v_g_mix, v_w_in, v_conv_w, v_a_log, v_dt_bias, v_rel_bias, v_w_onorm, v_w_branch_a, v_w_branch_b, v_w_out, v_g_ffn, v_w_gate_up, v_w_down, v_g_ple, v_w_ple_gate, v_w_ple_proj, v_g_final):
    given = dict(g_mix=g_mix, w_in=w_in, conv_w=conv_w, a_log=a_log, dt_bias=dt_bias, rel_bias=rel_bias, w_onorm=w_onorm,
                 w_branch_a=w_branch_a, w_branch_b=w_branch_b, w_out=w_out, g_ffn=g_ffn, w_gate_up=w_gate_up, w_down=w_down,
                 g_ple=g_ple, w_ple_gate=w_ple_gate, w_ple_proj=w_ple_proj, g_final=g_final)
    mom1 = dict(g_mix=m_g_mix, w_in=m_w_in, conv_w=m_conv_w, a_log=m_a_log, dt_bias=m_dt_bias, rel_bias=m_rel_bias,
                w_onorm=m_w_onorm, w_branch_a=m_w_branch_a, w_branch_b=m_w_branch_b, w_out=m_w_out, g_ffn=m_g_ffn,
                w_gate_up=m_w_gate_up, w_down=m_w_down, g_ple=m_g_ple, w_ple_gate=m_w_ple_gate, w_ple_proj=m_w_ple_proj,
                g_final=m_g_final)
    mom2 = dict(g_mix=v_g_mix, w_in=v_w_in, conv_w=v_conv_w, a_log=v_a_log, dt_bias=v_dt_bias, rel_bias=v_rel_bias,
                w_onorm=v_w_onorm, w_branch_a=v_w_branch_a, w_branch_b=v_w_branch_b, w_out=v_w_out, g_ffn=v_g_ffn,
                w_gate_up=v_w_gate_up, w_down=v_w_down, g_ple=v_g_ple, w_ple_gate=v_w_ple_gate, w_ple_proj=v_w_ple_proj,
                g_final=v_g_final)
    mine = 4 * lax.axis_index("x") + 2 * lax.axis_index("y") + lax.axis_index("c")

    my_slot = (jnp.arange(N_DEV) == mine)[:, None, None]
    rest = MATRICES[1:]
    in_flight = {}

    got_in, got_taps = _weights_allgather([w_in[0].astype(BF16), conv_w[0]])
    in_flight["weights"], weights_sent = _exchange_start([given[name][0].astype(BF16) for name, _, _ in rest], False,
                                                         name="weights_start")
    small = dict(g_mix=g_mix + weights_sent[0, 0], g_ffn=g_ffn, g_ple=g_ple, g_final=g_final, w_onorm=w_onorm, a_log=a_log,
                 dt_bias=dt_bias, rel_bias=rel_bias, conv_w=_from_gathered(got_taps, 1))

    def rest_weights(after):
        shards, landed = _exchange_wait(in_flight.pop("weights"), after, False, name="weights_wait")
        return {name: _from_gathered(jnp.where(my_slot, shard[None], slabs), axis)
                for (name, _, axis), shard, slabs in zip(rest, shards, landed)}

    def send_grads(gw):
        in_flight["grads"], sent = _exchange_start([_to_owner(gw[name], axis) for name, _, axis in rest], True, name="grads_start")
        return sent

    def send_w_in(g_in):
        in_flight["grad_in"], sent = _exchange_start([_to_owner(_unpermute_w_in(g_in), 1)], True, name="grad_in_start")
        return sent

    loss_part, grad_x, gs = _local_step(x, p[0], loss_target, _permute_w_in(_from_gathered(got_in, 1)), small, rest_weights,
                                        send_grads, send_w_in)
    gs["loss"] = loss_part

    mine_rest, got_rest = _exchange_wait(in_flight["grads"], grad_x, True, name="grads_wait")
    mine_in, got_in_grad = _exchange_wait(in_flight["grad_in"], grad_x, True, name="grad_in_wait")
    names = [name for name, _, _ in rest] + ["w_in"]
    grads, own = dict(zip(names, got_rest + got_in_grad)), dict(zip(names, mine_rest + mine_in))
    (got_small,) = _grads_exchange([jnp.broadcast_to(_pack_small(gs), (N_DEV, SMALL_ROWS, D))])
    small_sum = _unpack_small(_sum_slots(got_small, name="sum_small_grads", tr=16))
    loss = small_sum.pop("loss")[0]
    conv_all = small_sum.pop("conv_w").reshape(4, N_DEV, TAPS_PER_SHARD)
    small_sum["conv_w"] = lax.dynamic_index_in_dim(conv_all, mine, axis=1, keepdims=False)

    out_g, out_d, out_m, out_v = [], [], [], []
    for name in WEIGHTS:
        if name in grads:
            mine_of = lax.dynamic_index_in_dim(own[name], mine, axis=0, keepdims=False)
            g, delta, new_m, new_v = _adamw(given[name], grads[name], mom1[name], mom2[name], name=f"adamw_{name}", own=mine_of)
        else:
            g, delta, new_m, new_v = _adamw(given[name], small_sum[name].reshape(given[name].shape), mom1[name], mom2[name],
                                            name=f"adamw_{name}")
        out_g.append(g)
        out_d.append(delta)
        out_m.append(new_m)
        out_v.append(new_v)
    return (loss, grad_x, *out_g, *out_d, *out_m, *out_v)
```

```python
import jax
import jax.numpy as jnp
from jax import lax
from jax.experimental import pallas as pl
from jax.experimental.pallas import tpu as pltpu

F32 = jnp.float32
BF16 = jnp.bfloat16
DELTA_PREC = lax.Precision.HIGH
MESH = pl.DeviceIdType.MESH

N_DEV = 8
D = 1024
CHUNK = 64
EPS = 1e-6
A_HEADS, A_DIM, A_WIDTH = 8, 64, 512
A_BAND = 9 * CHUNK
A_PAD = 8 * CHUNK
REL_CLIP = 128
B_HEADS, B_DIM = 4, 128
B_CONV = 1536
D_FF = 2816
D_IN = 5640
P_CONV, P_Z, P_GATE, P_BD, P_END = 1536, 3072, 3584, 5632, 5760
LANE = 128

ADAM_LR, ADAM_B1, ADAM_B2, ADAM_EPS, ADAM_WD, ADAM_STEP = 0.001, 0.9, 0.999, 1e-08, 0.01, 10

NT = (((1,), (1,)), ((), ()))
TN = (((0,), (0,)), ((), ()))
NN = (((1,), (0,)), ((), ()))

HBM_SPEC = pl.BlockSpec(memory_space=pltpu.HBM)


def _tile(n, target, align=LANE):
    if n <= target:
        return n
    best = None
    for t in range(align, target + 1, align):
        if n % t == 0:
            best = t
    assert best is not None, (n, target, align)
    return best


def _mm(a, b, *, name, ta=False, tb=False, add=None, out_dtype=F32, tm=1024, tn=640, tk=None):
    assert not (ta and tb)
    if ta:
        k_dim, m_dim = a.shape
    else:
        m_dim, k_dim = a.shape
    n_dim = b.shape[0] if tb else b.shape[1]
    assert b.shape[1 if tb else 0] == k_dim
    tm, tn = _tile(m_dim, tm), _tile(n_dim, tn)
    tk = _tile(k_dim, tk or (4096 if ta else 1024), 8 if ta else LANE)
    nk = k_dim // tk
    dn = TN if ta else NT if tb else NN

    def body(*refs):
        if add is None:
            a_ref, b_ref, o_ref = refs[:3]
            add_ref = None
        else:
            a_ref, b_ref, add_ref, o_ref = refs[:4]
        part = lax.dot_general(a_ref[...].astype(BF16), b_ref[...].astype(BF16), dn, preferred_element_type=F32)

        def finish(r):
            if add_ref is not None:
                r = r + add_ref[...]
            o_ref[...] = r.astype(o_ref.dtype)

        if nk == 1:
            finish(part)
        else:
            acc_ref = refs[-1]
            k = pl.program_id(2)

            @pl.when(k == 0)
            def _():
                acc_ref[...] = part

            @pl.when(k > 0)
            def _():
                acc_ref[...] += part

            @pl.when(k == nk - 1)
            def _():
                finish(acc_ref[...])

    a_spec = pl.BlockSpec((tk, tm), lambda i, j, k: (k, i)) if ta else pl.BlockSpec((tm, tk), lambda i, j, k: (i, k))
    b_spec = pl.BlockSpec((tn, tk), lambda i, j, k: (j, k)) if tb else pl.BlockSpec((tk, tn), lambda i, j, k: (k, j))
    in_specs = [a_spec, b_spec]
    args = [a, b]
    if add is not None:
        in_specs.append(pl.BlockSpec((tm, tn), lambda i, j, k: (i, j)))
        args.append(add)
    return pl.pallas_call(
        body,
        name=name,
        grid=(m_dim // tm, n_dim // tn, nk),
        in_specs=in_specs,
        out_specs=pl.BlockSpec((tm, tn), lambda i, j, k: (i, j)),
        out_shape=jax.ShapeDtypeStruct((m_dim, n_dim), out_dtype),
        scratch_shapes=[pltpu.VMEM((tm, tn), F32)] if nk > 1 else [],
        compiler_params=pltpu.CompilerParams(dimension_semantics=("parallel", "parallel", "arbitrary")),
    )(*args)


def _rowwise(fn, rows, bcs, outs, reds=(), *, name, tr, ncol=1):
    n_rows = rows[0][0].shape[0]
    tr = _tile(n_rows, tr, 8)
    nrow = n_rows // tr
    n_in, n_out = len(rows) + len(bcs), len(outs)

    def body(*refs):
        j, i = pl.program_id(0), pl.program_id(1)
        o_vals, r_vals = fn(*[r[...] for r in refs[:n_in]])
        for ref, val in zip(refs[n_in:n_in + n_out], o_vals):
            ref[...] = val.astype(ref.dtype)
        for ref, val, (_, _, stride) in zip(refs[n_in + n_out:], r_vals, reds):
            first = (i == 0) if stride else jnp.logical_and(i == 0, j == 0)

            @pl.when(first)
            def _():
                ref[...] = val

            @pl.when(jnp.logical_not(first))
            def _():
                ref[...] += val

    def spec(r, w, off, st, row_dep=True):
        if row_dep:
            return pl.BlockSpec((r, w), lambda j, i: (i, off + st * j))
        return pl.BlockSpec((r, w), lambda j, i: (0, off + st * j))

    in_specs = [spec(tr, w, off, st) for (_, w, off, st) in rows]
    in_specs += [spec(a.shape[0], w, off, st, False) for (a, w, off, st) in bcs]
    out_specs = [spec(tr, w, off, st) for (_, _, w, off, st) in outs]
    out_specs += [spec(1, w, 0, st, False) for (_, w, st) in reds]
    out_shape = [jax.ShapeDtypeStruct((n_rows, c), dt) for (c, dt, _, _, _) in outs]
    out_shape += [jax.ShapeDtypeStruct((1, c), F32) for (c, _, _) in reds]
    return pl.pallas_call(
        body,
        name=name,
        grid=(ncol, nrow),
        in_specs=in_specs,
        out_specs=out_specs,
        out_shape=out_shape,
        compiler_params=pltpu.CompilerParams(dimension_semantics=("arbitrary", "arbitrary")),
    )(*[r[0] for r in rows], *[b[0] for b in bcs])


def _full(a):
    return (a, a.shape[1], 0, 0)


def _rms(x, g):
    return x * lax.rsqrt(jnp.mean(x * x, axis=-1, keepdims=True) + EPS) * g


def _silu(x):
    return x * jax.nn.sigmoid(x)


def _softplus(x):
    return jnp.maximum(x, 0.0) + jnp.log(1.0 + jnp.exp(-jnp.abs(x)))


def _rms_fwd(x, g, *, name):
    (h,) = _rowwise(lambda xb, gb: ([_rms(xb, gb)], []), [_full(x)], [_full(g)], [(D, BF16, D, 0, 0)], name=name, tr=512)
    return h


def _rms_bwd(x, g, dh, dres, *, name):
    def fn(xb, dhb, dresb, gb):
        _, vjp = jax.vjp(_rms, xb, gb)
        dx, dg = vjp(dhb)
        return [dx + dresb], [dg]

    return _rowwise(fn, [_full(x), _full(dh), _full(dres)], [_full(g)], [(D, F32, D, 0, 0)], [(D, D, 0)], name=name, tr=256)


def _gate_scalars(raw, al_row, dtb_row):
    lane = lax.broadcasted_iota(jnp.int32, raw.shape, 1)
    beta = jax.nn.sigmoid(raw)
    g = -jnp.exp(al_row) * _softplus(raw + dtb_row)
    return jnp.where(lane < B_HEADS, beta, jnp.where(lane < 2 * B_HEADS, g, 0.0))


def _gated_norm(o, z, w):
    return _rms(o, w) * _silu(z)


def _merge(ga, gb, ta, tb):
    return jax.nn.sigmoid(ga) * ta + jax.nn.sigmoid(gb) * tb


def _swiglu(gu):
    return _silu(gu[:, :D_FF]) * gu[:, D_FF:]


def _head_loss(x2, pg, pp, tgt, g):
    x3 = x2 + jax.nn.sigmoid(pg) * pp
    err = _rms(x3, g) - tgt
    return 0.5 * jnp.sum(jnp.mean(err * err, axis=-1))


CONV_W = 256


def _conv_taps(x, w):
    row = lax.broadcasted_iota(jnp.int32, x.shape, 0)
    shifted = [x] + [jnp.where(row >= s, pltpu.roll(x, s, 0), 0.0) for s in (1, 2, 3)]
    pre = shifted[0] * w[3:4]
    for s in (1, 2, 3):
        pre = pre + shifted[s] * w[3 - s:4 - s]
    return pre, shifted


def _conv_fwd(projp, conv_w, n_batch, seq):
    ncol = B_CONV // CONV_W
    first = P_CONV // CONV_W

    def body(x_ref, w_ref, o_ref):
        pre, _ = _conv_taps(x_ref[...], w_ref[...])
        o_ref[...] = _silu(pre)

    return pl.pallas_call(
        body,
        name="conv_fwd",
        grid=(ncol, n_batch),
        in_specs=[pl.BlockSpec((seq, CONV_W), lambda j, b: (b, first + j)), pl.BlockSpec((4, CONV_W), lambda j, b: (0, j))],
        out_specs=pl.BlockSpec((seq, CONV_W), lambda j, b: (b, j)),
        out_shape=jax.ShapeDtypeStruct((n_batch * seq, B_CONV), F32),
        compiler_params=pltpu.CompilerParams(dimension_semantics=("parallel", "parallel")),
    )(projp, conv_w)


def _conv_bwd(projp, conv_w, dc, n_batch, seq):
    width = dc.shape[1]
    ncol = width // CONV_W
    first_x = P_CONV // CONV_W

    def body(x_ref, w_ref, dc_ref, dx_ref, dw_ref):
        b = pl.program_id(1)
        w = w_ref[...]
        pre, shifted = _conv_taps(x_ref[...], w)
        sg = jax.nn.sigmoid(pre)
        dpre = dc_ref[...] * (sg * (1.0 + pre * (1.0 - sg)))
        row = lax.broadcasted_iota(jnp.int32, dpre.shape, 0)
        dx = dpre * w[3:4]
        for s in (1, 2, 3):
            dx = dx + jnp.where(row < seq - s, pltpu.roll(dpre, seq - s, 0), 0.0) * w[3 - s:4 - s]
        dx_ref[...] = dx.astype(dx_ref.dtype)
        for s in (0, 1, 2, 3):
            part = jnp.sum(dpre * shifted[s], axis=0, keepdims=True)

            @pl.when(b == 0)
            def _():
                dw_ref[3 - s:4 - s, :] = part

            @pl.when(b > 0)
            def _():
                dw_ref[3 - s:4 - s, :] += part

    return pl.pallas_call(
        body,
        name="conv_bwd",
        grid=(ncol, n_batch),
        in_specs=[
            pl.BlockSpec((seq, CONV_W), lambda j, b: (b, first_x + j)),
            pl.BlockSpec((4, CONV_W), lambda j, b: (0, j)),
            pl.BlockSpec((seq, CONV_W), lambda j, b: (b, j)),
        ],
        out_specs=[pl.BlockSpec((seq, CONV_W), lambda j, b: (b, j)), pl.BlockSpec((4, CONV_W), lambda j, b: (0, j))],
        out_shape=[jax.ShapeDtypeStruct((n_batch * seq, width), BF16), jax.ShapeDtypeStruct((4, width), F32)],
        compiler_params=pltpu.CompilerParams(dimension_semantics=("arbitrary", "arbitrary")),
    )(projp, conv_w, dc)


def _attn_chunk(qc, kb, vb, bias2, valid, lane_lo):
    sel = (lane_lo, jnp.logical_not(lane_lo))
    items = [(i, e) for i in range(len(qc)) for e in (0, 1)]
    k16, v16 = [t.astype(BF16) for t in kb], [t.astype(BF16) for t in vb]
    qm = [(jnp.where(sel[e], qc[i], 0.0) * (A_DIM ** -0.5)).astype(BF16) for i, e in items]
    s = [lax.dot_general(qm[n], k16[i], NT, preferred_element_type=F32) + bias2[e] for n, (i, e) in enumerate(items)]
    s = [jnp.where(valid[i], s[n], -1e30) for n, (i, e) in enumerate(items)]
    p = [jnp.exp(t - lax.stop_gradient(jnp.max(t, axis=-1, keepdims=True))) for t in s]
    p = [t * (1.0 / jnp.sum(t, axis=-1, keepdims=True)) for t in p]
    o = [jnp.where(sel[e], jnp.dot(p[n].astype(BF16), v16[i], preferred_element_type=F32), 0.0)
         for n, (i, e) in enumerate(items)]
    return [o[2 * i] + o[2 * i + 1] for i in range(len(qc))]


ATTN_GROUP_FWD, ATTN_GROUP_BWD = 4, 4


def _attn_group(g, group, q_ref, kp_ref, vp_ref):
    col = lax.broadcasted_iota(jnp.int32, (CHUNK, A_BAND), 1)
    lane_lo = lax.broadcasted_iota(jnp.int32, (1, LANE), 1) < A_DIM
    starts = [pl.multiple_of((g * group + i) * CHUNK, CHUNK) for i in range(group)]
    rows = [pl.ds(r0, CHUNK) for r0 in starts]
    bands = [pl.ds(r0, A_BAND) for r0 in starts]
    valid = [col + r0 >= A_PAD for r0 in starts]
    loaded = [q_ref[r, :] for r in rows], [kp_ref[b, :] for b in bands], [vp_ref[b, :] for b in bands]
    return rows, bands, loaded, valid, lane_lo


def _attn_specs(seq):
    def blk(first):
        return pl.BlockSpec((seq, LANE), lambda hp, b: (b, first + hp))

    return blk, pl.BlockSpec((2, CHUNK, A_BAND), lambda hp, b: (hp, 0, 0))


def _attn_fwd(projp, bias, n_batch, seq):
    nc = seq // CHUNK
    blk, bias_spec = _attn_specs(seq)

    def body(q_ref, k_ref, v_ref, b_ref, o_ref, kp_ref, vp_ref):
        kp_ref[0:A_PAD, :] = jnp.zeros((A_PAD, LANE), F32)
        vp_ref[0:A_PAD, :] = jnp.zeros((A_PAD, LANE), F32)
        kp_ref[A_PAD:, :] = k_ref[...]
        vp_ref[A_PAD:, :] = v_ref[...]
        bias2 = b_ref[...]

        def step(g, carry):
            rows, _, (qc, kb, vb), valid, lane_lo = _attn_group(g, ATTN_GROUP_FWD, q_ref, kp_ref, vp_ref)
            out = _attn_chunk(qc, kb, vb, bias2, valid, lane_lo)
            for r, o in zip(rows, out):
                o_ref[r, :] = o.astype(o_ref.dtype)
            return carry

        lax.fori_loop(0, nc // ATTN_GROUP_FWD, step, 0)

    return pl.pallas_call(
        body,
        name="attn_fwd",
        grid=(A_HEADS // 2, n_batch),
        in_specs=[blk(0), blk(4), blk(8), bias_spec],
        out_specs=pl.BlockSpec((seq, LANE), lambda hp, b: (b, hp)),
        out_shape=jax.ShapeDtypeStruct((n_batch * seq, A_WIDTH), BF16),
        scratch_shapes=[pltpu.VMEM((A_PAD + seq, LANE), F32), pltpu.VMEM((A_PAD + seq, LANE), F32)],
        compiler_params=pltpu.CompilerParams(dimension_semantics=("parallel", "parallel")),
    )(projp, projp, projp, bias)


def _attn_bwd(projp, bias, dy, n_batch, seq):
    nc = seq // CHUNK
    blk, bias_spec = _attn_specs(seq)
    out_blk = pl.BlockSpec((seq, LANE), lambda hp, b: (b, hp))

    def body(q_ref, k_ref, v_ref, b_ref, dy_ref, dq_ref, dk_ref, dv_ref, db_ref, kp_ref, vp_ref, dkp_ref, dvp_ref):
        b = pl.program_id(1)
        kp_ref[0:A_PAD, :] = jnp.zeros((A_PAD, LANE), F32)
        vp_ref[0:A_PAD, :] = jnp.zeros((A_PAD, LANE), F32)
        kp_ref[A_PAD:, :] = k_ref[...]
        vp_ref[A_PAD:, :] = v_ref[...]
        dkp_ref[...] = jnp.zeros_like(dkp_ref)
        dvp_ref[...] = jnp.zeros_like(dvp_ref)
        bias2 = b_ref[...]

        @pl.when(b == 0)
        def _():
            db_ref[...] = jnp.zeros_like(db_ref)

        def step(g, carry):
            rows, bands, (qc, kb, vb), valid, lane_lo = _attn_group(g, ATTN_GROUP_BWD, q_ref, kp_ref, vp_ref)
            _, vjp = jax.vjp(lambda q, k, v, bb: _attn_chunk(q, k, v, bb, valid, lane_lo), qc, kb, vb, bias2)
            dq, dk, dv, dbias = vjp([dy_ref[r, :] for r in rows])
            for i, r in enumerate(rows):
                dq_ref[r, :] = dq[i].astype(dq_ref.dtype)
            for i, band in enumerate(bands):
                dkp_ref[band, :] += dk[i]
                dvp_ref[band, :] += dv[i]
            db_ref[...] += dbias
            return carry

        lax.fori_loop(0, nc // ATTN_GROUP_BWD, step, 0)
        dk_ref[...] = dkp_ref[A_PAD:, :].astype(dk_ref.dtype)
        dv_ref[...] = dvp_ref[A_PAD:, :].astype(dv_ref.dtype)

    n_tok = n_batch * seq
    pad = pltpu.VMEM((A_PAD + seq, LANE), F32)
    return pl.pallas_call(
        body,
        name="attn_bwd",
        grid=(A_HEADS // 2, n_batch),
        in_specs=[blk(0), blk(4), blk(8), bias_spec, out_blk],
        out_specs=[out_blk, out_blk, out_blk, bias_spec],
        out_shape=[jax.ShapeDtypeStruct((n_tok, A_WIDTH), BF16)] * 3 + [jax.ShapeDtypeStruct((A_HEADS, CHUNK, A_BAND), F32)],
        scratch_shapes=[pad, pad, pad, pad],
        compiler_params=pltpu.CompilerParams(dimension_semantics=("arbitrary", "arbitrary")),
    )(projp, projp, projp, bias, dy)


def _rel_bias_table(rel_bias):
    span = CHUNK + A_BAND - 1
    near = REL_CLIP + CHUNK
    far = jnp.broadcast_to(rel_bias[:, 2 * REL_CLIP:], (A_HEADS, span - near))
    t = jnp.concatenate([rel_bias[:, 2 * REL_CLIP + 1 - near:], far], axis=1)
    u = jnp.concatenate([t[:, :A_BAND][:, ::-1], t[:, A_BAND:][:, ::-1]], axis=1)
    rolled = jnp.tile(u, (1, CHUNK))[:, :CHUNK * (span - 1)].reshape(A_HEADS, CHUNK, span - 1)
    return rolled[:, :, :A_BAND]


def _dot(a, b, dn=NN):
    return lax.dot_general(a, b, dn, precision=DELTA_PREC, preferred_element_type=F32)


def _each(fn, *lists):
    return [fn(*vals) for vals in zip(*lists)]


@jax.custom_vjp
def _saved_inverse(x, inv):
    return inv


def _saved_inverse_fwd(x, inv):
    return inv, inv


def _saved_inverse_bwd(inv, ct):
    return _dot(_dot(inv, ct, TN), inv, NT), jnp.zeros_like(inv)


_saved_inverse.defvjp(_saved_inverse_fwd, _saved_inverse_bwd)


def _delta_chunk(r_state, cq, ck, cv, beta, g, saved_inv=None):
    ii = lax.broadcasted_iota(jnp.int32, (CHUNK, CHUNK), 0)
    jj = lax.broadcasted_iota(jnp.int32, (CHUNK, CHUNK), 1)
    incl, strict, eye = ii >= jj, ii > jj, ii == jj
    q = _each(lambda t: t * lax.rsqrt(jnp.sum(t * t, axis=-1, keepdims=True) + EPS) * (B_DIM ** -0.5), cq)
    k = _each(lambda t: t * lax.rsqrt(jnp.sum(t * t, axis=-1, keepdims=True) + EPS), ck)
    g_b = _each(lambda t: jnp.broadcast_to(t, (CHUNK, CHUNK)), g)
    g_row = _each(lambda t: jnp.sum(jnp.where(eye, t, 0.0), axis=0, keepdims=True), g_b)
    gc_col = _each(lambda t: jnp.sum(jnp.where(incl, t, 0.0), axis=1, keepdims=True), g_row)
    gc_row = _each(lambda t: jnp.sum(jnp.where(ii <= jj, t, 0.0), axis=0, keepdims=True), g_b)
    decay = _each(lambda c, r: jnp.where(incl, jnp.exp(jnp.where(incl, c - r, 0.0)), 0.0), gc_col, gc_row)
    kk = _each(lambda t: _dot(t, t, NT), k)
    x = _each(lambda b, m, d: jnp.where(strict, -(b * m * d), 0.0), beta, kk, decay)
    if saved_inv is None:
        inv = _each(lambda t: jnp.where(eye, 1.0, 0.0) + t, x)
        pw = x
        for _ in range(5):
            pw = _each(lambda t: _dot(t, t), pw)
            inv = _each(lambda t, s: t + _dot(t, s), inv, pw)
    else:
        inv = _each(_saved_inverse, x, saved_inv)
    egc = _each(jnp.exp, gc_col)
    u = _each(lambda t, b, v: _dot(t, b * v), inv, beta, cv)
    wk = _each(lambda t, b, e, kh: _dot(t, (b * e) * kh), inv, beta, egc, k)
    pqk = _each(lambda qh, kh, d: _dot(qh, kh, NT) * d, q, k, decay)
    g_last = _each(lambda c: c[CHUNK - 1:CHUNK, :], gc_col)
    kdec = _each(lambda kh, gl, c: kh * jnp.exp(gl - c), k, g_last, gc_col)
    w = _each(lambda uh, wkh, r: uh - _dot(wkh, r), u, wk, r_state)
    o = _each(lambda e, qh, r, ph, wh: e * _dot(qh, r) + _dot(ph, wh), egc, q, r_state, pqk, w)
    r_new = _each(lambda gl, r, kd, wh: jnp.exp(gl) * r + _dot(kd, wh, TN), g_last, r_state, kdec, w)
    return o, r_new, inv


DELTA_BLK = 512


def _delta_blocks(n_batch, seq):
    nblk = seq // DELTA_BLK
    cpb = DELTA_BLK // CHUNK

    def rows(width, order):
        return pl.BlockSpec((DELTA_BLK, width), lambda b, i: (b * nblk + order(i), 0))

    def states(order, side):
        return pl.BlockSpec((cpb, B_HEADS, side, side), lambda b, i: (b * nblk + order(i), 0, 0, 0))

    return nblk, cpb, rows, states


def _head_cols(h):
    return [pl.ds(part * B_HEADS * B_DIM + h * B_DIM, B_DIM) for part in range(3)]


def _load_heads(c_ref, bg_ref, state_ref, rows):
    bg_c = bg_ref[rows, :]
    cols = [_head_cols(h) for h in range(B_HEADS)]
    return ([state_ref[h] for h in range(B_HEADS)], [c_ref[rows, c[0]] for c in cols], [c_ref[rows, c[1]] for c in cols],
            [c_ref[rows, c[2]] for c in cols], [bg_c[:, h:h + 1] for h in range(B_HEADS)],
            [bg_c[:, B_HEADS + h:B_HEADS + h + 1] for h in range(B_HEADS)])


def _delta_fwd(conv, bg, n_batch, seq):
    nblk, cpb, rows_spec, states_spec = _delta_blocks(n_batch, seq)

    def forward(i):
        return i

    def body(c_ref, bg_ref, o_ref, st_ref, inv_ref, r_ref):
        @pl.when(pl.program_id(1) == 0)
        def _():
            r_ref[...] = jnp.zeros_like(r_ref)

        def step(c, carry):
            rows = pl.ds(pl.multiple_of(c * CHUNK, CHUNK), CHUNK)
            args = _load_heads(c_ref, bg_ref, r_ref, rows)
            o, r_new, inv = _delta_chunk(*args)
            for h in range(B_HEADS):
                st_ref[c, h] = args[0][h]
                inv_ref[c, h] = inv[h]
                o_ref[rows, pl.ds(h * B_DIM, B_DIM)] = o[h]
            for h in range(B_HEADS):
                r_ref[h] = r_new[h]
            return carry

        lax.fori_loop(0, cpb, step, 0)

    n_tok = n_batch * seq
    return pl.pallas_call(
        body,
        name="delta_fwd",
        grid=(n_batch, nblk),
        in_specs=[rows_spec(B_CONV, forward), rows_spec(LANE, forward)],
        out_specs=[rows_spec(B_HEADS * B_DIM, forward), states_spec(forward, B_DIM), states_spec(forward, CHUNK)],
        out_shape=[jax.ShapeDtypeStruct((n_tok, B_HEADS * B_DIM), F32),
                   jax.ShapeDtypeStruct((n_tok // CHUNK, B_HEADS, B_DIM, B_DIM), F32),
                   jax.ShapeDtypeStruct((n_tok // CHUNK, B_HEADS, CHUNK, CHUNK), F32)],
        scratch_shapes=[pltpu.VMEM((B_HEADS, B_DIM, B_DIM), F32)],
        compiler_params=pltpu.CompilerParams(dimension_semantics=("arbitrary", "arbitrary")),
    )(conv, bg)


def _delta_bwd(conv, bg, states, inverses, do, n_batch, seq):
    nblk, cpb, rows_spec, states_spec = _delta_blocks(n_batch, seq)

    def backward(i):
        return nblk - 1 - i

    def body(c_ref, bg_ref, st_ref, inv_ref, do_ref, dc_ref, dbg_ref, dr_ref):
        @pl.when(pl.program_id(1) == 0)
        def _():
            dr_ref[...] = jnp.zeros_like(dr_ref)

        def step(n, carry):
            c = cpb - 1 - n
            rows = pl.ds(pl.multiple_of(c * CHUNK, CHUNK), CHUNK)
            saved = [inv_ref[c, h] for h in range(B_HEADS)]
            _, vjp = jax.vjp(lambda *args: _delta_chunk(*args, saved_inv=saved)[:2],
                             *_load_heads(c_ref, bg_ref, st_ref.at[c], rows))
            do = [do_ref[rows, pl.ds(h * B_DIM, B_DIM)] for h in range(B_HEADS)]
            dr, dq, dk, dv, dbeta, dg = vjp((do, [dr_ref[h] for h in range(B_HEADS)]))
            lane = lax.broadcasted_iota(jnp.int32, (CHUNK, LANE), 1)
            dbg = jnp.zeros((CHUNK, LANE), F32)
            for h in range(B_HEADS):
                cq, ck, cv = _head_cols(h)
                dr_ref[h] = dr[h]
                dc_ref[rows, cq] = dq[h]
                dc_ref[rows, ck] = dk[h]
                dc_ref[rows, cv] = dv[h]
                dbg = dbg + jnp.where(lane == h, dbeta[h], 0.0) + jnp.where(lane == h + B_HEADS, dg[h], 0.0)
            dbg_ref[rows, :] = dbg
            return carry

        lax.fori_loop(0, cpb, step, 0)

    n_tok = n_batch * seq
    return pl.pallas_call(
        body,
        name="delta_bwd",
        grid=(n_batch, nblk),
        in_specs=[rows_spec(B_CONV, backward), rows_spec(LANE, backward), states_spec(backward, B_DIM),
                  states_spec(backward, CHUNK), rows_spec(B_HEADS * B_DIM, backward)],
        out_specs=[rows_spec(B_CONV, backward), rows_spec(LANE, backward)],
        out_shape=[jax.ShapeDtypeStruct((n_tok, B_CONV), F32), jax.ShapeDtypeStruct((n_tok, LANE), F32)],
        scratch_shapes=[pltpu.VMEM((B_HEADS, B_DIM, B_DIM), F32)],
        compiler_params=pltpu.CompilerParams(dimension_semantics=("arbitrary", "arbitrary")),
    )(conv, bg, states, inverses, do)


def _lane_row(vec4, first):
    return jnp.concatenate([jnp.zeros((1, first), F32), vec4.reshape(1, B_HEADS).astype(F32),
                            jnp.zeros((1, LANE - first - B_HEADS), F32)], axis=1)


def _local_step(x3d, p3d, tgt3d, w_in, small, rest_weights, send_grads, send_w_in):
    n_batch, seq, _ = x3d.shape
    n_tok = n_batch * seq
    x, p, tgt = x3d.reshape(n_tok, D), p3d.reshape(n_tok, -1), tgt3d.reshape(n_tok, D)
    g_mix, g_ffn, g_ple, g_final = (small[k].reshape(1, D) for k in ("g_mix", "g_ffn", "g_ple", "g_final"))
    w_onorm = small["w_onorm"].reshape(1, B_DIM)
    al_row = _lane_row(small["a_log"], B_HEADS)
    dtb_row = _lane_row(small["dt_bias"], B_HEADS)
    rel_bias = small["rel_bias"].reshape(A_HEADS, -1)
    bias = _rel_bias_table(rel_bias)
    conv_w = small["conv_w"].reshape(4, B_CONV)
    bd_blk = P_BD // LANE

    h1 = _rms_fwd(x, g_mix, name="rms_mix")
    projp = _mm(h1, w_in, tb=True, name="mm_proj", tn=640)
    y_a = _attn_fwd(projp, bias, n_batch, seq)
    conv = _conv_fwd(projp, conv_w, n_batch, seq)
    (bg,) = _rowwise(lambda raw, al, dtb: ([_gate_scalars(raw, al, dtb)], []), [(projp, LANE, bd_blk, 0)],
                     [_full(al_row), _full(dtb_row)], [(LANE, F32, LANE, 0, 0)], name="gate_scalars", tr=1024)
    o_b, states, inverses = _delta_fwd(conv, bg, n_batch, seq)
    (y_b,) = _rowwise(lambda o, z, wn: ([_gated_norm(o, z, wn)], []), [(o_b, LANE, 0, 1), (projp, LANE, P_Z // LANE, 1)],
                      [_full(w_onorm)], [(B_HEADS * B_DIM, BF16, LANE, 0, 1)], name="gated_norm", tr=1024, ncol=B_HEADS)
    w = rest_weights(y_b)
    t_a = _mm(y_a, w["w_branch_a"], tb=True, name="mm_branch_a", tn=1024)
    t_b = _mm(y_b, w["w_branch_b"], tb=True, name="mm_branch_b", tn=1024)
    half = D // 2
    gate_rows = [(projp, half, P_GATE // half, 1), (projp, half, P_GATE // half + 2, 1), (t_a, half, 0, 1), (t_b, half, 0, 1)]
    (merged,) = _rowwise(lambda ga, gb, ta, tb: ([_merge(ga, gb, ta, tb)], []), gate_rows, [], [(D, BF16, half, 0, 1)],
                         name="merge", tr=512, ncol=2)
    x1 = _mm(merged, w["w_out"], add=x, name="mm_out", tn=1024)
    h2 = _rms_fwd(x1, g_ffn, name="rms_ffn")
    gu = _mm(h2, w["w_gate_up"], tb=True, name="mm_gate_up", tn=512)
    (act,) = _rowwise(lambda gub: ([_swiglu(gub)], []), [_full(gu)], [], [(D_FF, BF16, D_FF, 0, 0)], name="swiglu", tr=256)
    x2 = _mm(act, w["w_down"], add=x1, name="mm_down", tn=1024, tk=1408)
    h3 = _rms_fwd(x2, g_ple, name="rms_ple")
    pg = _mm(h3, w["w_ple_gate"], name="mm_ple_gate", tn=1024)
    pp = _mm(p, w["w_ple_proj"], tb=True, name="mm_ple_proj", tn=1024)

    def head_fn(x2b, pgb, ppb, tb, gb):
        loss, (dx2, dpg, dpp, dg) = jax.value_and_grad(_head_loss, argnums=(0, 1, 2, 4))(x2b, pgb, ppb, tb, gb)
        return [dx2, dpg, dpp], [dg, jnp.full((1, LANE), loss, F32)]

    dx3, dpg, dpp, dg_final, loss_row = _rowwise(
        head_fn, [_full(x2), _full(pg), _full(pp), _full(tgt)], [_full(g_final)],
        [(D, F32, D, 0, 0), (D, BF16, D, 0, 0), (D, BF16, D, 0, 0)], [(D, D, 0), (LANE, LANE, 0)], name="loss_head", tr=256)
    gw = {}
    gw["w_ple_proj"] = _mm(dpp, p, ta=True, out_dtype=BF16, name="mm_d_ple_proj", tn=256)
    gw["w_ple_gate"] = _mm(h3, dpg, ta=True, out_dtype=BF16, name="mm_d_ple_gate", tn=512)
    dh3 = _mm(dpg, w["w_ple_gate"], tb=True, name="mm_dh3", tn=1024)
    dx2, dg_ple = _rms_bwd(x2, g_ple, dh3, dx3, name="rms_ple_bwd")
    gw["w_down"] = _mm(act, dx2, ta=True, out_dtype=BF16, name="mm_d_down", tm=1408, tn=512, tk=2048)
    dact = _mm(dx2, w["w_down"], tb=True, name="mm_dact", tn=1408)

    def swiglu_bwd(gub, dab):
        _, vjp = jax.vjp(_swiglu, gub)
        return [vjp(dab)[0]], []

    (dgu,) = _rowwise(swiglu_bwd, [_full(gu), _full(dact)], [], [(2 * D_FF, BF16, 2 * D_FF, 0, 0)], name="swiglu_bwd", tr=256)
    gw["w_gate_up"] = _mm(dgu, h2, ta=True, out_dtype=BF16, name="mm_d_gate_up", tm=512, tn=1024)
    dh2 = _mm(dgu, w["w_gate_up"], name="mm_dh2", tn=1024, tk=1408)
    dx1, dg_ffn = _rms_bwd(x1, g_ffn, dh2, dx2, name="rms_ffn_bwd")
    gw["w_out"] = _mm(merged, dx1, ta=True, out_dtype=BF16, name="mm_d_out", tn=512)
    dmerged = _mm(dx1, w["w_out"], tb=True, name="mm_dmerged", tn=1024)

    def merge_bwd(ga, gb, ta, tb, dm):
        _, vjp = jax.vjp(_merge, ga, gb, ta, tb)
        return list(vjp(dm)), []

    dga, dgb, dta, dtb = _rowwise(merge_bwd, gate_rows + [(dmerged, half, 0, 1)], [], [(D, BF16, half, 0, 1)] * 4,
                                  name="merge_bwd", tr=512, ncol=2)
    gw["w_branch_a"] = _mm(dta, y_a, ta=True, out_dtype=BF16, name="mm_d_branch_a", tn=512)
    gw["w_branch_b"] = _mm(dtb, y_b, ta=True, out_dtype=BF16, name="mm_d_branch_b", tn=512)
    dya = _mm(dta, w["w_branch_a"], name="mm_dya", tn=512)
    dyb = _mm(dtb, w["w_branch_b"], name="mm_dyb", tn=512)

    w_onorm = w_onorm + send_grads(gw)[0, 0]

    def gated_norm_bwd(o, z, dy, wn):
        _, vjp = jax.vjp(_gated_norm, o, z, wn)
        do, dz, dwn = vjp(dy)
        return [do, dz], [dwn]

    do_b, dz, dw_onorm = _rowwise(
        gated_norm_bwd, [(o_b, LANE, 0, 1), (projp, LANE, P_Z // LANE, 1), (dyb, LANE, 0, 1)], [_full(w_onorm)],
        [(B_HEADS * B_DIM, F32, LANE, 0, 1), (B_HEADS * B_DIM, BF16, LANE, 0, 1)], [(B_DIM, B_DIM, 0)],
        name="gated_norm_bwd", tr=1024, ncol=B_HEADS)
    dconv_out, dbg = _delta_bwd(conv, bg, states, inverses, do_b, n_batch, seq)

    def gate_scalars_bwd(raw, dbgb, al, dtb):
        _, vjp = jax.vjp(_gate_scalars, raw, al, dtb)
        draw, dal, ddtb = vjp(dbgb)
        return [draw], [dal, ddtb]

    dbd, dal_row, ddtb_row = _rowwise(gate_scalars_bwd, [(projp, LANE, bd_blk, 0), _full(dbg)], [_full(al_row), _full(dtb_row)],
                                      [(LANE, BF16, LANE, 0, 0)], [(LANE, LANE, 0), (LANE, LANE, 0)], name="gate_scalars_bwd",
                                      tr=1024)
    dconv, dconv_w = _conv_bwd(projp, conv_w, dconv_out, n_batch, seq)
    dq_a, dk_a, dv_a, dbias = _attn_bwd(projp, bias, dya, n_batch, seq)
    dprojp = jnp.concatenate([dq_a, dk_a, dv_a, dconv, dz, dga, dgb, dbd], axis=1)
    sent = send_w_in(_mm(dprojp, h1, ta=True, out_dtype=BF16, name="mm_d_in", tm=640, tn=1024))
    sent, dprojp = lax.optimization_barrier((sent, dprojp))
    dh1 = _mm(dprojp, w_in, name="mm_dh1", tn=1024, tk=1152)
    grad_x, dg_mix = _rms_bwd(x, g_mix + sent[0, 0], dh1, dx1, name="rms_mix_bwd")

    _, bias_vjp = jax.vjp(_rel_bias_table, rel_bias)
    gs = {
        "g_mix": dg_mix, "g_ffn": dg_ffn, "g_ple": dg_ple, "g_final": dg_final, "w_onorm": dw_onorm,
        "conv_w": dconv_w, "rel_bias": bias_vjp(dbias)[0],
        "a_log": dal_row[0, B_HEADS:2 * B_HEADS], "dt_bias": ddtb_row[0, B_HEADS:2 * B_HEADS],
    }
    return loss_row[:, :1], grad_x.reshape(n_batch, seq, D), gs


MATRICES = (("w_in", 1), ("w_gate_up", 1), ("w_branch_a", 1), ("w_branch_b", 1), ("w_out", 0), ("w_down", 0),
            ("w_ple_gate", 0), ("w_ple_proj", 1))
TAPS_PER_SHARD = B_CONV // N_DEV


def _held(shard, axis):
    return shard if axis == 0 else shard.T


def _from_gathered(slabs):
    return slabs.reshape(-1, slabs.shape[-1])


def _to_owner(held):
    return held.reshape(N_DEV, held.shape[0] // N_DEV, held.shape[1])


def _permute_w_in(held):
    n_gate = P_BD - P_GATE
    row = lax.broadcasted_iota(jnp.int32, (P_END, 1), 0)
    same = jnp.pad(held, ((0, P_END - D_IN), (0, 0)))
    up = jnp.pad(held[8:], ((0, P_END - D_IN + 8), (0, 0)))
    down = jnp.pad(held[:P_GATE + 8], ((n_gate, P_END - P_BD - 8), (0, 0)))
    zero = jnp.zeros((), held.dtype)
    return jnp.where(row < P_GATE, same, jnp.where(row < P_BD, up, jnp.where(row < P_BD + 8, down, zero)))


def _unpermute_w_in(gp):
    n_gate = P_BD - P_GATE
    row = lax.broadcasted_iota(jnp.int32, (D_IN, 1), 0)
    same = gp[:D_IN]
    up = jnp.pad(gp[n_gate:], ((0, D_IN - (P_END - n_gate)), (0, 0)))
    down = jnp.pad(gp[:P_BD], ((8, 0), (0, 0)))
    return jnp.where(row < P_GATE, same, jnp.where(row < P_GATE + 8, up, down))


SMALL_ROWS = 16
SMALL_LAYOUT = (("g_mix", 0, D), ("g_ffn", 1, D), ("g_ple", 2, D), ("g_final", 3, D), ("conv_w", 4, 4 * B_CONV),
                ("rel_bias", 10, A_HEADS * (2 * REL_CLIP + 1)), ("w_onorm", 13, B_DIM), ("a_log", 14, B_HEADS),
                ("dt_bias", 14, B_HEADS), ("loss", 15, 1))


def _pack_small(gs):
    rows = {}
    for name, row, n in SMALL_LAYOUT:
        rows.setdefault(row, []).append(gs[name].reshape(-1).astype(F32))
    parts = []
    for row in sorted(rows):
        flat = jnp.concatenate(rows[row])
        parts.append(jnp.concatenate([flat, jnp.zeros((-flat.shape[0] % D,), F32)]))
    flat = jnp.concatenate(parts)
    assert flat.shape[0] == SMALL_ROWS * D, flat.shape
    return flat.reshape(SMALL_ROWS, D)


def _unpack_small(blk):
    flat, out, used = blk.reshape(-1), {}, {}
    for name, row, n in SMALL_LAYOUT:
        start = row * D + used.get(row, 0)
        out[name] = flat[start:start + n]
        used[row] = used.get(row, 0) + n
    return out


def _position():
    return lax.axis_index("x"), lax.axis_index("y"), lax.axis_index("c")


PEERS = N_DEV - 1


def _comm_call(body, arrays, out_shapes, *, name):
    n = len(arrays)
    return pl.pallas_call(
        body,
        name=name,
        out_shape=out_shapes,
        in_specs=[HBM_SPEC] * n,
        out_specs=[HBM_SPEC] * n,
        scratch_shapes=[pltpu.SemaphoreType.DMA((PEERS * n,)), pltpu.SemaphoreType.DMA((PEERS * n,)),
                        pltpu.SemaphoreType.DMA((n,))],
    )(*arrays)


def _weights_allgather(shards):
    n = len(shards)

    def body(*refs):
        ins, outs = refs[:n], refs[n:2 * n]
        send_sems, recv_sems, local_sems = refs[2 * n:]
        x, y, c = _position()
        me, sibling = (x, y, c), (x, y, 1 - c)
        chips = [(1 - x, y), (x, 1 - y), (1 - x, 1 - y)]

        def slab(a, px, py, pc):
            return outs[a].at[4 * px + 2 * py + pc]

        def copy(a, k, block, to, src=None):
            return pltpu.make_async_remote_copy(src_ref=slab(a, *block) if src is None else src, dst_ref=slab(a, *block),
                                                send_sem=send_sems.at[PEERS * a + k], recv_sem=recv_sems.at[PEERS * a + k],
                                                device_id=to, device_id_type=MESH)

        local = [pltpu.make_async_copy(ins[a], slab(a, *me), local_sems.at[a]) for a in range(n)]
        sent = [copy(a, 1 + j, me, (*chip, c), src=ins[a]) for a in range(n) for j, chip in enumerate(chips)]
        sent += [copy(a, 0, me, sibling, src=ins[a]) for a in range(n)]
        for cp in sent + local:
            cp.start()
        for a in range(n):
            for j, chip in enumerate(chips):
                copy(a, 1 + j, (*chip, c), me).wait_recv()
                passed = copy(a, 4 + j, (*chip, c), sibling)
                passed.start()
                sent.append(passed)
        for a in range(n):
            copy(a, 0, sibling, me).wait_recv()
            for j, chip in enumerate(chips):
                copy(a, 4 + j, (*chip, 1 - c), me).wait_recv()
        for cp in sent:
            cp.wait_send()
        for cp in local:
            cp.wait()

    return _comm_call(body, shards, [jax.ShapeDtypeStruct((N_DEV,) + s.shape, s.dtype) for s in shards],
                      name="weights_allgather")


def _grads_exchange(by_owner):
    n = len(by_owner)

    def body(*refs):
        ins, outs = refs[:n], refs[n:2 * n]
        send_sems, recv_sems, local_sems = refs[2 * n:]
        x, y, c = _position()
        mine = 4 * x + 2 * y + c
        local = [pltpu.make_async_copy(ins[a].at[mine], outs[a].at[mine], local_sems.at[a]) for a in range(n)]
        for cp in local:
            cp.start()
        flips = [(dx, dy, dc) for dx in (0, 1) for dy in (0, 1) for dc in (0, 1) if dx + dy + dc]
        pending = []
        for k, (dx, dy, dc) in enumerate(flips):
            px, py, pc = (1 - x if dx else x), (1 - y if dy else y), (1 - c if dc else c)
            peer = 4 * px + 2 * py + pc
            for a in range(n):
                def remote(slot):
                    return pltpu.make_async_remote_copy(src_ref=ins[a].at[peer], dst_ref=outs[a].at[slot],
                                                        send_sem=send_sems.at[PEERS * a + k], recv_sem=recv_sems.at[PEERS * a + k],
                                                        device_id=(px, py, pc), device_id_type=MESH)

                sent = remote(mine)
                sent.start()
                pending.append((sent, remote(peer)))
        for sent, landed in pending:
            landed.wait_recv()
            sent.wait_send()
        for cp in local:
            cp.wait()

    return _comm_call(body, by_owner, [jax.ShapeDtypeStruct(g.shape, g.dtype) for g in by_owner], name="grads_exchange")


SEM_SPEC = pl.BlockSpec(memory_space=pltpu.SEMAPHORE)
DATAFLOW = pltpu.SideEffectType.DATAFLOW_SIDE_EFFECTING


def _peer_copies(srcs, lands, send_sems, recv_sems, by_owner, arrival):
    x, y, c = _position()
    mine = 4 * x + 2 * y + c
    copies = []
    for k, (dx, dy, dc) in enumerate([(dx, dy, dc) for dx in (0, 1) for dy in (0, 1) for dc in (0, 1) if dx + dy + dc]):
        px, py, pc = (1 - x if dx else x), (1 - y if dy else y), (1 - c if dc else c)
        peer = 4 * px + 2 * py + pc
        for a, (src, land) in enumerate(zip(srcs, lands)):
            copies.append(pltpu.make_async_remote_copy(
                src_ref=src.at[peer] if by_owner else src, dst_ref=land.at[peer if arrival else mine],
                send_sem=send_sems.at[PEERS * a + k], recv_sem=recv_sems.at[PEERS * a + k],
                device_id=(px, py, pc), device_id_type=MESH))
    return copies


def _exchange_start(sources, by_owner, *, name):
    n = len(sources)
    lands = [lax.empty((N_DEV,) + (s.shape[1:] if by_owner else s.shape), s.dtype) for s in sources]

    def body(*refs):
        send_sems, recv_sems, token = refs[2 * n], refs[2 * n + 1], refs[-1]
        for copy in _peer_copies(refs[:n], refs[n:2 * n], send_sems, recv_sems, by_owner, arrival=False):
            copy.start()
        token[...] = jnp.zeros_like(token)

    sems = pltpu.SemaphoreType.DMA((PEERS * n,))
    outs = pl.pallas_call(
        body,
        name=name,
        out_shape=(sems, sems, *[pltpu.HBM(a.shape, a.dtype) for a in sources + lands], jax.ShapeDtypeStruct((8, LANE), F32)),
        in_specs=[HBM_SPEC] * (2 * n),
        out_specs=(SEM_SPEC, SEM_SPEC, *[HBM_SPEC] * (2 * n), pl.BlockSpec(memory_space=pltpu.VMEM)),
        input_output_aliases={i: 2 + i for i in range(2 * n)},
        compiler_params=pltpu.CompilerParams(has_side_effects=DATAFLOW),
    )(*[pltpu.with_memory_space_constraint(a, pltpu.HBM) for a in sources + lands])
    return outs[:-1], outs[-1]


def _exchange_wait(started, after, by_owner, *, name):
    send_sems, recv_sems, *arrays = started
    n = len(arrays) // 2

    def body(*refs):
        for copy in _peer_copies(refs[:n], refs[n:2 * n], refs[2 * n], refs[2 * n + 1], by_owner, arrival=True):
            copy.wait_send()
            copy.wait_recv()

    outs = pl.pallas_call(
        body,
        name=name,
        out_shape=[pltpu.HBM(a.shape, a.dtype) for a in arrays],
        in_specs=[HBM_SPEC] * (2 * n) + [SEM_SPEC, SEM_SPEC, pl.BlockSpec(memory_space=pl.ANY)],
        out_specs=[HBM_SPEC] * (2 * n),
        input_output_aliases={i: i for i in range(2 * n)},
        compiler_params=pltpu.CompilerParams(has_side_effects=DATAFLOW),
    )(*arrays, send_sems, recv_sems, after)
    return outs[:n], outs[n:]


def _slot_sum(g_ref, own_ref):
    if own_ref is not None:
        x, y, c = _position()
        mine = 4 * x + 2 * y + c
    acc = None
    for j in range(N_DEV):
        part = g_ref[j] if own_ref is None else jnp.where(mine == j, own_ref[...], g_ref[j])
        acc = part.astype(F32) if acc is None else acc + part.astype(F32)
    return acc


def _sum_slots(got, *, name, tr, own=None):
    _, rows, cols = got.shape
    tr = _tile(rows, tr, 16) if rows % 16 == 0 else rows
    spec = pl.BlockSpec((tr, cols), lambda i: (i, 0))

    def body(g_ref, *refs):
        refs[-1][...] = _slot_sum(g_ref, refs[0] if own is not None else None)

    return pl.pallas_call(
        body,
        name=name,
        grid=(rows // tr,),
        in_specs=[pl.BlockSpec((N_DEV, tr, cols), lambda i: (0, i, 0))] + ([spec] if own is not None else []),
        out_specs=spec,
        out_shape=jax.ShapeDtypeStruct((rows, cols), F32),
        compiler_params=pltpu.CompilerParams(dimension_semantics=("parallel",)),
    )(got, *([own] if own is not None else []))


def _adamw(wt, g, m, v, *, name, own=None):
    slots = own is not None
    shape = wt.shape
    two_d = (-1, shape[-1]) if wt.ndim > 1 else (1, -1)
    args = [a.reshape(two_d) for a in (wt, m, v)]
    rows, cols = args[0].shape
    tr = _tile(rows, 256, 16) if rows % 16 == 0 else rows
    args.insert(1, g.reshape((N_DEV, rows, cols) if slots else (rows, cols)))
    if slots:
        args.append(own.reshape(rows, cols))

    def body(w_ref, g_ref, m_ref, v_ref, *refs):
        go_ref, d_ref, nm_ref, nv_ref = refs[-4:]
        gv = _slot_sum(g_ref, refs[0]) if slots else g_ref[...]
        go_ref[...] = gv
        m2 = ADAM_B1 * m_ref[...] + (1.0 - ADAM_B1) * gv
        v2 = ADAM_B2 * v_ref[...] + (1.0 - ADAM_B2) * (gv * gv)
        m_hat = m2 / (1.0 - ADAM_B1 ** ADAM_STEP)
        v_hat = v2 / (1.0 - ADAM_B2 ** ADAM_STEP)
        d_ref[...] = -ADAM_LR * (m_hat / (jnp.sqrt(v_hat) + ADAM_EPS) + ADAM_WD * w_ref[...])
        nm_ref[...] = m2
        nv_ref[...] = v2

    spec = pl.BlockSpec((tr, cols), lambda i: (i, 0))
    g_spec = pl.BlockSpec((N_DEV, tr, cols), lambda i: (0, i, 0)) if slots else spec
    outs = pl.pallas_call(
        body,
        name=name,
        grid=(rows // tr,),
        in_specs=[spec, g_spec, spec, spec] + ([spec] if slots else []),
        out_specs=[spec] * 4,
        out_shape=[jax.ShapeDtypeStruct((rows, cols), F32)] * 4,
        compiler_params=pltpu.CompilerParams(dimension_semantics=("parallel",)),
    )(*args)
    return tuple(o.reshape(shape) for o in outs)


WEIGHTS = ("g_mix", "w_in", "conv_w", "a_log", "dt_bias", "rel_bias", "w_onorm", "w_branch_a", "w_branch_b", "w_out", "g_ffn",
           "w_gate_up", "w_down", "g_ple", "w_ple_gate", "w_ple_proj", "g_final")


def kernel(x, p, g_mix, w_in, conv_w, a_log, dt_bias, rel_bias, w_onorm, w_branch_a, w_branch_b, w_out, g_ffn, w_gate_up, w_down, g_ple, w_ple_gate, w_ple_proj, g_final, loss_target, m_g_mix, m_w_in, m_conv_w, m_a_log, m_dt_bias, m_rel_bias, m_w_onorm, m_w_branch_a, m_w_branch_b, m_w_out, m_g_ffn, m_w_gate_up, m_w_down, m_g_ple, m_w_ple_gate, m_w_ple_proj, m_g_final, v_g_mix, v_w_in, v_conv_w, v_a_log, v_dt_bias, v_rel_bias, v_w_onorm, v_w_branch_a, v_w_branch_b, v_w_out, v_g_ffn, v_w_gate_up, v_w_down, v_g_ple, v_w_ple_gate, v_w_ple_proj, v_g_final):
    given = dict(g_mix=g_mix, w_in=w_in, conv_w=conv_w, a_log=a_log, dt_bias=dt_bias, rel_bias=rel_bias, w_onorm=w_onorm,
                 w_branch_a=w_branch_a, w_branch_b=w_branch_b, w_out=w_out, g_ffn=g_ffn, w_gate_up=w_gate_up, w_down=w_down,
                 g_ple=g_ple, w_ple_gate=w_ple_gate, w_ple_proj=w_ple_proj, g_final=g_final)
    mom1 = dict(g_mix=m_g_mix, w_in=m_w_in, conv_w=m_conv_w, a_log=m_a_log, dt_bias=m_dt_bias, rel_bias=m_rel_bias,
                w_onorm=m_w_onorm, w_branch_a=m_w_branch_a, w_branch_b=m_w_branch_b, w_out=m_w_out, g_ffn=m_g_ffn,
                w_gate_up=m_w_gate_up, w_down=m_w_down, g_ple=m_g_ple, w_ple_gate=m_w_ple_gate, w_ple_proj=m_w_ple_proj,
                g_final=m_g_final)
    mom2 = dict(g_mix=v_g_mix, w_in=v_w_in, conv_w=v_conv_w, a_log=v_a_log, dt_bias=v_dt_bias, rel_bias=v_rel_bias,
                w_onorm=v_w_onorm, w_branch_a=v_w_branch_a, w_branch_b=v_w_branch_b, w_out=v_w_out, g_ffn=v_g_ffn,
                w_gate_up=v_w_gate_up, w_down=v_w_down, g_ple=v_g_ple, w_ple_gate=v_w_ple_gate, w_ple_proj=v_w_ple_proj,
                g_final=v_g_final)
    mine = 4 * lax.axis_index("x") + 2 * lax.axis_index("y") + lax.axis_index("c")

    my_slot = (jnp.arange(N_DEV) == mine)[:, None, None]
    rest = MATRICES[1:]
    in_flight = {}

    got_in, got_taps = _weights_allgather([_held(w_in[0], 1).astype(BF16), conv_w[0]])
    in_flight["weights"], weights_sent = _exchange_start([_held(given[name][0], axis).astype(BF16) for name, axis in rest], False,
                                                         name="weights_start")
    small = dict(g_mix=g_mix + weights_sent[0, 0], g_ffn=g_ffn, g_ple=g_ple, g_final=g_final, w_onorm=w_onorm, a_log=a_log,
                 dt_bias=dt_bias, rel_bias=rel_bias, conv_w=jnp.transpose(got_taps, (1, 0, 2)).reshape(4, B_CONV))

    def rest_weights(after):
        shards, landed = _exchange_wait(in_flight.pop("weights"), after, False, name="weights_wait")
        return {name: _from_gathered(jnp.where(my_slot, shard[None], slabs)) for (name, _), shard, slabs in zip(rest, shards, landed)}

    def send_grads(gw):
        in_flight["grads"], sent = _exchange_start([_to_owner(gw[name]) for name, _ in rest], True, name="grads_start")
        return sent

    def send_w_in(g_in):
        in_flight["grad_in"], sent = _exchange_start([_to_owner(_unpermute_w_in(g_in))], True, name="grad_in_start")
        return sent

    loss_part, grad_x, gs = _local_step(x, p[0], loss_target, _permute_w_in(_from_gathered(got_in)), small, rest_weights,
                                        send_grads, send_w_in)
    gs["loss"] = loss_part

    updates = {}

    def update_matrices(matrices, own_slabs, landed):
        for (name, axis), own_slab, slots in zip(matrices, own_slabs, landed):
            mine_of = lax.dynamic_index_in_dim(own_slab, mine, axis=0, keepdims=False)
            if axis == 0:
                updates[name] = _adamw(given[name], slots, mom1[name], mom2[name], name=f"adamw_{name}", own=mine_of)
            else:
                g = _sum_slots(slots, own=mine_of, name=f"sum_{name}", tr=256).T
                updates[name] = _adamw(given[name], g.reshape(given[name].shape), mom1[name], mom2[name], name=f"adamw_{name}")

    update_matrices(rest, *_exchange_wait(in_flight["grads"], grad_x, True, name="grads_wait"))
    update_matrices(MATRICES[:1], *_exchange_wait(in_flight["grad_in"], updates[rest[-1][0]][0], True, name="grad_in_wait"))
    small_block, _ = lax.optimization_barrier((_pack_small(gs), updates["w_in"][0]))
    (got_small,) = _grads_exchange([jnp.broadcast_to(small_block, (N_DEV, SMALL_ROWS, D))])
    small_sum = _unpack_small(_sum_slots(got_small, name="sum_small_grads", tr=16))
    loss = small_sum.pop("loss")[0]
    conv_all = small_sum.pop("conv_w").reshape(4, N_DEV, TAPS_PER_SHARD)
    small_sum["conv_w"] = lax.dynamic_index_in_dim(conv_all, mine, axis=1, keepdims=False)
    for name, g in small_sum.items():
        updates[name] = _adamw(given[name], g.reshape(given[name].shape), mom1[name], mom2[name], name=f"adamw_{name}")
    return (loss, grad_x, *[updates[name][k] for k in range(4) for name in WEIGHTS])
```

```python
import jax
import jax.numpy as jnp
from jax import lax
from jax.experimental import pallas as pl
from jax.experimental.pallas import tpu as pltpu

F32 = jnp.float32
BF16 = jnp.bfloat16
DELTA_PREC = lax.Precision.HIGH
MESH = pl.DeviceIdType.MESH

N_DEV = 8
D = 1024
CHUNK = 64
EPS = 1e-6
A_HEADS, A_DIM, A_WIDTH = 8, 64, 512
A_BAND = 9 * CHUNK
A_PAD = 8 * CHUNK
REL_CLIP = 128
B_HEADS, B_DIM = 4, 128
B_CONV = 1536
D_FF = 2816
D_IN = 5640
P_CONV, P_Z, P_GATE, P_BD, P_END = 1536, 3072, 3584, 5632, 5760
LANE = 128

ADAM_LR, ADAM_B1, ADAM_B2, ADAM_EPS, ADAM_WD, ADAM_STEP = 0.001, 0.9, 0.999, 1e-08, 0.01, 10

NT = (((1,), (1,)), ((), ()))
TN = (((0,), (0,)), ((), ()))
NN = (((1,), (0,)), ((), ()))

HBM_SPEC = pl.BlockSpec(memory_space=pltpu.HBM)


def _tile(n, target, align=LANE):
    if n <= target:
        return n
    best = None
    for t in range(align, target + 1, align):
        if n % t == 0:
            best = t
    assert best is not None, (n, target, align)
    return best


def _mm(a, b, *, name, ta=False, tb=False, add=None, out_dtype=F32, tm=1024, tn=640, tk=None):
    assert not (ta and tb)
    if ta:
        k_dim, m_dim = a.shape
    else:
        m_dim, k_dim = a.shape
    n_dim = b.shape[0] if tb else b.shape[1]
    assert b.shape[1 if tb else 0] == k_dim
    tm, tn = _tile(m_dim, tm), _tile(n_dim, tn)
    tk = _tile(k_dim, tk or (4096 if ta else 1024), 8 if ta else LANE)
    nk = k_dim // tk
    dn = TN if ta else NT if tb else NN

    def body(*refs):
        if add is None:
            a_ref, b_ref, o_ref = refs[:3]
            add_ref = None
        else:
            a_ref, b_ref, add_ref, o_ref = refs[:4]
        part = lax.dot_general(a_ref[...].astype(BF16), b_ref[...].astype(BF16), dn, preferred_element_type=F32)

        def finish(r):
            if add_ref is not None:
                r = r + add_ref[...]
            o_ref[...] = r.astype(o_ref.dtype)

        if nk == 1:
            finish(part)
        else:
            acc_ref = refs[-1]
            k = pl.program_id(2)

            @pl.when(k == 0)
            def _():
                acc_ref[...] = part

            @pl.when(k > 0)
            def _():
                acc_ref[...] += part

            @pl.when(k == nk - 1)
            def _():
                finish(acc_ref[...])

    a_spec = pl.BlockSpec((tk, tm), lambda i, j, k: (k, i)) if ta else pl.BlockSpec((tm, tk), lambda i, j, k: (i, k))
    b_spec = pl.BlockSpec((tn, tk), lambda i, j, k: (j, k)) if tb else pl.BlockSpec((tk, tn), lambda i, j, k: (k, j))
    in_specs = [a_spec, b_spec]
    args = [a, b]
    if add is not None:
        in_specs.append(pl.BlockSpec((tm, tn), lambda i, j, k: (i, j)))
        args.append(add)
    return pl.pallas_call(
        body,
        name=name,
        grid=(m_dim // tm, n_dim // tn, nk),
        in_specs=in_specs,
        out_specs=pl.BlockSpec((tm, tn), lambda i, j, k: (i, j)),
        out_shape=jax.ShapeDtypeStruct((m_dim, n_dim), out_dtype),
        scratch_shapes=[pltpu.VMEM((tm, tn), F32)] if nk > 1 else [],
        compiler_params=pltpu.CompilerParams(dimension_semantics=("parallel", "parallel", "arbitrary")),
    )(*args)


def _rowwise(fn, rows, bcs, outs, reds=(), *, name, tr, ncol=1):
    n_rows = rows[0][0].shape[0]
    tr = _tile(n_rows, tr, 8)
    nrow = n_rows // tr
    n_in, n_out = len(rows) + len(bcs), len(outs)

    def body(*refs):
        j, i = pl.program_id(0), pl.program_id(1)
        o_vals, r_vals = fn(*[r[...] for r in refs[:n_in]])
        for ref, val in zip(refs[n_in:n_in + n_out], o_vals):
            ref[...] = val.astype(ref.dtype)
        for ref, val, (_, _, stride) in zip(refs[n_in + n_out:], r_vals, reds):
            first = (i == 0) if stride else jnp.logical_and(i == 0, j == 0)

            @pl.when(first)
            def _():
                ref[...] = val

            @pl.when(jnp.logical_not(first))
            def _():
                ref[...] += val

    def spec(r, w, off, st, row_dep=True):
        if row_dep:
            return pl.BlockSpec((r, w), lambda j, i: (i, off + st * j))
        return pl.BlockSpec((r, w), lambda j, i: (0, off + st * j))

    in_specs = [spec(tr, w, off, st) for (_, w, off, st) in rows]
    in_specs += [spec(a.shape[0], w, off, st, False) for (a, w, off, st) in bcs]
    out_specs = [spec(tr, w, off, st) for (_, _, w, off, st) in outs]
    out_specs += [spec(1, w, 0, st, False) for (_, w, st) in reds]
    out_shape = [jax.ShapeDtypeStruct((n_rows, c), dt) for (c, dt, _, _, _) in outs]
    out_shape += [jax.ShapeDtypeStruct((1, c), F32) for (c, _, _) in reds]
    return pl.pallas_call(
        body,
        name=name,
        grid=(ncol, nrow),
        in_specs=in_specs,
        out_specs=out_specs,
        out_shape=out_shape,
        compiler_params=pltpu.CompilerParams(dimension_semantics=("arbitrary", "arbitrary")),
    )(*[r[0] for r in rows], *[b[0] for b in bcs])


def _full(a):
    return (a, a.shape[1], 0, 0)


def _rms(x, g):
    return x * lax.rsqrt(jnp.mean(x * x, axis=-1, keepdims=True) + EPS) * g


def _silu(x):
    return x * jax.nn.sigmoid(x)


def _softplus(x):
    return jnp.maximum(x, 0.0) + jnp.log(1.0 + jnp.exp(-jnp.abs(x)))


def _rms_fwd(x, g, *, name):
    (h,) = _rowwise(lambda xb, gb: ([_rms(xb, gb)], []), [_full(x)], [_full(g)], [(D, BF16, D, 0, 0)], name=name, tr=512)
    return h


def _rms_bwd(x, g, dh, dres, *, name):
    def fn(xb, dhb, dresb, gb):
        _, vjp = jax.vjp(_rms, xb, gb)
        dx, dg = vjp(dhb)
        return [dx + dresb], [dg]

    return _rowwise(fn, [_full(x), _full(dh), _full(dres)], [_full(g)], [(D, F32, D, 0, 0)], [(D, D, 0)], name=name, tr=256)


def _gate_scalars(raw, al_row, dtb_row):
    lane = lax.broadcasted_iota(jnp.int32, raw.shape, 1)
    beta = jax.nn.sigmoid(raw)
    g = -jnp.exp(al_row) * _softplus(raw + dtb_row)
    return jnp.where(lane < B_HEADS, beta, jnp.where(lane < 2 * B_HEADS, g, 0.0))


def _gated_norm(o, z, w):
    return _rms(o, w) * _silu(z)


def _merge(ga, gb, ta, tb):
    return jax.nn.sigmoid(ga) * ta + jax.nn.sigmoid(gb) * tb


def _swiglu(gu):
    return _silu(gu[:, :D_FF]) * gu[:, D_FF:]


def _head_loss(x2, pg, pp, tgt, g):
    x3 = x2 + jax.nn.sigmoid(pg) * pp
    err = _rms(x3, g) - tgt
    return 0.5 * jnp.sum(jnp.mean(err * err, axis=-1))


CONV_W = 256


def _conv_taps(x, w):
    row = lax.broadcasted_iota(jnp.int32, x.shape, 0)
    shifted = [x] + [jnp.where(row >= s, pltpu.roll(x, s, 0), 0.0) for s in (1, 2, 3)]
    pre = shifted[0] * w[3:4]
    for s in (1, 2, 3):
        pre = pre + shifted[s] * w[3 - s:4 - s]
    return pre, shifted


def _conv_fwd(projp, conv_w, n_batch, seq):
    ncol = B_CONV // CONV_W
    first = P_CONV // CONV_W

    def body(x_ref, w_ref, o_ref):
        pre, _ = _conv_taps(x_ref[...], w_ref[...])
        o_ref[...] = _silu(pre)

    return pl.pallas_call(
        body,
        name="conv_fwd",
        grid=(ncol, n_batch),
        in_specs=[pl.BlockSpec((seq, CONV_W), lambda j, b: (b, first + j)), pl.BlockSpec((4, CONV_W), lambda j, b: (0, j))],
        out_specs=pl.BlockSpec((seq, CONV_W), lambda j, b: (b, j)),
        out_shape=jax.ShapeDtypeStruct((n_batch * seq, B_CONV), F32),
        compiler_params=pltpu.CompilerParams(dimension_semantics=("parallel", "parallel")),
    )(projp, conv_w)


def _conv_bwd(projp, conv_w, dc, n_batch, seq):
    width = dc.shape[1]
    ncol = width // CONV_W
    first_x = P_CONV // CONV_W

    def body(x_ref, w_ref, dc_ref, dx_ref, dw_ref):
        b = pl.program_id(1)
        w = w_ref[...]
        pre, shifted = _conv_taps(x_ref[...], w)
        sg = jax.nn.sigmoid(pre)
        dpre = dc_ref[...] * (sg * (1.0 + pre * (1.0 - sg)))
        row = lax.broadcasted_iota(jnp.int32, dpre.shape, 0)
        dx = dpre * w[3:4]
        for s in (1, 2, 3):
            dx = dx + jnp.where(row < seq - s, pltpu.roll(dpre, seq - s, 0), 0.0) * w[3 - s:4 - s]
        dx_ref[...] = dx.astype(dx_ref.dtype)
        for s in (0, 1, 2, 3):
            part = jnp.sum(dpre * shifted[s], axis=0, keepdims=True)

            @pl.when(b == 0)
            def _():
                dw_ref[3 - s:4 - s, :] = part

            @pl.when(b > 0)
            def _():
                dw_ref[3 - s:4 - s, :] += part

    return pl.pallas_call(
        body,
        name="conv_bwd",
        grid=(ncol, n_batch),
        in_specs=[
            pl.BlockSpec((seq, CONV_W), lambda j, b: (b, first_x + j)),
            pl.BlockSpec((4, CONV_W), lambda j, b: (0, j)),
            pl.BlockSpec((seq, CONV_W), lambda j, b: (b, j)),
        ],
        out_specs=[pl.BlockSpec((seq, CONV_W), lambda j, b: (b, j)), pl.BlockSpec((4, CONV_W), lambda j, b: (0, j))],
        out_shape=[jax.ShapeDtypeStruct((n_batch * seq, width), BF16), jax.ShapeDtypeStruct((4, width), F32)],
        compiler_params=pltpu.CompilerParams(dimension_semantics=("arbitrary", "arbitrary")),
    )(projp, conv_w, dc)


def _attn_chunk(qc, kb, vb, bias2, valid, lane_lo):
    sel = (lane_lo, jnp.logical_not(lane_lo))
    items = [(i, e) for i in range(len(qc)) for e in (0, 1)]
    k16, v16 = [t.astype(BF16) for t in kb], [t.astype(BF16) for t in vb]
    qm = [(jnp.where(sel[e], qc[i], 0.0) * (A_DIM ** -0.5)).astype(BF16) for i, e in items]
    s = [lax.dot_general(qm[n], k16[i], NT, preferred_element_type=F32) + bias2[e] for n, (i, e) in enumerate(items)]
    s = [jnp.where(valid[i], s[n], -1e30) for n, (i, e) in enumerate(items)]
    p = [jnp.exp(t - lax.stop_gradient(jnp.max(t, axis=-1, keepdims=True))) for t in s]
    p = [t * (1.0 / jnp.sum(t, axis=-1, keepdims=True)) for t in p]
    o = [jnp.where(sel[e], jnp.dot(p[n].astype(BF16), v16[i], preferred_element_type=F32), 0.0)
         for n, (i, e) in enumerate(items)]
    return [o[2 * i] + o[2 * i + 1] for i in range(len(qc))]


ATTN_GROUP_FWD, ATTN_GROUP_BWD = 4, 4


def _attn_group(g, group, q_ref, kp_ref, vp_ref):
    col = lax.broadcasted_iota(jnp.int32, (CHUNK, A_BAND), 1)
    lane_lo = lax.broadcasted_iota(jnp.int32, (1, LANE), 1) < A_DIM
    starts = [pl.multiple_of((g * group + i) * CHUNK, CHUNK) for i in range(group)]
    rows = [pl.ds(r0, CHUNK) for r0 in starts]
    bands = [pl.ds(r0, A_BAND) for r0 in starts]
    valid = [col + r0 >= A_PAD for r0 in starts]
    loaded = [q_ref[r, :] for r in rows], [kp_ref[b, :] for b in bands], [vp_ref[b, :] for b in bands]
    return rows, bands, loaded, valid, lane_lo


def _attn_specs(seq):
    def blk(first):
        return pl.BlockSpec((seq, LANE), lambda hp, b: (b, first + hp))

    return blk, pl.BlockSpec((2, CHUNK, A_BAND), lambda hp, b: (hp, 0, 0))


def _attn_fwd(projp, bias, n_batch, seq):
    nc = seq // CHUNK
    blk, bias_spec = _attn_specs(seq)

    def body(q_ref, k_ref, v_ref, b_ref, o_ref, kp_ref, vp_ref):
        kp_ref[0:A_PAD, :] = jnp.zeros((A_PAD, LANE), F32)
        vp_ref[0:A_PAD, :] = jnp.zeros((A_PAD, LANE), F32)
        kp_ref[A_PAD:, :] = k_ref[...]
        vp_ref[A_PAD:, :] = v_ref[...]
        bias2 = b_ref[...]

        def step(g, carry):
            rows, _, (qc, kb, vb), valid, lane_lo = _attn_group(g, ATTN_GROUP_FWD, q_ref, kp_ref, vp_ref)
            out = _attn_chunk(qc, kb, vb, bias2, valid, lane_lo)
            for r, o in zip(rows, out):
                o_ref[r, :] = o.astype(o_ref.dtype)
            return carry

        lax.fori_loop(0, nc // ATTN_GROUP_FWD, step, 0)

    return pl.pallas_call(
        body,
        name="attn_fwd",
        grid=(A_HEADS // 2, n_batch),
        in_specs=[blk(0), blk(4), blk(8), bias_spec],
        out_specs=pl.BlockSpec((seq, LANE), lambda hp, b: (b, hp)),
        out_shape=jax.ShapeDtypeStruct((n_batch * seq, A_WIDTH), BF16),
        scratch_shapes=[pltpu.VMEM((A_PAD + seq, LANE), F32), pltpu.VMEM((A_PAD + seq, LANE), F32)],
        compiler_params=pltpu.CompilerParams(dimension_semantics=("parallel", "parallel")),
    )(projp, projp, projp, bias)


def _attn_bwd(projp, bias, dy, n_batch, seq):
    nc = seq // CHUNK
    blk, bias_spec = _attn_specs(seq)
    out_blk = pl.BlockSpec((seq, LANE), lambda hp, b: (b, hp))

    def body(q_ref, k_ref, v_ref, b_ref, dy_ref, dq_ref, dk_ref, dv_ref, db_ref, kp_ref, vp_ref, dkp_ref, dvp_ref):
        b = pl.program_id(1)
        kp_ref[0:A_PAD, :] = jnp.zeros((A_PAD, LANE), F32)
        vp_ref[0:A_PAD, :] = jnp.zeros((A_PAD, LANE), F32)
        kp_ref[A_PAD:, :] = k_ref[...]
        vp_ref[A_PAD:, :] = v_ref[...]
        dkp_ref[...] = jnp.zeros_like(dkp_ref)
        dvp_ref[...] = jnp.zeros_like(dvp_ref)
        bias2 = b_ref[...]

        @pl.when(b == 0)
        def _():
            db_ref[...] = jnp.zeros_like(db_ref)

        def step(g, carry):
            rows, bands, (qc, kb, vb), valid, lane_lo = _attn_group(g, ATTN_GROUP_BWD, q_ref, kp_ref, vp_ref)
            _, vjp = jax.vjp(lambda q, k, v, bb: _attn_chunk(q, k, v, bb, valid, lane_lo), qc, kb, vb, bias2)
            dq, dk, dv, dbias = vjp([dy_ref[r, :] for r in rows])
            for i, r in enumerate(rows):
                dq_ref[r, :] = dq[i].astype(dq_ref.dtype)
            for i, band in enumerate(bands):
                dkp_ref[band, :] += dk[i]
                dvp_ref[band, :] += dv[i]
            db_ref[...] += dbias
            return carry

        lax.fori_loop(0, nc // ATTN_GROUP_BWD, step, 0)
        dk_ref[...] = dkp_ref[A_PAD:, :].astype(dk_ref.dtype)
        dv_ref[...] = dvp_ref[A_PAD:, :].astype(dv_ref.dtype)

    n_tok = n_batch * seq
    pad = pltpu.VMEM((A_PAD + seq, LANE), F32)
    return pl.pallas_call(
        body,
        name="attn_bwd",
        grid=(A_HEADS // 2, n_batch),
        in_specs=[blk(0), blk(4), blk(8), bias_spec, out_blk],
        out_specs=[out_blk, out_blk, out_blk, bias_spec],
        out_shape=[jax.ShapeDtypeStruct((n_tok, A_WIDTH), BF16)] * 3 + [jax.ShapeDtypeStruct((A_HEADS, CHUNK, A_BAND), F32)],
        scratch_shapes=[pad, pad, pad, pad],
        compiler_params=pltpu.CompilerParams(dimension_semantics=("arbitrary", "arbitrary")),
    )(projp, projp, projp, bias, dy)


def _rel_bias_table(rel_bias):
    span = CHUNK + A_BAND - 1
    near = REL_CLIP + CHUNK
    far = jnp.broadcast_to(rel_bias[:, 2 * REL_CLIP:], (A_HEADS, span - near))
    t = jnp.concatenate([rel_bias[:, 2 * REL_CLIP + 1 - near:], far], axis=1)
    u = jnp.concatenate([t[:, :A_BAND][:, ::-1], t[:, A_BAND:][:, ::-1]], axis=1)
    rolled = jnp.tile(u, (1, CHUNK))[:, :CHUNK * (span - 1)].reshape(A_HEADS, CHUNK, span - 1)
    return rolled[:, :, :A_BAND]


def _dot(a, b, dn=NN):
    return lax.dot_general(a, b, dn, precision=DELTA_PREC, preferred_element_type=F32)


def _each(fn, *lists):
    return [fn(*vals) for vals in zip(*lists)]


@jax.custom_vjp
def _saved_inverse(x, inv):
    return inv


def _saved_inverse_fwd(x, inv):
    return inv, inv


def _saved_inverse_bwd(inv, ct):
    return _dot(_dot(inv, ct, TN), inv, NT), jnp.zeros_like(inv)


_saved_inverse.defvjp(_saved_inverse_fwd, _saved_inverse_bwd)


def _delta_chunk(r_state, cq, ck, cv, beta, g, saved_inv=None):
    ii = lax.broadcasted_iota(jnp.int32, (CHUNK, CHUNK), 0)
    jj = lax.broadcasted_iota(jnp.int32, (CHUNK, CHUNK), 1)
    incl, strict, eye = ii >= jj, ii > jj, ii == jj
    q = _each(lambda t: t * lax.rsqrt(jnp.sum(t * t, axis=-1, keepdims=True) + EPS) * (B_DIM ** -0.5), cq)
    k = _each(lambda t: t * lax.rsqrt(jnp.sum(t * t, axis=-1, keepdims=True) + EPS), ck)
    g_b = _each(lambda t: jnp.broadcast_to(t, (CHUNK, CHUNK)), g)
    g_row = _each(lambda t: jnp.sum(jnp.where(eye, t, 0.0), axis=0, keepdims=True), g_b)
    gc_col = _each(lambda t: jnp.sum(jnp.where(incl, t, 0.0), axis=1, keepdims=True), g_row)
    gc_row = _each(lambda t: jnp.sum(jnp.where(ii <= jj, t, 0.0), axis=0, keepdims=True), g_b)
    decay = _each(lambda c, r: jnp.where(incl, jnp.exp(jnp.where(incl, c - r, 0.0)), 0.0), gc_col, gc_row)
    kk = _each(lambda t: _dot(t, t, NT), k)
    x = _each(lambda b, m, d: jnp.where(strict, -(b * m * d), 0.0), beta, kk, decay)
    if saved_inv is None:
        inv = _each(lambda t: jnp.where(eye, 1.0, 0.0) + t, x)
        pw = x
        for _ in range(5):
            pw = _each(lambda t: _dot(t, t), pw)
            inv = _each(lambda t, s: t + _dot(t, s), inv, pw)
    else:
        inv = _each(_saved_inverse, x, saved_inv)
    egc = _each(jnp.exp, gc_col)
    u = _each(lambda t, b, v: _dot(t, b * v), inv, beta, cv)
    wk = _each(lambda t, b, e, kh: _dot(t, (b * e) * kh), inv, beta, egc, k)
    pqk = _each(lambda qh, kh, d: _dot(qh, kh, NT) * d, q, k, decay)
    g_last = _each(lambda c: c[CHUNK - 1:CHUNK, :], gc_col)
    kdec = _each(lambda kh, gl, c: kh * jnp.exp(gl - c), k, g_last, gc_col)
    w = _each(lambda uh, wkh, r: uh - _dot(wkh, r), u, wk, r_state)
    o = _each(lambda e, qh, r, ph, wh: e * _dot(qh, r) + _dot(ph, wh), egc, q, r_state, pqk, w)
    r_new = _each(lambda gl, r, kd, wh: jnp.exp(gl) * r + _dot(kd, wh, TN), g_last, r_state, kdec, w)
    return o, r_new, inv


DELTA_BLK = 512


def _delta_blocks(n_batch, seq):
    nblk = seq // DELTA_BLK
    cpb = DELTA_BLK // CHUNK

    def rows(width, order):
        return pl.BlockSpec((DELTA_BLK, width), lambda b, i: (b * nblk + order(i), 0))

    def states(order, side):
        return pl.BlockSpec((cpb, B_HEADS, side, side), lambda b, i: (b * nblk + order(i), 0, 0, 0))

    return nblk, cpb, rows, states


def _head_cols(h):
    return [pl.ds(part * B_HEADS * B_DIM + h * B_DIM, B_DIM) for part in range(3)]


def _load_heads(c_ref, bg_ref, state_ref, rows):
    bg_c = bg_ref[rows, :]
    cols = [_head_cols(h) for h in range(B_HEADS)]
    return ([state_ref[h] for h in range(B_HEADS)], [c_ref[rows, c[0]] for c in cols], [c_ref[rows, c[1]] for c in cols],
            [c_ref[rows, c[2]] for c in cols], [bg_c[:, h:h + 1] for h in range(B_HEADS)],
            [bg_c[:, B_HEADS + h:B_HEADS + h + 1] for h in range(B_HEADS)])


def _delta_fwd(conv, bg, n_batch, seq):
    nblk, cpb, rows_spec, states_spec = _delta_blocks(n_batch, seq)

    def forward(i):
        return i

    def body(c_ref, bg_ref, o_ref, st_ref, inv_ref, r_ref):
        @pl.when(pl.program_id(1) == 0)
        def _():
            r_ref[...] = jnp.zeros_like(r_ref)

        def step(c, carry):
            rows = pl.ds(pl.multiple_of(c * CHUNK, CHUNK), CHUNK)
            args = _load_heads(c_ref, bg_ref, r_ref, rows)
            o, r_new, inv = _delta_chunk(*args)
            for h in range(B_HEADS):
                st_ref[c, h] = args[0][h]
                inv_ref[c, h] = inv[h]
                o_ref[rows, pl.ds(h * B_DIM, B_DIM)] = o[h]
            for h in range(B_HEADS):
                r_ref[h] = r_new[h]
            return carry

        lax.fori_loop(0, cpb, step, 0)

    n_tok = n_batch * seq
    return pl.pallas_call(
        body,
        name="delta_fwd",
        grid=(n_batch, nblk),
        in_specs=[rows_spec(B_CONV, forward), rows_spec(LANE, forward)],
        out_specs=[rows_spec(B_HEADS * B_DIM, forward), states_spec(forward, B_DIM), states_spec(forward, CHUNK)],
        out_shape=[jax.ShapeDtypeStruct((n_tok, B_HEADS * B_DIM), F32),
                   jax.ShapeDtypeStruct((n_tok // CHUNK, B_HEADS, B_DIM, B_DIM), F32),
                   jax.ShapeDtypeStruct((n_tok // CHUNK, B_HEADS, CHUNK, CHUNK), F32)],
        scratch_shapes=[pltpu.VMEM((B_HEADS, B_DIM, B_DIM), F32)],
        compiler_params=pltpu.CompilerParams(dimension_semantics=("arbitrary", "arbitrary")),
    )(conv, bg)


def _delta_bwd(conv, bg, states, inverses, do, n_batch, seq):
    nblk, cpb, rows_spec, states_spec = _delta_blocks(n_batch, seq)

    def backward(i):
        return nblk - 1 - i

    def body(c_ref, bg_ref, st_ref, inv_ref, do_ref, dc_ref, dbg_ref, dr_ref):
        @pl.when(pl.program_id(1) == 0)
        def _():
            dr_ref[...] = jnp.zeros_like(dr_ref)

        def step(n, carry):
            c = cpb - 1 - n
            rows = pl.ds(pl.multiple_of(c * CHUNK, CHUNK), CHUNK)
            saved = [inv_ref[c, h] for h in range(B_HEADS)]
            _, vjp = jax.vjp(lambda *args: _delta_chunk(*args, saved_inv=saved)[:2],
                             *_load_heads(c_ref, bg_ref, st_ref.at[c], rows))
            do = [do_ref[rows, pl.ds(h * B_DIM, B_DIM)] for h in range(B_HEADS)]
            dr, dq, dk, dv, dbeta, dg = vjp((do, [dr_ref[h] for h in range(B_HEADS)]))
            lane = lax.broadcasted_iota(jnp.int32, (CHUNK, LANE), 1)
            dbg = jnp.zeros((CHUNK, LANE), F32)
            for h in range(B_HEADS):
                cq, ck, cv = _head_cols(h)
                dr_ref[h] = dr[h]
                dc_ref[rows, cq] = dq[h]
                dc_ref[rows, ck] = dk[h]
                dc_ref[rows, cv] = dv[h]
                dbg = dbg + jnp.where(lane == h, dbeta[h], 0.0) + jnp.where(lane == h + B_HEADS, dg[h], 0.0)
            dbg_ref[rows, :] = dbg
            return carry

        lax.fori_loop(0, cpb, step, 0)

    n_tok = n_batch * seq
    return pl.pallas_call(
        body,
        name="delta_bwd",
        grid=(n_batch, nblk),
        in_specs=[rows_spec(B_CONV, backward), rows_spec(LANE, backward), states_spec(backward, B_DIM),
                  states_spec(backward, CHUNK), rows_spec(B_HEADS * B_DIM, backward)],
        out_specs=[rows_spec(B_CONV, backward), rows_spec(LANE, backward)],
        out_shape=[jax.ShapeDtypeStruct((n_tok, B_CONV), F32), jax.ShapeDtypeStruct((n_tok, LANE), F32)],
        scratch_shapes=[pltpu.VMEM((B_HEADS, B_DIM, B_DIM), F32)],
        compiler_params=pltpu.CompilerParams(dimension_semantics=("arbitrary", "arbitrary")),
    )(conv, bg, states, inverses, do)


def _lane_row(vec4, first):
    return jnp.concatenate([jnp.zeros((1, first), F32), vec4.reshape(1, B_HEADS).astype(F32),
                            jnp.zeros((1, LANE - first - B_HEADS), F32)], axis=1)


def _local_step(x3d, p3d, tgt3d, w_in, small, rest_weights, send_grads, send_w_in):
    n_batch, seq, _ = x3d.shape
    n_tok = n_batch * seq
    x, p, tgt = x3d.reshape(n_tok, D), p3d.reshape(n_tok, -1), tgt3d.reshape(n_tok, D)
    g_mix, g_ffn, g_ple, g_final = (small[k].reshape(1, D) for k in ("g_mix", "g_ffn", "g_ple", "g_final"))
    w_onorm = small["w_onorm"].reshape(1, B_DIM)
    al_row = _lane_row(small["a_log"], B_HEADS)
    dtb_row = _lane_row(small["dt_bias"], B_HEADS)
    rel_bias = small["rel_bias"].reshape(A_HEADS, -1)
    bias = _rel_bias_table(rel_bias)
    conv_w = small["conv_w"].reshape(4, B_CONV)
    bd_blk = P_BD // LANE

    h1 = _rms_fwd(x, g_mix, name="rms_mix")
    projp = _mm(h1, w_in, tb=True, name="mm_proj", tn=640)
    y_a = _attn_fwd(projp, bias, n_batch, seq)
    conv = _conv_fwd(projp, conv_w, n_batch, seq)
    (bg,) = _rowwise(lambda raw, al, dtb: ([_gate_scalars(raw, al, dtb)], []), [(projp, LANE, bd_blk, 0)],
                     [_full(al_row), _full(dtb_row)], [(LANE, F32, LANE, 0, 0)], name="gate_scalars", tr=1024)
    o_b, states, inverses = _delta_fwd(conv, bg, n_batch, seq)
    (y_b,) = _rowwise(lambda o, z, wn: ([_gated_norm(o, z, wn)], []), [(o_b, LANE, 0, 1), (projp, LANE, P_Z // LANE, 1)],
                      [_full(w_onorm)], [(B_HEADS * B_DIM, BF16, LANE, 0, 1)], name="gated_norm", tr=1024, ncol=B_HEADS)
    w = rest_weights(y_b)
    t_a = _mm(y_a, w["w_branch_a"], tb=True, name="mm_branch_a", tn=1024)
    t_b = _mm(y_b, w["w_branch_b"], tb=True, name="mm_branch_b", tn=1024)
    half = D // 2
    gate_rows = [(projp, half, P_GATE // half, 1), (projp, half, P_GATE // half + 2, 1), (t_a, half, 0, 1), (t_b, half, 0, 1)]
    (merged,) = _rowwise(lambda ga, gb, ta, tb: ([_merge(ga, gb, ta, tb)], []), gate_rows, [], [(D, BF16, half, 0, 1)],
                         name="merge", tr=512, ncol=2)
    x1 = _mm(merged, w["w_out"], add=x, name="mm_out", tn=1024)
    h2 = _rms_fwd(x1, g_ffn, name="rms_ffn")
    gu = _mm(h2, w["w_gate_up"], tb=True, name="mm_gate_up", tn=512)
    (act,) = _rowwise(lambda gub: ([_swiglu(gub)], []), [_full(gu)], [], [(D_FF, BF16, D_FF, 0, 0)], name="swiglu", tr=256)
    x2 = _mm(act, w["w_down"], add=x1, name="mm_down", tn=1024, tk=1408)
    h3 = _rms_fwd(x2, g_ple, name="rms_ple")
    pg = _mm(h3, w["w_ple_gate"], name="mm_ple_gate", tn=1024)
    pp = _mm(p, w["w_ple_proj"], tb=True, name="mm_ple_proj", tn=1024)

    def head_fn(x2b, pgb, ppb, tb, gb):
        loss, (dx2, dpg, dpp, dg) = jax.value_and_grad(_head_loss, argnums=(0, 1, 2, 4))(x2b, pgb, ppb, tb, gb)
        return [dx2, dpg, dpp], [dg, jnp.full((1, LANE), loss, F32)]

    dx3, dpg, dpp, dg_final, loss_row = _rowwise(
        head_fn, [_full(x2), _full(pg), _full(pp), _full(tgt)], [_full(g_final)],
        [(D, F32, D, 0, 0), (D, BF16, D, 0, 0), (D, BF16, D, 0, 0)], [(D, D, 0), (LANE, LANE, 0)], name="loss_head", tr=256)
    gw = {}
    gw["w_ple_proj"] = _mm(dpp, p, ta=True, out_dtype=BF16, name="mm_d_ple_proj", tn=256)
    gw["w_ple_gate"] = _mm(h3, dpg, ta=True, out_dtype=BF16, name="mm_d_ple_gate", tn=512)
    dh3 = _mm(dpg, w["w_ple_gate"], tb=True, name="mm_dh3", tn=1024)
    dx2, dg_ple = _rms_bwd(x2, g_ple, dh3, dx3, name="rms_ple_bwd")
    gw["w_down"] = _mm(act, dx2, ta=True, out_dtype=BF16, name="mm_d_down", tm=1408, tn=512, tk=2048)
    dact = _mm(dx2, w["w_down"], tb=True, name="mm_dact", tn=1408)

    def swiglu_bwd(gub, dab):
        _, vjp = jax.vjp(_swiglu, gub)
        return [vjp(dab)[0]], []

    (dgu,) = _rowwise(swiglu_bwd, [_full(gu), _full(dact)], [], [(2 * D_FF, BF16, 2 * D_FF, 0, 0)], name="swiglu_bwd", tr=256)
    gw["w_gate_up"] = _mm(dgu, h2, ta=True, out_dtype=BF16, name="mm_d_gate_up", tm=512, tn=1024)
    dh2 = _mm(dgu, w["w_gate_up"], name="mm_dh2", tn=1024, tk=1408)
    dx1, dg_ffn = _rms_bwd(x1, g_ffn, dh2, dx2, name="rms_ffn_bwd")
    gw["w_out"] = _mm(merged, dx1, ta=True, out_dtype=BF16, name="mm_d_out", tn=512)
    dmerged = _mm(dx1, w["w_out"], tb=True, name="mm_dmerged", tn=1024)

    def merge_bwd(ga, gb, ta, tb, dm):
        _, vjp = jax.vjp(_merge, ga, gb, ta, tb)
        return list(vjp(dm)), []

    dga, dgb, dta, dtb = _rowwise(merge_bwd, gate_rows + [(dmerged, half, 0, 1)], [], [(D, BF16, half, 0, 1)] * 4,
                                  name="merge_bwd", tr=512, ncol=2)
    gw["w_branch_a"] = _mm(dta, y_a, ta=True, out_dtype=BF16, name="mm_d_branch_a", tn=512)
    gw["w_branch_b"] = _mm(dtb, y_b, ta=True, out_dtype=BF16, name="mm_d_branch_b", tn=512)
    dya = _mm(dta, w["w_branch_a"], name="mm_dya", tn=512)
    dyb = _mm(dtb, w["w_branch_b"], name="mm_dyb", tn=512)

    w_onorm = w_onorm + send_grads(gw)[0, 0]

    def gated_norm_bwd(o, z, dy, wn):
        _, vjp = jax.vjp(_gated_norm, o, z, wn)
        do, dz, dwn = vjp(dy)
        return [do, dz], [dwn]

    do_b, dz, dw_onorm = _rowwise(
        gated_norm_bwd, [(o_b, LANE, 0, 1), (projp, LANE, P_Z // LANE, 1), (dyb, LANE, 0, 1)], [_full(w_onorm)],
        [(B_HEADS * B_DIM, F32, LANE, 0, 1), (B_HEADS * B_DIM, BF16, LANE, 0, 1)], [(B_DIM, B_DIM, 0)],
        name="gated_norm_bwd", tr=1024, ncol=B_HEADS)
    dconv_out, dbg = _delta_bwd(conv, bg, states, inverses, do_b, n_batch, seq)

    def gate_scalars_bwd(raw, dbgb, al, dtb):
        _, vjp = jax.vjp(_gate_scalars, raw, al, dtb)
        draw, dal, ddtb = vjp(dbgb)
        return [draw], [dal, ddtb]

    dbd, dal_row, ddtb_row = _rowwise(gate_scalars_bwd, [(projp, LANE, bd_blk, 0), _full(dbg)], [_full(al_row), _full(dtb_row)],
                                      [(LANE, BF16, LANE, 0, 0)], [(LANE, LANE, 0), (LANE, LANE, 0)], name="gate_scalars_bwd",
                                      tr=1024)
    dconv, dconv_w = _conv_bwd(projp, conv_w, dconv_out, n_batch, seq)
    dq_a, dk_a, dv_a, dbias = _attn_bwd(projp, bias, dya, n_batch, seq)
    dprojp = jnp.concatenate([dq_a, dk_a, dv_a, dconv, dz, dga, dgb, dbd], axis=1)
    sent = send_w_in(_mm(dprojp, h1, ta=True, out_dtype=BF16, name="mm_d_in", tm=640, tn=1024))
    sent, dprojp = lax.optimization_barrier((sent, dprojp))
    dh1 = _mm(dprojp, w_in, name="mm_dh1", tn=1024, tk=1152)
    grad_x, dg_mix = _rms_bwd(x, g_mix + sent[0, 0], dh1, dx1, name="rms_mix_bwd")

    _, bias_vjp = jax.vjp(_rel_bias_table, rel_bias)
    gs = {
        "g_mix": dg_mix, "g_ffn": dg_ffn, "g_ple": dg_ple, "g_final": dg_final, "w_onorm": dw_onorm,
        "conv_w": dconv_w, "rel_bias": bias_vjp(dbias)[0],
        "a_log": dal_row[0, B_HEADS:2 * B_HEADS], "dt_bias": ddtb_row[0, B_HEADS:2 * B_HEADS],
    }
    return loss_row[:, :1], grad_x.reshape(n_batch, seq, D), gs


MATRICES = (("w_in", 1), ("w_gate_up", 1), ("w_branch_a", 1), ("w_branch_b", 1), ("w_out", 0), ("w_down", 0),
            ("w_ple_gate", 0), ("w_ple_proj", 1))
TAPS_PER_SHARD = B_CONV // N_DEV


def _held(shard, axis):
    return shard if axis == 0 else shard.T


def _from_gathered(slabs):
    return slabs.reshape(-1, slabs.shape[-1])


def _to_owner(held):
    return held.reshape(N_DEV, held.shape[0] // N_DEV, held.shape[1])


def _permute_w_in(held):
    n_gate = P_BD - P_GATE
    row = lax.broadcasted_iota(jnp.int32, (P_END, 1), 0)
    same = jnp.pad(held, ((0, P_END - D_IN), (0, 0)))
    up = jnp.pad(held[8:], ((0, P_END - D_IN + 8), (0, 0)))
    down = jnp.pad(held[:P_GATE + 8], ((n_gate, P_END - P_BD - 8), (0, 0)))
    zero = jnp.zeros((), held.dtype)
    return jnp.where(row < P_GATE, same, jnp.where(row < P_BD, up, jnp.where(row < P_BD + 8, down, zero)))


def _unpermute_w_in(gp):
    n_gate = P_BD - P_GATE
    row = lax.broadcasted_iota(jnp.int32, (D_IN, 1), 0)
    same = gp[:D_IN]
    up = jnp.pad(gp[n_gate:], ((0, D_IN - (P_END - n_gate)), (0, 0)))
    down = jnp.pad(gp[:P_BD], ((8, 0), (0, 0)))
    return jnp.where(row < P_GATE, same, jnp.where(row < P_GATE + 8, up, down))


SMALL_ROWS = 16
SMALL_LAYOUT = (("g_mix", 0, D), ("g_ffn", 1, D), ("g_ple", 2, D), ("g_final", 3, D), ("conv_w", 4, 4 * B_CONV),
                ("rel_bias", 10, A_HEADS * (2 * REL_CLIP + 1)), ("w_onorm", 13, B_DIM), ("a_log", 14, B_HEADS),
                ("dt_bias", 14, B_HEADS), ("loss", 15, 1))


def _pack_small(gs):
    rows = {}
    for name, row, n in SMALL_LAYOUT:
        rows.setdefault(row, []).append(gs[name].reshape(-1).astype(F32))
    parts = []
    for row in sorted(rows):
        flat = jnp.concatenate(rows[row])
        parts.append(jnp.concatenate([flat, jnp.zeros((-flat.shape[0] % D,), F32)]))
    flat = jnp.concatenate(parts)
    assert flat.shape[0] == SMALL_ROWS * D, flat.shape
    return flat.reshape(SMALL_ROWS, D)


def _unpack_small(blk):
    flat, out, used = blk.reshape(-1), {}, {}
    for name, row, n in SMALL_LAYOUT:
        start = row * D + used.get(row, 0)
        out[name] = flat[start:start + n]
        used[row] = used.get(row, 0) + n
    return out


def _position():
    return lax.axis_index("x"), lax.axis_index("y"), lax.axis_index("c")


PEERS = N_DEV - 1


def _comm_call(body, arrays, out_shapes, *, name):
    n = len(arrays)
    return pl.pallas_call(
        body,
        name=name,
        out_shape=out_shapes,
        in_specs=[HBM_SPEC] * n,
        out_specs=[HBM_SPEC] * n,
        scratch_shapes=[pltpu.SemaphoreType.DMA((PEERS * n,)), pltpu.SemaphoreType.DMA((PEERS * n,)),
                        pltpu.SemaphoreType.DMA((n,))],
    )(*arrays)


def _weights_allgather(shards):
    n = len(shards)

    def body(*refs):
        ins, outs = refs[:n], refs[n:2 * n]
        send_sems, recv_sems, local_sems = refs[2 * n:]
        x, y, c = _position()
        me, sibling = (x, y, c), (x, y, 1 - c)
        chips = [(1 - x, y), (x, 1 - y), (1 - x, 1 - y)]

        def slab(a, px, py, pc):
            return outs[a].at[4 * px + 2 * py + pc]

        def copy(a, k, block, to, src=None):
            return pltpu.make_async_remote_copy(src_ref=slab(a, *block) if src is None else src, dst_ref=slab(a, *block),
                                                send_sem=send_sems.at[PEERS * a + k], recv_sem=recv_sems.at[PEERS * a + k],
                                                device_id=to, device_id_type=MESH)

        local = [pltpu.make_async_copy(ins[a], slab(a, *me), local_sems.at[a]) for a in range(n)]
        sent = [copy(a, 1 + j, me, (*chip, c), src=ins[a]) for a in range(n) for j, chip in enumerate(chips)]
        sent += [copy(a, 0, me, sibling, src=ins[a]) for a in range(n)]
        for cp in sent + local:
            cp.start()
        for a in range(n):
            for j, chip in enumerate(chips):
                copy(a, 1 + j, (*chip, c), me).wait_recv()
                passed = copy(a, 4 + j, (*chip, c), sibling)
                passed.start()
                sent.append(passed)
        for a in range(n):
            copy(a, 0, sibling, me).wait_recv()
            for j, chip in enumerate(chips):
                copy(a, 4 + j, (*chip, 1 - c), me).wait_recv()
        for cp in sent:
            cp.wait_send()
        for cp in local:
            cp.wait()

    return _comm_call(body, shards, [jax.ShapeDtypeStruct((N_DEV,) + s.shape, s.dtype) for s in shards],
                      name="weights_allgather")


def _grads_exchange(by_owner):
    n = len(by_owner)

    def body(*refs):
        ins, outs = refs[:n], refs[n:2 * n]
        send_sems, recv_sems, local_sems = refs[2 * n:]
        x, y, c = _position()
        mine = 4 * x + 2 * y + c
        local = [pltpu.make_async_copy(ins[a].at[mine], outs[a].at[mine], local_sems.at[a]) for a in range(n)]
        for cp in local:
            cp.start()
        flips = [(dx, dy, dc) for dx in (0, 1) for dy in (0, 1) for dc in (0, 1) if dx + dy + dc]
        pending = []
        for k, (dx, dy, dc) in enumerate(flips):
            px, py, pc = (1 - x if dx else x), (1 - y if dy else y), (1 - c if dc else c)
            peer = 4 * px + 2 * py + pc
            for a in range(n):
                def remote(slot):
                    return pltpu.make_async_remote_copy(src_ref=ins[a].at[peer], dst_ref=outs[a].at[slot],
                                                        send_sem=send_sems.at[PEERS * a + k], recv_sem=recv_sems.at[PEERS * a + k],
                                                        device_id=(px, py, pc), device_id_type=MESH)

                sent = remote(mine)
                sent.start()
                pending.append((sent, remote(peer)))
        for sent, landed in pending:
            landed.wait_recv()
            sent.wait_send()
        for cp in local:
            cp.wait()

    return _comm_call(body, by_owner, [jax.ShapeDtypeStruct(g.shape, g.dtype) for g in by_owner], name="grads_exchange")


SEM_SPEC = pl.BlockSpec(memory_space=pltpu.SEMAPHORE)
DATAFLOW = pltpu.SideEffectType.DATAFLOW_SIDE_EFFECTING


def _peer_copies(srcs, lands, send_sems, recv_sems, by_owner, arrival):
    x, y, c = _position()
    mine = 4 * x + 2 * y + c
    copies = []
    for k, (dx, dy, dc) in enumerate([(dx, dy, dc) for dx in (0, 1) for dy in (0, 1) for dc in (0, 1) if dx + dy + dc]):
        px, py, pc = (1 - x if dx else x), (1 - y if dy else y), (1 - c if dc else c)
        peer = 4 * px + 2 * py + pc
        for a, (src, land) in enumerate(zip(srcs, lands)):
            copies.append(pltpu.make_async_remote_copy(
                src_ref=src.at[peer] if by_owner else src, dst_ref=land.at[peer if arrival else mine],
                send_sem=send_sems.at[PEERS * a + k], recv_sem=recv_sems.at[PEERS * a + k],
                device_id=(px, py, pc), device_id_type=MESH))
    return copies


def _exchange_start(sources, by_owner, *, name):
    n = len(sources)
    lands = [lax.empty((N_DEV,) + (s.shape[1:] if by_owner else s.shape), s.dtype) for s in sources]

    def body(*refs):
        send_sems, recv_sems, token = refs[2 * n], refs[2 * n + 1], refs[-1]
        for copy in _peer_copies(refs[:n], refs[n:2 * n], send_sems, recv_sems, by_owner, arrival=False):
            copy.start()
        token[...] = jnp.zeros_like(token)

    sems = pltpu.SemaphoreType.DMA((PEERS * n,))
    outs = pl.pallas_call(
        body,
        name=name,
        out_shape=(sems, sems, *[pltpu.HBM(a.shape, a.dtype) for a in sources + lands], jax.ShapeDtypeStruct((8, LANE), F32)),
        in_specs=[HBM_SPEC] * (2 * n),
        out_specs=(SEM_SPEC, SEM_SPEC, *[HBM_SPEC] * (2 * n), pl.BlockSpec(memory_space=pltpu.VMEM)),
        input_output_aliases={i: 2 + i for i in range(2 * n)},
        compiler_params=pltpu.CompilerParams(has_side_effects=DATAFLOW),
    )(*[pltpu.with_memory_space_constraint(a, pltpu.HBM) for a in sources + lands])
    return outs[:-1], outs[-1]


def _exchange_wait(started, after, by_owner, *, name):
    send_sems, recv_sems, *arrays = started
    n = len(arrays) // 2

    def body(*refs):
        for copy in _peer_copies(refs[:n], refs[n:2 * n], refs[2 * n], refs[2 * n + 1], by_owner, arrival=True):
            copy.wait_send()
            copy.wait_recv()

    outs = pl.pallas_call(
        body,
        name=name,
        out_shape=[pltpu.HBM(a.shape, a.dtype) for a in arrays],
        in_specs=[HBM_SPEC] * (2 * n) + [SEM_SPEC, SEM_SPEC, pl.BlockSpec(memory_space=pl.ANY)],
        out_specs=[HBM_SPEC] * (2 * n),
        input_output_aliases={i: i for i in range(2 * n)},
        compiler_params=pltpu.CompilerParams(has_side_effects=DATAFLOW),
    )(*arrays, send_sems, recv_sems, after)
    return outs[:n], outs[n:]


def _slot_sum(g_ref, own_ref):
    if own_ref is not None:
        x, y, c = _position()
        mine = 4 * x + 2 * y + c
    acc = None
    for j in range(N_DEV):
        part = g_ref[j] if own_ref is None else jnp.where(mine == j, own_ref[...], g_ref[j])
        acc = part.astype(F32) if acc is None else acc + part.astype(F32)
    return acc


def _sum_slots(got, *, name, tr):
    _, rows, cols = got.shape
    tr = _tile(rows, tr, 16)

    def body(g_ref, o_ref):
        o_ref[...] = _slot_sum(g_ref, None)

    return pl.pallas_call(
        body,
        name=name,
        grid=(rows // tr,),
        in_specs=[pl.BlockSpec((N_DEV, tr, cols), lambda i: (0, i, 0))],
        out_specs=pl.BlockSpec((tr, cols), lambda i: (i, 0)),
        out_shape=jax.ShapeDtypeStruct((rows, cols), F32),
        compiler_params=pltpu.CompilerParams(dimension_semantics=("parallel",)),
    )(got)


def _adamw(wt, g, m, v, *, name, own=None):
    slots = own is not None
    shape = wt.shape
    two_d = (-1, shape[-1]) if wt.ndim > 1 else (1, -1)
    args = [a.reshape(two_d) for a in (wt, m, v)]
    rows, cols = args[0].shape
    if rows % 16 == 0:
        tr, tc = _tile(rows, 256, 16), cols
    else:
        tr, tc = rows, _tile(cols, 256 if rows > 64 else 512)
    args.insert(1, g.reshape((N_DEV, rows, cols) if slots else (rows, cols)))
    if slots:
        args.append(own.reshape(rows, cols))

    def body(w_ref, g_ref, m_ref, v_ref, *refs):
        go_ref, d_ref, nm_ref, nv_ref = refs[-4:]
        gv = _slot_sum(g_ref, refs[0]) if slots else g_ref[...]
        go_ref[...] = gv
        m2 = ADAM_B1 * m_ref[...] + (1.0 - ADAM_B1) * gv
        v2 = ADAM_B2 * v_ref[...] + (1.0 - ADAM_B2) * (gv * gv)
        m_hat = m2 / (1.0 - ADAM_B1 ** ADAM_STEP)
        v_hat = v2 / (1.0 - ADAM_B2 ** ADAM_STEP)
        d_ref[...] = -ADAM_LR * (m_hat / (jnp.sqrt(v_hat) + ADAM_EPS) + ADAM_WD * w_ref[...])
        nm_ref[...] = m2
        nv_ref[...] = v2

    spec = pl.BlockSpec((tr, tc), lambda i, j: (i, j))
    g_spec = pl.BlockSpec((N_DEV, tr, tc), lambda i, j: (0, i, j)) if slots else spec
    outs = pl.pallas_call(
        body,
        name=name,
        grid=(rows // tr, cols // tc),
        in_specs=[spec, g_spec, spec, spec] + ([spec] if slots else []),
        out_specs=[spec] * 4,
        out_shape=[jax.ShapeDtypeStruct((rows, cols), F32)] * 4,
        compiler_params=pltpu.CompilerParams(dimension_semantics=("parallel", "parallel")),
    )(*args)
    return tuple(o.reshape(shape) for o in outs)


WEIGHTS = ("g_mix", "w_in", "conv_w", "a_log", "dt_bias", "rel_bias", "w_onorm", "w_branch_a", "w_branch_b", "w_out", "g_ffn",
           "w_gate_up", "w_down", "g_ple", "w_ple_gate", "w_ple_proj", "g_final")


def kernel(x, p, g_mix, w_in, conv_w, a_log, dt_bias, rel_bias, w_onorm, w_branch_a, w_branch_b, w_out, g_ffn, w_gate_up, w_down, g_ple, w_ple_gate, w_ple_proj, g_final, loss_target, m_g_mix, m_w_in, m_conv_w, m_a_log, m_dt_bias, m_rel_bias, m_w_onorm, m_w_branch_a, m_w_branch_b, m_w_out, m_g_ffn, m_w_gate_up, m_w_down, m_g_ple, m_w_ple_gate, m_w_ple_proj, m_g_final, v_g_mix, v_w_in, v_conv_w, v_a_log, v_dt_bias, v_rel_bias, v_w_onorm, v_w_branch_a, v_w_branch_b, v_w_out, v_g_ffn, v_w_gate_up, v_w_down, v_g_ple, v_w_ple_gate, v_w_ple_proj, v_g_final):
    given = dict(g_mix=g_mix, w_in=w_in, conv_w=conv_w, a_log=a_log, dt_bias=dt_bias, rel_bias=rel_bias, w_onorm=w_onorm,
                 w_branch_a=w_branch_a, w_branch_b=w_branch_b, w_out=w_out, g_ffn=g_ffn, w_gate_up=w_gate_up, w_down=w_down,
                 g_ple=g_ple, w_ple_gate=w_ple_gate, w_ple_proj=w_ple_proj, g_final=g_final)
    mom1 = dict(g_mix=m_g_mix, w_in=m_w_in, conv_w=m_conv_w, a_log=m_a_log, dt_bias=m_dt_bias, rel_bias=m_rel_bias,
                w_onorm=m_w_onorm, w_branch_a=m_w_branch_a, w_branch_b=m_w_branch_b, w_out=m_w_out, g_ffn=m_g_ffn,
                w_gate_up=m_w_gate_up, w_down=m_w_down, g_ple=m_g_ple, w_ple_gate=m_w_ple_gate, w_ple_proj=m_w_ple_proj,
                g_final=m_g_final)
    mom2 = dict(g_mix=v_g_mix, w_in=v_w_in, conv_w=v_conv_w, a_log=v_a_log, dt_bias=v_dt_bias, rel_bias=v_rel_bias,
                w_onorm=v_w_onorm, w_branch_a=v_w_branch_a, w_branch_b=v_w_branch_b, w_out=v_w_out, g_ffn=v_g_ffn,
                w_gate_up=v_w_gate_up, w_down=v_w_down, g_ple=v_g_ple, w_ple_gate=v_w_ple_gate, w_ple_proj=v_w_ple_proj,
                g_final=v_g_final)
    mine = 4 * lax.axis_index("x") + 2 * lax.axis_index("y") + lax.axis_index("c")

    my_slot = (jnp.arange(N_DEV) == mine)[:, None, None]
    rest = MATRICES[1:]
    in_flight = {}

    got_in, got_taps = _weights_allgather([_held(w_in[0], 1).astype(BF16), conv_w[0]])
    in_flight["weights"], weights_sent = _exchange_start([_held(given[name][0], axis).astype(BF16) for name, axis in rest], False,
                                                         name="weights_start")
    small = dict(g_mix=g_mix + weights_sent[0, 0], g_ffn=g_ffn, g_ple=g_ple, g_final=g_final, w_onorm=w_onorm, a_log=a_log,
                 dt_bias=dt_bias, rel_bias=rel_bias, conv_w=jnp.transpose(got_taps, (1, 0, 2)).reshape(4, B_CONV))

    def rest_weights(after):
        shards, landed = _exchange_wait(in_flight.pop("weights"), after, False, name="weights_wait")
        return {name: _from_gathered(jnp.where(my_slot, shard[None], slabs)) for (name, _), shard, slabs in zip(rest, shards, landed)}

    def send_grads(gw):
        in_flight["grads"], sent = _exchange_start([_to_owner(gw[name]) for name, _ in rest], True, name="grads_start")
        return sent

    def send_w_in(g_in):
        in_flight["grad_in"], sent = _exchange_start([_to_owner(_unpermute_w_in(g_in))], True, name="grad_in_start")
        return sent

    loss_part, grad_x, gs = _local_step(x, p[0], loss_target, _permute_w_in(_from_gathered(got_in)), small, rest_weights,
                                        send_grads, send_w_in)
    gs["loss"] = loss_part

    updates = {}

    def update_matrices(matrices, own_slabs, landed):
        for (name, axis), own_slab, slots in zip(matrices, own_slabs, landed):
            mine_of = lax.dynamic_index_in_dim(own_slab, mine, axis=0, keepdims=False)
            w_held, m_held, v_held = (_held(a[name][0], axis) for a in (given, mom1, mom2))
            outs = _adamw(w_held, slots, m_held, v_held, name=f"adamw_{name}", own=mine_of)
            updates[name] = tuple(_held(o, axis)[None] for o in outs)

    update_matrices(rest, *_exchange_wait(in_flight["grads"], grad_x, True, name="grads_wait"))
    update_matrices(MATRICES[:1], *_exchange_wait(in_flight["grad_in"], updates[rest[-1][0]][0], True, name="grad_in_wait"))
    small_block, _ = lax.optimization_barrier((_pack_small(gs), updates["w_in"][0]))
    (got_small,) = _grads_exchange([jnp.broadcast_to(small_block, (N_DEV, SMALL_ROWS, D))])
    small_sum = _unpack_small(_sum_slots(got_small, name="sum_small_grads", tr=16))
    loss = small_sum.pop("loss")[0]
    conv_all = small_sum.pop("conv_w").reshape(4, N_DEV, TAPS_PER_SHARD)
    small_sum["conv_w"] = lax.dynamic_index_in_dim(conv_all, mine, axis=1, keepdims=False)
    for name, g in small_sum.items():
        updates[name] = _adamw(given[name], g.reshape(given[name].shape), mom1[name], mom2[name], name=f"adamw_{name}")
    return (loss, grad_x, *[updates[name][k] for k in range(4) for name in WEIGHTS])
```

```python
import jax
import jax.numpy as jnp
from jax import lax
from jax.experimental import pallas as pl
from jax.experimental.pallas import tpu as pltpu

F32 = jnp.float32
BF16 = jnp.bfloat16
DELTA_PREC = lax.Precision.HIGH
MESH = pl.DeviceIdType.MESH

N_DEV = 8
D = 1024
CHUNK = 64
EPS = 1e-6
A_HEADS, A_DIM, A_WIDTH = 8, 64, 512
A_BAND = 9 * CHUNK
A_PAD = 8 * CHUNK
REL_CLIP = 128
B_HEADS, B_DIM = 4, 128
B_CONV = 1536
D_FF = 2816
D_IN = 5640
P_CONV, P_Z, P_GATE, P_BD, P_END = 1536, 3072, 3584, 5632, 5760
LANE = 128

ADAM_LR, ADAM_B1, ADAM_B2, ADAM_EPS, ADAM_WD, ADAM_STEP = 0.001, 0.9, 0.999, 1e-08, 0.01, 10

NT = (((1,), (1,)), ((), ()))
TN = (((0,), (0,)), ((), ()))
NN = (((1,), (0,)), ((), ()))

HBM_SPEC = pl.BlockSpec(memory_space=pltpu.HBM)


def _tile(n, target, align=LANE):
    if n <= target:
        return n
    best = None
    for t in range(align, target + 1, align):
        if n % t == 0:
            best = t
    assert best is not None, (n, target, align)
    return best


def _mm(a, b, *, name, ta=False, tb=False, add=None, out_dtype=F32, tm=1024, tn=640, tk=None):
    assert not (ta and tb)
    if ta:
        k_dim, m_dim = a.shape
    else:
        m_dim, k_dim = a.shape
    n_dim = b.shape[0] if tb else b.shape[1]
    assert b.shape[1 if tb else 0] == k_dim
    tm, tn = _tile(m_dim, tm), _tile(n_dim, tn)
    tk = _tile(k_dim, tk or (4096 if ta else 1024), 8 if ta else LANE)
    nk = k_dim // tk
    dn = TN if ta else NT if tb else NN

    def body(*refs):
        if add is None:
            a_ref, b_ref, o_ref = refs[:3]
            add_ref = None
        else:
            a_ref, b_ref, add_ref, o_ref = refs[:4]
        part = lax.dot_general(a_ref[...].astype(BF16), b_ref[...].astype(BF16), dn, preferred_element_type=F32)

        def finish(r):
            if add_ref is not None:
                r = r + add_ref[...]
            o_ref[...] = r.astype(o_ref.dtype)

        if nk == 1:
            finish(part)
        else:
            acc_ref = refs[-1]
            k = pl.program_id(2)

            @pl.when(k == 0)
            def _():
                acc_ref[...] = part

            @pl.when(k > 0)
            def _():
                acc_ref[...] += part

            @pl.when(k == nk - 1)
            def _():
                finish(acc_ref[...])

    a_spec = pl.BlockSpec((tk, tm), lambda i, j, k: (k, i)) if ta else pl.BlockSpec((tm, tk), lambda i, j, k: (i, k))
    b_spec = pl.BlockSpec((tn, tk), lambda i, j, k: (j, k)) if tb else pl.BlockSpec((tk, tn), lambda i, j, k: (k, j))
    in_specs = [a_spec, b_spec]
    args = [a, b]
    if add is not None:
        in_specs.append(pl.BlockSpec((tm, tn), lambda i, j, k: (i, j)))
        args.append(add)
    return pl.pallas_call(
        body,
        name=name,
        grid=(m_dim // tm, n_dim // tn, nk),
        in_specs=in_specs,
        out_specs=pl.BlockSpec((tm, tn), lambda i, j, k: (i, j)),
        out_shape=jax.ShapeDtypeStruct((m_dim, n_dim), out_dtype),
        scratch_shapes=[pltpu.VMEM((tm, tn), F32)] if nk > 1 else [],
        compiler_params=pltpu.CompilerParams(dimension_semantics=("parallel", "parallel", "arbitrary")),
    )(*args)


def _rowwise(fn, rows, bcs, outs, reds=(), *, name, tr, ncol=1):
    n_rows = rows[0][0].shape[0]
    tr = _tile(n_rows, tr, 8)
    nrow = n_rows // tr
    n_in, n_out = len(rows) + len(bcs), len(outs)

    def body(*refs):
        j, i = pl.program_id(0), pl.program_id(1)
        o_vals, r_vals = fn(*[r[...] for r in refs[:n_in]])
        for ref, val in zip(refs[n_in:n_in + n_out], o_vals):
            ref[...] = val.astype(ref.dtype)
        for ref, val, (_, _, stride) in zip(refs[n_in + n_out:], r_vals, reds):
            first = (i == 0) if stride else jnp.logical_and(i == 0, j == 0)

            @pl.when(first)
            def _():
                ref[...] = val

            @pl.when(jnp.logical_not(first))
            def _():
                ref[...] += val

    def spec(r, w, off, st, row_dep=True):
        if row_dep:
            return pl.BlockSpec((r, w), lambda j, i: (i, off + st * j))
        return pl.BlockSpec((r, w), lambda j, i: (0, off + st * j))

    in_specs = [spec(tr, w, off, st) for (_, w, off, st) in rows]
    in_specs += [spec(a.shape[0], w, off, st, False) for (a, w, off, st) in bcs]
    out_specs = [spec(tr, w, off, st) for (_, _, w, off, st) in outs]
    out_specs += [spec(1, w, 0, st, False) for (_, w, st) in reds]
    out_shape = [jax.ShapeDtypeStruct((n_rows, c), dt) for (c, dt, _, _, _) in outs]
    out_shape += [jax.ShapeDtypeStruct((1, c), F32) for (c, _, _) in reds]
    return pl.pallas_call(
        body,
        name=name,
        grid=(ncol, nrow),
        in_specs=in_specs,
        out_specs=out_specs,
        out_shape=out_shape,
        compiler_params=pltpu.CompilerParams(dimension_semantics=("arbitrary", "arbitrary")),
    )(*[r[0] for r in rows], *[b[0] for b in bcs])


def _full(a):
    return (a, a.shape[1], 0, 0)


def _rms(x, g):
    return x * lax.rsqrt(jnp.mean(x * x, axis=-1, keepdims=True) + EPS) * g


def _silu(x):
    return x * jax.nn.sigmoid(x)


def _softplus(x):
    return jnp.maximum(x, 0.0) + jnp.log(1.0 + jnp.exp(-jnp.abs(x)))


def _rms_fwd(x, g, *, name):
    (h,) = _rowwise(lambda xb, gb: ([_rms(xb, gb)], []), [_full(x)], [_full(g)], [(D, BF16, D, 0, 0)], name=name, tr=512)
    return h


def _rms_bwd(x, g, dh, dres, *, name):
    def fn(xb, dhb, dresb, gb):
        _, vjp = jax.vjp(_rms, xb, gb)
        dx, dg = vjp(dhb)
        return [dx + dresb], [dg]

    return _rowwise(fn, [_full(x), _full(dh), _full(dres)], [_full(g)], [(D, F32, D, 0, 0)], [(D, D, 0)], name=name, tr=256)


def _gate_scalars(raw, al_row, dtb_row):
    lane = lax.broadcasted_iota(jnp.int32, raw.shape, 1)
    beta = jax.nn.sigmoid(raw)
    g = -jnp.exp(al_row) * _softplus(raw + dtb_row)
    return jnp.where(lane < B_HEADS, beta, jnp.where(lane < 2 * B_HEADS, g, 0.0))


def _gated_norm(o, z, w):
    return _rms(o, w) * _silu(z)


def _merge(ga, gb, ta, tb):
    return jax.nn.sigmoid(ga) * ta + jax.nn.sigmoid(gb) * tb


def _swiglu(gu):
    return _silu(gu[:, :D_FF]) * gu[:, D_FF:]


def _head_loss(x2, pg, pp, tgt, g):
    x3 = x2 + jax.nn.sigmoid(pg) * pp
    err = _rms(x3, g) - tgt
    return 0.5 * jnp.sum(jnp.mean(err * err, axis=-1))


CONV_W = 256


def _conv_taps(x, w):
    row = lax.broadcasted_iota(jnp.int32, x.shape, 0)
    shifted = [x] + [jnp.where(row >= s, pltpu.roll(x, s, 0), 0.0) for s in (1, 2, 3)]
    pre = shifted[0] * w[3:4]
    for s in (1, 2, 3):
        pre = pre + shifted[s] * w[3 - s:4 - s]
    return pre, shifted


def _conv_fwd(projp, conv_w, n_batch, seq):
    ncol = B_CONV // CONV_W
    first = P_CONV // CONV_W

    def body(x_ref, w_ref, o_ref):
        pre, _ = _conv_taps(x_ref[...], w_ref[...])
        o_ref[...] = _silu(pre)

    return pl.pallas_call(
        body,
        name="conv_fwd",
        grid=(ncol, n_batch),
        in_specs=[pl.BlockSpec((seq, CONV_W), lambda j, b: (b, first + j)), pl.BlockSpec((4, CONV_W), lambda j, b: (0, j))],
        out_specs=pl.BlockSpec((seq, CONV_W), lambda j, b: (b, j)),
        out_shape=jax.ShapeDtypeStruct((n_batch * seq, B_CONV), F32),
        compiler_params=pltpu.CompilerParams(dimension_semantics=("parallel", "parallel")),
    )(projp, conv_w)


def _conv_bwd(projp, conv_w, dc, n_batch, seq):
    width = dc.shape[1]
    ncol = width // CONV_W
    first_x = P_CONV // CONV_W

    def body(x_ref, w_ref, dc_ref, dx_ref, dw_ref):
        b = pl.program_id(1)
        w = w_ref[...]
        pre, shifted = _conv_taps(x_ref[...], w)
        sg = jax.nn.sigmoid(pre)
        dpre = dc_ref[...] * (sg * (1.0 + pre * (1.0 - sg)))
        row = lax.broadcasted_iota(jnp.int32, dpre.shape, 0)
        dx = dpre * w[3:4]
        for s in (1, 2, 3):
            dx = dx + jnp.where(row < seq - s, pltpu.roll(dpre, seq - s, 0), 0.0) * w[3 - s:4 - s]
        dx_ref[...] = dx.astype(dx_ref.dtype)
        for s in (0, 1, 2, 3):
            part = jnp.sum(dpre * shifted[s], axis=0, keepdims=True)

            @pl.when(b == 0)
            def _():
                dw_ref[3 - s:4 - s, :] = part

            @pl.when(b > 0)
            def _():
                dw_ref[3 - s:4 - s, :] += part

    return pl.pallas_call(
        body,
        name="conv_bwd",
        grid=(ncol, n_batch),
        in_specs=[
            pl.BlockSpec((seq, CONV_W), lambda j, b: (b, first_x + j)),
            pl.BlockSpec((4, CONV_W), lambda j, b: (0, j)),
            pl.BlockSpec((seq, CONV_W), lambda j, b: (b, j)),
        ],
        out_specs=[pl.BlockSpec((seq, CONV_W), lambda j, b: (b, j)), pl.BlockSpec((4, CONV_W), lambda j, b: (0, j))],
        out_shape=[jax.ShapeDtypeStruct((n_batch * seq, width), BF16), jax.ShapeDtypeStruct((4, width), F32)],
        compiler_params=pltpu.CompilerParams(dimension_semantics=("arbitrary", "arbitrary")),
    )(projp, conv_w, dc)


def _attn_chunk(qc, kb, vb, bias2, valid, lane_lo):
    sel = (lane_lo, jnp.logical_not(lane_lo))
    items = [(i, e) for i in range(len(qc)) for e in (0, 1)]
    k16, v16 = [t.astype(BF16) for t in kb], [t.astype(BF16) for t in vb]
    qm = [(jnp.where(sel[e], qc[i], 0.0) * (A_DIM ** -0.5)).astype(BF16) for i, e in items]
    s = [lax.dot_general(qm[n], k16[i], NT, preferred_element_type=F32) + bias2[e] for n, (i, e) in enumerate(items)]
    s = [jnp.where(valid[i], s[n], -1e30) for n, (i, e) in enumerate(items)]
    p = [jnp.exp(t - lax.stop_gradient(jnp.max(t, axis=-1, keepdims=True))) for t in s]
    p = [t * (1.0 / jnp.sum(t, axis=-1, keepdims=True)) for t in p]
    o = [jnp.where(sel[e], jnp.dot(p[n].astype(BF16), v16[i], preferred_element_type=F32), 0.0)
         for n, (i, e) in enumerate(items)]
    return [o[2 * i] + o[2 * i + 1] for i in range(len(qc))]


ATTN_GROUP_FWD, ATTN_GROUP_BWD = 4, 4


def _attn_group(g, group, q_ref, kp_ref, vp_ref):
    col = lax.broadcasted_iota(jnp.int32, (CHUNK, A_BAND), 1)
    lane_lo = lax.broadcasted_iota(jnp.int32, (1, LANE), 1) < A_DIM
    starts = [pl.multiple_of((g * group + i) * CHUNK, CHUNK) for i in range(group)]
    rows = [pl.ds(r0, CHUNK) for r0 in starts]
    bands = [pl.ds(r0, A_BAND) for r0 in starts]
    valid = [col + r0 >= A_PAD for r0 in starts]
    loaded = [q_ref[r, :] for r in rows], [kp_ref[b, :] for b in bands], [vp_ref[b, :] for b in bands]
    return rows, bands, loaded, valid, lane_lo


def _attn_specs(seq):
    def blk(first):
        return pl.BlockSpec((seq, LANE), lambda hp, b: (b, first + hp))

    return blk, pl.BlockSpec((2, CHUNK, A_BAND), lambda hp, b: (hp, 0, 0))


def _attn_fwd(projp, bias, n_batch, seq):
    nc = seq // CHUNK
    blk, bias_spec = _attn_specs(seq)

    def body(q_ref, k_ref, v_ref, b_ref, o_ref, kp_ref, vp_ref):
        kp_ref[0:A_PAD, :] = jnp.zeros((A_PAD, LANE), F32)
        vp_ref[0:A_PAD, :] = jnp.zeros((A_PAD, LANE), F32)
        kp_ref[A_PAD:, :] = k_ref[...]
        vp_ref[A_PAD:, :] = v_ref[...]
        bias2 = b_ref[...]

        def step(g, carry):
            rows, _, (qc, kb, vb), valid, lane_lo = _attn_group(g, ATTN_GROUP_FWD, q_ref, kp_ref, vp_ref)
            out = _attn_chunk(qc, kb, vb, bias2, valid, lane_lo)
            for r, o in zip(rows, out):
                o_ref[r, :] = o.astype(o_ref.dtype)
            return carry

        lax.fori_loop(0, nc // ATTN_GROUP_FWD, step, 0)

    return pl.pallas_call(
        body,
        name="attn_fwd",
        grid=(A_HEADS // 2, n_batch),
        in_specs=[blk(0), blk(4), blk(8), bias_spec],
        out_specs=pl.BlockSpec((seq, LANE), lambda hp, b: (b, hp)),
        out_shape=jax.ShapeDtypeStruct((n_batch * seq, A_WIDTH), BF16),
        scratch_shapes=[pltpu.VMEM((A_PAD + seq, LANE), F32), pltpu.VMEM((A_PAD + seq, LANE), F32)],
        compiler_params=pltpu.CompilerParams(dimension_semantics=("parallel", "parallel")),
    )(projp, projp, projp, bias)


def _attn_bwd(projp, bias, dy, n_batch, seq):
    nc = seq // CHUNK
    blk, bias_spec = _attn_specs(seq)
    out_blk = pl.BlockSpec((seq, LANE), lambda hp, b: (b, hp))

    def body(q_ref, k_ref, v_ref, b_ref, dy_ref, dq_ref, dk_ref, dv_ref, db_ref, kp_ref, vp_ref, dkp_ref, dvp_ref):
        b = pl.program_id(1)
        kp_ref[0:A_PAD, :] = jnp.zeros((A_PAD, LANE), F32)
        vp_ref[0:A_PAD, :] = jnp.zeros((A_PAD, LANE), F32)
        kp_ref[A_PAD:, :] = k_ref[...]
        vp_ref[A_PAD:, :] = v_ref[...]
        dkp_ref[...] = jnp.zeros_like(dkp_ref)
        dvp_ref[...] = jnp.zeros_like(dvp_ref)
        bias2 = b_ref[...]

        @pl.when(b == 0)
        def _():
            db_ref[...] = jnp.zeros_like(db_ref)

        def step(g, carry):
            rows, bands, (qc, kb, vb), valid, lane_lo = _attn_group(g, ATTN_GROUP_BWD, q_ref, kp_ref, vp_ref)
            _, vjp = jax.vjp(lambda q, k, v, bb: _attn_chunk(q, k, v, bb, valid, lane_lo), qc, kb, vb, bias2)
            dq, dk, dv, dbias = vjp([dy_ref[r, :] for r in rows])
            for i, r in enumerate(rows):
                dq_ref[r, :] = dq[i].astype(dq_ref.dtype)
            for i, band in enumerate(bands):
                dkp_ref[band, :] += dk[i]
                dvp_ref[band, :] += dv[i]
            db_ref[...] += dbias
            return carry

        lax.fori_loop(0, nc // ATTN_GROUP_BWD, step, 0)
        dk_ref[...] = dkp_ref[A_PAD:, :].astype(dk_ref.dtype)
        dv_ref[...] = dvp_ref[A_PAD:, :].astype(dv_ref.dtype)

    n_tok = n_batch * seq
    pad = pltpu.VMEM((A_PAD + seq, LANE), F32)
    return pl.pallas_call(
        body,
        name="attn_bwd",
        grid=(A_HEADS // 2, n_batch),
        in_specs=[blk(0), blk(4), blk(8), bias_spec, out_blk],
        out_specs=[out_blk, out_blk, out_blk, bias_spec],
        out_shape=[jax.ShapeDtypeStruct((n_tok, A_WIDTH), BF16)] * 3 + [jax.ShapeDtypeStruct((A_HEADS, CHUNK, A_BAND), F32)],
        scratch_shapes=[pad, pad, pad, pad],
        compiler_params=pltpu.CompilerParams(dimension_semantics=("arbitrary", "arbitrary")),
    )(projp, projp, projp, bias, dy)


def _rel_bias_table(rel_bias):
    span = CHUNK + A_BAND - 1
    near = REL_CLIP + CHUNK
    far = jnp.broadcast_to(rel_bias[:, 2 * REL_CLIP:], (A_HEADS, span - near))
    t = jnp.concatenate([rel_bias[:, 2 * REL_CLIP + 1 - near:], far], axis=1)
    u = jnp.concatenate([t[:, :A_BAND][:, ::-1], t[:, A_BAND:][:, ::-1]], axis=1)
    rolled = jnp.tile(u, (1, CHUNK))[:, :CHUNK * (span - 1)].reshape(A_HEADS, CHUNK, span - 1)
    return rolled[:, :, :A_BAND]


def _dot(a, b, dn=NN):
    return lax.dot_general(a, b, dn, precision=DELTA_PREC, preferred_element_type=F32)


def _dot16(a, b, dn=NN):
    return lax.dot_general(a.astype(BF16), b.astype(BF16), dn, preferred_element_type=F32)


def _each(fn, *lists):
    return [fn(*vals) for vals in zip(*lists)]


@jax.custom_vjp
def _saved_inverse(x, inv):
    return inv


def _saved_inverse_fwd(x, inv):
    return inv, inv


def _saved_inverse_bwd(inv, ct):
    return _dot(_dot(inv, ct, TN), inv, NT), jnp.zeros_like(inv)


_saved_inverse.defvjp(_saved_inverse_fwd, _saved_inverse_bwd)


def _delta_chunk(r_state, cq, ck, cv, beta, g, saved_inv=None):
    ii = lax.broadcasted_iota(jnp.int32, (CHUNK, CHUNK), 0)
    jj = lax.broadcasted_iota(jnp.int32, (CHUNK, CHUNK), 1)
    incl, strict, eye = ii >= jj, ii > jj, ii == jj
    q = _each(lambda t: t * lax.rsqrt(jnp.sum(t * t, axis=-1, keepdims=True) + EPS) * (B_DIM ** -0.5), cq)
    k = _each(lambda t: t * lax.rsqrt(jnp.sum(t * t, axis=-1, keepdims=True) + EPS), ck)
    g_b = _each(lambda t: jnp.broadcast_to(t, (CHUNK, CHUNK)), g)
    g_row = _each(lambda t: jnp.sum(jnp.where(eye, t, 0.0), axis=0, keepdims=True), g_b)
    gc_col = _each(lambda t: jnp.sum(jnp.where(incl, t, 0.0), axis=1, keepdims=True), g_row)
    gc_row = _each(lambda t: jnp.sum(jnp.where(ii <= jj, t, 0.0), axis=0, keepdims=True), g_b)
    decay = _each(lambda c, r: jnp.where(incl, jnp.exp(jnp.where(incl, c - r, 0.0)), 0.0), gc_col, gc_row)
    kk = _each(lambda t: _dot(t, t, NT), k)
    x = _each(lambda b, m, d: jnp.where(strict, -(b * m * d), 0.0), beta, kk, decay)
    if saved_inv is None:
        inv = _each(lambda t: jnp.where(eye, 1.0, 0.0) + t, x)
        pw = x
        for _ in range(5):
            pw = _each(lambda t: _dot(t, t), pw)
            inv = _each(lambda t, s: t + _dot(t, s), inv, pw)
    else:
        inv = _each(_saved_inverse, x, saved_inv)
    egc = _each(jnp.exp, gc_col)
    u = _each(lambda t, b, v: _dot16(t, b * v), inv, beta, cv)
    wk = _each(lambda t, b, e, kh: _dot16(t, (b * e) * kh), inv, beta, egc, k)
    pqk = _each(lambda qh, kh, d: _dot16(qh, kh, NT) * d, q, k, decay)
    g_last = _each(lambda c: c[CHUNK - 1:CHUNK, :], gc_col)
    kdec = _each(lambda kh, gl, c: kh * jnp.exp(gl - c), k, g_last, gc_col)
    w = _each(lambda uh, wkh, r: uh - _dot16(wkh, r), u, wk, r_state)
    o = _each(lambda e, qh, r, ph, wh: e * _dot16(qh, r) + _dot16(ph, wh), egc, q, r_state, pqk, w)
    r_new = _each(lambda gl, r, kd, wh: jnp.exp(gl) * r + _dot16(kd, wh, TN), g_last, r_state, kdec, w)
    return o, r_new, inv


DELTA_BLK = 512


def _delta_blocks(n_batch, seq):
    nblk = seq // DELTA_BLK
    cpb = DELTA_BLK // CHUNK

    def rows(width, order):
        return pl.BlockSpec((DELTA_BLK, width), lambda b, i: (b * nblk + order(i), 0))

    def states(order, side):
        return pl.BlockSpec((cpb, B_HEADS, side, side), lambda b, i: (b * nblk + order(i), 0, 0, 0))

    return nblk, cpb, rows, states


def _head_cols(h):
    return [pl.ds(part * B_HEADS * B_DIM + h * B_DIM, B_DIM) for part in range(3)]


def _load_heads(c_ref, bg_ref, state_ref, rows):
    bg_c = bg_ref[rows, :]
    cols = [_head_cols(h) for h in range(B_HEADS)]
    return ([state_ref[h] for h in range(B_HEADS)], [c_ref[rows, c[0]] for c in cols], [c_ref[rows, c[1]] for c in cols],
            [c_ref[rows, c[2]] for c in cols], [bg_c[:, h:h + 1] for h in range(B_HEADS)],
            [bg_c[:, B_HEADS + h:B_HEADS + h + 1] for h in range(B_HEADS)])


def _delta_fwd(conv, bg, n_batch, seq):
    nblk, cpb, rows_spec, states_spec = _delta_blocks(n_batch, seq)

    def forward(i):
        return i

    def body(c_ref, bg_ref, o_ref, st_ref, inv_ref, r_ref):
        @pl.when(pl.program_id(1) == 0)
        def _():
            r_ref[...] = jnp.zeros_like(r_ref)

        def step(c, carry):
            rows = pl.ds(pl.multiple_of(c * CHUNK, CHUNK), CHUNK)
            args = _load_heads(c_ref, bg_ref, r_ref, rows)
            o, r_new, inv = _delta_chunk(*args)
            for h in range(B_HEADS):
                st_ref[c, h] = args[0][h]
                inv_ref[c, h] = inv[h]
                o_ref[rows, pl.ds(h * B_DIM, B_DIM)] = o[h]
            for h in range(B_HEADS):
                r_ref[h] = r_new[h]
            return carry

        lax.fori_loop(0, cpb, step, 0)

    n_tok = n_batch * seq
    return pl.pallas_call(
        body,
        name="delta_fwd",
        grid=(n_batch, nblk),
        in_specs=[rows_spec(B_CONV, forward), rows_spec(LANE, forward)],
        out_specs=[rows_spec(B_HEADS * B_DIM, forward), states_spec(forward, B_DIM), states_spec(forward, CHUNK)],
        out_shape=[jax.ShapeDtypeStruct((n_tok, B_HEADS * B_DIM), F32),
                   jax.ShapeDtypeStruct((n_tok // CHUNK, B_HEADS, B_DIM, B_DIM), F32),
                   jax.ShapeDtypeStruct((n_tok // CHUNK, B_HEADS, CHUNK, CHUNK), F32)],
        scratch_shapes=[pltpu.VMEM((B_HEADS, B_DIM, B_DIM), F32)],
        compiler_params=pltpu.CompilerParams(dimension_semantics=("arbitrary", "arbitrary")),
    )(conv, bg)


def _delta_bwd(conv, bg, states, inverses, do, n_batch, seq):
    nblk, cpb, rows_spec, states_spec = _delta_blocks(n_batch, seq)

    def backward(i):
        return nblk - 1 - i

    def body(c_ref, bg_ref, st_ref, inv_ref, do_ref, dc_ref, dbg_ref, dr_ref):
        @pl.when(pl.program_id(1) == 0)
        def _():
            dr_ref[...] = jnp.zeros_like(dr_ref)

        def step(n, carry):
            c = cpb - 1 - n
            rows = pl.ds(pl.multiple_of(c * CHUNK, CHUNK), CHUNK)
            saved = [inv_ref[c, h] for h in range(B_HEADS)]
            _, vjp = jax.vjp(lambda *args: _delta_chunk(*args, saved_inv=saved)[:2],
                             *_load_heads(c_ref, bg_ref, st_ref.at[c], rows))
            do = [do_ref[rows, pl.ds(h * B_DIM, B_DIM)] for h in range(B_HEADS)]
            dr, dq, dk, dv, dbeta, dg = vjp((do, [dr_ref[h] for h in range(B_HEADS)]))
            lane = lax.broadcasted_iota(jnp.int32, (CHUNK, LANE), 1)
            dbg = jnp.zeros((CHUNK, LANE), F32)
            for h in range(B_HEADS):
                cq, ck, cv = _head_cols(h)
                dr_ref[h] = dr[h]
                dc_ref[rows, cq] = dq[h]
                dc_ref[rows, ck] = dk[h]
                dc_ref[rows, cv] = dv[h]
                dbg = dbg + jnp.where(lane == h, dbeta[h], 0.0) + jnp.where(lane == h + B_HEADS, dg[h], 0.0)
            dbg_ref[rows, :] = dbg
            return carry

        lax.fori_loop(0, cpb, step, 0)

    n_tok = n_batch * seq
    return pl.pallas_call(
        body,
        name="delta_bwd",
        grid=(n_batch, nblk),
        in_specs=[rows_spec(B_CONV, backward), rows_spec(LANE, backward), states_spec(backward, B_DIM),
                  states_spec(backward, CHUNK), rows_spec(B_HEADS * B_DIM, backward)],
        out_specs=[rows_spec(B_CONV, backward), rows_spec(LANE, backward)],
        out_shape=[jax.ShapeDtypeStruct((n_tok, B_CONV), F32), jax.ShapeDtypeStruct((n_tok, LANE), F32)],
        scratch_shapes=[pltpu.VMEM((B_HEADS, B_DIM, B_DIM), F32)],
        compiler_params=pltpu.CompilerParams(dimension_semantics=("arbitrary", "arbitrary")),
    )(conv, bg, states, inverses, do)


def _lane_row(vec4, first):
    return jnp.concatenate([jnp.zeros((1, first), F32), vec4.reshape(1, B_HEADS).astype(F32),
                            jnp.zeros((1, LANE - first - B_HEADS), F32)], axis=1)


def _local_step(x3d, p3d, tgt3d, w_in, small, rest_weights, send_grads, send_w_in):
    n_batch, seq, _ = x3d.shape
    n_tok = n_batch * seq
    x, p, tgt = x3d.reshape(n_tok, D), p3d.reshape(n_tok, -1), tgt3d.reshape(n_tok, D)
    g_mix, g_ffn, g_ple, g_final = (small[k].reshape(1, D) for k in ("g_mix", "g_ffn", "g_ple", "g_final"))
    w_onorm = small["w_onorm"].reshape(1, B_DIM)
    al_row = _lane_row(small["a_log"], B_HEADS)
    dtb_row = _lane_row(small["dt_bias"], B_HEADS)
    rel_bias = small["rel_bias"].reshape(A_HEADS, -1)
    bias = _rel_bias_table(rel_bias)
    conv_w = small["conv_w"].reshape(4, B_CONV)
    bd_blk = P_BD // LANE

    h1 = _rms_fwd(x, g_mix, name="rms_mix")
    projp = _mm(h1, w_in, tb=True, name="mm_proj", tn=640)
    y_a = _attn_fwd(projp, bias, n_batch, seq)
    conv = _conv_fwd(projp, conv_w, n_batch, seq)
    (bg,) = _rowwise(lambda raw, al, dtb: ([_gate_scalars(raw, al, dtb)], []), [(projp, LANE, bd_blk, 0)],
                     [_full(al_row), _full(dtb_row)], [(LANE, F32, LANE, 0, 0)], name="gate_scalars", tr=1024)
    o_b, states, inverses = _delta_fwd(conv, bg, n_batch, seq)
    (y_b,) = _rowwise(lambda o, z, wn: ([_gated_norm(o, z, wn)], []), [(o_b, LANE, 0, 1), (projp, LANE, P_Z // LANE, 1)],
                      [_full(w_onorm)], [(B_HEADS * B_DIM, BF16, LANE, 0, 1)], name="gated_norm", tr=1024, ncol=B_HEADS)
    w = rest_weights(y_b)
    t_a = _mm(y_a, w["w_branch_a"], tb=True, name="mm_branch_a", tn=1024)
    t_b = _mm(y_b, w["w_branch_b"], tb=True, name="mm_branch_b", tn=1024)
    half = D // 2
    gate_rows = [(projp, half, P_GATE // half, 1), (projp, half, P_GATE // half + 2, 1), (t_a, half, 0, 1), (t_b, half, 0, 1)]
    (merged,) = _rowwise(lambda ga, gb, ta, tb: ([_merge(ga, gb, ta, tb)], []), gate_rows, [], [(D, BF16, half, 0, 1)],
                         name="merge", tr=512, ncol=2)
    x1 = _mm(merged, w["w_out"], add=x, name="mm_out", tn=1024)
    h2 = _rms_fwd(x1, g_ffn, name="rms_ffn")
    gu = _mm(h2, w["w_gate_up"], tb=True, out_dtype=BF16, name="mm_gate_up", tn=512)
    (act,) = _rowwise(lambda gub: ([_swiglu(gub.astype(F32))], []), [_full(gu)], [], [(D_FF, BF16, D_FF, 0, 0)], name="swiglu",
                      tr=512)
    x2 = _mm(act, w["w_down"], add=x1, name="mm_down", tn=1024, tk=1408)
    h3 = _rms_fwd(x2, g_ple, name="rms_ple")
    pg = _mm(h3, w["w_ple_gate"], name="mm_ple_gate", tn=1024)
    pp = _mm(p, w["w_ple_proj"], tb=True, name="mm_ple_proj", tn=1024)

    def head_fn(x2b, pgb, ppb, tb, gb):
        loss, (dx2, dpg, dpp, dg) = jax.value_and_grad(_head_loss, argnums=(0, 1, 2, 4))(x2b, pgb, ppb, tb, gb)
        return [dx2, dpg, dpp], [dg, jnp.full((1, LANE), loss, F32)]

    dx3, dpg, dpp, dg_final, loss_row = _rowwise(
        head_fn, [_full(x2), _full(pg), _full(pp), _full(tgt)], [_full(g_final)],
        [(D, F32, D, 0, 0), (D, BF16, D, 0, 0), (D, BF16, D, 0, 0)], [(D, D, 0), (LANE, LANE, 0)], name="loss_head", tr=256)
    gw = {}
    gw["w_ple_proj"] = _mm(dpp, p, ta=True, out_dtype=BF16, name="mm_d_ple_proj", tn=256)
    gw["w_ple_gate"] = _mm(h3, dpg, ta=True, out_dtype=BF16, name="mm_d_ple_gate", tn=512)
    dh3 = _mm(dpg, w["w_ple_gate"], tb=True, name="mm_dh3", tn=1024)
    dx2, dg_ple = _rms_bwd(x2, g_ple, dh3, dx3, name="rms_ple_bwd")
    gw["w_down"] = _mm(act, dx2, ta=True, out_dtype=BF16, name="mm_d_down", tm=1408, tn=512, tk=2048)
    dact = _mm(dx2, w["w_down"], tb=True, out_dtype=BF16, name="mm_dact", tn=1408)

    def swiglu_bwd(gub, dab):
        _, vjp = jax.vjp(_swiglu, gub.astype(F32))
        return [vjp(dab.astype(F32))[0]], []

    (dgu,) = _rowwise(swiglu_bwd, [_full(gu), _full(dact)], [], [(2 * D_FF, BF16, 2 * D_FF, 0, 0)], name="swiglu_bwd", tr=256)
    gw["w_gate_up"] = _mm(dgu, h2, ta=True, out_dtype=BF16, name="mm_d_gate_up", tm=512, tn=1024)
    dh2 = _mm(dgu, w["w_gate_up"], name="mm_dh2", tn=1024, tk=1408)
    dx1, dg_ffn = _rms_bwd(x1, g_ffn, dh2, dx2, name="rms_ffn_bwd")
    gw["w_out"] = _mm(merged, dx1, ta=True, out_dtype=BF16, name="mm_d_out", tn=512)
    dmerged = _mm(dx1, w["w_out"], tb=True, name="mm_dmerged", tn=1024)

    def merge_bwd(ga, gb, ta, tb, dm):
        _, vjp = jax.vjp(_merge, ga, gb, ta, tb)
        return list(vjp(dm)), []

    dga, dgb, dta, dtb = _rowwise(merge_bwd, gate_rows + [(dmerged, half, 0, 1)], [], [(D, BF16, half, 0, 1)] * 4,
                                  name="merge_bwd", tr=512, ncol=2)
    gw["w_branch_a"] = _mm(dta, y_a, ta=True, out_dtype=BF16, name="mm_d_branch_a", tn=512)
    gw["w_branch_b"] = _mm(dtb, y_b, ta=True, out_dtype=BF16, name="mm_d_branch_b", tn=512)
    dya = _mm(dta, w["w_branch_a"], name="mm_dya", tn=512)
    dyb = _mm(dtb, w["w_branch_b"], name="mm_dyb", tn=512)

    w_onorm = w_onorm + send_grads(gw)[0, 0]

    def gated_norm_bwd(o, z, dy, wn):
        _, vjp = jax.vjp(_gated_norm, o, z, wn)
        do, dz, dwn = vjp(dy)
        return [do, dz], [dwn]

    do_b, dz, dw_onorm = _rowwise(
        gated_norm_bwd, [(o_b, LANE, 0, 1), (projp, LANE, P_Z // LANE, 1), (dyb, LANE, 0, 1)], [_full(w_onorm)],
        [(B_HEADS * B_DIM, F32, LANE, 0, 1), (B_HEADS * B_DIM, BF16, LANE, 0, 1)], [(B_DIM, B_DIM, 0)],
        name="gated_norm_bwd", tr=1024, ncol=B_HEADS)
    dconv_out, dbg = _delta_bwd(conv, bg, states, inverses, do_b, n_batch, seq)

    def gate_scalars_bwd(raw, dbgb, al, dtb):
        _, vjp = jax.vjp(_gate_scalars, raw, al, dtb)
        draw, dal, ddtb = vjp(dbgb)
        return [draw], [dal, ddtb]

    dbd, dal_row, ddtb_row = _rowwise(gate_scalars_bwd, [(projp, LANE, bd_blk, 0), _full(dbg)], [_full(al_row), _full(dtb_row)],
                                      [(LANE, BF16, LANE, 0, 0)], [(LANE, LANE, 0), (LANE, LANE, 0)], name="gate_scalars_bwd",
                                      tr=1024)
    dconv, dconv_w = _conv_bwd(projp, conv_w, dconv_out, n_batch, seq)
    dq_a, dk_a, dv_a, dbias = _attn_bwd(projp, bias, dya, n_batch, seq)
    dprojp = jnp.concatenate([dq_a, dk_a, dv_a, dconv, dz, dga, dgb, dbd], axis=1)
    sent = send_w_in(_mm(dprojp, h1, ta=True, out_dtype=BF16, name="mm_d_in", tm=640, tn=1024))
    sent, dprojp = lax.optimization_barrier((sent, dprojp))
    dh1 = _mm(dprojp, w_in, name="mm_dh1", tn=1024, tk=1152)
    grad_x, dg_mix = _rms_bwd(x, g_mix + sent[0, 0], dh1, dx1, name="rms_mix_bwd")

    _, bias_vjp = jax.vjp(_rel_bias_table, rel_bias)
    gs = {
        "g_mix": dg_mix, "g_ffn": dg_ffn, "g_ple": dg_ple, "g_final": dg_final, "w_onorm": dw_onorm,
        "conv_w": dconv_w, "rel_bias": bias_vjp(dbias)[0],
        "a_log": dal_row[0, B_HEADS:2 * B_HEADS], "dt_bias": ddtb_row[0, B_HEADS:2 * B_HEADS],
    }
    return loss_row[:, :1], grad_x.reshape(n_batch, seq, D), gs


MATRICES = (("w_in", 1), ("w_gate_up", 1), ("w_branch_a", 1), ("w_branch_b", 1), ("w_out", 0), ("w_down", 0),
            ("w_ple_gate", 0), ("w_ple_proj", 1))
TAPS_PER_SHARD = B_CONV // N_DEV


def _held(shard, axis):
    return shard if axis == 0 else shard.T


def _from_gathered(slabs):
    return slabs.reshape(-1, slabs.shape[-1])


def _to_owner(held):
    return held.reshape(N_DEV, held.shape[0] // N_DEV, held.shape[1])


def _permute_w_in(held):
    n_gate = P_BD - P_GATE
    row = lax.broadcasted_iota(jnp.int32, (P_END, 1), 0)
    same = jnp.pad(held, ((0, P_END - D_IN), (0, 0)))
    up = jnp.pad(held[8:], ((0, P_END - D_IN + 8), (0, 0)))
    down = jnp.pad(held[:P_GATE + 8], ((n_gate, P_END - P_BD - 8), (0, 0)))
    zero = jnp.zeros((), held.dtype)
    return jnp.where(row < P_GATE, same, jnp.where(row < P_BD, up, jnp.where(row < P_BD + 8, down, zero)))


def _unpermute_w_in(gp):
    n_gate = P_BD - P_GATE
    row = lax.broadcasted_iota(jnp.int32, (D_IN, 1), 0)
    same = gp[:D_IN]
    up = jnp.pad(gp[n_gate:], ((0, D_IN - (P_END - n_gate)), (0, 0)))
    down = jnp.pad(gp[:P_BD], ((8, 0), (0, 0)))
    return jnp.where(row < P_GATE, same, jnp.where(row < P_GATE + 8, up, down))


SMALL_ROWS = 16
SMALL_LAYOUT = (("g_mix", 0, D), ("g_ffn", 1, D), ("g_ple", 2, D), ("g_final", 3, D), ("conv_w", 4, 4 * B_CONV),
                ("rel_bias", 10, A_HEADS * (2 * REL_CLIP + 1)), ("w_onorm", 13, B_DIM), ("a_log", 14, B_HEADS),
                ("dt_bias", 14, B_HEADS), ("loss", 15, 1))


def _pack_small(gs):
    rows = {}
    for name, row, n in SMALL_LAYOUT:
        rows.setdefault(row, []).append(gs[name].reshape(-1).astype(F32))
    parts = []
    for row in sorted(rows):
        flat = jnp.concatenate(rows[row])
        parts.append(jnp.concatenate([flat, jnp.zeros((-flat.shape[0] % D,), F32)]))
    flat = jnp.concatenate(parts)
    assert flat.shape[0] == SMALL_ROWS * D, flat.shape
    return flat.reshape(SMALL_ROWS, D)


def _unpack_small(blk):
    flat, out, used = blk.reshape(-1), {}, {}
    for name, row, n in SMALL_LAYOUT:
        start = row * D + used.get(row, 0)
        out[name] = flat[start:start + n]
        used[row] = used.get(row, 0) + n
    return out


def _position():
    return lax.axis_index("x"), lax.axis_index("y"), lax.axis_index("c")


PEERS = N_DEV - 1


def _comm_call(body, arrays, out_shapes, *, name):
    n = len(arrays)
    return pl.pallas_call(
        body,
        name=name,
        out_shape=out_shapes,
        in_specs=[HBM_SPEC] * n,
        out_specs=[HBM_SPEC] * n,
        scratch_shapes=[pltpu.SemaphoreType.DMA((PEERS * n,)), pltpu.SemaphoreType.DMA((PEERS * n,)),
                        pltpu.SemaphoreType.DMA((n,))],
    )(*arrays)


def _weights_allgather(shards):
    n = len(shards)

    def body(*refs):
        ins, outs = refs[:n], refs[n:2 * n]
        send_sems, recv_sems, local_sems = refs[2 * n:]
        x, y, c = _position()
        me, sibling = (x, y, c), (x, y, 1 - c)
        chips = [(1 - x, y), (x, 1 - y), (1 - x, 1 - y)]

        def slab(a, px, py, pc):
            return outs[a].at[4 * px + 2 * py + pc]

        def copy(a, k, block, to, src=None):
            return pltpu.make_async_remote_copy(src_ref=slab(a, *block) if src is None else src, dst_ref=slab(a, *block),
                                                send_sem=send_sems.at[PEERS * a + k], recv_sem=recv_sems.at[PEERS * a + k],
                                                device_id=to, device_id_type=MESH)

        local = [pltpu.make_async_copy(ins[a], slab(a, *me), local_sems.at[a]) for a in range(n)]
        sent = [copy(a, 1 + j, me, (*chip, c), src=ins[a]) for a in range(n) for j, chip in enumerate(chips)]
        sent += [copy(a, 0, me, sibling, src=ins[a]) for a in range(n)]
        for cp in sent + local:
            cp.start()
        for a in range(n):
            for j, chip in enumerate(chips):
                copy(a, 1 + j, (*chip, c), me).wait_recv()
                passed = copy(a, 4 + j, (*chip, c), sibling)
                passed.start()
                sent.append(passed)
        for a in range(n):
            copy(a, 0, sibling, me).wait_recv()
            for j, chip in enumerate(chips):
                copy(a, 4 + j, (*chip, 1 - c), me).wait_recv()
        for cp in sent:
            cp.wait_send()
        for cp in local:
            cp.wait()

    return _comm_call(body, shards, [jax.ShapeDtypeStruct((N_DEV,) + s.shape, s.dtype) for s in shards],
                      name="weights_allgather")


def _grads_exchange(by_owner):
    n = len(by_owner)

    def body(*refs):
        ins, outs = refs[:n], refs[n:2 * n]
        send_sems, recv_sems, local_sems = refs[2 * n:]
        x, y, c = _position()
        mine = 4 * x + 2 * y + c
        local = [pltpu.make_async_copy(ins[a].at[mine], outs[a].at[mine], local_sems.at[a]) for a in range(n)]
        for cp in local:
            cp.start()
        flips = [(dx, dy, dc) for dx in (0, 1) for dy in (0, 1) for dc in (0, 1) if dx + dy + dc]
        pending = []
        for k, (dx, dy, dc) in enumerate(flips):
            px, py, pc = (1 - x if dx else x), (1 - y if dy else y), (1 - c if dc else c)
            peer = 4 * px + 2 * py + pc
            for a in range(n):
                def remote(slot):
                    return pltpu.make_async_remote_copy(src_ref=ins[a].at[peer], dst_ref=outs[a].at[slot],
                                                        send_sem=send_sems.at[PEERS * a + k], recv_sem=recv_sems.at[PEERS * a + k],
                                                        device_id=(px, py, pc), device_id_type=MESH)

                sent = remote(mine)
                sent.start()
                pending.append((sent, remote(peer)))
        for sent, landed in pending:
            landed.wait_recv()
            sent.wait_send()
        for cp in local:
            cp.wait()

    return _comm_call(body, by_owner, [jax.ShapeDtypeStruct(g.shape, g.dtype) for g in by_owner], name="grads_exchange")


SEM_SPEC = pl.BlockSpec(memory_space=pltpu.SEMAPHORE)
DATAFLOW = pltpu.SideEffectType.DATAFLOW_SIDE_EFFECTING


def _peer_copies(srcs, lands, send_sems, recv_sems, by_owner, arrival):
    x, y, c = _position()
    mine = 4 * x + 2 * y + c
    copies = []
    for k, (dx, dy, dc) in enumerate([(dx, dy, dc) for dx in (0, 1) for dy in (0, 1) for dc in (0, 1) if dx + dy + dc]):
        px, py, pc = (1 - x if dx else x), (1 - y if dy else y), (1 - c if dc else c)
        peer = 4 * px + 2 * py + pc
        for a, (src, land) in enumerate(zip(srcs, lands)):
            copies.append(pltpu.make_async_remote_copy(
                src_ref=src.at[peer] if by_owner else src, dst_ref=land.at[peer if arrival else mine],
                send_sem=send_sems.at[PEERS * a + k], recv_sem=recv_sems.at[PEERS * a + k],
                device_id=(px, py, pc), device_id_type=MESH))
    return copies


def _exchange_start(sources, by_owner, *, name):
    n = len(sources)
    lands = [lax.empty((N_DEV,) + (s.shape[1:] if by_owner else s.shape), s.dtype) for s in sources]

    def body(*refs):
        send_sems, recv_sems, token = refs[2 * n], refs[2 * n + 1], refs[-1]
        for copy in _peer_copies(refs[:n], refs[n:2 * n], send_sems, recv_sems, by_owner, arrival=False):
            copy.start()
        token[...] = jnp.zeros_like(token)

    sems = pltpu.SemaphoreType.DMA((PEERS * n,))
    outs = pl.pallas_call(
        body,
        name=name,
        out_shape=(sems, sems, *[pltpu.HBM(a.shape, a.dtype) for a in sources + lands], jax.ShapeDtypeStruct((8, LANE), F32)),
        in_specs=[HBM_SPEC] * (2 * n),
        out_specs=(SEM_SPEC, SEM_SPEC, *[HBM_SPEC] * (2 * n), pl.BlockSpec(memory_space=pltpu.VMEM)),
        input_output_aliases={i: 2 + i for i in range(2 * n)},
        compiler_params=pltpu.CompilerParams(has_side_effects=DATAFLOW),
    )(*[pltpu.with_memory_space_constraint(a, pltpu.HBM) for a in sources + lands])
    return outs[:-1], outs[-1]


def _exchange_wait(started, after, by_owner, *, name):
    send_sems, recv_sems, *arrays = started
    n = len(arrays) // 2

    def body(*refs):
        for copy in _peer_copies(refs[:n], refs[n:2 * n], refs[2 * n], refs[2 * n + 1], by_owner, arrival=True):
            copy.wait_send()
            copy.wait_recv()

    outs = pl.pallas_call(
        body,
        name=name,
        out_shape=[pltpu.HBM(a.shape, a.dtype) for a in arrays],
        in_specs=[HBM_SPEC] * (2 * n) + [SEM_SPEC, SEM_SPEC, pl.BlockSpec(memory_space=pl.ANY)],
        out_specs=[HBM_SPEC] * (2 * n),
        input_output_aliases={i: i for i in range(2 * n)},
        compiler_params=pltpu.CompilerParams(has_side_effects=DATAFLOW),
    )(*arrays, send_sems, recv_sems, after)
    return outs[:n], outs[n:]


def _slot_sum(g_ref, own_ref):
    if own_ref is not None:
        x, y, c = _position()
        mine = 4 * x + 2 * y + c
    acc = None
    for j in range(N_DEV):
        part = g_ref[j] if own_ref is None else jnp.where(mine == j, own_ref[...], g_ref[j])
        acc = part.astype(F32) if acc is None else acc + part.astype(F32)
    return acc


def _sum_slots(got, *, name, tr):
    _, rows, cols = got.shape
    tr = _tile(rows, tr, 16)

    def body(g_ref, o_ref):
        o_ref[...] = _slot_sum(g_ref, None)

    return pl.pallas_call(
        body,
        name=name,
        grid=(rows // tr,),
        in_specs=[pl.BlockSpec((N_DEV, tr, cols), lambda i: (0, i, 0))],
        out_specs=pl.BlockSpec((tr, cols), lambda i: (i, 0)),
        out_shape=jax.ShapeDtypeStruct((rows, cols), F32),
        compiler_params=pltpu.CompilerParams(dimension_semantics=("parallel",)),
    )(got)


def _adamw(wt, g, m, v, *, name, own=None):
    slots = own is not None
    shape = wt.shape
    two_d = (-1, shape[-1]) if wt.ndim > 1 else (1, -1)
    args = [a.reshape(two_d) for a in (wt, m, v)]
    rows, cols = args[0].shape
    if rows % 16 == 0:
        tr, tc = _tile(rows, 256, 16), cols
    else:
        tr, tc = rows, _tile(cols, 256 if rows > 64 else 512)
    args.insert(1, g.reshape((N_DEV, rows, cols) if slots else (rows, cols)))
    if slots:
        args.append(own.reshape(rows, cols))

    def body(w_ref, g_ref, m_ref, v_ref, *refs):
        go_ref, d_ref, nm_ref, nv_ref = refs[-4:]
        gv = _slot_sum(g_ref, refs[0]) if slots else g_ref[...]
        go_ref[...] = gv
        m2 = ADAM_B1 * m_ref[...] + (1.0 - ADAM_B1) * gv
        v2 = ADAM_B2 * v_ref[...] + (1.0 - ADAM_B2) * (gv * gv)
        m_hat = m2 / (1.0 - ADAM_B1 ** ADAM_STEP)
        v_hat = v2 / (1.0 - ADAM_B2 ** ADAM_STEP)
        d_ref[...] = -ADAM_LR * (m_hat / (jnp.sqrt(v_hat) + ADAM_EPS) + ADAM_WD * w_ref[...])
        nm_ref[...] = m2
        nv_ref[...] = v2

    spec = pl.BlockSpec((tr, tc), lambda i, j: (i, j))
    g_spec = pl.BlockSpec((N_DEV, tr, tc), lambda i, j: (0, i, j)) if slots else spec
    outs = pl.pallas_call(
        body,
        name=name,
        grid=(rows // tr, cols // tc),
        in_specs=[spec, g_spec, spec, spec] + ([spec] if slots else []),
        out_specs=[spec] * 4,
        out_shape=[jax.ShapeDtypeStruct((rows, cols), F32)] * 4,
        compiler_params=pltpu.CompilerParams(dimension_semantics=("parallel", "parallel")),
    )(*args)
    return tuple(o.reshape(shape) for o in outs)


WEIGHTS = ("g_mix", "w_in", "conv_w", "a_log", "dt_bias", "rel_bias", "w_onorm", "w_branch_a", "w_branch_b", "w_out", "g_ffn",
           "w_gate_up", "w_down", "g_ple", "w_ple_gate", "w_ple_proj", "g_final")


def kernel(x, p, g_mix, w_in, conv_w, a_log, dt_bias, rel_bias, w_onorm, w_branch_a, w_branch_b, w_out, g_ffn, w_gate_up, w_down, g_ple, w_ple_gate, w_ple_proj, g_final, loss_target, m_g_mix, m_w_in, m_conv_w, m_a_log, m_dt_bias, m_rel_bias, m_w_onorm, m_w_branch_a, m_w_branch_b, m_w_out, m_g_ffn, m_w_gate_up, m_w_down, m_g_ple, m_w_ple_gate, m_w_ple_proj, m_g_final, v_g_mix, v_w_in, v_conv_w, v_a_log, v_dt_bias, v_rel_bias, v_w_onorm, v_w_branch_a, v_w_branch_b, v_w_out, v_g_ffn, v_w_gate_up, v_w_down, v_g_ple, v_w_ple_gate, v_w_ple_proj, v_g_final):
    given = dict(g_mix=g_mix, w_in=w_in, conv_w=conv_w, a_log=a_log, dt_bias=dt_bias, rel_bias=rel_bias, w_onorm=w_onorm,
                 w_branch_a=w_branch_a, w_branch_b=w_branch_b, w_out=w_out, g_ffn=g_ffn, w_gate_up=w_gate_up, w_down=w_down,
                 g_ple=g_ple, w_ple_gate=w_ple_gate, w_ple_proj=w_ple_proj, g_final=g_final)
    mom1 = dict(g_mix=m_g_mix, w_in=m_w_in, conv_w=m_conv_w, a_log=m_a_log, dt_bias=m_dt_bias, rel_bias=m_rel_bias,
                w_onorm=m_w_onorm, w_branch_a=m_w_branch_a, w_branch_b=m_w_branch_b, w_out=m_w_out, g_ffn=m_g_ffn,
                w_gate_up=m_w_gate_up, w_down=m_w_down, g_ple=m_g_ple, w_ple_gate=m_w_ple_gate, w_ple_proj=m_w_ple_proj,
                g_final=m_g_final)
    mom2 = dict(g_mix=v_g_mix, w_in=v_w_in, conv_w=v_conv_w, a_log=v_a_log, dt_bias=v_dt_bias, rel_bias=v_rel_bias,
                w_onorm=v_w_onorm, w_branch_a=v_w_branch_a, w_branch_b=v_w_branch_b, w_out=v_w_out, g_ffn=v_g_ffn,
                w_gate_up=v_w_gate_up, w_down=v_w_down, g_ple=v_g_ple, w_ple_gate=v_w_ple_gate, w_ple_proj=v_w_ple_proj,
                g_final=v_g_final)
    mine = 4 * lax.axis_index("x") + 2 * lax.axis_index("y") + lax.axis_index("c")

    my_slot = (jnp.arange(N_DEV) == mine)[:, None, None]
    rest = MATRICES[1:]
    in_flight = {}

    got_in, got_taps = _weights_allgather([_held(w_in[0], 1).astype(BF16), conv_w[0]])
    in_flight["weights"], weights_sent = _exchange_start([_held(given[name][0], axis).astype(BF16) for name, axis in rest], False,
                                                         name="weights_start")
    small = dict(g_mix=g_mix + weights_sent[0, 0], g_ffn=g_ffn, g_ple=g_ple, g_final=g_final, w_onorm=w_onorm, a_log=a_log,
                 dt_bias=dt_bias, rel_bias=rel_bias, conv_w=jnp.transpose(got_taps, (1, 0, 2)).reshape(4, B_CONV))

    def rest_weights(after):
        shards, landed = _exchange_wait(in_flight.pop("weights"), after, False, name="weights_wait")
        return {name: _from_gathered(jnp.where(my_slot, shard[None], slabs)) for (name, _), shard, slabs in zip(rest, shards, landed)}

    def send_grads(gw):
        in_flight["grads"], sent = _exchange_start([_to_owner(gw[name]) for name, _ in rest], True, name="grads_start")
        return sent

    def send_w_in(g_in):
        in_flight["grad_in"], sent = _exchange_start([_to_owner(_unpermute_w_in(g_in))], True, name="grad_in_start")
        return sent

    loss_part, grad_x, gs = _local_step(x, p[0], loss_target, _permute_w_in(_from_gathered(got_in)), small, rest_weights,
                                        send_grads, send_w_in)
    gs["loss"] = loss_part

    updates = {}

    def update_matrices(matrices, own_slabs, landed):
        for (name, axis), own_slab, slots in zip(matrices, own_slabs, landed):
            mine_of = lax.dynamic_index_in_dim(own_slab, mine, axis=0, keepdims=False)
            w_held, m_held, v_held = (_held(a[name][0], axis) for a in (given, mom1, mom2))
            outs = _adamw(w_held, slots, m_held, v_held, name=f"adamw_{name}", own=mine_of)
            updates[name] = tuple(_held(o, axis)[None] for o in outs)

    update_matrices(rest, *_exchange_wait(in_flight["grads"], grad_x, True, name="grads_wait"))
    update_matrices(MATRICES[:1], *_exchange_wait(in_flight["grad_in"], updates[rest[-1][0]][0], True, name="grad_in_wait"))
    small_block, _ = lax.optimization_barrier((_pack_small(gs), updates["w_in"][0]))
    (got_small,) = _grads_exchange([jnp.broadcast_to(small_block, (N_DEV, SMALL_ROWS, D))])
    small_sum = _unpack_small(_sum_slots(got_small, name="sum_small_grads", tr=16))
    loss = small_sum.pop("loss")[0]
    conv_all = small_sum.pop("conv_w").reshape(4, N_DEV, TAPS_PER_SHARD)
    small_sum["conv_w"] = lax.dynamic_index_in_dim(conv_all, mine, axis=1, keepdims=False)
    for name, g in small_sum.items():
        updates[name] = _adamw(given[name], g.reshape(given[name].shape), mom1[name], mom2[name], name=f"adamw_{name}")
    return (loss, grad_x, *[updates[name][k] for k in range(4) for name in WEIGHTS])
```

```python
import jax
import jax.numpy as jnp
from jax import lax
from jax.experimental import pallas as pl
from jax.experimental.pallas import tpu as pltpu

F32 = jnp.float32
BF16 = jnp.bfloat16
DELTA_PREC = lax.Precision.HIGH
MESH = pl.DeviceIdType.MESH

N_DEV = 8
D = 1024
CHUNK = 64
EPS = 1e-6
A_HEADS, A_DIM, A_WIDTH = 8, 64, 512
A_BAND = 9 * CHUNK
A_PAD = 8 * CHUNK
REL_CLIP = 128
B_HEADS, B_DIM = 4, 128
B_CONV = 1536
D_FF = 2816
D_IN = 5640
P_CONV, P_Z, P_GATE, P_BD, P_END = 1536, 3072, 3584, 5632, 5760
LANE = 128

ADAM_LR, ADAM_B1, ADAM_B2, ADAM_EPS, ADAM_WD, ADAM_STEP = 0.001, 0.9, 0.999, 1e-08, 0.01, 10

NT = (((1,), (1,)), ((), ()))
TN = (((0,), (0,)), ((), ()))
NN = (((1,), (0,)), ((), ()))

HBM_SPEC = pl.BlockSpec(memory_space=pltpu.HBM)


def _tile(n, target, align=LANE):
    if n <= target:
        return n
    best = None
    for t in range(align, target + 1, align):
        if n % t == 0:
            best = t
    assert best is not None, (n, target, align)
    return best


def _mm(a, b, *, name, ta=False, tb=False, add=None, out_dtype=F32, tm=1024, tn=640, tk=None):
    assert not (ta and tb)
    if ta:
        k_dim, m_dim = a.shape
    else:
        m_dim, k_dim = a.shape
    n_dim = b.shape[0] if tb else b.shape[1]
    assert b.shape[1 if tb else 0] == k_dim
    tm, tn = _tile(m_dim, tm), _tile(n_dim, tn)
    tk = _tile(k_dim, tk or (4096 if ta else 1024), 8 if ta else LANE)
    nk = k_dim // tk
    dn = TN if ta else NT if tb else NN

    def body(*refs):
        if add is None:
            a_ref, b_ref, o_ref = refs[:3]
            add_ref = None
        else:
            a_ref, b_ref, add_ref, o_ref = refs[:4]
        part = lax.dot_general(a_ref[...].astype(BF16), b_ref[...].astype(BF16), dn, preferred_element_type=F32)

        def finish(r):
            if add_ref is not None:
                r = r + add_ref[...]
            o_ref[...] = r.astype(o_ref.dtype)

        if nk == 1:
            finish(part)
        else:
            acc_ref = refs[-1]
            k = pl.program_id(2)

            @pl.when(k == 0)
            def _():
                acc_ref[...] = part

            @pl.when(k > 0)
            def _():
                acc_ref[...] += part

            @pl.when(k == nk - 1)
            def _():
                finish(acc_ref[...])

    a_spec = pl.BlockSpec((tk, tm), lambda i, j, k: (k, i)) if ta else pl.BlockSpec((tm, tk), lambda i, j, k: (i, k))
    b_spec = pl.BlockSpec((tn, tk), lambda i, j, k: (j, k)) if tb else pl.BlockSpec((tk, tn), lambda i, j, k: (k, j))
    in_specs = [a_spec, b_spec]
    args = [a, b]
    if add is not None:
        in_specs.append(pl.BlockSpec((tm, tn), lambda i, j, k: (i, j)))
        args.append(add)
    return pl.pallas_call(
        body,
        name=name,
        grid=(m_dim // tm, n_dim // tn, nk),
        in_specs=in_specs,
        out_specs=pl.BlockSpec((tm, tn), lambda i, j, k: (i, j)),
        out_shape=jax.ShapeDtypeStruct((m_dim, n_dim), out_dtype),
        scratch_shapes=[pltpu.VMEM((tm, tn), F32)] if nk > 1 else [],
        compiler_params=pltpu.CompilerParams(dimension_semantics=("parallel", "parallel", "arbitrary")),
    )(*args)


def _rowwise(fn, rows, bcs, outs, reds=(), *, name, tr, ncol=1):
    n_rows = rows[0][0].shape[0]
    tr = _tile(n_rows, tr, 8)
    nrow = n_rows // tr
    n_in, n_out = len(rows) + len(bcs), len(outs)

    def body(*refs):
        j, i = pl.program_id(0), pl.program_id(1)
        o_vals, r_vals = fn(*[r[...].astype(F32) for r in refs[:n_in]])
        for ref, val in zip(refs[n_in:n_in + n_out], o_vals):
            ref[...] = val.astype(ref.dtype)
        for ref, val, (_, _, stride) in zip(refs[n_in + n_out:], r_vals, reds):
            first = (i == 0) if stride else jnp.logical_and(i == 0, j == 0)

            @pl.when(first)
            def _():
                ref[...] = val

            @pl.when(jnp.logical_not(first))
            def _():
                ref[...] += val

    def spec(r, w, off, st, row_dep=True):
        if row_dep:
            return pl.BlockSpec((r, w), lambda j, i: (i, off + st * j))
        return pl.BlockSpec((r, w), lambda j, i: (0, off + st * j))

    in_specs = [spec(tr, w, off, st) for (_, w, off, st) in rows]
    in_specs += [spec(a.shape[0], w, off, st, False) for (a, w, off, st) in bcs]
    out_specs = [spec(tr, w, off, st) for (_, _, w, off, st) in outs]
    out_specs += [spec(1, w, 0, st, False) for (_, w, st) in reds]
    out_shape = [jax.ShapeDtypeStruct((n_rows, c), dt) for (c, dt, _, _, _) in outs]
    out_shape += [jax.ShapeDtypeStruct((1, c), F32) for (c, _, _) in reds]
    return pl.pallas_call(
        body,
        name=name,
        grid=(ncol, nrow),
        in_specs=in_specs,
        out_specs=out_specs,
        out_shape=out_shape,
        compiler_params=pltpu.CompilerParams(dimension_semantics=("arbitrary", "arbitrary")),
    )(*[r[0] for r in rows], *[b[0] for b in bcs])


def _full(a):
    return (a, a.shape[1], 0, 0)


def _rms(x, g):
    return x * lax.rsqrt(jnp.mean(x * x, axis=-1, keepdims=True) + EPS) * g


def _silu(x):
    return x * jax.nn.sigmoid(x)


def _softplus(x):
    return jnp.maximum(x, 0.0) + jnp.log(1.0 + jnp.exp(-jnp.abs(x)))


def _rms_fwd(x, g, *, name):
    (h,) = _rowwise(lambda xb, gb: ([_rms(xb, gb)], []), [_full(x)], [_full(g)], [(D, BF16, D, 0, 0)], name=name, tr=512)
    return h


def _rms_bwd(x, g, dh, dres, *, name):
    def fn(xb, dhb, dresb, gb):
        _, vjp = jax.vjp(_rms, xb, gb)
        dx, dg = vjp(dhb)
        return [dx + dresb], [dg]

    return _rowwise(fn, [_full(x), _full(dh), _full(dres)], [_full(g)], [(D, F32, D, 0, 0)], [(D, D, 0)], name=name, tr=256)


def _gate_scalars(raw, al_row, dtb_row):
    lane = lax.broadcasted_iota(jnp.int32, raw.shape, 1)
    beta = jax.nn.sigmoid(raw)
    g = -jnp.exp(al_row) * _softplus(raw + dtb_row)
    return jnp.where(lane < B_HEADS, beta, jnp.where(lane < 2 * B_HEADS, g, 0.0))


def _gated_norm(o, z, w):
    return _rms(o, w) * _silu(z)


def _merge(ga, gb, ta, tb):
    return jax.nn.sigmoid(ga) * ta + jax.nn.sigmoid(gb) * tb


def _swiglu(gu):
    return _silu(gu[:, :D_FF]) * gu[:, D_FF:]


def _head_loss(x2, pg, pp, tgt, g):
    x3 = x2 + jax.nn.sigmoid(pg) * pp
    err = _rms(x3, g) - tgt
    return 0.5 * jnp.sum(jnp.mean(err * err, axis=-1))


CONV_W = 256


def _conv_taps(x, w):
    row = lax.broadcasted_iota(jnp.int32, x.shape, 0)
    shifted = [x] + [jnp.where(row >= s, pltpu.roll(x, s, 0), 0.0) for s in (1, 2, 3)]
    pre = shifted[0] * w[3:4]
    for s in (1, 2, 3):
        pre = pre + shifted[s] * w[3 - s:4 - s]
    return pre, shifted


def _conv_fwd(projp, conv_w, n_batch, seq):
    ncol = B_CONV // CONV_W
    first = P_CONV // CONV_W

    def body(x_ref, w_ref, o_ref):
        pre, _ = _conv_taps(x_ref[...].astype(F32), w_ref[...])
        o_ref[...] = _silu(pre)

    return pl.pallas_call(
        body,
        name="conv_fwd",
        grid=(ncol, n_batch),
        in_specs=[pl.BlockSpec((seq, CONV_W), lambda j, b: (b, first + j)), pl.BlockSpec((4, CONV_W), lambda j, b: (0, j))],
        out_specs=pl.BlockSpec((seq, CONV_W), lambda j, b: (b, j)),
        out_shape=jax.ShapeDtypeStruct((n_batch * seq, B_CONV), F32),
        compiler_params=pltpu.CompilerParams(dimension_semantics=("parallel", "parallel")),
    )(projp, conv_w)


def _conv_bwd(projp, conv_w, dc, n_batch, seq):
    width = dc.shape[1]
    ncol = width // CONV_W
    first_x = P_CONV // CONV_W

    def body(x_ref, w_ref, dc_ref, dx_ref, dw_ref):
        b = pl.program_id(1)
        w = w_ref[...]
        pre, shifted = _conv_taps(x_ref[...].astype(F32), w)
        sg = jax.nn.sigmoid(pre)
        dpre = dc_ref[...] * (sg * (1.0 + pre * (1.0 - sg)))
        row = lax.broadcasted_iota(jnp.int32, dpre.shape, 0)
        dx = dpre * w[3:4]
        for s in (1, 2, 3):
            dx = dx + jnp.where(row < seq - s, pltpu.roll(dpre, seq - s, 0), 0.0) * w[3 - s:4 - s]
        dx_ref[...] = dx.astype(dx_ref.dtype)
        for s in (0, 1, 2, 3):
            part = jnp.sum(dpre * shifted[s], axis=0, keepdims=True)

            @pl.when(b == 0)
            def _():
                dw_ref[3 - s:4 - s, :] = part

            @pl.when(b > 0)
            def _():
                dw_ref[3 - s:4 - s, :] += part

    return pl.pallas_call(
        body,
        name="conv_bwd",
        grid=(ncol, n_batch),
        in_specs=[
            pl.BlockSpec((seq, CONV_W), lambda j, b: (b, first_x + j)),
            pl.BlockSpec((4, CONV_W), lambda j, b: (0, j)),
            pl.BlockSpec((seq, CONV_W), lambda j, b: (b, j)),
        ],
        out_specs=[pl.BlockSpec((seq, CONV_W), lambda j, b: (b, j)), pl.BlockSpec((4, CONV_W), lambda j, b: (0, j))],
        out_shape=[jax.ShapeDtypeStruct((n_batch * seq, width), BF16), jax.ShapeDtypeStruct((4, width), F32)],
        compiler_params=pltpu.CompilerParams(dimension_semantics=("arbitrary", "arbitrary")),
    )(projp, conv_w, dc)


def _attn_chunk(qc, kb, vb, bias2, valid, lane_lo):
    sel = (lane_lo, jnp.logical_not(lane_lo))
    items = [(i, e) for i in range(len(qc)) for e in (0, 1)]
    k16, v16 = [t.astype(BF16) for t in kb], [t.astype(BF16) for t in vb]
    qm = [(jnp.where(sel[e], qc[i], 0.0) * (A_DIM ** -0.5)).astype(BF16) for i, e in items]
    s = [lax.dot_general(qm[n], k16[i], NT, preferred_element_type=F32) + bias2[e] for n, (i, e) in enumerate(items)]
    s = [jnp.where(valid[i], s[n], -1e30) for n, (i, e) in enumerate(items)]
    p = [jnp.exp(t - lax.stop_gradient(jnp.max(t, axis=-1, keepdims=True))) for t in s]
    p = [t * (1.0 / jnp.sum(t, axis=-1, keepdims=True)) for t in p]
    o = [jnp.where(sel[e], jnp.dot(p[n].astype(BF16), v16[i], preferred_element_type=F32), 0.0)
         for n, (i, e) in enumerate(items)]
    return [o[2 * i] + o[2 * i + 1] for i in range(len(qc))]


ATTN_GROUP_FWD, ATTN_GROUP_BWD = 4, 4


def _attn_group(g, group, q_ref, kp_ref, vp_ref):
    col = lax.broadcasted_iota(jnp.int32, (CHUNK, A_BAND), 1)
    lane_lo = lax.broadcasted_iota(jnp.int32, (1, LANE), 1) < A_DIM
    starts = [pl.multiple_of((g * group + i) * CHUNK, CHUNK) for i in range(group)]
    rows = [pl.ds(r0, CHUNK) for r0 in starts]
    bands = [pl.ds(r0, A_BAND) for r0 in starts]
    valid = [col + r0 >= A_PAD for r0 in starts]
    loaded = [q_ref[r, :].astype(F32) for r in rows], [kp_ref[b, :] for b in bands], [vp_ref[b, :] for b in bands]
    return rows, bands, loaded, valid, lane_lo


def _attn_specs(seq):
    def blk(first):
        return pl.BlockSpec((seq, LANE), lambda hp, b: (b, first + hp))

    return blk, pl.BlockSpec((2, CHUNK, A_BAND), lambda hp, b: (hp, 0, 0))


def _attn_fwd(projp, bias, n_batch, seq):
    nc = seq // CHUNK
    blk, bias_spec = _attn_specs(seq)

    def body(q_ref, k_ref, v_ref, b_ref, o_ref, kp_ref, vp_ref):
        kp_ref[0:A_PAD, :] = jnp.zeros((A_PAD, LANE), F32)
        vp_ref[0:A_PAD, :] = jnp.zeros((A_PAD, LANE), F32)
        kp_ref[A_PAD:, :] = k_ref[...].astype(F32)
        vp_ref[A_PAD:, :] = v_ref[...].astype(F32)
        bias2 = b_ref[...]

        def step(g, carry):
            rows, _, (qc, kb, vb), valid, lane_lo = _attn_group(g, ATTN_GROUP_FWD, q_ref, kp_ref, vp_ref)
            out = _attn_chunk(qc, kb, vb, bias2, valid, lane_lo)
            for r, o in zip(rows, out):
                o_ref[r, :] = o.astype(o_ref.dtype)
            return carry

        lax.fori_loop(0, nc // ATTN_GROUP_FWD, step, 0)

    return pl.pallas_call(
        body,
        name="attn_fwd",
        grid=(A_HEADS // 2, n_batch),
        in_specs=[blk(0), blk(4), blk(8), bias_spec],
        out_specs=pl.BlockSpec((seq, LANE), lambda hp, b: (b, hp)),
        out_shape=jax.ShapeDtypeStruct((n_batch * seq, A_WIDTH), BF16),
        scratch_shapes=[pltpu.VMEM((A_PAD + seq, LANE), F32), pltpu.VMEM((A_PAD + seq, LANE), F32)],
        compiler_params=pltpu.CompilerParams(dimension_semantics=("parallel", "parallel")),
    )(projp, projp, projp, bias)


def _attn_bwd(projp, bias, dy, n_batch, seq):
    nc = seq // CHUNK
    blk, bias_spec = _attn_specs(seq)
    out_blk = pl.BlockSpec((seq, LANE), lambda hp, b: (b, hp))

    def body(q_ref, k_ref, v_ref, b_ref, dy_ref, dq_ref, dk_ref, dv_ref, db_ref, kp_ref, vp_ref, dkp_ref, dvp_ref):
        b = pl.program_id(1)
        kp_ref[0:A_PAD, :] = jnp.zeros((A_PAD, LANE), F32)
        vp_ref[0:A_PAD, :] = jnp.zeros((A_PAD, LANE), F32)
        kp_ref[A_PAD:, :] = k_ref[...].astype(F32)
        vp_ref[A_PAD:, :] = v_ref[...].astype(F32)
        dkp_ref[...] = jnp.zeros_like(dkp_ref)
        dvp_ref[...] = jnp.zeros_like(dvp_ref)
        bias2 = b_ref[...]

        @pl.when(b == 0)
        def _():
            db_ref[...] = jnp.zeros_like(db_ref)

        def step(g, carry):
            rows, bands, (qc, kb, vb), valid, lane_lo = _attn_group(g, ATTN_GROUP_BWD, q_ref, kp_ref, vp_ref)
            _, vjp = jax.vjp(lambda q, k, v, bb: _attn_chunk(q, k, v, bb, valid, lane_lo), qc, kb, vb, bias2)
            dq, dk, dv, dbias = vjp([dy_ref[r, :] for r in rows])
            for i, r in enumerate(rows):
                dq_ref[r, :] = dq[i].astype(dq_ref.dtype)
            for i, band in enumerate(bands):
                dkp_ref[band, :] += dk[i]
                dvp_ref[band, :] += dv[i]
            db_ref[...] += dbias
            return carry

        lax.fori_loop(0, nc // ATTN_GROUP_BWD, step, 0)
        dk_ref[...] = dkp_ref[A_PAD:, :].astype(dk_ref.dtype)
        dv_ref[...] = dvp_ref[A_PAD:, :].astype(dv_ref.dtype)

    n_tok = n_batch * seq
    pad = pltpu.VMEM((A_PAD + seq, LANE), F32)
    return pl.pallas_call(
        body,
        name="attn_bwd",
        grid=(A_HEADS // 2, n_batch),
        in_specs=[blk(0), blk(4), blk(8), bias_spec, out_blk],
        out_specs=[out_blk, out_blk, out_blk, bias_spec],
        out_shape=[jax.ShapeDtypeStruct((n_tok, A_WIDTH), BF16)] * 3 + [jax.ShapeDtypeStruct((A_HEADS, CHUNK, A_BAND), F32)],
        scratch_shapes=[pad, pad, pad, pad],
        compiler_params=pltpu.CompilerParams(dimension_semantics=("arbitrary", "arbitrary")),
    )(projp, projp, projp, bias, dy)


def _rel_bias_table(rel_bias):
    span = CHUNK + A_BAND - 1
    near = REL_CLIP + CHUNK
    far = jnp.broadcast_to(rel_bias[:, 2 * REL_CLIP:], (A_HEADS, span - near))
    t = jnp.concatenate([rel_bias[:, 2 * REL_CLIP + 1 - near:], far], axis=1)
    u = jnp.concatenate([t[:, :A_BAND][:, ::-1], t[:, A_BAND:][:, ::-1]], axis=1)
    rolled = jnp.tile(u, (1, CHUNK))[:, :CHUNK * (span - 1)].reshape(A_HEADS, CHUNK, span - 1)
    return rolled[:, :, :A_BAND]


def _dot(a, b, dn=NN):
    return lax.dot_general(a, b, dn, precision=DELTA_PREC, preferred_element_type=F32)


def _dot16(a, b, dn=NN):
    return lax.dot_general(a.astype(BF16), b.astype(BF16), dn, preferred_element_type=F32)


def _each(fn, *lists):
    return [fn(*vals) for vals in zip(*lists)]


@jax.custom_vjp
def _saved_inverse(x, inv):
    return inv


def _saved_inverse_fwd(x, inv):
    return inv, inv


def _saved_inverse_bwd(inv, ct):
    return _dot(_dot(inv, ct, TN), inv, NT), jnp.zeros_like(inv)


_saved_inverse.defvjp(_saved_inverse_fwd, _saved_inverse_bwd)


def _delta_chunk(r_state, cq, ck, cv, beta, g, saved_inv=None):
    ii = lax.broadcasted_iota(jnp.int32, (CHUNK, CHUNK), 0)
    jj = lax.broadcasted_iota(jnp.int32, (CHUNK, CHUNK), 1)
    incl, strict, eye = ii >= jj, ii > jj, ii == jj
    q = _each(lambda t: t * lax.rsqrt(jnp.sum(t * t, axis=-1, keepdims=True) + EPS) * (B_DIM ** -0.5), cq)
    k = _each(lambda t: t * lax.rsqrt(jnp.sum(t * t, axis=-1, keepdims=True) + EPS), ck)
    g_b = _each(lambda t: jnp.broadcast_to(t, (CHUNK, CHUNK)), g)
    g_row = _each(lambda t: jnp.sum(jnp.where(eye, t, 0.0), axis=0, keepdims=True), g_b)
    gc_col = _each(lambda t: jnp.sum(jnp.where(incl, t, 0.0), axis=1, keepdims=True), g_row)
    gc_row = _each(lambda t: jnp.sum(jnp.where(ii <= jj, t, 0.0), axis=0, keepdims=True), g_b)
    decay = _each(lambda c, r: jnp.where(incl, jnp.exp(jnp.where(incl, c - r, 0.0)), 0.0), gc_col, gc_row)
    kk = _each(lambda t: _dot(t, t, NT), k)
    x = _each(lambda b, m, d: jnp.where(strict, -(b * m * d), 0.0), beta, kk, decay)
    if saved_inv is None:
        inv = _each(lambda t: jnp.where(eye, 1.0, 0.0) + t, x)
        pw = x
        for _ in range(5):
            pw = _each(lambda t: _dot(t, t), pw)
            inv = _each(lambda t, s: t + _dot(t, s), inv, pw)
    else:
        inv = _each(_saved_inverse, x, saved_inv)
    egc = _each(jnp.exp, gc_col)
    u = _each(lambda t, b, v: _dot16(t, b * v), inv, beta, cv)
    wk = _each(lambda t, b, e, kh: _dot16(t, (b * e) * kh), inv, beta, egc, k)
    pqk = _each(lambda qh, kh, d: _dot16(qh, kh, NT) * d, q, k, decay)
    g_last = _each(lambda c: c[CHUNK - 1:CHUNK, :], gc_col)
    kdec = _each(lambda kh, gl, c: kh * jnp.exp(gl - c), k, g_last, gc_col)
    w = _each(lambda uh, wkh, r: uh - _dot16(wkh, r), u, wk, r_state)
    o = _each(lambda e, qh, r, ph, wh: e * _dot16(qh, r) + _dot16(ph, wh), egc, q, r_state, pqk, w)
    r_new = _each(lambda gl, r, kd, wh: jnp.exp(gl) * r + _dot16(kd, wh, TN), g_last, r_state, kdec, w)
    return o, r_new, inv


DELTA_BLK = 512


def _delta_blocks(n_batch, seq):
    nblk = seq // DELTA_BLK
    cpb = DELTA_BLK // CHUNK

    def rows(width, order):
        return pl.BlockSpec((DELTA_BLK, width), lambda b, i: (b * nblk + order(i), 0))

    def states(order, side):
        return pl.BlockSpec((cpb, B_HEADS, side, side), lambda b, i: (b * nblk + order(i), 0, 0, 0))

    return nblk, cpb, rows, states


def _head_cols(h):
    return [pl.ds(part * B_HEADS * B_DIM + h * B_DIM, B_DIM) for part in range(3)]


def _load_heads(c_ref, bg_ref, state_ref, rows):
    bg_c = bg_ref[rows, :]
    cols = [_head_cols(h) for h in range(B_HEADS)]
    return ([state_ref[h] for h in range(B_HEADS)], [c_ref[rows, c[0]] for c in cols], [c_ref[rows, c[1]] for c in cols],
            [c_ref[rows, c[2]] for c in cols], [bg_c[:, h:h + 1] for h in range(B_HEADS)],
            [bg_c[:, B_HEADS + h:B_HEADS + h + 1] for h in range(B_HEADS)])


def _delta_fwd(conv, bg, n_batch, seq):
    nblk, cpb, rows_spec, states_spec = _delta_blocks(n_batch, seq)

    def forward(i):
        return i

    def body(c_ref, bg_ref, o_ref, st_ref, inv_ref, r_ref):
        @pl.when(pl.program_id(1) == 0)
        def _():
            r_ref[...] = jnp.zeros_like(r_ref)

        def step(c, carry):
            rows = pl.ds(pl.multiple_of(c * CHUNK, CHUNK), CHUNK)
            args = _load_heads(c_ref, bg_ref, r_ref, rows)
            o, r_new, inv = _delta_chunk(*args)
            for h in range(B_HEADS):
                st_ref[c, h] = args[0][h]
                inv_ref[c, h] = inv[h]
                o_ref[rows, pl.ds(h * B_DIM, B_DIM)] = o[h]
            for h in range(B_HEADS):
                r_ref[h] = r_new[h]
            return carry

        lax.fori_loop(0, cpb, step, 0)

    n_tok = n_batch * seq
    return pl.pallas_call(
        body,
        name="delta_fwd",
        grid=(n_batch, nblk),
        in_specs=[rows_spec(B_CONV, forward), rows_spec(LANE, forward)],
        out_specs=[rows_spec(B_HEADS * B_DIM, forward), states_spec(forward, B_DIM), states_spec(forward, CHUNK)],
        out_shape=[jax.ShapeDtypeStruct((n_tok, B_HEADS * B_DIM), F32),
                   jax.ShapeDtypeStruct((n_tok // CHUNK, B_HEADS, B_DIM, B_DIM), F32),
                   jax.ShapeDtypeStruct((n_tok // CHUNK, B_HEADS, CHUNK, CHUNK), F32)],
        scratch_shapes=[pltpu.VMEM((B_HEADS, B_DIM, B_DIM), F32)],
        compiler_params=pltpu.CompilerParams(dimension_semantics=("arbitrary", "arbitrary")),
    )(conv, bg)


def _delta_bwd(conv, bg, states, inverses, do, n_batch, seq):
    nblk, cpb, rows_spec, states_spec = _delta_blocks(n_batch, seq)

    def backward(i):
        return nblk - 1 - i

    def body(c_ref, bg_ref, st_ref, inv_ref, do_ref, dc_ref, dbg_ref, dr_ref):
        @pl.when(pl.program_id(1) == 0)
        def _():
            dr_ref[...] = jnp.zeros_like(dr_ref)

        def step(n, carry):
            c = cpb - 1 - n
            rows = pl.ds(pl.multiple_of(c * CHUNK, CHUNK), CHUNK)
            saved = [inv_ref[c, h] for h in range(B_HEADS)]
            _, vjp = jax.vjp(lambda *args: _delta_chunk(*args, saved_inv=saved)[:2],
                             *_load_heads(c_ref, bg_ref, st_ref.at[c], rows))
            do = [do_ref[rows, pl.ds(h * B_DIM, B_DIM)] for h in range(B_HEADS)]
            dr, dq, dk, dv, dbeta, dg = vjp((do, [dr_ref[h] for h in range(B_HEADS)]))
            lane = lax.broadcasted_iota(jnp.int32, (CHUNK, LANE), 1)
            dbg = jnp.zeros((CHUNK, LANE), F32)
            for h in range(B_HEADS):
                cq, ck, cv = _head_cols(h)
                dr_ref[h] = dr[h]
                dc_ref[rows, cq] = dq[h]
                dc_ref[rows, ck] = dk[h]
                dc_ref[rows, cv] = dv[h]
                dbg = dbg + jnp.where(lane == h, dbeta[h], 0.0) + jnp.where(lane == h + B_HEADS, dg[h], 0.0)
            dbg_ref[rows, :] = dbg
            return carry

        lax.fori_loop(0, cpb, step, 0)

    n_tok = n_batch * seq
    return pl.pallas_call(
        body,
        name="delta_bwd",
        grid=(n_batch, nblk),
        in_specs=[rows_spec(B_CONV, backward), rows_spec(LANE, backward), states_spec(backward, B_DIM),
                  states_spec(backward, CHUNK), rows_spec(B_HEADS * B_DIM, backward)],
        out_specs=[rows_spec(B_CONV, backward), rows_spec(LANE, backward)],
        out_shape=[jax.ShapeDtypeStruct((n_tok, B_CONV), F32), jax.ShapeDtypeStruct((n_tok, LANE), F32)],
        scratch_shapes=[pltpu.VMEM((B_HEADS, B_DIM, B_DIM), F32)],
        compiler_params=pltpu.CompilerParams(dimension_semantics=("arbitrary", "arbitrary")),
    )(conv, bg, states, inverses, do)


def _lane_row(vec4, first):
    return jnp.concatenate([jnp.zeros((1, first), F32), vec4.reshape(1, B_HEADS).astype(F32),
                            jnp.zeros((1, LANE - first - B_HEADS), F32)], axis=1)


def _local_step(x3d, p3d, tgt3d, w_in, small, rest_weights, send_grads, send_w_in):
    n_batch, seq, _ = x3d.shape
    n_tok = n_batch * seq
    x, p, tgt = x3d.reshape(n_tok, D), p3d.reshape(n_tok, -1), tgt3d.reshape(n_tok, D)
    g_mix, g_ffn, g_ple, g_final = (small[k].reshape(1, D) for k in ("g_mix", "g_ffn", "g_ple", "g_final"))
    w_onorm = small["w_onorm"].reshape(1, B_DIM)
    al_row = _lane_row(small["a_log"], B_HEADS)
    dtb_row = _lane_row(small["dt_bias"], B_HEADS)
    rel_bias = small["rel_bias"].reshape(A_HEADS, -1)
    bias = _rel_bias_table(rel_bias)
    conv_w = small["conv_w"].reshape(4, B_CONV)

    h1 = _rms_fwd(x, g_mix, name="rms_mix")
    projp = _mm(h1, w_in, tb=True, out_dtype=BF16, name="mm_proj", tn=640)
    bd = _mm(h1, w_in[P_BD:], tb=True, name="mm_beta_decay", tn=LANE)
    y_a = _attn_fwd(projp, bias, n_batch, seq)
    conv = _conv_fwd(projp, conv_w, n_batch, seq)
    (bg,) = _rowwise(lambda raw, al, dtb: ([_gate_scalars(raw, al, dtb)], []), [_full(bd)],
                     [_full(al_row), _full(dtb_row)], [(LANE, F32, LANE, 0, 0)], name="gate_scalars", tr=1024)
    o_b, states, inverses = _delta_fwd(conv, bg, n_batch, seq)
    (y_b,) = _rowwise(lambda o, z, wn: ([_gated_norm(o, z, wn)], []), [(o_b, LANE, 0, 1), (projp, LANE, P_Z // LANE, 1)],
                      [_full(w_onorm)], [(B_HEADS * B_DIM, BF16, LANE, 0, 1)], name="gated_norm", tr=1024, ncol=B_HEADS)
    w = rest_weights(y_b)
    t_a = _mm(y_a, w["w_branch_a"], tb=True, name="mm_branch_a", tn=1024)
    t_b = _mm(y_b, w["w_branch_b"], tb=True, name="mm_branch_b", tn=1024)
    half = D // 2
    gate_rows = [(projp, half, P_GATE // half, 1), (projp, half, P_GATE // half + 2, 1), (t_a, half, 0, 1), (t_b, half, 0, 1)]
    (merged,) = _rowwise(lambda ga, gb, ta, tb: ([_merge(ga, gb, ta, tb)], []), gate_rows, [], [(D, BF16, half, 0, 1)],
                         name="merge", tr=512, ncol=2)
    x1 = _mm(merged, w["w_out"], add=x, name="mm_out", tn=1024)
    h2 = _rms_fwd(x1, g_ffn, name="rms_ffn")
    gu = _mm(h2, w["w_gate_up"], tb=True, out_dtype=BF16, name="mm_gate_up", tn=512)
    (act,) = _rowwise(lambda gub: ([_swiglu(gub)], []), [_full(gu)], [], [(D_FF, BF16, D_FF, 0, 0)], name="swiglu",
                      tr=512)
    x2 = _mm(act, w["w_down"], add=x1, name="mm_down", tn=1024, tk=1408)
    h3 = _rms_fwd(x2, g_ple, name="rms_ple")
    pg = _mm(h3, w["w_ple_gate"], name="mm_ple_gate", tn=1024)
    pp = _mm(p, w["w_ple_proj"], tb=True, name="mm_ple_proj", tn=1024)

    def head_fn(x2b, pgb, ppb, tb, gb):
        loss, (dx2, dpg, dpp, dg) = jax.value_and_grad(_head_loss, argnums=(0, 1, 2, 4))(x2b, pgb, ppb, tb, gb)
        return [dx2, dpg, dpp], [dg, jnp.full((1, LANE), loss, F32)]

    dx3, dpg, dpp, dg_final, loss_row = _rowwise(
        head_fn, [_full(x2), _full(pg), _full(pp), _full(tgt)], [_full(g_final)],
        [(D, F32, D, 0, 0), (D, BF16, D, 0, 0), (D, BF16, D, 0, 0)], [(D, D, 0), (LANE, LANE, 0)], name="loss_head", tr=256)
    gw = {}
    gw["w_ple_proj"] = _mm(dpp, p, ta=True, out_dtype=BF16, name="mm_d_ple_proj", tn=256)
    gw["w_ple_gate"] = _mm(h3, dpg, ta=True, out_dtype=BF16, name="mm_d_ple_gate", tn=512)
    dh3 = _mm(dpg, w["w_ple_gate"], tb=True, name="mm_dh3", tn=1024)
    dx2, dg_ple = _rms_bwd(x2, g_ple, dh3, dx3, name="rms_ple_bwd")
    gw["w_down"] = _mm(act, dx2, ta=True, out_dtype=BF16, name="mm_d_down", tm=1408, tn=512, tk=2048)
    dact = _mm(dx2, w["w_down"], tb=True, out_dtype=BF16, name="mm_dact", tn=1408)

    def swiglu_bwd(gub, dab):
        _, vjp = jax.vjp(_swiglu, gub)
        return [vjp(dab)[0]], []

    (dgu,) = _rowwise(swiglu_bwd, [_full(gu), _full(dact)], [], [(2 * D_FF, BF16, 2 * D_FF, 0, 0)], name="swiglu_bwd", tr=256)
    gw["w_gate_up"] = _mm(dgu, h2, ta=True, out_dtype=BF16, name="mm_d_gate_up", tm=512, tn=1024)
    dh2 = _mm(dgu, w["w_gate_up"], name="mm_dh2", tn=1024, tk=1408)
    dx1, dg_ffn = _rms_bwd(x1, g_ffn, dh2, dx2, name="rms_ffn_bwd")
    gw["w_out"] = _mm(merged, dx1, ta=True, out_dtype=BF16, name="mm_d_out", tn=512)
    dmerged = _mm(dx1, w["w_out"], tb=True, name="mm_dmerged", tn=1024)

    def merge_bwd(ga, gb, ta, tb, dm):
        _, vjp = jax.vjp(_merge, ga, gb, ta, tb)
        return list(vjp(dm)), []

    dga, dgb, dta, dtb = _rowwise(merge_bwd, gate_rows + [(dmerged, half, 0, 1)], [], [(D, BF16, half, 0, 1)] * 4,
                                  name="merge_bwd", tr=512, ncol=2)
    gw["w_branch_a"] = _mm(dta, y_a, ta=True, out_dtype=BF16, name="mm_d_branch_a", tn=512)
    gw["w_branch_b"] = _mm(dtb, y_b, ta=True, out_dtype=BF16, name="mm_d_branch_b", tn=512)
    dya = _mm(dta, w["w_branch_a"], name="mm_dya", tn=512)
    dyb = _mm(dtb, w["w_branch_b"], name="mm_dyb", tn=512)

    w_onorm = w_onorm + send_grads(gw)[0, 0]

    def gated_norm_bwd(o, z, dy, wn):
        _, vjp = jax.vjp(_gated_norm, o, z, wn)
        do, dz, dwn = vjp(dy)
        return [do, dz], [dwn]

    do_b, dz, dw_onorm = _rowwise(
        gated_norm_bwd, [(o_b, LANE, 0, 1), (projp, LANE, P_Z // LANE, 1), (dyb, LANE, 0, 1)], [_full(w_onorm)],
        [(B_HEADS * B_DIM, F32, LANE, 0, 1), (B_HEADS * B_DIM, BF16, LANE, 0, 1)], [(B_DIM, B_DIM, 0)],
        name="gated_norm_bwd", tr=1024, ncol=B_HEADS)
    dconv_out, dbg = _delta_bwd(conv, bg, states, inverses, do_b, n_batch, seq)

    def gate_scalars_bwd(raw, dbgb, al, dtb):
        _, vjp = jax.vjp(_gate_scalars, raw, al, dtb)
        draw, dal, ddtb = vjp(dbgb)
        return [draw], [dal, ddtb]

    dbd, dal_row, ddtb_row = _rowwise(gate_scalars_bwd, [_full(bd), _full(dbg)], [_full(al_row), _full(dtb_row)],
                                      [(LANE, BF16, LANE, 0, 0)], [(LANE, LANE, 0), (LANE, LANE, 0)], name="gate_scalars_bwd",
                                      tr=1024)
    dconv, dconv_w = _conv_bwd(projp, conv_w, dconv_out, n_batch, seq)
    dq_a, dk_a, dv_a, dbias = _attn_bwd(projp, bias, dya, n_batch, seq)
    dprojp = jnp.concatenate([dq_a, dk_a, dv_a, dconv, dz, dga, dgb, dbd], axis=1)
    sent = send_w_in(_mm(dprojp, h1, ta=True, out_dtype=BF16, name="mm_d_in", tm=640, tn=1024))
    sent, dprojp = lax.optimization_barrier((sent, dprojp))
    dh1 = _mm(dprojp, w_in, name="mm_dh1", tn=1024, tk=1152)
    grad_x, dg_mix = _rms_bwd(x, g_mix + sent[0, 0], dh1, dx1, name="rms_mix_bwd")

    _, bias_vjp = jax.vjp(_rel_bias_table, rel_bias)
    gs = {
        "g_mix": dg_mix, "g_ffn": dg_ffn, "g_ple": dg_ple, "g_final": dg_final, "w_onorm": dw_onorm,
        "conv_w": dconv_w, "rel_bias": bias_vjp(dbias)[0],
        "a_log": dal_row[0, B_HEADS:2 * B_HEADS], "dt_bias": ddtb_row[0, B_HEADS:2 * B_HEADS],
    }
    return loss_row[:, :1], grad_x.reshape(n_batch, seq, D), gs


MATRICES = (("w_in", 1), ("w_gate_up", 1), ("w_branch_a", 1), ("w_branch_b", 1), ("w_out", 0), ("w_down", 0),
            ("w_ple_gate", 0), ("w_ple_proj", 1))
TAPS_PER_SHARD = B_CONV // N_DEV


def _held(shard, axis):
    return shard if axis == 0 else shard.T


def _from_gathered(slabs):
    return slabs.reshape(-1, slabs.shape[-1])


def _to_owner(held):
    return held.reshape(N_DEV, held.shape[0] // N_DEV, held.shape[1])


def _permute_w_in(held):
    n_gate = P_BD - P_GATE
    row = lax.broadcasted_iota(jnp.int32, (P_END, 1), 0)
    same = jnp.pad(held, ((0, P_END - D_IN), (0, 0)))
    up = jnp.pad(held[8:], ((0, P_END - D_IN + 8), (0, 0)))
    down = jnp.pad(held[:P_GATE + 8], ((n_gate, P_END - P_BD - 8), (0, 0)))
    zero = jnp.zeros((), held.dtype)
    return jnp.where(row < P_GATE, same, jnp.where(row < P_BD, up, jnp.where(row < P_BD + 8, down, zero)))


def _unpermute_w_in(gp):
    n_gate = P_BD - P_GATE
    row = lax.broadcasted_iota(jnp.int32, (D_IN, 1), 0)
    same = gp[:D_IN]
    up = jnp.pad(gp[n_gate:], ((0, D_IN - (P_END - n_gate)), (0, 0)))
    down = jnp.pad(gp[:P_BD], ((8, 0), (0, 0)))
    return jnp.where(row < P_GATE, same, jnp.where(row < P_GATE + 8, up, down))


SMALL_ROWS = 16
SMALL_LAYOUT = (("g_mix", 0, D), ("g_ffn", 1, D), ("g_ple", 2, D), ("g_final", 3, D), ("conv_w", 4, 4 * B_CONV),
                ("rel_bias", 10, A_HEADS * (2 * REL_CLIP + 1)), ("w_onorm", 13, B_DIM), ("a_log", 14, B_HEADS),
                ("dt_bias", 14, B_HEADS), ("loss", 15, 1))


def _pack_small(gs):
    rows = {}
    for name, row, n in SMALL_LAYOUT:
        rows.setdefault(row, []).append(gs[name].reshape(-1).astype(F32))
    parts = []
    for row in sorted(rows):
        flat = jnp.concatenate(rows[row])
        parts.append(jnp.concatenate([flat, jnp.zeros((-flat.shape[0] % D,), F32)]))
    flat = jnp.concatenate(parts)
    assert flat.shape[0] == SMALL_ROWS * D, flat.shape
    return flat.reshape(SMALL_ROWS, D)


def _unpack_small(blk):
    flat, out, used = blk.reshape(-1), {}, {}
    for name, row, n in SMALL_LAYOUT:
        start = row * D + used.get(row, 0)
        out[name] = flat[start:start + n]
        used[row] = used.get(row, 0) + n
    return out


def _position():
    return lax.axis_index("x"), lax.axis_index("y"), lax.axis_index("c")


PEERS = N_DEV - 1


def _comm_call(body, arrays, out_shapes, *, name):
    n = len(arrays)
    return pl.pallas_call(
        body,
        name=name,
        out_shape=out_shapes,
        in_specs=[HBM_SPEC] * n,
        out_specs=[HBM_SPEC] * n,
        scratch_shapes=[pltpu.SemaphoreType.DMA((PEERS * n,)), pltpu.SemaphoreType.DMA((PEERS * n,)),
                        pltpu.SemaphoreType.DMA((n,))],
    )(*arrays)


def _weights_allgather(shards):
    n = len(shards)

    def body(*refs):
        ins, outs = refs[:n], refs[n:2 * n]
        send_sems, recv_sems, local_sems = refs[2 * n:]
        x, y, c = _position()
        me, sibling = (x, y, c), (x, y, 1 - c)
        chips = [(1 - x, y), (x, 1 - y), (1 - x, 1 - y)]

        def slab(a, px, py, pc):
            return outs[a].at[4 * px + 2 * py + pc]

        def copy(a, k, block, to, src=None):
            return pltpu.make_async_remote_copy(src_ref=slab(a, *block) if src is None else src, dst_ref=slab(a, *block),
                                                send_sem=send_sems.at[PEERS * a + k], recv_sem=recv_sems.at[PEERS * a + k],
                                                device_id=to, device_id_type=MESH)

        local = [pltpu.make_async_copy(ins[a], slab(a, *me), local_sems.at[a]) for a in range(n)]
        sent = [copy(a, 1 + j, me, (*chip, c), src=ins[a]) for a in range(n) for j, chip in enumerate(chips)]
        sent += [copy(a, 0, me, sibling, src=ins[a]) for a in range(n)]
        for cp in sent + local:
            cp.start()
        for a in range(n):
            for j, chip in enumerate(chips):
                copy(a, 1 + j, (*chip, c), me).wait_recv()
                passed = copy(a, 4 + j, (*chip, c), sibling)
                passed.start()
                sent.append(passed)
        for a in range(n):
            copy(a, 0, sibling, me).wait_recv()
            for j, chip in enumerate(chips):
                copy(a, 4 + j, (*chip, 1 - c), me).wait_recv()
        for cp in sent:
            cp.wait_send()
        for cp in local:
            cp.wait()

    return _comm_call(body, shards, [jax.ShapeDtypeStruct((N_DEV,) + s.shape, s.dtype) for s in shards],
                      name="weights_allgather")


def _grads_exchange(by_owner):
    n = len(by_owner)

    def body(*refs):
        ins, outs = refs[:n], refs[n:2 * n]
        send_sems, recv_sems, local_sems = refs[2 * n:]
        x, y, c = _position()
        mine = 4 * x + 2 * y + c
        local = [pltpu.make_async_copy(ins[a].at[mine], outs[a].at[mine], local_sems.at[a]) for a in range(n)]
        for cp in local:
            cp.start()
        flips = [(dx, dy, dc) for dx in (0, 1) for dy in (0, 1) for dc in (0, 1) if dx + dy + dc]
        pending = []
        for k, (dx, dy, dc) in enumerate(flips):
            px, py, pc = (1 - x if dx else x), (1 - y if dy else y), (1 - c if dc else c)
            peer = 4 * px + 2 * py + pc
            for a in range(n):
                def remote(slot):
                    return pltpu.make_async_remote_copy(src_ref=ins[a].at[peer], dst_ref=outs[a].at[slot],
                                                        send_sem=send_sems.at[PEERS * a + k], recv_sem=recv_sems.at[PEERS * a + k],
                                                        device_id=(px, py, pc), device_id_type=MESH)

                sent = remote(mine)
                sent.start()
                pending.append((sent, remote(peer)))
        for sent, landed in pending:
            landed.wait_recv()
            sent.wait_send()
        for cp in local:
            cp.wait()

    return _comm_call(body, by_owner, [jax.ShapeDtypeStruct(g.shape, g.dtype) for g in by_owner], name="grads_exchange")


SEM_SPEC = pl.BlockSpec(memory_space=pltpu.SEMAPHORE)
DATAFLOW = pltpu.SideEffectType.DATAFLOW_SIDE_EFFECTING


def _peer_copies(srcs, lands, send_sems, recv_sems, by_owner, arrival):
    x, y, c = _position()
    mine = 4 * x + 2 * y + c
    copies = []
    for k, (dx, dy, dc) in enumerate([(dx, dy, dc) for dx in (0, 1) for dy in (0, 1) for dc in (0, 1) if dx + dy + dc]):
        px, py, pc = (1 - x if dx else x), (1 - y if dy else y), (1 - c if dc else c)
        peer = 4 * px + 2 * py + pc
        for a, (src, land) in enumerate(zip(srcs, lands)):
            copies.append(pltpu.make_async_remote_copy(
                src_ref=src.at[peer] if by_owner else src, dst_ref=land.at[peer if arrival else mine],
                send_sem=send_sems.at[PEERS * a + k], recv_sem=recv_sems.at[PEERS * a + k],
                device_id=(px, py, pc), device_id_type=MESH))
    return copies


def _exchange_start(sources, by_owner, *, name):
    n = len(sources)
    lands = [lax.empty((N_DEV,) + (s.shape[1:] if by_owner else s.shape), s.dtype) for s in sources]

    def body(*refs):
        send_sems, recv_sems, token = refs[2 * n], refs[2 * n + 1], refs[-1]
        for copy in _peer_copies(refs[:n], refs[n:2 * n], send_sems, recv_sems, by_owner, arrival=False):
            copy.start()
        token[...] = jnp.zeros_like(token)

    sems = pltpu.SemaphoreType.DMA((PEERS * n,))
    outs = pl.pallas_call(
        body,
        name=name,
        out_shape=(sems, sems, *[pltpu.HBM(a.shape, a.dtype) for a in sources + lands], jax.ShapeDtypeStruct((8, LANE), F32)),
        in_specs=[HBM_SPEC] * (2 * n),
        out_specs=(SEM_SPEC, SEM_SPEC, *[HBM_SPEC] * (2 * n), pl.BlockSpec(memory_space=pltpu.VMEM)),
        input_output_aliases={i: 2 + i for i in range(2 * n)},
        compiler_params=pltpu.CompilerParams(has_side_effects=DATAFLOW),
    )(*[pltpu.with_memory_space_constraint(a, pltpu.HBM) for a in sources + lands])
    return outs[:-1], outs[-1]


def _exchange_wait(started, after, by_owner, *, name):
    send_sems, recv_sems, *arrays = started
    n = len(arrays) // 2

    def body(*refs):
        for copy in _peer_copies(refs[:n], refs[n:2 * n], refs[2 * n], refs[2 * n + 1], by_owner, arrival=True):
            copy.wait_send()
            copy.wait_recv()

    outs = pl.pallas_call(
        body,
        name=name,
        out_shape=[pltpu.HBM(a.shape, a.dtype) for a in arrays],
        in_specs=[HBM_SPEC] * (2 * n) + [SEM_SPEC, SEM_SPEC, pl.BlockSpec(memory_space=pl.ANY)],
        out_specs=[HBM_SPEC] * (2 * n),
        input_output_aliases={i: i for i in range(2 * n)},
        compiler_params=pltpu.CompilerParams(has_side_effects=DATAFLOW),
    )(*arrays, send_sems, recv_sems, after)
    return outs[:n], outs[n:]


def _slot_sum(g_ref, own_ref):
    if own_ref is not None:
        x, y, c = _position()
        mine = 4 * x + 2 * y + c
    acc = None
    for j in range(N_DEV):
        part = g_ref[j] if own_ref is None else jnp.where(mine == j, own_ref[...], g_ref[j])
        acc = part.astype(F32) if acc is None else acc + part.astype(F32)
    return acc


def _sum_slots(got, *, name, tr):
    _, rows, cols = got.shape
    tr = _tile(rows, tr, 16)

    def body(g_ref, o_ref):
        o_ref[...] = _slot_sum(g_ref, None)

    return pl.pallas_call(
        body,
        name=name,
        grid=(rows // tr,),
        in_specs=[pl.BlockSpec((N_DEV, tr, cols), lambda i: (0, i, 0))],
        out_specs=pl.BlockSpec((tr, cols), lambda i: (i, 0)),
        out_shape=jax.ShapeDtypeStruct((rows, cols), F32),
        compiler_params=pltpu.CompilerParams(dimension_semantics=("parallel",)),
    )(got)


def _adamw(wt, g, m, v, *, name, own=None):
    slots = own is not None
    shape = wt.shape
    two_d = (-1, shape[-1]) if wt.ndim > 1 else (1, -1)
    args = [a.reshape(two_d) for a in (wt, m, v)]
    rows, cols = args[0].shape
    if rows % 16 == 0:
        tr, tc = _tile(rows, 256, 16), cols
    else:
        tr, tc = rows, _tile(cols, 256 if rows > 64 else 512)
    args.insert(1, g.reshape((N_DEV, rows, cols) if slots else (rows, cols)))
    if slots:
        args.append(own.reshape(rows, cols))

    def body(w_ref, g_ref, m_ref, v_ref, *refs):
        go_ref, d_ref, nm_ref, nv_ref = refs[-4:]
        gv = _slot_sum(g_ref, refs[0]) if slots else g_ref[...]
        go_ref[...] = gv
        m2 = ADAM_B1 * m_ref[...] + (1.0 - ADAM_B1) * gv
        v2 = ADAM_B2 * v_ref[...] + (1.0 - ADAM_B2) * (gv * gv)
        m_hat = m2 / (1.0 - ADAM_B1 ** ADAM_STEP)
        v_hat = v2 / (1.0 - ADAM_B2 ** ADAM_STEP)
        d_ref[...] = -ADAM_LR * (m_hat / (jnp.sqrt(v_hat) + ADAM_EPS) + ADAM_WD * w_ref[...])
        nm_ref[...] = m2
        nv_ref[...] = v2

    spec = pl.BlockSpec((tr, tc), lambda i, j: (i, j))
    g_spec = pl.BlockSpec((N_DEV, tr, tc), lambda i, j: (0, i, j)) if slots else spec
    outs = pl.pallas_call(
        body,
        name=name,
        grid=(rows // tr, cols // tc),
        in_specs=[spec, g_spec, spec, spec] + ([spec] if slots else []),
        out_specs=[spec] * 4,
        out_shape=[jax.ShapeDtypeStruct((rows, cols), F32)] * 4,
        compiler_params=pltpu.CompilerParams(dimension_semantics=("parallel", "parallel")),
    )(*args)
    return tuple(o.reshape(shape) for o in outs)


WEIGHTS = ("g_mix", "w_in", "conv_w", "a_log", "dt_bias", "rel_bias", "w_onorm", "w_branch_a", "w_branch_b", "w_out", "g_ffn",
           "w_gate_up", "w_down", "g_ple", "w_ple_gate", "w_ple_proj", "g_final")


def kernel(x, p, g_mix, w_in, conv_w, a_log, dt_bias, rel_bias, w_onorm, w_branch_a, w_branch_b, w_out, g_ffn, w_gate_up, w_down, g_ple, w_ple_gate, w_ple_proj, g_final, loss_target, m_g_mix, m_w_in, m_conv_w, m_a_log, m_dt_bias, m_rel_bias, m_w_onorm, m_w_branch_a, m_w_branch_b, m_w_out, m_g_ffn, m_w_gate_up, m_w_down, m_g_ple, m_w_ple_gate, m_w_ple_proj, m_g_final, v_g_mix, v_w_in, v_conv_w, v_a_log, v_dt_bias, v_rel_bias, v_w_onorm, v_w_branch_a, v_w_branch_b, v_w_out, v_g_ffn, v_w_gate_up, v_w_down, v_g_ple, v_w_ple_gate, v_w_ple_proj, v_g_final):
    given = dict(g_mix=g_mix, w_in=w_in, conv_w=conv_w, a_log=a_log, dt_bias=dt_bias, rel_bias=rel_bias, w_onorm=w_onorm,
                 w_branch_a=w_branch_a, w_branch_b=w_branch_b, w_out=w_out, g_ffn=g_ffn, w_gate_up=w_gate_up, w_down=w_down,
                 g_ple=g_ple, w_ple_gate=w_ple_gate, w_ple_proj=w_ple_proj, g_final=g_final)
    mom1 = dict(g_mix=m_g_mix, w_in=m_w_in, conv_w=m_conv_w, a_log=m_a_log, dt_bias=m_dt_bias, rel_bias=m_rel_bias,
                w_onorm=m_w_onorm, w_branch_a=m_w_branch_a, w_branch_b=m_w_branch_b, w_out=m_w_out, g_ffn=m_g_ffn,
                w_gate_up=m_w_gate_up, w_down=m_w_down, g_ple=m_g_ple, w_ple_gate=m_w_ple_gate, w_ple_proj=m_w_ple_proj,
                g_final=m_g_final)
    mom2 = dict(g_mix=v_g_mix, w_in=v_w_in, conv_w=v_conv_w, a_log=v_a_log, dt_bias=v_dt_bias, rel_bias=v_rel_bias,
                w_onorm=v_w_onorm, w_branch_a=v_w_branch_a, w_branch_b=v_w_branch_b, w_out=v_w_out, g_ffn=v_g_ffn,
                w_gate_up=v_w_gate_up, w_down=v_w_down, g_ple=v_g_ple, w_ple_gate=v_w_ple_gate, w_ple_proj=v_w_ple_proj,
                g_final=v_g_final)
    mine = 4 * lax.axis_index("x") + 2 * lax.axis_index("y") + lax.axis_index("c")

    my_slot = (jnp.arange(N_DEV) == mine)[:, None, None]
    rest = MATRICES[1:]
    in_flight = {}

    got_in, got_taps = _weights_allgather([_held(w_in[0], 1).astype(BF16), conv_w[0]])
    in_flight["weights"], weights_sent = _exchange_start([_held(given[name][0], axis).astype(BF16) for name, axis in rest], False,
                                                         name="weights_start")
    small = dict(g_mix=g_mix + weights_sent[0, 0], g_ffn=g_ffn, g_ple=g_ple, g_final=g_final, w_onorm=w_onorm, a_log=a_log,
                 dt_bias=dt_bias, rel_bias=rel_bias, conv_w=jnp.transpose(got_taps, (1, 0, 2)).reshape(4, B_CONV))

    def rest_weights(after):
        shards, landed = _exchange_wait(in_flight.pop("weights"), after, False, name="weights_wait")
        return {name: _from_gathered(jnp.where(my_slot, shard[None], slabs)) for (name, _), shard, slabs in zip(rest, shards, landed)}

    def send_grads(gw):
        in_flight["grads"], sent = _exchange_start([_to_owner(gw[name]) for name, _ in rest], True, name="grads_start")
        return sent

    def send_w_in(g_in):
        in_flight["grad_in"], sent = _exchange_start([_to_owner(_unpermute_w_in(g_in))], True, name="grad_in_start")
        return sent

    loss_part, grad_x, gs = _local_step(x, p[0], loss_target, _permute_w_in(_from_gathered(got_in)), small, rest_weights,
                                        send_grads, send_w_in)
    gs["loss"] = loss_part

    updates = {}

    def update_matrices(matrices, own_slabs, landed):
        for (name, axis), own_slab, slots in zip(matrices, own_slabs, landed):
            mine_of = lax.dynamic_index_in_dim(own_slab, mine, axis=0, keepdims=False)
            w_held, m_held, v_held = (_held(a[name][0], axis) for a in (given, mom1, mom2))
            outs = _adamw(w_held, slots, m_held, v_held, name=f"adamw_{name}", own=mine_of)
            updates[name] = tuple(_held(o, axis)[None] for o in outs)

    update_matrices(rest, *_exchange_wait(in_flight["grads"], grad_x, True, name="grads_wait"))
    update_matrices(MATRICES[:1], *_exchange_wait(in_flight["grad_in"], updates[rest[-1][0]][0], True, name="grad_in_wait"))
    small_block, _ = lax.optimization_barrier((_pack_small(gs), updates["w_in"][0]))
    (got_small,) = _grads_exchange([jnp.broadcast_to(small_block, (N_DEV, SMALL_ROWS, D))])
    small_sum = _unpack_small(_sum_slots(got_small, name="sum_small_grads", tr=16))
    loss = small_sum.pop("loss")[0]
    conv_all = small_sum.pop("conv_w").reshape(4, N_DEV, TAPS_PER_SHARD)
    small_sum["conv_w"] = lax.dynamic_index_in_dim(conv_all, mine, axis=1, keepdims=False)
    for name, g in small_sum.items():
        updates[name] = _adamw(given[name], g.reshape(given[name].shape), mom1[name], mom2[name], name=f"adamw_{name}")
    return (loss, grad_x, *[updates[name][k] for k in range(4) for name in WEIGHTS])
```

```python
import jax
import jax.numpy as jnp
from jax import lax
from jax.experimental import pallas as pl
from jax.experimental.pallas import tpu as pltpu

F32 = jnp.float32
BF16 = jnp.bfloat16
DELTA_PREC = lax.Precision.HIGH
MESH = pl.DeviceIdType.MESH

N_DEV = 8
D = 1024
CHUNK = 64
EPS = 1e-6
A_HEADS, A_DIM, A_WIDTH = 8, 64, 512
A_BAND = 9 * CHUNK
A_PAD = 8 * CHUNK
REL_CLIP = 128
B_HEADS, B_DIM = 4, 128
B_CONV = 1536
D_FF = 2816
D_IN = 5640
P_CONV, P_Z, P_GATE, P_BD, P_END = 1536, 3072, 3584, 5632, 5760
LANE = 128

ADAM_LR, ADAM_B1, ADAM_B2, ADAM_EPS, ADAM_WD, ADAM_STEP = 0.001, 0.9, 0.999, 1e-08, 0.01, 10

NT = (((1,), (1,)), ((), ()))
TN = (((0,), (0,)), ((), ()))
NN = (((1,), (0,)), ((), ()))

HBM_SPEC = pl.BlockSpec(memory_space=pltpu.HBM)


def _tile(n, target, align=LANE):
    if n <= target:
        return n
    best = None
    for t in range(align, target + 1, align):
        if n % t == 0:
            best = t
    assert best is not None, (n, target, align)
    return best


def _mm(a, b, *, name, ta=False, tb=False, add=None, out_dtype=F32, tm=1024, tn=640, tk=None):
    assert not (ta and tb)
    if ta:
        k_dim, m_dim = a.shape
    else:
        m_dim, k_dim = a.shape
    n_dim = b.shape[0] if tb else b.shape[1]
    assert b.shape[1 if tb else 0] == k_dim
    tm, tn = _tile(m_dim, tm), _tile(n_dim, tn)
    tk = _tile(k_dim, tk or (4096 if ta else 1024), 8 if ta else LANE)
    nk = k_dim // tk
    dn = TN if ta else NT if tb else NN

    def body(*refs):
        if add is None:
            a_ref, b_ref, o_ref = refs[:3]
            add_ref = None
        else:
            a_ref, b_ref, add_ref, o_ref = refs[:4]
        part = lax.dot_general(a_ref[...].astype(BF16), b_ref[...].astype(BF16), dn, preferred_element_type=F32)

        def finish(r):
            if add_ref is not None:
                r = r + add_ref[...]
            o_ref[...] = r.astype(o_ref.dtype)

        if nk == 1:
            finish(part)
        else:
            acc_ref = refs[-1]
            k = pl.program_id(2)

            @pl.when(k == 0)
            def _():
                acc_ref[...] = part

            @pl.when(k > 0)
            def _():
                acc_ref[...] += part

            @pl.when(k == nk - 1)
            def _():
                finish(acc_ref[...])

    a_spec = pl.BlockSpec((tk, tm), lambda i, j, k: (k, i)) if ta else pl.BlockSpec((tm, tk), lambda i, j, k: (i, k))
    b_spec = pl.BlockSpec((tn, tk), lambda i, j, k: (j, k)) if tb else pl.BlockSpec((tk, tn), lambda i, j, k: (k, j))
    in_specs = [a_spec, b_spec]
    args = [a, b]
    if add is not None:
        in_specs.append(pl.BlockSpec((tm, tn), lambda i, j, k: (i, j)))
        args.append(add)
    return pl.pallas_call(
        body,
        name=name,
        grid=(m_dim // tm, n_dim // tn, nk),
        in_specs=in_specs,
        out_specs=pl.BlockSpec((tm, tn), lambda i, j, k: (i, j)),
        out_shape=jax.ShapeDtypeStruct((m_dim, n_dim), out_dtype),
        scratch_shapes=[pltpu.VMEM((tm, tn), F32)] if nk > 1 else [],
        compiler_params=pltpu.CompilerParams(dimension_semantics=("parallel", "parallel", "arbitrary")),
    )(*args)


def _rowwise(fn, rows, bcs, outs, reds=(), *, name, tr, ncol=1):
    n_rows = rows[0][0].shape[0]
    tr = _tile(n_rows, tr, 8)
    nrow = n_rows // tr
    n_in, n_out = len(rows) + len(bcs), len(outs)

    def body(*refs):
        j, i = pl.program_id(0), pl.program_id(1)
        o_vals, r_vals = fn(*[r[...].astype(F32) for r in refs[:n_in]])
        for ref, val in zip(refs[n_in:n_in + n_out], o_vals):
            ref[...] = val.astype(ref.dtype)
        for ref, val, (_, _, stride) in zip(refs[n_in + n_out:], r_vals, reds):
            first = (i == 0) if stride else jnp.logical_and(i == 0, j == 0)

            @pl.when(first)
            def _():
                ref[...] = val

            @pl.when(jnp.logical_not(first))
            def _():
                ref[...] += val

    def spec(r, w, off, st, row_dep=True):
        if row_dep:
            return pl.BlockSpec((r, w), lambda j, i: (i, off + st * j))
        return pl.BlockSpec((r, w), lambda j, i: (0, off + st * j))

    in_specs = [spec(tr, w, off, st) for (_, w, off, st) in rows]
    in_specs += [spec(a.shape[0], w, off, st, False) for (a, w, off, st) in bcs]
    out_specs = [spec(tr, w, off, st) for (_, _, w, off, st) in outs]
    out_specs += [spec(1, w, 0, st, False) for (_, w, st) in reds]
    out_shape = [jax.ShapeDtypeStruct((n_rows, c), dt) for (c, dt, _, _, _) in outs]
    out_shape += [jax.ShapeDtypeStruct((1, c), F32) for (c, _, _) in reds]
    return pl.pallas_call(
        body,
        name=name,
        grid=(ncol, nrow),
        in_specs=in_specs,
        out_specs=out_specs,
        out_shape=out_shape,
        compiler_params=pltpu.CompilerParams(dimension_semantics=("arbitrary", "arbitrary")),
    )(*[r[0] for r in rows], *[b[0] for b in bcs])


def _full(a):
    return (a, a.shape[1], 0, 0)


def _rms(x, g):
    return x * lax.rsqrt(jnp.mean(x * x, axis=-1, keepdims=True) + EPS) * g


def _silu(x):
    return x * jax.nn.sigmoid(x)


def _softplus(x):
    return jnp.maximum(x, 0.0) + jnp.log(1.0 + jnp.exp(-jnp.abs(x)))


def _rms_fwd(x, g, *, name):
    (h,) = _rowwise(lambda xb, gb: ([_rms(xb, gb)], []), [_full(x)], [_full(g)], [(D, BF16, D, 0, 0)], name=name, tr=512)
    return h


def _rms_bwd(x, g, dh, dres, *, name):
    def fn(xb, dhb, dresb, gb):
        _, vjp = jax.vjp(_rms, xb, gb)
        dx, dg = vjp(dhb)
        return [dx + dresb], [dg]

    return _rowwise(fn, [_full(x), _full(dh), _full(dres)], [_full(g)], [(D, F32, D, 0, 0)], [(D, D, 0)], name=name, tr=256)


def _gate_scalars(raw, al_row, dtb_row):
    lane = lax.broadcasted_iota(jnp.int32, raw.shape, 1)
    beta = jax.nn.sigmoid(raw)
    g = -jnp.exp(al_row) * _softplus(raw + dtb_row)
    return jnp.where(lane < B_HEADS, beta, jnp.where(lane < 2 * B_HEADS, g, 0.0))


def _gated_norm(o, z, w):
    return _rms(o, w) * _silu(z)


def _merge(ga, gb, ta, tb):
    return jax.nn.sigmoid(ga) * ta + jax.nn.sigmoid(gb) * tb


def _swiglu(gu):
    return _silu(gu[:, :D_FF]) * gu[:, D_FF:]


def _head_loss(x2, pg, pp, tgt, g):
    x3 = x2 + jax.nn.sigmoid(pg) * pp
    err = _rms(x3, g) - tgt
    return 0.5 * jnp.sum(jnp.mean(err * err, axis=-1))


CONV_W = 256


def _conv_taps(x, w):
    row = lax.broadcasted_iota(jnp.int32, x.shape, 0)
    shifted = [x] + [jnp.where(row >= s, pltpu.roll(x, s, 0), 0.0) for s in (1, 2, 3)]
    pre = shifted[0] * w[3:4]
    for s in (1, 2, 3):
        pre = pre + shifted[s] * w[3 - s:4 - s]
    return pre, shifted


def _conv_fwd(projp, conv_w, n_batch, seq):
    ncol = B_CONV // CONV_W
    first = P_CONV // CONV_W

    def body(x_ref, w_ref, o_ref):
        pre, _ = _conv_taps(x_ref[...].astype(F32), w_ref[...])
        o_ref[...] = _silu(pre)

    return pl.pallas_call(
        body,
        name="conv_fwd",
        grid=(ncol, n_batch),
        in_specs=[pl.BlockSpec((seq, CONV_W), lambda j, b: (b, first + j)), pl.BlockSpec((4, CONV_W), lambda j, b: (0, j))],
        out_specs=pl.BlockSpec((seq, CONV_W), lambda j, b: (b, j)),
        out_shape=jax.ShapeDtypeStruct((n_batch * seq, B_CONV), F32),
        compiler_params=pltpu.CompilerParams(dimension_semantics=("parallel", "parallel")),
    )(projp, conv_w)


def _conv_bwd(projp, conv_w, dc, n_batch, seq):
    width = dc.shape[1]
    ncol = width // CONV_W
    first_x = P_CONV // CONV_W

    def body(x_ref, w_ref, dc_ref, dx_ref, dw_ref):
        b = pl.program_id(1)
        w = w_ref[...]
        pre, shifted = _conv_taps(x_ref[...].astype(F32), w)
        sg = jax.nn.sigmoid(pre)
        dpre = dc_ref[...] * (sg * (1.0 + pre * (1.0 - sg)))
        row = lax.broadcasted_iota(jnp.int32, dpre.shape, 0)
        dx = dpre * w[3:4]
        for s in (1, 2, 3):
            dx = dx + jnp.where(row < seq - s, pltpu.roll(dpre, seq - s, 0), 0.0) * w[3 - s:4 - s]
        dx_ref[...] = dx.astype(dx_ref.dtype)
        for s in (0, 1, 2, 3):
            part = jnp.sum(dpre * shifted[s], axis=0, keepdims=True)

            @pl.when(b == 0)
            def _():
                dw_ref[3 - s:4 - s, :] = part

            @pl.when(b > 0)
            def _():
                dw_ref[3 - s:4 - s, :] += part

    return pl.pallas_call(
        body,
        name="conv_bwd",
        grid=(ncol, n_batch),
        in_specs=[
            pl.BlockSpec((seq, CONV_W), lambda j, b: (b, first_x + j)),
            pl.BlockSpec((4, CONV_W), lambda j, b: (0, j)),
            pl.BlockSpec((seq, CONV_W), lambda j, b: (b, j)),
        ],
        out_specs=[pl.BlockSpec((seq, CONV_W), lambda j, b: (b, j)), pl.BlockSpec((4, CONV_W), lambda j, b: (0, j))],
        out_shape=[jax.ShapeDtypeStruct((n_batch * seq, width), BF16), jax.ShapeDtypeStruct((4, width), F32)],
        compiler_params=pltpu.CompilerParams(dimension_semantics=("arbitrary", "arbitrary")),
    )(projp, conv_w, dc)


def _attn_chunk(qc, kb, vb, bias2, valid, lane_lo):
    sel = (lane_lo, jnp.logical_not(lane_lo))
    items = [(i, e) for i in range(len(qc)) for e in (0, 1)]
    k16, v16 = [t.astype(BF16) for t in kb], [t.astype(BF16) for t in vb]
    qm = [(jnp.where(sel[e], qc[i], 0.0) * (A_DIM ** -0.5)).astype(BF16) for i, e in items]
    s = [lax.dot_general(qm[n], k16[i], NT, preferred_element_type=F32) + bias2[e] for n, (i, e) in enumerate(items)]
    s = [jnp.where(valid[i], s[n], -1e30) for n, (i, e) in enumerate(items)]
    p = [jnp.exp(t - lax.stop_gradient(jnp.max(t, axis=-1, keepdims=True))) for t in s]
    p = [t * (1.0 / jnp.sum(t, axis=-1, keepdims=True)) for t in p]
    o = [jnp.where(sel[e], jnp.dot(p[n].astype(BF16), v16[i], preferred_element_type=F32), 0.0)
         for n, (i, e) in enumerate(items)]
    return [o[2 * i] + o[2 * i + 1] for i in range(len(qc))]


ATTN_GROUP_FWD, ATTN_GROUP_BWD = 4, 4


def _attn_group(g, group, q_ref, kp_ref, vp_ref):
    col = lax.broadcasted_iota(jnp.int32, (CHUNK, A_BAND), 1)
    lane_lo = lax.broadcasted_iota(jnp.int32, (1, LANE), 1) < A_DIM
    starts = [pl.multiple_of((g * group + i) * CHUNK, CHUNK) for i in range(group)]
    rows = [pl.ds(r0, CHUNK) for r0 in starts]
    bands = [pl.ds(r0, A_BAND) for r0 in starts]
    valid = [col + r0 >= A_PAD for r0 in starts]
    loaded = [q_ref[r, :].astype(F32) for r in rows], [kp_ref[b, :] for b in bands], [vp_ref[b, :] for b in bands]
    return rows, bands, loaded, valid, lane_lo


def _attn_specs(seq):
    def blk(first):
        return pl.BlockSpec((seq, LANE), lambda hp, b: (b, first + hp))

    return blk, pl.BlockSpec((2, CHUNK, A_BAND), lambda hp, b: (hp, 0, 0))


def _attn_fwd(projp, bias, n_batch, seq):
    nc = seq // CHUNK
    blk, bias_spec = _attn_specs(seq)

    def body(q_ref, k_ref, v_ref, b_ref, o_ref, kp_ref, vp_ref):
        kp_ref[0:A_PAD, :] = jnp.zeros((A_PAD, LANE), F32)
        vp_ref[0:A_PAD, :] = jnp.zeros((A_PAD, LANE), F32)
        kp_ref[A_PAD:, :] = k_ref[...].astype(F32)
        vp_ref[A_PAD:, :] = v_ref[...].astype(F32)
        bias2 = b_ref[...]

        def step(g, carry):
            rows, _, (qc, kb, vb), valid, lane_lo = _attn_group(g, ATTN_GROUP_FWD, q_ref, kp_ref, vp_ref)
            out = _attn_chunk(qc, kb, vb, bias2, valid, lane_lo)
            for r, o in zip(rows, out):
                o_ref[r, :] = o.astype(o_ref.dtype)
            return carry

        lax.fori_loop(0, nc // ATTN_GROUP_FWD, step, 0)

    return pl.pallas_call(
        body,
        name="attn_fwd",
        grid=(A_HEADS // 2, n_batch),
        in_specs=[blk(0), blk(4), blk(8), bias_spec],
        out_specs=pl.BlockSpec((seq, LANE), lambda hp, b: (b, hp)),
        out_shape=jax.ShapeDtypeStruct((n_batch * seq, A_WIDTH), BF16),
        scratch_shapes=[pltpu.VMEM((A_PAD + seq, LANE), F32), pltpu.VMEM((A_PAD + seq, LANE), F32)],
        compiler_params=pltpu.CompilerParams(dimension_semantics=("parallel", "parallel")),
    )(projp, projp, projp, bias)


def _attn_bwd(projp, bias, dy, n_batch, seq):
    nc = seq // CHUNK
    blk, bias_spec = _attn_specs(seq)
    out_blk = pl.BlockSpec((seq, LANE), lambda hp, b: (b, hp))

    def body(q_ref, k_ref, v_ref, b_ref, dy_ref, dq_ref, dk_ref, dv_ref, db_ref, kp_ref, vp_ref, dkp_ref, dvp_ref):
        b = pl.program_id(1)
        kp_ref[0:A_PAD, :] = jnp.zeros((A_PAD, LANE), F32)
        vp_ref[0:A_PAD, :] = jnp.zeros((A_PAD, LANE), F32)
        kp_ref[A_PAD:, :] = k_ref[...].astype(F32)
        vp_ref[A_PAD:, :] = v_ref[...].astype(F32)
        dkp_ref[...] = jnp.zeros_like(dkp_ref)
        dvp_ref[...] = jnp.zeros_like(dvp_ref)
        bias2 = b_ref[...]

        @pl.when(b == 0)
        def _():
            db_ref[...] = jnp.zeros_like(db_ref)

        def step(g, carry):
            rows, bands, (qc, kb, vb), valid, lane_lo = _attn_group(g, ATTN_GROUP_BWD, q_ref, kp_ref, vp_ref)
            _, vjp = jax.vjp(lambda q, k, v, bb: _attn_chunk(q, k, v, bb, valid, lane_lo), qc, kb, vb, bias2)
            dq, dk, dv, dbias = vjp([dy_ref[r, :] for r in rows])
            for i, r in enumerate(rows):
                dq_ref[r, :] = dq[i].astype(dq_ref.dtype)
            for i, band in enumerate(bands):
                dkp_ref[band, :] += dk[i]
                dvp_ref[band, :] += dv[i]
            db_ref[...] += dbias
            return carry

        lax.fori_loop(0, nc // ATTN_GROUP_BWD, step, 0)
        dk_ref[...] = dkp_ref[A_PAD:, :].astype(dk_ref.dtype)
        dv_ref[...] = dvp_ref[A_PAD:, :].astype(dv_ref.dtype)

    n_tok = n_batch * seq
    pad = pltpu.VMEM((A_PAD + seq, LANE), F32)
    return pl.pallas_call(
        body,
        name="attn_bwd",
        grid=(A_HEADS // 2, n_batch),
        in_specs=[blk(0), blk(4), blk(8), bias_spec, out_blk],
        out_specs=[out_blk, out_blk, out_blk, bias_spec],
        out_shape=[jax.ShapeDtypeStruct((n_tok, A_WIDTH), BF16)] * 3 + [jax.ShapeDtypeStruct((A_HEADS, CHUNK, A_BAND), F32)],
        scratch_shapes=[pad, pad, pad, pad],
        compiler_params=pltpu.CompilerParams(dimension_semantics=("arbitrary", "arbitrary")),
    )(projp, projp, projp, bias, dy)


def _rel_bias_table(rel_bias):
    span = CHUNK + A_BAND - 1
    near = REL_CLIP + CHUNK
    far = jnp.broadcast_to(rel_bias[:, 2 * REL_CLIP:], (A_HEADS, span - near))
    t = jnp.concatenate([rel_bias[:, 2 * REL_CLIP + 1 - near:], far], axis=1)
    u = jnp.concatenate([t[:, :A_BAND][:, ::-1], t[:, A_BAND:][:, ::-1]], axis=1)
    rolled = jnp.tile(u, (1, CHUNK))[:, :CHUNK * (span - 1)].reshape(A_HEADS, CHUNK, span - 1)
    return rolled[:, :, :A_BAND]


def _dot(a, b, dn=NN):
    return lax.dot_general(a, b, dn, precision=DELTA_PREC, preferred_element_type=F32)


def _dot16(a, b, dn=NN):
    return lax.dot_general(a.astype(BF16), b.astype(BF16), dn, preferred_element_type=F32)


def _each(fn, *lists):
    return [fn(*vals) for vals in zip(*lists)]


@jax.custom_vjp
def _saved_inverse(x, inv):
    return inv


def _saved_inverse_fwd(x, inv):
    return inv, inv


def _saved_inverse_bwd(inv, ct):
    return _dot(_dot(inv, ct, TN), inv, NT), jnp.zeros_like(inv)


_saved_inverse.defvjp(_saved_inverse_fwd, _saved_inverse_bwd)


def _delta_chunk(r_state, cq, ck, cv, beta, g, saved_inv=None):
    ii = lax.broadcasted_iota(jnp.int32, (CHUNK, CHUNK), 0)
    jj = lax.broadcasted_iota(jnp.int32, (CHUNK, CHUNK), 1)
    incl, strict, eye = ii >= jj, ii > jj, ii == jj
    q = _each(lambda t: t * lax.rsqrt(jnp.sum(t * t, axis=-1, keepdims=True) + EPS) * (B_DIM ** -0.5), cq)
    k = _each(lambda t: t * lax.rsqrt(jnp.sum(t * t, axis=-1, keepdims=True) + EPS), ck)
    g_b = _each(lambda t: jnp.broadcast_to(t, (CHUNK, CHUNK)), g)
    g_row = _each(lambda t: jnp.sum(jnp.where(eye, t, 0.0), axis=0, keepdims=True), g_b)
    gc_col = _each(lambda t: jnp.sum(jnp.where(incl, t, 0.0), axis=1, keepdims=True), g_row)
    gc_row = _each(lambda t: jnp.sum(jnp.where(ii <= jj, t, 0.0), axis=0, keepdims=True), g_b)
    decay = _each(lambda c, r: jnp.where(incl, jnp.exp(jnp.where(incl, c - r, 0.0)), 0.0), gc_col, gc_row)
    kk = _each(lambda t: _dot(t, t, NT), k)
    x = _each(lambda b, m, d: jnp.where(strict, -(b * m * d), 0.0), beta, kk, decay)
    if saved_inv is None:
        inv = _each(lambda t: jnp.where(eye, 1.0, 0.0) + t, x)
        pw = x
        for _ in range(5):
            pw = _each(lambda t: _dot(t, t), pw)
            inv = _each(lambda t, s: t + _dot(t, s), inv, pw)
    else:
        inv = _each(_saved_inverse, x, saved_inv)
    egc = _each(jnp.exp, gc_col)
    u = _each(lambda t, b, v: _dot16(t, b * v), inv, beta, cv)
    wk = _each(lambda t, b, e, kh: _dot16(t, (b * e) * kh), inv, beta, egc, k)
    pqk = _each(lambda qh, kh, d: _dot16(qh, kh, NT) * d, q, k, decay)
    g_last = _each(lambda c: c[CHUNK - 1:CHUNK, :], gc_col)
    kdec = _each(lambda kh, gl, c: kh * jnp.exp(gl - c), k, g_last, gc_col)
    w = _each(lambda uh, wkh, r: uh - _dot16(wkh, r), u, wk, r_state)
    o = _each(lambda e, qh, r, ph, wh: e * _dot16(qh, r) + _dot16(ph, wh), egc, q, r_state, pqk, w)
    r_new = _each(lambda gl, r, kd, wh: jnp.exp(gl) * r + _dot16(kd, wh, TN), g_last, r_state, kdec, w)
    return o, r_new, inv


DELTA_BLK = 512


def _delta_blocks(n_batch, seq):
    nblk = seq // DELTA_BLK
    cpb = DELTA_BLK // CHUNK

    def rows(width, order):
        return pl.BlockSpec((DELTA_BLK, width), lambda b, i: (b * nblk + order(i), 0))

    def states(order, side):
        return pl.BlockSpec((cpb, B_HEADS, side, side), lambda b, i: (b * nblk + order(i), 0, 0, 0))

    return nblk, cpb, rows, states


def _head_cols(h):
    return [pl.ds(part * B_HEADS * B_DIM + h * B_DIM, B_DIM) for part in range(3)]


def _load_heads(c_ref, bg_ref, state_ref, rows):
    bg_c = bg_ref[rows, :]
    cols = [_head_cols(h) for h in range(B_HEADS)]
    return ([state_ref[h] for h in range(B_HEADS)], [c_ref[rows, c[0]] for c in cols], [c_ref[rows, c[1]] for c in cols],
            [c_ref[rows, c[2]] for c in cols], [bg_c[:, h:h + 1] for h in range(B_HEADS)],
            [bg_c[:, B_HEADS + h:B_HEADS + h + 1] for h in range(B_HEADS)])


def _delta_fwd(conv, bg, n_batch, seq):
    nblk, cpb, rows_spec, states_spec = _delta_blocks(n_batch, seq)

    def forward(i):
        return i

    def body(c_ref, bg_ref, o_ref, st_ref, inv_ref, r_ref):
        @pl.when(pl.program_id(1) == 0)
        def _():
            r_ref[...] = jnp.zeros_like(r_ref)

        def step(c, carry):
            rows = pl.ds(pl.multiple_of(c * CHUNK, CHUNK), CHUNK)
            args = _load_heads(c_ref, bg_ref, r_ref, rows)
            o, r_new, inv = _delta_chunk(*args)
            for h in range(B_HEADS):
                st_ref[c, h] = args[0][h]
                inv_ref[c, h] = inv[h]
                o_ref[rows, pl.ds(h * B_DIM, B_DIM)] = o[h]
            for h in range(B_HEADS):
                r_ref[h] = r_new[h]
            return carry

        lax.fori_loop(0, cpb, step, 0)

    n_tok = n_batch * seq
    return pl.pallas_call(
        body,
        name="delta_fwd",
        grid=(n_batch, nblk),
        in_specs=[rows_spec(B_CONV, forward), rows_spec(LANE, forward)],
        out_specs=[rows_spec(B_HEADS * B_DIM, forward), states_spec(forward, B_DIM), states_spec(forward, CHUNK)],
        out_shape=[jax.ShapeDtypeStruct((n_tok, B_HEADS * B_DIM), F32),
                   jax.ShapeDtypeStruct((n_tok // CHUNK, B_HEADS, B_DIM, B_DIM), F32),
                   jax.ShapeDtypeStruct((n_tok // CHUNK, B_HEADS, CHUNK, CHUNK), F32)],
        scratch_shapes=[pltpu.VMEM((B_HEADS, B_DIM, B_DIM), F32)],
        compiler_params=pltpu.CompilerParams(dimension_semantics=("arbitrary", "arbitrary")),
    )(conv, bg)


def _delta_bwd(conv, bg, states, inverses, do, n_batch, seq):
    nblk, cpb, rows_spec, states_spec = _delta_blocks(n_batch, seq)

    def backward(i):
        return nblk - 1 - i

    def body(c_ref, bg_ref, st_ref, inv_ref, do_ref, dc_ref, dbg_ref, dr_ref):
        @pl.when(pl.program_id(1) == 0)
        def _():
            dr_ref[...] = jnp.zeros_like(dr_ref)

        def step(n, carry):
            c = cpb - 1 - n
            rows = pl.ds(pl.multiple_of(c * CHUNK, CHUNK), CHUNK)
            saved = [inv_ref[c, h] for h in range(B_HEADS)]
            _, vjp = jax.vjp(lambda *args: _delta_chunk(*args, saved_inv=saved)[:2],
                             *_load_heads(c_ref, bg_ref, st_ref.at[c], rows))
            do = [do_ref[rows, pl.ds(h * B_DIM, B_DIM)] for h in range(B_HEADS)]
            dr, dq, dk, dv, dbeta, dg = vjp((do, [dr_ref[h] for h in range(B_HEADS)]))
            lane = lax.broadcasted_iota(jnp.int32, (CHUNK, LANE), 1)
            dbg = jnp.zeros((CHUNK, LANE), F32)
            for h in range(B_HEADS):
                cq, ck, cv = _head_cols(h)
                dr_ref[h] = dr[h]
                dc_ref[rows, cq] = dq[h]
                dc_ref[rows, ck] = dk[h]
                dc_ref[rows, cv] = dv[h]
                dbg = dbg + jnp.where(lane == h, dbeta[h], 0.0) + jnp.where(lane == h + B_HEADS, dg[h], 0.0)
            dbg_ref[rows, :] = dbg
            return carry

        lax.fori_loop(0, cpb, step, 0)

    n_tok = n_batch * seq
    return pl.pallas_call(
        body,
        name="delta_bwd",
        grid=(n_batch, nblk),
        in_specs=[rows_spec(B_CONV, backward), rows_spec(LANE, backward), states_spec(backward, B_DIM),
                  states_spec(backward, CHUNK), rows_spec(B_HEADS * B_DIM, backward)],
        out_specs=[rows_spec(B_CONV, backward), rows_spec(LANE, backward)],
        out_shape=[jax.ShapeDtypeStruct((n_tok, B_CONV), F32), jax.ShapeDtypeStruct((n_tok, LANE), F32)],
        scratch_shapes=[pltpu.VMEM((B_HEADS, B_DIM, B_DIM), F32)],
        compiler_params=pltpu.CompilerParams(dimension_semantics=("arbitrary", "arbitrary")),
    )(conv, bg, states, inverses, do)


def _lane_row(vec4, first):
    return jnp.concatenate([jnp.zeros((1, first), F32), vec4.reshape(1, B_HEADS).astype(F32),
                            jnp.zeros((1, LANE - first - B_HEADS), F32)], axis=1)


def _local_step(x3d, p3d, tgt3d, w_in, small, rest_weights, send_grads, send_w_in):
    n_batch, seq, _ = x3d.shape
    n_tok = n_batch * seq
    x, p, tgt = x3d.reshape(n_tok, D), p3d.reshape(n_tok, -1), tgt3d.reshape(n_tok, D)
    g_mix, g_ffn, g_ple, g_final = (small[k].reshape(1, D) for k in ("g_mix", "g_ffn", "g_ple", "g_final"))
    w_onorm = small["w_onorm"].reshape(1, B_DIM)
    al_row = _lane_row(small["a_log"], B_HEADS)
    dtb_row = _lane_row(small["dt_bias"], B_HEADS)
    rel_bias = small["rel_bias"].reshape(A_HEADS, -1)
    bias = _rel_bias_table(rel_bias)
    conv_w = small["conv_w"].reshape(4, B_CONV)

    h1 = _rms_fwd(x, g_mix, name="rms_mix")
    projp = _mm(h1, w_in, tb=True, out_dtype=BF16, name="mm_proj", tn=1920)
    bd = _mm(h1, w_in[P_BD:], tb=True, name="mm_beta_decay", tn=LANE)
    y_a = _attn_fwd(projp, bias, n_batch, seq)
    conv = _conv_fwd(projp, conv_w, n_batch, seq)
    (bg,) = _rowwise(lambda raw, al, dtb: ([_gate_scalars(raw, al, dtb)], []), [_full(bd)],
                     [_full(al_row), _full(dtb_row)], [(LANE, F32, LANE, 0, 0)], name="gate_scalars", tr=1024)
    o_b, states, inverses = _delta_fwd(conv, bg, n_batch, seq)
    (y_b,) = _rowwise(lambda o, z, wn: ([_gated_norm(o, z, wn)], []), [(o_b, LANE, 0, 1), (projp, LANE, P_Z // LANE, 1)],
                      [_full(w_onorm)], [(B_HEADS * B_DIM, BF16, LANE, 0, 1)], name="gated_norm", tr=1024, ncol=B_HEADS)
    w = rest_weights(y_b)
    t_a = _mm(y_a, w["w_branch_a"], tb=True, name="mm_branch_a", tn=1024)
    t_b = _mm(y_b, w["w_branch_b"], tb=True, name="mm_branch_b", tn=1024)
    half = D // 2
    gate_rows = [(projp, half, P_GATE // half, 1), (projp, half, P_GATE // half + 2, 1), (t_a, half, 0, 1), (t_b, half, 0, 1)]
    (merged,) = _rowwise(lambda ga, gb, ta, tb: ([_merge(ga, gb, ta, tb)], []), gate_rows, [], [(D, BF16, half, 0, 1)],
                         name="merge", tr=512, ncol=2)
    x1 = _mm(merged, w["w_out"], add=x, name="mm_out", tn=1024)
    h2 = _rms_fwd(x1, g_ffn, name="rms_ffn")
    gu = _mm(h2, w["w_gate_up"], tb=True, out_dtype=BF16, name="mm_gate_up", tn=2816)
    (act,) = _rowwise(lambda gub: ([_swiglu(gub)], []), [_full(gu)], [], [(D_FF, BF16, D_FF, 0, 0)], name="swiglu",
                      tr=512)
    x2 = _mm(act, w["w_down"], add=x1, name="mm_down", tm=512, tn=1024, tk=D_FF)
    h3 = _rms_fwd(x2, g_ple, name="rms_ple")
    pg = _mm(h3, w["w_ple_gate"], name="mm_ple_gate", tn=1024)
    pp = _mm(p, w["w_ple_proj"], tb=True, name="mm_ple_proj", tn=1024)

    def head_fn(x2b, pgb, ppb, tb, gb):
        loss, (dx2, dpg, dpp, dg) = jax.value_and_grad(_head_loss, argnums=(0, 1, 2, 4))(x2b, pgb, ppb, tb, gb)
        return [dx2, dpg, dpp], [dg, jnp.full((1, LANE), loss, F32)]

    dx3, dpg, dpp, dg_final, loss_row = _rowwise(
        head_fn, [_full(x2), _full(pg), _full(pp), _full(tgt)], [_full(g_final)],
        [(D, F32, D, 0, 0), (D, BF16, D, 0, 0), (D, BF16, D, 0, 0)], [(D, D, 0), (LANE, LANE, 0)], name="loss_head", tr=256)
    gw = {}
    gw["w_ple_proj"] = _mm(dpp, p, ta=True, out_dtype=BF16, name="mm_d_ple_proj", tn=256)
    gw["w_ple_gate"] = _mm(h3, dpg, ta=True, out_dtype=BF16, name="mm_d_ple_gate", tn=512)
    dh3 = _mm(dpg, w["w_ple_gate"], tb=True, name="mm_dh3", tn=1024)
    dx2, dg_ple = _rms_bwd(x2, g_ple, dh3, dx3, name="rms_ple_bwd")
    gw["w_down"] = _mm(act, dx2, ta=True, out_dtype=BF16, name="mm_d_down", tm=1408, tn=256)
    dact = _mm(dx2, w["w_down"], tb=True, out_dtype=BF16, name="mm_dact", tn=D_FF)

    def swiglu_bwd(gub, dab):
        _, vjp = jax.vjp(_swiglu, gub)
        return [vjp(dab)[0]], []

    (dgu,) = _rowwise(swiglu_bwd, [_full(gu), _full(dact)], [], [(2 * D_FF, BF16, 2 * D_FF, 0, 0)], name="swiglu_bwd", tr=256)
    gw["w_gate_up"] = _mm(dgu, h2, ta=True, out_dtype=BF16, name="mm_d_gate_up", tm=512, tn=1024)
    dh2 = _mm(dgu, w["w_gate_up"], name="mm_dh2", tm=512, tn=1024, tk=2 * D_FF)
    dx1, dg_ffn = _rms_bwd(x1, g_ffn, dh2, dx2, name="rms_ffn_bwd")
    gw["w_out"] = _mm(merged, dx1, ta=True, out_dtype=BF16, name="mm_d_out", tn=512)
    dmerged = _mm(dx1, w["w_out"], tb=True, name="mm_dmerged", tn=1024)

    def merge_bwd(ga, gb, ta, tb, dm):
        _, vjp = jax.vjp(_merge, ga, gb, ta, tb)
        return list(vjp(dm)), []

    dga, dgb, dta, dtb = _rowwise(merge_bwd, gate_rows + [(dmerged, half, 0, 1)], [], [(D, BF16, half, 0, 1)] * 4,
                                  name="merge_bwd", tr=512, ncol=2)
    gw["w_branch_a"] = _mm(dta, y_a, ta=True, out_dtype=BF16, name="mm_d_branch_a", tn=512)
    gw["w_branch_b"] = _mm(dtb, y_b, ta=True, out_dtype=BF16, name="mm_d_branch_b", tn=512)
    dya = _mm(dta, w["w_branch_a"], name="mm_dya", tn=512)
    dyb = _mm(dtb, w["w_branch_b"], name="mm_dyb", tn=512)

    w_onorm = w_onorm + send_grads(gw)[0, 0]

    def gated_norm_bwd(o, z, dy, wn):
        _, vjp = jax.vjp(_gated_norm, o, z, wn)
        do, dz, dwn = vjp(dy)
        return [do, dz], [dwn]

    do_b, dz, dw_onorm = _rowwise(
        gated_norm_bwd, [(o_b, LANE, 0, 1), (projp, LANE, P_Z // LANE, 1), (dyb, LANE, 0, 1)], [_full(w_onorm)],
        [(B_HEADS * B_DIM, F32, LANE, 0, 1), (B_HEADS * B_DIM, BF16, LANE, 0, 1)], [(B_DIM, B_DIM, 0)],
        name="gated_norm_bwd", tr=1024, ncol=B_HEADS)
    dconv_out, dbg = _delta_bwd(conv, bg, states, inverses, do_b, n_batch, seq)

    def gate_scalars_bwd(raw, dbgb, al, dtb):
        _, vjp = jax.vjp(_gate_scalars, raw, al, dtb)
        draw, dal, ddtb = vjp(dbgb)
        return [draw], [dal, ddtb]

    dbd, dal_row, ddtb_row = _rowwise(gate_scalars_bwd, [_full(bd), _full(dbg)], [_full(al_row), _full(dtb_row)],
                                      [(LANE, BF16, LANE, 0, 0)], [(LANE, LANE, 0), (LANE, LANE, 0)], name="gate_scalars_bwd",
                                      tr=1024)
    dconv, dconv_w = _conv_bwd(projp, conv_w, dconv_out, n_batch, seq)
    dq_a, dk_a, dv_a, dbias = _attn_bwd(projp, bias, dya, n_batch, seq)
    dprojp = jnp.concatenate([dq_a, dk_a, dv_a, dconv, dz, dga, dgb, dbd], axis=1)
    sent = send_w_in(_mm(dprojp, h1, ta=True, out_dtype=BF16, name="mm_d_in", tm=640, tn=1024))
    sent, dprojp = lax.optimization_barrier((sent, dprojp))
    dh1 = _mm(dprojp, w_in, name="mm_dh1", tm=512, tn=1024, tk=P_END)
    grad_x, dg_mix = _rms_bwd(x, g_mix + sent[0, 0], dh1, dx1, name="rms_mix_bwd")

    _, bias_vjp = jax.vjp(_rel_bias_table, rel_bias)
    gs = {
        "g_mix": dg_mix, "g_ffn": dg_ffn, "g_ple": dg_ple, "g_final": dg_final, "w_onorm": dw_onorm,
        "conv_w": dconv_w, "rel_bias": bias_vjp(dbias)[0],
        "a_log": dal_row[0, B_HEADS:2 * B_HEADS], "dt_bias": ddtb_row[0, B_HEADS:2 * B_HEADS],
    }
    return loss_row[:, :1], grad_x.reshape(n_batch, seq, D), gs


MATRICES = (("w_in", 1), ("w_gate_up", 1), ("w_branch_a", 1), ("w_branch_b", 1), ("w_out", 0), ("w_down", 0),
            ("w_ple_gate", 0), ("w_ple_proj", 1))
TAPS_PER_SHARD = B_CONV // N_DEV


def _held(shard, axis):
    return shard if axis == 0 else shard.T


def _from_gathered(slabs):
    return slabs.reshape(-1, slabs.shape[-1])


def _to_owner(held):
    return held.reshape(N_DEV, held.shape[0] // N_DEV, held.shape[1])


def _permute_w_in(held):
    n_gate = P_BD - P_GATE
    row = lax.broadcasted_iota(jnp.int32, (P_END, 1), 0)
    same = jnp.pad(held, ((0, P_END - D_IN), (0, 0)))
    up = jnp.pad(held[8:], ((0, P_END - D_IN + 8), (0, 0)))
    down = jnp.pad(held[:P_GATE + 8], ((n_gate, P_END - P_BD - 8), (0, 0)))
    zero = jnp.zeros((), held.dtype)
    return jnp.where(row < P_GATE, same, jnp.where(row < P_BD, up, jnp.where(row < P_BD + 8, down, zero)))


def _unpermute_w_in(gp):
    n_gate = P_BD - P_GATE
    row = lax.broadcasted_iota(jnp.int32, (D_IN, 1), 0)
    same = gp[:D_IN]
    up = jnp.pad(gp[n_gate:], ((0, D_IN - (P_END - n_gate)), (0, 0)))
    down = jnp.pad(gp[:P_BD], ((8, 0), (0, 0)))
    return jnp.where(row < P_GATE, same, jnp.where(row < P_GATE + 8, up, down))


SMALL_ROWS = 16
SMALL_LAYOUT = (("g_mix", 0, D), ("g_ffn", 1, D), ("g_ple", 2, D), ("g_final", 3, D), ("conv_w", 4, 4 * B_CONV),
                ("rel_bias", 10, A_HEADS * (2 * REL_CLIP + 1)), ("w_onorm", 13, B_DIM), ("a_log", 14, B_HEADS),
                ("dt_bias", 14, B_HEADS), ("loss", 15, 1))


def _pack_small(gs):
    rows = {}
    for name, row, n in SMALL_LAYOUT:
        rows.setdefault(row, []).append(gs[name].reshape(-1).astype(F32))
    parts = []
    for row in sorted(rows):
        flat = jnp.concatenate(rows[row])
        parts.append(jnp.concatenate([flat, jnp.zeros((-flat.shape[0] % D,), F32)]))
    flat = jnp.concatenate(parts)
    assert flat.shape[0] == SMALL_ROWS * D, flat.shape
    return flat.reshape(SMALL_ROWS, D)


def _unpack_small(blk):
    flat, out, used = blk.reshape(-1), {}, {}
    for name, row, n in SMALL_LAYOUT:
        start = row * D + used.get(row, 0)
        out[name] = flat[start:start + n]
        used[row] = used.get(row, 0) + n
    return out


def _position():
    return lax.axis_index("x"), lax.axis_index("y"), lax.axis_index("c")


PEERS = N_DEV - 1


def _comm_call(body, arrays, out_shapes, *, name):
    n = len(arrays)
    return pl.pallas_call(
        body,
        name=name,
        out_shape=out_shapes,
        in_specs=[HBM_SPEC] * n,
        out_specs=[HBM_SPEC] * n,
        scratch_shapes=[pltpu.SemaphoreType.DMA((PEERS * n,)), pltpu.SemaphoreType.DMA((PEERS * n,)),
                        pltpu.SemaphoreType.DMA((n,))],
    )(*arrays)


def _weights_allgather(shards):
    n = len(shards)

    def body(*refs):
        ins, outs = refs[:n], refs[n:2 * n]
        send_sems, recv_sems, local_sems = refs[2 * n:]
        x, y, c = _position()
        me, sibling = (x, y, c), (x, y, 1 - c)
        chips = [(1 - x, y), (x, 1 - y), (1 - x, 1 - y)]

        def slab(a, px, py, pc):
            return outs[a].at[4 * px + 2 * py + pc]

        def copy(a, k, block, to, src=None):
            return pltpu.make_async_remote_copy(src_ref=slab(a, *block) if src is None else src, dst_ref=slab(a, *block),
                                                send_sem=send_sems.at[PEERS * a + k], recv_sem=recv_sems.at[PEERS * a + k],
                                                device_id=to, device_id_type=MESH)

        local = [pltpu.make_async_copy(ins[a], slab(a, *me), local_sems.at[a]) for a in range(n)]
        sent = [copy(a, 1 + j, me, (*chip, c), src=ins[a]) for a in range(n) for j, chip in enumerate(chips)]
        sent += [copy(a, 0, me, sibling, src=ins[a]) for a in range(n)]
        for cp in sent + local:
            cp.start()
        for a in range(n):
            for j, chip in enumerate(chips):
                copy(a, 1 + j, (*chip, c), me).wait_recv()
                passed = copy(a, 4 + j, (*chip, c), sibling)
                passed.start()
                sent.append(passed)
        for a in range(n):
            copy(a, 0, sibling, me).wait_recv()
            for j, chip in enumerate(chips):
                copy(a, 4 + j, (*chip, 1 - c), me).wait_recv()
        for cp in sent:
            cp.wait_send()
        for cp in local:
            cp.wait()

    return _comm_call(body, shards, [jax.ShapeDtypeStruct((N_DEV,) + s.shape, s.dtype) for s in shards],
                      name="weights_allgather")


def _grads_exchange(by_owner):
    n = len(by_owner)

    def body(*refs):
        ins, outs = refs[:n], refs[n:2 * n]
        send_sems, recv_sems, local_sems = refs[2 * n:]
        x, y, c = _position()
        mine = 4 * x + 2 * y + c
        local = [pltpu.make_async_copy(ins[a].at[mine], outs[a].at[mine], local_sems.at[a]) for a in range(n)]
        for cp in local:
            cp.start()
        flips = [(dx, dy, dc) for dx in (0, 1) for dy in (0, 1) for dc in (0, 1) if dx + dy + dc]
        pending = []
        for k, (dx, dy, dc) in enumerate(flips):
            px, py, pc = (1 - x if dx else x), (1 - y if dy else y), (1 - c if dc else c)
            peer = 4 * px + 2 * py + pc
            for a in range(n):
                def remote(slot):
                    return pltpu.make_async_remote_copy(src_ref=ins[a].at[peer], dst_ref=outs[a].at[slot],
                                                        send_sem=send_sems.at[PEERS * a + k], recv_sem=recv_sems.at[PEERS * a + k],
                                                        device_id=(px, py, pc), device_id_type=MESH)

                sent = remote(mine)
                sent.start()
                pending.append((sent, remote(peer)))
        for sent, landed in pending:
            landed.wait_recv()
            sent.wait_send()
        for cp in local:
            cp.wait()

    return _comm_call(body, by_owner, [jax.ShapeDtypeStruct(g.shape, g.dtype) for g in by_owner], name="grads_exchange")


SEM_SPEC = pl.BlockSpec(memory_space=pltpu.SEMAPHORE)
DATAFLOW = pltpu.SideEffectType.DATAFLOW_SIDE_EFFECTING


def _peer_copies(srcs, lands, send_sems, recv_sems, by_owner, arrival):
    x, y, c = _position()
    mine = 4 * x + 2 * y + c
    copies = []
    for k, (dx, dy, dc) in enumerate([(dx, dy, dc) for dx in (0, 1) for dy in (0, 1) for dc in (0, 1) if dx + dy + dc]):
        px, py, pc = (1 - x if dx else x), (1 - y if dy else y), (1 - c if dc else c)
        peer = 4 * px + 2 * py + pc
        for a, (src, land) in enumerate(zip(srcs, lands)):
            copies.append(pltpu.make_async_remote_copy(
                src_ref=src.at[peer] if by_owner else src, dst_ref=land.at[peer if arrival else mine],
                send_sem=send_sems.at[PEERS * a + k], recv_sem=recv_sems.at[PEERS * a + k],
                device_id=(px, py, pc), device_id_type=MESH))
    return copies


def _exchange_start(sources, by_owner, *, name):
    n = len(sources)
    lands = [lax.empty((N_DEV,) + (s.shape[1:] if by_owner else s.shape), s.dtype) for s in sources]

    def body(*refs):
        send_sems, recv_sems, token = refs[2 * n], refs[2 * n + 1], refs[-1]
        for copy in _peer_copies(refs[:n], refs[n:2 * n], send_sems, recv_sems, by_owner, arrival=False):
            copy.start()
        token[...] = jnp.zeros_like(token)

    sems = pltpu.SemaphoreType.DMA((PEERS * n,))
    outs = pl.pallas_call(
        body,
        name=name,
        out_shape=(sems, sems, *[pltpu.HBM(a.shape, a.dtype) for a in sources + lands], jax.ShapeDtypeStruct((8, LANE), F32)),
        in_specs=[HBM_SPEC] * (2 * n),
        out_specs=(SEM_SPEC, SEM_SPEC, *[HBM_SPEC] * (2 * n), pl.BlockSpec(memory_space=pltpu.VMEM)),
        input_output_aliases={i: 2 + i for i in range(2 * n)},
        compiler_params=pltpu.CompilerParams(has_side_effects=DATAFLOW),
    )(*[pltpu.with_memory_space_constraint(a, pltpu.HBM) for a in sources + lands])
    return outs[:-1], outs[-1]


def _exchange_wait(started, after, by_owner, *, name):
    send_sems, recv_sems, *arrays = started
    n = len(arrays) // 2

    def body(*refs):
        for copy in _peer_copies(refs[:n], refs[n:2 * n], refs[2 * n], refs[2 * n + 1], by_owner, arrival=True):
            copy.wait_send()
            copy.wait_recv()

    outs = pl.pallas_call(
        body,
        name=name,
        out_shape=[pltpu.HBM(a.shape, a.dtype) for a in arrays],
        in_specs=[HBM_SPEC] * (2 * n) + [SEM_SPEC, SEM_SPEC, pl.BlockSpec(memory_space=pl.ANY)],
        out_specs=[HBM_SPEC] * (2 * n),
        input_output_aliases={i: i for i in range(2 * n)},
        compiler_params=pltpu.CompilerParams(has_side_effects=DATAFLOW),
    )(*arrays, send_sems, recv_sems, after)
    return outs[:n], outs[n:]


def _slot_sum(g_ref, own_ref):
    if own_ref is not None:
        x, y, c = _position()
        mine = 4 * x + 2 * y + c
    acc = None
    for j in range(N_DEV):
        part = g_ref[j] if own_ref is None else jnp.where(mine == j, own_ref[...], g_ref[j])
        acc = part.astype(F32) if acc is None else acc + part.astype(F32)
    return acc


def _sum_slots(got, *, name, tr):
    _, rows, cols = got.shape
    tr = _tile(rows, tr, 16)

    def body(g_ref, o_ref):
        o_ref[...] = _slot_sum(g_ref, None)

    return pl.pallas_call(
        body,
        name=name,
        grid=(rows // tr,),
        in_specs=[pl.BlockSpec((N_DEV, tr, cols), lambda i: (0, i, 0))],
        out_specs=pl.BlockSpec((tr, cols), lambda i: (i, 0)),
        out_shape=jax.ShapeDtypeStruct((rows, cols), F32),
        compiler_params=pltpu.CompilerParams(dimension_semantics=("parallel",)),
    )(got)


def _adamw(wt, g, m, v, *, name, own=None):
    slots = own is not None
    shape = wt.shape
    two_d = (-1, shape[-1]) if wt.ndim > 1 else (1, -1)
    args = [a.reshape(two_d) for a in (wt, m, v)]
    rows, cols = args[0].shape
    if rows % 16 == 0:
        tr, tc = _tile(rows, 256, 16), cols
    else:
        tr, tc = rows, _tile(cols, 256 if rows > 64 else 512)
    args.insert(1, g.reshape((N_DEV, rows, cols) if slots else (rows, cols)))
    if slots:
        args.append(own.reshape(rows, cols))

    def body(w_ref, g_ref, m_ref, v_ref, *refs):
        go_ref, d_ref, nm_ref, nv_ref = refs[-4:]
        gv = _slot_sum(g_ref, refs[0]) if slots else g_ref[...]
        go_ref[...] = gv
        m2 = ADAM_B1 * m_ref[...] + (1.0 - ADAM_B1) * gv
        v2 = ADAM_B2 * v_ref[...] + (1.0 - ADAM_B2) * (gv * gv)
        m_hat = m2 / (1.0 - ADAM_B1 ** ADAM_STEP)
        v_hat = v2 / (1.0 - ADAM_B2 ** ADAM_STEP)
        d_ref[...] = -ADAM_LR * (m_hat / (jnp.sqrt(v_hat) + ADAM_EPS) + ADAM_WD * w_ref[...])
        nm_ref[...] = m2
        nv_ref[...] = v2

    spec = pl.BlockSpec((tr, tc), lambda i, j: (i, j))
    g_spec = pl.BlockSpec((N_DEV, tr, tc), lambda i, j: (0, i, j)) if slots else spec
    outs = pl.pallas_call(
        body,
        name=name,
        grid=(rows // tr, cols // tc),
        in_specs=[spec, g_spec, spec, spec] + ([spec] if slots else []),
        out_specs=[spec] * 4,
        out_shape=[jax.ShapeDtypeStruct((rows, cols), F32)] * 4,
        compiler_params=pltpu.CompilerParams(dimension_semantics=("parallel", "parallel")),
    )(*args)
    return tuple(o.reshape(shape) for o in outs)


WEIGHTS = ("g_mix", "w_in", "conv_w", "a_log", "dt_bias", "rel_bias", "w_onorm", "w_branch_a", "w_branch_b", "w_out", "g_ffn",
           "w_gate_up", "w_down", "g_ple", "w_ple_gate", "w_ple_proj", "g_final")


def kernel(x, p, g_mix, w_in, conv_w, a_log, dt_bias, rel_bias, w_onorm, w_branch_a, w_branch_b, w_out, g_ffn, w_gate_up, w_down, g_ple, w_ple_gate, w_ple_proj, g_final, loss_target, m_g_mix, m_w_in, m_conv_w, m_a_log, m_dt_bias, m_rel_bias, m_w_onorm, m_w_branch_a, m_w_branch_b, m_w_out, m_g_ffn, m_w_gate_up, m_w_down, m_g_ple, m_w_ple_gate, m_w_ple_proj, m_g_final, v_g_mix, v_w_in, v_conv_w, v_a_log, v_dt_bias, v_rel_bias, v_w_onorm, v_w_branch_a, v_w_branch_b, v_w_out, v_g_ffn, v_w_gate_up, v_w_down, v_g_ple, v_w_ple_gate, v_w_ple_proj, v_g_final):
    given = dict(g_mix=g_mix, w_in=w_in, conv_w=conv_w, a_log=a_log, dt_bias=dt_bias, rel_bias=rel_bias, w_onorm=w_onorm,
                 w_branch_a=w_branch_a, w_branch_b=w_branch_b, w_out=w_out, g_ffn=g_ffn, w_gate_up=w_gate_up, w_down=w_down,
                 g_ple=g_ple, w_ple_gate=w_ple_gate, w_ple_proj=w_ple_proj, g_final=g_final)
    mom1 = dict(g_mix=m_g_mix, w_in=m_w_in, conv_w=m_conv_w, a_log=m_a_log, dt_bias=m_dt_bias, rel_bias=m_rel_bias,
                w_onorm=m_w_onorm, w_branch_a=m_w_branch_a, w_branch_b=m_w_branch_b, w_out=m_w_out, g_ffn=m_g_ffn,
                w_gate_up=m_w_gate_up, w_down=m_w_down, g_ple=m_g_ple, w_ple_gate=m_w_ple_gate, w_ple_proj=m_w_ple_proj,
                g_final=m_g_final)
    mom2 = dict(g_mix=v_g_mix, w_in=v_w_in, conv_w=v_conv_w, a_log=v_a_log, dt_bias=v_dt_bias, rel_bias=v_rel_bias,
                w_onorm=v_w_onorm, w_branch_a=v_w_branch_a, w_branch_b=v_w_branch_b, w_out=v_w_out, g_ffn=v_g_ffn,
                w_gate_up=v_w_gate_up, w_down=v_w_down, g_ple=v_g_ple, w_ple_gate=v_w_ple_gate, w_ple_proj=v_w_ple_proj,
                g_final=v_g_final)
    mine = 4 * lax.axis_index("x") + 2 * lax.axis_index("y") + lax.axis_index("c")

    my_slot = (jnp.arange(N_DEV) == mine)[:, None, None]
    rest = MATRICES[1:]
    in_flight = {}

    got_in, got_taps = _weights_allgather([_held(w_in[0], 1).astype(BF16), conv_w[0]])
    in_flight["weights"], weights_sent = _exchange_start([_held(given[name][0], axis).astype(BF16) for name, axis in rest], False,
                                                         name="weights_start")
    small = dict(g_mix=g_mix + weights_sent[0, 0], g_ffn=g_ffn, g_ple=g_ple, g_final=g_final, w_onorm=w_onorm, a_log=a_log,
                 dt_bias=dt_bias, rel_bias=rel_bias, conv_w=jnp.transpose(got_taps, (1, 0, 2)).reshape(4, B_CONV))

    def rest_weights(after):
        shards, landed = _exchange_wait(in_flight.pop("weights"), after, False, name="weights_wait")
        return {name: _from_gathered(jnp.where(my_slot, shard[None], slabs)) for (name, _), shard, slabs in zip(rest, shards, landed)}

    def send_grads(gw):
        in_flight["grads"], sent = _exchange_start([_to_owner(gw[name]) for name, _ in rest], True, name="grads_start")
        return sent

    def send_w_in(g_in):
        in_flight["grad_in"], sent = _exchange_start([_to_owner(_unpermute_w_in(g_in))], True, name="grad_in_start")
        return sent

    loss_part, grad_x, gs = _local_step(x, p[0], loss_target, _permute_w_in(_from_gathered(got_in)), small, rest_weights,
                                        send_grads, send_w_in)
    gs["loss"] = loss_part

    updates = {}

    def update_matrices(matrices, own_slabs, landed):
        for (name, axis), own_slab, slots in zip(matrices, own_slabs, landed):
            mine_of = lax.dynamic_index_in_dim(own_slab, mine, axis=0, keepdims=False)
            w_held, m_held, v_held = (_held(a[name][0], axis) for a in (given, mom1, mom2))
            outs = _adamw(w_held, slots, m_held, v_held, name=f"adamw_{name}", own=mine_of)
            updates[name] = tuple(_held(o, axis)[None] for o in outs)

    update_matrices(rest, *_exchange_wait(in_flight["grads"], grad_x, True, name="grads_wait"))
    update_matrices(MATRICES[:1], *_exchange_wait(in_flight["grad_in"], updates[rest[-1][0]][0], True, name="grad_in_wait"))
    small_block, _ = lax.optimization_barrier((_pack_small(gs), updates["w_in"][0]))
    (got_small,) = _grads_exchange([jnp.broadcast_to(small_block, (N_DEV, SMALL_ROWS, D))])
    small_sum = _unpack_small(_sum_slots(got_small, name="sum_small_grads", tr=16))
    loss = small_sum.pop("loss")[0]
    conv_all = small_sum.pop("conv_w").reshape(4, N_DEV, TAPS_PER_SHARD)
    small_sum["conv_w"] = lax.dynamic_index_in_dim(conv_all, mine, axis=1, keepdims=False)
    for name, g in small_sum.items():
        updates[name] = _adamw(given[name], g.reshape(given[name].shape), mom1[name], mom2[name], name=f"adamw_{name}")
    return (loss, grad_x, *[updates[name][k] for k in range(4) for name in WEIGHTS])
```

```python
import jax
import jax.numpy as jnp
from jax import lax
from jax.experimental import pallas as pl
from jax.experimental.pallas import tpu as pltpu

F32 = jnp.float32
BF16 = jnp.bfloat16
DELTA_PREC = lax.Precision.HIGH
MESH = pl.DeviceIdType.MESH

N_DEV = 8
D = 1024
CHUNK = 64
EPS = 1e-6
A_HEADS, A_DIM, A_WIDTH = 8, 64, 512
A_BAND = 9 * CHUNK
A_PAD = 8 * CHUNK
REL_CLIP = 128
B_HEADS, B_DIM = 4, 128
B_CONV = 1536
D_FF = 2816
D_IN = 5640
P_CONV, P_Z, P_GATE, P_BD, P_END = 1536, 3072, 3584, 5632, 5760
LANE = 128

ADAM_LR, ADAM_B1, ADAM_B2, ADAM_EPS, ADAM_WD, ADAM_STEP = 0.001, 0.9, 0.999, 1e-08, 0.01, 10

NT = (((1,), (1,)), ((), ()))
TN = (((0,), (0,)), ((), ()))
NN = (((1,), (0,)), ((), ()))

HBM_SPEC = pl.BlockSpec(memory_space=pltpu.HBM)


def _tile(n, target, align=LANE):
    if n <= target:
        return n
    best = None
    for t in range(align, target + 1, align):
        if n % t == 0:
            best = t
    assert best is not None, (n, target, align)
    return best


def _mm(a, b, *, name, ta=False, tb=False, out_dtype=F32, tm=1024, tn=640, tk=None, epilogue=None, rows=(), bcs=(), outs=(),
        n_red=0):
    assert not (ta and tb)
    if ta:
        k_dim, m_dim = a.shape
    else:
        m_dim, k_dim = a.shape
    n_dim = b.shape[0] if tb else b.shape[1]
    assert b.shape[1 if tb else 0] == k_dim
    tm, tn = _tile(m_dim, tm), _tile(n_dim, tn)
    tk = _tile(k_dim, tk or (4096 if ta else 1024), 8 if ta else LANE)
    nk = k_dim // tk
    dn = TN if ta else NT if tb else NN
    if epilogue is None:
        outs = (out_dtype,)
    assert not n_red or tn == n_dim
    n_extra, n_out = len(rows) + len(bcs), len(outs)

    def body(a_ref, b_ref, *refs):
        part = lax.dot_general(a_ref[...].astype(BF16), b_ref[...].astype(BF16), dn, preferred_element_type=F32)

        def finish(r):
            o_vals, r_vals = ([r], []) if epilogue is None else epilogue(r, *[x[...] for x in refs[:n_extra]])
            for ref, val in zip(refs[n_extra:n_extra + n_out], o_vals):
                ref[...] = val.astype(ref.dtype)
            first = pl.program_id(0) == 0
            for ref, val in zip(refs[n_extra + n_out:n_extra + n_out + n_red], r_vals):
                @pl.when(first)
                def _():
                    ref[...] = val

                @pl.when(jnp.logical_not(first))
                def _():
                    ref[...] += val

        if nk == 1:
            finish(part)
        else:
            acc_ref = refs[-1]
            k = pl.program_id(2)

            @pl.when(k == 0)
            def _():
                acc_ref[...] = part

            @pl.when(k > 0)
            def _():
                acc_ref[...] += part

            @pl.when(k == nk - 1)
            def _():
                finish(acc_ref[...])

    tile = pl.BlockSpec((tm, tn), lambda i, j, k: (i, j))
    col = pl.BlockSpec((1, tn), lambda i, j, k: (0, j))
    a_spec = pl.BlockSpec((tk, tm), lambda i, j, k: (k, i)) if ta else pl.BlockSpec((tm, tk), lambda i, j, k: (i, k))
    b_spec = pl.BlockSpec((tn, tk), lambda i, j, k: (j, k)) if tb else pl.BlockSpec((tk, tn), lambda i, j, k: (k, j))
    result = pl.pallas_call(
        body,
        name=name,
        grid=(m_dim // tm, n_dim // tn, nk),
        in_specs=[a_spec, b_spec] + [tile] * len(rows) + [col] * len(bcs),
        out_specs=[tile] * n_out + [col] * n_red,
        out_shape=[jax.ShapeDtypeStruct((m_dim, n_dim), dt) for dt in outs] + [jax.ShapeDtypeStruct((1, n_dim), F32)] * n_red,
        scratch_shapes=[pltpu.VMEM((tm, tn), F32)] if nk > 1 else [],
        compiler_params=pltpu.CompilerParams(dimension_semantics=("arbitrary",) * 3 if n_red else ("parallel", "parallel", "arbitrary")),
    )(a, b, *rows, *bcs)
    return result[0] if epilogue is None else result


def _rowwise(fn, rows, bcs, outs, reds=(), *, name, tr, ncol=1):
    n_rows = rows[0][0].shape[0]
    tr = _tile(n_rows, tr, 8)
    nrow = n_rows // tr
    n_in, n_out = len(rows) + len(bcs), len(outs)

    def body(*refs):
        j, i = pl.program_id(0), pl.program_id(1)
        o_vals, r_vals = fn(*[r[...].astype(F32) for r in refs[:n_in]])
        for ref, val in zip(refs[n_in:n_in + n_out], o_vals):
            ref[...] = val.astype(ref.dtype)
        for ref, val, (_, _, stride) in zip(refs[n_in + n_out:], r_vals, reds):
            first = (i == 0) if stride else jnp.logical_and(i == 0, j == 0)

            @pl.when(first)
            def _():
                ref[...] = val

            @pl.when(jnp.logical_not(first))
            def _():
                ref[...] += val

    def spec(r, w, off, st, row_dep=True):
        if row_dep:
            return pl.BlockSpec((r, w), lambda j, i: (i, off + st * j))
        return pl.BlockSpec((r, w), lambda j, i: (0, off + st * j))

    in_specs = [spec(tr, w, off, st) for (_, w, off, st) in rows]
    in_specs += [spec(a.shape[0], w, off, st, False) for (a, w, off, st) in bcs]
    out_specs = [spec(tr, w, off, st) for (_, _, w, off, st) in outs]
    out_specs += [spec(1, w, 0, st, False) for (_, w, st) in reds]
    out_shape = [jax.ShapeDtypeStruct((n_rows, c), dt) for (c, dt, _, _, _) in outs]
    out_shape += [jax.ShapeDtypeStruct((1, c), F32) for (c, _, _) in reds]
    return pl.pallas_call(
        body,
        name=name,
        grid=(ncol, nrow),
        in_specs=in_specs,
        out_specs=out_specs,
        out_shape=out_shape,
        compiler_params=pltpu.CompilerParams(dimension_semantics=("arbitrary", "arbitrary")),
    )(*[r[0] for r in rows], *[b[0] for b in bcs])


def _full(a):
    return (a, a.shape[1], 0, 0)


def _rms(x, g):
    return x * lax.rsqrt(jnp.mean(x * x, axis=-1, keepdims=True) + EPS) * g


def _silu(x):
    return x * jax.nn.sigmoid(x)


def _softplus(x):
    return jnp.maximum(x, 0.0) + jnp.log(1.0 + jnp.exp(-jnp.abs(x)))


def _rms_fwd(x, g, *, name):
    (h,) = _rowwise(lambda xb, gb: ([_rms(xb, gb)], []), [_full(x)], [_full(g)], [(D, BF16, D, 0, 0)], name=name, tr=512)
    return h


def _residual_rms(r, x, g):
    x_new = x + r
    return [x_new, _rms(x_new, g)], []


def _rms_bwd(dh, x, dres, g):
    _, vjp = jax.vjp(_rms, x, g)
    dx, dg = vjp(dh)
    return [dx + dres], [dg]


def _gate_scalars(raw, al_row, dtb_row):
    lane = lax.broadcasted_iota(jnp.int32, raw.shape, 1)
    beta = jax.nn.sigmoid(raw)
    g = -jnp.exp(al_row) * _softplus(raw + dtb_row)
    return jnp.where(lane < B_HEADS, beta, jnp.where(lane < 2 * B_HEADS, g, 0.0))


def _gated_norm(o, z, w):
    return _rms(o, w) * _silu(z)


def _merge(ga, gb, ta, tb):
    return jax.nn.sigmoid(ga) * ta + jax.nn.sigmoid(gb) * tb


def _swiglu(gu):
    return _silu(gu[:, :D_FF]) * gu[:, D_FF:]


def _head_loss(x2, pg, pp, tgt, g):
    x3 = x2 + jax.nn.sigmoid(pg) * pp
    err = _rms(x3, g) - tgt
    return 0.5 * jnp.sum(jnp.mean(err * err, axis=-1))


CONV_W = 256


def _conv_taps(x, w):
    row = lax.broadcasted_iota(jnp.int32, x.shape, 0)
    shifted = [x] + [jnp.where(row >= s, pltpu.roll(x, s, 0), 0.0) for s in (1, 2, 3)]
    pre = shifted[0] * w[3:4]
    for s in (1, 2, 3):
        pre = pre + shifted[s] * w[3 - s:4 - s]
    return pre, shifted


def _conv_fwd(projp, conv_w, n_batch, seq):
    ncol = B_CONV // CONV_W
    first = P_CONV // CONV_W

    def body(x_ref, w_ref, o_ref):
        pre, _ = _conv_taps(x_ref[...].astype(F32), w_ref[...])
        o_ref[...] = _silu(pre)

    return pl.pallas_call(
        body,
        name="conv_fwd",
        grid=(ncol, n_batch),
        in_specs=[pl.BlockSpec((seq, CONV_W), lambda j, b: (b, first + j)), pl.BlockSpec((4, CONV_W), lambda j, b: (0, j))],
        out_specs=pl.BlockSpec((seq, CONV_W), lambda j, b: (b, j)),
        out_shape=jax.ShapeDtypeStruct((n_batch * seq, B_CONV), F32),
        compiler_params=pltpu.CompilerParams(dimension_semantics=("parallel", "parallel")),
    )(projp, conv_w)


def _conv_bwd(projp, conv_w, dc, n_batch, seq):
    width = dc.shape[1]
    ncol = width // CONV_W
    first_x = P_CONV // CONV_W

    def body(x_ref, w_ref, dc_ref, dx_ref, dw_ref):
        b = pl.program_id(1)
        w = w_ref[...]
        pre, shifted = _conv_taps(x_ref[...].astype(F32), w)
        sg = jax.nn.sigmoid(pre)
        dpre = dc_ref[...] * (sg * (1.0 + pre * (1.0 - sg)))
        row = lax.broadcasted_iota(jnp.int32, dpre.shape, 0)
        dx = dpre * w[3:4]
        for s in (1, 2, 3):
            dx = dx + jnp.where(row < seq - s, pltpu.roll(dpre, seq - s, 0), 0.0) * w[3 - s:4 - s]
        dx_ref[...] = dx.astype(dx_ref.dtype)
        for s in (0, 1, 2, 3):
            part = jnp.sum(dpre * shifted[s], axis=0, keepdims=True)

            @pl.when(b == 0)
            def _():
                dw_ref[3 - s:4 - s, :] = part

            @pl.when(b > 0)
            def _():
                dw_ref[3 - s:4 - s, :] += part

    return pl.pallas_call(
        body,
        name="conv_bwd",
        grid=(ncol, n_batch),
        in_specs=[
            pl.BlockSpec((seq, CONV_W), lambda j, b: (b, first_x + j)),
            pl.BlockSpec((4, CONV_W), lambda j, b: (0, j)),
            pl.BlockSpec((seq, CONV_W), lambda j, b: (b, j)),
        ],
        out_specs=[pl.BlockSpec((seq, CONV_W), lambda j, b: (b, j)), pl.BlockSpec((4, CONV_W), lambda j, b: (0, j))],
        out_shape=[jax.ShapeDtypeStruct((n_batch * seq, width), BF16), jax.ShapeDtypeStruct((4, width), F32)],
        compiler_params=pltpu.CompilerParams(dimension_semantics=("arbitrary", "arbitrary")),
    )(projp, conv_w, dc)


def _attn_chunk(qc, kb, vb, bias2, valid, lane_lo):
    sel = (lane_lo, jnp.logical_not(lane_lo))
    items = [(i, e) for i in range(len(qc)) for e in (0, 1)]
    k16, v16 = [t.astype(BF16) for t in kb], [t.astype(BF16) for t in vb]
    qm = [(jnp.where(sel[e], qc[i], 0.0) * (A_DIM ** -0.5)).astype(BF16) for i, e in items]
    s = [lax.dot_general(qm[n], k16[i], NT, preferred_element_type=F32) + bias2[e] for n, (i, e) in enumerate(items)]
    if valid is not None:
        s = [jnp.where(valid[i], s[n], -1e30) for n, (i, e) in enumerate(items)]
    p = [jnp.exp(t - lax.stop_gradient(jnp.max(t, axis=-1, keepdims=True))) for t in s]
    p = [t * (1.0 / jnp.sum(t, axis=-1, keepdims=True)) for t in p]
    o = [jnp.where(sel[e], jnp.dot(p[n].astype(BF16), v16[i], preferred_element_type=F32), 0.0)
         for n, (i, e) in enumerate(items)]
    return [o[2 * i] + o[2 * i + 1] for i in range(len(qc))]


ATTN_GROUP_FWD, ATTN_GROUP_BWD = 4, 4


def _attn_loops(step, n_groups, n_masked):
    lax.fori_loop(0, n_masked, lambda g, c: step(g, c, True), 0)
    lax.fori_loop(n_masked, n_groups, lambda g, c: step(g, c, False), 0)


def _attn_group(g, group, q_ref, kp_ref, vp_ref):
    col = lax.broadcasted_iota(jnp.int32, (CHUNK, A_BAND), 1)
    lane_lo = lax.broadcasted_iota(jnp.int32, (1, LANE), 1) < A_DIM
    starts = [pl.multiple_of((g * group + i) * CHUNK, CHUNK) for i in range(group)]
    rows = [pl.ds(r0, CHUNK) for r0 in starts]
    bands = [pl.ds(r0, A_BAND) for r0 in starts]
    valid = [col + r0 >= A_PAD for r0 in starts]
    loaded = [q_ref[r, :].astype(F32) for r in rows], [kp_ref[b, :] for b in bands], [vp_ref[b, :] for b in bands]
    return rows, bands, loaded, valid, lane_lo


def _attn_specs(seq):
    def blk(first):
        return pl.BlockSpec((seq, LANE), lambda hp, b: (b, first + hp))

    return blk, pl.BlockSpec((2, CHUNK, A_BAND), lambda hp, b: (hp, 0, 0))


def _attn_fwd(projp, bias, n_batch, seq):
    nc = seq // CHUNK
    blk, bias_spec = _attn_specs(seq)

    def body(q_ref, k_ref, v_ref, b_ref, o_ref, kp_ref, vp_ref):
        kp_ref[0:A_PAD, :] = jnp.zeros((A_PAD, LANE), F32)
        vp_ref[0:A_PAD, :] = jnp.zeros((A_PAD, LANE), F32)
        kp_ref[A_PAD:, :] = k_ref[...].astype(F32)
        vp_ref[A_PAD:, :] = v_ref[...].astype(F32)
        bias2 = b_ref[...]

        def step(g, carry, masked):
            rows, _, (qc, kb, vb), valid, lane_lo = _attn_group(g, ATTN_GROUP_FWD, q_ref, kp_ref, vp_ref)
            out = _attn_chunk(qc, kb, vb, bias2, valid if masked else None, lane_lo)
            for r, o in zip(rows, out):
                o_ref[r, :] = o.astype(o_ref.dtype)
            return carry

        _attn_loops(step, nc // ATTN_GROUP_FWD, A_PAD // (CHUNK * ATTN_GROUP_FWD))

    return pl.pallas_call(
        body,
        name="attn_fwd",
        grid=(A_HEADS // 2, n_batch),
        in_specs=[blk(0), blk(4), blk(8), bias_spec],
        out_specs=pl.BlockSpec((seq, LANE), lambda hp, b: (b, hp)),
        out_shape=jax.ShapeDtypeStruct((n_batch * seq, A_WIDTH), BF16),
        scratch_shapes=[pltpu.VMEM((A_PAD + seq, LANE), F32), pltpu.VMEM((A_PAD + seq, LANE), F32)],
        compiler_params=pltpu.CompilerParams(dimension_semantics=("parallel", "parallel")),
    )(projp, projp, projp, bias)


def _attn_bwd(projp, bias, dy, n_batch, seq):
    nc = seq // CHUNK
    blk, bias_spec = _attn_specs(seq)
    out_blk = pl.BlockSpec((seq, LANE), lambda hp, b: (b, hp))

    def body(q_ref, k_ref, v_ref, b_ref, dy_ref, dq_ref, dk_ref, dv_ref, db_ref, kp_ref, vp_ref, dkp_ref, dvp_ref):
        b = pl.program_id(1)
        kp_ref[0:A_PAD, :] = jnp.zeros((A_PAD, LANE), F32)
        vp_ref[0:A_PAD, :] = jnp.zeros((A_PAD, LANE), F32)
        kp_ref[A_PAD:, :] = k_ref[...].astype(F32)
        vp_ref[A_PAD:, :] = v_ref[...].astype(F32)
        dkp_ref[...] = jnp.zeros_like(dkp_ref)
        dvp_ref[...] = jnp.zeros_like(dvp_ref)
        bias2 = b_ref[...]

        @pl.when(b == 0)
        def _():
            db_ref[...] = jnp.zeros_like(db_ref)

        def step(g, carry, masked):
            rows, bands, (qc, kb, vb), valid, lane_lo = _attn_group(g, ATTN_GROUP_BWD, q_ref, kp_ref, vp_ref)
            _, vjp = jax.vjp(lambda q, k, v, bb: _attn_chunk(q, k, v, bb, valid if masked else None, lane_lo), qc, kb, vb, bias2)
            dq, dk, dv, dbias = vjp([dy_ref[r, :] for r in rows])
            for i, r in enumerate(rows):
                dq_ref[r, :] = dq[i].astype(dq_ref.dtype)
            for i, band in enumerate(bands):
                dkp_ref[band, :] += dk[i]
                dvp_ref[band, :] += dv[i]
            db_ref[...] += dbias
            return carry

        _attn_loops(step, nc // ATTN_GROUP_BWD, A_PAD // (CHUNK * ATTN_GROUP_BWD))
        dk_ref[...] = dkp_ref[A_PAD:, :].astype(dk_ref.dtype)
        dv_ref[...] = dvp_ref[A_PAD:, :].astype(dv_ref.dtype)

    n_tok = n_batch * seq
    pad = pltpu.VMEM((A_PAD + seq, LANE), F32)
    return pl.pallas_call(
        body,
        name="attn_bwd",
        grid=(A_HEADS // 2, n_batch),
        in_specs=[blk(0), blk(4), blk(8), bias_spec, out_blk],
        out_specs=[out_blk, out_blk, out_blk, bias_spec],
        out_shape=[jax.ShapeDtypeStruct((n_tok, A_WIDTH), BF16)] * 3 + [jax.ShapeDtypeStruct((A_HEADS, CHUNK, A_BAND), F32)],
        scratch_shapes=[pad, pad, pad, pad],
        compiler_params=pltpu.CompilerParams(dimension_semantics=("arbitrary", "arbitrary")),
    )(projp, projp, projp, bias, dy)


def _rel_bias_table(rel_bias):
    span = CHUNK + A_BAND - 1
    near = REL_CLIP + CHUNK
    far = jnp.broadcast_to(rel_bias[:, 2 * REL_CLIP:], (A_HEADS, span - near))
    t = jnp.concatenate([rel_bias[:, 2 * REL_CLIP + 1 - near:], far], axis=1)
    u = jnp.concatenate([t[:, :A_BAND][:, ::-1], t[:, A_BAND:][:, ::-1]], axis=1)
    rolled = jnp.tile(u, (1, CHUNK))[:, :CHUNK * (span - 1)].reshape(A_HEADS, CHUNK, span - 1)
    return rolled[:, :, :A_BAND]


def _dot(a, b, dn=NN):
    return lax.dot_general(a, b, dn, precision=DELTA_PREC, preferred_element_type=F32)


def _dot16(a, b, dn=NN):
    return lax.dot_general(a.astype(BF16), b.astype(BF16), dn, preferred_element_type=F32)


def _each(fn, *lists):
    return [fn(*vals) for vals in zip(*lists)]


@jax.custom_vjp
def _saved_inverse(x, inv):
    return inv


def _saved_inverse_fwd(x, inv):
    return inv, inv


def _saved_inverse_bwd(inv, ct):
    return _dot(_dot(inv, ct, TN), inv, NT), jnp.zeros_like(inv)


_saved_inverse.defvjp(_saved_inverse_fwd, _saved_inverse_bwd)


def _delta_chunk(r_state, cq, ck, cv, beta, g, saved_inv=None):
    ii = lax.broadcasted_iota(jnp.int32, (CHUNK, CHUNK), 0)
    jj = lax.broadcasted_iota(jnp.int32, (CHUNK, CHUNK), 1)
    incl, strict, eye = ii >= jj, ii > jj, ii == jj
    q = _each(lambda t: t * lax.rsqrt(jnp.sum(t * t, axis=-1, keepdims=True) + EPS) * (B_DIM ** -0.5), cq)
    k = _each(lambda t: t * lax.rsqrt(jnp.sum(t * t, axis=-1, keepdims=True) + EPS), ck)
    g_b = _each(lambda t: jnp.broadcast_to(t, (CHUNK, CHUNK)), g)
    g_row = _each(lambda t: jnp.sum(jnp.where(eye, t, 0.0), axis=0, keepdims=True), g_b)
    gc_col = _each(lambda t: jnp.sum(jnp.where(incl, t, 0.0), axis=1, keepdims=True), g_row)
    gc_row = _each(lambda t: jnp.sum(jnp.where(ii <= jj, t, 0.0), axis=0, keepdims=True), g_b)
    decay = _each(lambda c, r: jnp.where(incl, jnp.exp(jnp.where(incl, c - r, 0.0)), 0.0), gc_col, gc_row)
    kk = _each(lambda t: _dot(t, t, NT), k)
    x = _each(lambda b, m, d: jnp.where(strict, -(b * m * d), 0.0), beta, kk, decay)
    if saved_inv is None:
        inv = _each(lambda t: jnp.where(eye, 1.0, 0.0) + t, x)
        pw = x
        for _ in range(5):
            pw = _each(lambda t: _dot(t, t), pw)
            inv = _each(lambda t, s: t + _dot(t, s), inv, pw)
    else:
        inv = _each(_saved_inverse, x, saved_inv)
    egc = _each(jnp.exp, gc_col)
    u = _each(lambda t, b, v: _dot16(t, b * v), inv, beta, cv)
    wk = _each(lambda t, b, e, kh: _dot16(t, (b * e) * kh), inv, beta, egc, k)
    pqk = _each(lambda qh, kh, d: _dot16(qh, kh, NT) * d, q, k, decay)
    g_last = _each(lambda c: c[CHUNK - 1:CHUNK, :], gc_col)
    kdec = _each(lambda kh, gl, c: kh * jnp.exp(gl - c), k, g_last, gc_col)
    w = _each(lambda uh, wkh, r: uh - _dot16(wkh, r), u, wk, r_state)
    o = _each(lambda e, qh, r, ph, wh: e * _dot16(qh, r) + _dot16(ph, wh), egc, q, r_state, pqk, w)
    r_new = _each(lambda gl, r, kd, wh: jnp.exp(gl) * r + _dot16(kd, wh, TN), g_last, r_state, kdec, w)
    return o, r_new, inv


DELTA_BLK = 512


def _delta_blocks(n_batch, seq):
    nblk = seq // DELTA_BLK
    cpb = DELTA_BLK // CHUNK

    def rows(width, order):
        return pl.BlockSpec((DELTA_BLK, width), lambda b, i: (b * nblk + order(i), 0))

    def states(order, side):
        return pl.BlockSpec((cpb, B_HEADS, side, side), lambda b, i: (b * nblk + order(i), 0, 0, 0))

    return nblk, cpb, rows, states


def _head_cols(h):
    return [pl.ds(part * B_HEADS * B_DIM + h * B_DIM, B_DIM) for part in range(3)]


def _load_heads(c_ref, bg_ref, state_ref, rows):
    bg_c = bg_ref[rows, :]
    cols = [_head_cols(h) for h in range(B_HEADS)]
    return ([state_ref[h] for h in range(B_HEADS)], [c_ref[rows, c[0]] for c in cols], [c_ref[rows, c[1]] for c in cols],
            [c_ref[rows, c[2]] for c in cols], [bg_c[:, h:h + 1] for h in range(B_HEADS)],
            [bg_c[:, B_HEADS + h:B_HEADS + h + 1] for h in range(B_HEADS)])


def _delta_fwd(conv, bg, n_batch, seq):
    nblk, cpb, rows_spec, states_spec = _delta_blocks(n_batch, seq)

    def forward(i):
        return i

    def body(c_ref, bg_ref, o_ref, st_ref, inv_ref, r_ref):
        @pl.when(pl.program_id(1) == 0)
        def _():
            r_ref[...] = jnp.zeros_like(r_ref)

        def step(c, carry):
            rows = pl.ds(pl.multiple_of(c * CHUNK, CHUNK), CHUNK)
            args = _load_heads(c_ref, bg_ref, r_ref, rows)
            o, r_new, inv = _delta_chunk(*args)
            for h in range(B_HEADS):
                st_ref[c, h] = args[0][h]
                inv_ref[c, h] = inv[h]
                o_ref[rows, pl.ds(h * B_DIM, B_DIM)] = o[h]
            for h in range(B_HEADS):
                r_ref[h] = r_new[h]
            return carry

        lax.fori_loop(0, cpb, step, 0)

    n_tok = n_batch * seq
    return pl.pallas_call(
        body,
        name="delta_fwd",
        grid=(n_batch, nblk),
        in_specs=[rows_spec(B_CONV, forward), rows_spec(LANE, forward)],
        out_specs=[rows_spec(B_HEADS * B_DIM, forward), states_spec(forward, B_DIM), states_spec(forward, CHUNK)],
        out_shape=[jax.ShapeDtypeStruct((n_tok, B_HEADS * B_DIM), F32),
                   jax.ShapeDtypeStruct((n_tok // CHUNK, B_HEADS, B_DIM, B_DIM), F32),
                   jax.ShapeDtypeStruct((n_tok // CHUNK, B_HEADS, CHUNK, CHUNK), F32)],
        scratch_shapes=[pltpu.VMEM((B_HEADS, B_DIM, B_DIM), F32)],
        compiler_params=pltpu.CompilerParams(dimension_semantics=("arbitrary", "arbitrary")),
    )(conv, bg)


def _delta_bwd(conv, bg, states, inverses, do, n_batch, seq):
    nblk, cpb, rows_spec, states_spec = _delta_blocks(n_batch, seq)

    def backward(i):
        return nblk - 1 - i

    def body(c_ref, bg_ref, st_ref, inv_ref, do_ref, dc_ref, dbg_ref, dr_ref):
        @pl.when(pl.program_id(1) == 0)
        def _():
            dr_ref[...] = jnp.zeros_like(dr_ref)

        def step(n, carry):
            c = cpb - 1 - n
            rows = pl.ds(pl.multiple_of(c * CHUNK, CHUNK), CHUNK)
            saved = [inv_ref[c, h] for h in range(B_HEADS)]
            _, vjp = jax.vjp(lambda *args: _delta_chunk(*args, saved_inv=saved)[:2],
                             *_load_heads(c_ref, bg_ref, st_ref.at[c], rows))
            do = [do_ref[rows, pl.ds(h * B_DIM, B_DIM)] for h in range(B_HEADS)]
            dr, dq, dk, dv, dbeta, dg = vjp((do, [dr_ref[h] for h in range(B_HEADS)]))
            lane = lax.broadcasted_iota(jnp.int32, (CHUNK, LANE), 1)
            dbg = jnp.zeros((CHUNK, LANE), F32)
            for h in range(B_HEADS):
                cq, ck, cv = _head_cols(h)
                dr_ref[h] = dr[h]
                dc_ref[rows, cq] = dq[h]
                dc_ref[rows, ck] = dk[h]
                dc_ref[rows, cv] = dv[h]
                dbg = dbg + jnp.where(lane == h, dbeta[h], 0.0) + jnp.where(lane == h + B_HEADS, dg[h], 0.0)
            dbg_ref[rows, :] = dbg
            return carry

        lax.fori_loop(0, cpb, step, 0)

    n_tok = n_batch * seq
    return pl.pallas_call(
        body,
        name="delta_bwd",
        grid=(n_batch, nblk),
        in_specs=[rows_spec(B_CONV, backward), rows_spec(LANE, backward), states_spec(backward, B_DIM),
                  states_spec(backward, CHUNK), rows_spec(B_HEADS * B_DIM, backward)],
        out_specs=[rows_spec(B_CONV, backward), rows_spec(LANE, backward)],
        out_shape=[jax.ShapeDtypeStruct((n_tok, B_CONV), F32), jax.ShapeDtypeStruct((n_tok, LANE), F32)],
        scratch_shapes=[pltpu.VMEM((B_HEADS, B_DIM, B_DIM), F32)],
        compiler_params=pltpu.CompilerParams(dimension_semantics=("arbitrary", "arbitrary")),
    )(conv, bg, states, inverses, do)


def _lane_row(vec4, first):
    return jnp.concatenate([jnp.zeros((1, first), F32), vec4.reshape(1, B_HEADS).astype(F32),
                            jnp.zeros((1, LANE - first - B_HEADS), F32)], axis=1)


def _local_step(x3d, p3d, tgt3d, w_in, small, rest_weights, send_grads, send_w_in):
    n_batch, seq, _ = x3d.shape
    n_tok = n_batch * seq
    x, p, tgt = x3d.reshape(n_tok, D), p3d.reshape(n_tok, -1), tgt3d.reshape(n_tok, D)
    g_mix, g_ffn, g_ple, g_final = (small[k].reshape(1, D) for k in ("g_mix", "g_ffn", "g_ple", "g_final"))
    w_onorm = small["w_onorm"].reshape(1, B_DIM)
    al_row = _lane_row(small["a_log"], B_HEADS)
    dtb_row = _lane_row(small["dt_bias"], B_HEADS)
    rel_bias = small["rel_bias"].reshape(A_HEADS, -1)
    bias = _rel_bias_table(rel_bias)
    conv_w = small["conv_w"].reshape(4, B_CONV)

    h1 = _rms_fwd(x, g_mix, name="rms_mix")
    projp = _mm(h1, w_in, tb=True, out_dtype=BF16, name="mm_proj", tn=1920)
    bd = _mm(h1, w_in[P_BD:], tb=True, name="mm_beta_decay", tn=LANE)
    y_a = _attn_fwd(projp, bias, n_batch, seq)
    conv = _conv_fwd(projp, conv_w, n_batch, seq)
    (bg,) = _rowwise(lambda raw, al, dtb: ([_gate_scalars(raw, al, dtb)], []), [_full(bd)],
                     [_full(al_row), _full(dtb_row)], [(LANE, F32, LANE, 0, 0)], name="gate_scalars", tr=1024)
    o_b, states, inverses = _delta_fwd(conv, bg, n_batch, seq)
    (y_b,) = _rowwise(lambda o, z, wn: ([_gated_norm(o, z, wn)], []), [(o_b, LANE, 0, 1), (projp, LANE, P_Z // LANE, 1)],
                      [_full(w_onorm)], [(B_HEADS * B_DIM, BF16, LANE, 0, 1)], name="gated_norm", tr=1024, ncol=B_HEADS)
    w = rest_weights(y_b)
    t_a = _mm(y_a, w["w_branch_a"], tb=True, name="mm_branch_a", tn=1024)
    t_b = _mm(y_b, w["w_branch_b"], tb=True, name="mm_branch_b", tn=1024)
    half = D // 2
    gate_rows = [(projp, half, P_GATE // half, 1), (projp, half, P_GATE // half + 2, 1), (t_a, half, 0, 1), (t_b, half, 0, 1)]
    (merged,) = _rowwise(lambda ga, gb, ta, tb: ([_merge(ga, gb, ta, tb)], []), gate_rows, [], [(D, BF16, half, 0, 1)],
                         name="merge", tr=512, ncol=2)
    x1, h2 = _mm(merged, w["w_out"], name="mm_out", tn=1024, epilogue=_residual_rms, rows=[x], bcs=[g_ffn], outs=(F32, BF16))
    gu = _mm(h2, w["w_gate_up"], tb=True, out_dtype=BF16, name="mm_gate_up", tn=2816)
    (act,) = _rowwise(lambda gub: ([_swiglu(gub)], []), [_full(gu)], [], [(D_FF, BF16, D_FF, 0, 0)], name="swiglu",
                      tr=512)
    x2, h3 = _mm(act, w["w_down"], name="mm_down", tm=512, tn=1024, tk=D_FF, epilogue=_residual_rms, rows=[x1], bcs=[g_ple],
                 outs=(F32, BF16))
    pg = _mm(h3, w["w_ple_gate"], name="mm_ple_gate", tn=1024)
    pp = _mm(p, w["w_ple_proj"], tb=True, name="mm_ple_proj", tn=1024)

    def head_fn(x2b, pgb, ppb, tb, gb):
        loss, (dx2, dpg, dpp, dg) = jax.value_and_grad(_head_loss, argnums=(0, 1, 2, 4))(x2b, pgb, ppb, tb, gb)
        return [dx2, dpg, dpp], [dg, jnp.full((1, LANE), loss, F32)]

    dx3, dpg, dpp, dg_final, loss_row = _rowwise(
        head_fn, [_full(x2), _full(pg), _full(pp), _full(tgt)], [_full(g_final)],
        [(D, F32, D, 0, 0), (D, BF16, D, 0, 0), (D, BF16, D, 0, 0)], [(D, D, 0), (LANE, LANE, 0)], name="loss_head", tr=256)
    gw = {}
    gw["w_ple_proj"] = _mm(dpp, p, ta=True, out_dtype=BF16, name="mm_d_ple_proj", tn=256)
    gw["w_ple_gate"] = _mm(h3, dpg, ta=True, out_dtype=BF16, name="mm_d_ple_gate", tn=512)
    dx2, dg_ple = _mm(dpg, w["w_ple_gate"], tb=True, name="mm_dh3", tm=512, tn=1024, epilogue=_rms_bwd, rows=[x2, dx3],
                      bcs=[g_ple], outs=(F32,), n_red=1)
    gw["w_down"] = _mm(act, dx2, ta=True, out_dtype=BF16, name="mm_d_down", tm=1408, tn=256)
    dact = _mm(dx2, w["w_down"], tb=True, out_dtype=BF16, name="mm_dact", tn=D_FF)

    def swiglu_bwd(gub, dab):
        _, vjp = jax.vjp(_swiglu, gub)
        return [vjp(dab)[0]], []

    (dgu,) = _rowwise(swiglu_bwd, [_full(gu), _full(dact)], [], [(2 * D_FF, BF16, 2 * D_FF, 0, 0)], name="swiglu_bwd", tr=256)
    gw["w_gate_up"] = _mm(dgu, h2, ta=True, out_dtype=BF16, name="mm_d_gate_up", tm=512, tn=1024)
    dx1, dg_ffn = _mm(dgu, w["w_gate_up"], name="mm_dh2", tm=256, tn=1024, tk=2 * D_FF, epilogue=_rms_bwd, rows=[x1, dx2],
                      bcs=[g_ffn], outs=(F32,), n_red=1)
    gw["w_out"] = _mm(merged, dx1, ta=True, out_dtype=BF16, name="mm_d_out", tn=512)
    dmerged = _mm(dx1, w["w_out"], tb=True, name="mm_dmerged", tn=1024)

    def merge_bwd(ga, gb, ta, tb, dm):
        _, vjp = jax.vjp(_merge, ga, gb, ta, tb)
        return list(vjp(dm)), []

    dga, dgb, dta, dtb = _rowwise(merge_bwd, gate_rows + [(dmerged, half, 0, 1)], [], [(D, BF16, half, 0, 1)] * 4,
                                  name="merge_bwd", tr=512, ncol=2)
    gw["w_branch_a"] = _mm(dta, y_a, ta=True, out_dtype=BF16, name="mm_d_branch_a", tn=512)
    gw["w_branch_b"] = _mm(dtb, y_b, ta=True, out_dtype=BF16, name="mm_d_branch_b", tn=512)
    dya = _mm(dta, w["w_branch_a"], name="mm_dya", tn=512)
    dyb = _mm(dtb, w["w_branch_b"], name="mm_dyb", tn=512)

    w_onorm = w_onorm + send_grads(gw)[0, 0]

    def gated_norm_bwd(o, z, dy, wn):
        _, vjp = jax.vjp(_gated_norm, o, z, wn)
        do, dz, dwn = vjp(dy)
        return [do, dz], [dwn]

    do_b, dz, dw_onorm = _rowwise(
        gated_norm_bwd, [(o_b, LANE, 0, 1), (projp, LANE, P_Z // LANE, 1), (dyb, LANE, 0, 1)], [_full(w_onorm)],
        [(B_HEADS * B_DIM, F32, LANE, 0, 1), (B_HEADS * B_DIM, BF16, LANE, 0, 1)], [(B_DIM, B_DIM, 0)],
        name="gated_norm_bwd", tr=1024, ncol=B_HEADS)
    dconv_out, dbg = _delta_bwd(conv, bg, states, inverses, do_b, n_batch, seq)

    def gate_scalars_bwd(raw, dbgb, al, dtb):
        _, vjp = jax.vjp(_gate_scalars, raw, al, dtb)
        draw, dal, ddtb = vjp(dbgb)
        return [draw], [dal, ddtb]

    dbd, dal_row, ddtb_row = _rowwise(gate_scalars_bwd, [_full(bd), _full(dbg)], [_full(al_row), _full(dtb_row)],
                                      [(LANE, BF16, LANE, 0, 0)], [(LANE, LANE, 0), (LANE, LANE, 0)], name="gate_scalars_bwd",
                                      tr=1024)
    dconv, dconv_w = _conv_bwd(projp, conv_w, dconv_out, n_batch, seq)
    dq_a, dk_a, dv_a, dbias = _attn_bwd(projp, bias, dya, n_batch, seq)
    dprojp = jnp.concatenate([dq_a, dk_a, dv_a, dconv, dz, dga, dgb, dbd], axis=1)
    sent = send_w_in(_mm(dprojp, h1, ta=True, out_dtype=BF16, name="mm_d_in", tm=640, tn=1024))
    sent, dprojp = lax.optimization_barrier((sent, dprojp))
    grad_x, dg_mix = _mm(dprojp, w_in, name="mm_dh1", tm=256, tn=1024, tk=P_END, epilogue=_rms_bwd, rows=[x, dx1],
                         bcs=[g_mix + sent[0, 0]], outs=(F32,), n_red=1)

    _, bias_vjp = jax.vjp(_rel_bias_table, rel_bias)
    gs = {
        "g_mix": dg_mix, "g_ffn": dg_ffn, "g_ple": dg_ple, "g_final": dg_final, "w_onorm": dw_onorm,
        "conv_w": dconv_w, "rel_bias": bias_vjp(dbias)[0],
        "a_log": dal_row[0, B_HEADS:2 * B_HEADS], "dt_bias": ddtb_row[0, B_HEADS:2 * B_HEADS],
    }
    return loss_row[:, :1], grad_x.reshape(n_batch, seq, D), gs


MATRICES = (("w_in", 1), ("w_gate_up", 1), ("w_branch_a", 1), ("w_branch_b", 1), ("w_out", 0), ("w_down", 0),
            ("w_ple_gate", 0), ("w_ple_proj", 1))
TAPS_PER_SHARD = B_CONV // N_DEV


def _held(shard, axis):
    return shard if axis == 0 else shard.T


def _from_gathered(slabs):
    return slabs.reshape(-1, slabs.shape[-1])


def _to_owner(held):
    return held.reshape(N_DEV, held.shape[0] // N_DEV, held.shape[1])


def _permute_w_in(held):
    n_gate = P_BD - P_GATE
    row = lax.broadcasted_iota(jnp.int32, (P_END, 1), 0)
    same = jnp.pad(held, ((0, P_END - D_IN), (0, 0)))
    up = jnp.pad(held[8:], ((0, P_END - D_IN + 8), (0, 0)))
    down = jnp.pad(held[:P_GATE + 8], ((n_gate, P_END - P_BD - 8), (0, 0)))
    zero = jnp.zeros((), held.dtype)
    return jnp.where(row < P_GATE, same, jnp.where(row < P_BD, up, jnp.where(row < P_BD + 8, down, zero)))


def _unpermute_w_in(gp):
    n_gate = P_BD - P_GATE
    row = lax.broadcasted_iota(jnp.int32, (D_IN, 1), 0)
    same = gp[:D_IN]
    up = jnp.pad(gp[n_gate:], ((0, D_IN - (P_END - n_gate)), (0, 0)))
    down = jnp.pad(gp[:P_BD], ((8, 0), (0, 0)))
    return jnp.where(row < P_GATE, same, jnp.where(row < P_GATE + 8, up, down))


SMALL_ROWS = 16
SMALL_LAYOUT = (("g_mix", 0, D), ("g_ffn", 1, D), ("g_ple", 2, D), ("g_final", 3, D), ("conv_w", 4, 4 * B_CONV),
                ("rel_bias", 10, A_HEADS * (2 * REL_CLIP + 1)), ("w_onorm", 13, B_DIM), ("a_log", 14, B_HEADS),
                ("dt_bias", 14, B_HEADS), ("loss", 15, 1))


def _pack_small(gs):
    rows = {}
    for name, row, n in SMALL_LAYOUT:
        rows.setdefault(row, []).append(gs[name].reshape(-1).astype(F32))
    parts = []
    for row in sorted(rows):
        flat = jnp.concatenate(rows[row])
        parts.append(jnp.concatenate([flat, jnp.zeros((-flat.shape[0] % D,), F32)]))
    flat = jnp.concatenate(parts)
    assert flat.shape[0] == SMALL_ROWS * D, flat.shape
    return flat.reshape(SMALL_ROWS, D)


def _unpack_small(blk):
    flat, out, used = blk.reshape(-1), {}, {}
    for name, row, n in SMALL_LAYOUT:
        start = row * D + used.get(row, 0)
        out[name] = flat[start:start + n]
        used[row] = used.get(row, 0) + n
    return out


def _position():
    return lax.axis_index("x"), lax.axis_index("y"), lax.axis_index("c")


PEERS = N_DEV - 1


def _comm_call(body, arrays, out_shapes, *, name):
    n = len(arrays)
    return pl.pallas_call(
        body,
        name=name,
        out_shape=out_shapes,
        in_specs=[HBM_SPEC] * n,
        out_specs=[HBM_SPEC] * n,
        scratch_shapes=[pltpu.SemaphoreType.DMA((PEERS * n,)), pltpu.SemaphoreType.DMA((PEERS * n,)),
                        pltpu.SemaphoreType.DMA((n,))],
    )(*arrays)


def _weights_allgather(shards):
    n = len(shards)

    def body(*refs):
        ins, outs = refs[:n], refs[n:2 * n]
        send_sems, recv_sems, local_sems = refs[2 * n:]
        x, y, c = _position()
        me, sibling = (x, y, c), (x, y, 1 - c)
        chips = [(1 - x, y), (x, 1 - y), (1 - x, 1 - y)]

        def slab(a, px, py, pc):
            return outs[a].at[4 * px + 2 * py + pc]

        def copy(a, k, block, to, src=None):
            return pltpu.make_async_remote_copy(src_ref=slab(a, *block) if src is None else src, dst_ref=slab(a, *block),
                                                send_sem=send_sems.at[PEERS * a + k], recv_sem=recv_sems.at[PEERS * a + k],
                                                device_id=to, device_id_type=MESH)

        local = [pltpu.make_async_copy(ins[a], slab(a, *me), local_sems.at[a]) for a in range(n)]
        sent = [copy(a, 1 + j, me, (*chip, c), src=ins[a]) for a in range(n) for j, chip in enumerate(chips)]
        sent += [copy(a, 0, me, sibling, src=ins[a]) for a in range(n)]
        for cp in sent + local:
            cp.start()
        for a in range(n):
            for j, chip in enumerate(chips):
                copy(a, 1 + j, (*chip, c), me).wait_recv()
                passed = copy(a, 4 + j, (*chip, c), sibling)
                passed.start()
                sent.append(passed)
        for a in range(n):
            copy(a, 0, sibling, me).wait_recv()
            for j, chip in enumerate(chips):
                copy(a, 4 + j, (*chip, 1 - c), me).wait_recv()
        for cp in sent:
            cp.wait_send()
        for cp in local:
            cp.wait()

    return _comm_call(body, shards, [jax.ShapeDtypeStruct((N_DEV,) + s.shape, s.dtype) for s in shards],
                      name="weights_allgather")


def _grads_exchange(by_owner):
    n = len(by_owner)

    def body(*refs):
        ins, outs = refs[:n], refs[n:2 * n]
        send_sems, recv_sems, local_sems = refs[2 * n:]
        x, y, c = _position()
        mine = 4 * x + 2 * y + c
        local = [pltpu.make_async_copy(ins[a].at[mine], outs[a].at[mine], local_sems.at[a]) for a in range(n)]
        for cp in local:
            cp.start()
        flips = [(dx, dy, dc) for dx in (0, 1) for dy in (0, 1) for dc in (0, 1) if dx + dy + dc]
        pending = []
        for k, (dx, dy, dc) in enumerate(flips):
            px, py, pc = (1 - x if dx else x), (1 - y if dy else y), (1 - c if dc else c)
            peer = 4 * px + 2 * py + pc
            for a in range(n):
                def remote(slot):
                    return pltpu.make_async_remote_copy(src_ref=ins[a].at[peer], dst_ref=outs[a].at[slot],
                                                        send_sem=send_sems.at[PEERS * a + k], recv_sem=recv_sems.at[PEERS * a + k],
                                                        device_id=(px, py, pc), device_id_type=MESH)

                sent = remote(mine)
                sent.start()
                pending.append((sent, remote(peer)))
        for sent, landed in pending:
            landed.wait_recv()
            sent.wait_send()
        for cp in local:
            cp.wait()

    return _comm_call(body, by_owner, [jax.ShapeDtypeStruct(g.shape, g.dtype) for g in by_owner], name="grads_exchange")


SEM_SPEC = pl.BlockSpec(memory_space=pltpu.SEMAPHORE)
DATAFLOW = pltpu.SideEffectType.DATAFLOW_SIDE_EFFECTING


def _peer_copies(srcs, lands, send_sems, recv_sems, by_owner, arrival):
    x, y, c = _position()
    mine = 4 * x + 2 * y + c
    copies = []
    for k, (dx, dy, dc) in enumerate([(dx, dy, dc) for dx in (0, 1) for dy in (0, 1) for dc in (0, 1) if dx + dy + dc]):
        px, py, pc = (1 - x if dx else x), (1 - y if dy else y), (1 - c if dc else c)
        peer = 4 * px + 2 * py + pc
        for a, (src, land) in enumerate(zip(srcs, lands)):
            copies.append(pltpu.make_async_remote_copy(
                src_ref=src.at[peer] if by_owner else src, dst_ref=land.at[peer if arrival else mine],
                send_sem=send_sems.at[PEERS * a + k], recv_sem=recv_sems.at[PEERS * a + k],
                device_id=(px, py, pc), device_id_type=MESH))
    return copies


def _exchange_start(sources, by_owner, *, name):
    n = len(sources)
    lands = [lax.empty((N_DEV,) + (s.shape[1:] if by_owner else s.shape), s.dtype) for s in sources]

    def body(*refs):
        send_sems, recv_sems, token = refs[2 * n], refs[2 * n + 1], refs[-1]
        for copy in _peer_copies(refs[:n], refs[n:2 * n], send_sems, recv_sems, by_owner, arrival=False):
            copy.start()
        token[...] = jnp.zeros_like(token)

    sems = pltpu.SemaphoreType.DMA((PEERS * n,))
    outs = pl.pallas_call(
        body,
        name=name,
        out_shape=(sems, sems, *[pltpu.HBM(a.shape, a.dtype) for a in sources + lands], jax.ShapeDtypeStruct((8, LANE), F32)),
        in_specs=[HBM_SPEC] * (2 * n),
        out_specs=(SEM_SPEC, SEM_SPEC, *[HBM_SPEC] * (2 * n), pl.BlockSpec(memory_space=pltpu.VMEM)),
        input_output_aliases={i: 2 + i for i in range(2 * n)},
        compiler_params=pltpu.CompilerParams(has_side_effects=DATAFLOW),
    )(*[pltpu.with_memory_space_constraint(a, pltpu.HBM) for a in sources + lands])
    return outs[:-1], outs[-1]


def _exchange_wait(started, after, by_owner, *, name):
    send_sems, recv_sems, *arrays = started
    n = len(arrays) // 2

    def body(*refs):
        for copy in _peer_copies(refs[:n], refs[n:2 * n], refs[2 * n], refs[2 * n + 1], by_owner, arrival=True):
            copy.wait_send()
            copy.wait_recv()

    outs = pl.pallas_call(
        body,
        name=name,
        out_shape=[pltpu.HBM(a.shape, a.dtype) for a in arrays],
        in_specs=[HBM_SPEC] * (2 * n) + [SEM_SPEC, SEM_SPEC, pl.BlockSpec(memory_space=pl.ANY)],
        out_specs=[HBM_SPEC] * (2 * n),
        input_output_aliases={i: i for i in range(2 * n)},
        compiler_params=pltpu.CompilerParams(has_side_effects=DATAFLOW),
    )(*arrays, send_sems, recv_sems, after)
    return outs[:n], outs[n:]


def _slot_sum(g_ref, own_ref):
    if own_ref is not None:
        x, y, c = _position()
        mine = 4 * x + 2 * y + c
    acc = None
    for j in range(N_DEV):
        part = g_ref[j] if own_ref is None else jnp.where(mine == j, own_ref[...], g_ref[j])
        acc = part.astype(F32) if acc is None else acc + part.astype(F32)
    return acc


def _sum_slots(got, *, name, tr):
    _, rows, cols = got.shape
    tr = _tile(rows, tr, 16)

    def body(g_ref, o_ref):
        o_ref[...] = _slot_sum(g_ref, None)

    return pl.pallas_call(
        body,
        name=name,
        grid=(rows // tr,),
        in_specs=[pl.BlockSpec((N_DEV, tr, cols), lambda i: (0, i, 0))],
        out_specs=pl.BlockSpec((tr, cols), lambda i: (i, 0)),
        out_shape=jax.ShapeDtypeStruct((rows, cols), F32),
        compiler_params=pltpu.CompilerParams(dimension_semantics=("parallel",)),
    )(got)


def _adamw(wt, g, m, v, *, name, own=None):
    slots = own is not None
    shape = wt.shape
    two_d = (-1, shape[-1]) if wt.ndim > 1 else (1, -1)
    args = [a.reshape(two_d) for a in (wt, m, v)]
    rows, cols = args[0].shape
    if rows % 16 == 0:
        tr, tc = _tile(rows, 256, 16), cols
    else:
        tr, tc = rows, _tile(cols, 256 if rows > 64 else 512)
    args.insert(1, g.reshape((N_DEV, rows, cols) if slots else (rows, cols)))
    if slots:
        args.append(own.reshape(rows, cols))

    def body(w_ref, g_ref, m_ref, v_ref, *refs):
        go_ref, d_ref, nm_ref, nv_ref = refs[-4:]
        gv = _slot_sum(g_ref, refs[0]) if slots else g_ref[...]
        go_ref[...] = gv
        m2 = ADAM_B1 * m_ref[...] + (1.0 - ADAM_B1) * gv
        v2 = ADAM_B2 * v_ref[...] + (1.0 - ADAM_B2) * (gv * gv)
        m_hat = m2 / (1.0 - ADAM_B1 ** ADAM_STEP)
        v_hat = v2 / (1.0 - ADAM_B2 ** ADAM_STEP)
        d_ref[...] = -ADAM_LR * (m_hat / (jnp.sqrt(v_hat) + ADAM_EPS) + ADAM_WD * w_ref[...])
        nm_ref[...] = m2
        nv_ref[...] = v2

    spec = pl.BlockSpec((tr, tc), lambda i, j: (i, j))
    g_spec = pl.BlockSpec((N_DEV, tr, tc), lambda i, j: (0, i, j)) if slots else spec
    outs = pl.pallas_call(
        body,
        name=name,
        grid=(rows // tr, cols // tc),
        in_specs=[spec, g_spec, spec, spec] + ([spec] if slots else []),
        out_specs=[spec] * 4,
        out_shape=[jax.ShapeDtypeStruct((rows, cols), F32)] * 4,
        compiler_params=pltpu.CompilerParams(dimension_semantics=("parallel", "parallel")),
    )(*args)
    return tuple(o.reshape(shape) for o in outs)


WEIGHTS = ("g_mix", "w_in", "conv_w", "a_log", "dt_bias", "rel_bias", "w_onorm", "w_branch_a", "w_branch_b", "w_out", "g_ffn",
           "w_gate_up", "w_down", "g_ple", "w_ple_gate", "w_ple_proj", "g_final")


def kernel(x, p, g_mix, w_in, conv_w, a_log, dt_bias, rel_bias, w_onorm, w_branch_a, w_branch_b, w_out, g_ffn, w_gate_up, w_down, g_ple, w_ple_gate, w_ple_proj, g_final, loss_target, m_g_mix, m_w_in, m_conv_w, m_a_log, m_dt_bias, m_rel_bias, m_w_onorm, m_w_branch_a, m_w_branch_b, m_w_out, m_g_ffn, m_w_gate_up, m_w_down, m_g_ple, m_w_ple_gate, m_w_ple_proj, m_g_final, v_g_mix, v_w_in, v_conv_w, v_a_log, v_dt_bias, v_rel_bias, v_w_onorm, v_w_branch_a, v_w_branch_b, v_w_out, v_g_ffn, v_w_gate_up, v_w_down, v_g_ple, v_w_ple_gate, v_w_ple_proj, v_g_final):
    given = dict(g_mix=g_mix, w_in=w_in, conv_w=conv_w, a_log=a_log, dt_bias=dt_bias, rel_bias=rel_bias, w_onorm=w_onorm,
                 w_branch_a=w_branch_a, w_branch_b=w_branch_b, w_out=w_out, g_ffn=g_ffn, w_gate_up=w_gate_up, w_down=w_down,
                 g_ple=g_ple, w_ple_gate=w_ple_gate, w_ple_proj=w_ple_proj, g_final=g_final)
    mom1 = dict(g_mix=m_g_mix, w_in=m_w_in, conv_w=m_conv_w, a_log=m_a_log, dt_bias=m_dt_bias, rel_bias=m_rel_bias,
                w_onorm=m_w_onorm, w_branch_a=m_w_branch_a, w_branch_b=m_w_branch_b, w_out=m_w_out, g_ffn=m_g_ffn,
                w_gate_up=m_w_gate_up, w_down=m_w_down, g_ple=m_g_ple, w_ple_gate=m_w_ple_gate, w_ple_proj=m_w_ple_proj,
                g_final=m_g_final)
    mom2 = dict(g_mix=v_g_mix, w_in=v_w_in, conv_w=v_conv_w, a_log=v_a_log, dt_bias=v_dt_bias, rel_bias=v_rel_bias,
                w_onorm=v_w_onorm, w_branch_a=v_w_branch_a, w_branch_b=v_w_branch_b, w_out=v_w_out, g_ffn=v_g_ffn,
                w_gate_up=v_w_gate_up, w_down=v_w_down, g_ple=v_g_ple, w_ple_gate=v_w_ple_gate, w_ple_proj=v_w_ple_proj,
                g_final=v_g_final)
    mine = 4 * lax.axis_index("x") + 2 * lax.axis_index("y") + lax.axis_index("c")

    my_slot = (jnp.arange(N_DEV) == mine)[:, None, None]
    rest = MATRICES[1:]
    in_flight = {}

    got_in, got_taps = _weights_allgather([_held(w_in[0], 1).astype(BF16), conv_w[0]])
    in_flight["weights"], weights_sent = _exchange_start([_held(given[name][0], axis).astype(BF16) for name, axis in rest], False,
                                                         name="weights_start")
    small = dict(g_mix=g_mix + weights_sent[0, 0], g_ffn=g_ffn, g_ple=g_ple, g_final=g_final, w_onorm=w_onorm, a_log=a_log,
                 dt_bias=dt_bias, rel_bias=rel_bias, conv_w=jnp.transpose(got_taps, (1, 0, 2)).reshape(4, B_CONV))

    def rest_weights(after):
        shards, landed = _exchange_wait(in_flight.pop("weights"), after, False, name="weights_wait")
        return {name: _from_gathered(jnp.where(my_slot, shard[None], slabs)) for (name, _), shard, slabs in zip(rest, shards, landed)}

    def send_grads(gw):
        in_flight["grads"], sent = _exchange_start([_to_owner(gw[name]) for name, _ in rest], True, name="grads_start")
        return sent

    def send_w_in(g_in):
        in_flight["grad_in"], sent = _exchange_start([_to_owner(_unpermute_w_in(g_in))], True, name="grad_in_start")
        return sent

    loss_part, grad_x, gs = _local_step(x, p[0], loss_target, _permute_w_in(_from_gathered(got_in)), small, rest_weights,
                                        send_grads, send_w_in)
    gs["loss"] = loss_part

    updates = {}

    def update_matrices(matrices, own_slabs, landed):
        for (name, axis), own_slab, slots in zip(matrices, own_slabs, landed):
            mine_of = lax.dynamic_index_in_dim(own_slab, mine, axis=0, keepdims=False)
            w_held, m_held, v_held = (_held(a[name][0], axis) for a in (given, mom1, mom2))
            outs = _adamw(w_held, slots, m_held, v_held, name=f"adamw_{name}", own=mine_of)
            updates[name] = tuple(_held(o, axis)[None] for o in outs)

    update_matrices(rest, *_exchange_wait(in_flight["grads"], grad_x, True, name="grads_wait"))
    update_matrices(MATRICES[:1], *_exchange_wait(in_flight["grad_in"], updates[rest[-1][0]][0], True, name="grad_in_wait"))
    small_block, _ = lax.optimization_barrier((_pack_small(gs), updates["w_in"][0]))
    (got_small,) = _grads_exchange([jnp.broadcast_to(small_block, (N_DEV, SMALL_ROWS, D))])
    small_sum = _unpack_small(_sum_slots(got_small, name="sum_small_grads", tr=16))
    loss = small_sum.pop("loss")[0]
    conv_all = small_sum.pop("conv_w").reshape(4, N_DEV, TAPS_PER_SHARD)
    small_sum["conv_w"] = lax.dynamic_index_in_dim(conv_all, mine, axis=1, keepdims=False)
    for name, g in small_sum.items():
        updates[name] = _adamw(given[name], g.reshape(given[name].shape), mom1[name], mom2[name], name=f"adamw_{name}")
    return (loss, grad_x, *[updates[name][k] for k in range(4) for name in WEIGHTS])
```

```python
import jax
import jax.numpy as jnp
from jax import lax
from jax.experimental import pallas as pl
from jax.experimental.pallas import tpu as pltpu

F32 = jnp.float32
BF16 = jnp.bfloat16
DELTA_PREC = lax.Precision.HIGH
MESH = pl.DeviceIdType.MESH

N_DEV = 8
D = 1024
CHUNK = 64
EPS = 1e-6
A_HEADS, A_DIM, A_WIDTH = 8, 64, 512
A_BAND = 9 * CHUNK
A_PAD = 8 * CHUNK
REL_CLIP = 128
B_HEADS, B_DIM = 4, 128
B_CONV = 1536
D_FF = 2816
D_IN = 5640
P_CONV, P_Z, P_GATE, P_BD, P_END = 1536, 3072, 3584, 5632, 5760
LANE = 128

ADAM_LR, ADAM_B1, ADAM_B2, ADAM_EPS, ADAM_WD, ADAM_STEP = 0.001, 0.9, 0.999, 1e-08, 0.01, 10

NT = (((1,), (1,)), ((), ()))
TN = (((0,), (0,)), ((), ()))
NN = (((1,), (0,)), ((), ()))

HBM_SPEC = pl.BlockSpec(memory_space=pltpu.HBM)


def _tile(n, target, align=LANE):
    if n <= target:
        return n
    best = None
    for t in range(align, target + 1, align):
        if n % t == 0:
            best = t
    assert best is not None, (n, target, align)
    return best


def _mm(a, b, *, name, ta=False, tb=False, out_dtype=F32, tm=1024, tn=640, tk=None, epilogue=None, rows=(), bcs=(), outs=(),
        n_red=0):
    assert not (ta and tb)
    if ta:
        k_dim, m_dim = a.shape
    else:
        m_dim, k_dim = a.shape
    n_dim = b.shape[0] if tb else b.shape[1]
    assert b.shape[1 if tb else 0] == k_dim
    tm, tn = _tile(m_dim, tm), _tile(n_dim, tn)
    tk = _tile(k_dim, tk or (4096 if ta else 1024), 8 if ta else LANE)
    nk = k_dim // tk
    dn = TN if ta else NT if tb else NN
    if epilogue is None:
        outs = (out_dtype,)
    assert not n_red or tn == n_dim
    n_extra, n_out = len(rows) + len(bcs), len(outs)

    def body(a_ref, b_ref, *refs):
        part = lax.dot_general(a_ref[...].astype(BF16), b_ref[...].astype(BF16), dn, preferred_element_type=F32)

        def finish(r):
            o_vals, r_vals = ([r], []) if epilogue is None else epilogue(r, *[x[...] for x in refs[:n_extra]])
            for ref, val in zip(refs[n_extra:n_extra + n_out], o_vals):
                ref[...] = val.astype(ref.dtype)
            first = pl.program_id(0) == 0
            for ref, val in zip(refs[n_extra + n_out:n_extra + n_out + n_red], r_vals):
                @pl.when(first)
                def _():
                    ref[...] = val

                @pl.when(jnp.logical_not(first))
                def _():
                    ref[...] += val

        if nk == 1:
            finish(part)
        else:
            acc_ref = refs[-1]
            k = pl.program_id(2)

            @pl.when(k == 0)
            def _():
                acc_ref[...] = part

            @pl.when(k > 0)
            def _():
                acc_ref[...] += part

            @pl.when(k == nk - 1)
            def _():
                finish(acc_ref[...])

    tile = pl.BlockSpec((tm, tn), lambda i, j, k: (i, j))
    col = pl.BlockSpec((1, tn), lambda i, j, k: (0, j))
    a_spec = pl.BlockSpec((tk, tm), lambda i, j, k: (k, i)) if ta else pl.BlockSpec((tm, tk), lambda i, j, k: (i, k))
    b_spec = pl.BlockSpec((tn, tk), lambda i, j, k: (j, k)) if tb else pl.BlockSpec((tk, tn), lambda i, j, k: (k, j))
    result = pl.pallas_call(
        body,
        name=name,
        grid=(m_dim // tm, n_dim // tn, nk),
        in_specs=[a_spec, b_spec] + [tile] * len(rows) + [col] * len(bcs),
        out_specs=[tile] * n_out + [col] * n_red,
        out_shape=[jax.ShapeDtypeStruct((m_dim, n_dim), dt) for dt in outs] + [jax.ShapeDtypeStruct((1, n_dim), F32)] * n_red,
        scratch_shapes=[pltpu.VMEM((tm, tn), F32)] if nk > 1 else [],
        compiler_params=pltpu.CompilerParams(dimension_semantics=("arbitrary",) * 3 if n_red else ("parallel", "parallel", "arbitrary")),
    )(a, b, *rows, *bcs)
    return result[0] if epilogue is None else result


def _rowwise(fn, rows, bcs, outs, reds=(), *, name, tr, ncol=1):
    n_rows = rows[0][0].shape[0]
    tr = _tile(n_rows, tr, 8)
    nrow = n_rows // tr
    n_in, n_out = len(rows) + len(bcs), len(outs)

    def body(*refs):
        j, i = pl.program_id(0), pl.program_id(1)
        o_vals, r_vals = fn(*[r[...].astype(F32) for r in refs[:n_in]])
        for ref, val in zip(refs[n_in:n_in + n_out], o_vals):
            ref[...] = val.astype(ref.dtype)
        for ref, val, (_, _, stride) in zip(refs[n_in + n_out:], r_vals, reds):
            first = (i == 0) if stride else jnp.logical_and(i == 0, j == 0)

            @pl.when(first)
            def _():
                ref[...] = val

            @pl.when(jnp.logical_not(first))
            def _():
                ref[...] += val

    def spec(r, w, off, st, row_dep=True):
        if row_dep:
            return pl.BlockSpec((r, w), lambda j, i: (i, off + st * j))
        return pl.BlockSpec((r, w), lambda j, i: (0, off + st * j))

    in_specs = [spec(tr, w, off, st) for (_, w, off, st) in rows]
    in_specs += [spec(a.shape[0], w, off, st, False) for (a, w, off, st) in bcs]
    out_specs = [spec(tr, w, off, st) for (_, _, w, off, st) in outs]
    out_specs += [spec(1, w, 0, st, False) for (_, w, st) in reds]
    out_shape = [jax.ShapeDtypeStruct((n_rows, c), dt) for (c, dt, _, _, _) in outs]
    out_shape += [jax.ShapeDtypeStruct((1, c), F32) for (c, _, _) in reds]
    return pl.pallas_call(
        body,
        name=name,
        grid=(ncol, nrow),
        in_specs=in_specs,
        out_specs=out_specs,
        out_shape=out_shape,
        compiler_params=pltpu.CompilerParams(dimension_semantics=("arbitrary", "arbitrary")),
    )(*[r[0] for r in rows], *[b[0] for b in bcs])


def _full(a):
    return (a, a.shape[1], 0, 0)


def _rms(x, g):
    return x * lax.rsqrt(jnp.mean(x * x, axis=-1, keepdims=True) + EPS) * g


def _silu(x):
    return x * jax.nn.sigmoid(x)


def _softplus(x):
    return jnp.maximum(x, 0.0) + jnp.log(1.0 + jnp.exp(-jnp.abs(x)))


def _rms_fwd(x, g, *, name):
    (h,) = _rowwise(lambda xb, gb: ([_rms(xb, gb)], []), [_full(x)], [_full(g)], [(D, BF16, D, 0, 0)], name=name, tr=512)
    return h


def _residual_rms(r, x, g):
    x_new = x + r
    return [x_new, _rms(x_new, g)], []


def _rms_bwd(dh, x, dres, g):
    _, vjp = jax.vjp(_rms, x, g)
    dx, dg = vjp(dh)
    return [dx + dres], [dg]


def _gate_scalars(raw, al_row, dtb_row):
    lane = lax.broadcasted_iota(jnp.int32, raw.shape, 1)
    beta = jax.nn.sigmoid(raw)
    g = -jnp.exp(al_row) * _softplus(raw + dtb_row)
    return jnp.where(lane < B_HEADS, beta, jnp.where(lane < 2 * B_HEADS, g, 0.0))


def _gated_norm(o, z, w):
    return _rms(o, w) * _silu(z)


def _merge(ga, gb, ta, tb):
    return jax.nn.sigmoid(ga) * ta + jax.nn.sigmoid(gb) * tb


def _swiglu(gu):
    return _silu(gu[:, :D_FF]) * gu[:, D_FF:]


def _head_loss(x2, pg, pp, tgt, g):
    x3 = x2 + jax.nn.sigmoid(pg) * pp
    err = _rms(x3, g) - tgt
    return 0.5 * jnp.sum(jnp.mean(err * err, axis=-1))


CONV_W = 256


def _conv_taps(x, w):
    row = lax.broadcasted_iota(jnp.int32, x.shape, 0)
    shifted = [x] + [jnp.where(row >= s, pltpu.roll(x, s, 0), 0.0) for s in (1, 2, 3)]
    pre = shifted[0] * w[3:4]
    for s in (1, 2, 3):
        pre = pre + shifted[s] * w[3 - s:4 - s]
    return pre, shifted


def _conv_fwd(projp, conv_w, n_batch, seq):
    ncol = B_CONV // CONV_W
    first = P_CONV // CONV_W

    def body(x_ref, w_ref, o_ref):
        pre, _ = _conv_taps(x_ref[...].astype(F32), w_ref[...])
        o_ref[...] = _silu(pre)

    return pl.pallas_call(
        body,
        name="conv_fwd",
        grid=(ncol, n_batch),
        in_specs=[pl.BlockSpec((seq, CONV_W), lambda j, b: (b, first + j)), pl.BlockSpec((4, CONV_W), lambda j, b: (0, j))],
        out_specs=pl.BlockSpec((seq, CONV_W), lambda j, b: (b, j)),
        out_shape=jax.ShapeDtypeStruct((n_batch * seq, B_CONV), F32),
        compiler_params=pltpu.CompilerParams(dimension_semantics=("parallel", "parallel")),
    )(projp, conv_w)


def _conv_bwd(projp, conv_w, dc, n_batch, seq):
    width = dc.shape[1]
    ncol = width // CONV_W
    first_x = P_CONV // CONV_W

    def body(x_ref, w_ref, dc_ref, dx_ref, dw_ref):
        b = pl.program_id(1)
        w = w_ref[...]
        pre, shifted = _conv_taps(x_ref[...].astype(F32), w)
        sg = jax.nn.sigmoid(pre)
        dpre = dc_ref[...] * (sg * (1.0 + pre * (1.0 - sg)))
        row = lax.broadcasted_iota(jnp.int32, dpre.shape, 0)
        dx = dpre * w[3:4]
        for s in (1, 2, 3):
            dx = dx + jnp.where(row < seq - s, pltpu.roll(dpre, seq - s, 0), 0.0) * w[3 - s:4 - s]
        dx_ref[...] = dx.astype(dx_ref.dtype)
        for s in (0, 1, 2, 3):
            part = jnp.sum(dpre * shifted[s], axis=0, keepdims=True)

            @pl.when(b == 0)
            def _():
                dw_ref[3 - s:4 - s, :] = part

            @pl.when(b > 0)
            def _():
                dw_ref[3 - s:4 - s, :] += part

    return pl.pallas_call(
        body,
        name="conv_bwd",
        grid=(ncol, n_batch),
        in_specs=[
            pl.BlockSpec((seq, CONV_W), lambda j, b: (b, first_x + j)),
            pl.BlockSpec((4, CONV_W), lambda j, b: (0, j)),
            pl.BlockSpec((seq, CONV_W), lambda j, b: (b, j)),
        ],
        out_specs=[pl.BlockSpec((seq, CONV_W), lambda j, b: (b, j)), pl.BlockSpec((4, CONV_W), lambda j, b: (0, j))],
        out_shape=[jax.ShapeDtypeStruct((n_batch * seq, width), BF16), jax.ShapeDtypeStruct((4, width), F32)],
        compiler_params=pltpu.CompilerParams(dimension_semantics=("arbitrary", "arbitrary")),
    )(projp, conv_w, dc)


@jax.custom_vjp
def _attend(s, v):
    return _attend_fwd(s, v)[0]


def _attend_fwd(s, v):
    v16 = [t.astype(BF16) for t in v]
    p = [jnp.exp(t - jnp.max(t, axis=-1, keepdims=True)) for t in s]
    p = [t * (1.0 / jnp.sum(t, axis=-1, keepdims=True)) for t in p]
    o = [jnp.dot(t.astype(BF16), v16[n // 2], preferred_element_type=F32) for n, t in enumerate(p)]
    return o, (p, v16, o)


def _attend_bwd(saved, do):
    p, v16, o = saved
    do16 = [t.astype(BF16) for t in do]
    dv = [lax.dot_general(t.astype(BF16), do16[n], TN, preferred_element_type=F32) for n, t in enumerate(p)]
    dp = [lax.dot_general(t, v16[n // 2], NT, preferred_element_type=F32) for n, t in enumerate(do16)]
    delta = [jnp.sum(a * b, axis=-1, keepdims=True) for a, b in zip(do, o)]
    ds = [a * (b - c) for a, b, c in zip(p, dp, delta)]
    return ds, [dv[2 * i] + dv[2 * i + 1] for i in range(len(v16))]


_attend.defvjp(_attend_fwd, _attend_bwd)


def _attn_chunk(qc, kb, vb, bias2, valid, lane_lo):
    sel = (lane_lo, jnp.logical_not(lane_lo))
    items = [(i, e) for i in range(len(qc)) for e in (0, 1)]
    k16 = [t.astype(BF16) for t in kb]
    qm = [(jnp.where(sel[e], qc[i], 0.0) * (A_DIM ** -0.5)).astype(BF16) for i, e in items]
    s = [lax.dot_general(qm[n], k16[i], NT, preferred_element_type=F32) + bias2[e] for n, (i, e) in enumerate(items)]
    if valid is not None:
        s = [jnp.where(valid[i], s[n], -1e30) for n, (i, e) in enumerate(items)]
    o = [jnp.where(sel[e], t, 0.0) for t, (i, e) in zip(_attend(s, vb), items)]
    return [o[2 * i] + o[2 * i + 1] for i in range(len(qc))]


ATTN_GROUP_FWD, ATTN_GROUP_BWD = 4, 4


def _attn_loops(step, n_groups, n_masked):
    lax.fori_loop(0, n_masked, lambda g, c: step(g, c, True), 0)
    lax.fori_loop(n_masked, n_groups, lambda g, c: step(g, c, False), 0)


def _attn_group(g, group, q_ref, kp_ref, vp_ref):
    col = lax.broadcasted_iota(jnp.int32, (CHUNK, A_BAND), 1)
    lane_lo = lax.broadcasted_iota(jnp.int32, (1, LANE), 1) < A_DIM
    starts = [pl.multiple_of((g * group + i) * CHUNK, CHUNK) for i in range(group)]
    rows = [pl.ds(r0, CHUNK) for r0 in starts]
    bands = [pl.ds(r0, A_BAND) for r0 in starts]
    valid = [col + r0 >= A_PAD for r0 in starts]
    loaded = [q_ref[r, :].astype(F32) for r in rows], [kp_ref[b, :] for b in bands], [vp_ref[b, :] for b in bands]
    return rows, bands, loaded, valid, lane_lo


def _attn_specs(seq):
    def blk(first):
        return pl.BlockSpec((seq, LANE), lambda hp, b: (b, first + hp))

    return blk, pl.BlockSpec((2, CHUNK, A_BAND), lambda hp, b: (hp, 0, 0))


def _attn_fwd(projp, bias, n_batch, seq):
    nc = seq // CHUNK
    blk, bias_spec = _attn_specs(seq)

    def body(q_ref, k_ref, v_ref, b_ref, o_ref, kp_ref, vp_ref):
        kp_ref[0:A_PAD, :] = jnp.zeros((A_PAD, LANE), F32)
        vp_ref[0:A_PAD, :] = jnp.zeros((A_PAD, LANE), F32)
        kp_ref[A_PAD:, :] = k_ref[...].astype(F32)
        vp_ref[A_PAD:, :] = v_ref[...].astype(F32)
        bias2 = b_ref[...]

        def step(g, carry, masked):
            rows, _, (qc, kb, vb), valid, lane_lo = _attn_group(g, ATTN_GROUP_FWD, q_ref, kp_ref, vp_ref)
            out = _attn_chunk(qc, kb, vb, bias2, valid if masked else None, lane_lo)
            for r, o in zip(rows, out):
                o_ref[r, :] = o.astype(o_ref.dtype)
            return carry

        _attn_loops(step, nc // ATTN_GROUP_FWD, A_PAD // (CHUNK * ATTN_GROUP_FWD))

    return pl.pallas_call(
        body,
        name="attn_fwd",
        grid=(A_HEADS // 2, n_batch),
        in_specs=[blk(0), blk(4), blk(8), bias_spec],
        out_specs=pl.BlockSpec((seq, LANE), lambda hp, b: (b, hp)),
        out_shape=jax.ShapeDtypeStruct((n_batch * seq, A_WIDTH), BF16),
        scratch_shapes=[pltpu.VMEM((A_PAD + seq, LANE), F32), pltpu.VMEM((A_PAD + seq, LANE), F32)],
        compiler_params=pltpu.CompilerParams(dimension_semantics=("parallel", "parallel")),
    )(projp, projp, projp, bias)


def _attn_bwd(projp, bias, dy, n_batch, seq):
    nc = seq // CHUNK
    blk, bias_spec = _attn_specs(seq)
    out_blk = pl.BlockSpec((seq, LANE), lambda hp, b: (b, hp))

    def body(q_ref, k_ref, v_ref, b_ref, dy_ref, dq_ref, dk_ref, dv_ref, db_ref, kp_ref, vp_ref, dkp_ref, dvp_ref):
        b = pl.program_id(1)
        kp_ref[0:A_PAD, :] = jnp.zeros((A_PAD, LANE), F32)
        vp_ref[0:A_PAD, :] = jnp.zeros((A_PAD, LANE), F32)
        kp_ref[A_PAD:, :] = k_ref[...].astype(F32)
        vp_ref[A_PAD:, :] = v_ref[...].astype(F32)
        dkp_ref[...] = jnp.zeros_like(dkp_ref)
        dvp_ref[...] = jnp.zeros_like(dvp_ref)
        bias2 = b_ref[...]

        @pl.when(b == 0)
        def _():
            db_ref[...] = jnp.zeros_like(db_ref)

        def step(g, carry, masked):
            rows, bands, (qc, kb, vb), valid, lane_lo = _attn_group(g, ATTN_GROUP_BWD, q_ref, kp_ref, vp_ref)
            _, vjp = jax.vjp(lambda q, k, v, bb: _attn_chunk(q, k, v, bb, valid if masked else None, lane_lo), qc, kb, vb, bias2)
            dq, dk, dv, dbias = vjp([dy_ref[r, :] for r in rows])
            for i, r in enumerate(rows):
                dq_ref[r, :] = dq[i].astype(dq_ref.dtype)
            for i, band in enumerate(bands):
                dkp_ref[band, :] += dk[i]
                dvp_ref[band, :] += dv[i]
            db_ref[...] += dbias
            return carry

        _attn_loops(step, nc // ATTN_GROUP_BWD, A_PAD // (CHUNK * ATTN_GROUP_BWD))
        dk_ref[...] = dkp_ref[A_PAD:, :].astype(dk_ref.dtype)
        dv_ref[...] = dvp_ref[A_PAD:, :].astype(dv_ref.dtype)

    n_tok = n_batch * seq
    pad = pltpu.VMEM((A_PAD + seq, LANE), F32)
    return pl.pallas_call(
        body,
        name="attn_bwd",
        grid=(A_HEADS // 2, n_batch),
        in_specs=[blk(0), blk(4), blk(8), bias_spec, out_blk],
        out_specs=[out_blk, out_blk, out_blk, bias_spec],
        out_shape=[jax.ShapeDtypeStruct((n_tok, A_WIDTH), BF16)] * 3 + [jax.ShapeDtypeStruct((A_HEADS, CHUNK, A_BAND), F32)],
        scratch_shapes=[pad, pad, pad, pad],
        compiler_params=pltpu.CompilerParams(dimension_semantics=("arbitrary", "arbitrary")),
    )(projp, projp, projp, bias, dy)


def _rel_bias_table(rel_bias):
    span = CHUNK + A_BAND - 1
    near = REL_CLIP + CHUNK
    far = jnp.broadcast_to(rel_bias[:, 2 * REL_CLIP:], (A_HEADS, span - near))
    t = jnp.concatenate([rel_bias[:, 2 * REL_CLIP + 1 - near:], far], axis=1)
    u = jnp.concatenate([t[:, :A_BAND][:, ::-1], t[:, A_BAND:][:, ::-1]], axis=1)
    rolled = jnp.tile(u, (1, CHUNK))[:, :CHUNK * (span - 1)].reshape(A_HEADS, CHUNK, span - 1)
    return rolled[:, :, :A_BAND]


def _dot(a, b, dn=NN):
    return lax.dot_general(a, b, dn, precision=DELTA_PREC, preferred_element_type=F32)


def _dot16(a, b, dn=NN):
    return lax.dot_general(a.astype(BF16), b.astype(BF16), dn, preferred_element_type=F32)


def _each(fn, *lists):
    return [fn(*vals) for vals in zip(*lists)]


@jax.custom_vjp
def _saved_inverse(x, inv):
    return inv


def _saved_inverse_fwd(x, inv):
    return inv, inv


def _saved_inverse_bwd(inv, ct):
    return _dot(_dot(inv, ct, TN), inv, NT), jnp.zeros_like(inv)


_saved_inverse.defvjp(_saved_inverse_fwd, _saved_inverse_bwd)


def _delta_chunk(r_state, cq, ck, cv, beta, g, saved_inv=None):
    ii = lax.broadcasted_iota(jnp.int32, (CHUNK, CHUNK), 0)
    jj = lax.broadcasted_iota(jnp.int32, (CHUNK, CHUNK), 1)
    incl, strict, eye = ii >= jj, ii > jj, ii == jj
    q = _each(lambda t: t * lax.rsqrt(jnp.sum(t * t, axis=-1, keepdims=True) + EPS) * (B_DIM ** -0.5), cq)
    k = _each(lambda t: t * lax.rsqrt(jnp.sum(t * t, axis=-1, keepdims=True) + EPS), ck)
    g_b = _each(lambda t: jnp.broadcast_to(t, (CHUNK, CHUNK)), g)
    g_row = _each(lambda t: jnp.sum(jnp.where(eye, t, 0.0), axis=0, keepdims=True), g_b)
    gc_col = _each(lambda t: jnp.sum(jnp.where(incl, t, 0.0), axis=1, keepdims=True), g_row)
    gc_row = _each(lambda t: jnp.sum(jnp.where(ii <= jj, t, 0.0), axis=0, keepdims=True), g_b)
    decay = _each(lambda c, r: jnp.where(incl, jnp.exp(jnp.where(incl, c - r, 0.0)), 0.0), gc_col, gc_row)
    kk = _each(lambda t: _dot(t, t, NT), k)
    x = _each(lambda b, m, d: jnp.where(strict, -(b * m * d), 0.0), beta, kk, decay)
    if saved_inv is None:
        inv = _each(lambda t: jnp.where(eye, 1.0, 0.0) + t, x)
        pw = x
        for _ in range(5):
            pw = _each(lambda t: _dot(t, t), pw)
            inv = _each(lambda t, s: t + _dot(t, s), inv, pw)
    else:
        inv = _each(_saved_inverse, x, saved_inv)
    egc = _each(jnp.exp, gc_col)
    u = _each(lambda t, b, v: _dot16(t, b * v), inv, beta, cv)
    wk = _each(lambda t, b, e, kh: _dot16(t, (b * e) * kh), inv, beta, egc, k)
    pqk = _each(lambda qh, kh, d: _dot16(qh, kh, NT) * d, q, k, decay)
    g_last = _each(lambda c: c[CHUNK - 1:CHUNK, :], gc_col)
    kdec = _each(lambda kh, gl, c: kh * jnp.exp(gl - c), k, g_last, gc_col)
    w = _each(lambda uh, wkh, r: uh - _dot16(wkh, r), u, wk, r_state)
    o = _each(lambda e, qh, r, ph, wh: e * _dot16(qh, r) + _dot16(ph, wh), egc, q, r_state, pqk, w)
    r_new = _each(lambda gl, r, kd, wh: jnp.exp(gl) * r + _dot16(kd, wh, TN), g_last, r_state, kdec, w)
    return o, r_new, inv


DELTA_BLK = 512


def _delta_blocks(n_batch, seq):
    nblk = seq // DELTA_BLK
    cpb = DELTA_BLK // CHUNK

    def rows(width, order):
        return pl.BlockSpec((DELTA_BLK, width), lambda b, i: (b * nblk + order(i), 0))

    def states(order, side):
        return pl.BlockSpec((cpb, B_HEADS, side, side), lambda b, i: (b * nblk + order(i), 0, 0, 0))

    return nblk, cpb, rows, states


def _head_cols(h):
    return [pl.ds(part * B_HEADS * B_DIM + h * B_DIM, B_DIM) for part in range(3)]


def _load_heads(c_ref, bg_ref, state_ref, rows):
    bg_c = bg_ref[rows, :]
    cols = [_head_cols(h) for h in range(B_HEADS)]
    return ([state_ref[h] for h in range(B_HEADS)], [c_ref[rows, c[0]] for c in cols], [c_ref[rows, c[1]] for c in cols],
            [c_ref[rows, c[2]] for c in cols], [bg_c[:, h:h + 1] for h in range(B_HEADS)],
            [bg_c[:, B_HEADS + h:B_HEADS + h + 1] for h in range(B_HEADS)])


def _delta_fwd(conv, bg, n_batch, seq):
    nblk, cpb, rows_spec, states_spec = _delta_blocks(n_batch, seq)

    def forward(i):
        return i

    def body(c_ref, bg_ref, o_ref, st_ref, inv_ref, r_ref):
        @pl.when(pl.program_id(1) == 0)
        def _():
            r_ref[...] = jnp.zeros_like(r_ref)

        def step(c, carry):
            rows = pl.ds(pl.multiple_of(c * CHUNK, CHUNK), CHUNK)
            args = _load_heads(c_ref, bg_ref, r_ref, rows)
            o, r_new, inv = _delta_chunk(*args)
            for h in range(B_HEADS):
                st_ref[c, h] = args[0][h]
                inv_ref[c, h] = inv[h]
                o_ref[rows, pl.ds(h * B_DIM, B_DIM)] = o[h]
            for h in range(B_HEADS):
                r_ref[h] = r_new[h]
            return carry

        lax.fori_loop(0, cpb, step, 0)

    n_tok = n_batch * seq
    return pl.pallas_call(
        body,
        name="delta_fwd",
        grid=(n_batch, nblk),
        in_specs=[rows_spec(B_CONV, forward), rows_spec(LANE, forward)],
        out_specs=[rows_spec(B_HEADS * B_DIM, forward), states_spec(forward, B_DIM), states_spec(forward, CHUNK)],
        out_shape=[jax.ShapeDtypeStruct((n_tok, B_HEADS * B_DIM), F32),
                   jax.ShapeDtypeStruct((n_tok // CHUNK, B_HEADS, B_DIM, B_DIM), F32),
                   jax.ShapeDtypeStruct((n_tok // CHUNK, B_HEADS, CHUNK, CHUNK), F32)],
        scratch_shapes=[pltpu.VMEM((B_HEADS, B_DIM, B_DIM), F32)],
        compiler_params=pltpu.CompilerParams(dimension_semantics=("arbitrary", "arbitrary")),
    )(conv, bg)


def _delta_bwd(conv, bg, states, inverses, do, n_batch, seq):
    nblk, cpb, rows_spec, states_spec = _delta_blocks(n_batch, seq)

    def backward(i):
        return nblk - 1 - i

    def body(c_ref, bg_ref, st_ref, inv_ref, do_ref, dc_ref, dbg_ref, dr_ref):
        @pl.when(pl.program_id(1) == 0)
        def _():
            dr_ref[...] = jnp.zeros_like(dr_ref)

        def step(n, carry):
            c = cpb - 1 - n
            rows = pl.ds(pl.multiple_of(c * CHUNK, CHUNK), CHUNK)
            saved = [inv_ref[c, h] for h in range(B_HEADS)]
            _, vjp = jax.vjp(lambda *args: _delta_chunk(*args, saved_inv=saved)[:2],
                             *_load_heads(c_ref, bg_ref, st_ref.at[c], rows))
            do = [do_ref[rows, pl.ds(h * B_DIM, B_DIM)] for h in range(B_HEADS)]
            dr, dq, dk, dv, dbeta, dg = vjp((do, [dr_ref[h] for h in range(B_HEADS)]))
            lane = lax.broadcasted_iota(jnp.int32, (CHUNK, LANE), 1)
            dbg = jnp.zeros((CHUNK, LANE), F32)
            for h in range(B_HEADS):
                cq, ck, cv = _head_cols(h)
                dr_ref[h] = dr[h]
                dc_ref[rows, cq] = dq[h]
                dc_ref[rows, ck] = dk[h]
                dc_ref[rows, cv] = dv[h]
                dbg = dbg + jnp.where(lane == h, dbeta[h], 0.0) + jnp.where(lane == h + B_HEADS, dg[h], 0.0)
            dbg_ref[rows, :] = dbg
            return carry

        lax.fori_loop(0, cpb, step, 0)

    n_tok = n_batch * seq
    return pl.pallas_call(
        body,
        name="delta_bwd",
        grid=(n_batch, nblk),
        in_specs=[rows_spec(B_CONV, backward), rows_spec(LANE, backward), states_spec(backward, B_DIM),
                  states_spec(backward, CHUNK), rows_spec(B_HEADS * B_DIM, backward)],
        out_specs=[rows_spec(B_CONV, backward), rows_spec(LANE, backward)],
        out_shape=[jax.ShapeDtypeStruct((n_tok, B_CONV), F32), jax.ShapeDtypeStruct((n_tok, LANE), F32)],
        scratch_shapes=[pltpu.VMEM((B_HEADS, B_DIM, B_DIM), F32)],
        compiler_params=pltpu.CompilerParams(dimension_semantics=("arbitrary", "arbitrary")),
    )(conv, bg, states, inverses, do)


def _lane_row(vec4, first):
    return jnp.concatenate([jnp.zeros((1, first), F32), vec4.reshape(1, B_HEADS).astype(F32),
                            jnp.zeros((1, LANE - first - B_HEADS), F32)], axis=1)


def _local_step(x3d, p3d, tgt3d, w_in, small, rest_weights, send_grads, send_w_in):
    n_batch, seq, _ = x3d.shape
    n_tok = n_batch * seq
    x, p, tgt = x3d.reshape(n_tok, D), p3d.reshape(n_tok, -1), tgt3d.reshape(n_tok, D)
    g_mix, g_ffn, g_ple, g_final = (small[k].reshape(1, D) for k in ("g_mix", "g_ffn", "g_ple", "g_final"))
    w_onorm = small["w_onorm"].reshape(1, B_DIM)
    al_row = _lane_row(small["a_log"], B_HEADS)
    dtb_row = _lane_row(small["dt_bias"], B_HEADS)
    rel_bias = small["rel_bias"].reshape(A_HEADS, -1)
    bias = _rel_bias_table(rel_bias)
    conv_w = small["conv_w"].reshape(4, B_CONV)

    h1 = _rms_fwd(x, g_mix, name="rms_mix")
    projp = _mm(h1, w_in, tb=True, out_dtype=BF16, name="mm_proj", tn=1920)
    bd = _mm(h1, w_in[P_BD:], tb=True, name="mm_beta_decay", tn=LANE)
    y_a = _attn_fwd(projp, bias, n_batch, seq)
    conv = _conv_fwd(projp, conv_w, n_batch, seq)
    (bg,) = _rowwise(lambda raw, al, dtb: ([_gate_scalars(raw, al, dtb)], []), [_full(bd)],
                     [_full(al_row), _full(dtb_row)], [(LANE, F32, LANE, 0, 0)], name="gate_scalars", tr=1024)
    o_b, states, inverses = _delta_fwd(conv, bg, n_batch, seq)
    (y_b,) = _rowwise(lambda o, z, wn: ([_gated_norm(o, z, wn)], []), [(o_b, LANE, 0, 1), (projp, LANE, P_Z // LANE, 1)],
                      [_full(w_onorm)], [(B_HEADS * B_DIM, BF16, LANE, 0, 1)], name="gated_norm", tr=1024, ncol=B_HEADS)
    w = rest_weights(y_b)
    t_a = _mm(y_a, w["w_branch_a"], tb=True, name="mm_branch_a", tn=1024)
    t_b = _mm(y_b, w["w_branch_b"], tb=True, name="mm_branch_b", tn=1024)
    half = D // 2
    gate_rows = [(projp, half, P_GATE // half, 1), (projp, half, P_GATE // half + 2, 1), (t_a, half, 0, 1), (t_b, half, 0, 1)]
    (merged,) = _rowwise(lambda ga, gb, ta, tb: ([_merge(ga, gb, ta, tb)], []), gate_rows, [], [(D, BF16, half, 0, 1)],
                         name="merge", tr=512, ncol=2)
    x1, h2 = _mm(merged, w["w_out"], name="mm_out", tn=1024, epilogue=_residual_rms, rows=[x], bcs=[g_ffn], outs=(F32, BF16))
    gu = _mm(h2, w["w_gate_up"], tb=True, out_dtype=BF16, name="mm_gate_up", tn=2816)
    (act,) = _rowwise(lambda gub: ([_swiglu(gub)], []), [_full(gu)], [], [(D_FF, BF16, D_FF, 0, 0)], name="swiglu",
                      tr=512)
    x2, h3 = _mm(act, w["w_down"], name="mm_down", tm=512, tn=1024, tk=D_FF, epilogue=_residual_rms, rows=[x1], bcs=[g_ple],
                 outs=(F32, BF16))
    pg = _mm(h3, w["w_ple_gate"], name="mm_ple_gate", tn=1024)
    pp = _mm(p, w["w_ple_proj"], tb=True, name="mm_ple_proj", tn=1024)

    def head_fn(x2b, pgb, ppb, tb, gb):
        loss, (dx2, dpg, dpp, dg) = jax.value_and_grad(_head_loss, argnums=(0, 1, 2, 4))(x2b, pgb, ppb, tb, gb)
        return [dx2, dpg, dpp], [dg, jnp.full((1, LANE), loss, F32)]

    dx3, dpg, dpp, dg_final, loss_row = _rowwise(
        head_fn, [_full(x2), _full(pg), _full(pp), _full(tgt)], [_full(g_final)],
        [(D, F32, D, 0, 0), (D, BF16, D, 0, 0), (D, BF16, D, 0, 0)], [(D, D, 0), (LANE, LANE, 0)], name="loss_head", tr=256)
    gw = {}
    gw["w_ple_proj"] = _mm(dpp, p, ta=True, out_dtype=BF16, name="mm_d_ple_proj", tn=256)
    gw["w_ple_gate"] = _mm(h3, dpg, ta=True, out_dtype=BF16, name="mm_d_ple_gate", tn=512)
    dx2, dg_ple = _mm(dpg, w["w_ple_gate"], tb=True, name="mm_dh3", tm=512, tn=1024, epilogue=_rms_bwd, rows=[x2, dx3],
                      bcs=[g_ple], outs=(F32,), n_red=1)
    gw["w_down"] = _mm(act, dx2, ta=True, out_dtype=BF16, name="mm_d_down", tm=1408, tn=256)
    dact = _mm(dx2, w["w_down"], tb=True, out_dtype=BF16, name="mm_dact", tn=D_FF)

    def swiglu_bwd(gub, dab):
        _, vjp = jax.vjp(_swiglu, gub)
        return [vjp(dab)[0]], []

    (dgu,) = _rowwise(swiglu_bwd, [_full(gu), _full(dact)], [], [(2 * D_FF, BF16, 2 * D_FF, 0, 0)], name="swiglu_bwd", tr=256)
    gw["w_gate_up"] = _mm(dgu, h2, ta=True, out_dtype=BF16, name="mm_d_gate_up", tm=512, tn=1024)
    dx1, dg_ffn = _mm(dgu, w["w_gate_up"], name="mm_dh2", tm=256, tn=1024, tk=2 * D_FF, epilogue=_rms_bwd, rows=[x1, dx2],
                      bcs=[g_ffn], outs=(F32,), n_red=1)
    gw["w_out"] = _mm(merged, dx1, ta=True, out_dtype=BF16, name="mm_d_out", tn=512)
    dmerged = _mm(dx1, w["w_out"], tb=True, name="mm_dmerged", tn=1024)

    def merge_bwd(ga, gb, ta, tb, dm):
        _, vjp = jax.vjp(_merge, ga, gb, ta, tb)
        return list(vjp(dm)), []

    dga, dgb, dta, dtb = _rowwise(merge_bwd, gate_rows + [(dmerged, half, 0, 1)], [], [(D, BF16, half, 0, 1)] * 4,
                                  name="merge_bwd", tr=512, ncol=2)
    gw["w_branch_a"] = _mm(dta, y_a, ta=True, out_dtype=BF16, name="mm_d_branch_a", tn=512)
    gw["w_branch_b"] = _mm(dtb, y_b, ta=True, out_dtype=BF16, name="mm_d_branch_b", tn=512)
    dya = _mm(dta, w["w_branch_a"], name="mm_dya", tn=512)
    dyb = _mm(dtb, w["w_branch_b"], name="mm_dyb", tn=512)

    w_onorm = w_onorm + send_grads(gw)[0, 0]

    def gated_norm_bwd(o, z, dy, wn):
        _, vjp = jax.vjp(_gated_norm, o, z, wn)
        do, dz, dwn = vjp(dy)
        return [do, dz], [dwn]

    do_b, dz, dw_onorm = _rowwise(
        gated_norm_bwd, [(o_b, LANE, 0, 1), (projp, LANE, P_Z // LANE, 1), (dyb, LANE, 0, 1)], [_full(w_onorm)],
        [(B_HEADS * B_DIM, F32, LANE, 0, 1), (B_HEADS * B_DIM, BF16, LANE, 0, 1)], [(B_DIM, B_DIM, 0)],
        name="gated_norm_bwd", tr=1024, ncol=B_HEADS)
    dconv_out, dbg = _delta_bwd(conv, bg, states, inverses, do_b, n_batch, seq)

    def gate_scalars_bwd(raw, dbgb, al, dtb):
        _, vjp = jax.vjp(_gate_scalars, raw, al, dtb)
        draw, dal, ddtb = vjp(dbgb)
        return [draw], [dal, ddtb]

    dbd, dal_row, ddtb_row = _rowwise(gate_scalars_bwd, [_full(bd), _full(dbg)], [_full(al_row), _full(dtb_row)],
                                      [(LANE, BF16, LANE, 0, 0)], [(LANE, LANE, 0), (LANE, LANE, 0)], name="gate_scalars_bwd",
                                      tr=1024)
    dconv, dconv_w = _conv_bwd(projp, conv_w, dconv_out, n_batch, seq)
    dq_a, dk_a, dv_a, dbias = _attn_bwd(projp, bias, dya, n_batch, seq)
    dprojp = jnp.concatenate([dq_a, dk_a, dv_a, dconv, dz, dga, dgb, dbd], axis=1)
    sent = send_w_in(_mm(dprojp, h1, ta=True, out_dtype=BF16, name="mm_d_in", tm=640, tn=1024))
    sent, dprojp = lax.optimization_barrier((sent, dprojp))
    grad_x, dg_mix = _mm(dprojp, w_in, name="mm_dh1", tm=256, tn=1024, tk=P_END, epilogue=_rms_bwd, rows=[x, dx1],
                         bcs=[g_mix + sent[0, 0]], outs=(F32,), n_red=1)

    _, bias_vjp = jax.vjp(_rel_bias_table, rel_bias)
    gs = {
        "g_mix": dg_mix, "g_ffn": dg_ffn, "g_ple": dg_ple, "g_final": dg_final, "w_onorm": dw_onorm,
        "conv_w": dconv_w, "rel_bias": bias_vjp(dbias)[0],
        "a_log": dal_row[0, B_HEADS:2 * B_HEADS], "dt_bias": ddtb_row[0, B_HEADS:2 * B_HEADS],
    }
    return loss_row[:, :1], grad_x.reshape(n_batch, seq, D), gs


MATRICES = (("w_in", 1), ("w_gate_up", 1), ("w_branch_a", 1), ("w_branch_b", 1), ("w_out", 0), ("w_down", 0),
            ("w_ple_gate", 0), ("w_ple_proj", 1))
TAPS_PER_SHARD = B_CONV // N_DEV


def _held(shard, axis):
    return shard if axis == 0 else shard.T


def _from_gathered(slabs):
    return slabs.reshape(-1, slabs.shape[-1])


def _to_owner(held):
    return held.reshape(N_DEV, held.shape[0] // N_DEV, held.shape[1])


def _permute_w_in(held):
    n_gate = P_BD - P_GATE
    row = lax.broadcasted_iota(jnp.int32, (P_END, 1), 0)
    same = jnp.pad(held, ((0, P_END - D_IN), (0, 0)))
    up = jnp.pad(held[8:], ((0, P_END - D_IN + 8), (0, 0)))
    down = jnp.pad(held[:P_GATE + 8], ((n_gate, P_END - P_BD - 8), (0, 0)))
    zero = jnp.zeros((), held.dtype)
    return jnp.where(row < P_GATE, same, jnp.where(row < P_BD, up, jnp.where(row < P_BD + 8, down, zero)))


def _unpermute_w_in(gp):
    n_gate = P_BD - P_GATE
    row = lax.broadcasted_iota(jnp.int32, (D_IN, 1), 0)
    same = gp[:D_IN]
    up = jnp.pad(gp[n_gate:], ((0, D_IN - (P_END - n_gate)), (0, 0)))
    down = jnp.pad(gp[:P_BD], ((8, 0), (0, 0)))
    return jnp.where(row < P_GATE, same, jnp.where(row < P_GATE + 8, up, down))


SMALL_ROWS = 16
SMALL_LAYOUT = (("g_mix", 0, D), ("g_ffn", 1, D), ("g_ple", 2, D), ("g_final", 3, D), ("conv_w", 4, 4 * B_CONV),
                ("rel_bias", 10, A_HEADS * (2 * REL_CLIP + 1)), ("w_onorm", 13, B_DIM), ("a_log", 14, B_HEADS),
                ("dt_bias", 14, B_HEADS), ("loss", 15, 1))


def _pack_small(gs):
    rows = {}
    for name, row, n in SMALL_LAYOUT:
        rows.setdefault(row, []).append(gs[name].reshape(-1).astype(F32))
    parts = []
    for row in sorted(rows):
        flat = jnp.concatenate(rows[row])
        parts.append(jnp.concatenate([flat, jnp.zeros((-flat.shape[0] % D,), F32)]))
    flat = jnp.concatenate(parts)
    assert flat.shape[0] == SMALL_ROWS * D, flat.shape
    return flat.reshape(SMALL_ROWS, D)


def _unpack_small(blk):
    flat, out, used = blk.reshape(-1), {}, {}
    for name, row, n in SMALL_LAYOUT:
        start = row * D + used.get(row, 0)
        out[name] = flat[start:start + n]
        used[row] = used.get(row, 0) + n
    return out


def _position():
    return lax.axis_index("x"), lax.axis_index("y"), lax.axis_index("c")


PEERS = N_DEV - 1


def _comm_call(body, arrays, out_shapes, *, name):
    n = len(arrays)
    return pl.pallas_call(
        body,
        name=name,
        out_shape=out_shapes,
        in_specs=[HBM_SPEC] * n,
        out_specs=[HBM_SPEC] * n,
        scratch_shapes=[pltpu.SemaphoreType.DMA((PEERS * n,)), pltpu.SemaphoreType.DMA((PEERS * n,)),
                        pltpu.SemaphoreType.DMA((n,))],
    )(*arrays)


def _weights_allgather(shards):
    n = len(shards)

    def body(*refs):
        ins, outs = refs[:n], refs[n:2 * n]
        send_sems, recv_sems, local_sems = refs[2 * n:]
        x, y, c = _position()
        me, sibling = (x, y, c), (x, y, 1 - c)
        chips = [(1 - x, y), (x, 1 - y), (1 - x, 1 - y)]

        def slab(a, px, py, pc):
            return outs[a].at[4 * px + 2 * py + pc]

        def copy(a, k, block, to, src=None):
            return pltpu.make_async_remote_copy(src_ref=slab(a, *block) if src is None else src, dst_ref=slab(a, *block),
                                                send_sem=send_sems.at[PEERS * a + k], recv_sem=recv_sems.at[PEERS * a + k],
                                                device_id=to, device_id_type=MESH)

        local = [pltpu.make_async_copy(ins[a], slab(a, *me), local_sems.at[a]) for a in range(n)]
        sent = [copy(a, 1 + j, me, (*chip, c), src=ins[a]) for a in range(n) for j, chip in enumerate(chips)]
        sent += [copy(a, 0, me, sibling, src=ins[a]) for a in range(n)]
        for cp in sent + local:
            cp.start()
        for a in range(n):
            for j, chip in enumerate(chips):
                copy(a, 1 + j, (*chip, c), me).wait_recv()
                passed = copy(a, 4 + j, (*chip, c), sibling)
                passed.start()
                sent.append(passed)
        for a in range(n):
            copy(a, 0, sibling, me).wait_recv()
            for j, chip in enumerate(chips):
                copy(a, 4 + j, (*chip, 1 - c), me).wait_recv()
        for cp in sent:
            cp.wait_send()
        for cp in local:
            cp.wait()

    return _comm_call(body, shards, [jax.ShapeDtypeStruct((N_DEV,) + s.shape, s.dtype) for s in shards],
                      name="weights_allgather")


def _grads_exchange(by_owner):
    n = len(by_owner)

    def body(*refs):
        ins, outs = refs[:n], refs[n:2 * n]
        send_sems, recv_sems, local_sems = refs[2 * n:]
        x, y, c = _position()
        mine = 4 * x + 2 * y + c
        local = [pltpu.make_async_copy(ins[a].at[mine], outs[a].at[mine], local_sems.at[a]) for a in range(n)]
        for cp in local:
            cp.start()
        flips = [(dx, dy, dc) for dx in (0, 1) for dy in (0, 1) for dc in (0, 1) if dx + dy + dc]
        pending = []
        for k, (dx, dy, dc) in enumerate(flips):
            px, py, pc = (1 - x if dx else x), (1 - y if dy else y), (1 - c if dc else c)
            peer = 4 * px + 2 * py + pc
            for a in range(n):
                def remote(slot):
                    return pltpu.make_async_remote_copy(src_ref=ins[a].at[peer], dst_ref=outs[a].at[slot],
                                                        send_sem=send_sems.at[PEERS * a + k], recv_sem=recv_sems.at[PEERS * a + k],
                                                        device_id=(px, py, pc), device_id_type=MESH)

                sent = remote(mine)
                sent.start()
                pending.append((sent, remote(peer)))
        for sent, landed in pending:
            landed.wait_recv()
            sent.wait_send()
        for cp in local:
            cp.wait()

    return _comm_call(body, by_owner, [jax.ShapeDtypeStruct(g.shape, g.dtype) for g in by_owner], name="grads_exchange")


SEM_SPEC = pl.BlockSpec(memory_space=pltpu.SEMAPHORE)
DATAFLOW = pltpu.SideEffectType.DATAFLOW_SIDE_EFFECTING


def _peer_copies(srcs, lands, send_sems, recv_sems, by_owner, arrival):
    x, y, c = _position()
    mine = 4 * x + 2 * y + c
    copies = []
    for k, (dx, dy, dc) in enumerate([(dx, dy, dc) for dx in (0, 1) for dy in (0, 1) for dc in (0, 1) if dx + dy + dc]):
        px, py, pc = (1 - x if dx else x), (1 - y if dy else y), (1 - c if dc else c)
        peer = 4 * px + 2 * py + pc
        for a, (src, land) in enumerate(zip(srcs, lands)):
            copies.append(pltpu.make_async_remote_copy(
                src_ref=src.at[peer] if by_owner else src, dst_ref=land.at[peer if arrival else mine],
                send_sem=send_sems.at[PEERS * a + k], recv_sem=recv_sems.at[PEERS * a + k],
                device_id=(px, py, pc), device_id_type=MESH))
    return copies


def _exchange_start(sources, by_owner, *, name):
    n = len(sources)
    lands = [lax.empty((N_DEV,) + (s.shape[1:] if by_owner else s.shape), s.dtype) for s in sources]

    def body(*refs):
        send_sems, recv_sems, token = refs[2 * n], refs[2 * n + 1], refs[-1]
        for copy in _peer_copies(refs[:n], refs[n:2 * n], send_sems, recv_sems, by_owner, arrival=False):
            copy.start()
        token[...] = jnp.zeros_like(token)

    sems = pltpu.SemaphoreType.DMA((PEERS * n,))
    outs = pl.pallas_call(
        body,
        name=name,
        out_shape=(sems, sems, *[pltpu.HBM(a.shape, a.dtype) for a in sources + lands], jax.ShapeDtypeStruct((8, LANE), F32)),
        in_specs=[HBM_SPEC] * (2 * n),
        out_specs=(SEM_SPEC, SEM_SPEC, *[HBM_SPEC] * (2 * n), pl.BlockSpec(memory_space=pltpu.VMEM)),
        input_output_aliases={i: 2 + i for i in range(2 * n)},
        compiler_params=pltpu.CompilerParams(has_side_effects=DATAFLOW),
    )(*[pltpu.with_memory_space_constraint(a, pltpu.HBM) for a in sources + lands])
    return outs[:-1], outs[-1]


def _exchange_wait(started, after, by_owner, *, name):
    send_sems, recv_sems, *arrays = started
    n = len(arrays) // 2

    def body(*refs):
        for copy in _peer_copies(refs[:n], refs[n:2 * n], refs[2 * n], refs[2 * n + 1], by_owner, arrival=True):
            copy.wait_send()
            copy.wait_recv()

    outs = pl.pallas_call(
        body,
        name=name,
        out_shape=[pltpu.HBM(a.shape, a.dtype) for a in arrays],
        in_specs=[HBM_SPEC] * (2 * n) + [SEM_SPEC, SEM_SPEC, pl.BlockSpec(memory_space=pl.ANY)],
        out_specs=[HBM_SPEC] * (2 * n),
        input_output_aliases={i: i for i in range(2 * n)},
        compiler_params=pltpu.CompilerParams(has_side_effects=DATAFLOW),
    )(*arrays, send_sems, recv_sems, after)
    return outs[:n], outs[n:]


def _slot_sum(g_ref, own_ref):
    if own_ref is not None:
        x, y, c = _position()
        mine = 4 * x + 2 * y + c
    acc = None
    for j in range(N_DEV):
        part = g_ref[j] if own_ref is None else jnp.where(mine == j, own_ref[...], g_ref[j])
        acc = part.astype(F32) if acc is None else acc + part.astype(F32)
    return acc


def _sum_slots(got, *, name, tr):
    _, rows, cols = got.shape
    tr = _tile(rows, tr, 16)

    def body(g_ref, o_ref):
        o_ref[...] = _slot_sum(g_ref, None)

    return pl.pallas_call(
        body,
        name=name,
        grid=(rows // tr,),
        in_specs=[pl.BlockSpec((N_DEV, tr, cols), lambda i: (0, i, 0))],
        out_specs=pl.BlockSpec((tr, cols), lambda i: (i, 0)),
        out_shape=jax.ShapeDtypeStruct((rows, cols), F32),
        compiler_params=pltpu.CompilerParams(dimension_semantics=("parallel",)),
    )(got)


def _adamw(wt, g, m, v, *, name, own=None):
    slots = own is not None
    shape = wt.shape
    two_d = (-1, shape[-1]) if wt.ndim > 1 else (1, -1)
    args = [a.reshape(two_d) for a in (wt, m, v)]
    rows, cols = args[0].shape
    if rows % 16 == 0:
        tr, tc = _tile(rows, 256, 16), cols
    else:
        tr, tc = rows, _tile(cols, 256 if rows > 64 else 512)
    args.insert(1, g.reshape((N_DEV, rows, cols) if slots else (rows, cols)))
    if slots:
        args.append(own.reshape(rows, cols))

    def body(w_ref, g_ref, m_ref, v_ref, *refs):
        go_ref, d_ref, nm_ref, nv_ref = refs[-4:]
        gv = _slot_sum(g_ref, refs[0]) if slots else g_ref[...]
        go_ref[...] = gv
        m2 = ADAM_B1 * m_ref[...] + (1.0 - ADAM_B1) * gv
        v2 = ADAM_B2 * v_ref[...] + (1.0 - ADAM_B2) * (gv * gv)
        m_hat = m2 / (1.0 - ADAM_B1 ** ADAM_STEP)
        v_hat = v2 / (1.0 - ADAM_B2 ** ADAM_STEP)
        d_ref[...] = -ADAM_LR * (m_hat / (jnp.sqrt(v_hat) + ADAM_EPS) + ADAM_WD * w_ref[...])
        nm_ref[...] = m2
        nv_ref[...] = v2

    spec = pl.BlockSpec((tr, tc), lambda i, j: (i, j))
    g_spec = pl.BlockSpec((N_DEV, tr, tc), lambda i, j: (0, i, j)) if slots else spec
    outs = pl.pallas_call(
        body,
        name=name,
        grid=(rows // tr, cols // tc),
        in_specs=[spec, g_spec, spec, spec] + ([spec] if slots else []),
        out_specs=[spec] * 4,
        out_shape=[jax.ShapeDtypeStruct((rows, cols), F32)] * 4,
        compiler_params=pltpu.CompilerParams(dimension_semantics=("parallel", "parallel")),
    )(*args)
    return tuple(o.reshape(shape) for o in outs)


WEIGHTS = ("g_mix", "w_in", "conv_w", "a_log", "dt_bias", "rel_bias", "w_onorm", "w_branch_a", "w_branch_b", "w_out", "g_ffn",
           "w_gate_up", "w_down", "g_ple", "w_ple_gate", "w_ple_proj", "g_final")


def kernel(x, p, g_mix, w_in, conv_w, a_log, dt_bias, rel_bias, w_onorm, w_branch_a, w_branch_b, w_out, g_ffn, w_gate_up, w_down, g_ple, w_ple_gate, w_ple_proj, g_final, loss_target, m_g_mix, m_w_in, m_conv_w, m_a_log, m_dt_bias, m_rel_bias, m_w_onorm, m_w_branch_a, m_w_branch_b, m_w_out, m_g_ffn, m_w_gate_up, m_w_down, m_g_ple, m_w_ple_gate, m_w_ple_proj, m_g_final, v_g_mix, v_w_in, v_conv_w, v_a_log, v_dt_bias, v_rel_bias, v_w_onorm, v_w_branch_a, v_w_branch_b, v_w_out, v_g_ffn, v_w_gate_up, v_w_down, v_g_ple, v_w_ple_gate, v_w_ple_proj, v_g_final):
    given = dict(g_mix=g_mix, w_in=w_in, conv_w=conv_w, a_log=a_log, dt_bias=dt_bias, rel_bias=rel_bias, w_onorm=w_onorm,
                 w_branch_a=w_branch_a, w_branch_b=w_branch_b, w_out=w_out, g_ffn=g_ffn, w_gate_up=w_gate_up, w_down=w_down,
                 g_ple=g_ple, w_ple_gate=w_ple_gate, w_ple_proj=w_ple_proj, g_final=g_final)
    mom1 = dict(g_mix=m_g_mix, w_in=m_w_in, conv_w=m_conv_w, a_log=m_a_log, dt_bias=m_dt_bias, rel_bias=m_rel_bias,
                w_onorm=m_w_onorm, w_branch_a=m_w_branch_a, w_branch_b=m_w_branch_b, w_out=m_w_out, g_ffn=m_g_ffn,
                w_gate_up=m_w_gate_up, w_down=m_w_down, g_ple=m_g_ple, w_ple_gate=m_w_ple_gate, w_ple_proj=m_w_ple_proj,
                g_final=m_g_final)
    mom2 = dict(g_mix=v_g_mix, w_in=v_w_in, conv_w=v_conv_w, a_log=v_a_log, dt_bias=v_dt_bias, rel_bias=v_rel_bias,
                w_onorm=v_w_onorm, w_branch_a=v_w_branch_a, w_branch_b=v_w_branch_b, w_out=v_w_out, g_ffn=v_g_ffn,
                w_gate_up=v_w_gate_up, w_down=v_w_down, g_ple=v_g_ple, w_ple_gate=v_w_ple_gate, w_ple_proj=v_w_ple_proj,
                g_final=v_g_final)
    mine = 4 * lax.axis_index("x") + 2 * lax.axis_index("y") + lax.axis_index("c")

    my_slot = (jnp.arange(N_DEV) == mine)[:, None, None]
    rest = MATRICES[1:]
    in_flight = {}

    got_in, got_taps = _weights_allgather([_held(w_in[0], 1).astype(BF16), conv_w[0]])
    in_flight["weights"], weights_sent = _exchange_start([_held(given[name][0], axis).astype(BF16) for name, axis in rest], False,
                                                         name="weights_start")
    small = dict(g_mix=g_mix + weights_sent[0, 0], g_ffn=g_ffn, g_ple=g_ple, g_final=g_final, w_onorm=w_onorm, a_log=a_log,
                 dt_bias=dt_bias, rel_bias=rel_bias, conv_w=jnp.transpose(got_taps, (1, 0, 2)).reshape(4, B_CONV))

    def rest_weights(after):
        shards, landed = _exchange_wait(in_flight.pop("weights"), after, False, name="weights_wait")
        return {name: _from_gathered(jnp.where(my_slot, shard[None], slabs)) for (name, _), shard, slabs in zip(rest, shards, landed)}

    def send_grads(gw):
        in_flight["grads"], sent = _exchange_start([_to_owner(gw[name]) for name, _ in rest], True, name="grads_start")
        return sent

    def send_w_in(g_in):
        in_flight["grad_in"], sent = _exchange_start([_to_owner(_unpermute_w_in(g_in))], True, name="grad_in_start")
        return sent

    loss_part, grad_x, gs = _local_step(x, p[0], loss_target, _permute_w_in(_from_gathered(got_in)), small, rest_weights,
                                        send_grads, send_w_in)
    gs["loss"] = loss_part

    updates = {}

    def update_matrices(matrices, own_slabs, landed):
        for (name, axis), own_slab, slots in zip(matrices, own_slabs, landed):
            mine_of = lax.dynamic_index_in_dim(own_slab, mine, axis=0, keepdims=False)
            w_held, m_held, v_held = (_held(a[name][0], axis) for a in (given, mom1, mom2))
            outs = _adamw(w_held, slots, m_held, v_held, name=f"adamw_{name}", own=mine_of)
            updates[name] = tuple(_held(o, axis)[None] for o in outs)

    update_matrices(rest, *_exchange_wait(in_flight["grads"], grad_x, True, name="grads_wait"))
    update_matrices(MATRICES[:1], *_exchange_wait(in_flight["grad_in"], updates[rest[-1][0]][0], True, name="grad_in_wait"))
    small_block, _ = lax.optimization_barrier((_pack_small(gs), updates["w_in"][0]))
    (got_small,) = _grads_exchange([jnp.broadcast_to(small_block, (N_DEV, SMALL_ROWS, D))])
    small_sum = _unpack_small(_sum_slots(got_small, name="sum_small_grads", tr=16))
    loss = small_sum.pop("loss")[0]
    conv_all = small_sum.pop("conv_w").reshape(4, N_DEV, TAPS_PER_SHARD)
    small_sum["conv_w"] = lax.dynamic_index_in_dim(conv_all, mine, axis=1, keepdims=False)
    for name, g in small_sum.items():
        updates[name] = _adamw(given[name], g.reshape(given[name].shape), mom1[name], mom2[name], name=f"adamw_{name}")
    return (loss, grad_x, *[updates[name][k] for k in range(4) for name in WEIGHTS])
```

```python
import jax
import jax.numpy as jnp
from jax import lax
from jax.experimental import pallas as pl
from jax.experimental.pallas import tpu as pltpu

F32 = jnp.float32
BF16 = jnp.bfloat16
DELTA_PREC = lax.Precision.HIGH
MESH = pl.DeviceIdType.MESH

N_DEV = 8
D = 1024
CHUNK = 64
EPS = 1e-6
A_HEADS, A_DIM, A_WIDTH = 8, 64, 512
A_BAND = 9 * CHUNK
A_PAD = 8 * CHUNK
REL_CLIP = 128
B_HEADS, B_DIM = 4, 128
B_CONV = 1536
D_FF = 2816
D_IN = 5640
P_CONV, P_Z, P_GATE, P_BD, P_END = 1536, 3072, 3584, 5632, 5760
LANE = 128

ADAM_LR, ADAM_B1, ADAM_B2, ADAM_EPS, ADAM_WD, ADAM_STEP = 0.001, 0.9, 0.999, 1e-08, 0.01, 10

NT = (((1,), (1,)), ((), ()))
TN = (((0,), (0,)), ((), ()))
NN = (((1,), (0,)), ((), ()))

HBM_SPEC = pl.BlockSpec(memory_space=pltpu.HBM)


def _tile(n, target, align=LANE):
    if n <= target:
        return n
    best = None
    for t in range(align, target + 1, align):
        if n % t == 0:
            best = t
    assert best is not None, (n, target, align)
    return best


def _mm(a, b, *, name, ta=False, tb=False, out_dtype=F32, tm=1024, tn=640, tk=None, epilogue=None, rows=(), bcs=(), outs=(),
        n_red=0):
    assert not (ta and tb)
    if ta:
        k_dim, m_dim = a.shape
    else:
        m_dim, k_dim = a.shape
    n_dim = b.shape[0] if tb else b.shape[1]
    assert b.shape[1 if tb else 0] == k_dim
    tm, tn = _tile(m_dim, tm), _tile(n_dim, tn)
    tk = _tile(k_dim, tk or (4096 if ta else 1024), 8 if ta else LANE)
    nk = k_dim // tk
    dn = TN if ta else NT if tb else NN
    if epilogue is None:
        outs = (out_dtype,)
    assert not n_red or tn == n_dim
    n_extra, n_out = len(rows) + len(bcs), len(outs)

    def body(a_ref, b_ref, *refs):
        part = lax.dot_general(a_ref[...].astype(BF16), b_ref[...].astype(BF16), dn, preferred_element_type=F32)

        def finish(r):
            o_vals, r_vals = ([r], []) if epilogue is None else epilogue(r, *[x[...] for x in refs[:n_extra]])
            for ref, val in zip(refs[n_extra:n_extra + n_out], o_vals):
                ref[...] = val.astype(ref.dtype)
            first = pl.program_id(0) == 0
            for ref, val in zip(refs[n_extra + n_out:n_extra + n_out + n_red], r_vals):
                @pl.when(first)
                def _():
                    ref[...] = val

                @pl.when(jnp.logical_not(first))
                def _():
                    ref[...] += val

        if nk == 1:
            finish(part)
        else:
            acc_ref = refs[-1]
            k = pl.program_id(2)

            @pl.when(k == 0)
            def _():
                acc_ref[...] = part

            @pl.when(k > 0)
            def _():
                acc_ref[...] += part

            @pl.when(k == nk - 1)
            def _():
                finish(acc_ref[...])

    tile = pl.BlockSpec((tm, tn), lambda i, j, k: (i, j))
    col = pl.BlockSpec((1, tn), lambda i, j, k: (0, j))
    a_spec = pl.BlockSpec((tk, tm), lambda i, j, k: (k, i)) if ta else pl.BlockSpec((tm, tk), lambda i, j, k: (i, k))
    b_spec = pl.BlockSpec((tn, tk), lambda i, j, k: (j, k)) if tb else pl.BlockSpec((tk, tn), lambda i, j, k: (k, j))
    result = pl.pallas_call(
        body,
        name=name,
        grid=(m_dim // tm, n_dim // tn, nk),
        in_specs=[a_spec, b_spec] + [tile] * len(rows) + [col] * len(bcs),
        out_specs=[tile] * n_out + [col] * n_red,
        out_shape=[jax.ShapeDtypeStruct((m_dim, n_dim), dt) for dt in outs] + [jax.ShapeDtypeStruct((1, n_dim), F32)] * n_red,
        scratch_shapes=[pltpu.VMEM((tm, tn), F32)] if nk > 1 else [],
        compiler_params=pltpu.CompilerParams(dimension_semantics=("arbitrary",) * 3 if n_red else ("parallel", "parallel", "arbitrary")),
    )(a, b, *rows, *bcs)
    return result[0] if epilogue is None else result


def _rowwise(fn, rows, bcs, outs, reds=(), *, name, tr, ncol=1):
    n_rows = rows[0][0].shape[0]
    tr = _tile(n_rows, tr, 8)
    nrow = n_rows // tr
    n_in, n_out = len(rows) + len(bcs), len(outs)

    def body(*refs):
        j, i = pl.program_id(0), pl.program_id(1)
        o_vals, r_vals = fn(*[r[...].astype(F32) for r in refs[:n_in]])
        for ref, val in zip(refs[n_in:n_in + n_out], o_vals):
            ref[...] = val.astype(ref.dtype)
        for ref, val, (_, _, stride) in zip(refs[n_in + n_out:], r_vals, reds):
            first = (i == 0) if stride else jnp.logical_and(i == 0, j == 0)

            @pl.when(first)
            def _():
                ref[...] = val

            @pl.when(jnp.logical_not(first))
            def _():
                ref[...] += val

    def spec(r, w, off, st, row_dep=True):
        if row_dep:
            return pl.BlockSpec((r, w), lambda j, i: (i, off + st * j))
        return pl.BlockSpec((r, w), lambda j, i: (0, off + st * j))

    in_specs = [spec(tr, w, off, st) for (_, w, off, st) in rows]
    in_specs += [spec(a.shape[0], w, off, st, False) for (a, w, off, st) in bcs]
    out_specs = [spec(tr, w, off, st) for (_, _, w, off, st) in outs]
    out_specs += [spec(1, w, 0, st, False) for (_, w, st) in reds]
    out_shape = [jax.ShapeDtypeStruct((n_rows, c), dt) for (c, dt, _, _, _) in outs]
    out_shape += [jax.ShapeDtypeStruct((1, c), F32) for (c, _, _) in reds]
    return pl.pallas_call(
        body,
        name=name,
        grid=(ncol, nrow),
        in_specs=in_specs,
        out_specs=out_specs,
        out_shape=out_shape,
        compiler_params=pltpu.CompilerParams(dimension_semantics=("arbitrary", "arbitrary")),
    )(*[r[0] for r in rows], *[b[0] for b in bcs])


def _full(a):
    return (a, a.shape[1], 0, 0)


def _rms(x, g):
    return x * lax.rsqrt(jnp.mean(x * x, axis=-1, keepdims=True) + EPS) * g


def _silu(x):
    return x * jax.nn.sigmoid(x)


def _softplus(x):
    return jnp.maximum(x, 0.0) + jnp.log(1.0 + jnp.exp(-jnp.abs(x)))


def _rms_fwd(x, g, *, name):
    (h,) = _rowwise(lambda xb, gb: ([_rms(xb, gb)], []), [_full(x)], [_full(g)], [(D, BF16, D, 0, 0)], name=name, tr=512)
    return h


def _residual_rms(r, x, g):
    x_new = x + r
    return [x_new, _rms(x_new, g)], []


def _rms_bwd(dh, x, dres, g):
    _, vjp = jax.vjp(_rms, x, g)
    dx, dg = vjp(dh)
    return [dx + dres], [dg]


def _gate_scalars(raw, al_row, dtb_row):
    lane = lax.broadcasted_iota(jnp.int32, raw.shape, 1)
    beta = jax.nn.sigmoid(raw)
    g = -jnp.exp(al_row) * _softplus(raw + dtb_row)
    return jnp.where(lane < B_HEADS, beta, jnp.where(lane < 2 * B_HEADS, g, 0.0))


def _gated_norm(o, z, w):
    return _rms(o, w) * _silu(z)


def _merge(ga, gb, ta, tb):
    return jax.nn.sigmoid(ga) * ta + jax.nn.sigmoid(gb) * tb


def _swiglu(gu):
    return _silu(gu[:, :D_FF]) * gu[:, D_FF:]


def _head_loss(x2, pg, pp, tgt, g):
    x3 = x2 + jax.nn.sigmoid(pg) * pp
    err = _rms(x3, g) - tgt
    return 0.5 * jnp.sum(jnp.mean(err * err, axis=-1))


CONV_W = 256


def _conv_taps(x, w):
    row = lax.broadcasted_iota(jnp.int32, x.shape, 0)
    shifted = [x] + [jnp.where(row >= s, pltpu.roll(x, s, 0), 0.0) for s in (1, 2, 3)]
    pre = shifted[0] * w[3:4]
    for s in (1, 2, 3):
        pre = pre + shifted[s] * w[3 - s:4 - s]
    return pre, shifted


def _conv_fwd(projp, conv_w, n_batch, seq):
    ncol = B_CONV // CONV_W
    first = P_CONV // CONV_W

    def body(x_ref, w_ref, o_ref):
        pre, _ = _conv_taps(x_ref[...].astype(F32), w_ref[...])
        o_ref[...] = _silu(pre)

    return pl.pallas_call(
        body,
        name="conv_fwd",
        grid=(ncol, n_batch),
        in_specs=[pl.BlockSpec((seq, CONV_W), lambda j, b: (b, first + j)), pl.BlockSpec((4, CONV_W), lambda j, b: (0, j))],
        out_specs=pl.BlockSpec((seq, CONV_W), lambda j, b: (b, j)),
        out_shape=jax.ShapeDtypeStruct((n_batch * seq, B_CONV), F32),
        compiler_params=pltpu.CompilerParams(dimension_semantics=("parallel", "parallel")),
    )(projp, conv_w)


def _conv_bwd(projp, conv_w, dc, n_batch, seq):
    width = dc.shape[1]
    ncol = width // CONV_W
    first_x = P_CONV // CONV_W

    def body(x_ref, w_ref, dc_ref, dx_ref, dw_ref):
        b = pl.program_id(1)
        w = w_ref[...]
        pre, shifted = _conv_taps(x_ref[...].astype(F32), w)
        sg = jax.nn.sigmoid(pre)
        dpre = dc_ref[...] * (sg * (1.0 + pre * (1.0 - sg)))
        row = lax.broadcasted_iota(jnp.int32, dpre.shape, 0)
        dx = dpre * w[3:4]
        for s in (1, 2, 3):
            dx = dx + jnp.where(row < seq - s, pltpu.roll(dpre, seq - s, 0), 0.0) * w[3 - s:4 - s]
        dx_ref[...] = dx.astype(dx_ref.dtype)
        for s in (0, 1, 2, 3):
            part = jnp.sum(dpre * shifted[s], axis=0, keepdims=True)

            @pl.when(b == 0)
            def _():
                dw_ref[3 - s:4 - s, :] = part

            @pl.when(b > 0)
            def _():
                dw_ref[3 - s:4 - s, :] += part

    return pl.pallas_call(
        body,
        name="conv_bwd",
        grid=(ncol, n_batch),
        in_specs=[
            pl.BlockSpec((seq, CONV_W), lambda j, b: (b, first_x + j)),
            pl.BlockSpec((4, CONV_W), lambda j, b: (0, j)),
            pl.BlockSpec((seq, CONV_W), lambda j, b: (b, j)),
        ],
        out_specs=[pl.BlockSpec((seq, CONV_W), lambda j, b: (b, j)), pl.BlockSpec((4, CONV_W), lambda j, b: (0, j))],
        out_shape=[jax.ShapeDtypeStruct((n_batch * seq, width), BF16), jax.ShapeDtypeStruct((4, width), F32)],
        compiler_params=pltpu.CompilerParams(dimension_semantics=("arbitrary", "arbitrary")),
    )(projp, conv_w, dc)


@jax.custom_vjp
def _attend(s, v):
    return _attend_fwd(s, v)[0]


def _attend_fwd(s, v):
    v16 = [t.astype(BF16) for t in v]
    p = [jnp.exp(t - jnp.max(t, axis=-1, keepdims=True)) for t in s]
    p = [t * (1.0 / jnp.sum(t, axis=-1, keepdims=True)) for t in p]
    o = [jnp.dot(t.astype(BF16), v16[n // 2], preferred_element_type=F32) for n, t in enumerate(p)]
    return o, (p, v16, o)


def _attend_bwd(saved, do):
    p, v16, o = saved
    do16 = [t.astype(BF16) for t in do]
    dv = [lax.dot_general(t.astype(BF16), do16[n], TN, preferred_element_type=F32) for n, t in enumerate(p)]
    dp = [lax.dot_general(t, v16[n // 2], NT, preferred_element_type=F32) for n, t in enumerate(do16)]
    delta = [jnp.sum(a * b, axis=-1, keepdims=True) for a, b in zip(do, o)]
    ds = [a * (b - c) for a, b, c in zip(p, dp, delta)]
    return ds, [dv[2 * i] + dv[2 * i + 1] for i in range(len(v16))]


_attend.defvjp(_attend_fwd, _attend_bwd)


def _attn_chunk(qc, kb, vb, bias2, valid, lane_lo):
    sel = (lane_lo, jnp.logical_not(lane_lo))
    items = [(i, e) for i in range(len(qc)) for e in (0, 1)]
    k16 = [t.astype(BF16) for t in kb]
    qm = [(jnp.where(sel[e], qc[i], 0.0) * (A_DIM ** -0.5)).astype(BF16) for i, e in items]
    s = [lax.dot_general(qm[n], k16[i], NT, preferred_element_type=F32) + bias2[e] for n, (i, e) in enumerate(items)]
    if valid is not None:
        s = [jnp.where(valid[i], s[n], -1e30) for n, (i, e) in enumerate(items)]
    o = [jnp.where(sel[e], t, 0.0) for t, (i, e) in zip(_attend(s, vb), items)]
    return [o[2 * i] + o[2 * i + 1] for i in range(len(qc))]


ATTN_GROUP_FWD, ATTN_GROUP_BWD = 8, 4


def _attn_loops(step, n_groups, n_masked):
    lax.fori_loop(0, n_masked, lambda g, c: step(g, c, True), 0)
    lax.fori_loop(n_masked, n_groups, lambda g, c: step(g, c, False), 0)


def _attn_group(g, group, q_ref, kp_ref, vp_ref):
    col = lax.broadcasted_iota(jnp.int32, (CHUNK, A_BAND), 1)
    lane_lo = lax.broadcasted_iota(jnp.int32, (1, LANE), 1) < A_DIM
    starts = [pl.multiple_of((g * group + i) * CHUNK, CHUNK) for i in range(group)]
    rows = [pl.ds(r0, CHUNK) for r0 in starts]
    bands = [pl.ds(r0, A_BAND) for r0 in starts]
    valid = [col + r0 >= A_PAD for r0 in starts]
    loaded = [q_ref[r, :].astype(F32) for r in rows], [kp_ref[b, :] for b in bands], [vp_ref[b, :] for b in bands]
    return rows, bands, loaded, valid, lane_lo


def _attn_specs(seq):
    def blk(first):
        return pl.BlockSpec((seq, LANE), lambda hp, b: (b, first + hp))

    return blk, pl.BlockSpec((2, CHUNK, A_BAND), lambda hp, b: (hp, 0, 0))


def _attn_fwd(projp, bias, n_batch, seq):
    nc = seq // CHUNK
    blk, bias_spec = _attn_specs(seq)

    def body(q_ref, k_ref, v_ref, b_ref, o_ref, kp_ref, vp_ref):
        kp_ref[0:A_PAD, :] = jnp.zeros((A_PAD, LANE), F32)
        vp_ref[0:A_PAD, :] = jnp.zeros((A_PAD, LANE), F32)
        kp_ref[A_PAD:, :] = k_ref[...].astype(F32)
        vp_ref[A_PAD:, :] = v_ref[...].astype(F32)
        bias2 = b_ref[...]

        def step(g, carry, masked):
            rows, _, (qc, kb, vb), valid, lane_lo = _attn_group(g, ATTN_GROUP_FWD, q_ref, kp_ref, vp_ref)
            out = _attn_chunk(qc, kb, vb, bias2, valid if masked else None, lane_lo)
            for r, o in zip(rows, out):
                o_ref[r, :] = o.astype(o_ref.dtype)
            return carry

        _attn_loops(step, nc // ATTN_GROUP_FWD, A_PAD // (CHUNK * ATTN_GROUP_FWD))

    return pl.pallas_call(
        body,
        name="attn_fwd",
        grid=(A_HEADS // 2, n_batch),
        in_specs=[blk(0), blk(4), blk(8), bias_spec],
        out_specs=pl.BlockSpec((seq, LANE), lambda hp, b: (b, hp)),
        out_shape=jax.ShapeDtypeStruct((n_batch * seq, A_WIDTH), BF16),
        scratch_shapes=[pltpu.VMEM((A_PAD + seq, LANE), F32), pltpu.VMEM((A_PAD + seq, LANE), F32)],
        compiler_params=pltpu.CompilerParams(dimension_semantics=("parallel", "parallel")),
    )(projp, projp, projp, bias)


def _attn_bwd(projp, bias, dy, n_batch, seq):
    nc = seq // CHUNK
    blk, bias_spec = _attn_specs(seq)
    out_blk = pl.BlockSpec((seq, LANE), lambda hp, b: (b, hp))

    def body(q_ref, k_ref, v_ref, b_ref, dy_ref, dq_ref, dk_ref, dv_ref, db_ref, kp_ref, vp_ref, dkp_ref, dvp_ref):
        b = pl.program_id(1)
        kp_ref[0:A_PAD, :] = jnp.zeros((A_PAD, LANE), F32)
        vp_ref[0:A_PAD, :] = jnp.zeros((A_PAD, LANE), F32)
        kp_ref[A_PAD:, :] = k_ref[...].astype(F32)
        vp_ref[A_PAD:, :] = v_ref[...].astype(F32)
        dkp_ref[...] = jnp.zeros_like(dkp_ref)
        dvp_ref[...] = jnp.zeros_like(dvp_ref)
        bias2 = b_ref[...]

        @pl.when(b == 0)
        def _():
            db_ref[...] = jnp.zeros_like(db_ref)

        def step(g, carry, masked):
            rows, bands, (qc, kb, vb), valid, lane_lo = _attn_group(g, ATTN_GROUP_BWD, q_ref, kp_ref, vp_ref)
            _, vjp = jax.vjp(lambda q, k, v, bb: _attn_chunk(q, k, v, bb, valid if masked else None, lane_lo), qc, kb, vb, bias2)
            dq, dk, dv, dbias = vjp([dy_ref[r, :] for r in rows])
            for i, r in enumerate(rows):
                dq_ref[r, :] = dq[i].astype(dq_ref.dtype)
            for i, band in enumerate(bands):
                dkp_ref[band, :] += dk[i]
                dvp_ref[band, :] += dv[i]
            db_ref[...] += dbias
            return carry

        _attn_loops(step, nc // ATTN_GROUP_BWD, A_PAD // (CHUNK * ATTN_GROUP_BWD))
        dk_ref[...] = dkp_ref[A_PAD:, :].astype(dk_ref.dtype)
        dv_ref[...] = dvp_ref[A_PAD:, :].astype(dv_ref.dtype)

    n_tok = n_batch * seq
    pad = pltpu.VMEM((A_PAD + seq, LANE), F32)
    return pl.pallas_call(
        body,
        name="attn_bwd",
        grid=(A_HEADS // 2, n_batch),
        in_specs=[blk(0), blk(4), blk(8), bias_spec, out_blk],
        out_specs=[out_blk, out_blk, out_blk, bias_spec],
        out_shape=[jax.ShapeDtypeStruct((n_tok, A_WIDTH), BF16)] * 3 + [jax.ShapeDtypeStruct((A_HEADS, CHUNK, A_BAND), F32)],
        scratch_shapes=[pad, pad, pad, pad],
        compiler_params=pltpu.CompilerParams(dimension_semantics=("arbitrary", "arbitrary")),
    )(projp, projp, projp, bias, dy)


def _rel_bias_table(rel_bias):
    span = CHUNK + A_BAND - 1
    near = REL_CLIP + CHUNK
    far = jnp.broadcast_to(rel_bias[:, 2 * REL_CLIP:], (A_HEADS, span - near))
    t = jnp.concatenate([rel_bias[:, 2 * REL_CLIP + 1 - near:], far], axis=1)
    u = jnp.concatenate([t[:, :A_BAND][:, ::-1], t[:, A_BAND:][:, ::-1]], axis=1)
    rolled = jnp.tile(u, (1, CHUNK))[:, :CHUNK * (span - 1)].reshape(A_HEADS, CHUNK, span - 1)
    return rolled[:, :, :A_BAND]


def _dot(a, b, dn=NN):
    return lax.dot_general(a, b, dn, precision=DELTA_PREC, preferred_element_type=F32)


def _dot16(a, b, dn=NN):
    return lax.dot_general(a.astype(BF16), b.astype(BF16), dn, preferred_element_type=F32)


def _each(fn, *lists):
    return [fn(*vals) for vals in zip(*lists)]


@jax.custom_vjp
def _saved_inverse(x, inv):
    return inv


def _saved_inverse_fwd(x, inv):
    return inv, inv


def _saved_inverse_bwd(inv, ct):
    return _dot(_dot(inv, ct, TN), inv, NT), jnp.zeros_like(inv)


_saved_inverse.defvjp(_saved_inverse_fwd, _saved_inverse_bwd)


def _delta_chunk(r_state, cq, ck, cv, beta, g, saved_inv=None):
    ii = lax.broadcasted_iota(jnp.int32, (CHUNK, CHUNK), 0)
    jj = lax.broadcasted_iota(jnp.int32, (CHUNK, CHUNK), 1)
    incl, strict, eye = ii >= jj, ii > jj, ii == jj
    q = _each(lambda t: t * lax.rsqrt(jnp.sum(t * t, axis=-1, keepdims=True) + EPS) * (B_DIM ** -0.5), cq)
    k = _each(lambda t: t * lax.rsqrt(jnp.sum(t * t, axis=-1, keepdims=True) + EPS), ck)
    g_b = _each(lambda t: jnp.broadcast_to(t, (CHUNK, CHUNK)), g)
    g_row = _each(lambda t: jnp.sum(jnp.where(eye, t, 0.0), axis=0, keepdims=True), g_b)
    gc_col = _each(lambda t: jnp.sum(jnp.where(incl, t, 0.0), axis=1, keepdims=True), g_row)
    gc_row = _each(lambda t: jnp.sum(jnp.where(ii <= jj, t, 0.0), axis=0, keepdims=True), g_b)
    decay = _each(lambda c, r: jnp.where(incl, jnp.exp(jnp.where(incl, c - r, 0.0)), 0.0), gc_col, gc_row)
    kk = _each(lambda t: _dot(t, t, NT), k)
    x = _each(lambda b, m, d: jnp.where(strict, -(b * m * d), 0.0), beta, kk, decay)
    if saved_inv is None:
        inv = _each(lambda t: jnp.where(eye, 1.0, 0.0) + t, x)
        pw = x
        for _ in range(5):
            pw = _each(lambda t: _dot(t, t), pw)
            inv = _each(lambda t, s: t + _dot(t, s), inv, pw)
    else:
        inv = _each(_saved_inverse, x, saved_inv)
    egc = _each(jnp.exp, gc_col)
    u = _each(lambda t, b, v: _dot16(t, b * v), inv, beta, cv)
    wk = _each(lambda t, b, e, kh: _dot16(t, (b * e) * kh), inv, beta, egc, k)
    pqk = _each(lambda qh, kh, d: _dot16(qh, kh, NT) * d, q, k, decay)
    g_last = _each(lambda c: c[CHUNK - 1:CHUNK, :], gc_col)
    kdec = _each(lambda kh, gl, c: kh * jnp.exp(gl - c), k, g_last, gc_col)
    w = _each(lambda uh, wkh, r: uh - _dot16(wkh, r), u, wk, r_state)
    o = _each(lambda e, qh, r, ph, wh: e * _dot16(qh, r) + _dot16(ph, wh), egc, q, r_state, pqk, w)
    r_new = _each(lambda gl, r, kd, wh: jnp.exp(gl) * r + _dot16(kd, wh, TN), g_last, r_state, kdec, w)
    return o, r_new, inv


DELTA_BLK = 512


def _delta_blocks(n_batch, seq):
    nblk = seq // DELTA_BLK
    cpb = DELTA_BLK // CHUNK

    def rows(width, order):
        return pl.BlockSpec((DELTA_BLK, width), lambda b, i: (b * nblk + order(i), 0))

    def states(order, side):
        return pl.BlockSpec((cpb, B_HEADS, side, side), lambda b, i: (b * nblk + order(i), 0, 0, 0))

    return nblk, cpb, rows, states


def _head_cols(h):
    return [pl.ds(part * B_HEADS * B_DIM + h * B_DIM, B_DIM) for part in range(3)]


def _load_heads(c_ref, bg_ref, state_ref, rows):
    bg_c = bg_ref[rows, :]
    cols = [_head_cols(h) for h in range(B_HEADS)]
    return ([state_ref[h] for h in range(B_HEADS)], [c_ref[rows, c[0]] for c in cols], [c_ref[rows, c[1]] for c in cols],
            [c_ref[rows, c[2]] for c in cols], [bg_c[:, h:h + 1] for h in range(B_HEADS)],
            [bg_c[:, B_HEADS + h:B_HEADS + h + 1] for h in range(B_HEADS)])


def _delta_fwd(conv, bg, n_batch, seq):
    nblk, cpb, rows_spec, states_spec = _delta_blocks(n_batch, seq)

    def forward(i):
        return i

    def body(c_ref, bg_ref, o_ref, st_ref, inv_ref, r_ref):
        @pl.when(pl.program_id(1) == 0)
        def _():
            r_ref[...] = jnp.zeros_like(r_ref)

        def step(c, carry):
            rows = pl.ds(pl.multiple_of(c * CHUNK, CHUNK), CHUNK)
            args = _load_heads(c_ref, bg_ref, r_ref, rows)
            o, r_new, inv = _delta_chunk(*args)
            for h in range(B_HEADS):
                st_ref[c, h] = args[0][h]
                inv_ref[c, h] = inv[h]
                o_ref[rows, pl.ds(h * B_DIM, B_DIM)] = o[h]
            for h in range(B_HEADS):
                r_ref[h] = r_new[h]
            return carry

        lax.fori_loop(0, cpb, step, 0)

    n_tok = n_batch * seq
    return pl.pallas_call(
        body,
        name="delta_fwd",
        grid=(n_batch, nblk),
        in_specs=[rows_spec(B_CONV, forward), rows_spec(LANE, forward)],
        out_specs=[rows_spec(B_HEADS * B_DIM, forward), states_spec(forward, B_DIM), states_spec(forward, CHUNK)],
        out_shape=[jax.ShapeDtypeStruct((n_tok, B_HEADS * B_DIM), F32),
                   jax.ShapeDtypeStruct((n_tok // CHUNK, B_HEADS, B_DIM, B_DIM), F32),
                   jax.ShapeDtypeStruct((n_tok // CHUNK, B_HEADS, CHUNK, CHUNK), F32)],
        scratch_shapes=[pltpu.VMEM((B_HEADS, B_DIM, B_DIM), F32)],
        compiler_params=pltpu.CompilerParams(dimension_semantics=("arbitrary", "arbitrary")),
    )(conv, bg)


def _delta_bwd(conv, bg, states, inverses, do, n_batch, seq):
    nblk, cpb, rows_spec, states_spec = _delta_blocks(n_batch, seq)

    def backward(i):
        return nblk - 1 - i

    def body(c_ref, bg_ref, st_ref, inv_ref, do_ref, dc_ref, dbg_ref, dr_ref):
        @pl.when(pl.program_id(1) == 0)
        def _():
            dr_ref[...] = jnp.zeros_like(dr_ref)

        def step(n, carry):
            c = cpb - 1 - n
            rows = pl.ds(pl.multiple_of(c * CHUNK, CHUNK), CHUNK)
            saved = [inv_ref[c, h] for h in range(B_HEADS)]
            _, vjp = jax.vjp(lambda *args: _delta_chunk(*args, saved_inv=saved)[:2],
                             *_load_heads(c_ref, bg_ref, st_ref.at[c], rows))
            do = [do_ref[rows, pl.ds(h * B_DIM, B_DIM)] for h in range(B_HEADS)]
            dr, dq, dk, dv, dbeta, dg = vjp((do, [dr_ref[h] for h in range(B_HEADS)]))
            lane = lax.broadcasted_iota(jnp.int32, (CHUNK, LANE), 1)
            dbg = jnp.zeros((CHUNK, LANE), F32)
            for h in range(B_HEADS):
                cq, ck, cv = _head_cols(h)
                dr_ref[h] = dr[h]
                dc_ref[rows, cq] = dq[h]
                dc_ref[rows, ck] = dk[h]
                dc_ref[rows, cv] = dv[h]
                dbg = dbg + jnp.where(lane == h, dbeta[h], 0.0) + jnp.where(lane == h + B_HEADS, dg[h], 0.0)
            dbg_ref[rows, :] = dbg
            return carry

        lax.fori_loop(0, cpb, step, 0)

    n_tok = n_batch * seq
    return pl.pallas_call(
        body,
        name="delta_bwd",
        grid=(n_batch, nblk),
        in_specs=[rows_spec(B_CONV, backward), rows_spec(LANE, backward), states_spec(backward, B_DIM),
                  states_spec(backward, CHUNK), rows_spec(B_HEADS * B_DIM, backward)],
        out_specs=[rows_spec(B_CONV, backward), rows_spec(LANE, backward)],
        out_shape=[jax.ShapeDtypeStruct((n_tok, B_CONV), F32), jax.ShapeDtypeStruct((n_tok, LANE), F32)],
        scratch_shapes=[pltpu.VMEM((B_HEADS, B_DIM, B_DIM), F32)],
        compiler_params=pltpu.CompilerParams(dimension_semantics=("arbitrary", "arbitrary")),
    )(conv, bg, states, inverses, do)


def _lane_row(vec4, first):
    return jnp.concatenate([jnp.zeros((1, first), F32), vec4.reshape(1, B_HEADS).astype(F32),
                            jnp.zeros((1, LANE - first - B_HEADS), F32)], axis=1)


def _local_step(x3d, p3d, tgt3d, w_in, small, rest_weights, send_grads, send_w_in):
    n_batch, seq, _ = x3d.shape
    n_tok = n_batch * seq
    x, p, tgt = x3d.reshape(n_tok, D), p3d.reshape(n_tok, -1), tgt3d.reshape(n_tok, D)
    g_mix, g_ffn, g_ple, g_final = (small[k].reshape(1, D) for k in ("g_mix", "g_ffn", "g_ple", "g_final"))
    w_onorm = small["w_onorm"].reshape(1, B_DIM)
    al_row = _lane_row(small["a_log"], B_HEADS)
    dtb_row = _lane_row(small["dt_bias"], B_HEADS)
    rel_bias = small["rel_bias"].reshape(A_HEADS, -1)
    bias = _rel_bias_table(rel_bias)
    conv_w = small["conv_w"].reshape(4, B_CONV)

    h1 = _rms_fwd(x, g_mix, name="rms_mix")
    projp = _mm(h1, w_in, tb=True, out_dtype=BF16, name="mm_proj", tn=1920)
    bd = _mm(h1, w_in[P_BD:], tb=True, name="mm_beta_decay", tn=LANE)
    y_a = _attn_fwd(projp, bias, n_batch, seq)
    conv = _conv_fwd(projp, conv_w, n_batch, seq)
    (bg,) = _rowwise(lambda raw, al, dtb: ([_gate_scalars(raw, al, dtb)], []), [_full(bd)],
                     [_full(al_row), _full(dtb_row)], [(LANE, F32, LANE, 0, 0)], name="gate_scalars", tr=1024)
    o_b, states, inverses = _delta_fwd(conv, bg, n_batch, seq)
    (y_b,) = _rowwise(lambda o, z, wn: ([_gated_norm(o, z, wn)], []), [(o_b, LANE, 0, 1), (projp, LANE, P_Z // LANE, 1)],
                      [_full(w_onorm)], [(B_HEADS * B_DIM, BF16, LANE, 0, 1)], name="gated_norm", tr=1024, ncol=B_HEADS)
    w = rest_weights(y_b)
    t_a = _mm(y_a, w["w_branch_a"], tb=True, name="mm_branch_a", tn=1024)
    t_b = _mm(y_b, w["w_branch_b"], tb=True, name="mm_branch_b", tn=1024)
    half = D // 2
    gate_rows = [(projp, half, P_GATE // half, 1), (projp, half, P_GATE // half + 2, 1), (t_a, half, 0, 1), (t_b, half, 0, 1)]
    (merged,) = _rowwise(lambda ga, gb, ta, tb: ([_merge(ga, gb, ta, tb)], []), gate_rows, [], [(D, BF16, half, 0, 1)],
                         name="merge", tr=512, ncol=2)
    x1, h2 = _mm(merged, w["w_out"], name="mm_out", tn=1024, epilogue=_residual_rms, rows=[x], bcs=[g_ffn], outs=(F32, BF16))
    gu = _mm(h2, w["w_gate_up"], tb=True, out_dtype=BF16, name="mm_gate_up", tn=2816)
    (act,) = _rowwise(lambda gub: ([_swiglu(gub)], []), [_full(gu)], [], [(D_FF, BF16, D_FF, 0, 0)], name="swiglu",
                      tr=512)
    x2, h3 = _mm(act, w["w_down"], name="mm_down", tm=512, tn=1024, tk=D_FF, epilogue=_residual_rms, rows=[x1], bcs=[g_ple],
                 outs=(F32, BF16))
    pp = _mm(p, w["w_ple_proj"], tb=True, name="mm_ple_proj", tn=1024)

    def head_fn(pgb, x2b, ppb, tb, gb):
        loss, (dx2, dpg, dpp, dg) = jax.value_and_grad(_head_loss, argnums=(0, 1, 2, 4))(x2b, pgb, ppb, tb, gb)
        return [dx2, dpg, dpp], [dg, jnp.full((1, D), loss, F32)]

    dx3, dpg, dpp, dg_final, loss_row = _mm(h3, w["w_ple_gate"], name="mm_ple_gate_loss", tm=256, tn=1024, epilogue=head_fn,
                                            rows=[x2, pp, tgt], bcs=[g_final], outs=(F32, BF16, BF16), n_red=2)
    gw = {}
    gw["w_ple_proj"] = _mm(dpp, p, ta=True, out_dtype=BF16, name="mm_d_ple_proj", tn=256)
    gw["w_ple_gate"] = _mm(h3, dpg, ta=True, out_dtype=BF16, name="mm_d_ple_gate", tn=512)
    dx2, dg_ple = _mm(dpg, w["w_ple_gate"], tb=True, name="mm_dh3", tm=512, tn=1024, epilogue=_rms_bwd, rows=[x2, dx3],
                      bcs=[g_ple], outs=(F32,), n_red=1)
    gw["w_down"] = _mm(act, dx2, ta=True, out_dtype=BF16, name="mm_d_down", tm=1408, tn=256)
    dact = _mm(dx2, w["w_down"], tb=True, out_dtype=BF16, name="mm_dact", tn=D_FF)

    def swiglu_bwd(gub, dab):
        _, vjp = jax.vjp(_swiglu, gub)
        return [vjp(dab)[0]], []

    (dgu,) = _rowwise(swiglu_bwd, [_full(gu), _full(dact)], [], [(2 * D_FF, BF16, 2 * D_FF, 0, 0)], name="swiglu_bwd", tr=256)
    gw["w_gate_up"] = _mm(dgu, h2, ta=True, out_dtype=BF16, name="mm_d_gate_up", tm=512, tn=1024)
    dx1, dg_ffn = _mm(dgu, w["w_gate_up"], name="mm_dh2", tm=256, tn=1024, tk=2 * D_FF, epilogue=_rms_bwd, rows=[x1, dx2],
                      bcs=[g_ffn], outs=(F32,), n_red=1)
    gw["w_out"] = _mm(merged, dx1, ta=True, out_dtype=BF16, name="mm_d_out", tn=512)
    dmerged = _mm(dx1, w["w_out"], tb=True, name="mm_dmerged", tn=1024)

    def merge_bwd(ga, gb, ta, tb, dm):
        _, vjp = jax.vjp(_merge, ga, gb, ta, tb)
        return list(vjp(dm)), []

    dga, dgb, dta, dtb = _rowwise(merge_bwd, gate_rows + [(dmerged, half, 0, 1)], [], [(D, BF16, half, 0, 1)] * 4,
                                  name="merge_bwd", tr=512, ncol=2)
    gw["w_branch_a"] = _mm(dta, y_a, ta=True, out_dtype=BF16, name="mm_d_branch_a", tn=512)
    gw["w_branch_b"] = _mm(dtb, y_b, ta=True, out_dtype=BF16, name="mm_d_branch_b", tn=512)
    dya = _mm(dta, w["w_branch_a"], name="mm_dya", tn=512)
    dyb = _mm(dtb, w["w_branch_b"], name="mm_dyb", tn=512)

    w_onorm = w_onorm + send_grads(gw)[0, 0]

    def gated_norm_bwd(o, z, dy, wn):
        _, vjp = jax.vjp(_gated_norm, o, z, wn)
        do, dz, dwn = vjp(dy)
        return [do, dz], [dwn]

    do_b, dz, dw_onorm = _rowwise(
        gated_norm_bwd, [(o_b, LANE, 0, 1), (projp, LANE, P_Z // LANE, 1), (dyb, LANE, 0, 1)], [_full(w_onorm)],
        [(B_HEADS * B_DIM, F32, LANE, 0, 1), (B_HEADS * B_DIM, BF16, LANE, 0, 1)], [(B_DIM, B_DIM, 0)],
        name="gated_norm_bwd", tr=1024, ncol=B_HEADS)
    dconv_out, dbg = _delta_bwd(conv, bg, states, inverses, do_b, n_batch, seq)

    def gate_scalars_bwd(raw, dbgb, al, dtb):
        _, vjp = jax.vjp(_gate_scalars, raw, al, dtb)
        draw, dal, ddtb = vjp(dbgb)
        return [draw], [dal, ddtb]

    dbd, dal_row, ddtb_row = _rowwise(gate_scalars_bwd, [_full(bd), _full(dbg)], [_full(al_row), _full(dtb_row)],
                                      [(LANE, BF16, LANE, 0, 0)], [(LANE, LANE, 0), (LANE, LANE, 0)], name="gate_scalars_bwd",
                                      tr=1024)
    dconv, dconv_w = _conv_bwd(projp, conv_w, dconv_out, n_batch, seq)
    dq_a, dk_a, dv_a, dbias = _attn_bwd(projp, bias, dya, n_batch, seq)
    dprojp = jnp.concatenate([dq_a, dk_a, dv_a, dconv, dz, dga, dgb, dbd], axis=1)
    sent = send_w_in(_mm(dprojp, h1, ta=True, out_dtype=BF16, name="mm_d_in", tm=640, tn=1024))
    sent, dprojp = lax.optimization_barrier((sent, dprojp))
    grad_x, dg_mix = _mm(dprojp, w_in, name="mm_dh1", tm=256, tn=1024, tk=P_END, epilogue=_rms_bwd, rows=[x, dx1],
                         bcs=[g_mix + sent[0, 0]], outs=(F32,), n_red=1)

    _, bias_vjp = jax.vjp(_rel_bias_table, rel_bias)
    gs = {
        "g_mix": dg_mix, "g_ffn": dg_ffn, "g_ple": dg_ple, "g_final": dg_final, "w_onorm": dw_onorm,
        "conv_w": dconv_w, "rel_bias": bias_vjp(dbias)[0],
        "a_log": dal_row[0, B_HEADS:2 * B_HEADS], "dt_bias": ddtb_row[0, B_HEADS:2 * B_HEADS],
    }
    return loss_row[:, :1], grad_x.reshape(n_batch, seq, D), gs


MATRICES = (("w_in", 1), ("w_gate_up", 1), ("w_branch_a", 1), ("w_branch_b", 1), ("w_out", 0), ("w_down", 0),
            ("w_ple_gate", 0), ("w_ple_proj", 1))
TAPS_PER_SHARD = B_CONV // N_DEV


def _held(shard, axis):
    return shard if axis == 0 else shard.T


def _from_gathered(slabs):
    return slabs.reshape(-1, slabs.shape[-1])


def _to_owner(held):
    return held.reshape(N_DEV, held.shape[0] // N_DEV, held.shape[1])


def _permute_w_in(held):
    n_gate = P_BD - P_GATE
    row = lax.broadcasted_iota(jnp.int32, (P_END, 1), 0)
    same = jnp.pad(held, ((0, P_END - D_IN), (0, 0)))
    up = jnp.pad(held[8:], ((0, P_END - D_IN + 8), (0, 0)))
    down = jnp.pad(held[:P_GATE + 8], ((n_gate, P_END - P_BD - 8), (0, 0)))
    zero = jnp.zeros((), held.dtype)
    return jnp.where(row < P_GATE, same, jnp.where(row < P_BD, up, jnp.where(row < P_BD + 8, down, zero)))


def _unpermute_w_in(gp):
    n_gate = P_BD - P_GATE
    row = lax.broadcasted_iota(jnp.int32, (D_IN, 1), 0)
    same = gp[:D_IN]
    up = jnp.pad(gp[n_gate:], ((0, D_IN - (P_END - n_gate)), (0, 0)))
    down = jnp.pad(gp[:P_BD], ((8, 0), (0, 0)))
    return jnp.where(row < P_GATE, same, jnp.where(row < P_GATE + 8, up, down))


SMALL_ROWS = 16
SMALL_LAYOUT = (("g_mix", 0, D), ("g_ffn", 1, D), ("g_ple", 2, D), ("g_final", 3, D), ("conv_w", 4, 4 * B_CONV),
                ("rel_bias", 10, A_HEADS * (2 * REL_CLIP + 1)), ("w_onorm", 13, B_DIM), ("a_log", 14, B_HEADS),
                ("dt_bias", 14, B_HEADS), ("loss", 15, 1))


def _pack_small(gs):
    rows = {}
    for name, row, n in SMALL_LAYOUT:
        rows.setdefault(row, []).append(gs[name].reshape(-1).astype(F32))
    parts = []
    for row in sorted(rows):
        flat = jnp.concatenate(rows[row])
        parts.append(jnp.concatenate([flat, jnp.zeros((-flat.shape[0] % D,), F32)]))
    flat = jnp.concatenate(parts)
    assert flat.shape[0] == SMALL_ROWS * D, flat.shape
    return flat.reshape(SMALL_ROWS, D)


def _unpack_small(blk):
    flat, out, used = blk.reshape(-1), {}, {}
    for name, row, n in SMALL_LAYOUT:
        start = row * D + used.get(row, 0)
        out[name] = flat[start:start + n]
        used[row] = used.get(row, 0) + n
    return out


def _position():
    return lax.axis_index("x"), lax.axis_index("y"), lax.axis_index("c")


PEERS = N_DEV - 1


def _comm_call(body, arrays, out_shapes, *, name):
    n = len(arrays)
    return pl.pallas_call(
        body,
        name=name,
        out_shape=out_shapes,
        in_specs=[HBM_SPEC] * n,
        out_specs=[HBM_SPEC] * n,
        scratch_shapes=[pltpu.SemaphoreType.DMA((PEERS * n,)), pltpu.SemaphoreType.DMA((PEERS * n,)),
                        pltpu.SemaphoreType.DMA((n,))],
    )(*arrays)


def _weights_allgather(shards):
    n = len(shards)

    def body(*refs):
        ins, outs = refs[:n], refs[n:2 * n]
        send_sems, recv_sems, local_sems = refs[2 * n:]
        x, y, c = _position()
        me, sibling = (x, y, c), (x, y, 1 - c)
        chips = [(1 - x, y), (x, 1 - y), (1 - x, 1 - y)]

        def slab(a, px, py, pc):
            return outs[a].at[4 * px + 2 * py + pc]

        def copy(a, k, block, to, src=None):
            return pltpu.make_async_remote_copy(src_ref=slab(a, *block) if src is None else src, dst_ref=slab(a, *block),
                                                send_sem=send_sems.at[PEERS * a + k], recv_sem=recv_sems.at[PEERS * a + k],
                                                device_id=to, device_id_type=MESH)

        local = [pltpu.make_async_copy(ins[a], slab(a, *me), local_sems.at[a]) for a in range(n)]
        sent = [copy(a, 1 + j, me, (*chip, c), src=ins[a]) for a in range(n) for j, chip in enumerate(chips)]
        sent += [copy(a, 0, me, sibling, src=ins[a]) for a in range(n)]
        for cp in sent + local:
            cp.start()
        for a in range(n):
            for j, chip in enumerate(chips):
                copy(a, 1 + j, (*chip, c), me).wait_recv()
                passed = copy(a, 4 + j, (*chip, c), sibling)
                passed.start()
                sent.append(passed)
        for a in range(n):
            copy(a, 0, sibling, me).wait_recv()
            for j, chip in enumerate(chips):
                copy(a, 4 + j, (*chip, 1 - c), me).wait_recv()
        for cp in sent:
            cp.wait_send()
        for cp in local:
            cp.wait()

    return _comm_call(body, shards, [jax.ShapeDtypeStruct((N_DEV,) + s.shape, s.dtype) for s in shards],
                      name="weights_allgather")


def _grads_exchange(by_owner):
    n = len(by_owner)

    def body(*refs):
        ins, outs = refs[:n], refs[n:2 * n]
        send_sems, recv_sems, local_sems = refs[2 * n:]
        x, y, c = _position()
        mine = 4 * x + 2 * y + c
        local = [pltpu.make_async_copy(ins[a].at[mine], outs[a].at[mine], local_sems.at[a]) for a in range(n)]
        for cp in local:
            cp.start()
        flips = [(dx, dy, dc) for dx in (0, 1) for dy in (0, 1) for dc in (0, 1) if dx + dy + dc]
        pending = []
        for k, (dx, dy, dc) in enumerate(flips):
            px, py, pc = (1 - x if dx else x), (1 - y if dy else y), (1 - c if dc else c)
            peer = 4 * px + 2 * py + pc
            for a in range(n):
                def remote(slot):
                    return pltpu.make_async_remote_copy(src_ref=ins[a].at[peer], dst_ref=outs[a].at[slot],
                                                        send_sem=send_sems.at[PEERS * a + k], recv_sem=recv_sems.at[PEERS * a + k],
                                                        device_id=(px, py, pc), device_id_type=MESH)

                sent = remote(mine)
                sent.start()
                pending.append((sent, remote(peer)))
        for sent, landed in pending:
            landed.wait_recv()
            sent.wait_send()
        for cp in local:
            cp.wait()

    return _comm_call(body, by_owner, [jax.ShapeDtypeStruct(g.shape, g.dtype) for g in by_owner], name="grads_exchange")


SEM_SPEC = pl.BlockSpec(memory_space=pltpu.SEMAPHORE)
DATAFLOW = pltpu.SideEffectType.DATAFLOW_SIDE_EFFECTING


def _peer_copies(srcs, lands, send_sems, recv_sems, by_owner, arrival):
    x, y, c = _position()
    mine = 4 * x + 2 * y + c
    copies = []
    for k, (dx, dy, dc) in enumerate([(dx, dy, dc) for dx in (0, 1) for dy in (0, 1) for dc in (0, 1) if dx + dy + dc]):
        px, py, pc = (1 - x if dx else x), (1 - y if dy else y), (1 - c if dc else c)
        peer = 4 * px + 2 * py + pc
        for a, (src, land) in enumerate(zip(srcs, lands)):
            copies.append(pltpu.make_async_remote_copy(
                src_ref=src.at[peer] if by_owner else src, dst_ref=land.at[peer if arrival else mine],
                send_sem=send_sems.at[PEERS * a + k], recv_sem=recv_sems.at[PEERS * a + k],
                device_id=(px, py, pc), device_id_type=MESH))
    return copies


def _exchange_start(sources, by_owner, *, name):
    n = len(sources)
    lands = [lax.empty((N_DEV,) + (s.shape[1:] if by_owner else s.shape), s.dtype) for s in sources]

    def body(*refs):
        send_sems, recv_sems, token = refs[2 * n], refs[2 * n + 1], refs[-1]
        for copy in _peer_copies(refs[:n], refs[n:2 * n], send_sems, recv_sems, by_owner, arrival=False):
            copy.start()
        token[...] = jnp.zeros_like(token)

    sems = pltpu.SemaphoreType.DMA((PEERS * n,))
    outs = pl.pallas_call(
        body,
        name=name,
        out_shape=(sems, sems, *[pltpu.HBM(a.shape, a.dtype) for a in sources + lands], jax.ShapeDtypeStruct((8, LANE), F32)),
        in_specs=[HBM_SPEC] * (2 * n),
        out_specs=(SEM_SPEC, SEM_SPEC, *[HBM_SPEC] * (2 * n), pl.BlockSpec(memory_space=pltpu.VMEM)),
        input_output_aliases={i: 2 + i for i in range(2 * n)},
        compiler_params=pltpu.CompilerParams(has_side_effects=DATAFLOW),
    )(*[pltpu.with_memory_space_constraint(a, pltpu.HBM) for a in sources + lands])
    return outs[:-1], outs[-1]


def _exchange_wait(started, after, by_owner, *, name):
    send_sems, recv_sems, *arrays = started
    n = len(arrays) // 2

    def body(*refs):
        for copy in _peer_copies(refs[:n], refs[n:2 * n], refs[2 * n], refs[2 * n + 1], by_owner, arrival=True):
            copy.wait_send()
            copy.wait_recv()

    outs = pl.pallas_call(
        body,
        name=name,
        out_shape=[pltpu.HBM(a.shape, a.dtype) for a in arrays],
        in_specs=[HBM_SPEC] * (2 * n) + [SEM_SPEC, SEM_SPEC, pl.BlockSpec(memory_space=pl.ANY)],
        out_specs=[HBM_SPEC] * (2 * n),
        input_output_aliases={i: i for i in range(2 * n)},
        compiler_params=pltpu.CompilerParams(has_side_effects=DATAFLOW),
    )(*arrays, send_sems, recv_sems, after)
    return outs[:n], outs[n:]


def _slot_sum(g_ref, own_ref):
    if own_ref is not None:
        x, y, c = _position()
        mine = 4 * x + 2 * y + c
    acc = None
    for j in range(N_DEV):
        part = g_ref[j] if own_ref is None else jnp.where(mine == j, own_ref[...], g_ref[j])
        acc = part.astype(F32) if acc is None else acc + part.astype(F32)
    return acc


def _sum_slots(got, *, name, tr):
    _, rows, cols = got.shape
    tr = _tile(rows, tr, 16)

    def body(g_ref, o_ref):
        o_ref[...] = _slot_sum(g_ref, None)

    return pl.pallas_call(
        body,
        name=name,
        grid=(rows // tr,),
        in_specs=[pl.BlockSpec((N_DEV, tr, cols), lambda i: (0, i, 0))],
        out_specs=pl.BlockSpec((tr, cols), lambda i: (i, 0)),
        out_shape=jax.ShapeDtypeStruct((rows, cols), F32),
        compiler_params=pltpu.CompilerParams(dimension_semantics=("parallel",)),
    )(got)


def _adamw(wt, g, m, v, *, name, own=None):
    slots = own is not None
    shape = wt.shape
    two_d = (-1, shape[-1]) if wt.ndim > 1 else (1, -1)
    args = [a.reshape(two_d) for a in (wt, m, v)]
    rows, cols = args[0].shape
    if rows % 16 == 0:
        tr, tc = _tile(rows, 256, 16), cols
    else:
        tr, tc = rows, _tile(cols, 256 if rows > 64 else 512)
    args.insert(1, g.reshape((N_DEV, rows, cols) if slots else (rows, cols)))
    if slots:
        args.append(own.reshape(rows, cols))

    def body(w_ref, g_ref, m_ref, v_ref, *refs):
        go_ref, d_ref, nm_ref, nv_ref = refs[-4:]
        gv = _slot_sum(g_ref, refs[0]) if slots else g_ref[...]
        go_ref[...] = gv
        m2 = ADAM_B1 * m_ref[...] + (1.0 - ADAM_B1) * gv
        v2 = ADAM_B2 * v_ref[...] + (1.0 - ADAM_B2) * (gv * gv)
        m_hat = m2 / (1.0 - ADAM_B1 ** ADAM_STEP)
        v_hat = v2 / (1.0 - ADAM_B2 ** ADAM_STEP)
        d_ref[...] = -ADAM_LR * (m_hat / (jnp.sqrt(v_hat) + ADAM_EPS) + ADAM_WD * w_ref[...])
        nm_ref[...] = m2
        nv_ref[...] = v2

    spec = pl.BlockSpec((tr, tc), lambda i, j: (i, j))
    g_spec = pl.BlockSpec((N_DEV, tr, tc), lambda i, j: (0, i, j)) if slots else spec
    outs = pl.pallas_call(
        body,
        name=name,
        grid=(rows // tr, cols // tc),
        in_specs=[spec, g_spec, spec, spec] + ([spec] if slots else []),
        out_specs=[spec] * 4,
        out_shape=[jax.ShapeDtypeStruct((rows, cols), F32)] * 4,
        compiler_params=pltpu.CompilerParams(dimension_semantics=("parallel", "parallel")),
    )(*args)
    return tuple(o.reshape(shape) for o in outs)


WEIGHTS = ("g_mix", "w_in", "conv_w", "a_log", "dt_bias", "rel_bias", "w_onorm", "w_branch_a", "w_branch_b", "w_out", "g_ffn",
           "w_gate_up", "w_down", "g_ple", "w_ple_gate", "w_ple_proj", "g_final")


def kernel(x, p, g_mix, w_in, conv_w, a_log, dt_bias, rel_bias, w_onorm, w_branch_a, w_branch_b, w_out, g_ffn, w_gate_up, w_down, g_ple, w_ple_gate, w_ple_proj, g_final, loss_target, m_g_mix, m_w_in, m_conv_w, m_a_log, m_dt_bias, m_rel_bias, m_w_onorm, m_w_branch_a, m_w_branch_b, m_w_out, m_g_ffn, m_w_gate_up, m_w_down, m_g_ple, m_w_ple_gate, m_w_ple_proj, m_g_final, v_g_mix, v_w_in, v_conv_w, v_a_log, v_dt_bias, v_rel_bias, v_w_onorm, v_w_branch_a, v_w_branch_b, v_w_out, v_g_ffn, v_w_gate_up, v_w_down, v_g_ple, v_w_ple_gate, v_w_ple_proj, v_g_final):
    given = dict(g_mix=g_mix, w_in=w_in, conv_w=conv_w, a_log=a_log, dt_bias=dt_bias, rel_bias=rel_bias, w_onorm=w_onorm,
                 w_branch_a=w_branch_a, w_branch_b=w_branch_b, w_out=w_out, g_ffn=g_ffn, w_gate_up=w_gate_up, w_down=w_down,
                 g_ple=g_ple, w_ple_gate=w_ple_gate, w_ple_proj=w_ple_proj, g_final=g_final)
    mom1 = dict(g_mix=m_g_mix, w_in=m_w_in, conv_w=m_conv_w, a_log=m_a_log, dt_bias=m_dt_bias, rel_bias=m_rel_bias,
                w_onorm=m_w_onorm, w_branch_a=m_w_branch_a, w_branch_b=m_w_branch_b, w_out=m_w_out, g_ffn=m_g_ffn,
                w_gate_up=m_w_gate_up, w_down=m_w_down, g_ple=m_g_ple, w_ple_gate=m_w_ple_gate, w_ple_proj=m_w_ple_proj,
                g_final=m_g_final)
    mom2 = dict(g_mix=v_g_mix, w_in=v_w_in, conv_w=v_conv_w, a_log=v_a_log, dt_bias=v_dt_bias, rel_bias=v_rel_bias,
                w_onorm=v_w_onorm, w_branch_a=v_w_branch_a, w_branch_b=v_w_branch_b, w_out=v_w_out, g_ffn=v_g_ffn,
                w_gate_up=v_w_gate_up, w_down=v_w_down, g_ple=v_g_ple, w_ple_gate=v_w_ple_gate, w_ple_proj=v_w_ple_proj,
                g_final=v_g_final)
    mine = 4 * lax.axis_index("x") + 2 * lax.axis_index("y") + lax.axis_index("c")

    my_slot = (jnp.arange(N_DEV) == mine)[:, None, None]
    rest = MATRICES[1:]
    in_flight = {}

    got_in, got_taps = _weights_allgather([_held(w_in[0], 1).astype(BF16), conv_w[0]])
    in_flight["weights"], weights_sent = _exchange_start([_held(given[name][0], axis).astype(BF16) for name, axis in rest], False,
                                                         name="weights_start")
    small = dict(g_mix=g_mix + weights_sent[0, 0], g_ffn=g_ffn, g_ple=g_ple, g_final=g_final, w_onorm=w_onorm, a_log=a_log,
                 dt_bias=dt_bias, rel_bias=rel_bias, conv_w=jnp.transpose(got_taps, (1, 0, 2)).reshape(4, B_CONV))

    def rest_weights(after):
        shards, landed = _exchange_wait(in_flight.pop("weights"), after, False, name="weights_wait")
        return {name: _from_gathered(jnp.where(my_slot, shard[None], slabs)) for (name, _), shard, slabs in zip(rest, shards, landed)}

    def send_grads(gw):
        in_flight["grads"], sent = _exchange_start([_to_owner(gw[name]) for name, _ in rest], True, name="grads_start")
        return sent

    def send_w_in(g_in):
        in_flight["grad_in"], sent = _exchange_start([_to_owner(_unpermute_w_in(g_in))], True, name="grad_in_start")
        return sent

    loss_part, grad_x, gs = _local_step(x, p[0], loss_target, _permute_w_in(_from_gathered(got_in)), small, rest_weights,
                                        send_grads, send_w_in)
    gs["loss"] = loss_part

    updates = {}

    def update_matrices(matrices, own_slabs, landed):
        for (name, axis), own_slab, slots in zip(matrices, own_slabs, landed):
            mine_of = lax.dynamic_index_in_dim(own_slab, mine, axis=0, keepdims=False)
            w_held, m_held, v_held = (_held(a[name][0], axis) for a in (given, mom1, mom2))
            outs = _adamw(w_held, slots, m_held, v_held, name=f"adamw_{name}", own=mine_of)
            updates[name] = tuple(_held(o, axis)[None] for o in outs)

    update_matrices(rest, *_exchange_wait(in_flight["grads"], grad_x, True, name="grads_wait"))
    update_matrices(MATRICES[:1], *_exchange_wait(in_flight["grad_in"], updates[rest[-1][0]][0], True, name="grad_in_wait"))
    small_block, _ = lax.optimization_barrier((_pack_small(gs), updates["w_in"][0]))
    (got_small,) = _grads_exchange([jnp.broadcast_to(small_block, (N_DEV, SMALL_ROWS, D))])
    small_sum = _unpack_small(_sum_slots(got_small, name="sum_small_grads", tr=16))
    loss = small_sum.pop("loss")[0]
    conv_all = small_sum.pop("conv_w").reshape(4, N_DEV, TAPS_PER_SHARD)
    small_sum["conv_w"] = lax.dynamic_index_in_dim(conv_all, mine, axis=1, keepdims=False)
    for name, g in small_sum.items():
        updates[name] = _adamw(given[name], g.reshape(given[name].shape), mom1[name], mom2[name], name=f"adamw_{name}")
    return (loss, grad_x, *[updates[name][k] for k in range(4) for name in WEIGHTS])
```

```python
import jax
import jax.numpy as jnp
from jax import lax
from jax.experimental import pallas as pl
from jax.experimental.pallas import tpu as pltpu

F32 = jnp.float32
BF16 = jnp.bfloat16
DELTA_PREC = lax.Precision.HIGH
MESH = pl.DeviceIdType.MESH

N_DEV = 8
D = 1024
CHUNK = 64
EPS = 1e-6
A_HEADS, A_DIM, A_WIDTH = 8, 64, 512
A_BAND = 9 * CHUNK
A_PAD = 8 * CHUNK
REL_CLIP = 128
B_HEADS, B_DIM = 4, 128
B_CONV = 1536
D_FF = 2816
D_IN = 5640
P_CONV, P_Z, P_GATE, P_BD, P_END = 1536, 3072, 3584, 5632, 5760
LANE = 128

ADAM_LR, ADAM_B1, ADAM_B2, ADAM_EPS, ADAM_WD, ADAM_STEP = 0.001, 0.9, 0.999, 1e-08, 0.01, 10

NT = (((1,), (1,)), ((), ()))
TN = (((0,), (0,)), ((), ()))
NN = (((1,), (0,)), ((), ()))

HBM_SPEC = pl.BlockSpec(memory_space=pltpu.HBM)


def _tile(n, target, align=LANE):
    if n <= target:
        return n
    best = None
    for t in range(align, target + 1, align):
        if n % t == 0:
            best = t
    assert best is not None, (n, target, align)
    return best


def _mm(a, b, *, name, ta=False, tb=False, out_dtype=F32, tm=1024, tn=640, tk=None, epilogue=None, rows=(), bcs=(), outs=(),
        n_red=0):
    assert not (ta and tb)
    if ta:
        k_dim, m_dim = a.shape
    else:
        m_dim, k_dim = a.shape
    n_dim = b.shape[0] if tb else b.shape[1]
    assert b.shape[1 if tb else 0] == k_dim
    tm, tn = _tile(m_dim, tm), _tile(n_dim, tn)
    tk = _tile(k_dim, tk or (4096 if ta else 1024), 8 if ta else LANE)
    nk = k_dim // tk
    dn = TN if ta else NT if tb else NN
    if epilogue is None:
        outs = (out_dtype,)
    assert not n_red or tn == n_dim
    n_extra, n_out = len(rows) + len(bcs), len(outs)

    def body(a_ref, b_ref, *refs):
        part = lax.dot_general(a_ref[...].astype(BF16), b_ref[...].astype(BF16), dn, preferred_element_type=F32)

        def finish(r):
            o_vals, r_vals = ([r], []) if epilogue is None else epilogue(r, *[x[...] for x in refs[:n_extra]])
            for ref, val in zip(refs[n_extra:n_extra + n_out], o_vals):
                ref[...] = val.astype(ref.dtype)
            first = pl.program_id(0) == 0
            for ref, val in zip(refs[n_extra + n_out:n_extra + n_out + n_red], r_vals):
                @pl.when(first)
                def _():
                    ref[...] = val

                @pl.when(jnp.logical_not(first))
                def _():
                    ref[...] += val

        if nk == 1:
            finish(part)
        else:
            acc_ref = refs[-1]
            k = pl.program_id(2)

            @pl.when(k == 0)
            def _():
                acc_ref[...] = part

            @pl.when(k > 0)
            def _():
                acc_ref[...] += part

            @pl.when(k == nk - 1)
            def _():
                finish(acc_ref[...])

    tile = pl.BlockSpec((tm, tn), lambda i, j, k: (i, j))
    col = pl.BlockSpec((1, tn), lambda i, j, k: (0, j))
    a_spec = pl.BlockSpec((tk, tm), lambda i, j, k: (k, i)) if ta else pl.BlockSpec((tm, tk), lambda i, j, k: (i, k))
    b_spec = pl.BlockSpec((tn, tk), lambda i, j, k: (j, k)) if tb else pl.BlockSpec((tk, tn), lambda i, j, k: (k, j))
    result = pl.pallas_call(
        body,
        name=name,
        grid=(m_dim // tm, n_dim // tn, nk),
        in_specs=[a_spec, b_spec] + [tile] * len(rows) + [col] * len(bcs),
        out_specs=[tile] * n_out + [col] * n_red,
        out_shape=[jax.ShapeDtypeStruct((m_dim, n_dim), dt) for dt in outs] + [jax.ShapeDtypeStruct((1, n_dim), F32)] * n_red,
        scratch_shapes=[pltpu.VMEM((tm, tn), F32)] if nk > 1 else [],
        compiler_params=pltpu.CompilerParams(dimension_semantics=("arbitrary",) * 3 if n_red else ("parallel", "parallel", "arbitrary")),
    )(a, b, *rows, *bcs)
    return result[0] if epilogue is None else result


def _rowwise(fn, rows, bcs, outs, reds=(), *, name, tr, ncol=1):
    n_rows = rows[0][0].shape[0]
    tr = _tile(n_rows, tr, 8)
    nrow = n_rows // tr
    n_in, n_out = len(rows) + len(bcs), len(outs)

    def body(*refs):
        j, i = pl.program_id(0), pl.program_id(1)
        o_vals, r_vals = fn(*[r[...].astype(F32) for r in refs[:n_in]])
        for ref, val in zip(refs[n_in:n_in + n_out], o_vals):
            ref[...] = val.astype(ref.dtype)
        for ref, val, (_, _, stride) in zip(refs[n_in + n_out:], r_vals, reds):
            first = (i == 0) if stride else jnp.logical_and(i == 0, j == 0)

            @pl.when(first)
            def _():
                ref[...] = val

            @pl.when(jnp.logical_not(first))
            def _():
                ref[...] += val

    def spec(r, w, off, st, row_dep=True):
        if row_dep:
            return pl.BlockSpec((r, w), lambda j, i: (i, off + st * j))
        return pl.BlockSpec((r, w), lambda j, i: (0, off + st * j))

    in_specs = [spec(tr, w, off, st) for (_, w, off, st) in rows]
    in_specs += [spec(a.shape[0], w, off, st, False) for (a, w, off, st) in bcs]
    out_specs = [spec(tr, w, off, st) for (_, _, w, off, st) in outs]
    out_specs += [spec(1, w, 0, st, False) for (_, w, st) in reds]
    out_shape = [jax.ShapeDtypeStruct((n_rows, c), dt) for (c, dt, _, _, _) in outs]
    out_shape += [jax.ShapeDtypeStruct((1, c), F32) for (c, _, _) in reds]
    return pl.pallas_call(
        body,
        name=name,
        grid=(ncol, nrow),
        in_specs=in_specs,
        out_specs=out_specs,
        out_shape=out_shape,
        compiler_params=pltpu.CompilerParams(dimension_semantics=("arbitrary", "arbitrary")),
    )(*[r[0] for r in rows], *[b[0] for b in bcs])


def _full(a):
    return (a, a.shape[1], 0, 0)


def _rms(x, g):
    return x * lax.rsqrt(jnp.mean(x * x, axis=-1, keepdims=True) + EPS) * g


def _silu(x):
    return x * jax.nn.sigmoid(x)


def _softplus(x):
    return jnp.maximum(x, 0.0) + jnp.log(1.0 + jnp.exp(-jnp.abs(x)))


def _rms_fwd(x, g, *, name):
    (h,) = _rowwise(lambda xb, gb: ([_rms(xb, gb)], []), [_full(x)], [_full(g)], [(D, BF16, D, 0, 0)], name=name, tr=512)
    return h


def _residual_rms(r, x, g):
    x_new = x + r
    return [x_new, _rms(x_new, g)], []


def _rms_bwd(dh, x, dres, g):
    _, vjp = jax.vjp(_rms, x, g)
    dx, dg = vjp(dh)
    return [dx + dres], [dg]


def _gate_scalars(raw, al_row, dtb_row):
    lane = lax.broadcasted_iota(jnp.int32, raw.shape, 1)
    beta = jax.nn.sigmoid(raw)
    g = -jnp.exp(al_row) * _softplus(raw + dtb_row)
    return jnp.where(lane < B_HEADS, beta, jnp.where(lane < 2 * B_HEADS, g, 0.0))


def _gated_norm(o, z, w):
    return _rms(o, w) * _silu(z)


def _merge(ga, gb, ta, tb):
    return jax.nn.sigmoid(ga) * ta + jax.nn.sigmoid(gb) * tb


def _swiglu(gu):
    return _silu(gu[:, :D_FF]) * gu[:, D_FF:]


def _head_loss(x2, pg, pp, tgt, g):
    x3 = x2 + jax.nn.sigmoid(pg) * pp
    err = _rms(x3, g) - tgt
    return 0.5 * jnp.sum(jnp.mean(err * err, axis=-1))


CONV_W = 256


def _conv_taps(x, w):
    row = lax.broadcasted_iota(jnp.int32, x.shape, 0)
    shifted = [x] + [jnp.where(row >= s, pltpu.roll(x, s, 0), 0.0) for s in (1, 2, 3)]
    pre = shifted[0] * w[3:4]
    for s in (1, 2, 3):
        pre = pre + shifted[s] * w[3 - s:4 - s]
    return pre, shifted


def _conv_fwd(projp, conv_w, n_batch, seq):
    ncol = B_CONV // CONV_W
    first = P_CONV // CONV_W

    def body(x_ref, w_ref, o_ref):
        pre, _ = _conv_taps(x_ref[...].astype(F32), w_ref[...])
        o_ref[...] = _silu(pre)

    return pl.pallas_call(
        body,
        name="conv_fwd",
        grid=(ncol, n_batch),
        in_specs=[pl.BlockSpec((seq, CONV_W), lambda j, b: (b, first + j)), pl.BlockSpec((4, CONV_W), lambda j, b: (0, j))],
        out_specs=pl.BlockSpec((seq, CONV_W), lambda j, b: (b, j)),
        out_shape=jax.ShapeDtypeStruct((n_batch * seq, B_CONV), F32),
        compiler_params=pltpu.CompilerParams(dimension_semantics=("parallel", "parallel")),
    )(projp, conv_w)


def _conv_bwd(projp, conv_w, dc, n_batch, seq):
    width = dc.shape[1]
    ncol = width // CONV_W
    first_x = P_CONV // CONV_W

    def body(x_ref, w_ref, dc_ref, dx_ref, dw_ref):
        b = pl.program_id(1)
        w = w_ref[...]
        pre, shifted = _conv_taps(x_ref[...].astype(F32), w)
        sg = jax.nn.sigmoid(pre)
        dpre = dc_ref[...] * (sg * (1.0 + pre * (1.0 - sg)))
        row = lax.broadcasted_iota(jnp.int32, dpre.shape, 0)
        dx = dpre * w[3:4]
        for s in (1, 2, 3):
            dx = dx + jnp.where(row < seq - s, pltpu.roll(dpre, seq - s, 0), 0.0) * w[3 - s:4 - s]
        dx_ref[...] = dx.astype(dx_ref.dtype)
        for s in (0, 1, 2, 3):
            part = jnp.sum(dpre * shifted[s], axis=0, keepdims=True)

            @pl.when(b == 0)
            def _():
                dw_ref[3 - s:4 - s, :] = part

            @pl.when(b > 0)
            def _():
                dw_ref[3 - s:4 - s, :] += part

    return pl.pallas_call(
        body,
        name="conv_bwd",
        grid=(ncol, n_batch),
        in_specs=[
            pl.BlockSpec((seq, CONV_W), lambda j, b: (b, first_x + j)),
            pl.BlockSpec((4, CONV_W), lambda j, b: (0, j)),
            pl.BlockSpec((seq, CONV_W), lambda j, b: (b, j)),
        ],
        out_specs=[pl.BlockSpec((seq, CONV_W), lambda j, b: (b, j)), pl.BlockSpec((4, CONV_W), lambda j, b: (0, j))],
        out_shape=[jax.ShapeDtypeStruct((n_batch * seq, width), BF16), jax.ShapeDtypeStruct((4, width), F32)],
        compiler_params=pltpu.CompilerParams(dimension_semantics=("arbitrary", "arbitrary")),
    )(projp, conv_w, dc)


@jax.custom_vjp
def _attend(s, v):
    return _attend_fwd(s, v)[0]


def _attend_fwd(s, v):
    v16 = [t.astype(BF16) for t in v]
    p = [jnp.exp(t - jnp.max(t, axis=-1, keepdims=True)) for t in s]
    p = [t * (1.0 / jnp.sum(t, axis=-1, keepdims=True)) for t in p]
    o = [jnp.dot(t.astype(BF16), v16[n // 2], preferred_element_type=F32) for n, t in enumerate(p)]
    return o, (p, v16, o)


def _attend_bwd(saved, do):
    p, v16, o = saved
    do16 = [t.astype(BF16) for t in do]
    dv = [lax.dot_general(t.astype(BF16), do16[n], TN, preferred_element_type=F32) for n, t in enumerate(p)]
    dp = [lax.dot_general(t, v16[n // 2], NT, preferred_element_type=F32) for n, t in enumerate(do16)]
    delta = [jnp.sum(a * b, axis=-1, keepdims=True) for a, b in zip(do, o)]
    ds = [a * (b - c) for a, b, c in zip(p, dp, delta)]
    return ds, [dv[2 * i] + dv[2 * i + 1] for i in range(len(v16))]


_attend.defvjp(_attend_fwd, _attend_bwd)


def _attn_chunk(qc, kb, vb, bias2, valid, lane_lo):
    sel = (lane_lo, jnp.logical_not(lane_lo))
    items = [(i, e) for i in range(len(qc)) for e in (0, 1)]
    k16 = [t.astype(BF16) for t in kb]
    qm = [(jnp.where(sel[e], qc[i], 0.0) * (A_DIM ** -0.5)).astype(BF16) for i, e in items]
    s = [lax.dot_general(qm[n], k16[i], NT, preferred_element_type=F32) + bias2[e] for n, (i, e) in enumerate(items)]
    if valid is not None:
        s = [jnp.where(valid[i], s[n], -1e30) for n, (i, e) in enumerate(items)]
    o = [jnp.where(sel[e], t, 0.0) for t, (i, e) in zip(_attend(s, vb), items)]
    return [o[2 * i] + o[2 * i + 1] for i in range(len(qc))]


ATTN_GROUP_FWD, ATTN_GROUP_BWD = 8, 8


def _attn_loops(step, n_groups, n_masked):
    lax.fori_loop(0, n_masked, lambda g, c: step(g, c, True), 0)
    lax.fori_loop(n_masked, n_groups, lambda g, c: step(g, c, False), 0)


def _attn_group(g, group, q_ref, kp_ref, vp_ref):
    col = lax.broadcasted_iota(jnp.int32, (CHUNK, A_BAND), 1)
    lane_lo = lax.broadcasted_iota(jnp.int32, (1, LANE), 1) < A_DIM
    starts = [pl.multiple_of((g * group + i) * CHUNK, CHUNK) for i in range(group)]
    rows = [pl.ds(r0, CHUNK) for r0 in starts]
    bands = [pl.ds(r0, A_BAND) for r0 in starts]
    valid = [col + r0 >= A_PAD for r0 in starts]
    loaded = [q_ref[r, :].astype(F32) for r in rows], [kp_ref[b, :] for b in bands], [vp_ref[b, :] for b in bands]
    return rows, bands, loaded, valid, lane_lo


def _attn_specs(seq):
    def blk(first):
        return pl.BlockSpec((seq, LANE), lambda hp, b: (b, first + hp))

    return blk, pl.BlockSpec((2, CHUNK, A_BAND), lambda hp, b: (hp, 0, 0))


def _attn_fwd(projp, bias, n_batch, seq):
    nc = seq // CHUNK
    blk, bias_spec = _attn_specs(seq)

    def body(q_ref, k_ref, v_ref, b_ref, o_ref, kp_ref, vp_ref):
        kp_ref[0:A_PAD, :] = jnp.zeros((A_PAD, LANE), F32)
        vp_ref[0:A_PAD, :] = jnp.zeros((A_PAD, LANE), F32)
        kp_ref[A_PAD:, :] = k_ref[...].astype(F32)
        vp_ref[A_PAD:, :] = v_ref[...].astype(F32)
        bias2 = b_ref[...]

        def step(g, carry, masked):
            rows, _, (qc, kb, vb), valid, lane_lo = _attn_group(g, ATTN_GROUP_FWD, q_ref, kp_ref, vp_ref)
            out = _attn_chunk(qc, kb, vb, bias2, valid if masked else None, lane_lo)
            for r, o in zip(rows, out):
                o_ref[r, :] = o.astype(o_ref.dtype)
            return carry

        _attn_loops(step, nc // ATTN_GROUP_FWD, A_PAD // (CHUNK * ATTN_GROUP_FWD))

    return pl.pallas_call(
        body,
        name="attn_fwd",
        grid=(A_HEADS // 2, n_batch),
        in_specs=[blk(0), blk(4), blk(8), bias_spec],
        out_specs=pl.BlockSpec((seq, LANE), lambda hp, b: (b, hp)),
        out_shape=jax.ShapeDtypeStruct((n_batch * seq, A_WIDTH), BF16),
        scratch_shapes=[pltpu.VMEM((A_PAD + seq, LANE), F32), pltpu.VMEM((A_PAD + seq, LANE), F32)],
        compiler_params=pltpu.CompilerParams(dimension_semantics=("parallel", "parallel")),
    )(projp, projp, projp, bias)


def _attn_bwd(projp, bias, dy, n_batch, seq):
    nc = seq // CHUNK
    blk, bias_spec = _attn_specs(seq)
    out_blk = pl.BlockSpec((seq, LANE), lambda hp, b: (b, hp))

    def body(q_ref, k_ref, v_ref, b_ref, dy_ref, dq_ref, dk_ref, dv_ref, db_ref, kp_ref, vp_ref, dkp_ref, dvp_ref):
        b = pl.program_id(1)
        kp_ref[0:A_PAD, :] = jnp.zeros((A_PAD, LANE), F32)
        vp_ref[0:A_PAD, :] = jnp.zeros((A_PAD, LANE), F32)
        kp_ref[A_PAD:, :] = k_ref[...].astype(F32)
        vp_ref[A_PAD:, :] = v_ref[...].astype(F32)
        dkp_ref[...] = jnp.zeros_like(dkp_ref)
        dvp_ref[...] = jnp.zeros_like(dvp_ref)
        bias2 = b_ref[...]

        @pl.when(b == 0)
        def _():
            db_ref[...] = jnp.zeros_like(db_ref)

        def step(g, carry, masked):
            rows, bands, (qc, kb, vb), valid, lane_lo = _attn_group(g, ATTN_GROUP_BWD, q_ref, kp_ref, vp_ref)
            _, vjp = jax.vjp(lambda q, k, v, bb: _attn_chunk(q, k, v, bb, valid if masked else None, lane_lo), qc, kb, vb, bias2)
            dq, dk, dv, dbias = vjp([dy_ref[r, :] for r in rows])
            for i, r in enumerate(rows):
                dq_ref[r, :] = dq[i].astype(dq_ref.dtype)
            for i, band in enumerate(bands):
                dkp_ref[band, :] += dk[i]
                dvp_ref[band, :] += dv[i]
            db_ref[...] += dbias
            return carry

        _attn_loops(step, nc // ATTN_GROUP_BWD, A_PAD // (CHUNK * ATTN_GROUP_BWD))
        dk_ref[...] = dkp_ref[A_PAD:, :].astype(dk_ref.dtype)
        dv_ref[...] = dvp_ref[A_PAD:, :].astype(dv_ref.dtype)

    n_tok = n_batch * seq
    pad = pltpu.VMEM((A_PAD + seq, LANE), F32)
    return pl.pallas_call(
        body,
        name="attn_bwd",
        grid=(A_HEADS // 2, n_batch),
        in_specs=[blk(0), blk(4), blk(8), bias_spec, out_blk],
        out_specs=[out_blk, out_blk, out_blk, bias_spec],
        out_shape=[jax.ShapeDtypeStruct((n_tok, A_WIDTH), BF16)] * 3 + [jax.ShapeDtypeStruct((A_HEADS, CHUNK, A_BAND), F32)],
        scratch_shapes=[pad, pad, pad, pad],
        compiler_params=pltpu.CompilerParams(dimension_semantics=("arbitrary", "arbitrary")),
    )(projp, projp, projp, bias, dy)


def _rel_bias_table(rel_bias):
    span = CHUNK + A_BAND - 1
    near = REL_CLIP + CHUNK
    far = jnp.broadcast_to(rel_bias[:, 2 * REL_CLIP:], (A_HEADS, span - near))
    t = jnp.concatenate([rel_bias[:, 2 * REL_CLIP + 1 - near:], far], axis=1)
    u = jnp.concatenate([t[:, :A_BAND][:, ::-1], t[:, A_BAND:][:, ::-1]], axis=1)
    rolled = jnp.tile(u, (1, CHUNK))[:, :CHUNK * (span - 1)].reshape(A_HEADS, CHUNK, span - 1)
    return rolled[:, :, :A_BAND]


def _dot(a, b, dn=NN):
    return lax.dot_general(a, b, dn, precision=DELTA_PREC, preferred_element_type=F32)


def _dot16(a, b, dn=NN):
    return lax.dot_general(a.astype(BF16), b.astype(BF16), dn, preferred_element_type=F32)


def _each(fn, *lists):
    return [fn(*vals) for vals in zip(*lists)]


@jax.custom_vjp
def _saved_inverse(x, inv):
    return inv


def _saved_inverse_fwd(x, inv):
    return inv, inv


def _saved_inverse_bwd(inv, ct):
    return _dot(_dot(inv, ct, TN), inv, NT), jnp.zeros_like(inv)


_saved_inverse.defvjp(_saved_inverse_fwd, _saved_inverse_bwd)


def _delta_chunk(r_state, cq, ck, cv, beta, g, saved_inv=None):
    ii = lax.broadcasted_iota(jnp.int32, (CHUNK, CHUNK), 0)
    jj = lax.broadcasted_iota(jnp.int32, (CHUNK, CHUNK), 1)
    incl, strict, eye = ii >= jj, ii > jj, ii == jj
    q = _each(lambda t: t * lax.rsqrt(jnp.sum(t * t, axis=-1, keepdims=True) + EPS) * (B_DIM ** -0.5), cq)
    k = _each(lambda t: t * lax.rsqrt(jnp.sum(t * t, axis=-1, keepdims=True) + EPS), ck)
    g_b = _each(lambda t: jnp.broadcast_to(t, (CHUNK, CHUNK)), g)
    g_row = _each(lambda t: jnp.sum(jnp.where(eye, t, 0.0), axis=0, keepdims=True), g_b)
    gc_col = _each(lambda t: jnp.sum(jnp.where(incl, t, 0.0), axis=1, keepdims=True), g_row)
    gc_row = _each(lambda t: jnp.sum(jnp.where(ii <= jj, t, 0.0), axis=0, keepdims=True), g_b)
    decay = _each(lambda c, r: jnp.where(incl, jnp.exp(jnp.where(incl, c - r, 0.0)), 0.0), gc_col, gc_row)
    kk = _each(lambda t: _dot(t, t, NT), k)
    x = _each(lambda b, m, d: jnp.where(strict, -(b * m * d), 0.0), beta, kk, decay)
    if saved_inv is None:
        inv = _each(lambda t: jnp.where(eye, 1.0, 0.0) + t, x)
        pw = x
        for _ in range(5):
            pw = _each(lambda t: _dot(t, t), pw)
            inv = _each(lambda t, s: t + _dot(t, s), inv, pw)
    else:
        inv = _each(_saved_inverse, x, saved_inv)
    egc = _each(jnp.exp, gc_col)
    u = _each(lambda t, b, v: _dot16(t, b * v), inv, beta, cv)
    wk = _each(lambda t, b, e, kh: _dot16(t, (b * e) * kh), inv, beta, egc, k)
    pqk = _each(lambda qh, kh, d: _dot16(qh, kh, NT) * d, q, k, decay)
    g_last = _each(lambda c: c[CHUNK - 1:CHUNK, :], gc_col)
    kdec = _each(lambda kh, gl, c: kh * jnp.exp(gl - c), k, g_last, gc_col)
    w = _each(lambda uh, wkh, r: uh - _dot16(wkh, r), u, wk, r_state)
    o = _each(lambda e, qh, r, ph, wh: e * _dot16(qh, r) + _dot16(ph, wh), egc, q, r_state, pqk, w)
    r_new = _each(lambda gl, r, kd, wh: jnp.exp(gl) * r + _dot16(kd, wh, TN), g_last, r_state, kdec, w)
    return o, r_new, inv


DELTA_BLK = 512


def _delta_blocks(n_batch, seq):
    nblk = seq // DELTA_BLK
    cpb = DELTA_BLK // CHUNK

    def rows(width, order):
        return pl.BlockSpec((DELTA_BLK, width), lambda b, i: (b * nblk + order(i), 0))

    def states(order, side):
        return pl.BlockSpec((cpb, B_HEADS, side, side), lambda b, i: (b * nblk + order(i), 0, 0, 0))

    return nblk, cpb, rows, states


def _head_cols(h):
    return [pl.ds(part * B_HEADS * B_DIM + h * B_DIM, B_DIM) for part in range(3)]


def _load_heads(c_ref, bg_ref, state_ref, rows):
    bg_c = bg_ref[rows, :]
    cols = [_head_cols(h) for h in range(B_HEADS)]
    return ([state_ref[h] for h in range(B_HEADS)], [c_ref[rows, c[0]] for c in cols], [c_ref[rows, c[1]] for c in cols],
            [c_ref[rows, c[2]] for c in cols], [bg_c[:, h:h + 1] for h in range(B_HEADS)],
            [bg_c[:, B_HEADS + h:B_HEADS + h + 1] for h in range(B_HEADS)])


def _delta_fwd(conv, bg, n_batch, seq):
    nblk, cpb, rows_spec, states_spec = _delta_blocks(n_batch, seq)

    def forward(i):
        return i

    def body(c_ref, bg_ref, o_ref, st_ref, inv_ref, r_ref):
        @pl.when(pl.program_id(1) == 0)
        def _():
            r_ref[...] = jnp.zeros_like(r_ref)

        def step(c, carry):
            rows = pl.ds(pl.multiple_of(c * CHUNK, CHUNK), CHUNK)
            args = _load_heads(c_ref, bg_ref, r_ref, rows)
            o, r_new, inv = _delta_chunk(*args)
            for h in range(B_HEADS):
                st_ref[c, h] = args[0][h]
                inv_ref[c, h] = inv[h]
                o_ref[rows, pl.ds(h * B_DIM, B_DIM)] = o[h]
            for h in range(B_HEADS):
                r_ref[h] = r_new[h]
            return carry

        lax.fori_loop(0, cpb, step, 0)

    n_tok = n_batch * seq
    return pl.pallas_call(
        body,
        name="delta_fwd",
        grid=(n_batch, nblk),
        in_specs=[rows_spec(B_CONV, forward), rows_spec(LANE, forward)],
        out_specs=[rows_spec(B_HEADS * B_DIM, forward), states_spec(forward, B_DIM), states_spec(forward, CHUNK)],
        out_shape=[jax.ShapeDtypeStruct((n_tok, B_HEADS * B_DIM), F32),
                   jax.ShapeDtypeStruct((n_tok // CHUNK, B_HEADS, B_DIM, B_DIM), F32),
                   jax.ShapeDtypeStruct((n_tok // CHUNK, B_HEADS, CHUNK, CHUNK), F32)],
        scratch_shapes=[pltpu.VMEM((B_HEADS, B_DIM, B_DIM), F32)],
        compiler_params=pltpu.CompilerParams(dimension_semantics=("arbitrary", "arbitrary")),
    )(conv, bg)


def _delta_bwd(conv, bg, states, inverses, do, n_batch, seq):
    nblk, cpb, rows_spec, states_spec = _delta_blocks(n_batch, seq)

    def backward(i):
        return nblk - 1 - i

    def body(c_ref, bg_ref, st_ref, inv_ref, do_ref, dc_ref, dbg_ref, dr_ref):
        @pl.when(pl.program_id(1) == 0)
        def _():
            dr_ref[...] = jnp.zeros_like(dr_ref)

        def step(n, carry):
            c = cpb - 1 - n
            rows = pl.ds(pl.multiple_of(c * CHUNK, CHUNK), CHUNK)
            saved = [inv_ref[c, h] for h in range(B_HEADS)]
            _, vjp = jax.vjp(lambda *args: _delta_chunk(*args, saved_inv=saved)[:2],
                             *_load_heads(c_ref, bg_ref, st_ref.at[c], rows))
            do = [do_ref[rows, pl.ds(h * B_DIM, B_DIM)] for h in range(B_HEADS)]
            dr, dq, dk, dv, dbeta, dg = vjp((do, [dr_ref[h] for h in range(B_HEADS)]))
            lane = lax.broadcasted_iota(jnp.int32, (CHUNK, LANE), 1)
            dbg = jnp.zeros((CHUNK, LANE), F32)
            for h in range(B_HEADS):
                cq, ck, cv = _head_cols(h)
                dr_ref[h] = dr[h]
                dc_ref[rows, cq] = dq[h]
                dc_ref[rows, ck] = dk[h]
                dc_ref[rows, cv] = dv[h]
                dbg = dbg + jnp.where(lane == h, dbeta[h], 0.0) + jnp.where(lane == h + B_HEADS, dg[h], 0.0)
            dbg_ref[rows, :] = dbg
            return carry

        lax.fori_loop(0, cpb, step, 0)

    n_tok = n_batch * seq
    return pl.pallas_call(
        body,
        name="delta_bwd",
        grid=(n_batch, nblk),
        in_specs=[rows_spec(B_CONV, backward), rows_spec(LANE, backward), states_spec(backward, B_DIM),
                  states_spec(backward, CHUNK), rows_spec(B_HEADS * B_DIM, backward)],
        out_specs=[rows_spec(B_CONV, backward), rows_spec(LANE, backward)],
        out_shape=[jax.ShapeDtypeStruct((n_tok, B_CONV), F32), jax.ShapeDtypeStruct((n_tok, LANE), F32)],
        scratch_shapes=[pltpu.VMEM((B_HEADS, B_DIM, B_DIM), F32)],
        compiler_params=pltpu.CompilerParams(dimension_semantics=("arbitrary", "arbitrary")),
    )(conv, bg, states, inverses, do)


def _lane_row(vec4, first):
    return jnp.concatenate([jnp.zeros((1, first), F32), vec4.reshape(1, B_HEADS).astype(F32),
                            jnp.zeros((1, LANE - first - B_HEADS), F32)], axis=1)


def _local_step(x3d, p3d, tgt3d, w_in, small, rest_weights, send_grads, send_w_in):
    n_batch, seq, _ = x3d.shape
    n_tok = n_batch * seq
    x, p, tgt = x3d.reshape(n_tok, D), p3d.reshape(n_tok, -1), tgt3d.reshape(n_tok, D)
    g_mix, g_ffn, g_ple, g_final = (small[k].reshape(1, D) for k in ("g_mix", "g_ffn", "g_ple", "g_final"))
    w_onorm = small["w_onorm"].reshape(1, B_DIM)
    al_row = _lane_row(small["a_log"], B_HEADS)
    dtb_row = _lane_row(small["dt_bias"], B_HEADS)
    rel_bias = small["rel_bias"].reshape(A_HEADS, -1)
    bias = _rel_bias_table(rel_bias)
    conv_w = small["conv_w"].reshape(4, B_CONV)

    h1 = _rms_fwd(x, g_mix, name="rms_mix")
    projp = _mm(h1, w_in, tb=True, out_dtype=BF16, name="mm_proj", tn=1920)
    bd = _mm(h1, w_in[P_BD:], tb=True, name="mm_beta_decay", tn=LANE)
    y_a = _attn_fwd(projp, bias, n_batch, seq)
    conv = _conv_fwd(projp, conv_w, n_batch, seq)
    (bg,) = _rowwise(lambda raw, al, dtb: ([_gate_scalars(raw, al, dtb)], []), [_full(bd)],
                     [_full(al_row), _full(dtb_row)], [(LANE, F32, LANE, 0, 0)], name="gate_scalars", tr=1024)
    o_b, states, inverses = _delta_fwd(conv, bg, n_batch, seq)
    (y_b,) = _rowwise(lambda o, z, wn: ([_gated_norm(o, z, wn)], []), [(o_b, LANE, 0, 1), (projp, LANE, P_Z // LANE, 1)],
                      [_full(w_onorm)], [(B_HEADS * B_DIM, BF16, LANE, 0, 1)], name="gated_norm", tr=1024, ncol=B_HEADS)
    w = rest_weights(y_b)
    t_a = _mm(y_a, w["w_branch_a"], tb=True, out_dtype=BF16, name="mm_branch_a", tn=1024)
    t_b = _mm(y_b, w["w_branch_b"], tb=True, out_dtype=BF16, name="mm_branch_b", tn=1024)
    half = D // 2
    gate_rows = [(projp, half, P_GATE // half, 1), (projp, half, P_GATE // half + 2, 1), (t_a, half, 0, 1), (t_b, half, 0, 1)]
    (merged,) = _rowwise(lambda ga, gb, ta, tb: ([_merge(ga, gb, ta, tb)], []), gate_rows, [], [(D, BF16, half, 0, 1)],
                         name="merge", tr=512, ncol=2)
    x1, h2 = _mm(merged, w["w_out"], name="mm_out", tn=1024, epilogue=_residual_rms, rows=[x], bcs=[g_ffn], outs=(F32, BF16))
    gu = _mm(h2, w["w_gate_up"], tb=True, out_dtype=BF16, name="mm_gate_up", tn=2816)
    (act,) = _rowwise(lambda gub: ([_swiglu(gub)], []), [_full(gu)], [], [(D_FF, BF16, D_FF, 0, 0)], name="swiglu",
                      tr=512)
    x2, h3 = _mm(act, w["w_down"], name="mm_down", tm=512, tn=1024, tk=D_FF, epilogue=_residual_rms, rows=[x1], bcs=[g_ple],
                 outs=(F32, BF16))
    pp = _mm(p, w["w_ple_proj"], tb=True, name="mm_ple_proj", tn=1024)

    def head_fn(pgb, x2b, ppb, tb, gb):
        loss, (dx2, dpg, dpp, dg) = jax.value_and_grad(_head_loss, argnums=(0, 1, 2, 4))(x2b, pgb, ppb, tb, gb)
        return [dx2, dpg, dpp], [dg, jnp.full((1, D), loss, F32)]

    dx3, dpg, dpp, dg_final, loss_row = _mm(h3, w["w_ple_gate"], name="mm_ple_gate_loss", tm=256, tn=1024, epilogue=head_fn,
                                            rows=[x2, pp, tgt], bcs=[g_final], outs=(F32, BF16, BF16), n_red=2)
    gw = {}
    gw["w_ple_proj"] = _mm(dpp, p, ta=True, out_dtype=BF16, name="mm_d_ple_proj", tn=256)
    gw["w_ple_gate"] = _mm(h3, dpg, ta=True, out_dtype=BF16, name="mm_d_ple_gate", tn=512)
    dx2, dg_ple = _mm(dpg, w["w_ple_gate"], tb=True, name="mm_dh3", tm=512, tn=1024, epilogue=_rms_bwd, rows=[x2, dx3],
                      bcs=[g_ple], outs=(F32,), n_red=1)
    gw["w_down"] = _mm(act, dx2, ta=True, out_dtype=BF16, name="mm_d_down", tm=1408, tn=256)
    dact = _mm(dx2, w["w_down"], tb=True, out_dtype=BF16, name="mm_dact", tn=D_FF)

    def swiglu_bwd(gub, dab):
        _, vjp = jax.vjp(_swiglu, gub)
        return [vjp(dab)[0]], []

    (dgu,) = _rowwise(swiglu_bwd, [_full(gu), _full(dact)], [], [(2 * D_FF, BF16, 2 * D_FF, 0, 0)], name="swiglu_bwd", tr=256)
    gw["w_gate_up"] = _mm(dgu, h2, ta=True, out_dtype=BF16, name="mm_d_gate_up", tm=512, tn=1024)
    dx1, dg_ffn = _mm(dgu, w["w_gate_up"], name="mm_dh2", tm=256, tn=1024, tk=2 * D_FF, epilogue=_rms_bwd, rows=[x1, dx2],
                      bcs=[g_ffn], outs=(F32,), n_red=1)
    gw["w_out"] = _mm(merged, dx1, ta=True, out_dtype=BF16, name="mm_d_out", tn=512)
    dmerged = _mm(dx1, w["w_out"], tb=True, name="mm_dmerged", tn=1024)

    def merge_bwd(ga, gb, ta, tb, dm):
        _, vjp = jax.vjp(_merge, ga, gb, ta, tb)
        return list(vjp(dm)), []

    dga, dgb, dta, dtb = _rowwise(merge_bwd, gate_rows + [(dmerged, half, 0, 1)], [], [(D, BF16, half, 0, 1)] * 4,
                                  name="merge_bwd", tr=512, ncol=2)
    gw["w_branch_a"] = _mm(dta, y_a, ta=True, out_dtype=BF16, name="mm_d_branch_a", tn=512)
    gw["w_branch_b"] = _mm(dtb, y_b, ta=True, out_dtype=BF16, name="mm_d_branch_b", tn=512)
    dya = _mm(dta, w["w_branch_a"], name="mm_dya", tn=512)
    dyb = _mm(dtb, w["w_branch_b"], name="mm_dyb", tn=512)

    w_onorm = w_onorm + send_grads(gw)[0, 0]

    def gated_norm_bwd(o, z, dy, wn):
        _, vjp = jax.vjp(_gated_norm, o, z, wn)
        do, dz, dwn = vjp(dy)
        return [do, dz], [dwn]

    do_b, dz, dw_onorm = _rowwise(
        gated_norm_bwd, [(o_b, LANE, 0, 1), (projp, LANE, P_Z // LANE, 1), (dyb, LANE, 0, 1)], [_full(w_onorm)],
        [(B_HEADS * B_DIM, F32, LANE, 0, 1), (B_HEADS * B_DIM, BF16, LANE, 0, 1)], [(B_DIM, B_DIM, 0)],
        name="gated_norm_bwd", tr=1024, ncol=B_HEADS)
    dconv_out, dbg = _delta_bwd(conv, bg, states, inverses, do_b, n_batch, seq)

    def gate_scalars_bwd(raw, dbgb, al, dtb):
        _, vjp = jax.vjp(_gate_scalars, raw, al, dtb)
        draw, dal, ddtb = vjp(dbgb)
        return [draw], [dal, ddtb]

    dbd, dal_row, ddtb_row = _rowwise(gate_scalars_bwd, [_full(bd), _full(dbg)], [_full(al_row), _full(dtb_row)],
                                      [(LANE, BF16, LANE, 0, 0)], [(LANE, LANE, 0), (LANE, LANE, 0)], name="gate_scalars_bwd",
                                      tr=1024)
    dconv, dconv_w = _conv_bwd(projp, conv_w, dconv_out, n_batch, seq)
    dq_a, dk_a, dv_a, dbias = _attn_bwd(projp, bias, dya, n_batch, seq)
    dprojp = jnp.concatenate([dq_a, dk_a, dv_a, dconv, dz, dga, dgb, dbd], axis=1)
    sent = send_w_in(_mm(dprojp, h1, ta=True, out_dtype=BF16, name="mm_d_in", tm=640, tn=1024))
    sent, dprojp = lax.optimization_barrier((sent, dprojp))
    grad_x, dg_mix = _mm(dprojp, w_in, name="mm_dh1", tm=256, tn=1024, tk=P_END, epilogue=_rms_bwd, rows=[x, dx1],
                         bcs=[g_mix + sent[0, 0]], outs=(F32,), n_red=1)

    _, bias_vjp = jax.vjp(_rel_bias_table, rel_bias)
    gs = {
        "g_mix": dg_mix, "g_ffn": dg_ffn, "g_ple": dg_ple, "g_final": dg_final, "w_onorm": dw_onorm,
        "conv_w": dconv_w, "rel_bias": bias_vjp(dbias)[0],
        "a_log": dal_row[0, B_HEADS:2 * B_HEADS], "dt_bias": ddtb_row[0, B_HEADS:2 * B_HEADS],
    }
    return loss_row[:, :1], grad_x.reshape(n_batch, seq, D), gs


MATRICES = (("w_in", 1), ("w_gate_up", 1), ("w_branch_a", 1), ("w_branch_b", 1), ("w_out", 0), ("w_down", 0),
            ("w_ple_gate", 0), ("w_ple_proj", 1))
TAPS_PER_SHARD = B_CONV // N_DEV


def _held(shard, axis):
    return shard if axis == 0 else shard.T


def _from_gathered(slabs):
    return slabs.reshape(-1, slabs.shape[-1])


def _to_owner(held):
    return held.reshape(N_DEV, held.shape[0] // N_DEV, held.shape[1])


def _permute_w_in(held):
    n_gate = P_BD - P_GATE
    row = lax.broadcasted_iota(jnp.int32, (P_END, 1), 0)
    same = jnp.pad(held, ((0, P_END - D_IN), (0, 0)))
    up = jnp.pad(held[8:], ((0, P_END - D_IN + 8), (0, 0)))
    down = jnp.pad(held[:P_GATE + 8], ((n_gate, P_END - P_BD - 8), (0, 0)))
    zero = jnp.zeros((), held.dtype)
    return jnp.where(row < P_GATE, same, jnp.where(row < P_BD, up, jnp.where(row < P_BD + 8, down, zero)))


def _unpermute_w_in(gp):
    n_gate = P_BD - P_GATE
    row = lax.broadcasted_iota(jnp.int32, (D_IN, 1), 0)
    same = gp[:D_IN]
    up = jnp.pad(gp[n_gate:], ((0, D_IN - (P_END - n_gate)), (0, 0)))
    down = jnp.pad(gp[:P_BD], ((8, 0), (0, 0)))
    return jnp.where(row < P_GATE, same, jnp.where(row < P_GATE + 8, up, down))


SMALL_ROWS = 16
SMALL_LAYOUT = (("g_mix", 0, D), ("g_ffn", 1, D), ("g_ple", 2, D), ("g_final", 3, D), ("conv_w", 4, 4 * B_CONV),
                ("rel_bias", 10, A_HEADS * (2 * REL_CLIP + 1)), ("w_onorm", 13, B_DIM), ("a_log", 14, B_HEADS),
                ("dt_bias", 14, B_HEADS), ("loss", 15, 1))


def _pack_small(gs):
    rows = {}
    for name, row, n in SMALL_LAYOUT:
        rows.setdefault(row, []).append(gs[name].reshape(-1).astype(F32))
    parts = []
    for row in sorted(rows):
        flat = jnp.concatenate(rows[row])
        parts.append(jnp.concatenate([flat, jnp.zeros((-flat.shape[0] % D,), F32)]))
    flat = jnp.concatenate(parts)
    assert flat.shape[0] == SMALL_ROWS * D, flat.shape
    return flat.reshape(SMALL_ROWS, D)


def _unpack_small(blk):
    flat, out, used = blk.reshape(-1), {}, {}
    for name, row, n in SMALL_LAYOUT:
        start = row * D + used.get(row, 0)
        out[name] = flat[start:start + n]
        used[row] = used.get(row, 0) + n
    return out


def _position():
    return lax.axis_index("x"), lax.axis_index("y"), lax.axis_index("c")


PEERS = N_DEV - 1


def _comm_call(body, arrays, out_shapes, *, name):
    n = len(arrays)
    return pl.pallas_call(
        body,
        name=name,
        out_shape=out_shapes,
        in_specs=[HBM_SPEC] * n,
        out_specs=[HBM_SPEC] * n,
        scratch_shapes=[pltpu.SemaphoreType.DMA((PEERS * n,)), pltpu.SemaphoreType.DMA((PEERS * n,)),
                        pltpu.SemaphoreType.DMA((n,))],
    )(*arrays)


def _weights_allgather(shards):
    n = len(shards)

    def body(*refs):
        ins, outs = refs[:n], refs[n:2 * n]
        send_sems, recv_sems, local_sems = refs[2 * n:]
        x, y, c = _position()
        me, sibling = (x, y, c), (x, y, 1 - c)
        chips = [(1 - x, y), (x, 1 - y), (1 - x, 1 - y)]

        def slab(a, px, py, pc):
            return outs[a].at[4 * px + 2 * py + pc]

        def copy(a, k, block, to, src=None):
            return pltpu.make_async_remote_copy(src_ref=slab(a, *block) if src is None else src, dst_ref=slab(a, *block),
                                                send_sem=send_sems.at[PEERS * a + k], recv_sem=recv_sems.at[PEERS * a + k],
                                                device_id=to, device_id_type=MESH)

        local = [pltpu.make_async_copy(ins[a], slab(a, *me), local_sems.at[a]) for a in range(n)]
        sent = [copy(a, 1 + j, me, (*chip, c), src=ins[a]) for a in range(n) for j, chip in enumerate(chips)]
        sent += [copy(a, 0, me, sibling, src=ins[a]) for a in range(n)]
        for cp in sent + local:
            cp.start()
        for a in range(n):
            for j, chip in enumerate(chips):
                copy(a, 1 + j, (*chip, c), me).wait_recv()
                passed = copy(a, 4 + j, (*chip, c), sibling)
                passed.start()
                sent.append(passed)
        for a in range(n):
            copy(a, 0, sibling, me).wait_recv()
            for j, chip in enumerate(chips):
                copy(a, 4 + j, (*chip, 1 - c), me).wait_recv()
        for cp in sent:
            cp.wait_send()
        for cp in local:
            cp.wait()

    return _comm_call(body, shards, [jax.ShapeDtypeStruct((N_DEV,) + s.shape, s.dtype) for s in shards],
                      name="weights_allgather")


def _grads_exchange(by_owner):
    n = len(by_owner)

    def body(*refs):
        ins, outs = refs[:n], refs[n:2 * n]
        send_sems, recv_sems, local_sems = refs[2 * n:]
        x, y, c = _position()
        mine = 4 * x + 2 * y + c
        local = [pltpu.make_async_copy(ins[a].at[mine], outs[a].at[mine], local_sems.at[a]) for a in range(n)]
        for cp in local:
            cp.start()
        flips = [(dx, dy, dc) for dx in (0, 1) for dy in (0, 1) for dc in (0, 1) if dx + dy + dc]
        pending = []
        for k, (dx, dy, dc) in enumerate(flips):
            px, py, pc = (1 - x if dx else x), (1 - y if dy else y), (1 - c if dc else c)
            peer = 4 * px + 2 * py + pc
            for a in range(n):
                def remote(slot):
                    return pltpu.make_async_remote_copy(src_ref=ins[a].at[peer], dst_ref=outs[a].at[slot],
                                                        send_sem=send_sems.at[PEERS * a + k], recv_sem=recv_sems.at[PEERS * a + k],
                                                        device_id=(px, py, pc), device_id_type=MESH)

                sent = remote(mine)
                sent.start()
                pending.append((sent, remote(peer)))
        for sent, landed in pending:
            landed.wait_recv()
            sent.wait_send()
        for cp in local:
            cp.wait()

    return _comm_call(body, by_owner, [jax.ShapeDtypeStruct(g.shape, g.dtype) for g in by_owner], name="grads_exchange")


SEM_SPEC = pl.BlockSpec(memory_space=pltpu.SEMAPHORE)
DATAFLOW = pltpu.SideEffectType.DATAFLOW_SIDE_EFFECTING


def _peer_copies(srcs, lands, send_sems, recv_sems, by_owner, arrival):
    x, y, c = _position()
    mine = 4 * x + 2 * y + c
    copies = []
    for k, (dx, dy, dc) in enumerate([(dx, dy, dc) for dx in (0, 1) for dy in (0, 1) for dc in (0, 1) if dx + dy + dc]):
        px, py, pc = (1 - x if dx else x), (1 - y if dy else y), (1 - c if dc else c)
        peer = 4 * px + 2 * py + pc
        for a, (src, land) in enumerate(zip(srcs, lands)):
            copies.append(pltpu.make_async_remote_copy(
                src_ref=src.at[peer] if by_owner else src, dst_ref=land.at[peer if arrival else mine],
                send_sem=send_sems.at[PEERS * a + k], recv_sem=recv_sems.at[PEERS * a + k],
                device_id=(px, py, pc), device_id_type=MESH))
    return copies


def _exchange_start(sources, by_owner, *, name):
    n = len(sources)
    lands = [lax.empty((N_DEV,) + (s.shape[1:] if by_owner else s.shape), s.dtype) for s in sources]

    def body(*refs):
        send_sems, recv_sems, token = refs[2 * n], refs[2 * n + 1], refs[-1]
        for copy in _peer_copies(refs[:n], refs[n:2 * n], send_sems, recv_sems, by_owner, arrival=False):
            copy.start()
        token[...] = jnp.zeros_like(token)

    sems = pltpu.SemaphoreType.DMA((PEERS * n,))
    outs = pl.pallas_call(
        body,
        name=name,
        out_shape=(sems, sems, *[pltpu.HBM(a.shape, a.dtype) for a in sources + lands], jax.ShapeDtypeStruct((8, LANE), F32)),
        in_specs=[HBM_SPEC] * (2 * n),
        out_specs=(SEM_SPEC, SEM_SPEC, *[HBM_SPEC] * (2 * n), pl.BlockSpec(memory_space=pltpu.VMEM)),
        input_output_aliases={i: 2 + i for i in range(2 * n)},
        compiler_params=pltpu.CompilerParams(has_side_effects=DATAFLOW),
    )(*[pltpu.with_memory_space_constraint(a, pltpu.HBM) for a in sources + lands])
    return outs[:-1], outs[-1]


def _exchange_wait(started, after, by_owner, *, name):
    send_sems, recv_sems, *arrays = started
    n = len(arrays) // 2

    def body(*refs):
        for copy in _peer_copies(refs[:n], refs[n:2 * n], refs[2 * n], refs[2 * n + 1], by_owner, arrival=True):
            copy.wait_send()
            copy.wait_recv()

    outs = pl.pallas_call(
        body,
        name=name,
        out_shape=[pltpu.HBM(a.shape, a.dtype) for a in arrays],
        in_specs=[HBM_SPEC] * (2 * n) + [SEM_SPEC, SEM_SPEC, pl.BlockSpec(memory_space=pl.ANY)],
        out_specs=[HBM_SPEC] * (2 * n),
        input_output_aliases={i: i for i in range(2 * n)},
        compiler_params=pltpu.CompilerParams(has_side_effects=DATAFLOW),
    )(*arrays, send_sems, recv_sems, after)
    return outs[:n], outs[n:]


def _slot_sum(g_ref, own_ref):
    if own_ref is not None:
        x, y, c = _position()
        mine = 4 * x + 2 * y + c
    acc = None
    for j in range(N_DEV):
        part = g_ref[j] if own_ref is None else jnp.where(mine == j, own_ref[...], g_ref[j])
        acc = part.astype(F32) if acc is None else acc + part.astype(F32)
    return acc


def _sum_slots(got, *, name, tr):
    _, rows, cols = got.shape
    tr = _tile(rows, tr, 16)

    def body(g_ref, o_ref):
        o_ref[...] = _slot_sum(g_ref, None)

    return pl.pallas_call(
        body,
        name=name,
        grid=(rows // tr,),
        in_specs=[pl.BlockSpec((N_DEV, tr, cols), lambda i: (0, i, 0))],
        out_specs=pl.BlockSpec((tr, cols), lambda i: (i, 0)),
        out_shape=jax.ShapeDtypeStruct((rows, cols), F32),
        compiler_params=pltpu.CompilerParams(dimension_semantics=("parallel",)),
    )(got)


def _adamw(wt, g, m, v, *, name, own=None):
    slots = own is not None
    shape = wt.shape
    two_d = (-1, shape[-1]) if wt.ndim > 1 else (1, -1)
    args = [a.reshape(two_d) for a in (wt, m, v)]
    rows, cols = args[0].shape
    if rows % 16 == 0:
        tr, tc = _tile(rows, 256, 16), cols
    else:
        tr, tc = rows, _tile(cols, 256 if rows > 64 else 512)
    args.insert(1, g.reshape((N_DEV, rows, cols) if slots else (rows, cols)))
    if slots:
        args.append(own.reshape(rows, cols))

    def body(w_ref, g_ref, m_ref, v_ref, *refs):
        go_ref, d_ref, nm_ref, nv_ref = refs[-4:]
        gv = _slot_sum(g_ref, refs[0]) if slots else g_ref[...]
        go_ref[...] = gv
        m2 = ADAM_B1 * m_ref[...] + (1.0 - ADAM_B1) * gv
        v2 = ADAM_B2 * v_ref[...] + (1.0 - ADAM_B2) * (gv * gv)
        m_hat = m2 / (1.0 - ADAM_B1 ** ADAM_STEP)
        v_hat = v2 / (1.0 - ADAM_B2 ** ADAM_STEP)
        d_ref[...] = -ADAM_LR * (m_hat / (jnp.sqrt(v_hat) + ADAM_EPS) + ADAM_WD * w_ref[...])
        nm_ref[...] = m2
        nv_ref[...] = v2

    spec = pl.BlockSpec((tr, tc), lambda i, j: (i, j))
    g_spec = pl.BlockSpec((N_DEV, tr, tc), lambda i, j: (0, i, j)) if slots else spec
    outs = pl.pallas_call(
        body,
        name=name,
        grid=(rows // tr, cols // tc),
        in_specs=[spec, g_spec, spec, spec] + ([spec] if slots else []),
        out_specs=[spec] * 4,
        out_shape=[jax.ShapeDtypeStruct((rows, cols), F32)] * 4,
        compiler_params=pltpu.CompilerParams(dimension_semantics=("parallel", "parallel")),
    )(*args)
    return tuple(o.reshape(shape) for o in outs)


WEIGHTS = ("g_mix", "w_in", "conv_w", "a_log", "dt_bias", "rel_bias", "w_onorm", "w_branch_a", "w_branch_b", "w_out", "g_ffn",
           "w_gate_up", "w_down", "g_ple", "w_ple_gate", "w_ple_proj", "g_final")


def kernel(x, p, g_mix, w_in, conv_w, a_log, dt_bias, rel_bias, w_onorm, w_branch_a, w_branch_b, w_out, g_ffn, w_gate_up, w_down, g_ple, w_ple_gate, w_ple_proj, g_final, loss_target, m_g_mix, m_w_in, m_conv_w, m_a_log, m_dt_bias, m_rel_bias, m_w_onorm, m_w_branch_a, m_w_branch_b, m_w_out, m_g_ffn, m_w_gate_up, m_w_down, m_g_ple, m_w_ple_gate, m_w_ple_proj, m_g_final, v_g_mix, v_w_in, v_conv_w, v_a_log, v_dt_bias, v_rel_bias, v_w_onorm, v_w_branch_a, v_w_branch_b, v_w_out, v_g_ffn, v_w_gate_up, v_w_down, v_g_ple, v_w_ple_gate, v_w_ple_proj, v_g_final):
    given = dict(g_mix=g_mix, w_in=w_in, conv_w=conv_w, a_log=a_log, dt_bias=dt_bias, rel_bias=rel_bias, w_onorm=w_onorm,
                 w_branch_a=w_branch_a, w_branch_b=w_branch_b, w_out=w_out, g_ffn=g_ffn, w_gate_up=w_gate_up, w_down=w_down,
                 g_ple=g_ple, w_ple_gate=w_ple_gate, w_ple_proj=w_ple_proj, g_final=g_final)
    mom1 = dict(g_mix=m_g_mix, w_in=m_w_in, conv_w=m_conv_w, a_log=m_a_log, dt_bias=m_dt_bias, rel_bias=m_rel_bias,
                w_onorm=m_w_onorm, w_branch_a=m_w_branch_a, w_branch_b=m_w_branch_b, w_out=m_w_out, g_ffn=m_g_ffn,
                w_gate_up=m_w_gate_up, w_down=m_w_down, g_ple=m_g_ple, w_ple_gate=m_w_ple_gate, w_ple_proj=m_w_ple_proj,
                g_final=m_g_final)
    mom2 = dict(g_mix=v_g_mix, w_in=v_w_in, conv_w=v_conv_w, a_log=v_a_log, dt_bias=v_dt_bias, rel_bias=v_rel_bias,
                w_onorm=v_w_onorm, w_branch_a=v_w_branch_a, w_branch_b=v_w_branch_b, w_out=v_w_out, g_ffn=v_g_ffn,
                w_gate_up=v_w_gate_up, w_down=v_w_down, g_ple=v_g_ple, w_ple_gate=v_w_ple_gate, w_ple_proj=v_w_ple_proj,
                g_final=v_g_final)
    mine = 4 * lax.axis_index("x") + 2 * lax.axis_index("y") + lax.axis_index("c")

    my_slot = (jnp.arange(N_DEV) == mine)[:, None, None]
    rest = MATRICES[1:]
    in_flight = {}

    got_in, got_taps = _weights_allgather([_held(w_in[0], 1).astype(BF16), conv_w[0]])
    in_flight["weights"], weights_sent = _exchange_start([_held(given[name][0], axis).astype(BF16) for name, axis in rest], False,
                                                         name="weights_start")
    small = dict(g_mix=g_mix + weights_sent[0, 0], g_ffn=g_ffn, g_ple=g_ple, g_final=g_final, w_onorm=w_onorm, a_log=a_log,
                 dt_bias=dt_bias, rel_bias=rel_bias, conv_w=jnp.transpose(got_taps, (1, 0, 2)).reshape(4, B_CONV))

    def rest_weights(after):
        shards, landed = _exchange_wait(in_flight.pop("weights"), after, False, name="weights_wait")
        return {name: _from_gathered(jnp.where(my_slot, shard[None], slabs)) for (name, _), shard, slabs in zip(rest, shards, landed)}

    def send_grads(gw):
        in_flight["grads"], sent = _exchange_start([_to_owner(gw[name]) for name, _ in rest], True, name="grads_start")
        return sent

    def send_w_in(g_in):
        in_flight["grad_in"], sent = _exchange_start([_to_owner(_unpermute_w_in(g_in))], True, name="grad_in_start")
        return sent

    loss_part, grad_x, gs = _local_step(x, p[0], loss_target, _permute_w_in(_from_gathered(got_in)), small, rest_weights,
                                        send_grads, send_w_in)
    gs["loss"] = loss_part

    updates = {}

    def update_matrices(matrices, own_slabs, landed):
        for (name, axis), own_slab, slots in zip(matrices, own_slabs, landed):
            mine_of = lax.dynamic_index_in_dim(own_slab, mine, axis=0, keepdims=False)
            w_held, m_held, v_held = (_held(a[name][0], axis) for a in (given, mom1, mom2))
            outs = _adamw(w_held, slots, m_held, v_held, name=f"adamw_{name}", own=mine_of)
            updates[name] = tuple(_held(o, axis)[None] for o in outs)

    update_matrices(rest, *_exchange_wait(in_flight["grads"], grad_x, True, name="grads_wait"))
    update_matrices(MATRICES[:1], *_exchange_wait(in_flight["grad_in"], updates[rest[-1][0]][0], True, name="grad_in_wait"))
    small_block, _ = lax.optimization_barrier((_pack_small(gs), updates["w_in"][0]))
    (got_small,) = _grads_exchange([jnp.broadcast_to(small_block, (N_DEV, SMALL_ROWS, D))])
    small_sum = _unpack_small(_sum_slots(got_small, name="sum_small_grads", tr=16))
    loss = small_sum.pop("loss")[0]
    conv_all = small_sum.pop("conv_w").reshape(4, N_DEV, TAPS_PER_SHARD)
    small_sum["conv_w"] = lax.dynamic_index_in_dim(conv_all, mine, axis=1, keepdims=False)
    for name, g in small_sum.items():
        updates[name] = _adamw(given[name], g.reshape(given[name].shape), mom1[name], mom2[name], name=f"adamw_{name}")
    return (loss, grad_x, *[updates[name][k] for k in range(4) for name in WEIGHTS])
```

```python
import jax
import jax.numpy as jnp
from jax import lax
from jax.experimental import pallas as pl
from jax.experimental.pallas import tpu as pltpu

F32 = jnp.float32
BF16 = jnp.bfloat16
DELTA_PREC = lax.Precision.HIGH
MESH = pl.DeviceIdType.MESH

N_DEV = 8
D = 1024
CHUNK = 64
EPS = 1e-6
A_HEADS, A_DIM, A_WIDTH = 8, 64, 512
A_BAND = 9 * CHUNK
A_PAD = 8 * CHUNK
REL_CLIP = 128
B_HEADS, B_DIM = 4, 128
B_CONV = 1536
D_FF = 2816
D_IN = 5640
P_CONV, P_Z, P_GATE, P_BD, P_END = 1536, 3072, 3584, 5632, 5760
LANE = 128

ADAM_LR, ADAM_B1, ADAM_B2, ADAM_EPS, ADAM_WD, ADAM_STEP = 0.001, 0.9, 0.999, 1e-08, 0.01, 10

NT = (((1,), (1,)), ((), ()))
TN = (((0,), (0,)), ((), ()))
NN = (((1,), (0,)), ((), ()))

HBM_SPEC = pl.BlockSpec(memory_space=pltpu.HBM)


def _tile(n, target, align=LANE):
    if n <= target:
        return n
    best = None
    for t in range(align, target + 1, align):
        if n % t == 0:
            best = t
    assert best is not None, (n, target, align)
    return best


def _mm(a, b, *, name, ta=False, tb=False, out_dtype=F32, tm=1024, tn=640, tk=None, epilogue=None, rows=(), bcs=(), outs=(),
        n_red=0):
    assert not (ta and tb)
    if ta:
        k_dim, m_dim = a.shape
    else:
        m_dim, k_dim = a.shape
    n_dim = b.shape[0] if tb else b.shape[1]
    assert b.shape[1 if tb else 0] == k_dim
    tm, tn = _tile(m_dim, tm), _tile(n_dim, tn)
    tk = _tile(k_dim, tk or (4096 if ta else 1024), 8 if ta else LANE)
    nk = k_dim // tk
    dn = TN if ta else NT if tb else NN
    if epilogue is None:
        outs = (out_dtype,)
    assert not n_red or tn == n_dim
    n_extra, n_out = len(rows) + len(bcs), len(outs)

    def body(a_ref, b_ref, *refs):
        part = lax.dot_general(a_ref[...].astype(BF16), b_ref[...].astype(BF16), dn, preferred_element_type=F32)

        def finish(r):
            o_vals, r_vals = ([r], []) if epilogue is None else epilogue(r, *[x[...] for x in refs[:n_extra]])
            for ref, val in zip(refs[n_extra:n_extra + n_out], o_vals):
                ref[...] = val.astype(ref.dtype)
            first = pl.program_id(0) == 0
            for ref, val in zip(refs[n_extra + n_out:n_extra + n_out + n_red], r_vals):
                @pl.when(first)
                def _():
                    ref[...] = val

                @pl.when(jnp.logical_not(first))
                def _():
                    ref[...] += val

        if nk == 1:
            finish(part)
        else:
            acc_ref = refs[-1]
            k = pl.program_id(2)

            @pl.when(k == 0)
            def _():
                acc_ref[...] = part

            @pl.when(k > 0)
            def _():
                acc_ref[...] += part

            @pl.when(k == nk - 1)
            def _():
                finish(acc_ref[...])

    tile = pl.BlockSpec((tm, tn), lambda i, j, k: (i, j))
    col = pl.BlockSpec((1, tn), lambda i, j, k: (0, j))
    a_spec = pl.BlockSpec((tk, tm), lambda i, j, k: (k, i)) if ta else pl.BlockSpec((tm, tk), lambda i, j, k: (i, k))
    b_spec = pl.BlockSpec((tn, tk), lambda i, j, k: (j, k)) if tb else pl.BlockSpec((tk, tn), lambda i, j, k: (k, j))
    result = pl.pallas_call(
        body,
        name=name,
        grid=(m_dim // tm, n_dim // tn, nk),
        in_specs=[a_spec, b_spec] + [tile] * len(rows) + [col] * len(bcs),
        out_specs=[tile] * n_out + [col] * n_red,
        out_shape=[jax.ShapeDtypeStruct((m_dim, n_dim), dt) for dt in outs] + [jax.ShapeDtypeStruct((1, n_dim), F32)] * n_red,
        scratch_shapes=[pltpu.VMEM((tm, tn), F32)] if nk > 1 else [],
        compiler_params=pltpu.CompilerParams(dimension_semantics=("arbitrary",) * 3 if n_red else ("parallel", "parallel", "arbitrary")),
    )(a, b, *rows, *bcs)
    return result[0] if epilogue is None else result


def _rowwise(fn, rows, bcs, outs, reds=(), *, name, tr, ncol=1):
    n_rows = rows[0][0].shape[0]
    tr = _tile(n_rows, tr, 8)
    nrow = n_rows // tr
    n_in, n_out = len(rows) + len(bcs), len(outs)

    def body(*refs):
        j, i = pl.program_id(0), pl.program_id(1)
        o_vals, r_vals = fn(*[r[...].astype(F32) for r in refs[:n_in]])
        for ref, val in zip(refs[n_in:n_in + n_out], o_vals):
            ref[...] = val.astype(ref.dtype)
        for ref, val, (_, _, stride) in zip(refs[n_in + n_out:], r_vals, reds):
            first = (i == 0) if stride else jnp.logical_and(i == 0, j == 0)

            @pl.when(first)
            def _():
                ref[...] = val

            @pl.when(jnp.logical_not(first))
            def _():
                ref[...] += val

    def spec(r, w, off, st, row_dep=True):
        if row_dep:
            return pl.BlockSpec((r, w), lambda j, i: (i, off + st * j))
        return pl.BlockSpec((r, w), lambda j, i: (0, off + st * j))

    in_specs = [spec(tr, w, off, st) for (_, w, off, st) in rows]
    in_specs += [spec(a.shape[0], w, off, st, False) for (a, w, off, st) in bcs]
    out_specs = [spec(tr, w, off, st) for (_, _, w, off, st) in outs]
    out_specs += [spec(1, w, 0, st, False) for (_, w, st) in reds]
    out_shape = [jax.ShapeDtypeStruct((n_rows, c), dt) for (c, dt, _, _, _) in outs]
    out_shape += [jax.ShapeDtypeStruct((1, c), F32) for (c, _, _) in reds]
    return pl.pallas_call(
        body,
        name=name,
        grid=(ncol, nrow),
        in_specs=in_specs,
        out_specs=out_specs,
        out_shape=out_shape,
        compiler_params=pltpu.CompilerParams(dimension_semantics=("arbitrary", "arbitrary")),
    )(*[r[0] for r in rows], *[b[0] for b in bcs])


def _full(a):
    return (a, a.shape[1], 0, 0)


def _rms(x, g):
    return x * lax.rsqrt(jnp.mean(x * x, axis=-1, keepdims=True) + EPS) * g


def _silu(x):
    return x * jax.nn.sigmoid(x)


def _softplus(x):
    return jnp.maximum(x, 0.0) + jnp.log(1.0 + jnp.exp(-jnp.abs(x)))


def _rms_fwd(x, g, *, name):
    (h,) = _rowwise(lambda xb, gb: ([_rms(xb, gb)], []), [_full(x)], [_full(g)], [(D, BF16, D, 0, 0)], name=name, tr=512)
    return h


def _residual_rms(r, x, g):
    x_new = x + r
    return [x_new, _rms(x_new, g)], []


def _rms_bwd(dh, x, dres, g):
    _, vjp = jax.vjp(_rms, x, g)
    dx, dg = vjp(dh)
    return [dx + dres], [dg]


def _gate_scalars(raw, al_row, dtb_row):
    lane = lax.broadcasted_iota(jnp.int32, raw.shape, 1)
    beta = jax.nn.sigmoid(raw)
    g = -jnp.exp(al_row) * _softplus(raw + dtb_row)
    return jnp.where(lane < B_HEADS, beta, jnp.where(lane < 2 * B_HEADS, g, 0.0))


def _gated_norm(o, z, w):
    return _rms(o, w) * _silu(z)


def _merge(ga, gb, ta, tb):
    return jax.nn.sigmoid(ga) * ta + jax.nn.sigmoid(gb) * tb


def _swiglu(gu):
    return _silu(gu[:, :D_FF]) * gu[:, D_FF:]


def _head_loss(x2, pg, pp, tgt, g):
    x3 = x2 + jax.nn.sigmoid(pg) * pp
    err = _rms(x3, g) - tgt
    return 0.5 * jnp.sum(jnp.mean(err * err, axis=-1))


CONV_W = 256


def _conv_taps(x, w):
    row = lax.broadcasted_iota(jnp.int32, x.shape, 0)
    shifted = [x] + [jnp.where(row >= s, pltpu.roll(x, s, 0), 0.0) for s in (1, 2, 3)]
    pre = shifted[0] * w[3:4]
    for s in (1, 2, 3):
        pre = pre + shifted[s] * w[3 - s:4 - s]
    return pre, shifted


def _conv_fwd(projp, conv_w, n_batch, seq):
    ncol = B_CONV // CONV_W
    first = P_CONV // CONV_W

    def body(x_ref, w_ref, o_ref):
        pre, _ = _conv_taps(x_ref[...].astype(F32), w_ref[...])
        o_ref[...] = _silu(pre)

    return pl.pallas_call(
        body,
        name="conv_fwd",
        grid=(ncol, n_batch),
        in_specs=[pl.BlockSpec((seq, CONV_W), lambda j, b: (b, first + j)), pl.BlockSpec((4, CONV_W), lambda j, b: (0, j))],
        out_specs=pl.BlockSpec((seq, CONV_W), lambda j, b: (b, j)),
        out_shape=jax.ShapeDtypeStruct((n_batch * seq, B_CONV), F32),
        compiler_params=pltpu.CompilerParams(dimension_semantics=("parallel", "parallel")),
    )(projp, conv_w)


def _conv_bwd(projp, conv_w, dc, n_batch, seq):
    width = dc.shape[1]
    ncol = width // CONV_W
    first_x = P_CONV // CONV_W

    def body(x_ref, w_ref, dc_ref, dx_ref, dw_ref):
        b = pl.program_id(1)
        w = w_ref[...]
        pre, shifted = _conv_taps(x_ref[...].astype(F32), w)
        sg = jax.nn.sigmoid(pre)
        dpre = dc_ref[...] * (sg * (1.0 + pre * (1.0 - sg)))
        row = lax.broadcasted_iota(jnp.int32, dpre.shape, 0)
        dx = dpre * w[3:4]
        for s in (1, 2, 3):
            dx = dx + jnp.where(row < seq - s, pltpu.roll(dpre, seq - s, 0), 0.0) * w[3 - s:4 - s]
        dx_ref[...] = dx.astype(dx_ref.dtype)
        for s in (0, 1, 2, 3):
            part = jnp.sum(dpre * shifted[s], axis=0, keepdims=True)

            @pl.when(b == 0)
            def _():
                dw_ref[3 - s:4 - s, :] = part

            @pl.when(b > 0)
            def _():
                dw_ref[3 - s:4 - s, :] += part

    return pl.pallas_call(
        body,
        name="conv_bwd",
        grid=(ncol, n_batch),
        in_specs=[
            pl.BlockSpec((seq, CONV_W), lambda j, b: (b, first_x + j)),
            pl.BlockSpec((4, CONV_W), lambda j, b: (0, j)),
            pl.BlockSpec((seq, CONV_W), lambda j, b: (b, j)),
        ],
        out_specs=[pl.BlockSpec((seq, CONV_W), lambda j, b: (b, j)), pl.BlockSpec((4, CONV_W), lambda j, b: (0, j))],
        out_shape=[jax.ShapeDtypeStruct((n_batch * seq, width), BF16), jax.ShapeDtypeStruct((4, width), F32)],
        compiler_params=pltpu.CompilerParams(dimension_semantics=("arbitrary", "arbitrary")),
    )(projp, conv_w, dc)


@jax.custom_vjp
def _attend(s, v):
    return _attend_fwd(s, v)[0]


def _attend_fwd(s, v):
    v16 = [t.astype(BF16) for t in v]
    p = [jnp.exp(t - jnp.max(t, axis=-1, keepdims=True)) for t in s]
    p = [t * (1.0 / jnp.sum(t, axis=-1, keepdims=True)) for t in p]
    o = [jnp.dot(t.astype(BF16), v16[n // 2], preferred_element_type=F32) for n, t in enumerate(p)]
    return o, (p, v16, o)


def _attend_bwd(saved, do):
    p, v16, o = saved
    do16 = [t.astype(BF16) for t in do]
    dv = [lax.dot_general(t.astype(BF16), do16[n], TN, preferred_element_type=F32) for n, t in enumerate(p)]
    dp = [lax.dot_general(t, v16[n // 2], NT, preferred_element_type=F32) for n, t in enumerate(do16)]
    delta = [jnp.sum(a * b, axis=-1, keepdims=True) for a, b in zip(do, o)]
    ds = [a * (b - c) for a, b, c in zip(p, dp, delta)]
    return ds, [dv[2 * i] + dv[2 * i + 1] for i in range(len(v16))]


_attend.defvjp(_attend_fwd, _attend_bwd)


def _attn_chunk(qc, kb, vb, bias2, valid, lane_lo):
    sel = (lane_lo, jnp.logical_not(lane_lo))
    items = [(i, e) for i in range(len(qc)) for e in (0, 1)]
    k16 = [t.astype(BF16) for t in kb]
    qm = [(jnp.where(sel[e], qc[i], 0.0) * (A_DIM ** -0.5)).astype(BF16) for i, e in items]
    s = [lax.dot_general(qm[n], k16[i], NT, preferred_element_type=F32) + bias2[e] for n, (i, e) in enumerate(items)]
    if valid is not None:
        s = [jnp.where(valid[i], s[n], -1e30) for n, (i, e) in enumerate(items)]
    o = [jnp.where(sel[e], t, 0.0) for t, (i, e) in zip(_attend(s, vb), items)]
    return [o[2 * i] + o[2 * i + 1] for i in range(len(qc))]


ATTN_GROUP_FWD, ATTN_GROUP_BWD = 8, 8


def _attn_loops(step, n_groups, n_masked):
    lax.fori_loop(0, n_masked, lambda g, c: step(g, c, True), 0)
    lax.fori_loop(n_masked, n_groups, lambda g, c: step(g, c, False), 0)


def _attn_group(g, group, q_ref, kp_ref, vp_ref):
    col = lax.broadcasted_iota(jnp.int32, (CHUNK, A_BAND), 1)
    lane_lo = lax.broadcasted_iota(jnp.int32, (1, LANE), 1) < A_DIM
    starts = [pl.multiple_of((g * group + i) * CHUNK, CHUNK) for i in range(group)]
    rows = [pl.ds(r0, CHUNK) for r0 in starts]
    bands = [pl.ds(r0, A_BAND) for r0 in starts]
    valid = [col + r0 >= A_PAD for r0 in starts]
    loaded = [q_ref[r, :].astype(F32) for r in rows], [kp_ref[b, :] for b in bands], [vp_ref[b, :] for b in bands]
    return rows, bands, loaded, valid, lane_lo


def _attn_specs(seq):
    def blk(first):
        return pl.BlockSpec((seq, LANE), lambda hp, b: (b, first + hp))

    return blk, pl.BlockSpec((2, CHUNK, A_BAND), lambda hp, b: (hp, 0, 0))


def _attn_fwd(projp, bias, n_batch, seq):
    nc = seq // CHUNK
    blk, bias_spec = _attn_specs(seq)

    def body(q_ref, k_ref, v_ref, b_ref, o_ref, kp_ref, vp_ref):
        kp_ref[0:A_PAD, :] = jnp.zeros((A_PAD, LANE), F32)
        vp_ref[0:A_PAD, :] = jnp.zeros((A_PAD, LANE), F32)
        kp_ref[A_PAD:, :] = k_ref[...].astype(F32)
        vp_ref[A_PAD:, :] = v_ref[...].astype(F32)
        bias2 = b_ref[...]

        def step(g, carry, masked):
            rows, _, (qc, kb, vb), valid, lane_lo = _attn_group(g, ATTN_GROUP_FWD, q_ref, kp_ref, vp_ref)
            out = _attn_chunk(qc, kb, vb, bias2, valid if masked else None, lane_lo)
            for r, o in zip(rows, out):
                o_ref[r, :] = o.astype(o_ref.dtype)
            return carry

        _attn_loops(step, nc // ATTN_GROUP_FWD, A_PAD // (CHUNK * ATTN_GROUP_FWD))

    return pl.pallas_call(
        body,
        name="attn_fwd",
        grid=(A_HEADS // 2, n_batch),
        in_specs=[blk(0), blk(4), blk(8), bias_spec],
        out_specs=pl.BlockSpec((seq, LANE), lambda hp, b: (b, hp)),
        out_shape=jax.ShapeDtypeStruct((n_batch * seq, A_WIDTH), BF16),
        scratch_shapes=[pltpu.VMEM((A_PAD + seq, LANE), F32), pltpu.VMEM((A_PAD + seq, LANE), F32)],
        compiler_params=pltpu.CompilerParams(dimension_semantics=("parallel", "parallel")),
    )(projp, projp, projp, bias)


def _attn_bwd(projp, bias, dy, n_batch, seq):
    nc = seq // CHUNK
    blk, bias_spec = _attn_specs(seq)
    out_blk = pl.BlockSpec((seq, LANE), lambda hp, b: (b, hp))

    def body(q_ref, k_ref, v_ref, b_ref, dy_ref, dq_ref, dk_ref, dv_ref, db_ref, kp_ref, vp_ref, dkp_ref, dvp_ref):
        b = pl.program_id(1)
        kp_ref[0:A_PAD, :] = jnp.zeros((A_PAD, LANE), F32)
        vp_ref[0:A_PAD, :] = jnp.zeros((A_PAD, LANE), F32)
        kp_ref[A_PAD:, :] = k_ref[...].astype(F32)
        vp_ref[A_PAD:, :] = v_ref[...].astype(F32)
        dkp_ref[...] = jnp.zeros_like(dkp_ref)
        dvp_ref[...] = jnp.zeros_like(dvp_ref)
        bias2 = b_ref[...]

        @pl.when(b == 0)
        def _():
            db_ref[...] = jnp.zeros_like(db_ref)

        def step(g, carry, masked):
            rows, bands, (qc, kb, vb), valid, lane_lo = _attn_group(g, ATTN_GROUP_BWD, q_ref, kp_ref, vp_ref)
            _, vjp = jax.vjp(lambda q, k, v, bb: _attn_chunk(q, k, v, bb, valid if masked else None, lane_lo), qc, kb, vb, bias2)
            dq, dk, dv, dbias = vjp([dy_ref[r, :] for r in rows])
            for i, r in enumerate(rows):
                dq_ref[r, :] = dq[i].astype(dq_ref.dtype)
            for i, band in enumerate(bands):
                dkp_ref[band, :] += dk[i]
                dvp_ref[band, :] += dv[i]
            db_ref[...] += dbias
            return carry

        _attn_loops(step, nc // ATTN_GROUP_BWD, A_PAD // (CHUNK * ATTN_GROUP_BWD))
        dk_ref[...] = dkp_ref[A_PAD:, :].astype(dk_ref.dtype)
        dv_ref[...] = dvp_ref[A_PAD:, :].astype(dv_ref.dtype)

    n_tok = n_batch * seq
    pad = pltpu.VMEM((A_PAD + seq, LANE), F32)
    return pl.pallas_call(
        body,
        name="attn_bwd",
        grid=(A_HEADS // 2, n_batch),
        in_specs=[blk(0), blk(4), blk(8), bias_spec, out_blk],
        out_specs=[out_blk, out_blk, out_blk, bias_spec],
        out_shape=[jax.ShapeDtypeStruct((n_tok, A_WIDTH), BF16)] * 3 + [jax.ShapeDtypeStruct((A_HEADS, CHUNK, A_BAND), F32)],
        scratch_shapes=[pad, pad, pad, pad],
        compiler_params=pltpu.CompilerParams(dimension_semantics=("arbitrary", "arbitrary")),
    )(projp, projp, projp, bias, dy)


def _rel_bias_table(rel_bias):
    span = CHUNK + A_BAND - 1
    near = REL_CLIP + CHUNK
    far = jnp.broadcast_to(rel_bias[:, 2 * REL_CLIP:], (A_HEADS, span - near))
    t = jnp.concatenate([rel_bias[:, 2 * REL_CLIP + 1 - near:], far], axis=1)
    u = jnp.concatenate([t[:, :A_BAND][:, ::-1], t[:, A_BAND:][:, ::-1]], axis=1)
    rolled = jnp.tile(u, (1, CHUNK))[:, :CHUNK * (span - 1)].reshape(A_HEADS, CHUNK, span - 1)
    return rolled[:, :, :A_BAND]


def _dot(a, b, dn=NN):
    return lax.dot_general(a, b, dn, precision=DELTA_PREC, preferred_element_type=F32)


def _dot16(a, b, dn=NN):
    return lax.dot_general(a.astype(BF16), b.astype(BF16), dn, preferred_element_type=F32)


def _each(fn, *lists):
    return [fn(*vals) for vals in zip(*lists)]


@jax.custom_vjp
def _saved_inverse(x, inv):
    return inv


def _saved_inverse_fwd(x, inv):
    return inv, inv


def _saved_inverse_bwd(inv, ct):
    return _dot(_dot(inv, ct, TN), inv, NT), jnp.zeros_like(inv)


_saved_inverse.defvjp(_saved_inverse_fwd, _saved_inverse_bwd)


def _delta_chunk(r_state, cq, ck, cv, beta, g, saved_inv=None):
    ii = lax.broadcasted_iota(jnp.int32, (CHUNK, CHUNK), 0)
    jj = lax.broadcasted_iota(jnp.int32, (CHUNK, CHUNK), 1)
    incl, strict, eye = ii >= jj, ii > jj, ii == jj
    q = _each(lambda t: t * lax.rsqrt(jnp.sum(t * t, axis=-1, keepdims=True) + EPS) * (B_DIM ** -0.5), cq)
    k = _each(lambda t: t * lax.rsqrt(jnp.sum(t * t, axis=-1, keepdims=True) + EPS), ck)
    g_b = _each(lambda t: jnp.broadcast_to(t, (CHUNK, CHUNK)), g)
    g_row = _each(lambda t: jnp.sum(jnp.where(eye, t, 0.0), axis=0, keepdims=True), g_b)
    gc_col = _each(lambda t: jnp.sum(jnp.where(incl, t, 0.0), axis=1, keepdims=True), g_row)
    gc_row = _each(lambda t: jnp.sum(jnp.where(ii <= jj, t, 0.0), axis=0, keepdims=True), g_b)
    decay = _each(lambda c, r: jnp.where(incl, jnp.exp(jnp.where(incl, c - r, 0.0)), 0.0), gc_col, gc_row)
    kk = _each(lambda t: _dot(t, t, NT), k)
    x = _each(lambda b, m, d: jnp.where(strict, -(b * m * d), 0.0), beta, kk, decay)
    if saved_inv is None:
        inv = _each(lambda t: jnp.where(eye, 1.0, 0.0) + t, x)
        pw = x
        for _ in range(5):
            pw = _each(lambda t: _dot(t, t), pw)
            inv = _each(lambda t, s: t + _dot(t, s), inv, pw)
    else:
        inv = _each(_saved_inverse, x, saved_inv)
    egc = _each(jnp.exp, gc_col)
    u = _each(lambda t, b, v: _dot16(t, b * v), inv, beta, cv)
    wk = _each(lambda t, b, e, kh: _dot16(t, (b * e) * kh), inv, beta, egc, k)
    pqk = _each(lambda qh, kh, d: _dot16(qh, kh, NT) * d, q, k, decay)
    g_last = _each(lambda c: c[CHUNK - 1:CHUNK, :], gc_col)
    kdec = _each(lambda kh, gl, c: kh * jnp.exp(gl - c), k, g_last, gc_col)
    w = _each(lambda uh, wkh, r: uh - _dot16(wkh, r), u, wk, r_state)
    o = _each(lambda e, qh, r, ph, wh: e * _dot16(qh, r) + _dot16(ph, wh), egc, q, r_state, pqk, w)
    r_new = _each(lambda gl, r, kd, wh: jnp.exp(gl) * r + _dot16(kd, wh, TN), g_last, r_state, kdec, w)
    return o, r_new, inv


DELTA_BLK = 512


def _delta_blocks(n_batch, seq):
    nblk = seq // DELTA_BLK
    cpb = DELTA_BLK // CHUNK

    def rows(width, order):
        return pl.BlockSpec((DELTA_BLK, width), lambda b, i: (b * nblk + order(i), 0))

    def states(order, side):
        return pl.BlockSpec((cpb, B_HEADS, side, side), lambda b, i: (b * nblk + order(i), 0, 0, 0))

    return nblk, cpb, rows, states


def _head_cols(h):
    return [pl.ds(part * B_HEADS * B_DIM + h * B_DIM, B_DIM) for part in range(3)]


def _load_heads(c_ref, bg_ref, state_ref, rows):
    bg_c = bg_ref[rows, :]
    cols = [_head_cols(h) for h in range(B_HEADS)]
    return ([state_ref[h] for h in range(B_HEADS)], [c_ref[rows, c[0]] for c in cols], [c_ref[rows, c[1]] for c in cols],
            [c_ref[rows, c[2]] for c in cols], [bg_c[:, h:h + 1] for h in range(B_HEADS)],
            [bg_c[:, B_HEADS + h:B_HEADS + h + 1] for h in range(B_HEADS)])


def _delta_fwd(conv, bg, n_batch, seq):
    nblk, cpb, rows_spec, states_spec = _delta_blocks(n_batch, seq)

    def forward(i):
        return i

    def body(c_ref, bg_ref, o_ref, st_ref, inv_ref, r_ref):
        @pl.when(pl.program_id(1) == 0)
        def _():
            r_ref[...] = jnp.zeros_like(r_ref)

        def step(c, carry):
            rows = pl.ds(pl.multiple_of(c * CHUNK, CHUNK), CHUNK)
            args = _load_heads(c_ref, bg_ref, r_ref, rows)
            o, r_new, inv = _delta_chunk(*args)
            for h in range(B_HEADS):
                st_ref[c, h] = args[0][h]
                inv_ref[c, h] = inv[h]
                o_ref[rows, pl.ds(h * B_DIM, B_DIM)] = o[h]
            for h in range(B_HEADS):
                r_ref[h] = r_new[h]
            return carry

        lax.fori_loop(0, cpb, step, 0)

    n_tok = n_batch * seq
    return pl.pallas_call(
        body,
        name="delta_fwd",
        grid=(n_batch, nblk),
        in_specs=[rows_spec(B_CONV, forward), rows_spec(LANE, forward)],
        out_specs=[rows_spec(B_HEADS * B_DIM, forward), states_spec(forward, B_DIM), states_spec(forward, CHUNK)],
        out_shape=[jax.ShapeDtypeStruct((n_tok, B_HEADS * B_DIM), F32),
                   jax.ShapeDtypeStruct((n_tok // CHUNK, B_HEADS, B_DIM, B_DIM), F32),
                   jax.ShapeDtypeStruct((n_tok // CHUNK, B_HEADS, CHUNK, CHUNK), F32)],
        scratch_shapes=[pltpu.VMEM((B_HEADS, B_DIM, B_DIM), F32)],
        compiler_params=pltpu.CompilerParams(dimension_semantics=("arbitrary", "arbitrary")),
    )(conv, bg)


def _delta_bwd(conv, bg, states, inverses, do, n_batch, seq):
    nblk, cpb, rows_spec, states_spec = _delta_blocks(n_batch, seq)

    def backward(i):
        return nblk - 1 - i

    def body(c_ref, bg_ref, st_ref, inv_ref, do_ref, dc_ref, dbg_ref, dr_ref):
        @pl.when(pl.program_id(1) == 0)
        def _():
            dr_ref[...] = jnp.zeros_like(dr_ref)

        def step(n, carry):
            c = cpb - 1 - n
            rows = pl.ds(pl.multiple_of(c * CHUNK, CHUNK), CHUNK)
            saved = [inv_ref[c, h] for h in range(B_HEADS)]
            _, vjp = jax.vjp(lambda *args: _delta_chunk(*args, saved_inv=saved)[:2],
                             *_load_heads(c_ref, bg_ref, st_ref.at[c], rows))
            do = [do_ref[rows, pl.ds(h * B_DIM, B_DIM)] for h in range(B_HEADS)]
            dr, dq, dk, dv, dbeta, dg = vjp((do, [dr_ref[h] for h in range(B_HEADS)]))
            lane = lax.broadcasted_iota(jnp.int32, (CHUNK, LANE), 1)
            dbg = jnp.zeros((CHUNK, LANE), F32)
            for h in range(B_HEADS):
                cq, ck, cv = _head_cols(h)
                dr_ref[h] = dr[h]
                dc_ref[rows, cq] = dq[h]
                dc_ref[rows, ck] = dk[h]
                dc_ref[rows, cv] = dv[h]
                dbg = dbg + jnp.where(lane == h, dbeta[h], 0.0) + jnp.where(lane == h + B_HEADS, dg[h], 0.0)
            dbg_ref[rows, :] = dbg
            return carry

        lax.fori_loop(0, cpb, step, 0)

    n_tok = n_batch * seq
    return pl.pallas_call(
        body,
        name="delta_bwd",
        grid=(n_batch, nblk),
        in_specs=[rows_spec(B_CONV, backward), rows_spec(LANE, backward), states_spec(backward, B_DIM),
                  states_spec(backward, CHUNK), rows_spec(B_HEADS * B_DIM, backward)],
        out_specs=[rows_spec(B_CONV, backward), rows_spec(LANE, backward)],
        out_shape=[jax.ShapeDtypeStruct((n_tok, B_CONV), F32), jax.ShapeDtypeStruct((n_tok, LANE), F32)],
        scratch_shapes=[pltpu.VMEM((B_HEADS, B_DIM, B_DIM), F32)],
        compiler_params=pltpu.CompilerParams(dimension_semantics=("arbitrary", "arbitrary")),
    )(conv, bg, states, inverses, do)


def _lane_row(vec4, first):
    return jnp.concatenate([jnp.zeros((1, first), F32), vec4.reshape(1, B_HEADS).astype(F32),
                            jnp.zeros((1, LANE - first - B_HEADS), F32)], axis=1)


def _local_step(x3d, p3d, tgt3d, w_in, small, rest_weights, send_grads, send_w_in):
    n_batch, seq, _ = x3d.shape
    n_tok = n_batch * seq
    x, p, tgt = x3d.reshape(n_tok, D), p3d.reshape(n_tok, -1), tgt3d.reshape(n_tok, D)
    g_mix, g_ffn, g_ple, g_final = (small[k].reshape(1, D) for k in ("g_mix", "g_ffn", "g_ple", "g_final"))
    w_onorm = small["w_onorm"].reshape(1, B_DIM)
    al_row = _lane_row(small["a_log"], B_HEADS)
    dtb_row = _lane_row(small["dt_bias"], B_HEADS)
    rel_bias = small["rel_bias"].reshape(A_HEADS, -1)
    bias = _rel_bias_table(rel_bias)
    conv_w = small["conv_w"].reshape(4, B_CONV)

    h1 = _rms_fwd(x, g_mix, name="rms_mix")
    projp = _mm(h1, w_in, tb=True, out_dtype=BF16, name="mm_proj", tn=1920)
    bd = _mm(h1, w_in[P_BD:], tb=True, name="mm_beta_decay", tn=LANE)
    y_a = _attn_fwd(projp, bias, n_batch, seq)
    conv = _conv_fwd(projp, conv_w, n_batch, seq)
    (bg,) = _rowwise(lambda raw, al, dtb: ([_gate_scalars(raw, al, dtb)], []), [_full(bd)],
                     [_full(al_row), _full(dtb_row)], [(LANE, F32, LANE, 0, 0)], name="gate_scalars", tr=1024)
    o_b, states, inverses = _delta_fwd(conv, bg, n_batch, seq)
    (y_b,) = _rowwise(lambda o, z, wn: ([_gated_norm(o, z, wn)], []), [(o_b, LANE, 0, 1), (projp, LANE, P_Z // LANE, 1)],
                      [_full(w_onorm)], [(B_HEADS * B_DIM, BF16, LANE, 0, 1)], name="gated_norm", tr=1024, ncol=B_HEADS)
    w = rest_weights(y_b)
    t_a = _mm(y_a, w["w_branch_a"], tb=True, out_dtype=BF16, name="mm_branch_a", tn=1024)
    t_b = _mm(y_b, w["w_branch_b"], tb=True, out_dtype=BF16, name="mm_branch_b", tn=1024)
    half = D // 2
    gate_rows = [(projp, half, P_GATE // half, 1), (projp, half, P_GATE // half + 2, 1), (t_a, half, 0, 1), (t_b, half, 0, 1)]
    (merged,) = _rowwise(lambda ga, gb, ta, tb: ([_merge(ga, gb, ta, tb)], []), gate_rows, [], [(D, BF16, half, 0, 1)],
                         name="merge", tr=512, ncol=2)
    x1, h2 = _mm(merged, w["w_out"], name="mm_out", tn=1024, epilogue=_residual_rms, rows=[x], bcs=[g_ffn], outs=(F32, BF16))
    gu = _mm(h2, w["w_gate_up"], tb=True, out_dtype=BF16, name="mm_gate_up", tn=2816)
    (act,) = _rowwise(lambda gub: ([_swiglu(gub)], []), [_full(gu)], [], [(D_FF, BF16, D_FF, 0, 0)], name="swiglu",
                      tr=512)
    x2, h3 = _mm(act, w["w_down"], name="mm_down", tm=512, tn=1024, tk=D_FF, epilogue=_residual_rms, rows=[x1], bcs=[g_ple],
                 outs=(F32, BF16))
    pp = _mm(p, w["w_ple_proj"], tb=True, name="mm_ple_proj", tn=1024)

    def head_fn(pgb, x2b, ppb, tb, gb):
        loss, (dx2, dpg, dpp, dg) = jax.value_and_grad(_head_loss, argnums=(0, 1, 2, 4))(x2b, pgb, ppb, tb, gb)
        return [dx2, dpg, dpp], [dg, jnp.full((1, D), loss, F32)]

    dx3, dpg, dpp, dg_final, loss_row = _mm(h3, w["w_ple_gate"], name="mm_ple_gate_loss", tm=256, tn=1024, epilogue=head_fn,
                                            rows=[x2, pp, tgt], bcs=[g_final], outs=(F32, BF16, BF16), n_red=2)
    gw = {}
    gw["w_ple_proj"] = _mm(dpp, p, ta=True, out_dtype=BF16, name="mm_d_ple_proj", tn=256)
    gw["w_ple_gate"] = _mm(h3, dpg, ta=True, out_dtype=BF16, name="mm_d_ple_gate", tn=512)
    dx2, dg_ple = _mm(dpg, w["w_ple_gate"], tb=True, name="mm_dh3", tm=512, tn=1024, epilogue=_rms_bwd, rows=[x2, dx3],
                      bcs=[g_ple], outs=(F32,), n_red=1)
    gw["w_down"] = _mm(act, dx2, ta=True, out_dtype=BF16, name="mm_d_down", tm=1408, tn=256)
    dact = _mm(dx2, w["w_down"], tb=True, out_dtype=BF16, name="mm_dact", tn=D_FF)

    def swiglu_bwd(gub, dab):
        _, vjp = jax.vjp(_swiglu, gub)
        return [vjp(dab)[0]], []

    (dgu,) = _rowwise(swiglu_bwd, [_full(gu), _full(dact)], [], [(2 * D_FF, BF16, 2 * D_FF, 0, 0)], name="swiglu_bwd", tr=256)
    gw["w_gate_up"] = _mm(dgu, h2, ta=True, out_dtype=BF16, name="mm_d_gate_up", tm=512, tn=1024)
    dx1, dg_ffn = _mm(dgu, w["w_gate_up"], name="mm_dh2", tm=256, tn=1024, tk=2 * D_FF, epilogue=_rms_bwd, rows=[x1, dx2],
                      bcs=[g_ffn], outs=(F32,), n_red=1)
    gw["w_out"] = _mm(merged, dx1, ta=True, out_dtype=BF16, name="mm_d_out", tn=512)
    dmerged = _mm(dx1, w["w_out"], tb=True, name="mm_dmerged", tn=1024)

    def merge_bwd(ga, gb, ta, tb, dm):
        _, vjp = jax.vjp(_merge, ga, gb, ta, tb)
        return list(vjp(dm)), []

    dga, dgb, dta, dtb = _rowwise(merge_bwd, gate_rows + [(dmerged, half, 0, 1)], [], [(D, BF16, half, 0, 1)] * 4,
                                  name="merge_bwd", tr=512, ncol=2)
    gw["w_branch_a"] = _mm(dta, y_a, ta=True, out_dtype=BF16, name="mm_d_branch_a", tn=512)
    gw["w_branch_b"] = _mm(dtb, y_b, ta=True, out_dtype=BF16, name="mm_d_branch_b", tn=512)
    dya = _mm(dta, w["w_branch_a"], name="mm_dya", tn=512)
    dyb = _mm(dtb, w["w_branch_b"], name="mm_dyb", tn=512)

    w_onorm = w_onorm + send_grads(gw)[0, 0]

    def gated_norm_bwd(o, z, dy, wn):
        _, vjp = jax.vjp(_gated_norm, o, z, wn)
        do, dz, dwn = vjp(dy)
        return [do, dz], [dwn]

    do_b, dz, dw_onorm = _rowwise(
        gated_norm_bwd, [(o_b, LANE, 0, 1), (projp, LANE, P_Z // LANE, 1), (dyb, LANE, 0, 1)], [_full(w_onorm)],
        [(B_HEADS * B_DIM, F32, LANE, 0, 1), (B_HEADS * B_DIM, BF16, LANE, 0, 1)], [(B_DIM, B_DIM, 0)],
        name="gated_norm_bwd", tr=1024, ncol=B_HEADS)
    dconv_out, dbg = _delta_bwd(conv, bg, states, inverses, do_b, n_batch, seq)

    def gate_scalars_bwd(raw, dbgb, al, dtb):
        _, vjp = jax.vjp(_gate_scalars, raw, al, dtb)
        draw, dal, ddtb = vjp(dbgb)
        return [draw], [dal, ddtb]

    dbd, dal_row, ddtb_row = _rowwise(gate_scalars_bwd, [_full(bd), _full(dbg)], [_full(al_row), _full(dtb_row)],
                                      [(LANE, BF16, LANE, 0, 0)], [(LANE, LANE, 0), (LANE, LANE, 0)], name="gate_scalars_bwd",
                                      tr=1024)
    dconv, dconv_w = _conv_bwd(projp, conv_w, dconv_out, n_batch, seq)
    dq_a, dk_a, dv_a, dbias = _attn_bwd(projp, bias, dya, n_batch, seq)
    dprojp = jnp.concatenate([dq_a, dk_a, dv_a, dconv, dz, dga, dgb, dbd], axis=1)
    sent = send_w_in(_mm(dprojp, h1, ta=True, out_dtype=BF16, name="mm_d_in", tm=640, tn=1024))
    sent, dprojp = lax.optimization_barrier((sent, dprojp))
    grad_x, dg_mix = _mm(dprojp, w_in, name="mm_dh1", tm=256, tn=1024, tk=P_END, epilogue=_rms_bwd, rows=[x, dx1],
                         bcs=[g_mix + sent[0, 0]], outs=(F32,), n_red=1)

    _, bias_vjp = jax.vjp(_rel_bias_table, rel_bias)
    gs = {
        "g_mix": dg_mix, "g_ffn": dg_ffn, "g_ple": dg_ple, "g_final": dg_final, "w_onorm": dw_onorm,
        "conv_w": dconv_w, "rel_bias": bias_vjp(dbias)[0],
        "a_log": dal_row[0, B_HEADS:2 * B_HEADS], "dt_bias": ddtb_row[0, B_HEADS:2 * B_HEADS],
    }
    return loss_row[:, :1], grad_x.reshape(n_batch, seq, D), gs


MATRICES = (("w_in", 1), ("w_gate_up", 1), ("w_branch_a", 1), ("w_branch_b", 1), ("w_out", 0), ("w_down", 0),
            ("w_ple_gate", 0), ("w_ple_proj", 1))
TAPS_PER_SHARD = B_CONV // N_DEV


def _held(shard, axis):
    return shard if axis == 0 else shard.T


def _from_gathered(slabs):
    return slabs.reshape(-1, slabs.shape[-1])


def _to_owner(held):
    return held.reshape(N_DEV, held.shape[0] // N_DEV, held.shape[1])


def _permute_w_in(held):
    n_gate = P_BD - P_GATE
    row = lax.broadcasted_iota(jnp.int32, (P_END, 1), 0)
    same = jnp.pad(held, ((0, P_END - D_IN), (0, 0)))
    up = jnp.pad(held[8:], ((0, P_END - D_IN + 8), (0, 0)))
    down = jnp.pad(held[:P_GATE + 8], ((n_gate, P_END - P_BD - 8), (0, 0)))
    zero = jnp.zeros((), held.dtype)
    return jnp.where(row < P_GATE, same, jnp.where(row < P_BD, up, jnp.where(row < P_BD + 8, down, zero)))


def _unpermute_w_in(gp):
    n_gate = P_BD - P_GATE
    row = lax.broadcasted_iota(jnp.int32, (D_IN, 1), 0)
    same = gp[:D_IN]
    up = jnp.pad(gp[n_gate:], ((0, D_IN - (P_END - n_gate)), (0, 0)))
    down = jnp.pad(gp[:P_BD], ((8, 0), (0, 0)))
    return jnp.where(row < P_GATE, same, jnp.where(row < P_GATE + 8, up, down))


SMALL_ROWS = 16
SMALL_LAYOUT = (("g_mix", 0, D), ("g_ffn", 1, D), ("g_ple", 2, D), ("g_final", 3, D), ("conv_w", 4, 4 * B_CONV),
                ("rel_bias", 10, A_HEADS * (2 * REL_CLIP + 1)), ("w_onorm", 13, B_DIM), ("a_log", 14, B_HEADS),
                ("dt_bias", 14, B_HEADS), ("loss", 15, 1))


def _pack_small(gs):
    rows = {}
    for name, row, n in SMALL_LAYOUT:
        rows.setdefault(row, []).append(gs[name].reshape(-1).astype(F32))
    parts = []
    for row in sorted(rows):
        flat = jnp.concatenate(rows[row])
        parts.append(jnp.concatenate([flat, jnp.zeros((-flat.shape[0] % D,), F32)]))
    flat = jnp.concatenate(parts)
    assert flat.shape[0] == SMALL_ROWS * D, flat.shape
    return flat.reshape(SMALL_ROWS, D)


def _unpack_small(blk):
    flat, out, used = blk.reshape(-1), {}, {}
    for name, row, n in SMALL_LAYOUT:
        start = row * D + used.get(row, 0)
        out[name] = flat[start:start + n]
        used[row] = used.get(row, 0) + n
    return out


def _position():
    return lax.axis_index("x"), lax.axis_index("y"), lax.axis_index("c")


PEERS = N_DEV - 1


def _comm_call(body, arrays, out_shapes, *, name):
    n = len(arrays)
    return pl.pallas_call(
        body,
        name=name,
        out_shape=out_shapes,
        in_specs=[HBM_SPEC] * n,
        out_specs=[HBM_SPEC] * n,
        scratch_shapes=[pltpu.SemaphoreType.DMA((PEERS * n,)), pltpu.SemaphoreType.DMA((PEERS * n,)),
                        pltpu.SemaphoreType.DMA((n,))],
    )(*arrays)


def _weights_allgather(shards):
    n = len(shards)

    def body(*refs):
        ins, outs = refs[:n], refs[n:2 * n]
        send_sems, recv_sems, local_sems = refs[2 * n:]
        x, y, c = _position()
        me, sibling = (x, y, c), (x, y, 1 - c)
        chips = [(1 - x, y), (x, 1 - y), (1 - x, 1 - y)]

        def slab(a, px, py, pc):
            return outs[a].at[4 * px + 2 * py + pc]

        def copy(a, k, block, to, src=None):
            return pltpu.make_async_remote_copy(src_ref=slab(a, *block) if src is None else src, dst_ref=slab(a, *block),
                                                send_sem=send_sems.at[PEERS * a + k], recv_sem=recv_sems.at[PEERS * a + k],
                                                device_id=to, device_id_type=MESH)

        local = [pltpu.make_async_copy(ins[a], slab(a, *me), local_sems.at[a]) for a in range(n)]
        sent = [copy(a, 1 + j, me, (*chip, c), src=ins[a]) for a in range(n) for j, chip in enumerate(chips)]
        sent += [copy(a, 0, me, sibling, src=ins[a]) for a in range(n)]
        for cp in sent + local:
            cp.start()
        for a in range(n):
            for j, chip in enumerate(chips):
                copy(a, 1 + j, (*chip, c), me).wait_recv()
                passed = copy(a, 4 + j, (*chip, c), sibling)
                passed.start()
                sent.append(passed)
        for a in range(n):
            copy(a, 0, sibling, me).wait_recv()
            for j, chip in enumerate(chips):
                copy(a, 4 + j, (*chip, 1 - c), me).wait_recv()
        for cp in sent:
            cp.wait_send()
        for cp in local:
            cp.wait()

    return _comm_call(body, shards, [jax.ShapeDtypeStruct((N_DEV,) + s.shape, s.dtype) for s in shards],
                      name="weights_allgather")


def _grads_exchange(by_owner):
    n = len(by_owner)

    def body(*refs):
        ins, outs = refs[:n], refs[n:2 * n]
        send_sems, recv_sems, local_sems = refs[2 * n:]
        x, y, c = _position()
        mine = 4 * x + 2 * y + c
        local = [pltpu.make_async_copy(ins[a].at[mine], outs[a].at[mine], local_sems.at[a]) for a in range(n)]
        for cp in local:
            cp.start()
        flips = [(dx, dy, dc) for dx in (0, 1) for dy in (0, 1) for dc in (0, 1) if dx + dy + dc]
        pending = []
        for k, (dx, dy, dc) in enumerate(flips):
            px, py, pc = (1 - x if dx else x), (1 - y if dy else y), (1 - c if dc else c)
            peer = 4 * px + 2 * py + pc
            for a in range(n):
                def remote(slot):
                    return pltpu.make_async_remote_copy(src_ref=ins[a].at[peer], dst_ref=outs[a].at[slot],
                                                        send_sem=send_sems.at[PEERS * a + k], recv_sem=recv_sems.at[PEERS * a + k],
                                                        device_id=(px, py, pc), device_id_type=MESH)

                sent = remote(mine)
                sent.start()
                pending.append((sent, remote(peer)))
        for sent, landed in pending:
            landed.wait_recv()
            sent.wait_send()
        for cp in local:
            cp.wait()

    return _comm_call(body, by_owner, [jax.ShapeDtypeStruct(g.shape, g.dtype) for g in by_owner], name="grads_exchange")


SEM_SPEC = pl.BlockSpec(memory_space=pltpu.SEMAPHORE)
DATAFLOW = pltpu.SideEffectType.DATAFLOW_SIDE_EFFECTING


def _peer_copies(srcs, lands, send_sems, recv_sems, by_owner, arrival):
    x, y, c = _position()
    mine = 4 * x + 2 * y + c
    copies = []
    for k, (dx, dy, dc) in enumerate([(dx, dy, dc) for dx in (0, 1) for dy in (0, 1) for dc in (0, 1) if dx + dy + dc]):
        px, py, pc = (1 - x if dx else x), (1 - y if dy else y), (1 - c if dc else c)
        peer = 4 * px + 2 * py + pc
        for a, (src, land) in enumerate(zip(srcs, lands)):
            copies.append(pltpu.make_async_remote_copy(
                src_ref=src.at[peer] if by_owner else src, dst_ref=land.at[peer if arrival else mine],
                send_sem=send_sems.at[PEERS * a + k], recv_sem=recv_sems.at[PEERS * a + k],
                device_id=(px, py, pc), device_id_type=MESH))
    return copies


def _exchange_start(sources, by_owner, *, name):
    n = len(sources)
    lands = [lax.empty((N_DEV,) + (s.shape[1:] if by_owner else s.shape), s.dtype) for s in sources]

    def body(*refs):
        send_sems, recv_sems, token = refs[2 * n], refs[2 * n + 1], refs[-1]
        for copy in _peer_copies(refs[:n], refs[n:2 * n], send_sems, recv_sems, by_owner, arrival=False):
            copy.start()
        token[...] = jnp.zeros_like(token)

    sems = pltpu.SemaphoreType.DMA((PEERS * n,))
    outs = pl.pallas_call(
        body,
        name=name,
        out_shape=(sems, sems, *[pltpu.HBM(a.shape, a.dtype) for a in sources + lands], jax.ShapeDtypeStruct((8, LANE), F32)),
        in_specs=[HBM_SPEC] * (2 * n),
        out_specs=(SEM_SPEC, SEM_SPEC, *[HBM_SPEC] * (2 * n), pl.BlockSpec(memory_space=pltpu.VMEM)),
        input_output_aliases={i: 2 + i for i in range(2 * n)},
        compiler_params=pltpu.CompilerParams(has_side_effects=DATAFLOW),
    )(*[pltpu.with_memory_space_constraint(a, pltpu.HBM) for a in sources + lands])
    return outs[:-1], outs[-1]


def _exchange_wait(started, after, by_owner, *, name):
    send_sems, recv_sems, *arrays = started
    n = len(arrays) // 2

    def body(*refs):
        for copy in _peer_copies(refs[:n], refs[n:2 * n], refs[2 * n], refs[2 * n + 1], by_owner, arrival=True):
            copy.wait_send()
            copy.wait_recv()

    outs = pl.pallas_call(
        body,
        name=name,
        out_shape=[pltpu.HBM(a.shape, a.dtype) for a in arrays],
        in_specs=[HBM_SPEC] * (2 * n) + [SEM_SPEC, SEM_SPEC, pl.BlockSpec(memory_space=pl.ANY)],
        out_specs=[HBM_SPEC] * (2 * n),
        input_output_aliases={i: i for i in range(2 * n)},
        compiler_params=pltpu.CompilerParams(has_side_effects=DATAFLOW),
    )(*arrays, send_sems, recv_sems, after)
    return outs[:n], outs[n:]


def _slot_sum(g_ref, own_ref):
    if own_ref is not None:
        x, y, c = _position()
        mine = 4 * x + 2 * y + c
    acc = None
    for j in range(N_DEV):
        part = g_ref[j] if own_ref is None else jnp.where(mine == j, own_ref[...], g_ref[j])
        acc = part.astype(F32) if acc is None else acc + part.astype(F32)
    return acc


def _sum_slots(got, *, name, tr):
    _, rows, cols = got.shape
    tr = _tile(rows, tr, 16)

    def body(g_ref, o_ref):
        o_ref[...] = _slot_sum(g_ref, None)

    return pl.pallas_call(
        body,
        name=name,
        grid=(rows // tr,),
        in_specs=[pl.BlockSpec((N_DEV, tr, cols), lambda i: (0, i, 0))],
        out_specs=pl.BlockSpec((tr, cols), lambda i: (i, 0)),
        out_shape=jax.ShapeDtypeStruct((rows, cols), F32),
        compiler_params=pltpu.CompilerParams(dimension_semantics=("parallel",)),
    )(got)


def _adamw(wt, g, m, v, *, name, own=None):
    slots = own is not None
    shape = wt.shape
    two_d = (-1, shape[-1]) if wt.ndim > 1 else (1, -1)
    args = [a.reshape(two_d) for a in (wt, m, v)]
    rows, cols = args[0].shape
    if rows % 16 == 0:
        tr, tc = _tile(rows, 256, 16), cols
    else:
        tr, tc = rows, _tile(cols, 256 if rows > 64 else 512)
    args.insert(1, g.reshape((N_DEV, rows, cols) if slots else (rows, cols)))
    if slots:
        args.append(own.reshape(N_DEV, rows, cols))

    def body(w_ref, g_ref, m_ref, v_ref, *refs):
        go_ref, d_ref, nm_ref, nv_ref = refs[-4:]
        gv = _slot_sum(g_ref, refs[0]) if slots else g_ref[...]
        go_ref[...] = gv
        m2 = ADAM_B1 * m_ref[...] + (1.0 - ADAM_B1) * gv
        v2 = ADAM_B2 * v_ref[...] + (1.0 - ADAM_B2) * (gv * gv)
        m_hat = m2 / (1.0 - ADAM_B1 ** ADAM_STEP)
        v_hat = v2 / (1.0 - ADAM_B2 ** ADAM_STEP)
        d_ref[...] = -ADAM_LR * (m_hat / (jnp.sqrt(v_hat) + ADAM_EPS) + ADAM_WD * w_ref[...])
        nm_ref[...] = m2
        nv_ref[...] = v2

    def own_block(i, j):
        x, y, c = _position()
        return 4 * x + 2 * y + c, i, j

    spec = pl.BlockSpec((tr, tc), lambda i, j: (i, j))
    g_spec = pl.BlockSpec((N_DEV, tr, tc), lambda i, j: (0, i, j)) if slots else spec
    own_spec = pl.BlockSpec((None, tr, tc), own_block)
    outs = pl.pallas_call(
        body,
        name=name,
        grid=(rows // tr, cols // tc),
        in_specs=[spec, g_spec, spec, spec] + ([own_spec] if slots else []),
        out_specs=[spec] * 4,
        out_shape=[jax.ShapeDtypeStruct((rows, cols), F32)] * 4,
        compiler_params=pltpu.CompilerParams(dimension_semantics=("parallel", "parallel")),
    )(*args)
    return tuple(o.reshape(shape) for o in outs)


WEIGHTS = ("g_mix", "w_in", "conv_w", "a_log", "dt_bias", "rel_bias", "w_onorm", "w_branch_a", "w_branch_b", "w_out", "g_ffn",
           "w_gate_up", "w_down", "g_ple", "w_ple_gate", "w_ple_proj", "g_final")


def kernel(x, p, g_mix, w_in, conv_w, a_log, dt_bias, rel_bias, w_onorm, w_branch_a, w_branch_b, w_out, g_ffn, w_gate_up, w_down, g_ple, w_ple_gate, w_ple_proj, g_final, loss_target, m_g_mix, m_w_in, m_conv_w, m_a_log, m_dt_bias, m_rel_bias, m_w_onorm, m_w_branch_a, m_w_branch_b, m_w_out, m_g_ffn, m_w_gate_up, m_w_down, m_g_ple, m_w_ple_gate, m_w_ple_proj, m_g_final, v_g_mix, v_w_in, v_conv_w, v_a_log, v_dt_bias, v_rel_bias, v_w_onorm, v_w_branch_a, v_w_branch_b, v_w_out, v_g_ffn, v_w_gate_up, v_w_down, v_g_ple, v_w_ple_gate, v_w_ple_proj, v_g_final):
    given = dict(g_mix=g_mix, w_in=w_in, conv_w=conv_w, a_log=a_log, dt_bias=dt_bias, rel_bias=rel_bias, w_onorm=w_onorm,
                 w_branch_a=w_branch_a, w_branch_b=w_branch_b, w_out=w_out, g_ffn=g_ffn, w_gate_up=w_gate_up, w_down=w_down,
                 g_ple=g_ple, w_ple_gate=w_ple_gate, w_ple_proj=w_ple_proj, g_final=g_final)
    mom1 = dict(g_mix=m_g_mix, w_in=m_w_in, conv_w=m_conv_w, a_log=m_a_log, dt_bias=m_dt_bias, rel_bias=m_rel_bias,
                w_onorm=m_w_onorm, w_branch_a=m_w_branch_a, w_branch_b=m_w_branch_b, w_out=m_w_out, g_ffn=m_g_ffn,
                w_gate_up=m_w_gate_up, w_down=m_w_down, g_ple=m_g_ple, w_ple_gate=m_w_ple_gate, w_ple_proj=m_w_ple_proj,
                g_final=m_g_final)
    mom2 = dict(g_mix=v_g_mix, w_in=v_w_in, conv_w=v_conv_w, a_log=v_a_log, dt_bias=v_dt_bias, rel_bias=v_rel_bias,
                w_onorm=v_w_onorm, w_branch_a=v_w_branch_a, w_branch_b=v_w_branch_b, w_out=v_w_out, g_ffn=v_g_ffn,
                w_gate_up=v_w_gate_up, w_down=v_w_down, g_ple=v_g_ple, w_ple_gate=v_w_ple_gate, w_ple_proj=v_w_ple_proj,
                g_final=v_g_final)
    mine = 4 * lax.axis_index("x") + 2 * lax.axis_index("y") + lax.axis_index("c")

    my_slot = (jnp.arange(N_DEV) == mine)[:, None, None]
    rest = MATRICES[1:]
    in_flight = {}

    got_in, got_taps = _weights_allgather([_held(w_in[0], 1).astype(BF16), conv_w[0]])
    in_flight["weights"], weights_sent = _exchange_start([_held(given[name][0], axis).astype(BF16) for name, axis in rest], False,
                                                         name="weights_start")
    small = dict(g_mix=g_mix + weights_sent[0, 0], g_ffn=g_ffn, g_ple=g_ple, g_final=g_final, w_onorm=w_onorm, a_log=a_log,
                 dt_bias=dt_bias, rel_bias=rel_bias, conv_w=jnp.transpose(got_taps, (1, 0, 2)).reshape(4, B_CONV))

    def rest_weights(after):
        shards, landed = _exchange_wait(in_flight.pop("weights"), after, False, name="weights_wait")
        return {name: _from_gathered(jnp.where(my_slot, shard[None], slabs)) for (name, _), shard, slabs in zip(rest, shards, landed)}

    def send_grads(gw):
        in_flight["grads"], sent = _exchange_start([_to_owner(gw[name]) for name, _ in rest], True, name="grads_start")
        return sent

    def send_w_in(g_in):
        in_flight["grad_in"], sent = _exchange_start([_to_owner(_unpermute_w_in(g_in))], True, name="grad_in_start")
        return sent

    loss_part, grad_x, gs = _local_step(x, p[0], loss_target, _permute_w_in(_from_gathered(got_in)), small, rest_weights,
                                        send_grads, send_w_in)
    gs["loss"] = loss_part

    updates = {}

    def update_matrices(matrices, own_slabs, landed):
        for (name, axis), own_slab, slots in zip(matrices, own_slabs, landed):
            w_held, m_held, v_held = (_held(a[name][0], axis) for a in (given, mom1, mom2))
            outs = _adamw(w_held, slots, m_held, v_held, name=f"adamw_{name}", own=own_slab)
            updates[name] = tuple(_held(o, axis)[None] for o in outs)

    update_matrices(rest, *_exchange_wait(in_flight["grads"], grad_x, True, name="grads_wait"))
    update_matrices(MATRICES[:1], *_exchange_wait(in_flight["grad_in"], updates[rest[-1][0]][0], True, name="grad_in_wait"))
    small_block, _ = lax.optimization_barrier((_pack_small(gs), updates["w_in"][0]))
    (got_small,) = _grads_exchange([jnp.broadcast_to(small_block, (N_DEV, SMALL_ROWS, D))])
    small_sum = _unpack_small(_sum_slots(got_small, name="sum_small_grads", tr=16))
    loss = small_sum.pop("loss")[0]
    conv_all = small_sum.pop("conv_w").reshape(4, N_DEV, TAPS_PER_SHARD)
    small_sum["conv_w"] = lax.dynamic_index_in_dim(conv_all, mine, axis=1, keepdims=False)
    for name, g in small_sum.items():
        updates[name] = _adamw(given[name], g.reshape(given[name].shape), mom1[name], mom2[name], name=f"adamw_{name}")
    return (loss, grad_x, *[updates[name][k] for k in range(4) for name in WEIGHTS])
```

```python
import jax
import jax.numpy as jnp
from jax import lax
from jax.experimental import pallas as pl
from jax.experimental.pallas import tpu as pltpu

F32 = jnp.float32
BF16 = jnp.bfloat16
DELTA_PREC = lax.Precision.HIGH
MESH = pl.DeviceIdType.MESH

N_DEV = 8
D = 1024
CHUNK = 64
EPS = 1e-6
A_HEADS, A_DIM, A_WIDTH = 8, 64, 512
A_BAND = 9 * CHUNK
A_PAD = 8 * CHUNK
REL_CLIP = 128
B_HEADS, B_DIM = 4, 128
B_CONV = 1536
D_FF = 2816
D_IN = 5640
P_CONV, P_Z, P_GATE, P_BD, P_END = 1536, 3072, 3584, 5632, 5760
LANE = 128

ADAM_LR, ADAM_B1, ADAM_B2, ADAM_EPS, ADAM_WD, ADAM_STEP = 0.001, 0.9, 0.999, 1e-08, 0.01, 10

NT = (((1,), (1,)), ((), ()))
TN = (((0,), (0,)), ((), ()))
NN = (((1,), (0,)), ((), ()))

HBM_SPEC = pl.BlockSpec(memory_space=pltpu.HBM)


def _tile(n, target, align=LANE):
    if n <= target:
        return n
    best = None
    for t in range(align, target + 1, align):
        if n % t == 0:
            best = t
    assert best is not None, (n, target, align)
    return best


def _mm(a, b, *, name, ta=False, tb=False, out_dtype=F32, tm=1024, tn=640, tk=None, epilogue=None, rows=(), bcs=(), outs=(),
        n_red=0):
    assert not (ta and tb)
    if ta:
        k_dim, m_dim = a.shape
    else:
        m_dim, k_dim = a.shape
    n_dim = b.shape[0] if tb else b.shape[1]
    assert b.shape[1 if tb else 0] == k_dim
    tm, tn = _tile(m_dim, tm), _tile(n_dim, tn)
    tk = _tile(k_dim, tk or (4096 if ta else 1024), 8 if ta else LANE)
    nk = k_dim // tk
    dn = TN if ta else NT if tb else NN
    if epilogue is None:
        outs = (out_dtype,)
    assert not n_red or tn == n_dim
    widths = [o[1] if isinstance(o, tuple) else None for o in outs]
    outs = [o[0] if isinstance(o, tuple) else o for o in outs]
    assert all(w is None for w in widths) or tn == n_dim
    n_extra, n_out = len(rows) + len(bcs), len(outs)

    def body(a_ref, b_ref, *refs):
        part = lax.dot_general(a_ref[...].astype(BF16), b_ref[...].astype(BF16), dn, preferred_element_type=F32)

        def finish(r):
            o_vals, r_vals = ([r], []) if epilogue is None else epilogue(r, *[x[...] for x in refs[:n_extra]])
            for ref, val in zip(refs[n_extra:n_extra + n_out], o_vals):
                ref[...] = val.astype(ref.dtype)
            first = pl.program_id(0) == 0
            for ref, val in zip(refs[n_extra + n_out:n_extra + n_out + n_red], r_vals):
                @pl.when(first)
                def _():
                    ref[...] = val

                @pl.when(jnp.logical_not(first))
                def _():
                    ref[...] += val

        if nk == 1:
            finish(part)
        else:
            acc_ref = refs[-1]
            k = pl.program_id(2)

            @pl.when(k == 0)
            def _():
                acc_ref[...] = part

            @pl.when(k > 0)
            def _():
                acc_ref[...] += part

            @pl.when(k == nk - 1)
            def _():
                finish(acc_ref[...])

    tile = pl.BlockSpec((tm, tn), lambda i, j, k: (i, j))
    col = pl.BlockSpec((1, tn), lambda i, j, k: (0, j))
    a_spec = pl.BlockSpec((tk, tm), lambda i, j, k: (k, i)) if ta else pl.BlockSpec((tm, tk), lambda i, j, k: (i, k))
    b_spec = pl.BlockSpec((tn, tk), lambda i, j, k: (j, k)) if tb else pl.BlockSpec((tk, tn), lambda i, j, k: (k, j))
    result = pl.pallas_call(
        body,
        name=name,
        grid=(m_dim // tm, n_dim // tn, nk),
        in_specs=[a_spec, b_spec] + [tile] * len(rows) + [col] * len(bcs),
        out_specs=[tile if w is None else pl.BlockSpec((tm, w), lambda i, j, k: (i, 0)) for w in widths] + [col] * n_red,
        out_shape=[jax.ShapeDtypeStruct((m_dim, w or n_dim), dt) for dt, w in zip(outs, widths)]
        + [jax.ShapeDtypeStruct((1, n_dim), F32)] * n_red,
        scratch_shapes=[pltpu.VMEM((tm, tn), F32)] if nk > 1 else [],
        compiler_params=pltpu.CompilerParams(dimension_semantics=("arbitrary",) * 3 if n_red else ("parallel", "parallel", "arbitrary")),
    )(a, b, *rows, *bcs)
    return result[0] if epilogue is None else result


def _rowwise(fn, rows, bcs, outs, reds=(), *, name, tr, ncol=1):
    n_rows = rows[0][0].shape[0]
    tr = _tile(n_rows, tr, 8)
    nrow = n_rows // tr
    n_in, n_out = len(rows) + len(bcs), len(outs)

    def body(*refs):
        j, i = pl.program_id(0), pl.program_id(1)
        o_vals, r_vals = fn(*[r[...].astype(F32) for r in refs[:n_in]])
        for ref, val in zip(refs[n_in:n_in + n_out], o_vals):
            ref[...] = val.astype(ref.dtype)
        for ref, val, (_, _, stride) in zip(refs[n_in + n_out:], r_vals, reds):
            first = (i == 0) if stride else jnp.logical_and(i == 0, j == 0)

            @pl.when(first)
            def _():
                ref[...] = val

            @pl.when(jnp.logical_not(first))
            def _():
                ref[...] += val

    def spec(r, w, off, st, row_dep=True):
        if row_dep:
            return pl.BlockSpec((r, w), lambda j, i: (i, off + st * j))
        return pl.BlockSpec((r, w), lambda j, i: (0, off + st * j))

    in_specs = [spec(tr, w, off, st) for (_, w, off, st) in rows]
    in_specs += [spec(a.shape[0], w, off, st, False) for (a, w, off, st) in bcs]
    out_specs = [spec(tr, w, off, st) for (_, _, w, off, st) in outs]
    out_specs += [spec(1, w, 0, st, False) for (_, w, st) in reds]
    out_shape = [jax.ShapeDtypeStruct((n_rows, c), dt) for (c, dt, _, _, _) in outs]
    out_shape += [jax.ShapeDtypeStruct((1, c), F32) for (c, _, _) in reds]
    return pl.pallas_call(
        body,
        name=name,
        grid=(ncol, nrow),
        in_specs=in_specs,
        out_specs=out_specs,
        out_shape=out_shape,
        compiler_params=pltpu.CompilerParams(dimension_semantics=("arbitrary", "arbitrary")),
    )(*[r[0] for r in rows], *[b[0] for b in bcs])


def _full(a):
    return (a, a.shape[1], 0, 0)


def _rms(x, g):
    return x * lax.rsqrt(jnp.mean(x * x, axis=-1, keepdims=True) + EPS) * g


def _silu(x):
    return x * jax.nn.sigmoid(x)


def _softplus(x):
    return jnp.maximum(x, 0.0) + jnp.log(1.0 + jnp.exp(-jnp.abs(x)))


def _rms_fwd(x, g, *, name):
    (h,) = _rowwise(lambda xb, gb: ([_rms(xb, gb)], []), [_full(x)], [_full(g)], [(D, BF16, D, 0, 0)], name=name, tr=512)
    return h


def _residual_rms(r, x, g):
    x_new = x + r
    return [x_new, _rms(x_new, g)], []


def _rms_bwd(dh, x, dres, g):
    _, vjp = jax.vjp(_rms, x, g)
    dx, dg = vjp(dh)
    return [dx + dres], [dg]


def _gate_scalars(raw, al_row, dtb_row):
    lane = lax.broadcasted_iota(jnp.int32, raw.shape, 1)
    beta = jax.nn.sigmoid(raw)
    g = -jnp.exp(al_row) * _softplus(raw + dtb_row)
    return jnp.where(lane < B_HEADS, beta, jnp.where(lane < 2 * B_HEADS, g, 0.0))


def _gated_norm(o, z, w):
    return _rms(o, w) * _silu(z)


def _merge(ga, gb, ta, tb):
    return jax.nn.sigmoid(ga) * ta + jax.nn.sigmoid(gb) * tb


def _swiglu(gu):
    return _silu(gu[:, :D_FF]) * gu[:, D_FF:]


def _head_loss(x2, pg, pp, tgt, g):
    x3 = x2 + jax.nn.sigmoid(pg) * pp
    err = _rms(x3, g) - tgt
    return 0.5 * jnp.sum(jnp.mean(err * err, axis=-1))


CONV_W = 256


def _conv_taps(x, w):
    row = lax.broadcasted_iota(jnp.int32, x.shape, 0)
    shifted = [x] + [jnp.where(row >= s, pltpu.roll(x, s, 0), 0.0) for s in (1, 2, 3)]
    pre = shifted[0] * w[3:4]
    for s in (1, 2, 3):
        pre = pre + shifted[s] * w[3 - s:4 - s]
    return pre, shifted


def _conv_fwd(projp, conv_w, n_batch, seq):
    ncol = B_CONV // CONV_W
    first = P_CONV // CONV_W

    def body(x_ref, w_ref, o_ref):
        pre, _ = _conv_taps(x_ref[...].astype(F32), w_ref[...])
        o_ref[...] = _silu(pre)

    return pl.pallas_call(
        body,
        name="conv_fwd",
        grid=(ncol, n_batch),
        in_specs=[pl.BlockSpec((seq, CONV_W), lambda j, b: (b, first + j)), pl.BlockSpec((4, CONV_W), lambda j, b: (0, j))],
        out_specs=pl.BlockSpec((seq, CONV_W), lambda j, b: (b, j)),
        out_shape=jax.ShapeDtypeStruct((n_batch * seq, B_CONV), F32),
        compiler_params=pltpu.CompilerParams(dimension_semantics=("parallel", "parallel")),
    )(projp, conv_w)


def _conv_bwd(projp, conv_w, dc, n_batch, seq):
    width = dc.shape[1]
    ncol = width // CONV_W
    first_x = P_CONV // CONV_W

    def body(x_ref, w_ref, dc_ref, dx_ref, dw_ref):
        b = pl.program_id(1)
        w = w_ref[...]
        pre, shifted = _conv_taps(x_ref[...].astype(F32), w)
        sg = jax.nn.sigmoid(pre)
        dpre = dc_ref[...] * (sg * (1.0 + pre * (1.0 - sg)))
        row = lax.broadcasted_iota(jnp.int32, dpre.shape, 0)
        dx = dpre * w[3:4]
        for s in (1, 2, 3):
            dx = dx + jnp.where(row < seq - s, pltpu.roll(dpre, seq - s, 0), 0.0) * w[3 - s:4 - s]
        dx_ref[...] = dx.astype(dx_ref.dtype)
        for s in (0, 1, 2, 3):
            part = jnp.sum(dpre * shifted[s], axis=0, keepdims=True)

            @pl.when(b == 0)
            def _():
                dw_ref[3 - s:4 - s, :] = part

            @pl.when(b > 0)
            def _():
                dw_ref[3 - s:4 - s, :] += part

    return pl.pallas_call(
        body,
        name="conv_bwd",
        grid=(ncol, n_batch),
        in_specs=[
            pl.BlockSpec((seq, CONV_W), lambda j, b: (b, first_x + j)),
            pl.BlockSpec((4, CONV_W), lambda j, b: (0, j)),
            pl.BlockSpec((seq, CONV_W), lambda j, b: (b, j)),
        ],
        out_specs=[pl.BlockSpec((seq, CONV_W), lambda j, b: (b, j)), pl.BlockSpec((4, CONV_W), lambda j, b: (0, j))],
        out_shape=[jax.ShapeDtypeStruct((n_batch * seq, width), BF16), jax.ShapeDtypeStruct((4, width), F32)],
        compiler_params=pltpu.CompilerParams(dimension_semantics=("arbitrary", "arbitrary")),
    )(projp, conv_w, dc)


@jax.custom_vjp
def _attend(s, v):
    return _attend_fwd(s, v)[0]


def _attend_fwd(s, v):
    v16 = [t.astype(BF16) for t in v]
    p = [jnp.exp(t - jnp.max(t, axis=-1, keepdims=True)) for t in s]
    p = [t * (1.0 / jnp.sum(t, axis=-1, keepdims=True)) for t in p]
    o = [jnp.dot(t.astype(BF16), v16[n // 2], preferred_element_type=F32) for n, t in enumerate(p)]
    return o, (p, v16, o)


def _attend_bwd(saved, do):
    p, v16, o = saved
    do16 = [t.astype(BF16) for t in do]
    dv = [lax.dot_general(t.astype(BF16), do16[n], TN, preferred_element_type=F32) for n, t in enumerate(p)]
    dp = [lax.dot_general(t, v16[n // 2], NT, preferred_element_type=F32) for n, t in enumerate(do16)]
    delta = [jnp.sum(a * b, axis=-1, keepdims=True) for a, b in zip(do, o)]
    ds = [a * (b - c) for a, b, c in zip(p, dp, delta)]
    return ds, [dv[2 * i] + dv[2 * i + 1] for i in range(len(v16))]


_attend.defvjp(_attend_fwd, _attend_bwd)


def _attn_chunk(qc, kb, vb, bias2, valid, lane_lo):
    sel = (lane_lo, jnp.logical_not(lane_lo))
    items = [(i, e) for i in range(len(qc)) for e in (0, 1)]
    k16 = [t.astype(BF16) for t in kb]
    qm = [(jnp.where(sel[e], qc[i], 0.0) * (A_DIM ** -0.5)).astype(BF16) for i, e in items]
    s = [lax.dot_general(qm[n], k16[i], NT, preferred_element_type=F32) + bias2[e] for n, (i, e) in enumerate(items)]
    if valid is not None:
        s = [jnp.where(valid[i], s[n], -1e30) for n, (i, e) in enumerate(items)]
    o = [jnp.where(sel[e], t, 0.0) for t, (i, e) in zip(_attend(s, vb), items)]
    return [o[2 * i] + o[2 * i + 1] for i in range(len(qc))]


ATTN_GROUP_FWD, ATTN_GROUP_BWD = 8, 8


def _attn_loops(step, n_groups, n_masked):
    lax.fori_loop(0, n_masked, lambda g, c: step(g, c, True), 0)
    lax.fori_loop(n_masked, n_groups, lambda g, c: step(g, c, False), 0)


def _attn_group(g, group, q_ref, kp_ref, vp_ref):
    col = lax.broadcasted_iota(jnp.int32, (CHUNK, A_BAND), 1)
    lane_lo = lax.broadcasted_iota(jnp.int32, (1, LANE), 1) < A_DIM
    starts = [pl.multiple_of((g * group + i) * CHUNK, CHUNK) for i in range(group)]
    rows = [pl.ds(r0, CHUNK) for r0 in starts]
    bands = [pl.ds(r0, A_BAND) for r0 in starts]
    valid = [col + r0 >= A_PAD for r0 in starts]
    loaded = [q_ref[r, :].astype(F32) for r in rows], [kp_ref[b, :] for b in bands], [vp_ref[b, :] for b in bands]
    return rows, bands, loaded, valid, lane_lo


def _attn_specs(seq):
    def blk(first):
        return pl.BlockSpec((seq, LANE), lambda hp, b: (b, first + hp))

    return blk, pl.BlockSpec((2, CHUNK, A_BAND), lambda hp, b: (hp, 0, 0))


def _attn_fwd(projp, bias, n_batch, seq):
    nc = seq // CHUNK
    blk, bias_spec = _attn_specs(seq)

    def body(q_ref, k_ref, v_ref, b_ref, o_ref, kp_ref, vp_ref):
        kp_ref[0:A_PAD, :] = jnp.zeros((A_PAD, LANE), F32)
        vp_ref[0:A_PAD, :] = jnp.zeros((A_PAD, LANE), F32)
        kp_ref[A_PAD:, :] = k_ref[...].astype(F32)
        vp_ref[A_PAD:, :] = v_ref[...].astype(F32)
        bias2 = b_ref[...]

        def step(g, carry, masked):
            rows, _, (qc, kb, vb), valid, lane_lo = _attn_group(g, ATTN_GROUP_FWD, q_ref, kp_ref, vp_ref)
            out = _attn_chunk(qc, kb, vb, bias2, valid if masked else None, lane_lo)
            for r, o in zip(rows, out):
                o_ref[r, :] = o.astype(o_ref.dtype)
            return carry

        _attn_loops(step, nc // ATTN_GROUP_FWD, A_PAD // (CHUNK * ATTN_GROUP_FWD))

    return pl.pallas_call(
        body,
        name="attn_fwd",
        grid=(A_HEADS // 2, n_batch),
        in_specs=[blk(0), blk(4), blk(8), bias_spec],
        out_specs=pl.BlockSpec((seq, LANE), lambda hp, b: (b, hp)),
        out_shape=jax.ShapeDtypeStruct((n_batch * seq, A_WIDTH), BF16),
        scratch_shapes=[pltpu.VMEM((A_PAD + seq, LANE), F32), pltpu.VMEM((A_PAD + seq, LANE), F32)],
        compiler_params=pltpu.CompilerParams(dimension_semantics=("parallel", "parallel")),
    )(projp, projp, projp, bias)


def _attn_bwd(projp, bias, dy, n_batch, seq):
    nc = seq // CHUNK
    blk, bias_spec = _attn_specs(seq)
    out_blk = pl.BlockSpec((seq, LANE), lambda hp, b: (b, hp))

    def body(q_ref, k_ref, v_ref, b_ref, dy_ref, dq_ref, dk_ref, dv_ref, db_ref, kp_ref, vp_ref, dkp_ref, dvp_ref):
        b = pl.program_id(1)
        kp_ref[0:A_PAD, :] = jnp.zeros((A_PAD, LANE), F32)
        vp_ref[0:A_PAD, :] = jnp.zeros((A_PAD, LANE), F32)
        kp_ref[A_PAD:, :] = k_ref[...].astype(F32)
        vp_ref[A_PAD:, :] = v_ref[...].astype(F32)
        dkp_ref[...] = jnp.zeros_like(dkp_ref)
        dvp_ref[...] = jnp.zeros_like(dvp_ref)
        bias2 = b_ref[...]

        @pl.when(b == 0)
        def _():
            db_ref[...] = jnp.zeros_like(db_ref)

        def step(g, carry, masked):
            rows, bands, (qc, kb, vb), valid, lane_lo = _attn_group(g, ATTN_GROUP_BWD, q_ref, kp_ref, vp_ref)
            _, vjp = jax.vjp(lambda q, k, v, bb: _attn_chunk(q, k, v, bb, valid if masked else None, lane_lo), qc, kb, vb, bias2)
            dq, dk, dv, dbias = vjp([dy_ref[r, :] for r in rows])
            for i, r in enumerate(rows):
                dq_ref[r, :] = dq[i].astype(dq_ref.dtype)
            for i, band in enumerate(bands):
                dkp_ref[band, :] += dk[i]
                dvp_ref[band, :] += dv[i]
            db_ref[...] += dbias
            return carry

        _attn_loops(step, nc // ATTN_GROUP_BWD, A_PAD // (CHUNK * ATTN_GROUP_BWD))
        dk_ref[...] = dkp_ref[A_PAD:, :].astype(dk_ref.dtype)
        dv_ref[...] = dvp_ref[A_PAD:, :].astype(dv_ref.dtype)

    n_tok = n_batch * seq
    pad = pltpu.VMEM((A_PAD + seq, LANE), F32)
    return pl.pallas_call(
        body,
        name="attn_bwd",
        grid=(A_HEADS // 2, n_batch),
        in_specs=[blk(0), blk(4), blk(8), bias_spec, out_blk],
        out_specs=[out_blk, out_blk, out_blk, bias_spec],
        out_shape=[jax.ShapeDtypeStruct((n_tok, A_WIDTH), BF16)] * 3 + [jax.ShapeDtypeStruct((A_HEADS, CHUNK, A_BAND), F32)],
        scratch_shapes=[pad, pad, pad, pad],
        compiler_params=pltpu.CompilerParams(dimension_semantics=("arbitrary", "arbitrary")),
    )(projp, projp, projp, bias, dy)


def _rel_bias_table(rel_bias):
    span = CHUNK + A_BAND - 1
    near = REL_CLIP + CHUNK
    far = jnp.broadcast_to(rel_bias[:, 2 * REL_CLIP:], (A_HEADS, span - near))
    t = jnp.concatenate([rel_bias[:, 2 * REL_CLIP + 1 - near:], far], axis=1)
    u = jnp.concatenate([t[:, :A_BAND][:, ::-1], t[:, A_BAND:][:, ::-1]], axis=1)
    rolled = jnp.tile(u, (1, CHUNK))[:, :CHUNK * (span - 1)].reshape(A_HEADS, CHUNK, span - 1)
    return rolled[:, :, :A_BAND]


def _dot(a, b, dn=NN):
    return lax.dot_general(a, b, dn, precision=DELTA_PREC, preferred_element_type=F32)


def _dot16(a, b, dn=NN):
    return lax.dot_general(a.astype(BF16), b.astype(BF16), dn, preferred_element_type=F32)


def _each(fn, *lists):
    return [fn(*vals) for vals in zip(*lists)]


@jax.custom_vjp
def _saved_inverse(x, inv):
    return inv


def _saved_inverse_fwd(x, inv):
    return inv, inv


def _saved_inverse_bwd(inv, ct):
    return _dot(_dot(inv, ct, TN), inv, NT), jnp.zeros_like(inv)


_saved_inverse.defvjp(_saved_inverse_fwd, _saved_inverse_bwd)


def _delta_chunk(r_state, cq, ck, cv, beta, g, saved_inv=None):
    ii = lax.broadcasted_iota(jnp.int32, (CHUNK, CHUNK), 0)
    jj = lax.broadcasted_iota(jnp.int32, (CHUNK, CHUNK), 1)
    incl, strict, eye = ii >= jj, ii > jj, ii == jj
    q = _each(lambda t: t * lax.rsqrt(jnp.sum(t * t, axis=-1, keepdims=True) + EPS) * (B_DIM ** -0.5), cq)
    k = _each(lambda t: t * lax.rsqrt(jnp.sum(t * t, axis=-1, keepdims=True) + EPS), ck)
    g_b = _each(lambda t: jnp.broadcast_to(t, (CHUNK, CHUNK)), g)
    g_row = _each(lambda t: jnp.sum(jnp.where(eye, t, 0.0), axis=0, keepdims=True), g_b)
    gc_col = _each(lambda t: jnp.sum(jnp.where(incl, t, 0.0), axis=1, keepdims=True), g_row)
    gc_row = _each(lambda t: jnp.sum(jnp.where(ii <= jj, t, 0.0), axis=0, keepdims=True), g_b)
    decay = _each(lambda c, r: jnp.where(incl, jnp.exp(jnp.where(incl, c - r, 0.0)), 0.0), gc_col, gc_row)
    kk = _each(lambda t: _dot(t, t, NT), k)
    x = _each(lambda b, m, d: jnp.where(strict, -(b * m * d), 0.0), beta, kk, decay)
    if saved_inv is None:
        inv = _each(lambda t: jnp.where(eye, 1.0, 0.0) + t, x)
        pw = x
        for _ in range(5):
            pw = _each(lambda t: _dot(t, t), pw)
            inv = _each(lambda t, s: t + _dot(t, s), inv, pw)
    else:
        inv = _each(_saved_inverse, x, saved_inv)
    egc = _each(jnp.exp, gc_col)
    u = _each(lambda t, b, v: _dot16(t, b * v), inv, beta, cv)
    wk = _each(lambda t, b, e, kh: _dot16(t, (b * e) * kh), inv, beta, egc, k)
    pqk = _each(lambda qh, kh, d: _dot16(qh, kh, NT) * d, q, k, decay)
    g_last = _each(lambda c: c[CHUNK - 1:CHUNK, :], gc_col)
    kdec = _each(lambda kh, gl, c: kh * jnp.exp(gl - c), k, g_last, gc_col)
    w = _each(lambda uh, wkh, r: uh - _dot16(wkh, r), u, wk, r_state)
    o = _each(lambda e, qh, r, ph, wh: e * _dot16(qh, r) + _dot16(ph, wh), egc, q, r_state, pqk, w)
    r_new = _each(lambda gl, r, kd, wh: jnp.exp(gl) * r + _dot16(kd, wh, TN), g_last, r_state, kdec, w)
    return o, r_new, inv


DELTA_BLK = 512


def _delta_blocks(n_batch, seq):
    nblk = seq // DELTA_BLK
    cpb = DELTA_BLK // CHUNK

    def rows(width, order):
        return pl.BlockSpec((DELTA_BLK, width), lambda b, i: (b * nblk + order(i), 0))

    def states(order, side):
        return pl.BlockSpec((cpb, B_HEADS, side, side), lambda b, i: (b * nblk + order(i), 0, 0, 0))

    return nblk, cpb, rows, states


def _head_cols(h):
    return [pl.ds(part * B_HEADS * B_DIM + h * B_DIM, B_DIM) for part in range(3)]


def _load_heads(c_ref, bg_ref, state_ref, rows):
    bg_c = bg_ref[rows, :]
    cols = [_head_cols(h) for h in range(B_HEADS)]
    return ([state_ref[h] for h in range(B_HEADS)], [c_ref[rows, c[0]] for c in cols], [c_ref[rows, c[1]] for c in cols],
            [c_ref[rows, c[2]] for c in cols], [bg_c[:, h:h + 1] for h in range(B_HEADS)],
            [bg_c[:, B_HEADS + h:B_HEADS + h + 1] for h in range(B_HEADS)])


def _delta_fwd(conv, bg, n_batch, seq):
    nblk, cpb, rows_spec, states_spec = _delta_blocks(n_batch, seq)

    def forward(i):
        return i

    def body(c_ref, bg_ref, o_ref, st_ref, inv_ref, r_ref):
        @pl.when(pl.program_id(1) == 0)
        def _():
            r_ref[...] = jnp.zeros_like(r_ref)

        def step(c, carry):
            rows = pl.ds(pl.multiple_of(c * CHUNK, CHUNK), CHUNK)
            args = _load_heads(c_ref, bg_ref, r_ref, rows)
            o, r_new, inv = _delta_chunk(*args)
            for h in range(B_HEADS):
                st_ref[c, h] = args[0][h]
                inv_ref[c, h] = inv[h]
                o_ref[rows, pl.ds(h * B_DIM, B_DIM)] = o[h]
            for h in range(B_HEADS):
                r_ref[h] = r_new[h]
            return carry

        lax.fori_loop(0, cpb, step, 0)

    n_tok = n_batch * seq
    return pl.pallas_call(
        body,
        name="delta_fwd",
        grid=(n_batch, nblk),
        in_specs=[rows_spec(B_CONV, forward), rows_spec(LANE, forward)],
        out_specs=[rows_spec(B_HEADS * B_DIM, forward), states_spec(forward, B_DIM), states_spec(forward, CHUNK)],
        out_shape=[jax.ShapeDtypeStruct((n_tok, B_HEADS * B_DIM), F32),
                   jax.ShapeDtypeStruct((n_tok // CHUNK, B_HEADS, B_DIM, B_DIM), F32),
                   jax.ShapeDtypeStruct((n_tok // CHUNK, B_HEADS, CHUNK, CHUNK), F32)],
        scratch_shapes=[pltpu.VMEM((B_HEADS, B_DIM, B_DIM), F32)],
        compiler_params=pltpu.CompilerParams(dimension_semantics=("arbitrary", "arbitrary")),
    )(conv, bg)


def _delta_bwd(conv, bg, states, inverses, do, n_batch, seq):
    nblk, cpb, rows_spec, states_spec = _delta_blocks(n_batch, seq)

    def backward(i):
        return nblk - 1 - i

    def body(c_ref, bg_ref, st_ref, inv_ref, do_ref, dc_ref, dbg_ref, dr_ref):
        @pl.when(pl.program_id(1) == 0)
        def _():
            dr_ref[...] = jnp.zeros_like(dr_ref)

        def step(n, carry):
            c = cpb - 1 - n
            rows = pl.ds(pl.multiple_of(c * CHUNK, CHUNK), CHUNK)
            saved = [inv_ref[c, h] for h in range(B_HEADS)]
            _, vjp = jax.vjp(lambda *args: _delta_chunk(*args, saved_inv=saved)[:2],
                             *_load_heads(c_ref, bg_ref, st_ref.at[c], rows))
            do = [do_ref[rows, pl.ds(h * B_DIM, B_DIM)] for h in range(B_HEADS)]
            dr, dq, dk, dv, dbeta, dg = vjp((do, [dr_ref[h] for h in range(B_HEADS)]))
            lane = lax.broadcasted_iota(jnp.int32, (CHUNK, LANE), 1)
            dbg = jnp.zeros((CHUNK, LANE), F32)
            for h in range(B_HEADS):
                cq, ck, cv = _head_cols(h)
                dr_ref[h] = dr[h]
                dc_ref[rows, cq] = dq[h]
                dc_ref[rows, ck] = dk[h]
                dc_ref[rows, cv] = dv[h]
                dbg = dbg + jnp.where(lane == h, dbeta[h], 0.0) + jnp.where(lane == h + B_HEADS, dg[h], 0.0)
            dbg_ref[rows, :] = dbg
            return carry

        lax.fori_loop(0, cpb, step, 0)

    n_tok = n_batch * seq
    return pl.pallas_call(
        body,
        name="delta_bwd",
        grid=(n_batch, nblk),
        in_specs=[rows_spec(B_CONV, backward), rows_spec(LANE, backward), states_spec(backward, B_DIM),
                  states_spec(backward, CHUNK), rows_spec(B_HEADS * B_DIM, backward)],
        out_specs=[rows_spec(B_CONV, backward), rows_spec(LANE, backward)],
        out_shape=[jax.ShapeDtypeStruct((n_tok, B_CONV), F32), jax.ShapeDtypeStruct((n_tok, LANE), F32)],
        scratch_shapes=[pltpu.VMEM((B_HEADS, B_DIM, B_DIM), F32)],
        compiler_params=pltpu.CompilerParams(dimension_semantics=("arbitrary", "arbitrary")),
    )(conv, bg, states, inverses, do)


def _lane_row(vec4, first):
    return jnp.concatenate([jnp.zeros((1, first), F32), vec4.reshape(1, B_HEADS).astype(F32),
                            jnp.zeros((1, LANE - first - B_HEADS), F32)], axis=1)


def _local_step(x3d, p3d, tgt3d, w_in, small, rest_weights, send_grads, send_w_in):
    n_batch, seq, _ = x3d.shape
    n_tok = n_batch * seq
    x, p, tgt = x3d.reshape(n_tok, D), p3d.reshape(n_tok, -1), tgt3d.reshape(n_tok, D)
    g_mix, g_ffn, g_ple, g_final = (small[k].reshape(1, D) for k in ("g_mix", "g_ffn", "g_ple", "g_final"))
    w_onorm = small["w_onorm"].reshape(1, B_DIM)
    al_row = _lane_row(small["a_log"], B_HEADS)
    dtb_row = _lane_row(small["dt_bias"], B_HEADS)
    rel_bias = small["rel_bias"].reshape(A_HEADS, -1)
    bias = _rel_bias_table(rel_bias)
    conv_w = small["conv_w"].reshape(4, B_CONV)

    h1 = _rms_fwd(x, g_mix, name="rms_mix")
    projp = _mm(h1, w_in, tb=True, out_dtype=BF16, name="mm_proj", tn=1920)
    bd = _mm(h1, w_in[P_BD:], tb=True, name="mm_beta_decay", tn=LANE)
    y_a = _attn_fwd(projp, bias, n_batch, seq)
    conv = _conv_fwd(projp, conv_w, n_batch, seq)
    (bg,) = _rowwise(lambda raw, al, dtb: ([_gate_scalars(raw, al, dtb)], []), [_full(bd)],
                     [_full(al_row), _full(dtb_row)], [(LANE, F32, LANE, 0, 0)], name="gate_scalars", tr=1024)
    o_b, states, inverses = _delta_fwd(conv, bg, n_batch, seq)
    (y_b,) = _rowwise(lambda o, z, wn: ([_gated_norm(o, z, wn)], []), [(o_b, LANE, 0, 1), (projp, LANE, P_Z // LANE, 1)],
                      [_full(w_onorm)], [(B_HEADS * B_DIM, BF16, LANE, 0, 1)], name="gated_norm", tr=1024, ncol=B_HEADS)
    w = rest_weights(y_b)
    t_a = _mm(y_a, w["w_branch_a"], tb=True, out_dtype=BF16, name="mm_branch_a", tn=1024)
    t_b = _mm(y_b, w["w_branch_b"], tb=True, out_dtype=BF16, name="mm_branch_b", tn=1024)
    half = D // 2
    gate_rows = [(projp, half, P_GATE // half, 1), (projp, half, P_GATE // half + 2, 1), (t_a, half, 0, 1), (t_b, half, 0, 1)]
    (merged,) = _rowwise(lambda ga, gb, ta, tb: ([_merge(ga, gb, ta, tb)], []), gate_rows, [], [(D, BF16, half, 0, 1)],
                         name="merge", tr=512, ncol=2)
    x1, h2 = _mm(merged, w["w_out"], name="mm_out", tn=1024, epilogue=_residual_rms, rows=[x], bcs=[g_ffn], outs=(F32, BF16))
    gu, act = _mm(h2, w["w_gate_up"], tb=True, name="mm_gate_up", tm=256, tn=2 * D_FF, epilogue=lambda r: ([r, _swiglu(r)], []),
                  outs=(BF16, (BF16, D_FF)))
    x2, h3 = _mm(act, w["w_down"], name="mm_down", tm=512, tn=1024, tk=D_FF, epilogue=_residual_rms, rows=[x1], bcs=[g_ple],
                 outs=(F32, BF16))
    pp = _mm(p, w["w_ple_proj"], tb=True, name="mm_ple_proj", tn=1024)

    def head_fn(pgb, x2b, ppb, tb, gb):
        loss, (dx2, dpg, dpp, dg) = jax.value_and_grad(_head_loss, argnums=(0, 1, 2, 4))(x2b, pgb, ppb, tb, gb)
        return [dx2, dpg, dpp], [dg, jnp.full((1, D), loss, F32)]

    dx3, dpg, dpp, dg_final, loss_row = _mm(h3, w["w_ple_gate"], name="mm_ple_gate_loss", tm=256, tn=1024, epilogue=head_fn,
                                            rows=[x2, pp, tgt], bcs=[g_final], outs=(F32, BF16, BF16), n_red=2)
    gw = {}
    gw["w_ple_proj"] = _mm(dpp, p, ta=True, out_dtype=BF16, name="mm_d_ple_proj", tn=256)
    gw["w_ple_gate"] = _mm(h3, dpg, ta=True, out_dtype=BF16, name="mm_d_ple_gate", tn=512)
    dx2, dg_ple = _mm(dpg, w["w_ple_gate"], tb=True, name="mm_dh3", tm=512, tn=1024, epilogue=_rms_bwd, rows=[x2, dx3],
                      bcs=[g_ple], outs=(F32,), n_red=1)
    gw["w_down"] = _mm(act, dx2, ta=True, out_dtype=BF16, name="mm_d_down", tm=1408, tn=256)
    dact = _mm(dx2, w["w_down"], tb=True, out_dtype=BF16, name="mm_dact", tn=D_FF)

    def swiglu_bwd(gub, dab):
        _, vjp = jax.vjp(_swiglu, gub)
        return [vjp(dab)[0]], []

    (dgu,) = _rowwise(swiglu_bwd, [_full(gu), _full(dact)], [], [(2 * D_FF, BF16, 2 * D_FF, 0, 0)], name="swiglu_bwd", tr=256)
    gw["w_gate_up"] = _mm(dgu, h2, ta=True, out_dtype=BF16, name="mm_d_gate_up", tm=512, tn=1024)
    dx1, dg_ffn = _mm(dgu, w["w_gate_up"], name="mm_dh2", tm=256, tn=1024, tk=2 * D_FF, epilogue=_rms_bwd, rows=[x1, dx2],
                      bcs=[g_ffn], outs=(F32,), n_red=1)
    gw["w_out"] = _mm(merged, dx1, ta=True, out_dtype=BF16, name="mm_d_out", tn=512)
    dmerged = _mm(dx1, w["w_out"], tb=True, name="mm_dmerged", tn=1024)

    def merge_bwd(ga, gb, ta, tb, dm):
        _, vjp = jax.vjp(_merge, ga, gb, ta, tb)
        return list(vjp(dm)), []

    dga, dgb, dta, dtb = _rowwise(merge_bwd, gate_rows + [(dmerged, half, 0, 1)], [], [(D, BF16, half, 0, 1)] * 4,
                                  name="merge_bwd", tr=512, ncol=2)
    gw["w_branch_a"] = _mm(dta, y_a, ta=True, out_dtype=BF16, name="mm_d_branch_a", tn=512)
    gw["w_branch_b"] = _mm(dtb, y_b, ta=True, out_dtype=BF16, name="mm_d_branch_b", tn=512)
    dya = _mm(dta, w["w_branch_a"], name="mm_dya", tn=512)
    dyb = _mm(dtb, w["w_branch_b"], name="mm_dyb", tn=512)

    w_onorm = w_onorm + send_grads(gw)[0, 0]

    def gated_norm_bwd(o, z, dy, wn):
        _, vjp = jax.vjp(_gated_norm, o, z, wn)
        do, dz, dwn = vjp(dy)
        return [do, dz], [dwn]

    do_b, dz, dw_onorm = _rowwise(
        gated_norm_bwd, [(o_b, LANE, 0, 1), (projp, LANE, P_Z // LANE, 1), (dyb, LANE, 0, 1)], [_full(w_onorm)],
        [(B_HEADS * B_DIM, F32, LANE, 0, 1), (B_HEADS * B_DIM, BF16, LANE, 0, 1)], [(B_DIM, B_DIM, 0)],
        name="gated_norm_bwd", tr=1024, ncol=B_HEADS)
    dconv_out, dbg = _delta_bwd(conv, bg, states, inverses, do_b, n_batch, seq)

    def gate_scalars_bwd(raw, dbgb, al, dtb):
        _, vjp = jax.vjp(_gate_scalars, raw, al, dtb)
        draw, dal, ddtb = vjp(dbgb)
        return [draw], [dal, ddtb]

    dbd, dal_row, ddtb_row = _rowwise(gate_scalars_bwd, [_full(bd), _full(dbg)], [_full(al_row), _full(dtb_row)],
                                      [(LANE, BF16, LANE, 0, 0)], [(LANE, LANE, 0), (LANE, LANE, 0)], name="gate_scalars_bwd",
                                      tr=1024)
    dconv, dconv_w = _conv_bwd(projp, conv_w, dconv_out, n_batch, seq)
    dq_a, dk_a, dv_a, dbias = _attn_bwd(projp, bias, dya, n_batch, seq)
    dprojp = jnp.concatenate([dq_a, dk_a, dv_a, dconv, dz, dga, dgb, dbd], axis=1)
    sent = send_w_in(_mm(dprojp, h1, ta=True, out_dtype=BF16, name="mm_d_in", tm=640, tn=1024))
    sent, dprojp = lax.optimization_barrier((sent, dprojp))
    grad_x, dg_mix = _mm(dprojp, w_in, name="mm_dh1", tm=256, tn=1024, tk=P_END, epilogue=_rms_bwd, rows=[x, dx1],
                         bcs=[g_mix + sent[0, 0]], outs=(F32,), n_red=1)

    _, bias_vjp = jax.vjp(_rel_bias_table, rel_bias)
    gs = {
        "g_mix": dg_mix, "g_ffn": dg_ffn, "g_ple": dg_ple, "g_final": dg_final, "w_onorm": dw_onorm,
        "conv_w": dconv_w, "rel_bias": bias_vjp(dbias)[0],
        "a_log": dal_row[0, B_HEADS:2 * B_HEADS], "dt_bias": ddtb_row[0, B_HEADS:2 * B_HEADS],
    }
    return loss_row[:, :1], grad_x.reshape(n_batch, seq, D), gs


MATRICES = (("w_in", 1), ("w_gate_up", 1), ("w_branch_a", 1), ("w_branch_b", 1), ("w_out", 0), ("w_down", 0),
            ("w_ple_gate", 0), ("w_ple_proj", 1))
TAPS_PER_SHARD = B_CONV // N_DEV


def _held(shard, axis):
    return shard if axis == 0 else shard.T


def _from_gathered(slabs):
    return slabs.reshape(-1, slabs.shape[-1])


def _to_owner(held):
    return held.reshape(N_DEV, held.shape[0] // N_DEV, held.shape[1])


def _permute_w_in(held):
    n_gate = P_BD - P_GATE
    row = lax.broadcasted_iota(jnp.int32, (P_END, 1), 0)
    same = jnp.pad(held, ((0, P_END - D_IN), (0, 0)))
    up = jnp.pad(held[8:], ((0, P_END - D_IN + 8), (0, 0)))
    down = jnp.pad(held[:P_GATE + 8], ((n_gate, P_END - P_BD - 8), (0, 0)))
    zero = jnp.zeros((), held.dtype)
    return jnp.where(row < P_GATE, same, jnp.where(row < P_BD, up, jnp.where(row < P_BD + 8, down, zero)))


def _unpermute_w_in(gp):
    n_gate = P_BD - P_GATE
    row = lax.broadcasted_iota(jnp.int32, (D_IN, 1), 0)
    same = gp[:D_IN]
    up = jnp.pad(gp[n_gate:], ((0, D_IN - (P_END - n_gate)), (0, 0)))
    down = jnp.pad(gp[:P_BD], ((8, 0), (0, 0)))
    return jnp.where(row < P_GATE, same, jnp.where(row < P_GATE + 8, up, down))


SMALL_ROWS = 16
SMALL_LAYOUT = (("g_mix", 0, D), ("g_ffn", 1, D), ("g_ple", 2, D), ("g_final", 3, D), ("conv_w", 4, 4 * B_CONV),
                ("rel_bias", 10, A_HEADS * (2 * REL_CLIP + 1)), ("w_onorm", 13, B_DIM), ("a_log", 14, B_HEADS),
                ("dt_bias", 14, B_HEADS), ("loss", 15, 1))


def _pack_small(gs):
    rows = {}
    for name, row, n in SMALL_LAYOUT:
        rows.setdefault(row, []).append(gs[name].reshape(-1).astype(F32))
    parts = []
    for row in sorted(rows):
        flat = jnp.concatenate(rows[row])
        parts.append(jnp.concatenate([flat, jnp.zeros((-flat.shape[0] % D,), F32)]))
    flat = jnp.concatenate(parts)
    assert flat.shape[0] == SMALL_ROWS * D, flat.shape
    return flat.reshape(SMALL_ROWS, D)


def _unpack_small(blk):
    flat, out, used = blk.reshape(-1), {}, {}
    for name, row, n in SMALL_LAYOUT:
        start = row * D + used.get(row, 0)
        out[name] = flat[start:start + n]
        used[row] = used.get(row, 0) + n
    return out


def _position():
    return lax.axis_index("x"), lax.axis_index("y"), lax.axis_index("c")


PEERS = N_DEV - 1


def _comm_call(body, arrays, out_shapes, *, name):
    n = len(arrays)
    return pl.pallas_call(
        body,
        name=name,
        out_shape=out_shapes,
        in_specs=[HBM_SPEC] * n,
        out_specs=[HBM_SPEC] * n,
        scratch_shapes=[pltpu.SemaphoreType.DMA((PEERS * n,)), pltpu.SemaphoreType.DMA((PEERS * n,)),
                        pltpu.SemaphoreType.DMA((n,))],
    )(*arrays)


def _weights_allgather(shards):
    n = len(shards)

    def body(*refs):
        ins, outs = refs[:n], refs[n:2 * n]
        send_sems, recv_sems, local_sems = refs[2 * n:]
        x, y, c = _position()
        me, sibling = (x, y, c), (x, y, 1 - c)
        chips = [(1 - x, y), (x, 1 - y), (1 - x, 1 - y)]

        def slab(a, px, py, pc):
            return outs[a].at[4 * px + 2 * py + pc]

        def copy(a, k, block, to, src=None):
            return pltpu.make_async_remote_copy(src_ref=slab(a, *block) if src is None else src, dst_ref=slab(a, *block),
                                                send_sem=send_sems.at[PEERS * a + k], recv_sem=recv_sems.at[PEERS * a + k],
                                                device_id=to, device_id_type=MESH)

        local = [pltpu.make_async_copy(ins[a], slab(a, *me), local_sems.at[a]) for a in range(n)]
        sent = [copy(a, 1 + j, me, (*chip, c), src=ins[a]) for a in range(n) for j, chip in enumerate(chips)]
        sent += [copy(a, 0, me, sibling, src=ins[a]) for a in range(n)]
        for cp in sent + local:
            cp.start()
        for a in range(n):
            for j, chip in enumerate(chips):
                copy(a, 1 + j, (*chip, c), me).wait_recv()
                passed = copy(a, 4 + j, (*chip, c), sibling)
                passed.start()
                sent.append(passed)
        for a in range(n):
            copy(a, 0, sibling, me).wait_recv()
            for j, chip in enumerate(chips):
                copy(a, 4 + j, (*chip, 1 - c), me).wait_recv()
        for cp in sent:
            cp.wait_send()
        for cp in local:
            cp.wait()

    return _comm_call(body, shards, [jax.ShapeDtypeStruct((N_DEV,) + s.shape, s.dtype) for s in shards],
                      name="weights_allgather")


def _grads_exchange(by_owner):
    n = len(by_owner)

    def body(*refs):
        ins, outs = refs[:n], refs[n:2 * n]
        send_sems, recv_sems, local_sems = refs[2 * n:]
        x, y, c = _position()
        mine = 4 * x + 2 * y + c
        local = [pltpu.make_async_copy(ins[a].at[mine], outs[a].at[mine], local_sems.at[a]) for a in range(n)]
        for cp in local:
            cp.start()
        flips = [(dx, dy, dc) for dx in (0, 1) for dy in (0, 1) for dc in (0, 1) if dx + dy + dc]
        pending = []
        for k, (dx, dy, dc) in enumerate(flips):
            px, py, pc = (1 - x if dx else x), (1 - y if dy else y), (1 - c if dc else c)
            peer = 4 * px + 2 * py + pc
            for a in range(n):
                def remote(slot):
                    return pltpu.make_async_remote_copy(src_ref=ins[a].at[peer], dst_ref=outs[a].at[slot],
                                                        send_sem=send_sems.at[PEERS * a + k], recv_sem=recv_sems.at[PEERS * a + k],
                                                        device_id=(px, py, pc), device_id_type=MESH)

                sent = remote(mine)
                sent.start()
                pending.append((sent, remote(peer)))
        for sent, landed in pending:
            landed.wait_recv()
            sent.wait_send()
        for cp in local:
            cp.wait()

    return _comm_call(body, by_owner, [jax.ShapeDtypeStruct(g.shape, g.dtype) for g in by_owner], name="grads_exchange")


SEM_SPEC = pl.BlockSpec(memory_space=pltpu.SEMAPHORE)
DATAFLOW = pltpu.SideEffectType.DATAFLOW_SIDE_EFFECTING


def _peer_copies(srcs, lands, send_sems, recv_sems, by_owner, arrival):
    x, y, c = _position()
    mine = 4 * x + 2 * y + c
    copies = []
    for k, (dx, dy, dc) in enumerate([(dx, dy, dc) for dx in (0, 1) for dy in (0, 1) for dc in (0, 1) if dx + dy + dc]):
        px, py, pc = (1 - x if dx else x), (1 - y if dy else y), (1 - c if dc else c)
        peer = 4 * px + 2 * py + pc
        for a, (src, land) in enumerate(zip(srcs, lands)):
            copies.append(pltpu.make_async_remote_copy(
                src_ref=src.at[peer] if by_owner else src, dst_ref=land.at[peer if arrival else mine],
                send_sem=send_sems.at[PEERS * a + k], recv_sem=recv_sems.at[PEERS * a + k],
                device_id=(px, py, pc), device_id_type=MESH))
    return copies


def _exchange_start(sources, by_owner, *, name):
    n = len(sources)
    lands = [lax.empty((N_DEV,) + (s.shape[1:] if by_owner else s.shape), s.dtype) for s in sources]

    def body(*refs):
        send_sems, recv_sems, token = refs[2 * n], refs[2 * n + 1], refs[-1]
        for copy in _peer_copies(refs[:n], refs[n:2 * n], send_sems, recv_sems, by_owner, arrival=False):
            copy.start()
        token[...] = jnp.zeros_like(token)

    sems = pltpu.SemaphoreType.DMA((PEERS * n,))
    outs = pl.pallas_call(
        body,
        name=name,
        out_shape=(sems, sems, *[pltpu.HBM(a.shape, a.dtype) for a in sources + lands], jax.ShapeDtypeStruct((8, LANE), F32)),
        in_specs=[HBM_SPEC] * (2 * n),
        out_specs=(SEM_SPEC, SEM_SPEC, *[HBM_SPEC] * (2 * n), pl.BlockSpec(memory_space=pltpu.VMEM)),
        input_output_aliases={i: 2 + i for i in range(2 * n)},
        compiler_params=pltpu.CompilerParams(has_side_effects=DATAFLOW),
    )(*[pltpu.with_memory_space_constraint(a, pltpu.HBM) for a in sources + lands])
    return outs[:-1], outs[-1]


def _exchange_wait(started, after, by_owner, *, name):
    send_sems, recv_sems, *arrays = started
    n = len(arrays) // 2

    def body(*refs):
        for copy in _peer_copies(refs[:n], refs[n:2 * n], refs[2 * n], refs[2 * n + 1], by_owner, arrival=True):
            copy.wait_send()
            copy.wait_recv()

    outs = pl.pallas_call(
        body,
        name=name,
        out_shape=[pltpu.HBM(a.shape, a.dtype) for a in arrays],
        in_specs=[HBM_SPEC] * (2 * n) + [SEM_SPEC, SEM_SPEC, pl.BlockSpec(memory_space=pl.ANY)],
        out_specs=[HBM_SPEC] * (2 * n),
        input_output_aliases={i: i for i in range(2 * n)},
        compiler_params=pltpu.CompilerParams(has_side_effects=DATAFLOW),
    )(*arrays, send_sems, recv_sems, after)
    return outs[:n], outs[n:]


def _slot_sum(g_ref, own_ref):
    if own_ref is not None:
        x, y, c = _position()
        mine = 4 * x + 2 * y + c
    acc = None
    for j in range(N_DEV):
        part = g_ref[j] if own_ref is None else jnp.where(mine == j, own_ref[...], g_ref[j])
        acc = part.astype(F32) if acc is None else acc + part.astype(F32)
    return acc


def _sum_slots(got, *, name, tr):
    _, rows, cols = got.shape
    tr = _tile(rows, tr, 16)

    def body(g_ref, o_ref):
        o_ref[...] = _slot_sum(g_ref, None)

    return pl.pallas_call(
        body,
        name=name,
        grid=(rows // tr,),
        in_specs=[pl.BlockSpec((N_DEV, tr, cols), lambda i: (0, i, 0))],
        out_specs=pl.BlockSpec((tr, cols), lambda i: (i, 0)),
        out_shape=jax.ShapeDtypeStruct((rows, cols), F32),
        compiler_params=pltpu.CompilerParams(dimension_semantics=("parallel",)),
    )(got)


def _adamw(wt, g, m, v, *, name, own=None):
    slots = own is not None
    shape = wt.shape
    two_d = (-1, shape[-1]) if wt.ndim > 1 else (1, -1)
    args = [a.reshape(two_d) for a in (wt, m, v)]
    rows, cols = args[0].shape
    if rows % 16 == 0:
        tr, tc = _tile(rows, 256, 16), cols
    else:
        tr, tc = rows, _tile(cols, 256 if rows > 64 else 512)
    args.insert(1, g.reshape((N_DEV, rows, cols) if slots else (rows, cols)))
    if slots:
        args.append(own.reshape(N_DEV, rows, cols))

    def body(w_ref, g_ref, m_ref, v_ref, *refs):
        go_ref, d_ref, nm_ref, nv_ref = refs[-4:]
        gv = _slot_sum(g_ref, refs[0]) if slots else g_ref[...]
        go_ref[...] = gv
        m2 = ADAM_B1 * m_ref[...] + (1.0 - ADAM_B1) * gv
        v2 = ADAM_B2 * v_ref[...] + (1.0 - ADAM_B2) * (gv * gv)
        m_hat = m2 / (1.0 - ADAM_B1 ** ADAM_STEP)
        v_hat = v2 / (1.0 - ADAM_B2 ** ADAM_STEP)
        d_ref[...] = -ADAM_LR * (m_hat / (jnp.sqrt(v_hat) + ADAM_EPS) + ADAM_WD * w_ref[...])
        nm_ref[...] = m2
        nv_ref[...] = v2

    def own_block(i, j):
        x, y, c = _position()
        return 4 * x + 2 * y + c, i, j

    spec = pl.BlockSpec((tr, tc), lambda i, j: (i, j))
    g_spec = pl.BlockSpec((N_DEV, tr, tc), lambda i, j: (0, i, j)) if slots else spec
    own_spec = pl.BlockSpec((None, tr, tc), own_block)
    outs = pl.pallas_call(
        body,
        name=name,
        grid=(rows // tr, cols // tc),
        in_specs=[spec, g_spec, spec, spec] + ([own_spec] if slots else []),
        out_specs=[spec] * 4,
        out_shape=[jax.ShapeDtypeStruct((rows, cols), F32)] * 4,
        compiler_params=pltpu.CompilerParams(dimension_semantics=("parallel", "parallel")),
    )(*args)
    return tuple(o.reshape(shape) for o in outs)


WEIGHTS = ("g_mix", "w_in", "conv_w", "a_log", "dt_bias", "rel_bias", "w_onorm", "w_branch_a", "w_branch_b", "w_out", "g_ffn",
           "w_gate_up", "w_down", "g_ple", "w_ple_gate", "w_ple_proj", "g_final")


def kernel(x, p, g_mix, w_in, conv_w, a_log, dt_bias, rel_bias, w_onorm, w_branch_a, w_branch_b, w_out, g_ffn, w_gate_up, w_down, g_ple, w_ple_gate, w_ple_proj, g_final, loss_target, m_g_mix, m_w_in, m_conv_w, m_a_log, m_dt_bias, m_rel_bias, m_w_onorm, m_w_branch_a, m_w_branch_b, m_w_out, m_g_ffn, m_w_gate_up, m_w_down, m_g_ple, m_w_ple_gate, m_w_ple_proj, m_g_final, v_g_mix, v_w_in, v_conv_w, v_a_log, v_dt_bias, v_rel_bias, v_w_onorm, v_w_branch_a, v_w_branch_b, v_w_out, v_g_ffn, v_w_gate_up, v_w_down, v_g_ple, v_w_ple_gate, v_w_ple_proj, v_g_final):
    given = dict(g_mix=g_mix, w_in=w_in, conv_w=conv_w, a_log=a_log, dt_bias=dt_bias, rel_bias=rel_bias, w_onorm=w_onorm,
                 w_branch_a=w_branch_a, w_branch_b=w_branch_b, w_out=w_out, g_ffn=g_ffn, w_gate_up=w_gate_up, w_down=w_down,
                 g_ple=g_ple, w_ple_gate=w_ple_gate, w_ple_proj=w_ple_proj, g_final=g_final)
    mom1 = dict(g_mix=m_g_mix, w_in=m_w_in, conv_w=m_conv_w, a_log=m_a_log, dt_bias=m_dt_bias, rel_bias=m_rel_bias,
                w_onorm=m_w_onorm, w_branch_a=m_w_branch_a, w_branch_b=m_w_branch_b, w_out=m_w_out, g_ffn=m_g_ffn,
                w_gate_up=m_w_gate_up, w_down=m_w_down, g_ple=m_g_ple, w_ple_gate=m_w_ple_gate, w_ple_proj=m_w_ple_proj,
                g_final=m_g_final)
    mom2 = dict(g_mix=v_g_mix, w_in=v_w_in, conv_w=v_conv_w, a_log=v_a_log, dt_bias=v_dt_bias, rel_bias=v_rel_bias,
                w_onorm=v_w_onorm, w_branch_a=v_w_branch_a, w_branch_b=v_w_branch_b, w_out=v_w_out, g_ffn=v_g_ffn,
                w_gate_up=v_w_gate_up, w_down=v_w_down, g_ple=v_g_ple, w_ple_gate=v_w_ple_gate, w_ple_proj=v_w_ple_proj,
                g_final=v_g_final)
    mine = 4 * lax.axis_index("x") + 2 * lax.axis_index("y") + lax.axis_index("c")

    my_slot = (jnp.arange(N_DEV) == mine)[:, None, None]
    rest = MATRICES[1:]
    in_flight = {}

    got_in, got_taps = _weights_allgather([_held(w_in[0], 1).astype(BF16), conv_w[0]])
    in_flight["weights"], weights_sent = _exchange_start([_held(given[name][0], axis).astype(BF16) for name, axis in rest], False,
                                                         name="weights_start")
    small = dict(g_mix=g_mix + weights_sent[0, 0], g_ffn=g_ffn, g_ple=g_ple, g_final=g_final, w_onorm=w_onorm, a_log=a_log,
                 dt_bias=dt_bias, rel_bias=rel_bias, conv_w=jnp.transpose(got_taps, (1, 0, 2)).reshape(4, B_CONV))

    def rest_weights(after):
        shards, landed = _exchange_wait(in_flight.pop("weights"), after, False, name="weights_wait")
        return {name: _from_gathered(jnp.where(my_slot, shard[None], slabs)) for (name, _), shard, slabs in zip(rest, shards, landed)}

    def send_grads(gw):
        in_flight["grads"], sent = _exchange_start([_to_owner(gw[name]) for name, _ in rest], True, name="grads_start")
        return sent

    def send_w_in(g_in):
        in_flight["grad_in"], sent = _exchange_start([_to_owner(_unpermute_w_in(g_in))], True, name="grad_in_start")
        return sent

    loss_part, grad_x, gs = _local_step(x, p[0], loss_target, _permute_w_in(_from_gathered(got_in)), small, rest_weights,
                                        send_grads, send_w_in)
    gs["loss"] = loss_part

    updates = {}

    def update_matrices(matrices, own_slabs, landed):
        for (name, axis), own_slab, slots in zip(matrices, own_slabs, landed):
            w_held, m_held, v_held = (_held(a[name][0], axis) for a in (given, mom1, mom2))
            outs = _adamw(w_held, slots, m_held, v_held, name=f"adamw_{name}", own=own_slab)
            updates[name] = tuple(_held(o, axis)[None] for o in outs)

    update_matrices(rest, *_exchange_wait(in_flight["grads"], grad_x, True, name="grads_wait"))
    update_matrices(MATRICES[:1], *_exchange_wait(in_flight["grad_in"], updates[rest[-1][0]][0], True, name="grad_in_wait"))
    small_block, _ = lax.optimization_barrier((_pack_small(gs), updates["w_in"][0]))
    (got_small,) = _grads_exchange([jnp.broadcast_to(small_block, (N_DEV, SMALL_ROWS, D))])
    small_sum = _unpack_small(_sum_slots(got_small, name="sum_small_grads", tr=16))
    loss = small_sum.pop("loss")[0]
    conv_all = small_sum.pop("conv_w").reshape(4, N_DEV, TAPS_PER_SHARD)
    small_sum["conv_w"] = lax.dynamic_index_in_dim(conv_all, mine, axis=1, keepdims=False)
    for name, g in small_sum.items():
        updates[name] = _adamw(given[name], g.reshape(given[name].shape), mom1[name], mom2[name], name=f"adamw_{name}")
    return (loss, grad_x, *[updates[name][k] for k in range(4) for name in WEIGHTS])
```

```python
import jax
import jax.numpy as jnp
from jax import lax
from jax.experimental import pallas as pl
from jax.experimental.pallas import tpu as pltpu

F32 = jnp.float32
BF16 = jnp.bfloat16
DELTA_PREC = lax.Precision.HIGH
MESH = pl.DeviceIdType.MESH

N_DEV = 8
D = 1024
CHUNK = 64
EPS = 1e-6
A_HEADS, A_DIM, A_WIDTH = 8, 64, 512
A_BAND = 9 * CHUNK
A_PAD = 8 * CHUNK
REL_CLIP = 128
B_HEADS, B_DIM = 4, 128
B_CONV = 1536
D_FF = 2816
D_IN = 5640
P_CONV, P_Z, P_GATE, P_BD, P_END = 1536, 3072, 3584, 5632, 5760
LANE = 128

ADAM_LR, ADAM_B1, ADAM_B2, ADAM_EPS, ADAM_WD, ADAM_STEP = 0.001, 0.9, 0.999, 1e-08, 0.01, 10

NT = (((1,), (1,)), ((), ()))
TN = (((0,), (0,)), ((), ()))
NN = (((1,), (0,)), ((), ()))

HBM_SPEC = pl.BlockSpec(memory_space=pltpu.HBM)


def _tile(n, target, align=LANE):
    if n <= target:
        return n
    best = None
    for t in range(align, target + 1, align):
        if n % t == 0:
            best = t
    assert best is not None, (n, target, align)
    return best


def _mm(a, b, *, name, ta=False, tb=False, out_dtype=F32, tm=1024, tn=640, tk=None, epilogue=None, rows=(), bcs=(), outs=(),
        n_red=0):
    assert not (ta and tb)
    if ta:
        k_dim, m_dim = a.shape
    else:
        m_dim, k_dim = a.shape
    n_dim = b.shape[0] if tb else b.shape[1]
    assert b.shape[1 if tb else 0] == k_dim
    tm, tn = _tile(m_dim, tm), _tile(n_dim, tn)
    tk = _tile(k_dim, tk or (4096 if ta else 1024), 8 if ta else LANE)
    nk = k_dim // tk
    dn = TN if ta else NT if tb else NN
    if epilogue is None:
        outs = (out_dtype,)
    assert not n_red or tn == n_dim
    widths = [o[1] if isinstance(o, tuple) else None for o in outs]
    outs = [o[0] if isinstance(o, tuple) else o for o in outs]
    assert all(w is None for w in widths) or tn == n_dim
    n_extra, n_out = len(rows) + len(bcs), len(outs)

    def body(a_ref, b_ref, *refs):
        part = lax.dot_general(a_ref[...].astype(BF16), b_ref[...].astype(BF16), dn, preferred_element_type=F32)

        def finish(r):
            o_vals, r_vals = ([r], []) if epilogue is None else epilogue(r, *[x[...] for x in refs[:n_extra]])
            for ref, val in zip(refs[n_extra:n_extra + n_out], o_vals):
                ref[...] = val.astype(ref.dtype)
            first = pl.program_id(0) == 0
            for ref, val in zip(refs[n_extra + n_out:n_extra + n_out + n_red], r_vals):
                @pl.when(first)
                def _():
                    ref[...] = val

                @pl.when(jnp.logical_not(first))
                def _():
                    ref[...] += val

        if nk == 1:
            finish(part)
        else:
            acc_ref = refs[-1]
            k = pl.program_id(2)

            @pl.when(k == 0)
            def _():
                acc_ref[...] = part

            @pl.when(k > 0)
            def _():
                acc_ref[...] += part

            @pl.when(k == nk - 1)
            def _():
                finish(acc_ref[...])

    tile = pl.BlockSpec((tm, tn), lambda i, j, k: (i, j))
    col = pl.BlockSpec((1, tn), lambda i, j, k: (0, j))
    a_spec = pl.BlockSpec((tk, tm), lambda i, j, k: (k, i)) if ta else pl.BlockSpec((tm, tk), lambda i, j, k: (i, k))
    b_spec = pl.BlockSpec((tn, tk), lambda i, j, k: (j, k)) if tb else pl.BlockSpec((tk, tn), lambda i, j, k: (k, j))
    result = pl.pallas_call(
        body,
        name=name,
        grid=(m_dim // tm, n_dim // tn, nk),
        in_specs=[a_spec, b_spec] + [tile if r.shape[1] == n_dim else pl.BlockSpec((tm, r.shape[1]), lambda i, j, k: (i, 0))
                                     for r in rows] + [col] * len(bcs),
        out_specs=[tile if w is None else pl.BlockSpec((tm, w), lambda i, j, k: (i, 0)) for w in widths] + [col] * n_red,
        out_shape=[jax.ShapeDtypeStruct((m_dim, w or n_dim), dt) for dt, w in zip(outs, widths)]
        + [jax.ShapeDtypeStruct((1, n_dim), F32)] * n_red,
        scratch_shapes=[pltpu.VMEM((tm, tn), F32)] if nk > 1 else [],
        compiler_params=pltpu.CompilerParams(dimension_semantics=("arbitrary",) * 3 if n_red else ("parallel", "parallel", "arbitrary")),
    )(a, b, *rows, *bcs)
    return result[0] if epilogue is None else result


def _rowwise(fn, rows, bcs, outs, reds=(), *, name, tr, ncol=1):
    n_rows = rows[0][0].shape[0]
    tr = _tile(n_rows, tr, 8)
    nrow = n_rows // tr
    n_in, n_out = len(rows) + len(bcs), len(outs)

    def body(*refs):
        j, i = pl.program_id(0), pl.program_id(1)
        o_vals, r_vals = fn(*[r[...].astype(F32) for r in refs[:n_in]])
        for ref, val in zip(refs[n_in:n_in + n_out], o_vals):
            ref[...] = val.astype(ref.dtype)
        for ref, val, (_, _, stride) in zip(refs[n_in + n_out:], r_vals, reds):
            first = (i == 0) if stride else jnp.logical_and(i == 0, j == 0)

            @pl.when(first)
            def _():
                ref[...] = val

            @pl.when(jnp.logical_not(first))
            def _():
                ref[...] += val

    def spec(r, w, off, st, row_dep=True):
        if row_dep:
            return pl.BlockSpec((r, w), lambda j, i: (i, off + st * j))
        return pl.BlockSpec((r, w), lambda j, i: (0, off + st * j))

    in_specs = [spec(tr, w, off, st) for (_, w, off, st) in rows]
    in_specs += [spec(a.shape[0], w, off, st, False) for (a, w, off, st) in bcs]
    out_specs = [spec(tr, w, off, st) for (_, _, w, off, st) in outs]
    out_specs += [spec(1, w, 0, st, False) for (_, w, st) in reds]
    out_shape = [jax.ShapeDtypeStruct((n_rows, c), dt) for (c, dt, _, _, _) in outs]
    out_shape += [jax.ShapeDtypeStruct((1, c), F32) for (c, _, _) in reds]
    return pl.pallas_call(
        body,
        name=name,
        grid=(ncol, nrow),
        in_specs=in_specs,
        out_specs=out_specs,
        out_shape=out_shape,
        compiler_params=pltpu.CompilerParams(dimension_semantics=("arbitrary", "arbitrary")),
    )(*[r[0] for r in rows], *[b[0] for b in bcs])


def _full(a):
    return (a, a.shape[1], 0, 0)


def _rms(x, g):
    return x * lax.rsqrt(jnp.mean(x * x, axis=-1, keepdims=True) + EPS) * g


def _silu(x):
    return x * jax.nn.sigmoid(x)


def _softplus(x):
    return jnp.maximum(x, 0.0) + jnp.log(1.0 + jnp.exp(-jnp.abs(x)))


def _rms_fwd(x, g, *, name):
    (h,) = _rowwise(lambda xb, gb: ([_rms(xb, gb)], []), [_full(x)], [_full(g)], [(D, BF16, D, 0, 0)], name=name, tr=512)
    return h


def _residual_rms(r, x, g):
    x_new = x + r
    return [x_new, _rms(x_new, g)], []


def _rms_bwd(dh, x, dres, g):
    _, vjp = jax.vjp(_rms, x, g)
    dx, dg = vjp(dh)
    return [dx + dres], [dg]


def _gate_scalars(raw, al_row, dtb_row):
    lane = lax.broadcasted_iota(jnp.int32, raw.shape, 1)
    beta = jax.nn.sigmoid(raw)
    g = -jnp.exp(al_row) * _softplus(raw + dtb_row)
    return jnp.where(lane < B_HEADS, beta, jnp.where(lane < 2 * B_HEADS, g, 0.0))


def _gated_norm(o, z, w):
    return _rms(o, w) * _silu(z)


def _merge(ga, gb, ta, tb):
    return jax.nn.sigmoid(ga) * ta + jax.nn.sigmoid(gb) * tb


def _swiglu(gu):
    return _silu(gu[:, :D_FF]) * gu[:, D_FF:]


def _head_loss(x2, pg, pp, tgt, g):
    x3 = x2 + jax.nn.sigmoid(pg) * pp
    err = _rms(x3, g) - tgt
    return 0.5 * jnp.sum(jnp.mean(err * err, axis=-1))


CONV_W = 256


def _conv_taps(x, w):
    row = lax.broadcasted_iota(jnp.int32, x.shape, 0)
    shifted = [x] + [jnp.where(row >= s, pltpu.roll(x, s, 0), 0.0) for s in (1, 2, 3)]
    pre = shifted[0] * w[3:4]
    for s in (1, 2, 3):
        pre = pre + shifted[s] * w[3 - s:4 - s]
    return pre, shifted


def _conv_fwd(projp, conv_w, n_batch, seq):
    ncol = B_CONV // CONV_W
    first = P_CONV // CONV_W

    def body(x_ref, w_ref, o_ref):
        pre, _ = _conv_taps(x_ref[...].astype(F32), w_ref[...])
        o_ref[...] = _silu(pre)

    return pl.pallas_call(
        body,
        name="conv_fwd",
        grid=(ncol, n_batch),
        in_specs=[pl.BlockSpec((seq, CONV_W), lambda j, b: (b, first + j)), pl.BlockSpec((4, CONV_W), lambda j, b: (0, j))],
        out_specs=pl.BlockSpec((seq, CONV_W), lambda j, b: (b, j)),
        out_shape=jax.ShapeDtypeStruct((n_batch * seq, B_CONV), F32),
        compiler_params=pltpu.CompilerParams(dimension_semantics=("parallel", "parallel")),
    )(projp, conv_w)


def _conv_bwd(projp, conv_w, dc, n_batch, seq):
    width = dc.shape[1]
    ncol = width // CONV_W
    first_x = P_CONV // CONV_W

    def body(x_ref, w_ref, dc_ref, dx_ref, dw_ref):
        b = pl.program_id(1)
        w = w_ref[...]
        pre, shifted = _conv_taps(x_ref[...].astype(F32), w)
        sg = jax.nn.sigmoid(pre)
        dpre = dc_ref[...] * (sg * (1.0 + pre * (1.0 - sg)))
        row = lax.broadcasted_iota(jnp.int32, dpre.shape, 0)
        dx = dpre * w[3:4]
        for s in (1, 2, 3):
            dx = dx + jnp.where(row < seq - s, pltpu.roll(dpre, seq - s, 0), 0.0) * w[3 - s:4 - s]
        dx_ref[...] = dx.astype(dx_ref.dtype)
        for s in (0, 1, 2, 3):
            part = jnp.sum(dpre * shifted[s], axis=0, keepdims=True)

            @pl.when(b == 0)
            def _():
                dw_ref[3 - s:4 - s, :] = part

            @pl.when(b > 0)
            def _():
                dw_ref[3 - s:4 - s, :] += part

    return pl.pallas_call(
        body,
        name="conv_bwd",
        grid=(ncol, n_batch),
        in_specs=[
            pl.BlockSpec((seq, CONV_W), lambda j, b: (b, first_x + j)),
            pl.BlockSpec((4, CONV_W), lambda j, b: (0, j)),
            pl.BlockSpec((seq, CONV_W), lambda j, b: (b, j)),
        ],
        out_specs=[pl.BlockSpec((seq, CONV_W), lambda j, b: (b, j)), pl.BlockSpec((4, CONV_W), lambda j, b: (0, j))],
        out_shape=[jax.ShapeDtypeStruct((n_batch * seq, width), BF16), jax.ShapeDtypeStruct((4, width), F32)],
        compiler_params=pltpu.CompilerParams(dimension_semantics=("arbitrary", "arbitrary")),
    )(projp, conv_w, dc)


@jax.custom_vjp
def _attend(s, v):
    return _attend_fwd(s, v)[0]


def _attend_fwd(s, v):
    v16 = [t.astype(BF16) for t in v]
    p = [jnp.exp(t - jnp.max(t, axis=-1, keepdims=True)) for t in s]
    p = [t * (1.0 / jnp.sum(t, axis=-1, keepdims=True)) for t in p]
    o = [jnp.dot(t.astype(BF16), v16[n // 2], preferred_element_type=F32) for n, t in enumerate(p)]
    return o, (p, v16, o)


def _attend_bwd(saved, do):
    p, v16, o = saved
    do16 = [t.astype(BF16) for t in do]
    dv = [lax.dot_general(t.astype(BF16), do16[n], TN, preferred_element_type=F32) for n, t in enumerate(p)]
    dp = [lax.dot_general(t, v16[n // 2], NT, preferred_element_type=F32) for n, t in enumerate(do16)]
    delta = [jnp.sum(a * b, axis=-1, keepdims=True) for a, b in zip(do, o)]
    ds = [a * (b - c) for a, b, c in zip(p, dp, delta)]
    return ds, [dv[2 * i] + dv[2 * i + 1] for i in range(len(v16))]


_attend.defvjp(_attend_fwd, _attend_bwd)


def _attn_chunk(qc, kb, vb, bias2, valid, lane_lo):
    sel = (lane_lo, jnp.logical_not(lane_lo))
    items = [(i, e) for i in range(len(qc)) for e in (0, 1)]
    k16 = [t.astype(BF16) for t in kb]
    qm = [(jnp.where(sel[e], qc[i], 0.0) * (A_DIM ** -0.5)).astype(BF16) for i, e in items]
    s = [lax.dot_general(qm[n], k16[i], NT, preferred_element_type=F32) + bias2[e] for n, (i, e) in enumerate(items)]
    if valid is not None:
        s = [jnp.where(valid[i], s[n], -1e30) for n, (i, e) in enumerate(items)]
    o = [jnp.where(sel[e], t, 0.0) for t, (i, e) in zip(_attend(s, vb), items)]
    return [o[2 * i] + o[2 * i + 1] for i in range(len(qc))]


ATTN_GROUP_FWD, ATTN_GROUP_BWD = 8, 8


def _attn_loops(step, n_groups, n_masked):
    lax.fori_loop(0, n_masked, lambda g, c: step(g, c, True), 0)
    lax.fori_loop(n_masked, n_groups, lambda g, c: step(g, c, False), 0)


def _attn_group(g, group, q_ref, kp_ref, vp_ref):
    col = lax.broadcasted_iota(jnp.int32, (CHUNK, A_BAND), 1)
    lane_lo = lax.broadcasted_iota(jnp.int32, (1, LANE), 1) < A_DIM
    starts = [pl.multiple_of((g * group + i) * CHUNK, CHUNK) for i in range(group)]
    rows = [pl.ds(r0, CHUNK) for r0 in starts]
    bands = [pl.ds(r0, A_BAND) for r0 in starts]
    valid = [col + r0 >= A_PAD for r0 in starts]
    loaded = [q_ref[r, :].astype(F32) for r in rows], [kp_ref[b, :] for b in bands], [vp_ref[b, :] for b in bands]
    return rows, bands, loaded, valid, lane_lo


def _attn_specs(seq):
    def blk(first):
        return pl.BlockSpec((seq, LANE), lambda hp, b: (b, first + hp))

    return blk, pl.BlockSpec((2, CHUNK, A_BAND), lambda hp, b: (hp, 0, 0))


def _attn_fwd(projp, bias, n_batch, seq):
    nc = seq // CHUNK
    blk, bias_spec = _attn_specs(seq)

    def body(q_ref, k_ref, v_ref, b_ref, o_ref, kp_ref, vp_ref):
        kp_ref[0:A_PAD, :] = jnp.zeros((A_PAD, LANE), F32)
        vp_ref[0:A_PAD, :] = jnp.zeros((A_PAD, LANE), F32)
        kp_ref[A_PAD:, :] = k_ref[...].astype(F32)
        vp_ref[A_PAD:, :] = v_ref[...].astype(F32)
        bias2 = b_ref[...]

        def step(g, carry, masked):
            rows, _, (qc, kb, vb), valid, lane_lo = _attn_group(g, ATTN_GROUP_FWD, q_ref, kp_ref, vp_ref)
            out = _attn_chunk(qc, kb, vb, bias2, valid if masked else None, lane_lo)
            for r, o in zip(rows, out):
                o_ref[r, :] = o.astype(o_ref.dtype)
            return carry

        _attn_loops(step, nc // ATTN_GROUP_FWD, A_PAD // (CHUNK * ATTN_GROUP_FWD))

    return pl.pallas_call(
        body,
        name="attn_fwd",
        grid=(A_HEADS // 2, n_batch),
        in_specs=[blk(0), blk(4), blk(8), bias_spec],
        out_specs=pl.BlockSpec((seq, LANE), lambda hp, b: (b, hp)),
        out_shape=jax.ShapeDtypeStruct((n_batch * seq, A_WIDTH), BF16),
        scratch_shapes=[pltpu.VMEM((A_PAD + seq, LANE), F32), pltpu.VMEM((A_PAD + seq, LANE), F32)],
        compiler_params=pltpu.CompilerParams(dimension_semantics=("parallel", "parallel")),
    )(projp, projp, projp, bias)


def _attn_bwd(projp, bias, dy, n_batch, seq):
    nc = seq // CHUNK
    blk, bias_spec = _attn_specs(seq)
    out_blk = pl.BlockSpec((seq, LANE), lambda hp, b: (b, hp))

    def body(q_ref, k_ref, v_ref, b_ref, dy_ref, dq_ref, dk_ref, dv_ref, db_ref, kp_ref, vp_ref, dkp_ref, dvp_ref):
        b = pl.program_id(1)
        kp_ref[0:A_PAD, :] = jnp.zeros((A_PAD, LANE), F32)
        vp_ref[0:A_PAD, :] = jnp.zeros((A_PAD, LANE), F32)
        kp_ref[A_PAD:, :] = k_ref[...].astype(F32)
        vp_ref[A_PAD:, :] = v_ref[...].astype(F32)
        dkp_ref[...] = jnp.zeros_like(dkp_ref)
        dvp_ref[...] = jnp.zeros_like(dvp_ref)
        bias2 = b_ref[...]

        @pl.when(b == 0)
        def _():
            db_ref[...] = jnp.zeros_like(db_ref)

        def step(g, carry, masked):
            rows, bands, (qc, kb, vb), valid, lane_lo = _attn_group(g, ATTN_GROUP_BWD, q_ref, kp_ref, vp_ref)
            _, vjp = jax.vjp(lambda q, k, v, bb: _attn_chunk(q, k, v, bb, valid if masked else None, lane_lo), qc, kb, vb, bias2)
            dq, dk, dv, dbias = vjp([dy_ref[r, :] for r in rows])
            for i, r in enumerate(rows):
                dq_ref[r, :] = dq[i].astype(dq_ref.dtype)
            for i, band in enumerate(bands):
                dkp_ref[band, :] += dk[i]
                dvp_ref[band, :] += dv[i]
            db_ref[...] += dbias
            return carry

        _attn_loops(step, nc // ATTN_GROUP_BWD, A_PAD // (CHUNK * ATTN_GROUP_BWD))
        dk_ref[...] = dkp_ref[A_PAD:, :].astype(dk_ref.dtype)
        dv_ref[...] = dvp_ref[A_PAD:, :].astype(dv_ref.dtype)

    n_tok = n_batch * seq
    pad = pltpu.VMEM((A_PAD + seq, LANE), F32)
    return pl.pallas_call(
        body,
        name="attn_bwd",
        grid=(A_HEADS // 2, n_batch),
        in_specs=[blk(0), blk(4), blk(8), bias_spec, out_blk],
        out_specs=[out_blk, out_blk, out_blk, bias_spec],
        out_shape=[jax.ShapeDtypeStruct((n_tok, A_WIDTH), BF16)] * 3 + [jax.ShapeDtypeStruct((A_HEADS, CHUNK, A_BAND), F32)],
        scratch_shapes=[pad, pad, pad, pad],
        compiler_params=pltpu.CompilerParams(dimension_semantics=("arbitrary", "arbitrary")),
    )(projp, projp, projp, bias, dy)


def _rel_bias_table(rel_bias):
    span = CHUNK + A_BAND - 1
    near = REL_CLIP + CHUNK
    far = jnp.broadcast_to(rel_bias[:, 2 * REL_CLIP:], (A_HEADS, span - near))
    t = jnp.concatenate([rel_bias[:, 2 * REL_CLIP + 1 - near:], far], axis=1)
    u = jnp.concatenate([t[:, :A_BAND][:, ::-1], t[:, A_BAND:][:, ::-1]], axis=1)
    rolled = jnp.tile(u, (1, CHUNK))[:, :CHUNK * (span - 1)].reshape(A_HEADS, CHUNK, span - 1)
    return rolled[:, :, :A_BAND]


def _dot(a, b, dn=NN):
    return lax.dot_general(a, b, dn, precision=DELTA_PREC, preferred_element_type=F32)


def _dot16(a, b, dn=NN):
    return lax.dot_general(a.astype(BF16), b.astype(BF16), dn, preferred_element_type=F32)


def _each(fn, *lists):
    return [fn(*vals) for vals in zip(*lists)]


@jax.custom_vjp
def _saved_inverse(x, inv):
    return inv


def _saved_inverse_fwd(x, inv):
    return inv, inv


def _saved_inverse_bwd(inv, ct):
    return _dot(_dot(inv, ct, TN), inv, NT), jnp.zeros_like(inv)


_saved_inverse.defvjp(_saved_inverse_fwd, _saved_inverse_bwd)


def _delta_chunk(r_state, cq, ck, cv, beta, g, saved_inv=None):
    ii = lax.broadcasted_iota(jnp.int32, (CHUNK, CHUNK), 0)
    jj = lax.broadcasted_iota(jnp.int32, (CHUNK, CHUNK), 1)
    incl, strict, eye = ii >= jj, ii > jj, ii == jj
    q = _each(lambda t: t * lax.rsqrt(jnp.sum(t * t, axis=-1, keepdims=True) + EPS) * (B_DIM ** -0.5), cq)
    k = _each(lambda t: t * lax.rsqrt(jnp.sum(t * t, axis=-1, keepdims=True) + EPS), ck)
    g_b = _each(lambda t: jnp.broadcast_to(t, (CHUNK, CHUNK)), g)
    g_row = _each(lambda t: jnp.sum(jnp.where(eye, t, 0.0), axis=0, keepdims=True), g_b)
    gc_col = _each(lambda t: jnp.sum(jnp.where(incl, t, 0.0), axis=1, keepdims=True), g_row)
    gc_row = _each(lambda t: jnp.sum(jnp.where(ii <= jj, t, 0.0), axis=0, keepdims=True), g_b)
    decay = _each(lambda c, r: jnp.where(incl, jnp.exp(jnp.where(incl, c - r, 0.0)), 0.0), gc_col, gc_row)
    kk = _each(lambda t: _dot(t, t, NT), k)
    x = _each(lambda b, m, d: jnp.where(strict, -(b * m * d), 0.0), beta, kk, decay)
    if saved_inv is None:
        inv = _each(lambda t: jnp.where(eye, 1.0, 0.0) + t, x)
        pw = x
        for _ in range(5):
            pw = _each(lambda t: _dot(t, t), pw)
            inv = _each(lambda t, s: t + _dot(t, s), inv, pw)
    else:
        inv = _each(_saved_inverse, x, saved_inv)
    egc = _each(jnp.exp, gc_col)
    u = _each(lambda t, b, v: _dot16(t, b * v), inv, beta, cv)
    wk = _each(lambda t, b, e, kh: _dot16(t, (b * e) * kh), inv, beta, egc, k)
    pqk = _each(lambda qh, kh, d: _dot16(qh, kh, NT) * d, q, k, decay)
    g_last = _each(lambda c: c[CHUNK - 1:CHUNK, :], gc_col)
    kdec = _each(lambda kh, gl, c: kh * jnp.exp(gl - c), k, g_last, gc_col)
    w = _each(lambda uh, wkh, r: uh - _dot16(wkh, r), u, wk, r_state)
    o = _each(lambda e, qh, r, ph, wh: e * _dot16(qh, r) + _dot16(ph, wh), egc, q, r_state, pqk, w)
    r_new = _each(lambda gl, r, kd, wh: jnp.exp(gl) * r + _dot16(kd, wh, TN), g_last, r_state, kdec, w)
    return o, r_new, inv


DELTA_BLK = 512


def _delta_blocks(n_batch, seq):
    nblk = seq // DELTA_BLK
    cpb = DELTA_BLK // CHUNK

    def rows(width, order):
        return pl.BlockSpec((DELTA_BLK, width), lambda b, i: (b * nblk + order(i), 0))

    def states(order, side):
        return pl.BlockSpec((cpb, B_HEADS, side, side), lambda b, i: (b * nblk + order(i), 0, 0, 0))

    return nblk, cpb, rows, states


def _head_cols(h):
    return [pl.ds(part * B_HEADS * B_DIM + h * B_DIM, B_DIM) for part in range(3)]


def _load_heads(c_ref, bg_ref, state_ref, rows):
    bg_c = bg_ref[rows, :]
    cols = [_head_cols(h) for h in range(B_HEADS)]
    return ([state_ref[h] for h in range(B_HEADS)], [c_ref[rows, c[0]] for c in cols], [c_ref[rows, c[1]] for c in cols],
            [c_ref[rows, c[2]] for c in cols], [bg_c[:, h:h + 1] for h in range(B_HEADS)],
            [bg_c[:, B_HEADS + h:B_HEADS + h + 1] for h in range(B_HEADS)])


def _delta_fwd(conv, bg, n_batch, seq):
    nblk, cpb, rows_spec, states_spec = _delta_blocks(n_batch, seq)

    def forward(i):
        return i

    def body(c_ref, bg_ref, o_ref, st_ref, inv_ref, r_ref):
        @pl.when(pl.program_id(1) == 0)
        def _():
            r_ref[...] = jnp.zeros_like(r_ref)

        def step(c, carry):
            rows = pl.ds(pl.multiple_of(c * CHUNK, CHUNK), CHUNK)
            args = _load_heads(c_ref, bg_ref, r_ref, rows)
            o, r_new, inv = _delta_chunk(*args)
            for h in range(B_HEADS):
                st_ref[c, h] = args[0][h]
                inv_ref[c, h] = inv[h]
                o_ref[rows, pl.ds(h * B_DIM, B_DIM)] = o[h]
            for h in range(B_HEADS):
                r_ref[h] = r_new[h]
            return carry

        lax.fori_loop(0, cpb, step, 0)

    n_tok = n_batch * seq
    return pl.pallas_call(
        body,
        name="delta_fwd",
        grid=(n_batch, nblk),
        in_specs=[rows_spec(B_CONV, forward), rows_spec(LANE, forward)],
        out_specs=[rows_spec(B_HEADS * B_DIM, forward), states_spec(forward, B_DIM), states_spec(forward, CHUNK)],
        out_shape=[jax.ShapeDtypeStruct((n_tok, B_HEADS * B_DIM), F32),
                   jax.ShapeDtypeStruct((n_tok // CHUNK, B_HEADS, B_DIM, B_DIM), F32),
                   jax.ShapeDtypeStruct((n_tok // CHUNK, B_HEADS, CHUNK, CHUNK), F32)],
        scratch_shapes=[pltpu.VMEM((B_HEADS, B_DIM, B_DIM), F32)],
        compiler_params=pltpu.CompilerParams(dimension_semantics=("arbitrary", "arbitrary")),
    )(conv, bg)


def _delta_bwd(conv, bg, states, inverses, do, n_batch, seq):
    nblk, cpb, rows_spec, states_spec = _delta_blocks(n_batch, seq)

    def backward(i):
        return nblk - 1 - i

    def body(c_ref, bg_ref, st_ref, inv_ref, do_ref, dc_ref, dbg_ref, dr_ref):
        @pl.when(pl.program_id(1) == 0)
        def _():
            dr_ref[...] = jnp.zeros_like(dr_ref)

        def step(n, carry):
            c = cpb - 1 - n
            rows = pl.ds(pl.multiple_of(c * CHUNK, CHUNK), CHUNK)
            saved = [inv_ref[c, h] for h in range(B_HEADS)]
            _, vjp = jax.vjp(lambda *args: _delta_chunk(*args, saved_inv=saved)[:2],
                             *_load_heads(c_ref, bg_ref, st_ref.at[c], rows))
            do = [do_ref[rows, pl.ds(h * B_DIM, B_DIM)] for h in range(B_HEADS)]
            dr, dq, dk, dv, dbeta, dg = vjp((do, [dr_ref[h] for h in range(B_HEADS)]))
            lane = lax.broadcasted_iota(jnp.int32, (CHUNK, LANE), 1)
            dbg = jnp.zeros((CHUNK, LANE), F32)
            for h in range(B_HEADS):
                cq, ck, cv = _head_cols(h)
                dr_ref[h] = dr[h]
                dc_ref[rows, cq] = dq[h]
                dc_ref[rows, ck] = dk[h]
                dc_ref[rows, cv] = dv[h]
                dbg = dbg + jnp.where(lane == h, dbeta[h], 0.0) + jnp.where(lane == h + B_HEADS, dg[h], 0.0)
            dbg_ref[rows, :] = dbg
            return carry

        lax.fori_loop(0, cpb, step, 0)

    n_tok = n_batch * seq
    return pl.pallas_call(
        body,
        name="delta_bwd",
        grid=(n_batch, nblk),
        in_specs=[rows_spec(B_CONV, backward), rows_spec(LANE, backward), states_spec(backward, B_DIM),
                  states_spec(backward, CHUNK), rows_spec(B_HEADS * B_DIM, backward)],
        out_specs=[rows_spec(B_CONV, backward), rows_spec(LANE, backward)],
        out_shape=[jax.ShapeDtypeStruct((n_tok, B_CONV), F32), jax.ShapeDtypeStruct((n_tok, LANE), F32)],
        scratch_shapes=[pltpu.VMEM((B_HEADS, B_DIM, B_DIM), F32)],
        compiler_params=pltpu.CompilerParams(dimension_semantics=("arbitrary", "arbitrary")),
    )(conv, bg, states, inverses, do)


def _lane_row(vec4, first):
    return jnp.concatenate([jnp.zeros((1, first), F32), vec4.reshape(1, B_HEADS).astype(F32),
                            jnp.zeros((1, LANE - first - B_HEADS), F32)], axis=1)


def _local_step(x3d, p3d, tgt3d, w_in, small, rest_weights, send_grads, send_w_in):
    n_batch, seq, _ = x3d.shape
    n_tok = n_batch * seq
    x, p, tgt = x3d.reshape(n_tok, D), p3d.reshape(n_tok, -1), tgt3d.reshape(n_tok, D)
    g_mix, g_ffn, g_ple, g_final = (small[k].reshape(1, D) for k in ("g_mix", "g_ffn", "g_ple", "g_final"))
    w_onorm = small["w_onorm"].reshape(1, B_DIM)
    al_row = _lane_row(small["a_log"], B_HEADS)
    dtb_row = _lane_row(small["dt_bias"], B_HEADS)
    rel_bias = small["rel_bias"].reshape(A_HEADS, -1)
    bias = _rel_bias_table(rel_bias)
    conv_w = small["conv_w"].reshape(4, B_CONV)

    h1 = _rms_fwd(x, g_mix, name="rms_mix")
    projp = _mm(h1, w_in, tb=True, out_dtype=BF16, name="mm_proj", tn=1920)
    bd = _mm(h1, w_in[P_BD:], tb=True, name="mm_beta_decay", tn=LANE)
    y_a = _attn_fwd(projp, bias, n_batch, seq)
    conv = _conv_fwd(projp, conv_w, n_batch, seq)
    (bg,) = _rowwise(lambda raw, al, dtb: ([_gate_scalars(raw, al, dtb)], []), [_full(bd)],
                     [_full(al_row), _full(dtb_row)], [(LANE, F32, LANE, 0, 0)], name="gate_scalars", tr=1024)
    o_b, states, inverses = _delta_fwd(conv, bg, n_batch, seq)
    (y_b,) = _rowwise(lambda o, z, wn: ([_gated_norm(o, z, wn)], []), [(o_b, LANE, 0, 1), (projp, LANE, P_Z // LANE, 1)],
                      [_full(w_onorm)], [(B_HEADS * B_DIM, BF16, LANE, 0, 1)], name="gated_norm", tr=1024, ncol=B_HEADS)
    w = rest_weights(y_b)
    t_a = _mm(y_a, w["w_branch_a"], tb=True, out_dtype=BF16, name="mm_branch_a", tn=1024)
    t_b = _mm(y_b, w["w_branch_b"], tb=True, out_dtype=BF16, name="mm_branch_b", tn=1024)
    half = D // 2
    gate_rows = [(projp, half, P_GATE // half, 1), (projp, half, P_GATE // half + 2, 1), (t_a, half, 0, 1), (t_b, half, 0, 1)]
    (merged,) = _rowwise(lambda ga, gb, ta, tb: ([_merge(ga, gb, ta, tb)], []), gate_rows, [], [(D, BF16, half, 0, 1)],
                         name="merge", tr=512, ncol=2)
    x1, h2 = _mm(merged, w["w_out"], name="mm_out", tn=1024, epilogue=_residual_rms, rows=[x], bcs=[g_ffn], outs=(F32, BF16))
    gu, act = _mm(h2, w["w_gate_up"], tb=True, name="mm_gate_up", tm=256, tn=2 * D_FF, epilogue=lambda r: ([r, _swiglu(r)], []),
                  outs=(BF16, (BF16, D_FF)))
    x2, h3 = _mm(act, w["w_down"], name="mm_down", tm=512, tn=1024, tk=D_FF, epilogue=_residual_rms, rows=[x1], bcs=[g_ple],
                 outs=(F32, BF16))
    pp = _mm(p, w["w_ple_proj"], tb=True, name="mm_ple_proj", tn=1024)

    def head_fn(pgb, x2b, ppb, tb, gb):
        loss, (dx2, dpg, dpp, dg) = jax.value_and_grad(_head_loss, argnums=(0, 1, 2, 4))(x2b, pgb, ppb, tb, gb)
        return [dx2, dpg, dpp], [dg, jnp.full((1, D), loss, F32)]

    dx3, dpg, dpp, dg_final, loss_row = _mm(h3, w["w_ple_gate"], name="mm_ple_gate_loss", tm=256, tn=1024, epilogue=head_fn,
                                            rows=[x2, pp, tgt], bcs=[g_final], outs=(F32, BF16, BF16), n_red=2)
    gw = {}
    gw["w_ple_proj"] = _mm(dpp, p, ta=True, out_dtype=BF16, name="mm_d_ple_proj", tn=256)
    gw["w_ple_gate"] = _mm(h3, dpg, ta=True, out_dtype=BF16, name="mm_d_ple_gate", tn=512)
    dx2, dg_ple = _mm(dpg, w["w_ple_gate"], tb=True, name="mm_dh3", tm=512, tn=1024, epilogue=_rms_bwd, rows=[x2, dx3],
                      bcs=[g_ple], outs=(F32,), n_red=1)
    gw["w_down"] = _mm(act, dx2, ta=True, out_dtype=BF16, name="mm_d_down", tm=1408, tn=256)
    def swiglu_bwd(dab, gub):
        _, vjp = jax.vjp(_swiglu, gub.astype(F32))
        return [vjp(dab)[0]], []

    (dgu,) = _mm(dx2, w["w_down"], tb=True, name="mm_dact", tm=256, tn=D_FF, epilogue=swiglu_bwd, rows=[gu],
                 outs=((BF16, 2 * D_FF),))
    gw["w_gate_up"] = _mm(dgu, h2, ta=True, out_dtype=BF16, name="mm_d_gate_up", tm=512, tn=1024)
    dx1, dg_ffn = _mm(dgu, w["w_gate_up"], name="mm_dh2", tm=256, tn=1024, tk=2 * D_FF, epilogue=_rms_bwd, rows=[x1, dx2],
                      bcs=[g_ffn], outs=(F32,), n_red=1)
    gw["w_out"] = _mm(merged, dx1, ta=True, out_dtype=BF16, name="mm_d_out", tn=512)
    dmerged = _mm(dx1, w["w_out"], tb=True, name="mm_dmerged", tn=1024)

    def merge_bwd(ga, gb, ta, tb, dm):
        _, vjp = jax.vjp(_merge, ga, gb, ta, tb)
        return list(vjp(dm)), []

    dga, dgb, dta, dtb = _rowwise(merge_bwd, gate_rows + [(dmerged, half, 0, 1)], [], [(D, BF16, half, 0, 1)] * 4,
                                  name="merge_bwd", tr=512, ncol=2)
    gw["w_branch_a"] = _mm(dta, y_a, ta=True, out_dtype=BF16, name="mm_d_branch_a", tn=512)
    gw["w_branch_b"] = _mm(dtb, y_b, ta=True, out_dtype=BF16, name="mm_d_branch_b", tn=512)
    dya = _mm(dta, w["w_branch_a"], name="mm_dya", tn=512)
    dyb = _mm(dtb, w["w_branch_b"], name="mm_dyb", tn=512)

    w_onorm = w_onorm + send_grads(gw)[0, 0]

    def gated_norm_bwd(o, z, dy, wn):
        _, vjp = jax.vjp(_gated_norm, o, z, wn)
        do, dz, dwn = vjp(dy)
        return [do, dz], [dwn]

    do_b, dz, dw_onorm = _rowwise(
        gated_norm_bwd, [(o_b, LANE, 0, 1), (projp, LANE, P_Z // LANE, 1), (dyb, LANE, 0, 1)], [_full(w_onorm)],
        [(B_HEADS * B_DIM, F32, LANE, 0, 1), (B_HEADS * B_DIM, BF16, LANE, 0, 1)], [(B_DIM, B_DIM, 0)],
        name="gated_norm_bwd", tr=1024, ncol=B_HEADS)
    dconv_out, dbg = _delta_bwd(conv, bg, states, inverses, do_b, n_batch, seq)

    def gate_scalars_bwd(raw, dbgb, al, dtb):
        _, vjp = jax.vjp(_gate_scalars, raw, al, dtb)
        draw, dal, ddtb = vjp(dbgb)
        return [draw], [dal, ddtb]

    dbd, dal_row, ddtb_row = _rowwise(gate_scalars_bwd, [_full(bd), _full(dbg)], [_full(al_row), _full(dtb_row)],
                                      [(LANE, BF16, LANE, 0, 0)], [(LANE, LANE, 0), (LANE, LANE, 0)], name="gate_scalars_bwd",
                                      tr=1024)
    dconv, dconv_w = _conv_bwd(projp, conv_w, dconv_out, n_batch, seq)
    dq_a, dk_a, dv_a, dbias = _attn_bwd(projp, bias, dya, n_batch, seq)
    dprojp = jnp.concatenate([dq_a, dk_a, dv_a, dconv, dz, dga, dgb, dbd], axis=1)
    sent = send_w_in(_mm(dprojp, h1, ta=True, out_dtype=BF16, name="mm_d_in", tm=640, tn=1024))
    sent, dprojp = lax.optimization_barrier((sent, dprojp))
    grad_x, dg_mix = _mm(dprojp, w_in, name="mm_dh1", tm=256, tn=1024, tk=P_END, epilogue=_rms_bwd, rows=[x, dx1],
                         bcs=[g_mix + sent[0, 0]], outs=(F32,), n_red=1)

    _, bias_vjp = jax.vjp(_rel_bias_table, rel_bias)
    gs = {
        "g_mix": dg_mix, "g_ffn": dg_ffn, "g_ple": dg_ple, "g_final": dg_final, "w_onorm": dw_onorm,
        "conv_w": dconv_w, "rel_bias": bias_vjp(dbias)[0],
        "a_log": dal_row[0, B_HEADS:2 * B_HEADS], "dt_bias": ddtb_row[0, B_HEADS:2 * B_HEADS],
    }
    return loss_row[:, :1], grad_x.reshape(n_batch, seq, D), gs


MATRICES = (("w_in", 1), ("w_gate_up", 1), ("w_branch_a", 1), ("w_branch_b", 1), ("w_out", 0), ("w_down", 0),
            ("w_ple_gate", 0), ("w_ple_proj", 1))
TAPS_PER_SHARD = B_CONV // N_DEV


def _held(shard, axis):
    return shard if axis == 0 else shard.T


def _from_gathered(slabs):
    return slabs.reshape(-1, slabs.shape[-1])


def _to_owner(held):
    return held.reshape(N_DEV, held.shape[0] // N_DEV, held.shape[1])


def _permute_w_in(held):
    n_gate = P_BD - P_GATE
    row = lax.broadcasted_iota(jnp.int32, (P_END, 1), 0)
    same = jnp.pad(held, ((0, P_END - D_IN), (0, 0)))
    up = jnp.pad(held[8:], ((0, P_END - D_IN + 8), (0, 0)))
    down = jnp.pad(held[:P_GATE + 8], ((n_gate, P_END - P_BD - 8), (0, 0)))
    zero = jnp.zeros((), held.dtype)
    return jnp.where(row < P_GATE, same, jnp.where(row < P_BD, up, jnp.where(row < P_BD + 8, down, zero)))


def _unpermute_w_in(gp):
    n_gate = P_BD - P_GATE
    row = lax.broadcasted_iota(jnp.int32, (D_IN, 1), 0)
    same = gp[:D_IN]
    up = jnp.pad(gp[n_gate:], ((0, D_IN - (P_END - n_gate)), (0, 0)))
    down = jnp.pad(gp[:P_BD], ((8, 0), (0, 0)))
    return jnp.where(row < P_GATE, same, jnp.where(row < P_GATE + 8, up, down))


SMALL_ROWS = 16
SMALL_LAYOUT = (("g_mix", 0, D), ("g_ffn", 1, D), ("g_ple", 2, D), ("g_final", 3, D), ("conv_w", 4, 4 * B_CONV),
                ("rel_bias", 10, A_HEADS * (2 * REL_CLIP + 1)), ("w_onorm", 13, B_DIM), ("a_log", 14, B_HEADS),
                ("dt_bias", 14, B_HEADS), ("loss", 15, 1))


def _pack_small(gs):
    rows = {}
    for name, row, n in SMALL_LAYOUT:
        rows.setdefault(row, []).append(gs[name].reshape(-1).astype(F32))
    parts = []
    for row in sorted(rows):
        flat = jnp.concatenate(rows[row])
        parts.append(jnp.concatenate([flat, jnp.zeros((-flat.shape[0] % D,), F32)]))
    flat = jnp.concatenate(parts)
    assert flat.shape[0] == SMALL_ROWS * D, flat.shape
    return flat.reshape(SMALL_ROWS, D)


def _unpack_small(blk):
    flat, out, used = blk.reshape(-1), {}, {}
    for name, row, n in SMALL_LAYOUT:
        start = row * D + used.get(row, 0)
        out[name] = flat[start:start + n]
        used[row] = used.get(row, 0) + n
    return out


def _position():
    return lax.axis_index("x"), lax.axis_index("y"), lax.axis_index("c")


PEERS = N_DEV - 1


def _comm_call(body, arrays, out_shapes, *, name):
    n = len(arrays)
    return pl.pallas_call(
        body,
        name=name,
        out_shape=out_shapes,
        in_specs=[HBM_SPEC] * n,
        out_specs=[HBM_SPEC] * n,
        scratch_shapes=[pltpu.SemaphoreType.DMA((PEERS * n,)), pltpu.SemaphoreType.DMA((PEERS * n,)),
                        pltpu.SemaphoreType.DMA((n,))],
    )(*arrays)


def _weights_allgather(shards):
    n = len(shards)

    def body(*refs):
        ins, outs = refs[:n], refs[n:2 * n]
        send_sems, recv_sems, local_sems = refs[2 * n:]
        x, y, c = _position()
        me, sibling = (x, y, c), (x, y, 1 - c)
        chips = [(1 - x, y), (x, 1 - y), (1 - x, 1 - y)]

        def slab(a, px, py, pc):
            return outs[a].at[4 * px + 2 * py + pc]

        def copy(a, k, block, to, src=None):
            return pltpu.make_async_remote_copy(src_ref=slab(a, *block) if src is None else src, dst_ref=slab(a, *block),
                                                send_sem=send_sems.at[PEERS * a + k], recv_sem=recv_sems.at[PEERS * a + k],
                                                device_id=to, device_id_type=MESH)

        local = [pltpu.make_async_copy(ins[a], slab(a, *me), local_sems.at[a]) for a in range(n)]
        sent = [copy(a, 1 + j, me, (*chip, c), src=ins[a]) for a in range(n) for j, chip in enumerate(chips)]
        sent += [copy(a, 0, me, sibling, src=ins[a]) for a in range(n)]
        for cp in sent + local:
            cp.start()
        for a in range(n):
            for j, chip in enumerate(chips):
                copy(a, 1 + j, (*chip, c), me).wait_recv()
                passed = copy(a, 4 + j, (*chip, c), sibling)
                passed.start()
                sent.append(passed)
        for a in range(n):
            copy(a, 0, sibling, me).wait_recv()
            for j, chip in enumerate(chips):
                copy(a, 4 + j, (*chip, 1 - c), me).wait_recv()
        for cp in sent:
            cp.wait_send()
        for cp in local:
            cp.wait()

    return _comm_call(body, shards, [jax.ShapeDtypeStruct((N_DEV,) + s.shape, s.dtype) for s in shards],
                      name="weights_allgather")


def _grads_exchange(by_owner):
    n = len(by_owner)

    def body(*refs):
        ins, outs = refs[:n], refs[n:2 * n]
        send_sems, recv_sems, local_sems = refs[2 * n:]
        x, y, c = _position()
        mine = 4 * x + 2 * y + c
        local = [pltpu.make_async_copy(ins[a].at[mine], outs[a].at[mine], local_sems.at[a]) for a in range(n)]
        for cp in local:
            cp.start()
        flips = [(dx, dy, dc) for dx in (0, 1) for dy in (0, 1) for dc in (0, 1) if dx + dy + dc]
        pending = []
        for k, (dx, dy, dc) in enumerate(flips):
            px, py, pc = (1 - x if dx else x), (1 - y if dy else y), (1 - c if dc else c)
            peer = 4 * px + 2 * py + pc
            for a in range(n):
                def remote(slot):
                    return pltpu.make_async_remote_copy(src_ref=ins[a].at[peer], dst_ref=outs[a].at[slot],
                                                        send_sem=send_sems.at[PEERS * a + k], recv_sem=recv_sems.at[PEERS * a + k],
                                                        device_id=(px, py, pc), device_id_type=MESH)

                sent = remote(mine)
                sent.start()
                pending.append((sent, remote(peer)))
        for sent, landed in pending:
            landed.wait_recv()
            sent.wait_send()
        for cp in local:
            cp.wait()

    return _comm_call(body, by_owner, [jax.ShapeDtypeStruct(g.shape, g.dtype) for g in by_owner], name="grads_exchange")


SEM_SPEC = pl.BlockSpec(memory_space=pltpu.SEMAPHORE)
DATAFLOW = pltpu.SideEffectType.DATAFLOW_SIDE_EFFECTING


def _peer_copies(srcs, lands, send_sems, recv_sems, by_owner, arrival):
    x, y, c = _position()
    mine = 4 * x + 2 * y + c
    copies = []
    for k, (dx, dy, dc) in enumerate([(dx, dy, dc) for dx in (0, 1) for dy in (0, 1) for dc in (0, 1) if dx + dy + dc]):
        px, py, pc = (1 - x if dx else x), (1 - y if dy else y), (1 - c if dc else c)
        peer = 4 * px + 2 * py + pc
        for a, (src, land) in enumerate(zip(srcs, lands)):
            copies.append(pltpu.make_async_remote_copy(
                src_ref=src.at[peer] if by_owner else src, dst_ref=land.at[peer if arrival else mine],
                send_sem=send_sems.at[PEERS * a + k], recv_sem=recv_sems.at[PEERS * a + k],
                device_id=(px, py, pc), device_id_type=MESH))
    return copies


def _exchange_start(sources, by_owner, *, name):
    n = len(sources)
    lands = [lax.empty((N_DEV,) + (s.shape[1:] if by_owner else s.shape), s.dtype) for s in sources]

    def body(*refs):
        send_sems, recv_sems, token = refs[2 * n], refs[2 * n + 1], refs[-1]
        for copy in _peer_copies(refs[:n], refs[n:2 * n], send_sems, recv_sems, by_owner, arrival=False):
            copy.start()
        token[...] = jnp.zeros_like(token)

    sems = pltpu.SemaphoreType.DMA((PEERS * n,))
    outs = pl.pallas_call(
        body,
        name=name,
        out_shape=(sems, sems, *[pltpu.HBM(a.shape, a.dtype) for a in sources + lands], jax.ShapeDtypeStruct((8, LANE), F32)),
        in_specs=[HBM_SPEC] * (2 * n),
        out_specs=(SEM_SPEC, SEM_SPEC, *[HBM_SPEC] * (2 * n), pl.BlockSpec(memory_space=pltpu.VMEM)),
        input_output_aliases={i: 2 + i for i in range(2 * n)},
        compiler_params=pltpu.CompilerParams(has_side_effects=DATAFLOW),
    )(*[pltpu.with_memory_space_constraint(a, pltpu.HBM) for a in sources + lands])
    return outs[:-1], outs[-1]


def _exchange_wait(started, after, by_owner, *, name):
    send_sems, recv_sems, *arrays = started
    n = len(arrays) // 2

    def body(*refs):
        for copy in _peer_copies(refs[:n], refs[n:2 * n], refs[2 * n], refs[2 * n + 1], by_owner, arrival=True):
            copy.wait_send()
            copy.wait_recv()

    outs = pl.pallas_call(
        body,
        name=name,
        out_shape=[pltpu.HBM(a.shape, a.dtype) for a in arrays],
        in_specs=[HBM_SPEC] * (2 * n) + [SEM_SPEC, SEM_SPEC, pl.BlockSpec(memory_space=pl.ANY)],
        out_specs=[HBM_SPEC] * (2 * n),
        input_output_aliases={i: i for i in range(2 * n)},
        compiler_params=pltpu.CompilerParams(has_side_effects=DATAFLOW),
    )(*arrays, send_sems, recv_sems, after)
    return outs[:n], outs[n:]


def _slot_sum(g_ref, own_ref):
    if own_ref is not None:
        x, y, c = _position()
        mine = 4 * x + 2 * y + c
    acc = None
    for j in range(N_DEV):
        part = g_ref[j] if own_ref is None else jnp.where(mine == j, own_ref[...], g_ref[j])
        acc = part.astype(F32) if acc is None else acc + part.astype(F32)
    return acc


def _sum_slots(got, *, name, tr):
    _, rows, cols = got.shape
    tr = _tile(rows, tr, 16)

    def body(g_ref, o_ref):
        o_ref[...] = _slot_sum(g_ref, None)

    return pl.pallas_call(
        body,
        name=name,
        grid=(rows // tr,),
        in_specs=[pl.BlockSpec((N_DEV, tr, cols), lambda i: (0, i, 0))],
        out_specs=pl.BlockSpec((tr, cols), lambda i: (i, 0)),
        out_shape=jax.ShapeDtypeStruct((rows, cols), F32),
        compiler_params=pltpu.CompilerParams(dimension_semantics=("parallel",)),
    )(got)


def _adamw(wt, g, m, v, *, name, own=None):
    slots = own is not None
    shape = wt.shape
    two_d = (-1, shape[-1]) if wt.ndim > 1 else (1, -1)
    args = [a.reshape(two_d) for a in (wt, m, v)]
    rows, cols = args[0].shape
    if rows % 16 == 0:
        tr, tc = _tile(rows, 256, 16), cols
    else:
        tr, tc = rows, _tile(cols, 256 if rows > 64 else 512)
    args.insert(1, g.reshape((N_DEV, rows, cols) if slots else (rows, cols)))
    if slots:
        args.append(own.reshape(N_DEV, rows, cols))

    def body(w_ref, g_ref, m_ref, v_ref, *refs):
        go_ref, d_ref, nm_ref, nv_ref = refs[-4:]
        gv = _slot_sum(g_ref, refs[0]) if slots else g_ref[...]
        go_ref[...] = gv
        m2 = ADAM_B1 * m_ref[...] + (1.0 - ADAM_B1) * gv
        v2 = ADAM_B2 * v_ref[...] + (1.0 - ADAM_B2) * (gv * gv)
        m_hat = m2 / (1.0 - ADAM_B1 ** ADAM_STEP)
        v_hat = v2 / (1.0 - ADAM_B2 ** ADAM_STEP)
        d_ref[...] = -ADAM_LR * (m_hat / (jnp.sqrt(v_hat) + ADAM_EPS) + ADAM_WD * w_ref[...])
        nm_ref[...] = m2
        nv_ref[...] = v2

    def own_block(i, j):
        x, y, c = _position()
        return 4 * x + 2 * y + c, i, j

    spec = pl.BlockSpec((tr, tc), lambda i, j: (i, j))
    g_spec = pl.BlockSpec((N_DEV, tr, tc), lambda i, j: (0, i, j)) if slots else spec
    own_spec = pl.BlockSpec((None, tr, tc), own_block)
    outs = pl.pallas_call(
        body,
        name=name,
        grid=(rows // tr, cols // tc),
        in_specs=[spec, g_spec, spec, spec] + ([own_spec] if slots else []),
        out_specs=[spec] * 4,
        out_shape=[jax.ShapeDtypeStruct((rows, cols), F32)] * 4,
        compiler_params=pltpu.CompilerParams(dimension_semantics=("parallel", "parallel")),
    )(*args)
    return tuple(o.reshape(shape) for o in outs)


WEIGHTS = ("g_mix", "w_in", "conv_w", "a_log", "dt_bias", "rel_bias", "w_onorm", "w_branch_a", "w_branch_b", "w_out", "g_ffn",
           "w_gate_up", "w_down", "g_ple", "w_ple_gate", "w_ple_proj", "g_final")


def kernel(x, p, g_mix, w_in, conv_w, a_log, dt_bias, rel_bias, w_onorm, w_branch_a, w_branch_b, w_out, g_ffn, w_gate_up, w_down, g_ple, w_ple_gate, w_ple_proj, g_final, loss_target, m_g_mix, m_w_in, m_conv_w, m_a_log, m_dt_bias, m_rel_bias, m_w_onorm, m_w_branch_a, m_w_branch_b, m_w_out, m_g_ffn, m_w_gate_up, m_w_down, m_g_ple, m_w_ple_gate, m_w_ple_proj, m_g_final, v_g_mix, v_w_in, v_conv_w, v_a_log, v_dt_bias, v_rel_bias, v_w_onorm, v_w_branch_a, v_w_branch_b, v_w_out, v_g_ffn, v_w_gate_up, v_w_down, v_g_ple, v_w_ple_gate, v_w_ple_proj, v_g_final):
    given = dict(g_mix=g_mix, w_in=w_in, conv_w=conv_w, a_log=a_log, dt_bias=dt_bias, rel_bias=rel_bias, w_onorm=w_onorm,
                 w_branch_a=w_branch_a, w_branch_b=w_branch_b, w_out=w_out, g_ffn=g_ffn, w_gate_up=w_gate_up, w_down=w_down,
                 g_ple=g_ple, w_ple_gate=w_ple_gate, w_ple_proj=w_ple_proj, g_final=g_final)
    mom1 = dict(g_mix=m_g_mix, w_in=m_w_in, conv_w=m_conv_w, a_log=m_a_log, dt_bias=m_dt_bias, rel_bias=m_rel_bias,
                w_onorm=m_w_onorm, w_branch_a=m_w_branch_a, w_branch_b=m_w_branch_b, w_out=m_w_out, g_ffn=m_g_ffn,
                w_gate_up=m_w_gate_up, w_down=m_w_down, g_ple=m_g_ple, w_ple_gate=m_w_ple_gate, w_ple_proj=m_w_ple_proj,
                g_final=m_g_final)
    mom2 = dict(g_mix=v_g_mix, w_in=v_w_in, conv_w=v_conv_w, a_log=v_a_log, dt_bias=v_dt_bias, rel_bias=v_rel_bias,
                w_onorm=v_w_onorm, w_branch_a=v_w_branch_a, w_branch_b=v_w_branch_b, w_out=v_w_out, g_ffn=v_g_ffn,
                w_gate_up=v_w_gate_up, w_down=v_w_down, g_ple=v_g_ple, w_ple_gate=v_w_ple_gate, w_ple_proj=v_w_ple_proj,
                g_final=v_g_final)
    mine = 4 * lax.axis_index("x") + 2 * lax.axis_index("y") + lax.axis_index("c")

    my_slot = (jnp.arange(N_DEV) == mine)[:, None, None]
    rest = MATRICES[1:]
    in_flight = {}

    got_in, got_taps = _weights_allgather([_held(w_in[0], 1).astype(BF16), conv_w[0]])
    in_flight["weights"], weights_sent = _exchange_start([_held(given[name][0], axis).astype(BF16) for name, axis in rest], False,
                                                         name="weights_start")
    small = dict(g_mix=g_mix + weights_sent[0, 0], g_ffn=g_ffn, g_ple=g_ple, g_final=g_final, w_onorm=w_onorm, a_log=a_log,
                 dt_bias=dt_bias, rel_bias=rel_bias, conv_w=jnp.transpose(got_taps, (1, 0, 2)).reshape(4, B_CONV))

    def rest_weights(after):
        shards, landed = _exchange_wait(in_flight.pop("weights"), after, False, name="weights_wait")
        return {name: _from_gathered(jnp.where(my_slot, shard[None], slabs)) for (name, _), shard, slabs in zip(rest, shards, landed)}

    def send_grads(gw):
        in_flight["grads"], sent = _exchange_start([_to_owner(gw[name]) for name, _ in rest], True, name="grads_start")
        return sent

    def send_w_in(g_in):
        in_flight["grad_in"], sent = _exchange_start([_to_owner(_unpermute_w_in(g_in))], True, name="grad_in_start")
        return sent

    loss_part, grad_x, gs = _local_step(x, p[0], loss_target, _permute_w_in(_from_gathered(got_in)), small, rest_weights,
                                        send_grads, send_w_in)
    gs["loss"] = loss_part

    updates = {}

    def update_matrices(matrices, own_slabs, landed):
        for (name, axis), own_slab, slots in zip(matrices, own_slabs, landed):
            w_held, m_held, v_held = (_held(a[name][0], axis) for a in (given, mom1, mom2))
            outs = _adamw(w_held, slots, m_held, v_held, name=f"adamw_{name}", own=own_slab)
            updates[name] = tuple(_held(o, axis)[None] for o in outs)

    update_matrices(rest, *_exchange_wait(in_flight["grads"], grad_x, True, name="grads_wait"))
    update_matrices(MATRICES[:1], *_exchange_wait(in_flight["grad_in"], updates[rest[-1][0]][0], True, name="grad_in_wait"))
    small_block, _ = lax.optimization_barrier((_pack_small(gs), updates["w_in"][0]))
    (got_small,) = _grads_exchange([jnp.broadcast_to(small_block, (N_DEV, SMALL_ROWS, D))])
    small_sum = _unpack_small(_sum_slots(got_small, name="sum_small_grads", tr=16))
    loss = small_sum.pop("loss")[0]
    conv_all = small_sum.pop("conv_w").reshape(4, N_DEV, TAPS_PER_SHARD)
    small_sum["conv_w"] = lax.dynamic_index_in_dim(conv_all, mine, axis=1, keepdims=False)
    for name, g in small_sum.items():
        updates[name] = _adamw(given[name], g.reshape(given[name].shape), mom1[name], mom2[name], name=f"adamw_{name}")
    return (loss, grad_x, *[updates[name][k] for k in range(4) for name in WEIGHTS])
```
